```python
import jax, jax.numpy as jnp
from jax import lax
import numpy as np

D_MODEL = 1024
BATCH = 16
SEQ = 4096
DEPTH = 1

HEAD_DIM = 64
N_ATTN_HEADS = 8
N_KV_HEADS = 2
GQA_GROUP = N_ATTN_HEADS // N_KV_HEADS
ATTN_WIDTH = N_ATTN_HEADS * HEAD_DIM
KV_WIDTH = N_KV_HEADS * HEAD_DIM
N_RWKV_HEADS = 8
RWKV_WIDTH = N_RWKV_HEADS * HEAD_DIM
MIX_WIDTH = ATTN_WIDTH + RWKV_WIDTH
WINDOW = 128
BLOCK = 128
DECAY_LORA = 64
ICLR_LORA = 64
GATE_LORA = 128
RWKV_SHIFT_WIDTH = 3 * RWKV_WIDTH + DECAY_LORA + ICLR_LORA + GATE_LORA
IN_WIDTH = ATTN_WIDTH + 2 * KV_WIDTH + RWKV_SHIFT_WIDTH
D_FF = 4 * D_MODEL
RMS_EPS = 1e-6
GN_EPS = 64e-5
L2_EPS = 1e-12
NEG_INF = -1e30

kernel_name = 'hybrid_swa_sink_alibi_rwkv7_block'


def rms_norm(x, g):
    xf = x.astype(jnp.float32)
    y = xf * lax.rsqrt(jnp.mean(xf * xf, axis=-1, keepdims=True) + RMS_EPS)
    return (y * g.astype(jnp.float32)).astype(x.dtype)


def alibi_slopes():
    h = jnp.arange(1, N_ATTN_HEADS + 1, dtype=jnp.float32)
    return jnp.exp2(-8.0 * h / N_ATTN_HEADS)


def sliding_window_attention(q, k, v, sinks):
    B, T = q.shape[0], q.shape[1]
    nb = T // BLOCK
    qb = q.reshape(B, nb, BLOCK, N_KV_HEADS, GQA_GROUP, HEAD_DIM)

    def band(t):
        tp = jnp.pad(t, ((0, 0), (BLOCK, 0), (0, 0), (0, 0)))
        tp = tp.reshape(B, nb + 1, BLOCK, N_KV_HEADS, HEAD_DIM)
        return jnp.concatenate([tp[:, :-1], tp[:, 1:]], axis=2)

    kb, vb = band(k), band(v)
    s = jnp.einsum('bnqkgd,bnskd->bnkgqs', qb, kb).astype(jnp.float32) * (HEAD_DIM ** -0.5)
    qi = jnp.arange(BLOCK)[:, None]
    kj = jnp.arange(2 * BLOCK)[None, :]
    dist = qi - kj + BLOCK
    bias = -alibi_slopes().reshape(N_KV_HEADS, GQA_GROUP, 1, 1) * dist.astype(jnp.float32)
    key_pos = jnp.arange(nb)[:, None] * BLOCK + jnp.arange(2 * BLOCK)[None, :] - BLOCK
    valid = ((dist >= 0) & (dist < WINDOW))[None] & (key_pos >= 0)[:, None, :]
    s = jnp.where(valid[None, :, None, None], s + bias, NEG_INF)
    sink = sinks.astype(jnp.float32).reshape(1, 1, N_KV_HEADS, GQA_GROUP, 1, 1)
    m = jnp.maximum(jnp.max(s, axis=-1, keepdims=True), sink)
    e = jnp.exp(s - m)
    p = e / (jnp.sum(e, axis=-1, keepdims=True) + jnp.exp(sink - m))
    o = jnp.einsum('bnkgqs,bnskd->bnqkgd', p.astype(vb.dtype), vb)
    return o.reshape(B, T, ATTN_WIDTH)


def rwkv7_recurrence(r, w, k, v, a, b):
    B, _, H, N = r.shape

    def step(S, inp):
        rt, wt, kt, vt, at, bt = inp
        sa = jnp.einsum('bhij,bhj->bhi', S, at)
        S = S * wt[:, :, None, :] + sa[..., None] * bt[:, :, None, :] + vt[..., None] * kt[:, :, None, :]
        return S, jnp.einsum('bhij,bhj->bhi', S, rt)

    xs = tuple(jnp.moveaxis(t, 1, 0) for t in (r, w, k, v, a, b))
    S0 = jnp.zeros((B, H, N, N), jnp.float32)
    _, y = lax.scan(step, S0, xs)
    return jnp.moveaxis(y, 0, 1)


def rwkv7_time_mix(xr, xk, xv, xw, xa, xg, w0, w2, a0, a2, g2, k_k, k_a, r_k, ln_w, ln_b):
    out_dtype = xr.dtype
    f = lambda t: t.astype(jnp.float32)
    xr, xk, xv, xw, xa, xg = f(xr), f(xk), f(xv), f(xw), f(xa), f(xg)
    B, T, _ = xr.shape
    heads = lambda t: t.reshape(B, T, N_RWKV_HEADS, HEAD_DIM)
    w_log = -jax.nn.softplus(-(f(w0) + jnp.tanh(xw) @ f(w2))) - 0.5
    decay = jnp.exp(-jnp.exp(w_log))
    a = jax.nn.sigmoid(f(a0) + xa @ f(a2))
    g = jax.nn.sigmoid(xg) @ f(g2)
    kk = heads(xk * f(k_k))
    kk = kk / jnp.maximum(jnp.sqrt(jnp.sum(kk * kk, axis=-1, keepdims=True)), L2_EPS)
    k = xk * (1.0 + (a - 1.0) * f(k_a))
    r_h, k_h, v_h, a_h = heads(xr), heads(k), heads(xv), heads(a)
    y = rwkv7_recurrence(r_h, heads(decay), k_h, v_h, -kk, kk * a_h)
    mu = jnp.mean(y, axis=-1, keepdims=True)
    var = jnp.mean(jnp.square(y - mu), axis=-1, keepdims=True)
    y = ((y - mu) * lax.rsqrt(var + GN_EPS)).reshape(B, T, RWKV_WIDTH) * f(ln_w) + f(ln_b)
    bonus = jnp.sum(r_h * k_h * f(r_k).reshape(N_RWKV_HEADS, HEAD_DIM), axis=-1, keepdims=True) * v_h
    y = (y + bonus.reshape(B, T, RWKV_WIDTH)) * g
    return y.astype(out_dtype)


def _fwd_setup_inputs(seed: int = 0) -> dict:
    key = jax.random.key(seed)
    ks = jax.random.split(key, 20)
    f32 = jnp.float32
    L = DEPTH

    def nrm(k, shape, scale):
        return jax.random.normal(k, shape, f32) * scale

    return {
        'x': nrm(ks[0], (BATCH, SEQ, D_MODEL), 1.0),
        'attn_norm_g': 1.0 + nrm(ks[1], (L, D_MODEL), 0.02),
        'w_in': nrm(ks[2], (L, D_MODEL, IN_WIDTH), D_MODEL ** -0.5),
        'attn_sinks': nrm(ks[3], (L, N_ATTN_HEADS), 1.0),
        'rwkv_mu': jax.random.uniform(ks[4], (L, RWKV_SHIFT_WIDTH), f32),
        'w0': jax.random.uniform(ks[5], (L, RWKV_WIDTH), f32, -4.0, 0.0),
        'w2': nrm(ks[6], (L, DECAY_LORA, RWKV_WIDTH), 0.5 * DECAY_LORA ** -0.5),
        'a0': nrm(ks[7], (L, RWKV_WIDTH), 0.1),
        'a2': nrm(ks[8], (L, ICLR_LORA, RWKV_WIDTH), ICLR_LORA ** -0.5),
        'g2': nrm(ks[9], (L, GATE_LORA, RWKV_WIDTH), GATE_LORA ** -0.5),
        'k_k': 0.85 + nrm(ks[10], (L, RWKV_WIDTH), 0.02),
        'k_a': 1.0 + nrm(ks[11], (L, RWKV_WIDTH), 0.02),
        'r_k': nrm(ks[12], (L, RWKV_WIDTH), 0.1),
        'ln_x_w': 1.0 + nrm(ks[13], (L, RWKV_WIDTH), 0.02),
        'ln_x_b': nrm(ks[14], (L, RWKV_WIDTH), 0.02),
        'w_out': nrm(ks[15], (L, MIX_WIDTH, D_MODEL), MIX_WIDTH ** -0.5),
        'mlp_norm_g': 1.0 + nrm(ks[16], (L, D_MODEL), 0.02),
        'w_up': nrm(ks[17], (L, D_MODEL, D_FF), D_MODEL ** -0.5),
        'w_down': nrm(ks[18], (L, D_FF, D_MODEL), D_FF ** -0.5),
        'final_norm_g': 1.0 + nrm(ks[19], (D_MODEL,), 0.02),
    }


def _fwd_reference(x, attn_norm_g, w_in, attn_sinks, rwkv_mu, w0, w2, a0, a2, g2, k_k, k_a, r_k,
              ln_x_w, ln_x_b, w_out, mlp_norm_g, w_up, w_down, final_norm_g):
    B, T, _ = x.shape
    c1 = ATTN_WIDTH
    c2 = c1 + KV_WIDTH
    c3 = c2 + KV_WIDTH
    r1 = RWKV_WIDTH
    r2 = 2 * RWKV_WIDTH
    r3 = 3 * RWKV_WIDTH
    r4 = r3 + DECAY_LORA
    r5 = r4 + ICLR_LORA
    for l in range(DEPTH):
        h = rms_norm(x, attn_norm_g[l])
        z = h @ w_in[l]
        q = z[..., :c1].reshape(B, T, N_ATTN_HEADS, HEAD_DIM)
        ka = z[..., c1:c2].reshape(B, T, N_KV_HEADS, HEAD_DIM)
        va = z[..., c2:c3].reshape(B, T, N_KV_HEADS, HEAD_DIM)
        attn_out = sliding_window_attention(q, ka, va, attn_sinks[l])
        zr = z[..., c3:]
        zr_prev = jnp.pad(zr, ((0, 0), (1, 0), (0, 0)))[:, :-1]
        zs = zr + (zr_prev - zr) * rwkv_mu[l]
        rwkv_out = rwkv7_time_mix(zs[..., :r1], zs[..., r1:r2], zs[..., r2:r3],
                                  zs[..., r3:r4], zs[..., r4:r5], zs[..., r5:],
                                  w0[l], w2[l], a0[l], a2[l], g2[l], k_k[l], k_a[l], r_k[l],
                                  ln_x_w[l], ln_x_b[l])
        x = x + jnp.concatenate([attn_out, rwkv_out], axis=-1) @ w_out[l]
        h = rms_norm(x, mlp_norm_g[l])
        x = x + jnp.square(jax.nn.relu(h @ w_up[l])) @ w_down[l]
    return rms_norm(x, final_norm_g)


import jax as _jax
import jax.numpy as _jnp

TWIN_FORMAT = 'train_step'
FWD_PARAMS = ['x', 'attn_norm_g', 'w_in', 'attn_sinks', 'rwkv_mu', 'w0', 'w2', 'a0', 'a2', 'g2', 'k_k', 'k_a', 'r_k', 'ln_x_w', 'ln_x_b', 'w_out', 'mlp_norm_g', 'w_up', 'w_down', 'final_norm_g']
TWIN_WEIGHTS = ['attn_norm_g', 'w_in', 'attn_sinks', 'rwkv_mu', 'w0', 'w2', 'a0', 'a2', 'g2', 'k_k', 'k_a', 'r_k', 'ln_x_w', 'ln_x_b', 'w_out', 'mlp_norm_g', 'w_up', 'w_down', 'final_norm_g']
TWIN_DIFF_INPUT = 'x'
TWIN_INPUTS = ['x', 'attn_norm_g', 'w_in', 'attn_sinks', 'rwkv_mu', 'w0', 'w2', 'a0', 'a2', 'g2', 'k_k', 'k_a', 'r_k', 'ln_x_w', 'ln_x_b', 'w_out', 'mlp_norm_g', 'w_up', 'w_down', 'final_norm_g', 'loss_target', 'm_attn_norm_g', 'm_w_in', 'm_attn_sinks', 'm_rwkv_mu', 'm_w0', 'm_w2', 'm_a0', 'm_a2', 'm_g2', 'm_k_k', 'm_k_a', 'm_r_k', 'm_ln_x_w', 'm_ln_x_b', 'm_w_out', 'm_mlp_norm_g', 'm_w_up', 'm_w_down', 'm_final_norm_g', 'v_attn_norm_g', 'v_w_in', 'v_attn_sinks', 'v_rwkv_mu', 'v_w0', 'v_w2', 'v_a0', 'v_a2', 'v_g2', 'v_k_k', 'v_k_a', 'v_r_k', 'v_ln_x_w', 'v_ln_x_b', 'v_w_out', 'v_mlp_norm_g', 'v_w_up', 'v_w_down', 'v_final_norm_g']
TWIN_OUTPUTS = ['loss', 'grad_x', 'grad_attn_norm_g', 'grad_w_in', 'grad_attn_sinks', 'grad_rwkv_mu', 'grad_w0', 'grad_w2', 'grad_a0', 'grad_a2', 'grad_g2', 'grad_k_k', 'grad_k_a', 'grad_r_k', 'grad_ln_x_w', 'grad_ln_x_b', 'grad_w_out', 'grad_mlp_norm_g', 'grad_w_up', 'grad_w_down', 'grad_final_norm_g', 'delta_attn_norm_g', 'delta_w_in', 'delta_attn_sinks', 'delta_rwkv_mu', 'delta_w0', 'delta_w2', 'delta_a0', 'delta_a2', 'delta_g2', 'delta_k_k', 'delta_k_a', 'delta_r_k', 'delta_ln_x_w', 'delta_ln_x_b', 'delta_w_out', 'delta_mlp_norm_g', 'delta_w_up', 'delta_w_down', 'delta_final_norm_g', 'new_m_attn_norm_g', 'new_m_w_in', 'new_m_attn_sinks', 'new_m_rwkv_mu', 'new_m_w0', 'new_m_w2', 'new_m_a0', 'new_m_a2', 'new_m_g2', 'new_m_k_k', 'new_m_k_a', 'new_m_r_k', 'new_m_ln_x_w', 'new_m_ln_x_b', 'new_m_w_out', 'new_m_mlp_norm_g', 'new_m_w_up', 'new_m_w_down', 'new_m_final_norm_g', 'new_v_attn_norm_g', 'new_v_w_in', 'new_v_attn_sinks', 'new_v_rwkv_mu', 'new_v_w0', 'new_v_w2', 'new_v_a0', 'new_v_a2', 'new_v_g2', 'new_v_k_k', 'new_v_k_a', 'new_v_r_k', 'new_v_ln_x_w', 'new_v_ln_x_b', 'new_v_w_out', 'new_v_mlp_norm_g', 'new_v_w_up', 'new_v_w_down', 'new_v_final_norm_g']
TWIN_LEAF_KINDS = {'loss': 'loss', 'grad_x': 'grad_x', 'grad_attn_norm_g': 'grad_w', 'grad_w_in': 'grad_w', 'grad_attn_sinks': 'grad_w', 'grad_rwkv_mu': 'grad_w', 'grad_w0': 'grad_w', 'grad_w2': 'grad_w', 'grad_a0': 'grad_w', 'grad_a2': 'grad_w', 'grad_g2': 'grad_w', 'grad_k_k': 'grad_w', 'grad_k_a': 'grad_w', 'grad_r_k': 'grad_w', 'grad_ln_x_w': 'grad_w', 'grad_ln_x_b': 'grad_w', 'grad_w_out': 'grad_w', 'grad_mlp_norm_g': 'grad_w', 'grad_w_up': 'grad_w', 'grad_w_down': 'grad_w', 'grad_final_norm_g': 'grad_w', 'delta_attn_norm_g': 'delta_w', 'delta_w_in': 'delta_w', 'delta_attn_sinks': 'delta_w', 'delta_rwkv_mu': 'delta_w', 'delta_w0': 'delta_w', 'delta_w2': 'delta_w', 'delta_a0': 'delta_w', 'delta_a2': 'delta_w', 'delta_g2': 'delta_w', 'delta_k_k': 'delta_w', 'delta_k_a': 'delta_w', 'delta_r_k': 'delta_w', 'delta_ln_x_w': 'delta_w', 'delta_ln_x_b': 'delta_w', 'delta_w_out': 'delta_w', 'delta_mlp_norm_g': 'delta_w', 'delta_w_up': 'delta_w', 'delta_w_down': 'delta_w', 'delta_final_norm_g': 'delta_w', 'new_m_attn_norm_g': 'new_m', 'new_m_w_in': 'new_m', 'new_m_attn_sinks': 'new_m', 'new_m_rwkv_mu': 'new_m', 'new_m_w0': 'new_m', 'new_m_w2': 'new_m', 'new_m_a0': 'new_m', 'new_m_a2': 'new_m', 'new_m_g2': 'new_m', 'new_m_k_k': 'new_m', 'new_m_k_a': 'new_m', 'new_m_r_k': 'new_m', 'new_m_ln_x_w': 'new_m', 'new_m_ln_x_b': 'new_m', 'new_m_w_out': 'new_m', 'new_m_mlp_norm_g': 'new_m', 'new_m_w_up': 'new_m', 'new_m_w_down': 'new_m', 'new_m_final_norm_g': 'new_m', 'new_v_attn_norm_g': 'new_v', 'new_v_w_in': 'new_v', 'new_v_attn_sinks': 'new_v', 'new_v_rwkv_mu': 'new_v', 'new_v_w0': 'new_v', 'new_v_w2': 'new_v', 'new_v_a0': 'new_v', 'new_v_a2': 'new_v', 'new_v_g2': 'new_v', 'new_v_k_k': 'new_v', 'new_v_k_a': 'new_v', 'new_v_r_k': 'new_v', 'new_v_ln_x_w': 'new_v', 'new_v_ln_x_b': 'new_v', 'new_v_w_out': 'new_v', 'new_v_mlp_norm_g': 'new_v', 'new_v_w_up': 'new_v', 'new_v_w_down': 'new_v', 'new_v_final_norm_g': 'new_v'}


def _forward(args):
    return _fwd_reference(*[args[k] for k in FWD_PARAMS])


def _output_shape():
    out = _jax.eval_shape(lambda: _forward(_fwd_setup_inputs(0)))
    return out.shape, out.dtype

N_MICROBATCH = 1
ADAM_LR = 0.001
ADAM_B1 = 0.9
ADAM_B2 = 0.999
ADAM_EPS = 1e-08
ADAM_WD = 0.01
ADAM_STEP = 10
PER_EXAMPLE_BATCH_AXIS = {'x': 0, 'loss_target': 0}
SHARED_INPUTS = []
_WEIGHT_DTYPES = {'attn_norm_g': _jnp.float32, 'w_in': _jnp.float32, 'attn_sinks': _jnp.float32, 'rwkv_mu': _jnp.float32, 'w0': _jnp.float32, 'w2': _jnp.float32, 'a0': _jnp.float32, 'a2': _jnp.float32, 'g2': _jnp.float32, 'k_k': _jnp.float32, 'k_a': _jnp.float32, 'r_k': _jnp.float32, 'ln_x_w': _jnp.float32, 'ln_x_b': _jnp.float32, 'w_out': _jnp.float32, 'mlp_norm_g': _jnp.float32, 'w_up': _jnp.float32, 'w_down': _jnp.float32, 'final_norm_g': _jnp.float32}
MOMENT_SCALE = {'attn_norm_g': 2.001259e-01, 'w_in': 1.221839e-01, 'attn_sinks': 1.357510e-01, 'rwkv_mu': 2.510389e-01, 'w0': 6.474439e-02, 'w2': 1.092317e-02, 'a0': 4.949625e-02, 'a2': 4.659376e-02, 'g2': 1.379251e-01, 'k_k': 8.869530e-02, 'k_a': 1.412764e-01, 'r_k': 2.883664e-01, 'ln_x_w': 1.285937e-01, 'ln_x_b': 1.902730e-01, 'w_out': 1.082520e-01, 'mlp_norm_g': 2.135448e-01, 'w_up': 1.083318e-01, 'w_down': 2.072501e-01, 'final_norm_g': 6.447126e+01}


def _to_microbatches(a, axis):
    t = _jnp.moveaxis(a, axis, 0)
    t = t.reshape((N_MICROBATCH, t.shape[0] // N_MICROBATCH) + t.shape[1:])
    return _jnp.moveaxis(t, 1, axis + 1)


def setup_inputs(seed: int = 0) -> dict:
    inp = _fwd_setup_inputs(seed)
    key = _jax.random.fold_in(_jax.random.key(seed), 7919)
    shape, _ = _output_shape()
    out = dict(inp)
    out["loss_target"] = _jax.random.normal(_jax.random.fold_in(key, 0), shape, _jnp.float32)
    for i, name in enumerate(TWIN_WEIGHTS):
        w = inp[name].astype(_jnp.float32)
        if MOMENT_SCALE is None:
            s = _jnp.sqrt(_jnp.mean(_jnp.square(w)) + 1e-30)
        else:
            s = MOMENT_SCALE[name]
        km, kv = _jax.random.split(_jax.random.fold_in(key, i + 1))
        out[name] = w
        out["m_" + name] = s * _jax.random.normal(km, w.shape, _jnp.float32)
        out["v_" + name] = (s * s) * _jax.random.uniform(kv, w.shape, _jnp.float32, 0.5, 1.5)
    if N_MICROBATCH > 1:
        for name, axis in PER_EXAMPLE_BATCH_AXIS.items():
            out[name] = _to_microbatches(out[name], axis)
    return {'x': out['x'], 'attn_norm_g': out['attn_norm_g'], 'w_in': out['w_in'], 'attn_sinks': out['attn_sinks'], 'rwkv_mu': out['rwkv_mu'], 'w0': out['w0'], 'w2': out['w2'], 'a0': out['a0'], 'a2': out['a2'], 'g2': out['g2'], 'k_k': out['k_k'], 'k_a': out['k_a'], 'r_k': out['r_k'], 'ln_x_w': out['ln_x_w'], 'ln_x_b': out['ln_x_b'], 'w_out': out['w_out'], 'mlp_norm_g': out['mlp_norm_g'], 'w_up': out['w_up'], 'w_down': out['w_down'], 'final_norm_g': out['final_norm_g'], 'loss_target': out['loss_target'], 'm_attn_norm_g': out['m_attn_norm_g'], 'm_w_in': out['m_w_in'], 'm_attn_sinks': out['m_attn_sinks'], 'm_rwkv_mu': out['m_rwkv_mu'], 'm_w0': out['m_w0'], 'm_w2': out['m_w2'], 'm_a0': out['m_a0'], 'm_a2': out['m_a2'], 'm_g2': out['m_g2'], 'm_k_k': out['m_k_k'], 'm_k_a': out['m_k_a'], 'm_r_k': out['m_r_k'], 'm_ln_x_w': out['m_ln_x_w'], 'm_ln_x_b': out['m_ln_x_b'], 'm_w_out': out['m_w_out'], 'm_mlp_norm_g': out['m_mlp_norm_g'], 'm_w_up': out['m_w_up'], 'm_w_down': out['m_w_down'], 'm_final_norm_g': out['m_final_norm_g'], 'v_attn_norm_g': out['v_attn_norm_g'], 'v_w_in': out['v_w_in'], 'v_attn_sinks': out['v_attn_sinks'], 'v_rwkv_mu': out['v_rwkv_mu'], 'v_w0': out['v_w0'], 'v_w2': out['v_w2'], 'v_a0': out['v_a0'], 'v_a2': out['v_a2'], 'v_g2': out['v_g2'], 'v_k_k': out['v_k_k'], 'v_k_a': out['v_k_a'], 'v_r_k': out['v_r_k'], 'v_ln_x_w': out['v_ln_x_w'], 'v_ln_x_b': out['v_ln_x_b'], 'v_w_out': out['v_w_out'], 'v_mlp_norm_g': out['v_mlp_norm_g'], 'v_w_up': out['v_w_up'], 'v_w_down': out['v_w_down'], 'v_final_norm_g': out['v_final_norm_g']}


def _loss(weights, diff, rest, loss_target):
    with _jax.named_scope("forward"):
        args = {**rest, TWIN_DIFF_INPUT: diff, **{k: w.astype(_WEIGHT_DTYPES[k]) for k, w in weights.items()}}
        y = _forward(args)
    with _jax.named_scope("loss_head"):
        err = _jnp.square(y.astype(_jnp.float32) - loss_target)
        return 0.5 * _jnp.sum(_jnp.mean(err, axis=-1)) if err.ndim else 0.5 * err


def _adamw(w, g, m, v):
    m = ADAM_B1 * m + (1.0 - ADAM_B1) * g
    v = ADAM_B2 * v + (1.0 - ADAM_B2) * _jnp.square(g)
    m_hat = m / (1.0 - ADAM_B1 ** ADAM_STEP)
    v_hat = v / (1.0 - ADAM_B2 ** ADAM_STEP)
    delta = -ADAM_LR * (m_hat / (_jnp.sqrt(v_hat) + ADAM_EPS) + ADAM_WD * w)
    return delta, m, v


def reference(x, attn_norm_g, w_in, attn_sinks, rwkv_mu, w0, w2, a0, a2, g2, k_k, k_a, r_k, ln_x_w, ln_x_b, w_out, mlp_norm_g, w_up, w_down, final_norm_g, loss_target, m_attn_norm_g, m_w_in, m_attn_sinks, m_rwkv_mu, m_w0, m_w2, m_a0, m_a2, m_g2, m_k_k, m_k_a, m_r_k, m_ln_x_w, m_ln_x_b, m_w_out, m_mlp_norm_g, m_w_up, m_w_down, m_final_norm_g, v_attn_norm_g, v_w_in, v_attn_sinks, v_rwkv_mu, v_w0, v_w2, v_a0, v_a2, v_g2, v_k_k, v_k_a, v_r_k, v_ln_x_w, v_ln_x_b, v_w_out, v_mlp_norm_g, v_w_up, v_w_down, v_final_norm_g):
    given = dict(x=x, attn_norm_g=attn_norm_g, w_in=w_in, attn_sinks=attn_sinks, rwkv_mu=rwkv_mu, w0=w0, w2=w2, a0=a0, a2=a2, g2=g2, k_k=k_k, k_a=k_a, r_k=r_k, ln_x_w=ln_x_w, ln_x_b=ln_x_b, w_out=w_out, mlp_norm_g=mlp_norm_g, w_up=w_up, w_down=w_down, final_norm_g=final_norm_g, loss_target=loss_target, m_attn_norm_g=m_attn_norm_g, m_w_in=m_w_in, m_attn_sinks=m_attn_sinks, m_rwkv_mu=m_rwkv_mu, m_w0=m_w0, m_w2=m_w2, m_a0=m_a0, m_a2=m_a2, m_g2=m_g2, m_k_k=m_k_k, m_k_a=m_k_a, m_r_k=m_r_k, m_ln_x_w=m_ln_x_w, m_ln_x_b=m_ln_x_b, m_w_out=m_w_out, m_mlp_norm_g=m_mlp_norm_g, m_w_up=m_w_up, m_w_down=m_w_down, m_final_norm_g=m_final_norm_g, v_attn_norm_g=v_attn_norm_g, v_w_in=v_w_in, v_attn_sinks=v_attn_sinks, v_rwkv_mu=v_rwkv_mu, v_w0=v_w0, v_w2=v_w2, v_a0=v_a0, v_a2=v_a2, v_g2=v_g2, v_k_k=v_k_k, v_k_a=v_k_a, v_r_k=v_r_k, v_ln_x_w=v_ln_x_w, v_ln_x_b=v_ln_x_b, v_w_out=v_w_out, v_mlp_norm_g=v_mlp_norm_g, v_w_up=v_w_up, v_w_down=v_w_down, v_final_norm_g=v_final_norm_g)
    weights = {n: given[n] for n in TWIN_WEIGHTS}
    shared = {n: given[n] for n in SHARED_INPUTS}
    per_example = {n: given[n] for n in ['x']}
    grad_fn = _jax.value_and_grad(_loss, argnums=(0, 1))

    def one_microbatch(ex, loss_target):
        ex = dict(ex)
        diff = ex.pop(TWIN_DIFF_INPUT)
        return grad_fn(weights, diff, {**shared, **ex}, loss_target)

    if N_MICROBATCH == 1:
        loss, (grad_w, grad_x) = one_microbatch(per_example, given["loss_target"])
    else:
        def body(carry, xs):
            loss_sum, grad_sum = carry
            l_k, (gw_k, gx_k) = one_microbatch(xs[0], xs[1])
            with _jax.named_scope("update"):
                return (loss_sum + l_k, _jax.tree.map(_jnp.add, grad_sum, gw_k)), gx_k

        init = (_jnp.zeros((), _jnp.float32), _jax.tree.map(_jnp.zeros_like, weights))
        (loss, grad_w), grad_x = _jax.lax.scan(body, init, (per_example, given["loss_target"]))
    with _jax.named_scope("update"):
        delta_w, new_m, new_v = {}, {}, {}
        for n in TWIN_WEIGHTS:
            delta_w[n], new_m[n], new_v[n] = _adamw(weights[n], grad_w[n], given["m_" + n], given["v_" + n])
    return (loss, grad_x, *[grad_w[n] for n in TWIN_WEIGHTS], *[delta_w[n] for n in TWIN_WEIGHTS],
            *[new_m[n] for n in TWIN_WEIGHTS], *[new_v[n] for n in TWIN_WEIGHTS])
```

```python
import functools
import math

import jax
import jax.numpy as jnp
from jax import lax
from jax.experimental import pallas as pl
from jax.experimental.pallas import tpu as pltpu

F32 = jnp.float32
BF16 = jnp.bfloat16
HI = lax.Precision.HIGHEST

N_DEV = 8
HEAD_DIM = 64
LANES = 128
N_ATTN_HEADS = 8
ATTN_WIDTH = 512
KV_WIDTH = 128
QKV_WIDTH = ATTN_WIDTH + 2 * KV_WIDTH
RWKV_WIDTH = 512
LORA_WA = 128
GATE_LORA = 128
RWKV_SHIFT_WIDTH = 3 * RWKV_WIDTH + LORA_WA + GATE_LORA
BLOCK = 128
CHUNK = 64
RMS_EPS = 1e-6
GN_EPS = 64e-5
L2_EPS = 1e-12
NEG_INF = -1e30
DECAY_SCALE = math.exp(-0.5)
ADAM_LR, ADAM_B1, ADAM_B2, ADAM_EPS, ADAM_WD, ADAM_STEP = 0.001, 0.9, 0.999, 1e-08, 0.01, 10

NN = (((1,), (0,)), ((), ()))
NT = (((1,), (1,)), ((), ()))
TN = (((0,), (0,)), ((), ()))
MESH = pl.DeviceIdType.MESH


def _dot(a, b, dn=NN, precision=None):
    return lax.dot_general(a, b, dn, precision=precision, preferred_element_type=F32)


def _bdot_raw(a, b, dn):
    return lax.dot_general(a.astype(BF16), b.astype(BF16), dn, preferred_element_type=F32)


@jax.custom_vjp
def _bdot_nn(a, b):
    return _bdot_raw(a, b, NN)


_bdot_nn.defvjp(lambda a, b: (_bdot_raw(a, b, NN), (a, b)),
                lambda res, ct: (_bdot_raw(ct, res[1], NT), _bdot_raw(res[0], ct, TN)))


@jax.custom_vjp
def _bdot_nt(a, b):
    return _bdot_raw(a, b, NT)


_bdot_nt.defvjp(lambda a, b: (_bdot_raw(a, b, NT), (a, b)),
                lambda res, ct: (_bdot_raw(ct, res[1], NN), _bdot_raw(ct, res[0], TN)))


def _sigmoid(x):
    return 1.0 / (1.0 + jnp.exp(-x))


def _pick(n, cands):
    for c in cands:
        if n % c == 0:
            return c
    return n


def _cparams(sem, vmem_mb=None):
    kw = dict(dimension_semantics=sem)
    if vmem_mb is not None:
        kw["vmem_limit_bytes"] = vmem_mb * 1024 * 1024
    return pltpu.CompilerParams(**kw)


def _matmul(a, b, mode, *, name, extras=(), epilogue=None, out_dtypes=(F32,), tm=512, tn=512, tk=512):
    if mode == "nn":
        (M, K), (_, N) = a.shape, b.shape
    elif mode == "tn":
        (K, M), (_, N) = a.shape, b.shape
    else:
        (M, K), (N, _) = a.shape, b.shape
    tm = _pick(M, (tm, 512, 256, 128))
    tn = _pick(N, (tn, 512, 384, 256, 128))
    tk = _pick(K, (tk, 512, 256, 128))
    nk = K // tk
    ne, nout = len(extras), len(out_dtypes)
    if mode == "nn":
        a_spec = pl.BlockSpec((tm, tk), lambda i, j, k: (i, k))
        b_spec = pl.BlockSpec((tk, tn), lambda i, j, k: (k, j))
        dn = NN
    elif mode == "tn":
        a_spec = pl.BlockSpec((tk, tm), lambda i, j, k: (k, i))
        b_spec = pl.BlockSpec((tk, tn), lambda i, j, k: (k, j))
        dn = TN
    else:
        a_spec = pl.BlockSpec((tm, tk), lambda i, j, k: (i, k))
        b_spec = pl.BlockSpec((tn, tk), lambda i, j, k: (j, k))
        dn = NT
    o_spec = pl.BlockSpec((tm, tn), lambda i, j, k: (i, j))

    def body(*refs):
        a_ref, b_ref = refs[:2]
        e_refs = refs[2:2 + ne]
        o_refs = refs[2 + ne:2 + ne + nout]
        acc = refs[-1]
        kstep = pl.program_id(2)

        @pl.when(kstep == 0)
        def _():
            acc[...] = jnp.zeros_like(acc)

        acc[...] += _bdot_raw(a_ref[...], b_ref[...], dn)

        @pl.when(kstep == nk - 1)
        def _():
            if epilogue is None:
                outs = (acc[...],)
            else:
                outs = epilogue(acc[...], *[e[...] for e in e_refs])
            for o_ref, o in zip(o_refs, outs):
                o_ref[...] = o.astype(o_ref.dtype)

    outs = pl.pallas_call(
        body,
        grid=(M // tm, N // tn, nk),
        in_specs=[a_spec, b_spec] + [o_spec] * ne,
        out_specs=[o_spec] * nout,
        out_shape=[jax.ShapeDtypeStruct((M, N), dt) for dt in out_dtypes],
        scratch_shapes=[pltpu.VMEM((tm, tn), F32)],
        compiler_params=_cparams(("parallel", "parallel", "arbitrary"), 48),
        name=name,
    )(a, b, *extras)
    return outs[0] if nout == 1 else outs


def _rowwise(fn, rows, pars, out_rows, out_accs, *, tile, name):
    rows = [r if isinstance(r, tuple) else (r, r.shape[1], 0) for r in rows]
    R = rows[0][0].shape[0]
    tile = min(tile, R)
    nr, npar, nor, noa = len(rows), len(pars), len(out_rows), len(out_accs)

    def body(*refs):
        rin = refs[:nr]
        pin = refs[nr:nr + npar]
        orow = refs[nr + npar:nr + npar + nor]
        oacc = refs[nr + npar + nor:]
        outs = fn(*[r[...] for r in rin], *[p[...] for p in pin])
        for ref, o in zip(orow, outs[:nor]):
            if isinstance(o, (tuple, list)):
                col = 0
                for piece in o:
                    ref[:, col:col + piece.shape[1]] = piece.astype(ref.dtype)
                    col += piece.shape[1]
            else:
                ref[...] = o.astype(ref.dtype)
        step = pl.program_id(0)

        def accumulate(ref, o):
            @pl.when(step == 0)
            def _():
                ref[...] = o

            @pl.when(step > 0)
            def _():
                ref[...] += o

        for ref, o in zip(oacc, outs[nor:]):
            accumulate(ref, o)

    def colspec(width, cb):
        return pl.BlockSpec((tile, width), lambda i: (i, cb))

    return pl.pallas_call(
        body,
        grid=(R // tile,),
        in_specs=[colspec(w, cb) for (_, w, cb) in rows]
        + [pl.BlockSpec(p.shape, lambda i: (0, 0)) for p in pars],
        out_specs=[colspec(w, 0) for (w, _) in out_rows]
        + [pl.BlockSpec(s, lambda i: (0, 0)) for s in out_accs],
        out_shape=[jax.ShapeDtypeStruct((R, w), dt) for (w, dt) in out_rows]
        + [jax.ShapeDtypeStruct(s, F32) for s in out_accs],
        compiler_params=_cparams(("arbitrary",), 56),
        name=name,
    )(*[r[0] for r in rows], *pars)


def _rms_fn(x, g):
    return x * lax.rsqrt(jnp.mean(x * x, axis=-1, keepdims=True) + RMS_EPS) * g


def _norm_fwd(x, g, name):
    return _rowwise(lambda xv, gv: (_rms_fn(xv, gv),), [x], [g], [(x.shape[1], BF16)], [], tile=512, name=name)[0]


def _norm_bwd(x, dh, dres, g, name):
    def fn(xv, dhv, dresv, gv):
        _, vjp = jax.vjp(_rms_fn, xv, gv)
        dx, dg = vjp(dhv)
        return dx + dresv, dg

    return _rowwise(fn, [x, dh, dres], [g], [(x.shape[1], F32)], [g.shape], tile=256, name=name)


def _final_loss(x, tgt, g):
    d = x.shape[1]

    def fn(xv, tv, gv):
        y, vjp = jax.vjp(_rms_fn, xv, gv)
        err = y - tv
        loss = 0.5 * jnp.sum(jnp.sum(err * err, axis=-1, keepdims=True), axis=0, keepdims=True) / d
        dx, dg = vjp(err / d)
        return dx, jnp.broadcast_to(loss, (1, LANES)), dg

    return _rowwise(fn, [x, tgt], [g], [(d, F32)], [(1, LANES), g.shape], tile=256, name="final_norm_loss")


def _head_sum_matrix():
    i = lax.broadcasted_iota(jnp.int32, (RWKV_WIDTH, RWKV_WIDTH), 0) // HEAD_DIM
    j = lax.broadcasted_iota(jnp.int32, (RWKV_WIDTH, RWKV_WIDTH), 1) // HEAD_DIM
    return (i == j).astype(F32)


def _prep_core(xr, xk, xv, xwa, xg, w0, w2p, a0, a2p, g2, k_k, k_a, esum):
    lw = -DECAY_SCALE * _sigmoid(w0 + _bdot_nn(jnp.tanh(xwa), w2p))
    a = _sigmoid(a0 + _bdot_nn(xwa, a2p))
    g = _bdot_nn(_sigmoid(xg), g2)
    kk0 = xk * k_k
    kk = kk0 / jnp.maximum(jnp.sqrt(_dot(kk0 * kk0, esum, precision=HI)), L2_EPS)
    k = xk * (1.0 + (a - 1.0) * k_a)
    return xr, lw, k, xv, kk, a, g


_SEGS = ((0, 512), (512, 1024), (1024, 1536), (1536, 1664), (1664, 1792))


def _prep_fwd(zc, zp, mu, pars):
    def fn(zcv, zpv, muv, *pv):
        zs = zcv + (zpv - zcv) * muv
        return _prep_core(*[zs[:, a:b] for a, b in _SEGS], *pv)

    return _rowwise(fn, [zc, zp], [mu, *pars], [(RWKV_WIDTH, F32)] * 7, [], tile=256, name="rwkv_prep_fwd")


def _prep_bwd(zc, zp, cts, mu, pars):
    npar = len(pars)

    def fn(zcv, zpv, dra, drb, dlw, dka, dkb, dva, dvb, dkk, da, dg, muv, *pv):
        diff = zpv - zcv
        zs = zcv + diff * muv
        segs = [zs[:, a:b] for a, b in _SEGS]
        esum = pv[-1]
        _, vjp = jax.vjp(lambda *args: _prep_core(*args, esum), *segs, *pv[:-1])
        grads = vjp((dra + drb, dlw, dka + dkb, dva + dvb, dkk, da, dg))
        dsegs, dpars = grads[:5], grads[5:]
        dzc = [ds * (1.0 - muv[:, a:b]) for ds, (a, b) in zip(dsegs, _SEGS)]
        dzp = [ds * muv[:, a:b] for ds, (a, b) in zip(dsegs, _SEGS)]
        dmu = [jnp.sum(ds * diff[:, a:b], axis=0, keepdims=True) for ds, (a, b) in zip(dsegs, _SEGS)]
        return (dzc, dzp, *dmu, *dpars)

    seg_shapes = [(1, b - a) for a, b in _SEGS]
    return _rowwise(fn, [zc, zp, *cts], [mu, *pars],
                    [(RWKV_SHIFT_WIDTH, F32)] * 2, seg_shapes + [p.shape for p in pars[:-1]],
                    tile=256, name="rwkv_prep_bwd")


def _post_fn(y, r, k, v, g, ln_w, ln_b, r_k, esum):
    mean = _dot(y, esum, precision=HI) * (1.0 / HEAD_DIM)
    yc = y - mean
    var = _dot(yc * yc, esum, precision=HI) * (1.0 / HEAD_DIM)
    yn = yc * lax.rsqrt(var + GN_EPS) * ln_w + ln_b
    bonus = _dot(r * k * r_k, esum, precision=HI) * v
    return (yn + bonus) * g


def _post_fwd(y, r, k, v, g, pars):
    return _rowwise(lambda *a: (_post_fn(*a),), [y, r, k, v, g], pars, [(RWKV_WIDTH, BF16)], [],
                    tile=256, name="rwkv_post_fwd")[0]


def _post_bwd(y, r, k, v, g, dout, pars):
    def fn(yv, rv, kv, vv, gv, dv_, ln_w, ln_b, r_k, esum):
        _, vjp = jax.vjp(lambda *a: _post_fn(*a, esum), yv, rv, kv, vv, gv, ln_w, ln_b, r_k)
        return vjp(dv_)

    return _rowwise(fn, [y, r, k, v, g, dout], pars, [(RWKV_WIDTH, F32)] * 5, [p.shape for p in pars[:-1]],
                    tile=256, name="rwkv_post_bwd")


def _tri_inverse(m, n):
    row = lax.broadcasted_iota(jnp.int32, (n, n), 0)
    col = lax.broadcasted_iota(jnp.int32, (n, n), 1)
    t = jnp.where(row == col, 1.0, 0.0) + m
    p = m
    for _ in range(int(math.log2(n)) - 1):
        p = _dot(p, p, precision=HI)
        t = t + _dot(t, p, precision=HI)
    return t


def _chunk_fn(s, r, lw, k, v, kk, al):
    c = r.shape[0]
    row = lax.broadcasted_iota(jnp.int32, (c, c), 0)
    col = lax.broadcasted_iota(jnp.int32, (c, c), 1)
    incl = row >= col
    strict = row > col
    lane = lax.broadcasted_iota(jnp.int32, (1, LANES), 1)
    srow = lax.broadcasted_iota(jnp.int32, (LANES, LANES), 0) // HEAD_DIM
    scol = lax.broadcasted_iota(jnp.int32, (LANES, LANES), 1) // HEAD_DIM

    cum = _dot(jnp.where(incl, 1.0, 0.0), lw, precision=HI)
    total = jnp.sum(lw, axis=0, keepdims=True)
    b = kk * al
    a_t = -kk * jnp.exp(cum - lw)
    grow = jnp.exp(-cum)
    b_t = b * grow
    k_t = k * grow
    r_t = r * jnp.exp(cum)
    tail = jnp.exp(total - cum)
    x0 = _dot(a_t, s, NT, precision=HI)
    y = _dot(r_t, s, NT, precision=HI)
    u = jnp.zeros_like(r)
    for half in range(2):
        m = jnp.where((lane // HEAD_DIM) == half, 1.0, 0.0)
        a_h = a_t * m
        r_h = r_t * m
        m_ab = jnp.where(strict, _dot(a_h, b_t, NT, precision=HI), 0.0)
        m_ak = jnp.where(strict, _dot(a_h, k_t, NT, precision=HI), 0.0)
        m_rb = jnp.where(incl, _dot(r_h, b_t, NT, precision=HI), 0.0)
        m_rk = jnp.where(incl, _dot(r_h, k_t, NT, precision=HI), 0.0)
        t_inv = _tri_inverse(m_ab, c)
        u_h = _dot(t_inv, x0 + _dot(m_ak, v, precision=HI), precision=HI) * m
        u = u + u_h
        y = y + (_dot(m_rb, u_h, precision=HI) + _dot(m_rk, v, precision=HI)) * m
    s_new = s * jnp.exp(total) + _dot(u, b * tail, TN, precision=HI) + _dot(v, k * tail, TN, precision=HI)
    return y, jnp.where(srow == scol, s_new, 0.0)


def _rwkv_fwd(r, lw, k, v, kk, al):
    bsz, t, w = r.shape
    npair, nchunk = w // LANES, t // CHUNK

    def body(r_ref, lw_ref, k_ref, v_ref, kk_ref, al_ref, y_ref, sall_ref, s_scr):
        @pl.when(pl.program_id(2) == 0)
        def _():
            s_scr[...] = jnp.zeros_like(s_scr)

        s = s_scr[...]
        sall_ref[0, 0, 0] = s
        y, s_new = _chunk_fn(s, r_ref[0], lw_ref[0], k_ref[0], v_ref[0], kk_ref[0], al_ref[0])
        y_ref[0] = y
        s_scr[...] = s_new

    spec = pl.BlockSpec((1, CHUNK, LANES), lambda b, p, c: (b, c, p))
    return pl.pallas_call(
        body,
        grid=(bsz, npair, nchunk),
        in_specs=[spec] * 6,
        out_specs=[spec, pl.BlockSpec((1, 1, 1, LANES, LANES), lambda b, p, c: (b, p, c, 0, 0))],
        out_shape=[jax.ShapeDtypeStruct((bsz, t, w), F32),
                   jax.ShapeDtypeStruct((bsz, npair, nchunk, LANES, LANES), F32)],
        scratch_shapes=[pltpu.VMEM((LANES, LANES), F32)],
        compiler_params=_cparams(("parallel", "parallel", "arbitrary"), 48),
        name="rwkv_chunk_fwd",
    )(r, lw, k, v, kk, al)


def _rwkv_bwd(r, lw, k, v, kk, al, s_all, dy):
    bsz, t, w = r.shape
    npair, nchunk = w // LANES, t // CHUNK

    def body(r_ref, lw_ref, k_ref, v_ref, kk_ref, al_ref, s_ref, dy_ref, *rest):
        out_refs, ds_scr = rest[:6], rest[6]

        @pl.when(pl.program_id(2) == 0)
        def _():
            ds_scr[...] = jnp.zeros_like(ds_scr)

        _, vjp = jax.vjp(_chunk_fn, s_ref[0, 0, 0], r_ref[0], lw_ref[0], k_ref[0], v_ref[0], kk_ref[0], al_ref[0])
        grads = vjp((dy_ref[0], ds_scr[...]))
        ds_scr[...] = grads[0]
        for ref, gval in zip(out_refs, grads[1:]):
            ref[0] = gval

    spec = pl.BlockSpec((1, CHUNK, LANES), lambda b, p, c: (b, nchunk - 1 - c, p))
    sspec = pl.BlockSpec((1, 1, 1, LANES, LANES), lambda b, p, c: (b, p, nchunk - 1 - c, 0, 0))
    return pl.pallas_call(
        body,
        grid=(bsz, npair, nchunk),
        in_specs=[spec] * 6 + [sspec, spec],
        out_specs=[spec] * 6,
        out_shape=[jax.ShapeDtypeStruct((bsz, t, w), F32)] * 6,
        scratch_shapes=[pltpu.VMEM((LANES, LANES), F32)],
        compiler_params=_cparams(("parallel", "parallel", "arbitrary"), 48),
        name="rwkv_chunk_bwd",
    )(r, lw, k, v, kk, al, s_all, dy)


def _alibi_slope(head):
    return 2.0 ** (-8.0 * (head + 1) / N_ATTN_HEADS)


def _attn_pair(q, kp, kc, vp, vc, sinks, first, pair):
    kvh = pair // 2
    row = lax.broadcasted_iota(jnp.int32, (BLOCK, BLOCK), 0)
    col = lax.broadcasted_iota(jnp.int32, (BLOCK, BLOCK), 1)
    lane = lax.broadcasted_iota(jnp.int32, (1, LANES), 1)
    swap = jnp.where((row + HEAD_DIM) % LANES == col, 1.0, 0.0)
    dist_c = (row - col).astype(F32)
    dist_p = dist_c + float(BLOCK)
    valid_c = row >= col
    valid_p = jnp.logical_and(col > row, jnp.logical_not(first))
    out = jnp.zeros_like(q)
    psinks = []
    for half in range(2):
        head = 2 * pair + half
        m = jnp.where((lane // HEAD_DIM) == half, 1.0, 0.0)
        if half == kvh:
            kpx, kcx, vpx, vcx = kp, kc, vp, vc
        else:
            kpx, kcx, vpx, vcx = (_bdot_nn(t, swap) for t in (kp, kc, vp, vc))
        qa = q * m
        slope = _alibi_slope(head)
        sp = jnp.where(valid_p, _bdot_nt(qa, kpx) * (HEAD_DIM ** -0.5) - slope * dist_p, NEG_INF)
        sc = jnp.where(valid_c, _bdot_nt(qa, kcx) * (HEAD_DIM ** -0.5) - slope * dist_c, NEG_INF)
        sink = sinks[half]
        mx = jnp.maximum(jnp.maximum(jnp.max(sp, axis=-1, keepdims=True), jnp.max(sc, axis=-1, keepdims=True)), sink)
        mx = lax.stop_gradient(mx)
        ep = jnp.exp(sp - mx)
        ec = jnp.exp(sc - mx)
        es = jnp.exp(sink - mx)
        inv = 1.0 / (jnp.sum(ep, axis=-1, keepdims=True) + jnp.sum(ec, axis=-1, keepdims=True) + es)
        o = _bdot_nn(ep * inv, vpx) + _bdot_nn(ec * inv, vcx)
        out = out + o * m
        psinks.append(lax.stop_gradient(es * inv))
    return out, psinks


def _sink_values(sink_ref, pair):
    return [jnp.max(sink_ref[2 * pair + half:2 * pair + half + 1, :], axis=-1, keepdims=True) for half in range(2)]


def _attn_fwd(z, sink_rows):
    bsz, t, _ = z.shape
    nb = t // BLOCK
    npair = ATTN_WIDTH // LANES

    def body(q_ref, kp_ref, kc_ref, vp_ref, vc_ref, sink_ref, o_ref):
        first = pl.program_id(1) == 0
        for pair in range(npair):
            q = q_ref[0, :, pair * LANES:(pair + 1) * LANES]
            o, _ = _attn_pair(q, kp_ref[0], kc_ref[0], vp_ref[0], vc_ref[0], _sink_values(sink_ref, pair), first, pair)
            o_ref[0, :, pair * LANES:(pair + 1) * LANES] = o.astype(o_ref.dtype)

    kcol, vcol = ATTN_WIDTH // KV_WIDTH, ATTN_WIDTH // KV_WIDTH + 1
    return pl.pallas_call(
        body,
        grid=(bsz, nb),
        in_specs=[pl.BlockSpec((1, BLOCK, ATTN_WIDTH), lambda b, n: (b, n, 0)),
                  pl.BlockSpec((1, BLOCK, KV_WIDTH), lambda b, n: (b, jnp.maximum(n - 1, 0), kcol)),
                  pl.BlockSpec((1, BLOCK, KV_WIDTH), lambda b, n: (b, n, kcol)),
                  pl.BlockSpec((1, BLOCK, KV_WIDTH), lambda b, n: (b, jnp.maximum(n - 1, 0), vcol)),
                  pl.BlockSpec((1, BLOCK, KV_WIDTH), lambda b, n: (b, n, vcol)),
                  pl.BlockSpec(sink_rows.shape, lambda b, n: (0, 0))],
        out_specs=pl.BlockSpec((1, BLOCK, ATTN_WIDTH), lambda b, n: (b, n, 0)),
        out_shape=jax.ShapeDtypeStruct((bsz, t, ATTN_WIDTH), BF16),
        compiler_params=_cparams(("parallel", "arbitrary"), 48),
        name="swa_fwd",
    )(z, z, z, z, z, sink_rows)


def _attn_bwd(z, dout, sink_rows):
    bsz, t, _ = z.shape
    nb = t // BLOCK
    npair = ATTN_WIDTH // LANES

    def body(q_ref, kp_ref, kc_ref, vp_ref, vc_ref, do_ref, sink_ref, dz_ref, dsink_ref, carry):
        step = pl.program_id(1)
        n = nb - 1 - step
        first = n == 0

        @pl.when(step == 0)
        def _():
            carry[...] = jnp.zeros_like(carry)

        @pl.when(jnp.logical_and(step == 0, pl.program_id(0) == 0))
        def _():
            dsink_ref[...] = jnp.zeros_like(dsink_ref)

        dkp = jnp.zeros((BLOCK, KV_WIDTH), F32)
        dkc, dvp, dvc = dkp, dkp, dkp
        lane = lax.broadcasted_iota(jnp.int32, (1, LANES), 1)
        for pair in range(npair):
            cols = slice(pair * LANES, (pair + 1) * LANES)
            sinks = _sink_values(sink_ref, pair)
            fn = functools.partial(_attn_pair, sinks=sinks, first=first, pair=pair)
            o, vjp, psinks = jax.vjp(fn, q_ref[0, :, cols], kp_ref[0], kc_ref[0], vp_ref[0], vc_ref[0], has_aux=True)
            do = do_ref[0, :, cols]
            dq, g_kp, g_kc, g_vp, g_vc = vjp(do)
            dz_ref[0, :, cols] = dq
            dkp, dkc, dvp, dvc = dkp + g_kp, dkc + g_kc, dvp + g_vp, dvc + g_vc
            for half in range(2):
                m = jnp.where((lane // HEAD_DIM) == half, 1.0, 0.0)
                delta = jnp.sum(do * o * m, axis=-1, keepdims=True)
                ds = -jnp.sum(psinks[half] * delta, axis=0, keepdims=True)
                head = 2 * pair + half
                dsink_ref[head:head + 1, :] += jnp.broadcast_to(ds, (1, LANES))
        dz_ref[0, :, ATTN_WIDTH:ATTN_WIDTH + KV_WIDTH] = dkc + carry[0]
        dz_ref[0, :, ATTN_WIDTH + KV_WIDTH:QKV_WIDTH] = dvc + carry[1]
        carry[0] = dkp
        carry[1] = dvp

    kcol, vcol = ATTN_WIDTH // KV_WIDTH, ATTN_WIDTH // KV_WIDTH + 1
    rev = lambda n: nb - 1 - n
    return pl.pallas_call(
        body,
        grid=(bsz, nb),
        in_specs=[pl.BlockSpec((1, BLOCK, ATTN_WIDTH), lambda b, n: (b, rev(n), 0)),
                  pl.BlockSpec((1, BLOCK, KV_WIDTH), lambda b, n: (b, jnp.maximum(rev(n) - 1, 0), kcol)),
                  pl.BlockSpec((1, BLOCK, KV_WIDTH), lambda b, n: (b, rev(n), kcol)),
                  pl.BlockSpec((1, BLOCK, KV_WIDTH), lambda b, n: (b, jnp.maximum(rev(n) - 1, 0), vcol)),
                  pl.BlockSpec((1, BLOCK, KV_WIDTH), lambda b, n: (b, rev(n), vcol)),
                  pl.BlockSpec((1, BLOCK, ATTN_WIDTH), lambda b, n: (b, rev(n), 0)),
                  pl.BlockSpec(sink_rows.shape, lambda b, n: (0, 0))],
        out_specs=[pl.BlockSpec((1, BLOCK, QKV_WIDTH), lambda b, n: (b, rev(n), 0)),
                   pl.BlockSpec((N_ATTN_HEADS, LANES), lambda b, n: (0, 0))],
        out_shape=[jax.ShapeDtypeStruct((bsz, t, QKV_WIDTH), F32),
                   jax.ShapeDtypeStruct((N_ATTN_HEADS, LANES), F32)],
        scratch_shapes=[pltpu.VMEM((2, BLOCK, KV_WIDTH), F32)],
        compiler_params=_cparams(("arbitrary", "arbitrary"), 48),
        name="swa_bwd",
    )(z, z, z, z, z, dout, sink_rows)


def _exchange(arrays, *, scatter, name):
    n = len(arrays)
    out_shape = [jax.ShapeDtypeStruct((N_DEV,) + (a.shape[1:] if scatter else a.shape), a.dtype) for a in arrays]

    def body(*refs):
        ins, outs = refs[:n], refs[n:2 * n]
        send_sems, recv_sems, local_sems = refs[2 * n:]
        x, y, c = lax.axis_index("x"), lax.axis_index("y"), lax.axis_index("c")
        me = 4 * x + 2 * y + c
        pending = []
        for i in range(n):
            own = pltpu.make_async_copy(ins[i].at[me] if scatter else ins[i], outs[i].at[me], local_sems.at[i])
            own.start()
            pending.append((own, None))
            for d in range(1, N_DEV):
                px = 1 - x if d & 4 else x
                py = 1 - y if d & 2 else y
                pc = 1 - c if d & 1 else c
                peer = 4 * px + 2 * py + pc
                src = ins[i].at[peer] if scatter else ins[i]
                send = pltpu.make_async_remote_copy(src, outs[i].at[me], send_sems.at[i, d - 1], recv_sems.at[i, d - 1],
                                                    device_id=(px, py, pc), device_id_type=MESH)
                send.start()
                recv = pltpu.make_async_remote_copy(src, outs[i].at[peer], send_sems.at[i, d - 1], recv_sems.at[i, d - 1],
                                                    device_id=(px, py, pc), device_id_type=MESH)
                pending.append((send, recv))
        for send, recv in pending:
            if recv is None:
                send.wait()
            else:
                send.wait_send()
                recv.wait_recv()

    hbm = pl.BlockSpec(memory_space=pltpu.HBM)
    return pl.pallas_call(
        body,
        in_specs=[hbm] * n,
        out_specs=[hbm] * n,
        out_shape=out_shape,
        scratch_shapes=[pltpu.SemaphoreType.DMA((n, N_DEV - 1)), pltpu.SemaphoreType.DMA((n, N_DEV - 1)),
                        pltpu.SemaphoreType.DMA((n,))],
        compiler_params=pltpu.CompilerParams(has_side_effects=True),
        name=name,
    )(*arrays)


def _adamw(parts, w, m, v, name):
    rows, cols = w.shape
    tr = _pick(rows, (256, 128, 64, 8))
    c1 = 1.0 / (1.0 - ADAM_B1 ** ADAM_STEP)
    c2 = 1.0 / (1.0 - ADAM_B2 ** ADAM_STEP)

    def body(p_ref, w_ref, m_ref, v_ref, g_ref, d_ref, mo_ref, vo_ref):
        g = p_ref[0]
        for s in range(1, N_DEV):
            g = g + p_ref[s]
        mn = ADAM_B1 * m_ref[...] + (1.0 - ADAM_B1) * g
        vn = ADAM_B2 * v_ref[...] + (1.0 - ADAM_B2) * (g * g)
        g_ref[...] = g
        mo_ref[...] = mn
        vo_ref[...] = vn
        d_ref[...] = -ADAM_LR * ((mn * c1) / (jnp.sqrt(vn * c2) + ADAM_EPS) + ADAM_WD * w_ref[...])

    spec = pl.BlockSpec((tr, cols), lambda i: (i, 0))
    return pl.pallas_call(
        body,
        grid=(rows // tr,),
        in_specs=[pl.BlockSpec((N_DEV, tr, cols), lambda i: (0, i, 0)), spec, spec, spec],
        out_specs=[spec] * 4,
        out_shape=[jax.ShapeDtypeStruct((rows, cols), F32)] * 4,
        compiler_params=_cparams(("parallel",), 48),
        name=name,
    )(parts, w, m, v)


def _shift_down(a):
    return jnp.concatenate([jnp.zeros_like(a[:, :1]), a[:, :-1]], axis=1)


def _shift_up(a):
    return jnp.concatenate([a[:, 1:], jnp.zeros_like(a[:, :1])], axis=1)


_VECTOR_PARAMS = ("attn_norm_g", "attn_sinks", "rwkv_mu", "w0", "a0", "k_k", "k_a", "r_k", "ln_x_w", "ln_x_b",
                  "mlp_norm_g", "final_norm_g")
_WEIGHT_NAMES = ("attn_norm_g", "w_in", "attn_sinks", "rwkv_mu", "w0", "w2", "a0", "a2", "g2", "k_k", "k_a", "r_k",
                 "ln_x_w", "ln_x_b", "w_out", "mlp_norm_g", "w_up", "w_down", "final_norm_g")


def _pack_vectors(vals):
    pieces = []
    for name in _VECTOR_PARAMS:
        flat = vals[name].reshape(1, -1)
        pad = (-flat.shape[1]) % LANES
        pieces.append(jnp.pad(flat, ((0, 0), (0, pad))) if pad else flat)
    return jnp.concatenate(pieces, axis=1)


def _unpack_vectors(packed, like):
    out, col = {}, 0
    for name in _VECTOR_PARAMS:
        size = like[name].size
        out[name] = packed[0, col:col + size].reshape(like[name].shape)
        col += size + (-size) % LANES
    return out


def kernel(x, attn_norm_g, w_in, attn_sinks, rwkv_mu, w0, w2, a0, a2, g2, k_k, k_a, r_k, ln_x_w, ln_x_b, w_out, mlp_norm_g, w_up, w_down, final_norm_g, loss_target, m_attn_norm_g, m_w_in, m_attn_sinks, m_rwkv_mu, m_w0, m_w2, m_a0, m_a2, m_g2, m_k_k, m_k_a, m_r_k, m_ln_x_w, m_ln_x_b, m_w_out, m_mlp_norm_g, m_w_up, m_w_down, m_final_norm_g, v_attn_norm_g, v_w_in, v_attn_sinks, v_rwkv_mu, v_w0, v_w2, v_a0, v_a2, v_g2, v_k_k, v_k_a, v_r_k, v_ln_x_w, v_ln_x_b, v_w_out, v_mlp_norm_g, v_w_up, v_w_down, v_final_norm_g):
    weights = dict(attn_norm_g=attn_norm_g, w_in=w_in, attn_sinks=attn_sinks, rwkv_mu=rwkv_mu, w0=w0, w2=w2, a0=a0,
                   a2=a2, g2=g2, k_k=k_k, k_a=k_a, r_k=r_k, ln_x_w=ln_x_w, ln_x_b=ln_x_b, w_out=w_out,
                   mlp_norm_g=mlp_norm_g, w_up=w_up, w_down=w_down, final_norm_g=final_norm_g)
    mom1 = dict(attn_norm_g=m_attn_norm_g, w_in=m_w_in, attn_sinks=m_attn_sinks, rwkv_mu=m_rwkv_mu, w0=m_w0, w2=m_w2,
                a0=m_a0, a2=m_a2, g2=m_g2, k_k=m_k_k, k_a=m_k_a, r_k=m_r_k, ln_x_w=m_ln_x_w, ln_x_b=m_ln_x_b,
                w_out=m_w_out, mlp_norm_g=m_mlp_norm_g, w_up=m_w_up, w_down=m_w_down, final_norm_g=m_final_norm_g)
    mom2 = dict(attn_norm_g=v_attn_norm_g, w_in=v_w_in, attn_sinks=v_attn_sinks, rwkv_mu=v_rwkv_mu, w0=v_w0, w2=v_w2,
                a0=v_a0, a2=v_a2, g2=v_g2, k_k=v_k_k, k_a=v_k_a, r_k=v_r_k, ln_x_w=v_ln_x_w, ln_x_b=v_ln_x_b,
                w_out=v_w_out, mlp_norm_g=v_mlp_norm_g, w_up=v_w_up, w_down=v_w_down, final_norm_g=v_final_norm_g)
    bsz, seq, d_model = x.shape
    rows = bsz * seq
    d_in = N_DEV * w_in.shape[2]
    d_ff = N_DEV * w_up.shape[2]

    gathered = _exchange([w_in[0].astype(BF16), w2[0], a2[0], g2[0], w_out[0].astype(BF16), w_up[0].astype(BF16),
                          w_down[0].astype(BF16)], scatter=False, name="gather_weights")
    cols_first = lambda a: a.transpose(1, 0, 2).reshape(a.shape[1], -1)
    w_in_f = cols_first(gathered[0])
    w_attn, w_rw = w_in_f[:, :QKV_WIDTH], w_in_f[:, QKV_WIDTH:]
    w2_f, a2_f, g2_f = cols_first(gathered[1]), cols_first(gathered[2]), cols_first(gathered[3])
    lora = w2_f.shape[0]
    w2p = jnp.concatenate([w2_f, jnp.zeros_like(a2_f)], axis=0)
    a2p = jnp.concatenate([jnp.zeros_like(w2_f), a2_f], axis=0)
    w_out_f = gathered[4].reshape(-1, d_model)
    w_up_f = cols_first(gathered[5])
    w_down_f = gathered[6].reshape(-1, d_model)

    esum = _head_sum_matrix()
    sink_rows = jnp.broadcast_to(attn_sinks.reshape(N_ATTN_HEADS, 1), (N_ATTN_HEADS, LANES))
    prep_pars = [w0, w2p, a0, a2p, g2_f, k_k, k_a, esum]
    post_pars = [ln_x_w, ln_x_b, r_k, esum]

    x2d = x.reshape(rows, d_model)
    h1 = _norm_fwd(x2d, attn_norm_g, "attn_norm_fwd")
    z_attn = _matmul(h1, w_attn, "nn", name="in_proj_attn")
    z_rw = _matmul(h1, w_rw, "nn", name="in_proj_rwkv")
    z_attn3 = z_attn.reshape(bsz, seq, QKV_WIDTH)
    attn_out = _attn_fwd(z_attn3, sink_rows)
    z_prev = _shift_down(z_rw.reshape(bsz, seq, -1)).reshape(rows, -1)
    r, lw, k, v, kk, al, gate = _prep_fwd(z_rw, z_prev, rwkv_mu, prep_pars)
    as3 = lambda a: a.reshape(bsz, seq, RWKV_WIDTH)
    y, s_all = _rwkv_fwd(as3(r), as3(lw), as3(k), as3(v), as3(kk), as3(al))
    y2 = y.reshape(rows, RWKV_WIDTH)
    rw_out = _post_fwd(y2, r, k, v, gate, post_pars)
    mix = jnp.concatenate([attn_out.reshape(rows, ATTN_WIDTH), rw_out], axis=1)
    residual = lambda acc, res: (acc + res,)
    x1 = _matmul(mix, w_out_f, "nn", name="out_proj", extras=(x2d,), epilogue=residual)
    h2 = _norm_fwd(x1, mlp_norm_g, "mlp_norm_fwd")

    def relu_sq(acc):
        pos = jnp.maximum(acc, 0.0)
        return acc, pos * pos

    u, act = _matmul(h2, w_up_f, "nn", name="mlp_up", epilogue=relu_sq, out_dtypes=(BF16, BF16))
    x2 = _matmul(act, w_down_f, "nn", name="mlp_down", extras=(x1,), epilogue=residual)
    dx2, loss_vec, g_final = _final_loss(x2, loss_target.reshape(rows, d_model), final_norm_g.reshape(1, d_model))

    g_w_down = _matmul(act, dx2, "tn", name="grad_w_down")
    du = _matmul(dx2, w_down_f, "nt", name="mlp_down_bwd", extras=(u,), out_dtypes=(BF16,),
                 epilogue=lambda acc, uv: (acc * (2.0 * jnp.maximum(uv.astype(F32), 0.0)),))
    g_w_up = _matmul(h2, du, "tn", name="grad_w_up")
    dh2 = _matmul(du, w_up_f, "nt", name="mlp_up_bwd")
    dx1, g_mlp_norm = _norm_bwd(x1, dh2, dx2, mlp_norm_g, "mlp_norm_bwd")
    g_w_out = _matmul(mix, dx1, "tn", name="grad_w_out")
    dmix = _matmul(dx1, w_out_f, "nt", name="out_proj_bwd")
    dy, dr_a, dk_a, dv_a, dgate, g_ln_w, g_ln_b, g_r_k = _post_bwd(
        y2, r, k, v, gate, (dmix, RWKV_WIDTH, ATTN_WIDTH // RWKV_WIDTH), post_pars)
    dr_b, dlw, dk_b, dv_b, dkk, dal = _rwkv_bwd(as3(r), as3(lw), as3(k), as3(v), as3(kk), as3(al), s_all, as3(dy))
    flat = lambda a: a.reshape(rows, RWKV_WIDTH)
    (dzc, dzp, gmu_r, gmu_k, gmu_v, gmu_wa, gmu_g, g_w0, g_w2p, g_a0, g_a2p, g_g2, g_k_k, g_k_a) = _prep_bwd(
        z_rw, z_prev, [dr_a, flat(dr_b), flat(dlw), dk_a, flat(dk_b), dv_a, flat(dv_b), flat(dkk), flat(dal), dgate],
        rwkv_mu, prep_pars)
    dz_rw = dzc + _shift_up(dzp.reshape(bsz, seq, -1)).reshape(rows, -1)
    dz_attn, g_sink_rows = _attn_bwd(z_attn3, dmix.reshape(bsz, seq, d_model), sink_rows)
    dz = jnp.concatenate([dz_attn.reshape(rows, QKV_WIDTH), dz_rw], axis=1).astype(BF16)
    g_w_in = _matmul(h1, dz, "tn", name="grad_w_in")
    dh1 = _matmul(dz, w_in_f, "nt", name="in_proj_bwd")
    dx, g_attn_norm = _norm_bwd(x2d, dh1, dx1, attn_norm_g, "attn_norm_bwd")

    by_cols = lambda a: a.reshape(a.shape[0], N_DEV, -1).transpose(1, 0, 2)
    lora_grads = jnp.concatenate([g_w2p[:lora], g_a2p[lora:], g_g2], axis=0)
    parts = _exchange([by_cols(g_w_in), by_cols(lora_grads), g_w_out.reshape(N_DEV, -1, d_model), by_cols(g_w_up),
                       g_w_down.reshape(N_DEV, -1, d_model)], scatter=True, name="scatter_grads")
    vec_grads = dict(attn_norm_g=g_attn_norm, attn_sinks=g_sink_rows[:, 0], rwkv_mu=jnp.concatenate(
        [gmu_r, gmu_k, gmu_v, gmu_wa, gmu_g], axis=1), w0=g_w0, a0=g_a0, k_k=g_k_k, k_a=g_k_a, r_k=g_r_k,
        ln_x_w=g_ln_w, ln_x_b=g_ln_b, mlp_norm_g=g_mlp_norm, final_norm_g=g_final)
    vec_parts = _exchange([_pack_vectors(vec_grads)], scatter=False, name="gather_vector_grads")[0]

    grads, delta, new_m, new_v = {}, {}, {}, {}

    def update(name, part, shape2d):
        res = _adamw(part, weights[name].reshape(shape2d), mom1[name].reshape(shape2d), mom2[name].reshape(shape2d),
                     "adamw_" + name)
        for store, val in zip((grads, delta, new_m, new_v), res):
            store[name] = val.reshape(weights[name].shape)

    update("w_in", parts[0], w_in.shape[1:])
    update("w_out", parts[2], w_out.shape[1:])
    update("w_up", parts[3], w_up.shape[1:])
    update("w_down", parts[4], w_down.shape[1:])
    stack = lambda d: jnp.concatenate([d["w2"][0], d["a2"][0], d["g2"][0]], axis=0)
    lora_res = _adamw(parts[1], stack(weights), stack(mom1), stack(mom2), "adamw_lora")
    for store, val in zip((grads, delta, new_m, new_v), lora_res):
        store["w2"], store["a2"], store["g2"] = val[None, :lora], val[None, lora:2 * lora], val[None, 2 * lora:]
    vec_res = _adamw(vec_parts, _pack_vectors(weights), _pack_vectors(mom1), _pack_vectors(mom2), "adamw_vectors")
    for store, val in zip((grads, delta, new_m, new_v), vec_res):
        store.update(_unpack_vectors(val, weights))

    loss = lax.psum(loss_vec[0, 0], ("x", "y", "c"))
    return (loss, dx.reshape(x.shape), *[grads[n] for n in _WEIGHT_NAMES], *[delta[n] for n in _WEIGHT_NAMES],
            *[new_m[n] for n in _WEIGHT_NAMES], *[new_v[n] for n in _WEIGHT_NAMES])
```

```python
import functools
import math

import jax
import jax.numpy as jnp
from jax import lax
from jax.experimental import pallas as pl
from jax.experimental.pallas import tpu as pltpu

F32 = jnp.float32
BF16 = jnp.bfloat16
HI = lax.Precision.HIGHEST

N_DEV = 8
HEAD_DIM = 64
LANES = 128
N_ATTN_HEADS = 8
ATTN_WIDTH = 512
KV_WIDTH = 128
QKV_WIDTH = ATTN_WIDTH + 2 * KV_WIDTH
RWKV_WIDTH = 512
LORA_WA = 128
GATE_LORA = 128
RWKV_SHIFT_WIDTH = 3 * RWKV_WIDTH + LORA_WA + GATE_LORA
BLOCK = 128
CHUNK = 64
RMS_EPS = 1e-6
GN_EPS = 64e-5
L2_EPS = 1e-12
NEG_INF = -1e30
DECAY_SCALE = math.exp(-0.5)
ADAM_LR, ADAM_B1, ADAM_B2, ADAM_EPS, ADAM_WD, ADAM_STEP = 0.001, 0.9, 0.999, 1e-08, 0.01, 10

NN = (((1,), (0,)), ((), ()))
NT = (((1,), (1,)), ((), ()))
TN = (((0,), (0,)), ((), ()))
MESH = pl.DeviceIdType.MESH


def _dot(a, b, dn=NN, precision=None):
    return lax.dot_general(a, b, dn, precision=precision, preferred_element_type=F32)


def _bdot_raw(a, b, dn):
    return lax.dot_general(a.astype(BF16), b.astype(BF16), dn, preferred_element_type=F32)


@functools.partial(jax.custom_vjp, nondiff_argnums=(2, 3))
def _bdot_c(a, b, ca, cb):
    return _bdot_raw(a, b, (((ca,), (cb,)), ((), ())))


def _bdot_c_fwd(a, b, ca, cb):
    return _bdot_c(a, b, ca, cb), (a, b)


def _bdot_c_bwd(ca, cb, res, ct):
    a, b = res
    fa, fb = 1 - ca, 1 - cb
    da = _bdot_raw(ct, b, (((1,), (fb,)), ((), ()))) if ca == 1 else _bdot_raw(b, ct, (((fb,), (1,)), ((), ())))
    db = _bdot_raw(a, ct, (((fa,), (0,)), ((), ()))) if cb == 0 else _bdot_raw(ct, a, (((0,), (fa,)), ((), ())))
    return da, db


_bdot_c.defvjp(_bdot_c_fwd, _bdot_c_bwd)


def _bdot(a, b, dn=NN):
    return _bdot_c(a, b, dn[0][0][0], dn[0][1][0])


def _bdot_nn(a, b):
    return _bdot(a, b, NN)


def _bdot_nt(a, b):
    return _bdot(a, b, NT)


def _split3(x):
    hi = x.astype(BF16)
    rest = x - hi.astype(F32)
    mid = rest.astype(BF16)
    return hi, mid, (rest - mid.astype(F32)).astype(BF16)


def _running_sum(x, dn):
    c = x.shape[0]
    row = lax.broadcasted_iota(jnp.int32, (c, c), 0)
    col = lax.broadcasted_iota(jnp.int32, (c, c), 1)
    tri = jnp.where(row >= col, 1.0, 0.0).astype(BF16)
    parts = [lax.dot_general(tri, p, dn, preferred_element_type=F32) for p in _split3(x)]
    return parts[0] + parts[1] + parts[2]


@jax.custom_vjp
def _cumsum_rows(x):
    return _running_sum(x, NN)


_cumsum_rows.defvjp(lambda x: (_running_sum(x, NN), None), lambda _, ct: (_running_sum(ct, TN),))


@jax.custom_vjp
def _fold_rows(x):
    c = x.shape[0] // 2
    return x[:c] + x[c:]


_fold_rows.defvjp(lambda x: (_fold_rows(x), None), lambda _, ct: (jnp.concatenate([ct, ct], axis=0),))


def _sigmoid(x):
    return 1.0 / (1.0 + jnp.exp(-x))


def _pick(n, cands):
    for c in cands:
        if n % c == 0:
            return c
    return n


def _cparams(sem, vmem_mb=None):
    kw = dict(dimension_semantics=sem)
    if vmem_mb is not None:
        kw["vmem_limit_bytes"] = vmem_mb * 1024 * 1024
    return pltpu.CompilerParams(**kw)


def _matmul(a, b, mode, *, name, extras=(), epilogue=None, out_dtypes=(F32,), tm=1024, tn=1024, tk=1024):
    if mode == "nn":
        (M, K), (_, N) = a.shape, b.shape
    elif mode == "tn":
        (K, M), (_, N) = a.shape, b.shape
    else:
        (M, K), (N, _) = a.shape, b.shape
    tm = _pick(M, (tm, 512, 256, 128))
    tn = _pick(N, (tn, 896, 768, 512, 384, 256, 128))
    tk = _pick(K, (tk, 512, 256, 128))
    nk = K // tk
    ne, nout = len(extras), len(out_dtypes)
    if mode == "nn":
        a_spec = pl.BlockSpec((tm, tk), lambda i, j, k: (i, k))
        b_spec = pl.BlockSpec((tk, tn), lambda i, j, k: (k, j))
        dn = NN
    elif mode == "tn":
        a_spec = pl.BlockSpec((tk, tm), lambda i, j, k: (k, i))
        b_spec = pl.BlockSpec((tk, tn), lambda i, j, k: (k, j))
        dn = TN
    else:
        a_spec = pl.BlockSpec((tm, tk), lambda i, j, k: (i, k))
        b_spec = pl.BlockSpec((tn, tk), lambda i, j, k: (j, k))
        dn = NT
    o_spec = pl.BlockSpec((tm, tn), lambda i, j, k: (i, j))

    def body(*refs):
        a_ref, b_ref = refs[:2]
        e_refs = refs[2:2 + ne]
        o_refs = refs[2 + ne:2 + ne + nout]
        acc = refs[-1]
        kstep = pl.program_id(2)

        @pl.when(kstep == 0)
        def _():
            acc[...] = jnp.zeros_like(acc)

        acc[...] += _bdot_raw(a_ref[...], b_ref[...], dn)

        @pl.when(kstep == nk - 1)
        def _():
            if epilogue is None:
                outs = (acc[...],)
            else:
                outs = epilogue(acc[...], *[e[...] for e in e_refs])
            for o_ref, o in zip(o_refs, outs):
                o_ref[...] = o.astype(o_ref.dtype)

    outs = pl.pallas_call(
        body,
        grid=(M // tm, N // tn, nk),
        in_specs=[a_spec, b_spec] + [o_spec] * ne,
        out_specs=[o_spec] * nout,
        out_shape=[jax.ShapeDtypeStruct((M, N), dt) for dt in out_dtypes],
        scratch_shapes=[pltpu.VMEM((tm, tn), F32)],
        compiler_params=_cparams(("parallel", "parallel", "arbitrary"), 56),
        name=name,
    )(a, b, *extras)
    return outs[0] if nout == 1 else outs


def _rowwise(fn, rows, pars, out_rows, out_accs, *, tile, name):
    rows = [r if isinstance(r, tuple) else (r, r.shape[1], 0) for r in rows]
    R = rows[0][0].shape[0]
    tile = min(tile, R)
    nr, npar, nor, noa = len(rows), len(pars), len(out_rows), len(out_accs)

    def body(*refs):
        rin = refs[:nr]
        pin = refs[nr:nr + npar]
        orow = refs[nr + npar:nr + npar + nor]
        oacc = refs[nr + npar + nor:]
        outs = fn(*[r[...] for r in rin], *[p[...] for p in pin])
        for ref, o in zip(orow, outs[:nor]):
            if isinstance(o, (tuple, list)):
                col = 0
                for piece in o:
                    ref[:, col:col + piece.shape[1]] = piece.astype(ref.dtype)
                    col += piece.shape[1]
            else:
                ref[...] = o.astype(ref.dtype)
        step = pl.program_id(0)

        def accumulate(ref, o):
            @pl.when(step == 0)
            def _():
                ref[...] = o

            @pl.when(step > 0)
            def _():
                ref[...] += o

        for ref, o in zip(oacc, outs[nor:]):
            accumulate(ref, o)

    def colspec(width, cb):
        return pl.BlockSpec((tile, width), lambda i: (i, cb))

    return pl.pallas_call(
        body,
        grid=(R // tile,),
        in_specs=[colspec(w, cb) for (_, w, cb) in rows]
        + [pl.BlockSpec(p.shape, lambda i: (0, 0)) for p in pars],
        out_specs=[colspec(w, 0) for (w, _) in out_rows]
        + [pl.BlockSpec(s, lambda i: (0, 0)) for s in out_accs],
        out_shape=[jax.ShapeDtypeStruct((R, w), dt) for (w, dt) in out_rows]
        + [jax.ShapeDtypeStruct(s, F32) for s in out_accs],
        compiler_params=_cparams(("arbitrary",), 56),
        name=name,
    )(*[r[0] for r in rows], *pars)


def _rms_fn(x, g):
    return x * lax.rsqrt(jnp.mean(x * x, axis=-1, keepdims=True) + RMS_EPS) * g


def _norm_fwd(x, g, name):
    return _rowwise(lambda xv, gv: (_rms_fn(xv, gv),), [x], [g], [(x.shape[1], BF16)], [], tile=512, name=name)[0]


def _norm_bwd(x, dh, dres, g, name):
    def fn(xv, dhv, dresv, gv):
        _, vjp = jax.vjp(_rms_fn, xv, gv)
        dx, dg = vjp(dhv)
        return dx + dresv, dg

    return _rowwise(fn, [x, dh, dres], [g], [(x.shape[1], F32)], [g.shape], tile=256, name=name)


def _final_loss(x, tgt, g):
    d = x.shape[1]

    def fn(xv, tv, gv):
        y, vjp = jax.vjp(_rms_fn, xv, gv)
        err = y - tv
        loss = 0.5 * jnp.sum(jnp.sum(err * err, axis=-1, keepdims=True), axis=0, keepdims=True) / d
        dx, dg = vjp(err / d)
        return dx, jnp.broadcast_to(loss, (1, LANES)), dg

    return _rowwise(fn, [x, tgt], [g], [(d, F32)], [(1, LANES), g.shape], tile=256, name="final_norm_loss")


def _head_sum_matrix():
    i = lax.broadcasted_iota(jnp.int32, (RWKV_WIDTH, RWKV_WIDTH), 0) // HEAD_DIM
    j = lax.broadcasted_iota(jnp.int32, (RWKV_WIDTH, RWKV_WIDTH), 1) // HEAD_DIM
    return (i == j).astype(F32)


def _prep_core(xr, xk, xv, xwa, xg, w0, w2p, a0, a2p, g2, k_k, k_a, esum):
    lw = -DECAY_SCALE * _sigmoid(w0 + _bdot_nn(jnp.tanh(xwa), w2p))
    a = _sigmoid(a0 + _bdot_nn(xwa, a2p))
    g = _bdot_nn(_sigmoid(xg), g2)
    kk0 = xk * k_k
    kk = kk0 / jnp.maximum(jnp.sqrt(_dot(kk0 * kk0, esum, precision=HI)), L2_EPS)
    k = xk * (1.0 + (a - 1.0) * k_a)
    return xr, lw, k, xv, kk, a, g


_SEGS = ((0, 512), (512, 1024), (1024, 1536), (1536, 1664), (1664, 1792))


def _prep_fwd(zc, zp, mu, pars):
    def fn(zcv, zpv, muv, *pv):
        zs = zcv + (zpv - zcv) * muv
        return _prep_core(*[zs[:, a:b] for a, b in _SEGS], *pv)

    return _rowwise(fn, [zc, zp], [mu, *pars], [(RWKV_WIDTH, F32)] * 7, [], tile=256, name="rwkv_prep_fwd")


def _prep_bwd(zc, zp, cts, mu, pars):
    npar = len(pars)

    def fn(zcv, zpv, dra, drb, dlw, dka, dkb, dva, dvb, dkk, da, dg, muv, *pv):
        diff = zpv - zcv
        zs = zcv + diff * muv
        segs = [zs[:, a:b] for a, b in _SEGS]
        esum = pv[-1]
        _, vjp = jax.vjp(lambda *args: _prep_core(*args, esum), *segs, *pv[:-1])
        grads = vjp((dra + drb, dlw, dka + dkb, dva + dvb, dkk, da, dg))
        dsegs, dpars = grads[:5], grads[5:]
        dzc = [ds * (1.0 - muv[:, a:b]) for ds, (a, b) in zip(dsegs, _SEGS)]
        dzp = [ds * muv[:, a:b] for ds, (a, b) in zip(dsegs, _SEGS)]
        dmu = [jnp.sum(ds * diff[:, a:b], axis=0, keepdims=True) for ds, (a, b) in zip(dsegs, _SEGS)]
        return (dzc, dzp, *dmu, *dpars)

    seg_shapes = [(1, b - a) for a, b in _SEGS]
    return _rowwise(fn, [zc, zp, *cts], [mu, *pars],
                    [(RWKV_SHIFT_WIDTH, F32)] * 2, seg_shapes + [p.shape for p in pars[:-1]],
                    tile=256, name="rwkv_prep_bwd")


def _post_fn(y, r, k, v, g, ln_w, ln_b, r_k, esum):
    mean = _dot(y, esum, precision=HI) * (1.0 / HEAD_DIM)
    yc = y - mean
    var = _dot(yc * yc, esum, precision=HI) * (1.0 / HEAD_DIM)
    yn = yc * lax.rsqrt(var + GN_EPS) * ln_w + ln_b
    bonus = _dot(r * k * r_k, esum, precision=HI) * v
    return (yn + bonus) * g


def _post_fwd(y, r, k, v, g, pars):
    return _rowwise(lambda *a: (_post_fn(*a),), [y, r, k, v, g], pars, [(RWKV_WIDTH, BF16)], [],
                    tile=256, name="rwkv_post_fwd")[0]


def _post_bwd(y, r, k, v, g, dout, pars):
    def fn(yv, rv, kv, vv, gv, dv_, ln_w, ln_b, r_k, esum):
        _, vjp = jax.vjp(lambda *a: _post_fn(*a, esum), yv, rv, kv, vv, gv, ln_w, ln_b, r_k)
        return vjp(dv_)

    return _rowwise(fn, [y, r, k, v, g, dout], pars, [(RWKV_WIDTH, F32)] * 5, [p.shape for p in pars[:-1]],
                    tile=256, name="rwkv_post_bwd")


def _chunk_fn(s, r, lw, k, v, kk, al):
    c = r.shape[0]
    n = 2 * c
    row = lax.broadcasted_iota(jnp.int32, (n, n), 0)
    col = lax.broadcasted_iota(jnp.int32, (n, n), 1)
    incl = (row % c) >= (col % c)
    strict = (row % c) > (col % c)
    lane = lax.broadcasted_iota(jnp.int32, (1, LANES), 1)
    m_lo = jnp.where(lane < HEAD_DIM, 1.0, 0.0)
    m_hi = 1.0 - m_lo

    def stack(a):
        return jnp.concatenate([a * m_lo, a * m_hi], axis=0)

    cum = _cumsum_rows(lw)
    total = jnp.sum(lw, axis=0, keepdims=True)
    b = kk * al
    grow = jnp.exp(-cum)
    tail = jnp.exp(total - cum)
    a_s, b_s, k_s, r_s, v_s = (stack(-kk * jnp.exp(cum - lw)), stack(b * grow), stack(k * grow),
                               stack(r * jnp.exp(cum)), stack(v))
    m_ab = jnp.where(strict, _bdot(a_s, b_s, NT), 0.0)
    m_ak = jnp.where(strict, _bdot(a_s, k_s, NT), 0.0)
    m_rb = jnp.where(incl, _bdot(r_s, b_s, NT), 0.0)
    m_rk = jnp.where(incl, _bdot(r_s, k_s, NT), 0.0)
    t_inv = jnp.where(row == col, 1.0, 0.0) + m_ab
    power = m_ab
    for _ in range(int(math.log2(c)) - 1):
        power = _bdot(power, power)
        t_inv = t_inv + _bdot(t_inv, power)
    u = _bdot(t_inv, _bdot(a_s, s, NT) + _bdot(m_ak, v_s))
    y = _fold_rows(_bdot(r_s, s, NT) + _bdot(m_rb, u) + _bdot(m_rk, v_s))
    s_new = s * jnp.exp(total) + _bdot(u, stack(b * tail), TN) + _bdot(v_s, stack(k * tail), TN)
    return y, s_new


def _rwkv_fwd(r, lw, k, v, kk, al):
    bsz, t, w = r.shape
    npair, nchunk = w // LANES, t // CHUNK

    def body(r_ref, lw_ref, k_ref, v_ref, kk_ref, al_ref, y_ref, sall_ref, s_scr):
        @pl.when(pl.program_id(1) == 0)
        def _():
            s_scr[...] = jnp.zeros_like(s_scr)

        for p in range(npair):
            cols = slice(p * LANES, (p + 1) * LANES)
            s = s_scr[p]
            sall_ref[0, 0, p] = s
            y, s_new = _chunk_fn(s, r_ref[0, :, cols], lw_ref[0, :, cols], k_ref[0, :, cols], v_ref[0, :, cols],
                                 kk_ref[0, :, cols], al_ref[0, :, cols])
            y_ref[0, :, cols] = y
            s_scr[p] = s_new

    spec = pl.BlockSpec((1, CHUNK, w), lambda b, c: (b, c, 0))
    return pl.pallas_call(
        body,
        grid=(bsz, nchunk),
        in_specs=[spec] * 6,
        out_specs=[spec, pl.BlockSpec((1, 1, npair, LANES, LANES), lambda b, c: (b, c, 0, 0, 0))],
        out_shape=[jax.ShapeDtypeStruct((bsz, t, w), F32),
                   jax.ShapeDtypeStruct((bsz, nchunk, npair, LANES, LANES), F32)],
        scratch_shapes=[pltpu.VMEM((npair, LANES, LANES), F32)],
        compiler_params=_cparams(("parallel", "arbitrary"), 48),
        name="rwkv_chunk_fwd",
    )(r, lw, k, v, kk, al)


def _rwkv_bwd(r, lw, k, v, kk, al, s_all, dy):
    bsz, t, w = r.shape
    npair, nchunk = w // LANES, t // CHUNK

    def body(r_ref, lw_ref, k_ref, v_ref, kk_ref, al_ref, s_ref, dy_ref, *rest):
        out_refs, ds_scr = rest[:6], rest[6]

        @pl.when(pl.program_id(1) == 0)
        def _():
            ds_scr[...] = jnp.zeros_like(ds_scr)

        for p in range(npair):
            cols = slice(p * LANES, (p + 1) * LANES)
            _, vjp = jax.vjp(_chunk_fn, s_ref[0, 0, p], r_ref[0, :, cols], lw_ref[0, :, cols], k_ref[0, :, cols],
                             v_ref[0, :, cols], kk_ref[0, :, cols], al_ref[0, :, cols])
            grads = vjp((dy_ref[0, :, cols], ds_scr[p]))
            ds_scr[p] = grads[0]
            for ref, gval in zip(out_refs, grads[1:]):
                ref[0, :, cols] = gval

    spec = pl.BlockSpec((1, CHUNK, w), lambda b, c: (b, nchunk - 1 - c, 0))
    sspec = pl.BlockSpec((1, 1, npair, LANES, LANES), lambda b, c: (b, nchunk - 1 - c, 0, 0, 0))
    return pl.pallas_call(
        body,
        grid=(bsz, nchunk),
        in_specs=[spec] * 6 + [sspec, spec],
        out_specs=[spec] * 6,
        out_shape=[jax.ShapeDtypeStruct((bsz, t, w), F32)] * 6,
        scratch_shapes=[pltpu.VMEM((npair, LANES, LANES), F32)],
        compiler_params=_cparams(("parallel", "arbitrary"), 48),
        name="rwkv_chunk_bwd",
    )(r, lw, k, v, kk, al, s_all, dy)


def _alibi_slope(head):
    return 2.0 ** (-8.0 * (head + 1) / N_ATTN_HEADS)


def _attn_pair(q, kp, kc, vp, vc, sinks, first, pair):
    kvh = pair // 2
    row = lax.broadcasted_iota(jnp.int32, (BLOCK, BLOCK), 0)
    col = lax.broadcasted_iota(jnp.int32, (BLOCK, BLOCK), 1)
    lane = lax.broadcasted_iota(jnp.int32, (1, LANES), 1)
    swap = jnp.where((row + HEAD_DIM) % LANES == col, 1.0, 0.0)
    dist_c = (row - col).astype(F32)
    dist_p = dist_c + float(BLOCK)
    valid_c = row >= col
    valid_p = jnp.logical_and(col > row, jnp.logical_not(first))
    out = jnp.zeros_like(q)
    psinks = []
    for half in range(2):
        head = 2 * pair + half
        m = jnp.where((lane // HEAD_DIM) == half, 1.0, 0.0)
        if half == kvh:
            kpx, kcx, vpx, vcx = kp, kc, vp, vc
        else:
            kpx, kcx, vpx, vcx = (_bdot_nn(t, swap) for t in (kp, kc, vp, vc))
        qa = q * m
        slope = _alibi_slope(head)
        sp = jnp.where(valid_p, _bdot_nt(qa, kpx) * (HEAD_DIM ** -0.5) - slope * dist_p, NEG_INF)
        sc = jnp.where(valid_c, _bdot_nt(qa, kcx) * (HEAD_DIM ** -0.5) - slope * dist_c, NEG_INF)
        sink = sinks[half]
        mx = jnp.maximum(jnp.maximum(jnp.max(sp, axis=-1, keepdims=True), jnp.max(sc, axis=-1, keepdims=True)), sink)
        mx = lax.stop_gradient(mx)
        ep = jnp.exp(sp - mx)
        ec = jnp.exp(sc - mx)
        es = jnp.exp(sink - mx)
        inv = 1.0 / (jnp.sum(ep, axis=-1, keepdims=True) + jnp.sum(ec, axis=-1, keepdims=True) + es)
        o = _bdot_nn(ep * inv, vpx) + _bdot_nn(ec * inv, vcx)
        out = out + o * m
        psinks.append(lax.stop_gradient(es * inv))
    return out, psinks


def _sink_values(sink_ref, pair):
    return [jnp.max(sink_ref[2 * pair + half:2 * pair + half + 1, :], axis=-1, keepdims=True) for half in range(2)]


def _attn_fwd(z, sink_rows):
    bsz, t, _ = z.shape
    nb = t // BLOCK
    npair = ATTN_WIDTH // LANES

    def body(q_ref, kp_ref, kc_ref, vp_ref, vc_ref, sink_ref, o_ref):
        first = pl.program_id(1) == 0
        for pair in range(npair):
            q = q_ref[0, :, pair * LANES:(pair + 1) * LANES]
            o, _ = _attn_pair(q, kp_ref[0], kc_ref[0], vp_ref[0], vc_ref[0], _sink_values(sink_ref, pair), first, pair)
            o_ref[0, :, pair * LANES:(pair + 1) * LANES] = o.astype(o_ref.dtype)

    kcol, vcol = ATTN_WIDTH // KV_WIDTH, ATTN_WIDTH // KV_WIDTH + 1
    return pl.pallas_call(
        body,
        grid=(bsz, nb),
        in_specs=[pl.BlockSpec((1, BLOCK, ATTN_WIDTH), lambda b, n: (b, n, 0)),
                  pl.BlockSpec((1, BLOCK, KV_WIDTH), lambda b, n: (b, jnp.maximum(n - 1, 0), kcol)),
                  pl.BlockSpec((1, BLOCK, KV_WIDTH), lambda b, n: (b, n, kcol)),
                  pl.BlockSpec((1, BLOCK, KV_WIDTH), lambda b, n: (b, jnp.maximum(n - 1, 0), vcol)),
                  pl.BlockSpec((1, BLOCK, KV_WIDTH), lambda b, n: (b, n, vcol)),
                  pl.BlockSpec(sink_rows.shape, lambda b, n: (0, 0))],
        out_specs=pl.BlockSpec((1, BLOCK, ATTN_WIDTH), lambda b, n: (b, n, 0)),
        out_shape=jax.ShapeDtypeStruct((bsz, t, ATTN_WIDTH), BF16),
        compiler_params=_cparams(("parallel", "arbitrary"), 48),
        name="swa_fwd",
    )(z, z, z, z, z, sink_rows)


def _attn_bwd(z, dout, sink_rows):
    bsz, t, _ = z.shape
    nb = t // BLOCK
    npair = ATTN_WIDTH // LANES

    def body(q_ref, kp_ref, kc_ref, vp_ref, vc_ref, do_ref, sink_ref, dz_ref, dsink_ref, carry):
        step = pl.program_id(1)
        n = nb - 1 - step
        first = n == 0

        @pl.when(step == 0)
        def _():
            carry[...] = jnp.zeros_like(carry)

        @pl.when(jnp.logical_and(step == 0, pl.program_id(0) == 0))
        def _():
            dsink_ref[...] = jnp.zeros_like(dsink_ref)

        dkp = jnp.zeros((BLOCK, KV_WIDTH), F32)
        dkc, dvp, dvc = dkp, dkp, dkp
        lane = lax.broadcasted_iota(jnp.int32, (1, LANES), 1)
        for pair in range(npair):
            cols = slice(pair * LANES, (pair + 1) * LANES)
            sinks = _sink_values(sink_ref, pair)
            fn = functools.partial(_attn_pair, sinks=sinks, first=first, pair=pair)
            o, vjp, psinks = jax.vjp(fn, q_ref[0, :, cols], kp_ref[0], kc_ref[0], vp_ref[0], vc_ref[0], has_aux=True)
            do = do_ref[0, :, cols]
            dq, g_kp, g_kc, g_vp, g_vc = vjp(do)
            dz_ref[0, :, cols] = dq
            dkp, dkc, dvp, dvc = dkp + g_kp, dkc + g_kc, dvp + g_vp, dvc + g_vc
            for half in range(2):
                m = jnp.where((lane // HEAD_DIM) == half, 1.0, 0.0)
                delta = jnp.sum(do * o * m, axis=-1, keepdims=True)
                ds = -jnp.sum(psinks[half] * delta, axis=0, keepdims=True)
                head = 2 * pair + half
                dsink_ref[head:head + 1, :] += jnp.broadcast_to(ds, (1, LANES))
        dz_ref[0, :, ATTN_WIDTH:ATTN_WIDTH + KV_WIDTH] = dkc + carry[0]
        dz_ref[0, :, ATTN_WIDTH + KV_WIDTH:QKV_WIDTH] = dvc + carry[1]
        carry[0] = dkp
        carry[1] = dvp

    kcol, vcol = ATTN_WIDTH // KV_WIDTH, ATTN_WIDTH // KV_WIDTH + 1
    rev = lambda n: nb - 1 - n
    return pl.pallas_call(
        body,
        grid=(bsz, nb),
        in_specs=[pl.BlockSpec((1, BLOCK, ATTN_WIDTH), lambda b, n: (b, rev(n), 0)),
                  pl.BlockSpec((1, BLOCK, KV_WIDTH), lambda b, n: (b, jnp.maximum(rev(n) - 1, 0), kcol)),
                  pl.BlockSpec((1, BLOCK, KV_WIDTH), lambda b, n: (b, rev(n), kcol)),
                  pl.BlockSpec((1, BLOCK, KV_WIDTH), lambda b, n: (b, jnp.maximum(rev(n) - 1, 0), vcol)),
                  pl.BlockSpec((1, BLOCK, KV_WIDTH), lambda b, n: (b, rev(n), vcol)),
                  pl.BlockSpec((1, BLOCK, ATTN_WIDTH), lambda b, n: (b, rev(n), 0)),
                  pl.BlockSpec(sink_rows.shape, lambda b, n: (0, 0))],
        out_specs=[pl.BlockSpec((1, BLOCK, QKV_WIDTH), lambda b, n: (b, rev(n), 0)),
                   pl.BlockSpec((N_ATTN_HEADS, LANES), lambda b, n: (0, 0))],
        out_shape=[jax.ShapeDtypeStruct((bsz, t, QKV_WIDTH), F32),
                   jax.ShapeDtypeStruct((N_ATTN_HEADS, LANES), F32)],
        scratch_shapes=[pltpu.VMEM((2, BLOCK, KV_WIDTH), F32)],
        compiler_params=_cparams(("arbitrary", "arbitrary"), 48),
        name="swa_bwd",
    )(z, z, z, z, z, dout, sink_rows)


def _exchange(arrays, *, scatter, name):
    n = len(arrays)
    out_shape = [jax.ShapeDtypeStruct((N_DEV,) + (a.shape[1:] if scatter else a.shape), a.dtype) for a in arrays]

    def body(*refs):
        ins, outs = refs[:n], refs[n:2 * n]
        send_sems, recv_sems, local_sems = refs[2 * n:]
        x, y, c = lax.axis_index("x"), lax.axis_index("y"), lax.axis_index("c")
        me = 4 * x + 2 * y + c
        pending = []
        for i in range(n):
            own = pltpu.make_async_copy(ins[i].at[me] if scatter else ins[i], outs[i].at[me], local_sems.at[i])
            own.start()
            pending.append((own, None))
            for d in range(1, N_DEV):
                px = 1 - x if d & 4 else x
                py = 1 - y if d & 2 else y
                pc = 1 - c if d & 1 else c
                peer = 4 * px + 2 * py + pc
                src = ins[i].at[peer] if scatter else ins[i]
                send = pltpu.make_async_remote_copy(src, outs[i].at[me], send_sems.at[i, d - 1], recv_sems.at[i, d - 1],
                                                    device_id=(px, py, pc), device_id_type=MESH)
                send.start()
                recv = pltpu.make_async_remote_copy(src, outs[i].at[peer], send_sems.at[i, d - 1], recv_sems.at[i, d - 1],
                                                    device_id=(px, py, pc), device_id_type=MESH)
                pending.append((send, recv))
        for send, recv in pending:
            if recv is None:
                send.wait()
            else:
                send.wait_send()
                recv.wait_recv()

    hbm = pl.BlockSpec(memory_space=pltpu.HBM)
    return pl.pallas_call(
        body,
        in_specs=[hbm] * n,
        out_specs=[hbm] * n,
        out_shape=out_shape,
        scratch_shapes=[pltpu.SemaphoreType.DMA((n, N_DEV - 1)), pltpu.SemaphoreType.DMA((n, N_DEV - 1)),
                        pltpu.SemaphoreType.DMA((n,))],
        compiler_params=pltpu.CompilerParams(has_side_effects=True),
        name=name,
    )(*arrays)


def _adamw(parts, w, m, v, name):
    rows, cols = w.shape
    tr = _pick(rows, (256, 128, 64, 8))
    c1 = 1.0 / (1.0 - ADAM_B1 ** ADAM_STEP)
    c2 = 1.0 / (1.0 - ADAM_B2 ** ADAM_STEP)

    def body(p_ref, w_ref, m_ref, v_ref, g_ref, d_ref, mo_ref, vo_ref):
        g = p_ref[0]
        for s in range(1, N_DEV):
            g = g + p_ref[s]
        mn = ADAM_B1 * m_ref[...] + (1.0 - ADAM_B1) * g
        vn = ADAM_B2 * v_ref[...] + (1.0 - ADAM_B2) * (g * g)
        g_ref[...] = g
        mo_ref[...] = mn
        vo_ref[...] = vn
        d_ref[...] = -ADAM_LR * ((mn * c1) / (jnp.sqrt(vn * c2) + ADAM_EPS) + ADAM_WD * w_ref[...])

    spec = pl.BlockSpec((tr, cols), lambda i: (i, 0))
    return pl.pallas_call(
        body,
        grid=(rows // tr,),
        in_specs=[pl.BlockSpec((N_DEV, tr, cols), lambda i: (0, i, 0)), spec, spec, spec],
        out_specs=[spec] * 4,
        out_shape=[jax.ShapeDtypeStruct((rows, cols), F32)] * 4,
        compiler_params=_cparams(("parallel",), 48),
        name=name,
    )(parts, w, m, v)


def _shift_down(a):
    return jnp.concatenate([jnp.zeros_like(a[:, :1]), a[:, :-1]], axis=1)


def _shift_up(a):
    return jnp.concatenate([a[:, 1:], jnp.zeros_like(a[:, :1])], axis=1)


_VECTOR_PARAMS = ("attn_norm_g", "attn_sinks", "rwkv_mu", "w0", "a0", "k_k", "k_a", "r_k", "ln_x_w", "ln_x_b",
                  "mlp_norm_g", "final_norm_g")
_WEIGHT_NAMES = ("attn_norm_g", "w_in", "attn_sinks", "rwkv_mu", "w0", "w2", "a0", "a2", "g2", "k_k", "k_a", "r_k",
                 "ln_x_w", "ln_x_b", "w_out", "mlp_norm_g", "w_up", "w_down", "final_norm_g")


def _pack_vectors(vals):
    pieces = []
    for name in _VECTOR_PARAMS:
        flat = vals[name].reshape(1, -1)
        pad = (-flat.shape[1]) % LANES
        pieces.append(jnp.pad(flat, ((0, 0), (0, pad))) if pad else flat)
    return jnp.concatenate(pieces, axis=1)


def _unpack_vectors(packed, like):
    out, col = {}, 0
    for name in _VECTOR_PARAMS:
        size = like[name].size
        out[name] = packed[0, col:col + size].reshape(like[name].shape)
        col += size + (-size) % LANES
    return out


def kernel(x, attn_norm_g, w_in, attn_sinks, rwkv_mu, w0, w2, a0, a2, g2, k_k, k_a, r_k, ln_x_w, ln_x_b, w_out, mlp_norm_g, w_up, w_down, final_norm_g, loss_target, m_attn_norm_g, m_w_in, m_attn_sinks, m_rwkv_mu, m_w0, m_w2, m_a0, m_a2, m_g2, m_k_k, m_k_a, m_r_k, m_ln_x_w, m_ln_x_b, m_w_out, m_mlp_norm_g, m_w_up, m_w_down, m_final_norm_g, v_attn_norm_g, v_w_in, v_attn_sinks, v_rwkv_mu, v_w0, v_w2, v_a0, v_a2, v_g2, v_k_k, v_k_a, v_r_k, v_ln_x_w, v_ln_x_b, v_w_out, v_mlp_norm_g, v_w_up, v_w_down, v_final_norm_g):
    weights = dict(attn_norm_g=attn_norm_g, w_in=w_in, attn_sinks=attn_sinks, rwkv_mu=rwkv_mu, w0=w0, w2=w2, a0=a0,
                   a2=a2, g2=g2, k_k=k_k, k_a=k_a, r_k=r_k, ln_x_w=ln_x_w, ln_x_b=ln_x_b, w_out=w_out,
                   mlp_norm_g=mlp_norm_g, w_up=w_up, w_down=w_down, final_norm_g=final_norm_g)
    mom1 = dict(attn_norm_g=m_attn_norm_g, w_in=m_w_in, attn_sinks=m_attn_sinks, rwkv_mu=m_rwkv_mu, w0=m_w0, w2=m_w2,
                a0=m_a0, a2=m_a2, g2=m_g2, k_k=m_k_k, k_a=m_k_a, r_k=m_r_k, ln_x_w=m_ln_x_w, ln_x_b=m_ln_x_b,
                w_out=m_w_out, mlp_norm_g=m_mlp_norm_g, w_up=m_w_up, w_down=m_w_down, final_norm_g=m_final_norm_g)
    mom2 = dict(attn_norm_g=v_attn_norm_g, w_in=v_w_in, attn_sinks=v_attn_sinks, rwkv_mu=v_rwkv_mu, w0=v_w0, w2=v_w2,
                a0=v_a0, a2=v_a2, g2=v_g2, k_k=v_k_k, k_a=v_k_a, r_k=v_r_k, ln_x_w=v_ln_x_w, ln_x_b=v_ln_x_b,
                w_out=v_w_out, mlp_norm_g=v_mlp_norm_g, w_up=v_w_up, w_down=v_w_down, final_norm_g=v_final_norm_g)
    bsz, seq, d_model = x.shape
    rows = bsz * seq
    d_in = N_DEV * w_in.shape[2]
    d_ff = N_DEV * w_up.shape[2]

    gathered = _exchange([w_in[0].astype(BF16), w2[0], a2[0], g2[0], w_out[0].astype(BF16), w_up[0].astype(BF16),
                          w_down[0].astype(BF16)], scatter=False, name="gather_weights")
    cols_first = lambda a: a.transpose(1, 0, 2).reshape(a.shape[1], -1)
    w_in_f = cols_first(gathered[0])
    w_attn, w_rw = w_in_f[:, :QKV_WIDTH], w_in_f[:, QKV_WIDTH:]
    w2_f, a2_f, g2_f = cols_first(gathered[1]), cols_first(gathered[2]), cols_first(gathered[3])
    lora = w2_f.shape[0]
    w2p = jnp.concatenate([w2_f, jnp.zeros_like(a2_f)], axis=0)
    a2p = jnp.concatenate([jnp.zeros_like(w2_f), a2_f], axis=0)
    w_out_f = gathered[4].reshape(-1, d_model)
    w_up_f = cols_first(gathered[5])
    w_down_f = gathered[6].reshape(-1, d_model)

    esum = _head_sum_matrix()
    sink_rows = jnp.broadcast_to(attn_sinks.reshape(N_ATTN_HEADS, 1), (N_ATTN_HEADS, LANES))
    prep_pars = [w0, w2p, a0, a2p, g2_f, k_k, k_a, esum]
    post_pars = [ln_x_w, ln_x_b, r_k, esum]

    x2d = x.reshape(rows, d_model)
    h1 = _norm_fwd(x2d, attn_norm_g, "attn_norm_fwd")
    z_attn = _matmul(h1, w_attn, "nn", name="in_proj_attn")
    z_rw = _matmul(h1, w_rw, "nn", name="in_proj_rwkv")
    z_attn3 = z_attn.reshape(bsz, seq, QKV_WIDTH)
    attn_out = _attn_fwd(z_attn3, sink_rows)
    z_prev = _shift_down(z_rw.reshape(bsz, seq, -1)).reshape(rows, -1)
    r, lw, k, v, kk, al, gate = _prep_fwd(z_rw, z_prev, rwkv_mu, prep_pars)
    as3 = lambda a: a.reshape(bsz, seq, RWKV_WIDTH)
    y, s_all = _rwkv_fwd(as3(r), as3(lw), as3(k), as3(v), as3(kk), as3(al))
    y2 = y.reshape(rows, RWKV_WIDTH)
    rw_out = _post_fwd(y2, r, k, v, gate, post_pars)
    mix = jnp.concatenate([attn_out.reshape(rows, ATTN_WIDTH), rw_out], axis=1)
    residual = lambda acc, res: (acc + res,)
    x1 = _matmul(mix, w_out_f, "nn", name="out_proj", extras=(x2d,), epilogue=residual)
    h2 = _norm_fwd(x1, mlp_norm_g, "mlp_norm_fwd")

    def relu_sq(acc):
        pos = jnp.maximum(acc, 0.0)
        return acc, pos * pos

    u, act = _matmul(h2, w_up_f, "nn", name="mlp_up", epilogue=relu_sq, out_dtypes=(BF16, BF16))
    x2 = _matmul(act, w_down_f, "nn", name="mlp_down", extras=(x1,), epilogue=residual)
    dx2, loss_vec, g_final = _final_loss(x2, loss_target.reshape(rows, d_model), final_norm_g.reshape(1, d_model))

    g_w_down = _matmul(act, dx2, "tn", name="grad_w_down")
    du = _matmul(dx2, w_down_f, "nt", name="mlp_down_bwd", extras=(u,), out_dtypes=(BF16,),
                 epilogue=lambda acc, uv: (acc * (2.0 * jnp.maximum(uv.astype(F32), 0.0)),))
    g_w_up = _matmul(h2, du, "tn", name="grad_w_up")
    dh2 = _matmul(du, w_up_f, "nt", name="mlp_up_bwd")
    dx1, g_mlp_norm = _norm_bwd(x1, dh2, dx2, mlp_norm_g, "mlp_norm_bwd")
    g_w_out = _matmul(mix, dx1, "tn", name="grad_w_out")
    dmix = _matmul(dx1, w_out_f, "nt", name="out_proj_bwd")
    dy, dr_a, dk_a, dv_a, dgate, g_ln_w, g_ln_b, g_r_k = _post_bwd(
        y2, r, k, v, gate, (dmix, RWKV_WIDTH, ATTN_WIDTH // RWKV_WIDTH), post_pars)
    dr_b, dlw, dk_b, dv_b, dkk, dal = _rwkv_bwd(as3(r), as3(lw), as3(k), as3(v), as3(kk), as3(al), s_all, as3(dy))
    flat = lambda a: a.reshape(rows, RWKV_WIDTH)
    (dzc, dzp, gmu_r, gmu_k, gmu_v, gmu_wa, gmu_g, g_w0, g_w2p, g_a0, g_a2p, g_g2, g_k_k, g_k_a) = _prep_bwd(
        z_rw, z_prev, [dr_a, flat(dr_b), flat(dlw), dk_a, flat(dk_b), dv_a, flat(dv_b), flat(dkk), flat(dal), dgate],
        rwkv_mu, prep_pars)
    dz_rw = dzc + _shift_up(dzp.reshape(bsz, seq, -1)).reshape(rows, -1)
    dz_attn, g_sink_rows = _attn_bwd(z_attn3, dmix.reshape(bsz, seq, d_model), sink_rows)
    dz = jnp.concatenate([dz_attn.reshape(rows, QKV_WIDTH), dz_rw], axis=1).astype(BF16)
    g_w_in = _matmul(h1, dz, "tn", name="grad_w_in")
    dh1 = _matmul(dz, w_in_f, "nt", name="in_proj_bwd")
    dx, g_attn_norm = _norm_bwd(x2d, dh1, dx1, attn_norm_g, "attn_norm_bwd")

    by_cols = lambda a: a.reshape(a.shape[0], N_DEV, -1).transpose(1, 0, 2)
    lora_grads = jnp.concatenate([g_w2p[:lora], g_a2p[lora:], g_g2], axis=0)
    parts = _exchange([by_cols(g_w_in), by_cols(lora_grads), g_w_out.reshape(N_DEV, -1, d_model), by_cols(g_w_up),
                       g_w_down.reshape(N_DEV, -1, d_model)], scatter=True, name="scatter_grads")
    vec_grads = dict(attn_norm_g=g_attn_norm, attn_sinks=g_sink_rows[:, 0], rwkv_mu=jnp.concatenate(
        [gmu_r, gmu_k, gmu_v, gmu_wa, gmu_g], axis=1), w0=g_w0, a0=g_a0, k_k=g_k_k, k_a=g_k_a, r_k=g_r_k,
        ln_x_w=g_ln_w, ln_x_b=g_ln_b, mlp_norm_g=g_mlp_norm, final_norm_g=g_final)
    vec_parts = _exchange([_pack_vectors(vec_grads)], scatter=False, name="gather_vector_grads")[0]

    grads, delta, new_m, new_v = {}, {}, {}, {}

    def update(name, part, shape2d):
        res = _adamw(part, weights[name].reshape(shape2d), mom1[name].reshape(shape2d), mom2[name].reshape(shape2d),
                     "adamw_" + name)
        for store, val in zip((grads, delta, new_m, new_v), res):
            store[name] = val.reshape(weights[name].shape)

    update("w_in", parts[0], w_in.shape[1:])
    update("w_out", parts[2], w_out.shape[1:])
    update("w_up", parts[3], w_up.shape[1:])
    update("w_down", parts[4], w_down.shape[1:])
    stack = lambda d: jnp.concatenate([d["w2"][0], d["a2"][0], d["g2"][0]], axis=0)
    lora_res = _adamw(parts[1], stack(weights), stack(mom1), stack(mom2), "adamw_lora")
    for store, val in zip((grads, delta, new_m, new_v), lora_res):
        store["w2"], store["a2"], store["g2"] = val[None, :lora], val[None, lora:2 * lora], val[None, 2 * lora:]
    vec_res = _adamw(vec_parts, _pack_vectors(weights), _pack_vectors(mom1), _pack_vectors(mom2), "adamw_vectors")
    for store, val in zip((grads, delta, new_m, new_v), vec_res):
        store.update(_unpack_vectors(val, weights))

    loss = lax.psum(loss_vec[0, 0], ("x", "y", "c"))
    return (loss, dx.reshape(x.shape), *[grads[n] for n in _WEIGHT_NAMES], *[delta[n] for n in _WEIGHT_NAMES],
            *[new_m[n] for n in _WEIGHT_NAMES], *[new_v[n] for n in _WEIGHT_NAMES])
```

```python
import functools
import math

import jax
import jax.numpy as jnp
from jax import lax
from jax.experimental import pallas as pl
from jax.experimental.pallas import tpu as pltpu

F32 = jnp.float32
BF16 = jnp.bfloat16
HI = lax.Precision.HIGHEST

N_DEV = 8
HEAD_DIM = 64
LANES = 128
N_ATTN_HEADS = 8
ATTN_WIDTH = 512
KV_WIDTH = 128
QKV_WIDTH = ATTN_WIDTH + 2 * KV_WIDTH
RWKV_WIDTH = 512
LORA_WA = 128
GATE_LORA = 128
RWKV_SHIFT_WIDTH = 3 * RWKV_WIDTH + LORA_WA + GATE_LORA
BLOCK = 128
CHUNK = 64
RMS_EPS = 1e-6
GN_EPS = 64e-5
L2_EPS = 1e-12
NEG_INF = -1e30
DECAY_SCALE = math.exp(-0.5)
ADAM_LR, ADAM_B1, ADAM_B2, ADAM_EPS, ADAM_WD, ADAM_STEP = 0.001, 0.9, 0.999, 1e-08, 0.01, 10

NN = (((1,), (0,)), ((), ()))
NT = (((1,), (1,)), ((), ()))
TN = (((0,), (0,)), ((), ()))
MESH = pl.DeviceIdType.MESH


def _dot(a, b, dn=NN, precision=None):
    return lax.dot_general(a, b, dn, precision=precision, preferred_element_type=F32)


def _bdot_raw(a, b, dn):
    return lax.dot_general(a.astype(BF16), b.astype(BF16), dn, preferred_element_type=F32)


@functools.partial(jax.custom_vjp, nondiff_argnums=(2, 3))
def _bdot_c(a, b, ca, cb):
    return _bdot_raw(a, b, (((ca,), (cb,)), ((), ())))


def _bdot_c_fwd(a, b, ca, cb):
    return _bdot_c(a, b, ca, cb), (a, b)


def _bdot_c_bwd(ca, cb, res, ct):
    a, b = res
    fa, fb = 1 - ca, 1 - cb
    da = _bdot_raw(ct, b, (((1,), (fb,)), ((), ()))) if ca == 1 else _bdot_raw(b, ct, (((fb,), (1,)), ((), ())))
    db = _bdot_raw(a, ct, (((fa,), (0,)), ((), ()))) if cb == 0 else _bdot_raw(ct, a, (((0,), (fa,)), ((), ())))
    return da, db


_bdot_c.defvjp(_bdot_c_fwd, _bdot_c_bwd)


def _bdot(a, b, dn=NN):
    return _bdot_c(a, b, dn[0][0][0], dn[0][1][0])


def _bdot_nn(a, b):
    return _bdot(a, b, NN)


def _bdot_nt(a, b):
    return _bdot(a, b, NT)


def _split3(x):
    hi = x.astype(BF16)
    rest = x - hi.astype(F32)
    mid = rest.astype(BF16)
    return hi, mid, (rest - mid.astype(F32)).astype(BF16)


def _running_sum(x, dn):
    c = x.shape[0]
    row = lax.broadcasted_iota(jnp.int32, (c, c), 0)
    col = lax.broadcasted_iota(jnp.int32, (c, c), 1)
    tri = jnp.where(row >= col, 1.0, 0.0).astype(BF16)
    parts = [lax.dot_general(tri, p, dn, preferred_element_type=F32) for p in _split3(x)]
    return parts[0] + parts[1] + parts[2]


@jax.custom_vjp
def _cumsum_rows(x):
    return _running_sum(x, NN)


_cumsum_rows.defvjp(lambda x: (_running_sum(x, NN), None), lambda _, ct: (_running_sum(ct, TN),))


@jax.custom_vjp
def _fold_rows(x):
    c = x.shape[0] // 2
    return x[:c] + x[c:]


_fold_rows.defvjp(lambda x: (_fold_rows(x), None), lambda _, ct: (jnp.concatenate([ct, ct], axis=0),))


def _sigmoid(x):
    return 1.0 / (1.0 + jnp.exp(-x))


def _pick(n, cands):
    for c in cands:
        if n % c == 0:
            return c
    return n


def _cparams(sem, vmem_mb=None):
    kw = dict(dimension_semantics=sem)
    if vmem_mb is not None:
        kw["vmem_limit_bytes"] = vmem_mb * 1024 * 1024
    return pltpu.CompilerParams(**kw)


def _matmul(a, b, mode, *, name, extras=(), epilogue=None, out_dtypes=(F32,), tm=1024, tn=1024, tk=1024):
    if mode == "nn":
        (M, K), (_, N) = a.shape, b.shape
    elif mode == "tn":
        (K, M), (_, N) = a.shape, b.shape
    else:
        (M, K), (N, _) = a.shape, b.shape
    tm = _pick(M, (tm, 512, 256, 128))
    tn = _pick(N, (tn, 896, 768, 512, 384, 256, 128))
    tk = _pick(K, (tk, 512, 256, 128))
    nk = K // tk
    ne, nout = len(extras), len(out_dtypes)
    if mode == "nn":
        a_spec = pl.BlockSpec((tm, tk), lambda i, j, k: (i, k))
        b_spec = pl.BlockSpec((tk, tn), lambda i, j, k: (k, j))
        dn = NN
    elif mode == "tn":
        a_spec = pl.BlockSpec((tk, tm), lambda i, j, k: (k, i))
        b_spec = pl.BlockSpec((tk, tn), lambda i, j, k: (k, j))
        dn = TN
    else:
        a_spec = pl.BlockSpec((tm, tk), lambda i, j, k: (i, k))
        b_spec = pl.BlockSpec((tn, tk), lambda i, j, k: (j, k))
        dn = NT
    o_spec = pl.BlockSpec((tm, tn), lambda i, j, k: (i, j))

    def body(*refs):
        a_ref, b_ref = refs[:2]
        e_refs = refs[2:2 + ne]
        o_refs = refs[2 + ne:2 + ne + nout]
        acc = refs[-1]
        kstep = pl.program_id(2)

        @pl.when(kstep == 0)
        def _():
            acc[...] = jnp.zeros_like(acc)

        acc[...] += _bdot_raw(a_ref[...], b_ref[...], dn)

        @pl.when(kstep == nk - 1)
        def _():
            if epilogue is None:
                outs = (acc[...],)
            else:
                outs = epilogue(acc[...], *[e[...] for e in e_refs])
            for o_ref, o in zip(o_refs, outs):
                o_ref[...] = o.astype(o_ref.dtype)

    outs = pl.pallas_call(
        body,
        grid=(M // tm, N // tn, nk),
        in_specs=[a_spec, b_spec] + [o_spec] * ne,
        out_specs=[o_spec] * nout,
        out_shape=[jax.ShapeDtypeStruct((M, N), dt) for dt in out_dtypes],
        scratch_shapes=[pltpu.VMEM((tm, tn), F32)],
        compiler_params=_cparams(("parallel", "parallel", "arbitrary"), 56),
        name=name,
    )(a, b, *extras)
    return outs[0] if nout == 1 else outs


def _rowwise(fn, rows, pars, out_rows, out_accs, *, tile, name):
    rows = [r if isinstance(r, tuple) else (r, r.shape[1], 0) for r in rows]
    R = rows[0][0].shape[0]
    tile = min(tile, R)
    nr, npar, nor, noa = len(rows), len(pars), len(out_rows), len(out_accs)

    def body(*refs):
        rin = refs[:nr]
        pin = refs[nr:nr + npar]
        orow = refs[nr + npar:nr + npar + nor]
        oacc = refs[nr + npar + nor:]
        outs = fn(*[r[...] for r in rin], *[p[...] for p in pin])
        for ref, o in zip(orow, outs[:nor]):
            if isinstance(o, (tuple, list)):
                col = 0
                for piece in o:
                    ref[:, col:col + piece.shape[1]] = piece.astype(ref.dtype)
                    col += piece.shape[1]
            else:
                ref[...] = o.astype(ref.dtype)
        step = pl.program_id(0)

        def accumulate(ref, o):
            @pl.when(step == 0)
            def _():
                ref[...] = o

            @pl.when(step > 0)
            def _():
                ref[...] += o

        for ref, o in zip(oacc, outs[nor:]):
            accumulate(ref, o)

    def colspec(width, cb):
        return pl.BlockSpec((tile, width), lambda i: (i, cb))

    return pl.pallas_call(
        body,
        grid=(R // tile,),
        in_specs=[colspec(w, cb) for (_, w, cb) in rows]
        + [pl.BlockSpec(p.shape, lambda i: (0, 0)) for p in pars],
        out_specs=[colspec(w, 0) for (w, _) in out_rows]
        + [pl.BlockSpec(s, lambda i: (0, 0)) for s in out_accs],
        out_shape=[jax.ShapeDtypeStruct((R, w), dt) for (w, dt) in out_rows]
        + [jax.ShapeDtypeStruct(s, F32) for s in out_accs],
        compiler_params=_cparams(("arbitrary",), 56),
        name=name,
    )(*[r[0] for r in rows], *pars)


def _rms_fn(x, g):
    return x * lax.rsqrt(jnp.mean(x * x, axis=-1, keepdims=True) + RMS_EPS) * g


def _norm_fwd(x, g, name):
    return _rowwise(lambda xv, gv: (_rms_fn(xv, gv),), [x], [g], [(x.shape[1], BF16)], [], tile=512, name=name)[0]


def _norm_bwd(x, dh, dres, g, name):
    def fn(xv, dhv, dresv, gv):
        _, vjp = jax.vjp(_rms_fn, xv, gv)
        dx, dg = vjp(dhv)
        return dx + dresv, dg

    return _rowwise(fn, [x, dh, dres], [g], [(x.shape[1], F32)], [g.shape], tile=256, name=name)


def _final_loss(x, tgt, g):
    d = x.shape[1]

    def fn(xv, tv, gv):
        y, vjp = jax.vjp(_rms_fn, xv, gv)
        err = y - tv
        loss = 0.5 * jnp.sum(jnp.sum(err * err, axis=-1, keepdims=True), axis=0, keepdims=True) / d
        dx, dg = vjp(err / d)
        return dx, jnp.broadcast_to(loss, (1, LANES)), dg

    return _rowwise(fn, [x, tgt], [g], [(d, F32)], [(1, LANES), g.shape], tile=256, name="final_norm_loss")


def _head_sum_matrix():
    i = lax.broadcasted_iota(jnp.int32, (RWKV_WIDTH, RWKV_WIDTH), 0) // HEAD_DIM
    j = lax.broadcasted_iota(jnp.int32, (RWKV_WIDTH, RWKV_WIDTH), 1) // HEAD_DIM
    return (i == j).astype(F32)


def _prep_core(xr, xk, xv, xwa, xg, w0, w2p, a0, a2p, g2, k_k, k_a, esum):
    lw = -DECAY_SCALE * _sigmoid(w0 + _bdot_nn(jnp.tanh(xwa), w2p))
    a = _sigmoid(a0 + _bdot_nn(xwa, a2p))
    g = _bdot_nn(_sigmoid(xg), g2)
    kk0 = xk * k_k
    kk = kk0 / jnp.maximum(jnp.sqrt(_dot(kk0 * kk0, esum, precision=HI)), L2_EPS)
    k = xk * (1.0 + (a - 1.0) * k_a)
    return xr, lw, k, xv, kk, a, g


_SEGS = ((0, 512), (512, 1024), (1024, 1536), (1536, 1664), (1664, 1792))


PREP_TILE = 256
SUBLANES = 8


def _shifted_tokens(z_ref, zprev_ref, tile_index, seq):
    zc = z_ref[...]
    start = (tile_index * PREP_TILE) % seq == 0
    before = jnp.where(start, 0.0, zprev_ref[SUBLANES - 1:SUBLANES, :])
    rowid = lax.broadcasted_iota(jnp.int32, zc.shape, 0)
    return zc, jnp.where(rowid == 0, before, pltpu.roll(zc, 1, 0))


def _prep_specs(z, mu, pars, index):
    width = z.shape[1]
    per = PREP_TILE // SUBLANES
    return ([pl.BlockSpec((PREP_TILE, width), lambda i: (index(i), 0)),
             pl.BlockSpec((SUBLANES, width), lambda i: (jnp.maximum(index(i) * per - 1, 0), 0))],
            [pl.BlockSpec(p.shape, lambda i: (0, 0)) for p in (mu, *pars)])


def _prep_fwd(z, seq, mu, pars):
    rows = z.shape[0]
    npar = len(pars)

    def body(z_ref, zprev_ref, mu_ref, *rest):
        par_refs, out_refs = rest[:npar], rest[npar:]
        zc, zp = _shifted_tokens(z_ref, zprev_ref, pl.program_id(0), seq)
        zs = zc + (zp - zc) * mu_ref[...]
        outs = _prep_core(*[zs[:, a:b] for a, b in _SEGS], *[p[...] for p in par_refs])
        for ref, o in zip(out_refs, outs):
            ref[...] = o

    zspecs, pspecs = _prep_specs(z, mu, pars, lambda i: i)
    return pl.pallas_call(
        body,
        grid=(rows // PREP_TILE,),
        in_specs=zspecs + pspecs,
        out_specs=[pl.BlockSpec((PREP_TILE, RWKV_WIDTH), lambda i: (i, 0))] * 7,
        out_shape=[jax.ShapeDtypeStruct((rows, RWKV_WIDTH), F32)] * 7,
        compiler_params=_cparams(("parallel",), 56),
        name="rwkv_prep_fwd",
    )(z, z, mu, *pars)


def _prep_bwd(z, seq, cts, mu, pars):
    rows, width = z.shape
    ntile = rows // PREP_TILE
    npar, nct = len(pars), len(cts)
    acc_shapes = [(1, b - a) for a, b in _SEGS] + [p.shape for p in pars[:-1]]

    def body(z_ref, zprev_ref, *rest):
        ct_refs = rest[:nct]
        mu_ref = rest[nct]
        par_refs = rest[nct + 1:nct + 1 + npar]
        dz_ref = rest[nct + 1 + npar]
        acc_refs = rest[nct + 2 + npar:-1]
        carry = rest[-1]
        step = pl.program_id(0)
        tile_index = ntile - 1 - step

        @pl.when(step == 0)
        def _():
            carry[...] = jnp.zeros_like(carry)

        zc, zp = _shifted_tokens(z_ref, zprev_ref, tile_index, seq)
        mu_v = mu_ref[...]
        diff = zp - zc
        zs = zc + diff * mu_v
        dra, drb, dlw, dka, dkb, dva, dvb, dkk, da, dg = [c[...] for c in ct_refs]
        pv = [p[...] for p in par_refs]
        _, vjp = jax.vjp(lambda *args: _prep_core(*args, pv[-1]), *[zs[:, a:b] for a, b in _SEGS], *pv[:-1])
        grads = vjp((dra + drb, dlw, dka + dkb, dva + dvb, dkk, da, dg))
        dsegs, dpars = grads[:5], grads[5:]
        last_of_sequence = ((tile_index + 1) * PREP_TILE) % seq == 0
        accs = []
        for ds, (a, b) in zip(dsegs, _SEGS):
            mu_s = mu_v[:, a:b]
            dzp = ds * mu_s
            after = jnp.where(last_of_sequence, 0.0, carry[0:1, a:b])
            rowid = lax.broadcasted_iota(jnp.int32, dzp.shape, 0)
            from_next = jnp.where(rowid == PREP_TILE - 1, after, pltpu.roll(dzp, PREP_TILE - 1, 0))
            dz_ref[:, a:b] = (ds * (1.0 - mu_s) + from_next).astype(dz_ref.dtype)
            carry[:, a:b] = dzp[0:SUBLANES, :]
            accs.append(jnp.sum(ds * diff[:, a:b], axis=0, keepdims=True))
        accs.extend(dpars)

        def accumulate(ref, o):
            @pl.when(step == 0)
            def _():
                ref[...] = o

            @pl.when(step > 0)
            def _():
                ref[...] += o

        for ref, o in zip(acc_refs, accs):
            accumulate(ref, o)

    rev = lambda i: ntile - 1 - i
    zspecs, pspecs = _prep_specs(z, mu, pars, rev)
    return pl.pallas_call(
        body,
        grid=(ntile,),
        in_specs=zspecs + [pl.BlockSpec((PREP_TILE, RWKV_WIDTH), lambda i: (rev(i), 0))] * nct + pspecs,
        out_specs=[pl.BlockSpec((PREP_TILE, width), lambda i: (rev(i), 0))]
        + [pl.BlockSpec(s, lambda i: (0, 0)) for s in acc_shapes],
        out_shape=[jax.ShapeDtypeStruct((rows, width), BF16)] + [jax.ShapeDtypeStruct(s, F32) for s in acc_shapes],
        scratch_shapes=[pltpu.VMEM((SUBLANES, width), F32)],
        compiler_params=_cparams(("arbitrary",), 56),
        name="rwkv_prep_bwd",
    )(z, z, *cts, mu, *pars)


def _post_fn(y, r, k, v, g, ln_w, ln_b, r_k, esum):
    mean = _dot(y, esum, precision=HI) * (1.0 / HEAD_DIM)
    yc = y - mean
    var = _dot(yc * yc, esum, precision=HI) * (1.0 / HEAD_DIM)
    yn = yc * lax.rsqrt(var + GN_EPS) * ln_w + ln_b
    bonus = _dot(r * k * r_k, esum, precision=HI) * v
    return (yn + bonus) * g


def _post_fwd(y, r, k, v, g, pars):
    return _rowwise(lambda *a: (_post_fn(*a),), [y, r, k, v, g], pars, [(RWKV_WIDTH, BF16)], [],
                    tile=256, name="rwkv_post_fwd")[0]


def _post_bwd(y, r, k, v, g, dout, pars):
    def fn(yv, rv, kv, vv, gv, dv_, ln_w, ln_b, r_k, esum):
        _, vjp = jax.vjp(lambda *a: _post_fn(*a, esum), yv, rv, kv, vv, gv, ln_w, ln_b, r_k)
        return vjp(dv_)

    return _rowwise(fn, [y, r, k, v, g, dout], pars, [(RWKV_WIDTH, F32)] * 5, [p.shape for p in pars[:-1]],
                    tile=256, name="rwkv_post_bwd")


def _chunk_fn(ss, rs, lws, ks, vs, kks, als):
    c = rs[0].shape[0]
    n = 2 * c
    row = lax.broadcasted_iota(jnp.int32, (n, n), 0)
    col = lax.broadcasted_iota(jnp.int32, (n, n), 1)
    incl = (row % c) >= (col % c)
    strict = (row % c) > (col % c)
    eye = jnp.where(row == col, 1.0, 0.0)
    lane = lax.broadcasted_iota(jnp.int32, (1, LANES), 1)
    m_lo = jnp.where(lane < HEAD_DIM, 1.0, 0.0)
    m_hi = 1.0 - m_lo

    def stack(a):
        return jnp.concatenate([a * m_lo, a * m_hi], axis=0)

    cums = [_cumsum_rows(lw) for lw in lws]
    totals = [jnp.sum(lw, axis=0, keepdims=True) for lw in lws]
    bs = [kk * al for kk, al in zip(kks, als)]
    grows = [jnp.exp(-cum) for cum in cums]
    a_s = [stack(-kk * jnp.exp(cum - lw)) for kk, cum, lw in zip(kks, cums, lws)]
    b_s = [stack(b * g) for b, g in zip(bs, grows)]
    k_s = [stack(k * g) for k, g in zip(ks, grows)]
    r_s = [stack(r * jnp.exp(cum)) for r, cum in zip(rs, cums)]
    v_s = [stack(v) for v in vs]
    m_ab = [jnp.where(strict, _bdot(a, b, NT), 0.0) for a, b in zip(a_s, b_s)]
    m_ak = [jnp.where(strict, _bdot(a, k, NT), 0.0) for a, k in zip(a_s, k_s)]
    m_rb = [jnp.where(incl, _bdot(r, b, NT), 0.0) for r, b in zip(r_s, b_s)]
    m_rk = [jnp.where(incl, _bdot(r, k, NT), 0.0) for r, k in zip(r_s, k_s)]
    t_inv = [eye + m for m in m_ab]
    power = m_ab
    for _ in range(int(math.log2(c)) - 1):
        power = [_bdot(p, p) for p in power]
        t_inv = [t + _bdot(t, p) for t, p in zip(t_inv, power)]
    x = [_bdot(a, s, NT) + _bdot(m, v) for a, s, m, v in zip(a_s, ss, m_ak, v_s)]
    u = [_bdot(t, xx) for t, xx in zip(t_inv, x)]
    y = [_fold_rows(_bdot(r, s, NT) + _bdot(mb, uu) + _bdot(mk, v))
         for r, s, mb, uu, mk, v in zip(r_s, ss, m_rb, u, m_rk, v_s)]
    tails = [jnp.exp(tot - cum) for tot, cum in zip(totals, cums)]
    s_new = [s * jnp.exp(tot) + _bdot(uu, stack(b * tl), TN) + _bdot(v, stack(k * tl), TN)
             for s, tot, uu, b, tl, v, k in zip(ss, totals, u, bs, tails, v_s, ks)]
    return tuple(y), tuple(s_new)


def _chains(bsz, npair):
    return [(b, p, slice(p * LANES, (p + 1) * LANES)) for b in range(bsz) for p in range(npair)]


def _rwkv_fwd(r, lw, k, v, kk, al):
    bsz, t, w = r.shape
    npair, nchunk = w // LANES, t // CHUNK
    chains = _chains(bsz, npair)

    def body(r_ref, lw_ref, k_ref, v_ref, kk_ref, al_ref, y_ref, sall_ref, s_scr):
        @pl.when(pl.program_id(0) == 0)
        def _():
            s_scr[...] = jnp.zeros_like(s_scr)

        ss = tuple(s_scr[i] for i in range(len(chains)))
        for i, s in enumerate(ss):
            sall_ref[0, i] = s
        ys, s_new = _chunk_fn(ss, *[tuple(ref[b, :, cols] for b, _, cols in chains)
                                    for ref in (r_ref, lw_ref, k_ref, v_ref, kk_ref, al_ref)])
        for i, (b, _, cols) in enumerate(chains):
            y_ref[b, :, cols] = ys[i]
            s_scr[i] = s_new[i]

    spec = pl.BlockSpec((bsz, CHUNK, w), lambda c: (0, c, 0))
    return pl.pallas_call(
        body,
        grid=(nchunk,),
        in_specs=[spec] * 6,
        out_specs=[spec, pl.BlockSpec((1, len(chains), LANES, LANES), lambda c: (c, 0, 0, 0))],
        out_shape=[jax.ShapeDtypeStruct((bsz, t, w), F32),
                   jax.ShapeDtypeStruct((nchunk, len(chains), LANES, LANES), F32)],
        scratch_shapes=[pltpu.VMEM((len(chains), LANES, LANES), F32)],
        compiler_params=_cparams(("arbitrary",), 48),
        name="rwkv_chunk_fwd",
    )(r, lw, k, v, kk, al)


def _rwkv_bwd(r, lw, k, v, kk, al, s_all, dy):
    bsz, t, w = r.shape
    npair, nchunk = w // LANES, t // CHUNK
    chains = _chains(bsz, npair)

    def body(r_ref, lw_ref, k_ref, v_ref, kk_ref, al_ref, s_ref, dy_ref, *rest):
        out_refs, ds_scr = rest[:6], rest[6]

        @pl.when(pl.program_id(0) == 0)
        def _():
            ds_scr[...] = jnp.zeros_like(ds_scr)

        ss = tuple(s_ref[0, i] for i in range(len(chains)))
        _, vjp = jax.vjp(_chunk_fn, ss, *[tuple(ref[b, :, cols] for b, _, cols in chains)
                                          for ref in (r_ref, lw_ref, k_ref, v_ref, kk_ref, al_ref)])
        grads = vjp((tuple(dy_ref[b, :, cols] for b, _, cols in chains),
                     tuple(ds_scr[i] for i in range(len(chains)))))
        for i, (b, _, cols) in enumerate(chains):
            ds_scr[i] = grads[0][i]
            for ref, gval in zip(out_refs, grads[1:]):
                ref[b, :, cols] = gval[i]

    spec = pl.BlockSpec((bsz, CHUNK, w), lambda c: (0, nchunk - 1 - c, 0))
    sspec = pl.BlockSpec((1, len(chains), LANES, LANES), lambda c: (nchunk - 1 - c, 0, 0, 0))
    return pl.pallas_call(
        body,
        grid=(nchunk,),
        in_specs=[spec] * 6 + [sspec, spec],
        out_specs=[spec] * 6,
        out_shape=[jax.ShapeDtypeStruct((bsz, t, w), F32)] * 6,
        scratch_shapes=[pltpu.VMEM((len(chains), LANES, LANES), F32)],
        compiler_params=_cparams(("arbitrary",), 48),
        name="rwkv_chunk_bwd",
    )(r, lw, k, v, kk, al, s_all, dy)


def _alibi_slope(head):
    return 2.0 ** (-8.0 * (head + 1) / N_ATTN_HEADS)


def _attn_pair(q, kp, kc, vp, vc, sinks, first, pair):
    kvh = pair // 2
    row = lax.broadcasted_iota(jnp.int32, (BLOCK, BLOCK), 0)
    col = lax.broadcasted_iota(jnp.int32, (BLOCK, BLOCK), 1)
    lane = lax.broadcasted_iota(jnp.int32, (1, LANES), 1)
    swap = jnp.where((row + HEAD_DIM) % LANES == col, 1.0, 0.0)
    dist_c = (row - col).astype(F32)
    dist_p = dist_c + float(BLOCK)
    valid_c = row >= col
    valid_p = jnp.logical_and(col > row, jnp.logical_not(first))
    out = jnp.zeros_like(q)
    psinks = []
    for half in range(2):
        head = 2 * pair + half
        m = jnp.where((lane // HEAD_DIM) == half, 1.0, 0.0)
        if half == kvh:
            kpx, kcx, vpx, vcx = kp, kc, vp, vc
        else:
            kpx, kcx, vpx, vcx = (_bdot_nn(t, swap) for t in (kp, kc, vp, vc))
        qa = q * m
        slope = _alibi_slope(head)
        sp = jnp.where(valid_p, _bdot_nt(qa, kpx) * (HEAD_DIM ** -0.5) - slope * dist_p, NEG_INF)
        sc = jnp.where(valid_c, _bdot_nt(qa, kcx) * (HEAD_DIM ** -0.5) - slope * dist_c, NEG_INF)
        sink = sinks[half]
        mx = jnp.maximum(jnp.maximum(jnp.max(sp, axis=-1, keepdims=True), jnp.max(sc, axis=-1, keepdims=True)), sink)
        mx = lax.stop_gradient(mx)
        ep = jnp.exp(sp - mx)
        ec = jnp.exp(sc - mx)
        es = jnp.exp(sink - mx)
        inv = 1.0 / (jnp.sum(ep, axis=-1, keepdims=True) + jnp.sum(ec, axis=-1, keepdims=True) + es)
        o = _bdot_nn(ep * inv, vpx) + _bdot_nn(ec * inv, vcx)
        out = out + o * m
        psinks.append(lax.stop_gradient(es * inv))
    return out, psinks


def _sink_values(sink_ref, pair):
    return [jnp.max(sink_ref[2 * pair + half:2 * pair + half + 1, :], axis=-1, keepdims=True) for half in range(2)]


def _attn_fwd(z, sink_rows):
    bsz, t, _ = z.shape
    nb = t // BLOCK
    npair = ATTN_WIDTH // LANES

    def body(q_ref, kp_ref, kc_ref, vp_ref, vc_ref, sink_ref, o_ref):
        first = pl.program_id(1) == 0
        for pair in range(npair):
            q = q_ref[0, :, pair * LANES:(pair + 1) * LANES]
            o, _ = _attn_pair(q, kp_ref[0], kc_ref[0], vp_ref[0], vc_ref[0], _sink_values(sink_ref, pair), first, pair)
            o_ref[0, :, pair * LANES:(pair + 1) * LANES] = o.astype(o_ref.dtype)

    kcol, vcol = ATTN_WIDTH // KV_WIDTH, ATTN_WIDTH // KV_WIDTH + 1
    return pl.pallas_call(
        body,
        grid=(bsz, nb),
        in_specs=[pl.BlockSpec((1, BLOCK, ATTN_WIDTH), lambda b, n: (b, n, 0)),
                  pl.BlockSpec((1, BLOCK, KV_WIDTH), lambda b, n: (b, jnp.maximum(n - 1, 0), kcol)),
                  pl.BlockSpec((1, BLOCK, KV_WIDTH), lambda b, n: (b, n, kcol)),
                  pl.BlockSpec((1, BLOCK, KV_WIDTH), lambda b, n: (b, jnp.maximum(n - 1, 0), vcol)),
                  pl.BlockSpec((1, BLOCK, KV_WIDTH), lambda b, n: (b, n, vcol)),
                  pl.BlockSpec(sink_rows.shape, lambda b, n: (0, 0))],
        out_specs=pl.BlockSpec((1, BLOCK, ATTN_WIDTH), lambda b, n: (b, n, 0)),
        out_shape=jax.ShapeDtypeStruct((bsz, t, ATTN_WIDTH), BF16),
        compiler_params=_cparams(("parallel", "arbitrary"), 48),
        name="swa_fwd",
    )(z, z, z, z, z, sink_rows)


def _attn_bwd(z, dout, sink_rows):
    bsz, t, _ = z.shape
    nb = t // BLOCK
    npair = ATTN_WIDTH // LANES

    def body(q_ref, kp_ref, kc_ref, vp_ref, vc_ref, do_ref, sink_ref, dz_ref, dsink_ref, carry):
        step = pl.program_id(1)
        n = nb - 1 - step
        first = n == 0

        @pl.when(step == 0)
        def _():
            carry[...] = jnp.zeros_like(carry)

        @pl.when(jnp.logical_and(step == 0, pl.program_id(0) == 0))
        def _():
            dsink_ref[...] = jnp.zeros_like(dsink_ref)

        dkp = jnp.zeros((BLOCK, KV_WIDTH), F32)
        dkc, dvp, dvc = dkp, dkp, dkp
        lane = lax.broadcasted_iota(jnp.int32, (1, LANES), 1)
        for pair in range(npair):
            cols = slice(pair * LANES, (pair + 1) * LANES)
            sinks = _sink_values(sink_ref, pair)
            fn = functools.partial(_attn_pair, sinks=sinks, first=first, pair=pair)
            o, vjp, psinks = jax.vjp(fn, q_ref[0, :, cols], kp_ref[0], kc_ref[0], vp_ref[0], vc_ref[0], has_aux=True)
            do = do_ref[0, :, cols]
            dq, g_kp, g_kc, g_vp, g_vc = vjp(do)
            dz_ref[0, :, cols] = dq.astype(dz_ref.dtype)
            dkp, dkc, dvp, dvc = dkp + g_kp, dkc + g_kc, dvp + g_vp, dvc + g_vc
            for half in range(2):
                m = jnp.where((lane // HEAD_DIM) == half, 1.0, 0.0)
                delta = jnp.sum(do * o * m, axis=-1, keepdims=True)
                ds = -jnp.sum(psinks[half] * delta, axis=0, keepdims=True)
                head = 2 * pair + half
                dsink_ref[head:head + 1, :] += jnp.broadcast_to(ds, (1, LANES))
        dz_ref[0, :, ATTN_WIDTH:ATTN_WIDTH + KV_WIDTH] = (dkc + carry[0]).astype(dz_ref.dtype)
        dz_ref[0, :, ATTN_WIDTH + KV_WIDTH:QKV_WIDTH] = (dvc + carry[1]).astype(dz_ref.dtype)
        carry[0] = dkp
        carry[1] = dvp

    kcol, vcol = ATTN_WIDTH // KV_WIDTH, ATTN_WIDTH // KV_WIDTH + 1
    rev = lambda n: nb - 1 - n
    return pl.pallas_call(
        body,
        grid=(bsz, nb),
        in_specs=[pl.BlockSpec((1, BLOCK, ATTN_WIDTH), lambda b, n: (b, rev(n), 0)),
                  pl.BlockSpec((1, BLOCK, KV_WIDTH), lambda b, n: (b, jnp.maximum(rev(n) - 1, 0), kcol)),
                  pl.BlockSpec((1, BLOCK, KV_WIDTH), lambda b, n: (b, rev(n), kcol)),
                  pl.BlockSpec((1, BLOCK, KV_WIDTH), lambda b, n: (b, jnp.maximum(rev(n) - 1, 0), vcol)),
                  pl.BlockSpec((1, BLOCK, KV_WIDTH), lambda b, n: (b, rev(n), vcol)),
                  pl.BlockSpec((1, BLOCK, ATTN_WIDTH), lambda b, n: (b, rev(n), 0)),
                  pl.BlockSpec(sink_rows.shape, lambda b, n: (0, 0))],
        out_specs=[pl.BlockSpec((1, BLOCK, QKV_WIDTH), lambda b, n: (b, rev(n), 0)),
                   pl.BlockSpec((N_ATTN_HEADS, LANES), lambda b, n: (0, 0))],
        out_shape=[jax.ShapeDtypeStruct((bsz, t, QKV_WIDTH), BF16),
                   jax.ShapeDtypeStruct((N_ATTN_HEADS, LANES), F32)],
        scratch_shapes=[pltpu.VMEM((2, BLOCK, KV_WIDTH), F32)],
        compiler_params=_cparams(("arbitrary", "arbitrary"), 48),
        name="swa_bwd",
    )(z, z, z, z, z, dout, sink_rows)


def _exchange(arrays, *, scatter, name):
    n = len(arrays)
    out_shape = [jax.ShapeDtypeStruct((N_DEV,) + (a.shape[1:] if scatter else a.shape), a.dtype) for a in arrays]

    def body(*refs):
        ins, outs = refs[:n], refs[n:2 * n]
        send_sems, recv_sems, local_sems = refs[2 * n:]
        x, y, c = lax.axis_index("x"), lax.axis_index("y"), lax.axis_index("c")
        me = 4 * x + 2 * y + c
        pending = []
        for i in range(n):
            own = pltpu.make_async_copy(ins[i].at[me] if scatter else ins[i], outs[i].at[me], local_sems.at[i])
            own.start()
            pending.append((own, None))
            for d in range(1, N_DEV):
                px = 1 - x if d & 4 else x
                py = 1 - y if d & 2 else y
                pc = 1 - c if d & 1 else c
                peer = 4 * px + 2 * py + pc
                src = ins[i].at[peer] if scatter else ins[i]
                send = pltpu.make_async_remote_copy(src, outs[i].at[me], send_sems.at[i, d - 1], recv_sems.at[i, d - 1],
                                                    device_id=(px, py, pc), device_id_type=MESH)
                send.start()
                recv = pltpu.make_async_remote_copy(src, outs[i].at[peer], send_sems.at[i, d - 1], recv_sems.at[i, d - 1],
                                                    device_id=(px, py, pc), device_id_type=MESH)
                pending.append((send, recv))
        for send, recv in pending:
            if recv is None:
                send.wait()
            else:
                send.wait_send()
                recv.wait_recv()

    hbm = pl.BlockSpec(memory_space=pltpu.HBM)
    return pl.pallas_call(
        body,
        in_specs=[hbm] * n,
        out_specs=[hbm] * n,
        out_shape=out_shape,
        scratch_shapes=[pltpu.SemaphoreType.DMA((n, N_DEV - 1)), pltpu.SemaphoreType.DMA((n, N_DEV - 1)),
                        pltpu.SemaphoreType.DMA((n,))],
        compiler_params=pltpu.CompilerParams(has_side_effects=True),
        name=name,
    )(*arrays)


def _adamw(parts, w, m, v, name):
    rows, cols = w.shape
    tr = _pick(rows, (256, 128, 64, 8))
    c1 = 1.0 / (1.0 - ADAM_B1 ** ADAM_STEP)
    c2 = 1.0 / (1.0 - ADAM_B2 ** ADAM_STEP)

    def body(p_ref, w_ref, m_ref, v_ref, g_ref, d_ref, mo_ref, vo_ref):
        g = p_ref[0].astype(F32)
        for s in range(1, N_DEV):
            g = g + p_ref[s].astype(F32)
        mn = ADAM_B1 * m_ref[...] + (1.0 - ADAM_B1) * g
        vn = ADAM_B2 * v_ref[...] + (1.0 - ADAM_B2) * (g * g)
        g_ref[...] = g
        mo_ref[...] = mn
        vo_ref[...] = vn
        d_ref[...] = -ADAM_LR * ((mn * c1) / (jnp.sqrt(vn * c2) + ADAM_EPS) + ADAM_WD * w_ref[...])

    spec = pl.BlockSpec((tr, cols), lambda i: (i, 0))
    return pl.pallas_call(
        body,
        grid=(rows // tr,),
        in_specs=[pl.BlockSpec((N_DEV, tr, cols), lambda i: (0, i, 0)), spec, spec, spec],
        out_specs=[spec] * 4,
        out_shape=[jax.ShapeDtypeStruct((rows, cols), F32)] * 4,
        compiler_params=_cparams(("parallel",), 48),
        name=name,
    )(parts, w, m, v)


_VECTOR_PARAMS = ("attn_norm_g", "attn_sinks", "rwkv_mu", "w0", "a0", "k_k", "k_a", "r_k", "ln_x_w", "ln_x_b",
                  "mlp_norm_g", "final_norm_g")
_WEIGHT_NAMES = ("attn_norm_g", "w_in", "attn_sinks", "rwkv_mu", "w0", "w2", "a0", "a2", "g2", "k_k", "k_a", "r_k",
                 "ln_x_w", "ln_x_b", "w_out", "mlp_norm_g", "w_up", "w_down", "final_norm_g")


def _pack_vectors(vals):
    pieces = []
    for name in _VECTOR_PARAMS:
        flat = vals[name].reshape(1, -1)
        pad = (-flat.shape[1]) % LANES
        pieces.append(jnp.pad(flat, ((0, 0), (0, pad))) if pad else flat)
    return jnp.concatenate(pieces, axis=1)


def _unpack_vectors(packed, like):
    out, col = {}, 0
    for name in _VECTOR_PARAMS:
        size = like[name].size
        out[name] = packed[0, col:col + size].reshape(like[name].shape)
        col += size + (-size) % LANES
    return out


def kernel(x, attn_norm_g, w_in, attn_sinks, rwkv_mu, w0, w2, a0, a2, g2, k_k, k_a, r_k, ln_x_w, ln_x_b, w_out, mlp_norm_g, w_up, w_down, final_norm_g, loss_target, m_attn_norm_g, m_w_in, m_attn_sinks, m_rwkv_mu, m_w0, m_w2, m_a0, m_a2, m_g2, m_k_k, m_k_a, m_r_k, m_ln_x_w, m_ln_x_b, m_w_out, m_mlp_norm_g, m_w_up, m_w_down, m_final_norm_g, v_attn_norm_g, v_w_in, v_attn_sinks, v_rwkv_mu, v_w0, v_w2, v_a0, v_a2, v_g2, v_k_k, v_k_a, v_r_k, v_ln_x_w, v_ln_x_b, v_w_out, v_mlp_norm_g, v_w_up, v_w_down, v_final_norm_g):
    weights = dict(attn_norm_g=attn_norm_g, w_in=w_in, attn_sinks=attn_sinks, rwkv_mu=rwkv_mu, w0=w0, w2=w2, a0=a0,
                   a2=a2, g2=g2, k_k=k_k, k_a=k_a, r_k=r_k, ln_x_w=ln_x_w, ln_x_b=ln_x_b, w_out=w_out,
                   mlp_norm_g=mlp_norm_g, w_up=w_up, w_down=w_down, final_norm_g=final_norm_g)
    mom1 = dict(attn_norm_g=m_attn_norm_g, w_in=m_w_in, attn_sinks=m_attn_sinks, rwkv_mu=m_rwkv_mu, w0=m_w0, w2=m_w2,
                a0=m_a0, a2=m_a2, g2=m_g2, k_k=m_k_k, k_a=m_k_a, r_k=m_r_k, ln_x_w=m_ln_x_w, ln_x_b=m_ln_x_b,
                w_out=m_w_out, mlp_norm_g=m_mlp_norm_g, w_up=m_w_up, w_down=m_w_down, final_norm_g=m_final_norm_g)
    mom2 = dict(attn_norm_g=v_attn_norm_g, w_in=v_w_in, attn_sinks=v_attn_sinks, rwkv_mu=v_rwkv_mu, w0=v_w0, w2=v_w2,
                a0=v_a0, a2=v_a2, g2=v_g2, k_k=v_k_k, k_a=v_k_a, r_k=v_r_k, ln_x_w=v_ln_x_w, ln_x_b=v_ln_x_b,
                w_out=v_w_out, mlp_norm_g=v_mlp_norm_g, w_up=v_w_up, w_down=v_w_down, final_norm_g=v_final_norm_g)
    bsz, seq, d_model = x.shape
    rows = bsz * seq
    d_in = N_DEV * w_in.shape[2]
    d_ff = N_DEV * w_up.shape[2]

    gathered = _exchange([w_in[0].astype(BF16), w2[0], a2[0], g2[0], w_out[0].astype(BF16), w_up[0].astype(BF16),
                          w_down[0].astype(BF16)], scatter=False, name="gather_weights")
    cols_first = lambda a: a.transpose(1, 0, 2).reshape(a.shape[1], -1)
    w_in_f = cols_first(gathered[0])
    w_attn, w_rw = w_in_f[:, :QKV_WIDTH], w_in_f[:, QKV_WIDTH:]
    w2_f, a2_f, g2_f = cols_first(gathered[1]), cols_first(gathered[2]), cols_first(gathered[3])
    lora = w2_f.shape[0]
    w2p = jnp.concatenate([w2_f, jnp.zeros_like(a2_f)], axis=0)
    a2p = jnp.concatenate([jnp.zeros_like(w2_f), a2_f], axis=0)
    w_out_f = gathered[4].reshape(-1, d_model)
    w_up_f = cols_first(gathered[5])
    w_down_f = gathered[6].reshape(-1, d_model)

    esum = _head_sum_matrix()
    sink_rows = jnp.broadcast_to(attn_sinks.reshape(N_ATTN_HEADS, 1), (N_ATTN_HEADS, LANES))
    prep_pars = [w0, w2p, a0, a2p, g2_f, k_k, k_a, esum]
    post_pars = [ln_x_w, ln_x_b, r_k, esum]

    x2d = x.reshape(rows, d_model)
    h1 = _norm_fwd(x2d, attn_norm_g, "attn_norm_fwd")
    z_attn = _matmul(h1, w_attn, "nn", name="in_proj_attn")
    z_rw = _matmul(h1, w_rw, "nn", name="in_proj_rwkv")
    z_attn3 = z_attn.reshape(bsz, seq, QKV_WIDTH)
    attn_out = _attn_fwd(z_attn3, sink_rows)
    r, lw, k, v, kk, al, gate = _prep_fwd(z_rw, seq, rwkv_mu, prep_pars)
    as3 = lambda a: a.reshape(bsz, seq, RWKV_WIDTH)
    y, s_all = _rwkv_fwd(as3(r), as3(lw), as3(k), as3(v), as3(kk), as3(al))
    y2 = y.reshape(rows, RWKV_WIDTH)
    rw_out = _post_fwd(y2, r, k, v, gate, post_pars)
    mix = jnp.concatenate([attn_out.reshape(rows, ATTN_WIDTH), rw_out], axis=1)
    residual = lambda acc, res: (acc + res,)
    x1 = _matmul(mix, w_out_f, "nn", name="out_proj", extras=(x2d,), epilogue=residual)
    h2 = _norm_fwd(x1, mlp_norm_g, "mlp_norm_fwd")

    def relu_sq(acc):
        pos = jnp.maximum(acc, 0.0)
        return acc, pos * pos

    u, act = _matmul(h2, w_up_f, "nn", name="mlp_up", epilogue=relu_sq, out_dtypes=(BF16, BF16))
    x2 = _matmul(act, w_down_f, "nn", name="mlp_down", extras=(x1,), epilogue=residual)
    dx2, loss_vec, g_final = _final_loss(x2, loss_target.reshape(rows, d_model), final_norm_g.reshape(1, d_model))

    g_w_down = _matmul(act, dx2, "tn", name="grad_w_down", out_dtypes=(BF16,))
    du = _matmul(dx2, w_down_f, "nt", name="mlp_down_bwd", extras=(u,), out_dtypes=(BF16,),
                 epilogue=lambda acc, uv: (acc * (2.0 * jnp.maximum(uv.astype(F32), 0.0)),))
    g_w_up = _matmul(h2, du, "tn", name="grad_w_up", out_dtypes=(BF16,))
    dh2 = _matmul(du, w_up_f, "nt", name="mlp_up_bwd")
    dx1, g_mlp_norm = _norm_bwd(x1, dh2, dx2, mlp_norm_g, "mlp_norm_bwd")
    g_w_out = _matmul(mix, dx1, "tn", name="grad_w_out", out_dtypes=(BF16,))
    dmix = _matmul(dx1, w_out_f, "nt", name="out_proj_bwd")
    dy, dr_a, dk_a, dv_a, dgate, g_ln_w, g_ln_b, g_r_k = _post_bwd(
        y2, r, k, v, gate, (dmix, RWKV_WIDTH, ATTN_WIDTH // RWKV_WIDTH), post_pars)
    dr_b, dlw, dk_b, dv_b, dkk, dal = _rwkv_bwd(as3(r), as3(lw), as3(k), as3(v), as3(kk), as3(al), s_all, as3(dy))
    flat = lambda a: a.reshape(rows, RWKV_WIDTH)
    (dz_rw, gmu_r, gmu_k, gmu_v, gmu_wa, gmu_g, g_w0, g_w2p, g_a0, g_a2p, g_g2, g_k_k, g_k_a) = _prep_bwd(
        z_rw, seq, [dr_a, flat(dr_b), flat(dlw), dk_a, flat(dk_b), dv_a, flat(dv_b), flat(dkk), flat(dal), dgate],
        rwkv_mu, prep_pars)
    dz_attn, g_sink_rows = _attn_bwd(z_attn3, dmix.reshape(bsz, seq, d_model), sink_rows)
    dz_attn = dz_attn.reshape(rows, QKV_WIDTH)
    g_w_in = jnp.concatenate([_matmul(h1, dz_attn, "tn", name="grad_w_in_attn", out_dtypes=(BF16,)),
                              _matmul(h1, dz_rw, "tn", name="grad_w_in_rwkv", out_dtypes=(BF16,))], axis=1)
    dh1 = _matmul(dz_attn, w_attn, "nt", name="in_proj_attn_bwd")
    dh1 = _matmul(dz_rw, w_rw, "nt", name="in_proj_rwkv_bwd", extras=(dh1,), epilogue=residual)
    dx, g_attn_norm = _norm_bwd(x2d, dh1, dx1, attn_norm_g, "attn_norm_bwd")

    by_cols = lambda a: a.reshape(a.shape[0], N_DEV, -1).transpose(1, 0, 2)
    lora_grads = jnp.concatenate([g_w2p[:lora], g_a2p[lora:], g_g2], axis=0)
    parts = _exchange([by_cols(g_w_in), by_cols(lora_grads), g_w_out.reshape(N_DEV, -1, d_model), by_cols(g_w_up),
                       g_w_down.reshape(N_DEV, -1, d_model)], scatter=True, name="scatter_grads")
    vec_grads = dict(attn_norm_g=g_attn_norm, attn_sinks=g_sink_rows[:, 0], rwkv_mu=jnp.concatenate(
        [gmu_r, gmu_k, gmu_v, gmu_wa, gmu_g], axis=1), w0=g_w0, a0=g_a0, k_k=g_k_k, k_a=g_k_a, r_k=g_r_k,
        ln_x_w=g_ln_w, ln_x_b=g_ln_b, mlp_norm_g=g_mlp_norm, final_norm_g=g_final)
    vec_parts = _exchange([_pack_vectors(vec_grads)], scatter=False, name="gather_vector_grads")[0]

    grads, delta, new_m, new_v = {}, {}, {}, {}

    def update(name, part, shape2d):
        res = _adamw(part, weights[name].reshape(shape2d), mom1[name].reshape(shape2d), mom2[name].reshape(shape2d),
                     "adamw_" + name)
        for store, val in zip((grads, delta, new_m, new_v), res):
            store[name] = val.reshape(weights[name].shape)

    update("w_in", parts[0], w_in.shape[1:])
    update("w_out", parts[2], w_out.shape[1:])
    update("w_up", parts[3], w_up.shape[1:])
    update("w_down", parts[4], w_down.shape[1:])
    stack = lambda d: jnp.concatenate([d["w2"][0], d["a2"][0], d["g2"][0]], axis=0)
    lora_res = _adamw(parts[1], stack(weights), stack(mom1), stack(mom2), "adamw_lora")
    for store, val in zip((grads, delta, new_m, new_v), lora_res):
        store["w2"], store["a2"], store["g2"] = val[None, :lora], val[None, lora:2 * lora], val[None, 2 * lora:]
    vec_res = _adamw(vec_parts, _pack_vectors(weights), _pack_vectors(mom1), _pack_vectors(mom2), "adamw_vectors")
    for store, val in zip((grads, delta, new_m, new_v), vec_res):
        store.update(_unpack_vectors(val, weights))

    loss = lax.psum(loss_vec[0, 0], ("x", "y", "c"))
    return (loss, dx.reshape(x.shape), *[grads[n] for n in _WEIGHT_NAMES], *[delta[n] for n in _WEIGHT_NAMES],
            *[new_m[n] for n in _WEIGHT_NAMES], *[new_v[n] for n in _WEIGHT_NAMES])
```

```python
import functools
import math

import jax
import jax.numpy as jnp
from jax import lax
from jax.experimental import pallas as pl
from jax.experimental.pallas import tpu as pltpu

F32 = jnp.float32
BF16 = jnp.bfloat16

N_DEV = 8
HEAD_DIM = 64
LANES = 128
N_ATTN_HEADS = 8
ATTN_WIDTH = 512
KV_WIDTH = 128
QKV_WIDTH = ATTN_WIDTH + 2 * KV_WIDTH
RWKV_WIDTH = 512
LORA_WA = 128
GATE_LORA = 128
RWKV_SHIFT_WIDTH = 3 * RWKV_WIDTH + LORA_WA + GATE_LORA
BLOCK = 128
CHUNK = 64
RMS_EPS = 1e-6
GN_EPS = 64e-5
L2_EPS = 1e-12
NEG_INF = -1e30
DECAY_SCALE = math.exp(-0.5)
ADAM_LR, ADAM_B1, ADAM_B2, ADAM_EPS, ADAM_WD, ADAM_STEP = 0.001, 0.9, 0.999, 1e-08, 0.01, 10

NN = (((1,), (0,)), ((), ()))
NT = (((1,), (1,)), ((), ()))
TN = (((0,), (0,)), ((), ()))
MESH = pl.DeviceIdType.MESH


def _dot(a, b, dn=NN, precision=None):
    return lax.dot_general(a, b, dn, precision=precision, preferred_element_type=F32)


def _bdot_raw(a, b, dn):
    return lax.dot_general(a.astype(BF16), b.astype(BF16), dn, preferred_element_type=F32)


@functools.partial(jax.custom_vjp, nondiff_argnums=(2, 3))
def _bdot_c(a, b, ca, cb):
    return _bdot_raw(a, b, (((ca,), (cb,)), ((), ())))


def _bdot_c_fwd(a, b, ca, cb):
    return _bdot_c(a, b, ca, cb), (a, b)


def _bdot_c_bwd(ca, cb, res, ct):
    a, b = res
    fa, fb = 1 - ca, 1 - cb
    da = _bdot_raw(ct, b, (((1,), (fb,)), ((), ()))) if ca == 1 else _bdot_raw(b, ct, (((fb,), (1,)), ((), ())))
    db = _bdot_raw(a, ct, (((fa,), (0,)), ((), ()))) if cb == 0 else _bdot_raw(ct, a, (((0,), (fa,)), ((), ())))
    return da, db


_bdot_c.defvjp(_bdot_c_fwd, _bdot_c_bwd)


def _bdot(a, b, dn=NN):
    return _bdot_c(a, b, dn[0][0][0], dn[0][1][0])


def _bdot_nn(a, b):
    return _bdot(a, b, NN)


def _bdot_nt(a, b):
    return _bdot(a, b, NT)


def _split3(x):
    hi = x.astype(BF16)
    rest = x - hi.astype(F32)
    mid = rest.astype(BF16)
    return hi, mid, (rest - mid.astype(F32)).astype(BF16)


def _running_sum(x, dn):
    c = x.shape[0]
    row = lax.broadcasted_iota(jnp.int32, (c, c), 0)
    col = lax.broadcasted_iota(jnp.int32, (c, c), 1)
    tri = jnp.where(row >= col, 1.0, 0.0).astype(BF16)
    parts = [lax.dot_general(tri, p, dn, preferred_element_type=F32) for p in _split3(x)]
    return parts[0] + parts[1] + parts[2]


@jax.custom_vjp
def _cumsum_rows(x):
    return _running_sum(x, NN)


_cumsum_rows.defvjp(lambda x: (_running_sum(x, NN), None), lambda _, ct: (_running_sum(ct, TN),))


@jax.custom_vjp
def _fold_rows(x):
    c = x.shape[0] // 2
    return x[:c] + x[c:]


_fold_rows.defvjp(lambda x: (_fold_rows(x), None), lambda _, ct: (jnp.concatenate([ct, ct], axis=0),))


def _sigmoid(x):
    return 1.0 / (1.0 + jnp.exp(-x))


def _pick(n, cands):
    for c in cands:
        if n % c == 0:
            return c
    return n


def _cparams(sem, vmem_mb=None):
    kw = dict(dimension_semantics=sem)
    if vmem_mb is not None:
        kw["vmem_limit_bytes"] = vmem_mb * 1024 * 1024
    return pltpu.CompilerParams(**kw)


def _matmul(a, b, mode, *, name, extras=(), epilogue=None, out_dtypes=(F32,), tm=1024, tn=1024, tk=1024):
    if mode == "nn":
        (M, K), (_, N) = a.shape, b.shape
    elif mode == "tn":
        (K, M), (_, N) = a.shape, b.shape
    else:
        (M, K), (N, _) = a.shape, b.shape
    tm = _pick(M, (tm, 512, 256, 128))
    tn = _pick(N, (tn, 896, 768, 512, 384, 256, 128))
    tk = _pick(K, (tk, 512, 256, 128))
    nk = K // tk
    ne, nout = len(extras), len(out_dtypes)
    if mode == "nn":
        a_spec = pl.BlockSpec((tm, tk), lambda i, j, k: (i, k))
        b_spec = pl.BlockSpec((tk, tn), lambda i, j, k: (k, j))
        dn = NN
    elif mode == "tn":
        a_spec = pl.BlockSpec((tk, tm), lambda i, j, k: (k, i))
        b_spec = pl.BlockSpec((tk, tn), lambda i, j, k: (k, j))
        dn = TN
    else:
        a_spec = pl.BlockSpec((tm, tk), lambda i, j, k: (i, k))
        b_spec = pl.BlockSpec((tn, tk), lambda i, j, k: (j, k))
        dn = NT
    o_spec = pl.BlockSpec((tm, tn), lambda i, j, k: (i, j))

    def body(*refs):
        a_ref, b_ref = refs[:2]
        e_refs = refs[2:2 + ne]
        o_refs = refs[2 + ne:2 + ne + nout]
        acc = refs[-1]
        kstep = pl.program_id(2)

        @pl.when(kstep == 0)
        def _():
            acc[...] = jnp.zeros_like(acc)

        acc[...] += _bdot_raw(a_ref[...], b_ref[...], dn)

        @pl.when(kstep == nk - 1)
        def _():
            if epilogue is None:
                outs = (acc[...],)
            else:
                outs = epilogue(acc[...], *[e[...] for e in e_refs])
            for o_ref, o in zip(o_refs, outs):
                o_ref[...] = o.astype(o_ref.dtype)

    outs = pl.pallas_call(
        body,
        grid=(M // tm, N // tn, nk),
        in_specs=[a_spec, b_spec] + [o_spec] * ne,
        out_specs=[o_spec] * nout,
        out_shape=[jax.ShapeDtypeStruct((M, N), dt) for dt in out_dtypes],
        scratch_shapes=[pltpu.VMEM((tm, tn), F32)],
        compiler_params=_cparams(("parallel", "parallel", "arbitrary"), 56),
        name=name,
    )(a, b, *extras)
    return outs[0] if nout == 1 else outs


def _rowwise(fn, rows, pars, out_rows, out_accs, *, tile, name):
    rows = [r if isinstance(r, tuple) else (r, r.shape[1], 0) for r in rows]
    R = rows[0][0].shape[0]
    tile = min(tile, R)
    nr, npar, nor, noa = len(rows), len(pars), len(out_rows), len(out_accs)

    def body(*refs):
        rin = refs[:nr]
        pin = refs[nr:nr + npar]
        orow = refs[nr + npar:nr + npar + nor]
        oacc = refs[nr + npar + nor:]
        outs = fn(*[r[...] for r in rin], *[p[...] for p in pin])
        for ref, o in zip(orow, outs[:nor]):
            if isinstance(o, (tuple, list)):
                col = 0
                for piece in o:
                    ref[:, col:col + piece.shape[1]] = piece.astype(ref.dtype)
                    col += piece.shape[1]
            else:
                ref[...] = o.astype(ref.dtype)
        step = pl.program_id(0)

        def accumulate(ref, o):
            @pl.when(step == 0)
            def _():
                ref[...] = o

            @pl.when(step > 0)
            def _():
                ref[...] += o

        for ref, o in zip(oacc, outs[nor:]):
            accumulate(ref, o)

    def colspec(width, cb):
        return pl.BlockSpec((tile, width), lambda i: (i, cb))

    return pl.pallas_call(
        body,
        grid=(R // tile,),
        in_specs=[colspec(w, cb) for (_, w, cb) in rows]
        + [pl.BlockSpec(p.shape, lambda i: (0, 0)) for p in pars],
        out_specs=[colspec(w, 0) for (w, _) in out_rows]
        + [pl.BlockSpec(s, lambda i: (0, 0)) for s in out_accs],
        out_shape=[jax.ShapeDtypeStruct((R, w), dt) for (w, dt) in out_rows]
        + [jax.ShapeDtypeStruct(s, F32) for s in out_accs],
        compiler_params=_cparams(("arbitrary",), 56),
        name=name,
    )(*[r[0] for r in rows], *pars)


def _rms_fn(x, g):
    return x * lax.rsqrt(jnp.mean(x * x, axis=-1, keepdims=True) + RMS_EPS) * g


def _norm_fwd(x, g, name):
    return _rowwise(lambda xv, gv: (_rms_fn(xv, gv),), [x], [g], [(x.shape[1], BF16)], [], tile=512, name=name)[0]


def _norm_bwd(x, dh, dres, g, name):
    def fn(xv, dhv, dresv, gv):
        _, vjp = jax.vjp(_rms_fn, xv, gv)
        dx, dg = vjp(dhv)
        return dx + dresv, dg

    return _rowwise(fn, [x, dh, dres], [g], [(x.shape[1], F32)], [g.shape], tile=256, name=name)


def _final_loss(x, tgt, g):
    d = x.shape[1]

    def fn(xv, tv, gv):
        y, vjp = jax.vjp(_rms_fn, xv, gv)
        err = y - tv
        loss = 0.5 * jnp.sum(jnp.sum(err * err, axis=-1, keepdims=True), axis=0, keepdims=True) / d
        dx, dg = vjp(err / d)
        return dx, jnp.broadcast_to(loss, (1, LANES)), dg

    return _rowwise(fn, [x, tgt], [g], [(d, F32)], [(1, LANES), g.shape], tile=256, name="final_norm_loss")


def _head_sum_matrix():
    i = lax.broadcasted_iota(jnp.int32, (RWKV_WIDTH, RWKV_WIDTH), 0) // HEAD_DIM
    j = lax.broadcasted_iota(jnp.int32, (RWKV_WIDTH, RWKV_WIDTH), 1) // HEAD_DIM
    return (i == j).astype(BF16)


def _head_sums_raw(x, esum):
    hi = x.astype(BF16)
    lo = (x - hi.astype(F32)).astype(BF16)
    return _dot(hi, esum) + _dot(lo, esum)


@jax.custom_vjp
def _head_sums(x, esum):
    return _head_sums_raw(x, esum)


_head_sums.defvjp(lambda x, esum: (_head_sums_raw(x, esum), esum),
                  lambda esum, ct: (_head_sums_raw(ct, esum), jnp.zeros_like(esum)))


def _prep_core(xr, xk, xv, xwa, xg, w0, w2p, a0, a2p, g2, k_k, k_a, esum):
    lw = -DECAY_SCALE * _sigmoid(w0 + _bdot_nn(jnp.tanh(xwa), w2p))
    a = _sigmoid(a0 + _bdot_nn(xwa, a2p))
    g = _bdot_nn(_sigmoid(xg), g2)
    kk0 = xk * k_k
    kk = kk0 / jnp.maximum(jnp.sqrt(_head_sums(kk0 * kk0, esum)), L2_EPS)
    k = xk * (1.0 + (a - 1.0) * k_a)
    return xr, lw, k, xv, kk, a, g


_SEGS = ((0, 512), (512, 1024), (1024, 1536), (1536, 1664), (1664, 1792))


PREP_TILE = 256
SUBLANES = 8


def _shifted_tokens(z_ref, zprev_ref, tile_index, seq):
    zc = z_ref[...]
    start = (tile_index * PREP_TILE) % seq == 0
    before = jnp.where(start, 0.0, zprev_ref[SUBLANES - 1:SUBLANES, :])
    rowid = lax.broadcasted_iota(jnp.int32, zc.shape, 0)
    return zc, jnp.where(rowid == 0, before, pltpu.roll(zc, 1, 0))


def _prep_specs(z, mu, pars, index):
    width = z.shape[1]
    per = PREP_TILE // SUBLANES
    return ([pl.BlockSpec((PREP_TILE, width), lambda i: (index(i), 0)),
             pl.BlockSpec((SUBLANES, width), lambda i: (jnp.maximum(index(i) * per - 1, 0), 0))],
            [pl.BlockSpec(p.shape, lambda i: (0, 0)) for p in (mu, *pars)])


def _prep_fwd(z, seq, mu, pars):
    rows = z.shape[0]
    npar = len(pars)

    def body(z_ref, zprev_ref, mu_ref, *rest):
        par_refs, out_refs = rest[:npar], rest[npar:]
        zc, zp = _shifted_tokens(z_ref, zprev_ref, pl.program_id(0), seq)
        zs = zc + (zp - zc) * mu_ref[...]
        outs = _prep_core(*[zs[:, a:b] for a, b in _SEGS], *[p[...] for p in par_refs])
        for ref, o in zip(out_refs, outs):
            ref[...] = o

    zspecs, pspecs = _prep_specs(z, mu, pars, lambda i: i)
    return pl.pallas_call(
        body,
        grid=(rows // PREP_TILE,),
        in_specs=zspecs + pspecs,
        out_specs=[pl.BlockSpec((PREP_TILE, RWKV_WIDTH), lambda i: (i, 0))] * 7,
        out_shape=[jax.ShapeDtypeStruct((rows, RWKV_WIDTH), F32)] * 7,
        compiler_params=_cparams(("parallel",), 56),
        name="rwkv_prep_fwd",
    )(z, z, mu, *pars)


def _prep_bwd(z, seq, cts, mu, pars):
    rows, width = z.shape
    ntile = rows // PREP_TILE
    npar, nct = len(pars), len(cts)
    acc_shapes = [(1, b - a) for a, b in _SEGS] + [p.shape for p in pars[:-1]]

    def body(z_ref, zprev_ref, *rest):
        ct_refs = rest[:nct]
        mu_ref = rest[nct]
        par_refs = rest[nct + 1:nct + 1 + npar]
        dz_ref = rest[nct + 1 + npar]
        acc_refs = rest[nct + 2 + npar:-1]
        carry = rest[-1]
        step = pl.program_id(0)
        tile_index = ntile - 1 - step

        @pl.when(step == 0)
        def _():
            carry[...] = jnp.zeros_like(carry)

        zc, zp = _shifted_tokens(z_ref, zprev_ref, tile_index, seq)
        mu_v = mu_ref[...]
        diff = zp - zc
        zs = zc + diff * mu_v
        dra, drb, dlw, dka, dkb, dva, dvb, dkk, da, dg = [c[...] for c in ct_refs]
        pv = [p[...] for p in par_refs]
        _, vjp = jax.vjp(lambda *args: _prep_core(*args, pv[-1]), *[zs[:, a:b] for a, b in _SEGS], *pv[:-1])
        grads = vjp((dra + drb, dlw, dka + dkb, dva + dvb, dkk, da, dg))
        dsegs, dpars = grads[:5], grads[5:]
        last_of_sequence = ((tile_index + 1) * PREP_TILE) % seq == 0
        accs = []
        for ds, (a, b) in zip(dsegs, _SEGS):
            mu_s = mu_v[:, a:b]
            dzp = ds * mu_s
            after = jnp.where(last_of_sequence, 0.0, carry[0:1, a:b])
            rowid = lax.broadcasted_iota(jnp.int32, dzp.shape, 0)
            from_next = jnp.where(rowid == PREP_TILE - 1, after, pltpu.roll(dzp, PREP_TILE - 1, 0))
            dz_ref[:, a:b] = (ds * (1.0 - mu_s) + from_next).astype(dz_ref.dtype)
            carry[:, a:b] = dzp[0:SUBLANES, :]
            accs.append(jnp.sum(ds * diff[:, a:b], axis=0, keepdims=True))
        accs.extend(dpars)

        def accumulate(ref, o):
            @pl.when(step == 0)
            def _():
                ref[...] = o

            @pl.when(step > 0)
            def _():
                ref[...] += o

        for ref, o in zip(acc_refs, accs):
            accumulate(ref, o)

    rev = lambda i: ntile - 1 - i
    zspecs, pspecs = _prep_specs(z, mu, pars, rev)
    return pl.pallas_call(
        body,
        grid=(ntile,),
        in_specs=zspecs + [pl.BlockSpec((PREP_TILE, RWKV_WIDTH), lambda i: (rev(i), 0))] * nct + pspecs,
        out_specs=[pl.BlockSpec((PREP_TILE, width), lambda i: (rev(i), 0))]
        + [pl.BlockSpec(s, lambda i: (0, 0)) for s in acc_shapes],
        out_shape=[jax.ShapeDtypeStruct((rows, width), BF16)] + [jax.ShapeDtypeStruct(s, F32) for s in acc_shapes],
        scratch_shapes=[pltpu.VMEM((SUBLANES, width), F32)],
        compiler_params=_cparams(("arbitrary",), 56),
        name="rwkv_prep_bwd",
    )(z, z, *cts, mu, *pars)


def _post_fn(y, r, k, v, g, ln_w, ln_b, r_k, esum):
    mean = _head_sums(y, esum) * (1.0 / HEAD_DIM)
    yc = y - mean
    var = _head_sums(yc * yc, esum) * (1.0 / HEAD_DIM)
    yn = yc * lax.rsqrt(var + GN_EPS) * ln_w + ln_b
    bonus = _head_sums(r * k * r_k, esum) * v
    return (yn + bonus) * g


def _post_fwd(y, r, k, v, g, pars):
    return _rowwise(lambda *a: (_post_fn(*a),), [y, r, k, v, g], pars, [(RWKV_WIDTH, BF16)], [],
                    tile=256, name="rwkv_post_fwd")[0]


def _post_bwd(y, r, k, v, g, dout, pars):
    def fn(yv, rv, kv, vv, gv, dv_, ln_w, ln_b, r_k, esum):
        _, vjp = jax.vjp(lambda *a: _post_fn(*a, esum), yv, rv, kv, vv, gv, ln_w, ln_b, r_k)
        return vjp(dv_)

    return _rowwise(fn, [y, r, k, v, g, dout], pars, [(RWKV_WIDTH, F32)] * 5, [p.shape for p in pars[:-1]],
                    tile=256, name="rwkv_post_bwd")


def _tri_inverses(ms):
    n = ms[0].shape[0]
    row = lax.broadcasted_iota(jnp.int32, (n, n), 0)
    col = lax.broadcasted_iota(jnp.int32, (n, n), 1)
    eye = jnp.where(row == col, 1.0, 0.0)
    t_inv = [eye + m for m in ms]
    power = list(ms)
    for _ in range(int(math.log2(n // 2)) - 1):
        power = [_bdot_raw(p, p, NN) for p in power]
        t_inv = [t + _bdot_raw(t, p, NN) for t, p in zip(t_inv, power)]
    return t_inv


@jax.custom_vjp
def _tri_solve(ms, xs):
    return tuple(_bdot_raw(t, x, NN) for t, x in zip(_tri_inverses(ms), xs))


def _tri_solve_fwd(ms, xs):
    t_inv = _tri_inverses(ms)
    us = tuple(_bdot_raw(t, x, NN) for t, x in zip(t_inv, xs))
    return us, (tuple(t_inv), us)


def _tri_solve_bwd(res, dus):
    t_inv, us = res
    dxs = tuple(_bdot_raw(t, du, TN) for t, du in zip(t_inv, dus))
    dms = tuple(_bdot_raw(dx, u, NT) for dx, u in zip(dxs, us))
    return dms, dxs


_tri_solve.defvjp(_tri_solve_fwd, _tri_solve_bwd)


def _chunk_fn(ss, rs, lws, ks, vs, kks, als):
    c = rs[0].shape[0]
    n = 2 * c
    row = lax.broadcasted_iota(jnp.int32, (n, n), 0)
    col = lax.broadcasted_iota(jnp.int32, (n, n), 1)
    incl = (row % c) >= (col % c)
    strict = (row % c) > (col % c)
    lane = lax.broadcasted_iota(jnp.int32, (1, LANES), 1)
    m_lo = jnp.where(lane < HEAD_DIM, 1.0, 0.0)
    m_hi = 1.0 - m_lo

    def stack(a):
        return jnp.concatenate([a * m_lo, a * m_hi], axis=0)

    cums = [_cumsum_rows(lw) for lw in lws]
    totals = [jnp.sum(lw, axis=0, keepdims=True) for lw in lws]
    bs = [kk * al for kk, al in zip(kks, als)]
    grows = [jnp.exp(-cum) for cum in cums]
    a_s = [stack(-kk * jnp.exp(cum - lw)) for kk, cum, lw in zip(kks, cums, lws)]
    b_s = [stack(b * g) for b, g in zip(bs, grows)]
    k_s = [stack(k * g) for k, g in zip(ks, grows)]
    r_s = [stack(r * jnp.exp(cum)) for r, cum in zip(rs, cums)]
    v_s = [stack(v) for v in vs]
    m_ab = [jnp.where(strict, _bdot(a, b, NT), 0.0) for a, b in zip(a_s, b_s)]
    m_ak = [jnp.where(strict, _bdot(a, k, NT), 0.0) for a, k in zip(a_s, k_s)]
    m_rb = [jnp.where(incl, _bdot(r, b, NT), 0.0) for r, b in zip(r_s, b_s)]
    m_rk = [jnp.where(incl, _bdot(r, k, NT), 0.0) for r, k in zip(r_s, k_s)]
    x = [_bdot(a, s, NT) + _bdot(m, v) for a, s, m, v in zip(a_s, ss, m_ak, v_s)]
    u = _tri_solve(tuple(m_ab), tuple(x))
    y = [_fold_rows(_bdot(r, s, NT) + _bdot(mb, uu) + _bdot(mk, v))
         for r, s, mb, uu, mk, v in zip(r_s, ss, m_rb, u, m_rk, v_s)]
    tails = [jnp.exp(tot - cum) for tot, cum in zip(totals, cums)]
    s_new = [s * jnp.exp(tot) + _bdot(uu, stack(b * tl), TN) + _bdot(v, stack(k * tl), TN)
             for s, tot, uu, b, tl, v, k in zip(ss, totals, u, bs, tails, v_s, ks)]
    return tuple(y), tuple(s_new)


def _chains(bsz, npair):
    return [(b, p, slice(p * LANES, (p + 1) * LANES)) for b in range(bsz) for p in range(npair)]


def _rwkv_fwd(r, lw, k, v, kk, al):
    bsz, t, w = r.shape
    npair, nchunk = w // LANES, t // CHUNK
    chains = _chains(bsz, npair)

    def body(r_ref, lw_ref, k_ref, v_ref, kk_ref, al_ref, y_ref, sall_ref, s_scr):
        @pl.when(pl.program_id(0) == 0)
        def _():
            s_scr[...] = jnp.zeros_like(s_scr)

        ss = tuple(s_scr[i] for i in range(len(chains)))
        for i, s in enumerate(ss):
            sall_ref[0, i] = s
        ys, s_new = _chunk_fn(ss, *[tuple(ref[b, :, cols] for b, _, cols in chains)
                                    for ref in (r_ref, lw_ref, k_ref, v_ref, kk_ref, al_ref)])
        for i, (b, _, cols) in enumerate(chains):
            y_ref[b, :, cols] = ys[i]
            s_scr[i] = s_new[i]

    spec = pl.BlockSpec((bsz, CHUNK, w), lambda c: (0, c, 0))
    return pl.pallas_call(
        body,
        grid=(nchunk,),
        in_specs=[spec] * 6,
        out_specs=[spec, pl.BlockSpec((1, len(chains), LANES, LANES), lambda c: (c, 0, 0, 0))],
        out_shape=[jax.ShapeDtypeStruct((bsz, t, w), F32),
                   jax.ShapeDtypeStruct((nchunk, len(chains), LANES, LANES), F32)],
        scratch_shapes=[pltpu.VMEM((len(chains), LANES, LANES), F32)],
        compiler_params=_cparams(("arbitrary",), 48),
        name="rwkv_chunk_fwd",
    )(r, lw, k, v, kk, al)


def _rwkv_bwd(r, lw, k, v, kk, al, s_all, dy):
    bsz, t, w = r.shape
    npair, nchunk = w // LANES, t // CHUNK
    chains = _chains(bsz, npair)

    def body(r_ref, lw_ref, k_ref, v_ref, kk_ref, al_ref, s_ref, dy_ref, *rest):
        out_refs, ds_scr = rest[:6], rest[6]

        @pl.when(pl.program_id(0) == 0)
        def _():
            ds_scr[...] = jnp.zeros_like(ds_scr)

        ss = tuple(s_ref[0, i] for i in range(len(chains)))
        _, vjp = jax.vjp(_chunk_fn, ss, *[tuple(ref[b, :, cols] for b, _, cols in chains)
                                          for ref in (r_ref, lw_ref, k_ref, v_ref, kk_ref, al_ref)])
        grads = vjp((tuple(dy_ref[b, :, cols] for b, _, cols in chains),
                     tuple(ds_scr[i] for i in range(len(chains)))))
        for i, (b, _, cols) in enumerate(chains):
            ds_scr[i] = grads[0][i]
            for ref, gval in zip(out_refs, grads[1:]):
                ref[b, :, cols] = gval[i]

    spec = pl.BlockSpec((bsz, CHUNK, w), lambda c: (0, nchunk - 1 - c, 0))
    sspec = pl.BlockSpec((1, len(chains), LANES, LANES), lambda c: (nchunk - 1 - c, 0, 0, 0))
    return pl.pallas_call(
        body,
        grid=(nchunk,),
        in_specs=[spec] * 6 + [sspec, spec],
        out_specs=[spec] * 6,
        out_shape=[jax.ShapeDtypeStruct((bsz, t, w), F32)] * 6,
        scratch_shapes=[pltpu.VMEM((len(chains), LANES, LANES), F32)],
        compiler_params=_cparams(("arbitrary",), 48),
        name="rwkv_chunk_bwd",
    )(r, lw, k, v, kk, al, s_all, dy)


def _alibi_slope(head):
    return 2.0 ** (-8.0 * (head + 1) / N_ATTN_HEADS)


def _attn_block(qs, kp, kc, vp, vc, sinks, first):
    row = lax.broadcasted_iota(jnp.int32, (BLOCK, BLOCK), 0)
    col = lax.broadcasted_iota(jnp.int32, (BLOCK, BLOCK), 1)
    lane = lax.broadcasted_iota(jnp.int32, (1, LANES), 1)
    halves = [jnp.where((lane // HEAD_DIM) == half, 1.0, 0.0) for half in range(2)]
    swap = jnp.where((row + HEAD_DIM) % LANES == col, 1.0, 0.0)
    dist_c = (row - col).astype(F32)
    dist_p = dist_c + float(BLOCK)
    valid_c = row >= col
    valid_p = jnp.logical_and(col > row, jnp.logical_not(first))
    scale = HEAD_DIM ** -0.5
    stored = (kp, kc, vp, vc)
    swapped = tuple(_bdot_nn(t, swap) for t in stored)
    heads = [(pair, half) for pair in range(len(qs)) for half in range(2)]
    kv = [stored if half == pair // 2 else swapped for pair, half in heads]
    slopes = [_alibi_slope(2 * pair + half) for pair, half in heads]
    qa = [qs[pair] * halves[half] for pair, half in heads]
    sp = [jnp.where(valid_p, _bdot_nt(q, t[0]) * scale - sl * dist_p, NEG_INF) for q, t, sl in zip(qa, kv, slopes)]
    sc = [jnp.where(valid_c, _bdot_nt(q, t[1]) * scale - sl * dist_c, NEG_INF) for q, t, sl in zip(qa, kv, slopes)]
    mx = [lax.stop_gradient(jnp.maximum(jnp.maximum(jnp.max(a, axis=-1, keepdims=True),
                                                    jnp.max(b, axis=-1, keepdims=True)), sk))
          for a, b, sk in zip(sp, sc, sinks)]
    ep = [jnp.exp(a - m) for a, m in zip(sp, mx)]
    ec = [jnp.exp(b - m) for b, m in zip(sc, mx)]
    es = [jnp.exp(sk - m) for sk, m in zip(sinks, mx)]
    inv = [1.0 / (jnp.sum(a, axis=-1, keepdims=True) + jnp.sum(b, axis=-1, keepdims=True) + s)
           for a, b, s in zip(ep, ec, es)]
    o = [_bdot_nn(a * i, t[2]) + _bdot_nn(b * i, t[3]) for a, b, i, t in zip(ep, ec, inv, kv)]
    outs = tuple(o[2 * pair] * halves[0] + o[2 * pair + 1] * halves[1] for pair in range(len(qs)))
    return outs, [lax.stop_gradient(s * i) for s, i in zip(es, inv)]


def _sink_values(sink_ref):
    return [jnp.max(sink_ref[h:h + 1, :], axis=-1, keepdims=True) for h in range(N_ATTN_HEADS)]


def _attn_fwd(z, sink_rows):
    bsz, t, _ = z.shape
    nb = t // BLOCK
    npair = ATTN_WIDTH // LANES

    def body(q_ref, kp_ref, kc_ref, vp_ref, vc_ref, sink_ref, o_ref):
        first = pl.program_id(1) == 0
        qs = tuple(q_ref[0, :, pair * LANES:(pair + 1) * LANES] for pair in range(npair))
        outs, _ = _attn_block(qs, kp_ref[0], kc_ref[0], vp_ref[0], vc_ref[0], _sink_values(sink_ref), first)
        for pair in range(npair):
            o_ref[0, :, pair * LANES:(pair + 1) * LANES] = outs[pair].astype(o_ref.dtype)

    kcol, vcol = ATTN_WIDTH // KV_WIDTH, ATTN_WIDTH // KV_WIDTH + 1
    return pl.pallas_call(
        body,
        grid=(bsz, nb),
        in_specs=[pl.BlockSpec((1, BLOCK, ATTN_WIDTH), lambda b, n: (b, n, 0)),
                  pl.BlockSpec((1, BLOCK, KV_WIDTH), lambda b, n: (b, jnp.maximum(n - 1, 0), kcol)),
                  pl.BlockSpec((1, BLOCK, KV_WIDTH), lambda b, n: (b, n, kcol)),
                  pl.BlockSpec((1, BLOCK, KV_WIDTH), lambda b, n: (b, jnp.maximum(n - 1, 0), vcol)),
                  pl.BlockSpec((1, BLOCK, KV_WIDTH), lambda b, n: (b, n, vcol)),
                  pl.BlockSpec(sink_rows.shape, lambda b, n: (0, 0))],
        out_specs=pl.BlockSpec((1, BLOCK, ATTN_WIDTH), lambda b, n: (b, n, 0)),
        out_shape=jax.ShapeDtypeStruct((bsz, t, ATTN_WIDTH), BF16),
        compiler_params=_cparams(("parallel", "arbitrary"), 48),
        name="swa_fwd",
    )(z, z, z, z, z, sink_rows)


def _attn_bwd(z, dout, sink_rows):
    bsz, t, _ = z.shape
    nb = t // BLOCK
    npair = ATTN_WIDTH // LANES

    def body(q_ref, kp_ref, kc_ref, vp_ref, vc_ref, do_ref, sink_ref, dz_ref, dsink_ref, carry):
        step = pl.program_id(1)
        n = nb - 1 - step
        first = n == 0

        @pl.when(step == 0)
        def _():
            carry[...] = jnp.zeros_like(carry)

        @pl.when(jnp.logical_and(step == 0, pl.program_id(0) == 0))
        def _():
            dsink_ref[...] = jnp.zeros_like(dsink_ref)

        lane = lax.broadcasted_iota(jnp.int32, (1, LANES), 1)
        qs = tuple(q_ref[0, :, pair * LANES:(pair + 1) * LANES] for pair in range(npair))
        dos = tuple(do_ref[0, :, pair * LANES:(pair + 1) * LANES] for pair in range(npair))
        fn = functools.partial(_attn_block, sinks=_sink_values(sink_ref), first=first)
        outs, vjp, psinks = jax.vjp(fn, qs, kp_ref[0], kc_ref[0], vp_ref[0], vc_ref[0], has_aux=True)
        dqs, dkp, dkc, dvp, dvc = vjp(dos)
        for pair in range(npair):
            dz_ref[0, :, pair * LANES:(pair + 1) * LANES] = dqs[pair].astype(dz_ref.dtype)
            for half in range(2):
                m = jnp.where((lane // HEAD_DIM) == half, 1.0, 0.0)
                delta = jnp.sum(dos[pair] * outs[pair] * m, axis=-1, keepdims=True)
                head = 2 * pair + half
                ds = -jnp.sum(psinks[head] * delta, axis=0, keepdims=True)
                dsink_ref[head:head + 1, :] += jnp.broadcast_to(ds, (1, LANES))
        dz_ref[0, :, ATTN_WIDTH:ATTN_WIDTH + KV_WIDTH] = (dkc + carry[0]).astype(dz_ref.dtype)
        dz_ref[0, :, ATTN_WIDTH + KV_WIDTH:QKV_WIDTH] = (dvc + carry[1]).astype(dz_ref.dtype)
        carry[0] = dkp
        carry[1] = dvp

    kcol, vcol = ATTN_WIDTH // KV_WIDTH, ATTN_WIDTH // KV_WIDTH + 1
    rev = lambda n: nb - 1 - n
    return pl.pallas_call(
        body,
        grid=(bsz, nb),
        in_specs=[pl.BlockSpec((1, BLOCK, ATTN_WIDTH), lambda b, n: (b, rev(n), 0)),
                  pl.BlockSpec((1, BLOCK, KV_WIDTH), lambda b, n: (b, jnp.maximum(rev(n) - 1, 0), kcol)),
                  pl.BlockSpec((1, BLOCK, KV_WIDTH), lambda b, n: (b, rev(n), kcol)),
                  pl.BlockSpec((1, BLOCK, KV_WIDTH), lambda b, n: (b, jnp.maximum(rev(n) - 1, 0), vcol)),
                  pl.BlockSpec((1, BLOCK, KV_WIDTH), lambda b, n: (b, rev(n), vcol)),
                  pl.BlockSpec((1, BLOCK, ATTN_WIDTH), lambda b, n: (b, rev(n), 0)),
                  pl.BlockSpec(sink_rows.shape, lambda b, n: (0, 0))],
        out_specs=[pl.BlockSpec((1, BLOCK, QKV_WIDTH), lambda b, n: (b, rev(n), 0)),
                   pl.BlockSpec((N_ATTN_HEADS, LANES), lambda b, n: (0, 0))],
        out_shape=[jax.ShapeDtypeStruct((bsz, t, QKV_WIDTH), BF16),
                   jax.ShapeDtypeStruct((N_ATTN_HEADS, LANES), F32)],
        scratch_shapes=[pltpu.VMEM((2, BLOCK, KV_WIDTH), F32)],
        compiler_params=_cparams(("arbitrary", "arbitrary"), 48),
        name="swa_bwd",
    )(z, z, z, z, z, dout, sink_rows)


def _exchange(arrays, *, scatter, name):
    n = len(arrays)
    out_shape = [jax.ShapeDtypeStruct((N_DEV,) + (a.shape[1:] if scatter else a.shape), a.dtype) for a in arrays]

    def body(*refs):
        ins, outs = refs[:n], refs[n:2 * n]
        send_sems, recv_sems, local_sems = refs[2 * n:]
        x, y, c = lax.axis_index("x"), lax.axis_index("y"), lax.axis_index("c")
        me = 4 * x + 2 * y + c
        pending = []
        for i in range(n):
            own = pltpu.make_async_copy(ins[i].at[me] if scatter else ins[i], outs[i].at[me], local_sems.at[i])
            own.start()
            pending.append((own, None))
            for d in range(1, N_DEV):
                px = 1 - x if d & 4 else x
                py = 1 - y if d & 2 else y
                pc = 1 - c if d & 1 else c
                peer = 4 * px + 2 * py + pc
                src = ins[i].at[peer] if scatter else ins[i]
                send = pltpu.make_async_remote_copy(src, outs[i].at[me], send_sems.at[i, d - 1], recv_sems.at[i, d - 1],
                                                    device_id=(px, py, pc), device_id_type=MESH)
                send.start()
                recv = pltpu.make_async_remote_copy(src, outs[i].at[peer], send_sems.at[i, d - 1], recv_sems.at[i, d - 1],
                                                    device_id=(px, py, pc), device_id_type=MESH)
                pending.append((send, recv))
        for send, recv in pending:
            if recv is None:
                send.wait()
            else:
                send.wait_send()
                recv.wait_recv()

    hbm = pl.BlockSpec(memory_space=pltpu.HBM)
    return pl.pallas_call(
        body,
        in_specs=[hbm] * n,
        out_specs=[hbm] * n,
        out_shape=out_shape,
        scratch_shapes=[pltpu.SemaphoreType.DMA((n, N_DEV - 1)), pltpu.SemaphoreType.DMA((n, N_DEV - 1)),
                        pltpu.SemaphoreType.DMA((n,))],
        compiler_params=pltpu.CompilerParams(has_side_effects=True),
        name=name,
    )(*arrays)


def _adamw(parts, w, m, v, name):
    rows, cols = w.shape
    tr = _pick(rows, (256, 128, 64, 8))
    c1 = 1.0 / (1.0 - ADAM_B1 ** ADAM_STEP)
    c2 = 1.0 / (1.0 - ADAM_B2 ** ADAM_STEP)

    def body(p_ref, w_ref, m_ref, v_ref, g_ref, d_ref, mo_ref, vo_ref):
        g = p_ref[0].astype(F32)
        for s in range(1, N_DEV):
            g = g + p_ref[s].astype(F32)
        mn = ADAM_B1 * m_ref[...] + (1.0 - ADAM_B1) * g
        vn = ADAM_B2 * v_ref[...] + (1.0 - ADAM_B2) * (g * g)
        g_ref[...] = g
        mo_ref[...] = mn
        vo_ref[...] = vn
        d_ref[...] = -ADAM_LR * ((mn * c1) / (jnp.sqrt(vn * c2) + ADAM_EPS) + ADAM_WD * w_ref[...])

    spec = pl.BlockSpec((tr, cols), lambda i: (i, 0))
    return pl.pallas_call(
        body,
        grid=(rows // tr,),
        in_specs=[pl.BlockSpec((N_DEV, tr, cols), lambda i: (0, i, 0)), spec, spec, spec],
        out_specs=[spec] * 4,
        out_shape=[jax.ShapeDtypeStruct((rows, cols), F32)] * 4,
        compiler_params=_cparams(("parallel",), 48),
        name=name,
    )(parts, w, m, v)


_VECTOR_PARAMS = ("attn_norm_g", "attn_sinks", "rwkv_mu", "w0", "a0", "k_k", "k_a", "r_k", "ln_x_w", "ln_x_b",
                  "mlp_norm_g", "final_norm_g")
_WEIGHT_NAMES = ("attn_norm_g", "w_in", "attn_sinks", "rwkv_mu", "w0", "w2", "a0", "a2", "g2", "k_k", "k_a", "r_k",
                 "ln_x_w", "ln_x_b", "w_out", "mlp_norm_g", "w_up", "w_down", "final_norm_g")


def _pack_vectors(vals):
    pieces = []
    for name in _VECTOR_PARAMS:
        flat = vals[name].reshape(1, -1)
        pad = (-flat.shape[1]) % LANES
        pieces.append(jnp.pad(flat, ((0, 0), (0, pad))) if pad else flat)
    return jnp.concatenate(pieces, axis=1)


def _unpack_vectors(packed, like):
    out, col = {}, 0
    for name in _VECTOR_PARAMS:
        size = like[name].size
        out[name] = packed[0, col:col + size].reshape(like[name].shape)
        col += size + (-size) % LANES
    return out


def kernel(x, attn_norm_g, w_in, attn_sinks, rwkv_mu, w0, w2, a0, a2, g2, k_k, k_a, r_k, ln_x_w, ln_x_b, w_out, mlp_norm_g, w_up, w_down, final_norm_g, loss_target, m_attn_norm_g, m_w_in, m_attn_sinks, m_rwkv_mu, m_w0, m_w2, m_a0, m_a2, m_g2, m_k_k, m_k_a, m_r_k, m_ln_x_w, m_ln_x_b, m_w_out, m_mlp_norm_g, m_w_up, m_w_down, m_final_norm_g, v_attn_norm_g, v_w_in, v_attn_sinks, v_rwkv_mu, v_w0, v_w2, v_a0, v_a2, v_g2, v_k_k, v_k_a, v_r_k, v_ln_x_w, v_ln_x_b, v_w_out, v_mlp_norm_g, v_w_up, v_w_down, v_final_norm_g):
    weights = dict(attn_norm_g=attn_norm_g, w_in=w_in, attn_sinks=attn_sinks, rwkv_mu=rwkv_mu, w0=w0, w2=w2, a0=a0,
                   a2=a2, g2=g2, k_k=k_k, k_a=k_a, r_k=r_k, ln_x_w=ln_x_w, ln_x_b=ln_x_b, w_out=w_out,
                   mlp_norm_g=mlp_norm_g, w_up=w_up, w_down=w_down, final_norm_g=final_norm_g)
    mom1 = dict(attn_norm_g=m_attn_norm_g, w_in=m_w_in, attn_sinks=m_attn_sinks, rwkv_mu=m_rwkv_mu, w0=m_w0, w2=m_w2,
                a0=m_a0, a2=m_a2, g2=m_g2, k_k=m_k_k, k_a=m_k_a, r_k=m_r_k, ln_x_w=m_ln_x_w, ln_x_b=m_ln_x_b,
                w_out=m_w_out, mlp_norm_g=m_mlp_norm_g, w_up=m_w_up, w_down=m_w_down, final_norm_g=m_final_norm_g)
    mom2 = dict(attn_norm_g=v_attn_norm_g, w_in=v_w_in, attn_sinks=v_attn_sinks, rwkv_mu=v_rwkv_mu, w0=v_w0, w2=v_w2,
                a0=v_a0, a2=v_a2, g2=v_g2, k_k=v_k_k, k_a=v_k_a, r_k=v_r_k, ln_x_w=v_ln_x_w, ln_x_b=v_ln_x_b,
                w_out=v_w_out, mlp_norm_g=v_mlp_norm_g, w_up=v_w_up, w_down=v_w_down, final_norm_g=v_final_norm_g)
    bsz, seq, d_model = x.shape
    rows = bsz * seq
    d_in = N_DEV * w_in.shape[2]
    d_ff = N_DEV * w_up.shape[2]

    gathered = _exchange([w_in[0].astype(BF16), w2[0], a2[0], g2[0], w_out[0].astype(BF16), w_up[0].astype(BF16),
                          w_down[0].astype(BF16)], scatter=False, name="gather_weights")
    cols_first = lambda a: a.transpose(1, 0, 2).reshape(a.shape[1], -1)
    w_in_f = cols_first(gathered[0])
    w_attn, w_rw = w_in_f[:, :QKV_WIDTH], w_in_f[:, QKV_WIDTH:]
    w2_f, a2_f, g2_f = cols_first(gathered[1]), cols_first(gathered[2]), cols_first(gathered[3])
    lora = w2_f.shape[0]
    w2p = jnp.concatenate([w2_f, jnp.zeros_like(a2_f)], axis=0)
    a2p = jnp.concatenate([jnp.zeros_like(w2_f), a2_f], axis=0)
    w_out_f = gathered[4].reshape(-1, d_model)
    w_up_f = cols_first(gathered[5])
    w_down_f = gathered[6].reshape(-1, d_model)

    esum = _head_sum_matrix()
    sink_rows = jnp.broadcast_to(attn_sinks.reshape(N_ATTN_HEADS, 1), (N_ATTN_HEADS, LANES))
    prep_pars = [w0, w2p, a0, a2p, g2_f, k_k, k_a, esum]
    post_pars = [ln_x_w, ln_x_b, r_k, esum]

    x2d = x.reshape(rows, d_model)
    h1 = _norm_fwd(x2d, attn_norm_g, "attn_norm_fwd")
    z_attn = _matmul(h1, w_attn, "nn", name="in_proj_attn")
    z_rw = _matmul(h1, w_rw, "nn", name="in_proj_rwkv")
    z_attn3 = z_attn.reshape(bsz, seq, QKV_WIDTH)
    attn_out = _attn_fwd(z_attn3, sink_rows)
    r, lw, k, v, kk, al, gate = _prep_fwd(z_rw, seq, rwkv_mu, prep_pars)
    as3 = lambda a: a.reshape(bsz, seq, RWKV_WIDTH)
    y, s_all = _rwkv_fwd(as3(r), as3(lw), as3(k), as3(v), as3(kk), as3(al))
    y2 = y.reshape(rows, RWKV_WIDTH)
    rw_out = _post_fwd(y2, r, k, v, gate, post_pars)
    mix = jnp.concatenate([attn_out.reshape(rows, ATTN_WIDTH), rw_out], axis=1)
    residual = lambda acc, res: (acc + res,)
    x1 = _matmul(mix, w_out_f, "nn", name="out_proj", extras=(x2d,), epilogue=residual)
    h2 = _norm_fwd(x1, mlp_norm_g, "mlp_norm_fwd")

    def relu_sq(acc):
        pos = jnp.maximum(acc, 0.0)
        return acc, pos * pos

    u, act = _matmul(h2, w_up_f, "nn", name="mlp_up", epilogue=relu_sq, out_dtypes=(BF16, BF16))
    x2 = _matmul(act, w_down_f, "nn", name="mlp_down", extras=(x1,), epilogue=residual)
    dx2, loss_vec, g_final = _final_loss(x2, loss_target.reshape(rows, d_model), final_norm_g.reshape(1, d_model))

    g_w_down = _matmul(act, dx2, "tn", name="grad_w_down", out_dtypes=(BF16,))
    du = _matmul(dx2, w_down_f, "nt", name="mlp_down_bwd", extras=(u,), out_dtypes=(BF16,),
                 epilogue=lambda acc, uv: (acc * (2.0 * jnp.maximum(uv.astype(F32), 0.0)),))
    g_w_up = _matmul(h2, du, "tn", name="grad_w_up", out_dtypes=(BF16,))
    dh2 = _matmul(du, w_up_f, "nt", name="mlp_up_bwd")
    dx1, g_mlp_norm = _norm_bwd(x1, dh2, dx2, mlp_norm_g, "mlp_norm_bwd")
    g_w_out = _matmul(mix, dx1, "tn", name="grad_w_out", out_dtypes=(BF16,))
    dmix = _matmul(dx1, w_out_f, "nt", name="out_proj_bwd")
    dy, dr_a, dk_a, dv_a, dgate, g_ln_w, g_ln_b, g_r_k = _post_bwd(
        y2, r, k, v, gate, (dmix, RWKV_WIDTH, ATTN_WIDTH // RWKV_WIDTH), post_pars)
    dr_b, dlw, dk_b, dv_b, dkk, dal = _rwkv_bwd(as3(r), as3(lw), as3(k), as3(v), as3(kk), as3(al), s_all, as3(dy))
    flat = lambda a: a.reshape(rows, RWKV_WIDTH)
    (dz_rw, gmu_r, gmu_k, gmu_v, gmu_wa, gmu_g, g_w0, g_w2p, g_a0, g_a2p, g_g2, g_k_k, g_k_a) = _prep_bwd(
        z_rw, seq, [dr_a, flat(dr_b), flat(dlw), dk_a, flat(dk_b), dv_a, flat(dv_b), flat(dkk), flat(dal), dgate],
        rwkv_mu, prep_pars)
    dz_attn, g_sink_rows = _attn_bwd(z_attn3, dmix.reshape(bsz, seq, d_model), sink_rows)
    dz_attn = dz_attn.reshape(rows, QKV_WIDTH)
    g_w_in = jnp.concatenate([_matmul(h1, dz_attn, "tn", name="grad_w_in_attn", out_dtypes=(BF16,)),
                              _matmul(h1, dz_rw, "tn", name="grad_w_in_rwkv", out_dtypes=(BF16,))], axis=1)
    dh1 = _matmul(dz_attn, w_attn, "nt", name="in_proj_attn_bwd")
    dh1 = _matmul(dz_rw, w_rw, "nt", name="in_proj_rwkv_bwd", extras=(dh1,), epilogue=residual)
    dx, g_attn_norm = _norm_bwd(x2d, dh1, dx1, attn_norm_g, "attn_norm_bwd")

    by_cols = lambda a: a.reshape(a.shape[0], N_DEV, -1).transpose(1, 0, 2)
    lora_grads = jnp.concatenate([g_w2p[:lora], g_a2p[lora:], g_g2], axis=0)
    parts = _exchange([by_cols(g_w_in), by_cols(lora_grads), g_w_out.reshape(N_DEV, -1, d_model), by_cols(g_w_up),
                       g_w_down.reshape(N_DEV, -1, d_model)], scatter=True, name="scatter_grads")
    vec_grads = dict(attn_norm_g=g_attn_norm, attn_sinks=g_sink_rows[:, 0], rwkv_mu=jnp.concatenate(
        [gmu_r, gmu_k, gmu_v, gmu_wa, gmu_g], axis=1), w0=g_w0, a0=g_a0, k_k=g_k_k, k_a=g_k_a, r_k=g_r_k,
        ln_x_w=g_ln_w, ln_x_b=g_ln_b, mlp_norm_g=g_mlp_norm, final_norm_g=g_final)
    vec_parts = _exchange([_pack_vectors(vec_grads)], scatter=False, name="gather_vector_grads")[0]

    grads, delta, new_m, new_v = {}, {}, {}, {}

    def update(name, part, shape2d):
        res = _adamw(part, weights[name].reshape(shape2d), mom1[name].reshape(shape2d), mom2[name].reshape(shape2d),
                     "adamw_" + name)
        for store, val in zip((grads, delta, new_m, new_v), res):
            store[name] = val.reshape(weights[name].shape)

    update("w_in", parts[0], w_in.shape[1:])
    update("w_out", parts[2], w_out.shape[1:])
    update("w_up", parts[3], w_up.shape[1:])
    update("w_down", parts[4], w_down.shape[1:])
    stack = lambda d: jnp.concatenate([d["w2"][0], d["a2"][0], d["g2"][0]], axis=0)
    lora_res = _adamw(parts[1], stack(weights), stack(mom1), stack(mom2), "adamw_lora")
    for store, val in zip((grads, delta, new_m, new_v), lora_res):
        store["w2"], store["a2"], store["g2"] = val[None, :lora], val[None, lora:2 * lora], val[None, 2 * lora:]
    vec_res = _adamw(vec_parts, _pack_vectors(weights), _pack_vectors(mom1), _pack_vectors(mom2), "adamw_vectors")
    for store, val in zip((grads, delta, new_m, new_v), vec_res):
        store.update(_unpack_vectors(val, weights))

    loss = lax.psum(loss_vec[0, 0], ("x", "y", "c"))
    return (loss, dx.reshape(x.shape), *[grads[n] for n in _WEIGHT_NAMES], *[delta[n] for n in _WEIGHT_NAMES],
            *[new_m[n] for n in _WEIGHT_NAMES], *[new_v[n] for n in _WEIGHT_NAMES])
```

```python
import functools
import math

import jax
import jax.numpy as jnp
from jax import lax
from jax.experimental import pallas as pl
from jax.experimental.pallas import tpu as pltpu

F32 = jnp.float32
BF16 = jnp.bfloat16

N_DEV = 8
HEAD_DIM = 64
LANES = 128
N_ATTN_HEADS = 8
ATTN_WIDTH = 512
KV_WIDTH = 128
QKV_WIDTH = ATTN_WIDTH + 2 * KV_WIDTH
RWKV_WIDTH = 512
LORA_WA = 128
GATE_LORA = 128
RWKV_SHIFT_WIDTH = 3 * RWKV_WIDTH + LORA_WA + GATE_LORA
BLOCK = 128
CHUNK = 64
RMS_EPS = 1e-6
GN_EPS = 64e-5
L2_EPS = 1e-12
NEG_INF = -1e30
DECAY_SCALE = math.exp(-0.5)
ADAM_LR, ADAM_B1, ADAM_B2, ADAM_EPS, ADAM_WD, ADAM_STEP = 0.001, 0.9, 0.999, 1e-08, 0.01, 10

NN = (((1,), (0,)), ((), ()))
NT = (((1,), (1,)), ((), ()))
TN = (((0,), (0,)), ((), ()))
MESH = pl.DeviceIdType.MESH


def _dot(a, b, dn=NN, precision=None):
    return lax.dot_general(a, b, dn, precision=precision, preferred_element_type=F32)


def _bdot_raw(a, b, dn):
    return lax.dot_general(a.astype(BF16), b.astype(BF16), dn, preferred_element_type=F32)


@functools.partial(jax.custom_vjp, nondiff_argnums=(2, 3))
def _bdot_c(a, b, ca, cb):
    return _bdot_raw(a, b, (((ca,), (cb,)), ((), ())))


def _bdot_c_fwd(a, b, ca, cb):
    return _bdot_c(a, b, ca, cb), (a, b)


def _bdot_c_bwd(ca, cb, res, ct):
    a, b = res
    fa, fb = 1 - ca, 1 - cb
    da = _bdot_raw(ct, b, (((1,), (fb,)), ((), ()))) if ca == 1 else _bdot_raw(b, ct, (((fb,), (1,)), ((), ())))
    db = _bdot_raw(a, ct, (((fa,), (0,)), ((), ()))) if cb == 0 else _bdot_raw(ct, a, (((0,), (fa,)), ((), ())))
    return da, db


_bdot_c.defvjp(_bdot_c_fwd, _bdot_c_bwd)


def _bdot(a, b, dn=NN):
    return _bdot_c(a, b, dn[0][0][0], dn[0][1][0])


def _bdot_nn(a, b):
    return _bdot(a, b, NN)


def _bdot_nt(a, b):
    return _bdot(a, b, NT)


def _split3(x):
    hi = x.astype(BF16)
    rest = x - hi.astype(F32)
    mid = rest.astype(BF16)
    return hi, mid, (rest - mid.astype(F32)).astype(BF16)


def _running_sum(x, dn):
    c = x.shape[0]
    row = lax.broadcasted_iota(jnp.int32, (c, c), 0)
    col = lax.broadcasted_iota(jnp.int32, (c, c), 1)
    tri = jnp.where(row >= col, 1.0, 0.0).astype(BF16)
    parts = [lax.dot_general(tri, p, dn, preferred_element_type=F32) for p in _split3(x)]
    return parts[0] + parts[1] + parts[2]


@jax.custom_vjp
def _cumsum_rows(x):
    return _running_sum(x, NN)


_cumsum_rows.defvjp(lambda x: (_running_sum(x, NN), None), lambda _, ct: (_running_sum(ct, TN),))


@jax.custom_vjp
def _fold_rows(x):
    c = x.shape[0] // 2
    return x[:c] + x[c:]


_fold_rows.defvjp(lambda x: (_fold_rows(x), None), lambda _, ct: (jnp.concatenate([ct, ct], axis=0),))


def _sigmoid(x):
    return 1.0 / (1.0 + jnp.exp(-x))


def _pick(n, cands):
    for c in cands:
        if n % c == 0:
            return c
    return n


def _cparams(sem, vmem_mb=None):
    kw = dict(dimension_semantics=sem)
    if vmem_mb is not None:
        kw["vmem_limit_bytes"] = vmem_mb * 1024 * 1024
    return pltpu.CompilerParams(**kw)


def _matmul(a, b, mode, *, name, extras=(), epilogue=None, out_dtypes=(F32,), tm=1024, tn=1024, tk=1024):
    if mode == "nn":
        (M, K), (_, N) = a.shape, b.shape
    elif mode == "tn":
        (K, M), (_, N) = a.shape, b.shape
    else:
        (M, K), (N, _) = a.shape, b.shape
    tm = _pick(M, (tm, 512, 256, 128))
    tn = _pick(N, (tn, 896, 768, 512, 384, 256, 128))
    tk = _pick(K, (tk, 512, 256, 128))
    nk = K // tk
    ne, nout = len(extras), len(out_dtypes)
    if mode == "nn":
        a_spec = pl.BlockSpec((tm, tk), lambda i, j, k: (i, k))
        b_spec = pl.BlockSpec((tk, tn), lambda i, j, k: (k, j))
        dn = NN
    elif mode == "tn":
        a_spec = pl.BlockSpec((tk, tm), lambda i, j, k: (k, i))
        b_spec = pl.BlockSpec((tk, tn), lambda i, j, k: (k, j))
        dn = TN
    else:
        a_spec = pl.BlockSpec((tm, tk), lambda i, j, k: (i, k))
        b_spec = pl.BlockSpec((tn, tk), lambda i, j, k: (j, k))
        dn = NT
    o_spec = pl.BlockSpec((tm, tn), lambda i, j, k: (i, j))

    def body(*refs):
        a_ref, b_ref = refs[:2]
        e_refs = refs[2:2 + ne]
        o_refs = refs[2 + ne:2 + ne + nout]
        acc = refs[-1]
        kstep = pl.program_id(2)

        @pl.when(kstep == 0)
        def _():
            acc[...] = jnp.zeros_like(acc)

        acc[...] += _bdot_raw(a_ref[...], b_ref[...], dn)

        @pl.when(kstep == nk - 1)
        def _():
            if epilogue is None:
                outs = (acc[...],)
            else:
                outs = epilogue(acc[...], *[e[...] for e in e_refs])
            for o_ref, o in zip(o_refs, outs):
                o_ref[...] = o.astype(o_ref.dtype)

    outs = pl.pallas_call(
        body,
        grid=(M // tm, N // tn, nk),
        in_specs=[a_spec, b_spec] + [o_spec] * ne,
        out_specs=[o_spec] * nout,
        out_shape=[jax.ShapeDtypeStruct((M, N), dt) for dt in out_dtypes],
        scratch_shapes=[pltpu.VMEM((tm, tn), F32)],
        compiler_params=_cparams(("parallel", "parallel", "arbitrary"), 56),
        name=name,
    )(a, b, *extras)
    return outs[0] if nout == 1 else outs


def _rowwise(fn, rows, pars, out_rows, out_accs, *, tile, name):
    rows = [r if isinstance(r, tuple) else (r, r.shape[1], 0) for r in rows]
    R = rows[0][0].shape[0]
    tile = min(tile, R)
    nr, npar, nor, noa = len(rows), len(pars), len(out_rows), len(out_accs)

    def body(*refs):
        rin = refs[:nr]
        pin = refs[nr:nr + npar]
        orow = refs[nr + npar:nr + npar + nor]
        oacc = refs[nr + npar + nor:]
        outs = fn(*[r[...] for r in rin], *[p[...] for p in pin])
        for ref, o in zip(orow, outs[:nor]):
            if isinstance(o, (tuple, list)):
                col = 0
                for piece in o:
                    ref[:, col:col + piece.shape[1]] = piece.astype(ref.dtype)
                    col += piece.shape[1]
            else:
                ref[...] = o.astype(ref.dtype)
        step = pl.program_id(0)

        def accumulate(ref, o):
            @pl.when(step == 0)
            def _():
                ref[...] = o

            @pl.when(step > 0)
            def _():
                ref[...] += o

        for ref, o in zip(oacc, outs[nor:]):
            accumulate(ref, o)

    def colspec(width, cb):
        return pl.BlockSpec((tile, width), lambda i: (i, cb))

    return pl.pallas_call(
        body,
        grid=(R // tile,),
        in_specs=[colspec(w, cb) for (_, w, cb) in rows]
        + [pl.BlockSpec(p.shape, lambda i: (0, 0)) for p in pars],
        out_specs=[colspec(w, 0) for (w, _) in out_rows]
        + [pl.BlockSpec(s, lambda i: (0, 0)) for s in out_accs],
        out_shape=[jax.ShapeDtypeStruct((R, w), dt) for (w, dt) in out_rows]
        + [jax.ShapeDtypeStruct(s, F32) for s in out_accs],
        compiler_params=_cparams(("arbitrary",), 56),
        name=name,
    )(*[r[0] for r in rows], *pars)


def _rms_fn(x, g):
    return x * lax.rsqrt(jnp.mean(x * x, axis=-1, keepdims=True) + RMS_EPS) * g


def _norm_fwd(x, g, name):
    return _rowwise(lambda xv, gv: (_rms_fn(xv, gv),), [x], [g], [(x.shape[1], BF16)], [], tile=512, name=name)[0]


def _norm_bwd(x, dh, dres, g, name):
    def fn(xv, dhv, dresv, gv):
        _, vjp = jax.vjp(_rms_fn, xv, gv)
        dx, dg = vjp(dhv)
        return dx + dresv, dg

    return _rowwise(fn, [x, dh, dres], [g], [(x.shape[1], F32)], [g.shape], tile=256, name=name)


def _final_loss(x, tgt, g):
    d = x.shape[1]

    def fn(xv, tv, gv):
        y, vjp = jax.vjp(_rms_fn, xv, gv)
        err = y - tv
        loss = 0.5 * jnp.sum(jnp.sum(err * err, axis=-1, keepdims=True), axis=0, keepdims=True) / d
        dx, dg = vjp(err / d)
        return dx, jnp.broadcast_to(loss, (1, LANES)), dg

    return _rowwise(fn, [x, tgt], [g], [(d, F32)], [(1, LANES), g.shape], tile=256, name="final_norm_loss")


def _head_sum_matrix():
    i = lax.broadcasted_iota(jnp.int32, (RWKV_WIDTH, RWKV_WIDTH), 0) // HEAD_DIM
    j = lax.broadcasted_iota(jnp.int32, (RWKV_WIDTH, RWKV_WIDTH), 1) // HEAD_DIM
    return (i == j).astype(BF16)


def _head_sums_raw(x, esum):
    hi = x.astype(BF16)
    lo = (x - hi.astype(F32)).astype(BF16)
    return _dot(hi, esum) + _dot(lo, esum)


@jax.custom_vjp
def _head_sums(x, esum):
    return _head_sums_raw(x, esum)


_head_sums.defvjp(lambda x, esum: (_head_sums_raw(x, esum), esum),
                  lambda esum, ct: (_head_sums_raw(ct, esum), jnp.zeros_like(esum)))


def _prep_core(xr, xk, xv, xwa, xg, w0, w2p, a0, a2p, g2, k_k, k_a, esum):
    lw = -DECAY_SCALE * _sigmoid(w0 + _bdot_nn(jnp.tanh(xwa), w2p))
    a = _sigmoid(a0 + _bdot_nn(xwa, a2p))
    g = _bdot_nn(_sigmoid(xg), g2)
    kk0 = xk * k_k
    kk = kk0 / jnp.maximum(jnp.sqrt(_head_sums(kk0 * kk0, esum)), L2_EPS)
    k = xk * (1.0 + (a - 1.0) * k_a)
    return xr, lw, k, xv, kk, a, g


_SEGS = ((0, 512), (512, 1024), (1024, 1536), (1536, 1664), (1664, 1792))


PREP_TILE = 256
SUBLANES = 8


def _shifted_tokens(z_ref, zprev_ref, tile_index, seq):
    zc = z_ref[...]
    start = (tile_index * PREP_TILE) % seq == 0
    before = jnp.where(start, 0.0, zprev_ref[SUBLANES - 1:SUBLANES, :])
    rowid = lax.broadcasted_iota(jnp.int32, zc.shape, 0)
    return zc, jnp.where(rowid == 0, before, pltpu.roll(zc, 1, 0))


def _prep_specs(z, mu, pars, index):
    width = z.shape[1]
    per = PREP_TILE // SUBLANES
    return ([pl.BlockSpec((PREP_TILE, width), lambda i: (index(i), 0)),
             pl.BlockSpec((SUBLANES, width), lambda i: (jnp.maximum(index(i) * per - 1, 0), 0))],
            [pl.BlockSpec(p.shape, lambda i: (0, 0)) for p in (mu, *pars)])


def _prep_fwd(z, seq, mu, pars):
    rows = z.shape[0]
    npar = len(pars)

    def body(z_ref, zprev_ref, mu_ref, *rest):
        par_refs, out_refs = rest[:npar], rest[npar:]
        zc, zp = _shifted_tokens(z_ref, zprev_ref, pl.program_id(0), seq)
        zs = zc + (zp - zc) * mu_ref[...]
        outs = _prep_core(*[zs[:, a:b] for a, b in _SEGS], *[p[...] for p in par_refs])
        for ref, o in zip(out_refs, outs):
            ref[...] = o

    zspecs, pspecs = _prep_specs(z, mu, pars, lambda i: i)
    return pl.pallas_call(
        body,
        grid=(rows // PREP_TILE,),
        in_specs=zspecs + pspecs,
        out_specs=[pl.BlockSpec((PREP_TILE, RWKV_WIDTH), lambda i: (i, 0))] * 7,
        out_shape=[jax.ShapeDtypeStruct((rows, RWKV_WIDTH), F32)] * 7,
        compiler_params=_cparams(("parallel",), 56),
        name="rwkv_prep_fwd",
    )(z, z, mu, *pars)


def _prep_bwd(z, seq, cts, mu, pars):
    rows, width = z.shape
    ntile = rows // PREP_TILE
    npar, nct = len(pars), len(cts)
    acc_shapes = [(1, b - a) for a, b in _SEGS] + [p.shape for p in pars[:-1]]

    def body(z_ref, zprev_ref, *rest):
        ct_refs = rest[:nct]
        mu_ref = rest[nct]
        par_refs = rest[nct + 1:nct + 1 + npar]
        dz_ref = rest[nct + 1 + npar]
        acc_refs = rest[nct + 2 + npar:-1]
        carry = rest[-1]
        step = pl.program_id(0)
        tile_index = ntile - 1 - step

        @pl.when(step == 0)
        def _():
            carry[...] = jnp.zeros_like(carry)

        zc, zp = _shifted_tokens(z_ref, zprev_ref, tile_index, seq)
        mu_v = mu_ref[...]
        diff = zp - zc
        zs = zc + diff * mu_v
        dra, drb, dlw, dka, dkb, dva, dvb, dkk, da, dg = [c[...] for c in ct_refs]
        pv = [p[...] for p in par_refs]
        _, vjp = jax.vjp(lambda *args: _prep_core(*args, pv[-1]), *[zs[:, a:b] for a, b in _SEGS], *pv[:-1])
        grads = vjp((dra + drb, dlw, dka + dkb, dva + dvb, dkk, da, dg))
        dsegs, dpars = grads[:5], grads[5:]
        last_of_sequence = ((tile_index + 1) * PREP_TILE) % seq == 0
        accs = []
        for ds, (a, b) in zip(dsegs, _SEGS):
            mu_s = mu_v[:, a:b]
            dzp = ds * mu_s
            after = jnp.where(last_of_sequence, 0.0, carry[0:1, a:b])
            rowid = lax.broadcasted_iota(jnp.int32, dzp.shape, 0)
            from_next = jnp.where(rowid == PREP_TILE - 1, after, pltpu.roll(dzp, PREP_TILE - 1, 0))
            dz_ref[:, a:b] = (ds * (1.0 - mu_s) + from_next).astype(dz_ref.dtype)
            carry[:, a:b] = dzp[0:SUBLANES, :]
            accs.append(jnp.sum(ds * diff[:, a:b], axis=0, keepdims=True))
        accs.extend(dpars)

        def accumulate(ref, o):
            @pl.when(step == 0)
            def _():
                ref[...] = o

            @pl.when(step > 0)
            def _():
                ref[...] += o

        for ref, o in zip(acc_refs, accs):
            accumulate(ref, o)

    rev = lambda i: ntile - 1 - i
    zspecs, pspecs = _prep_specs(z, mu, pars, rev)
    return pl.pallas_call(
        body,
        grid=(ntile,),
        in_specs=zspecs + [pl.BlockSpec((PREP_TILE, RWKV_WIDTH), lambda i: (rev(i), 0))] * nct + pspecs,
        out_specs=[pl.BlockSpec((PREP_TILE, width), lambda i: (rev(i), 0))]
        + [pl.BlockSpec(s, lambda i: (0, 0)) for s in acc_shapes],
        out_shape=[jax.ShapeDtypeStruct((rows, width), BF16)] + [jax.ShapeDtypeStruct(s, F32) for s in acc_shapes],
        scratch_shapes=[pltpu.VMEM((SUBLANES, width), F32)],
        compiler_params=_cparams(("arbitrary",), 56),
        name="rwkv_prep_bwd",
    )(z, z, *cts, mu, *pars)


def _post_fn(y, r, k, v, g, ln_w, ln_b, r_k, esum):
    mean = _head_sums(y, esum) * (1.0 / HEAD_DIM)
    yc = y - mean
    var = _head_sums(yc * yc, esum) * (1.0 / HEAD_DIM)
    yn = yc * lax.rsqrt(var + GN_EPS) * ln_w + ln_b
    bonus = _head_sums(r * k * r_k, esum) * v
    return (yn + bonus) * g


def _post_fwd(y, r, k, v, g, pars):
    return _rowwise(lambda *a: (_post_fn(*a),), [y, r, k, v, g], pars, [(RWKV_WIDTH, BF16)], [],
                    tile=256, name="rwkv_post_fwd")[0]


def _post_bwd(y, r, k, v, g, dout, pars):
    def fn(yv, rv, kv, vv, gv, dv_, ln_w, ln_b, r_k, esum):
        _, vjp = jax.vjp(lambda *a: _post_fn(*a, esum), yv, rv, kv, vv, gv, ln_w, ln_b, r_k)
        return vjp(dv_)

    return _rowwise(fn, [y, r, k, v, g, dout], pars, [(RWKV_WIDTH, F32)] * 5, [p.shape for p in pars[:-1]],
                    tile=256, name="rwkv_post_bwd")


def _tri_inverses(ms):
    n = ms[0].shape[0]
    row = lax.broadcasted_iota(jnp.int32, (n, n), 0)
    col = lax.broadcasted_iota(jnp.int32, (n, n), 1)
    eye = jnp.where(row == col, 1.0, 0.0)
    t_inv = [eye + m for m in ms]
    power = list(ms)
    for _ in range(int(math.log2(n // 2)) - 1):
        power = [_bdot_raw(p, p, NN) for p in power]
        t_inv = [t + _bdot_raw(t, p, NN) for t, p in zip(t_inv, power)]
    return t_inv


@jax.custom_vjp
def _tri_solve(ms, xs):
    return tuple(_bdot_raw(t, x, NN) for t, x in zip(_tri_inverses(ms), xs))


def _tri_solve_fwd(ms, xs):
    t_inv = _tri_inverses(ms)
    us = tuple(_bdot_raw(t, x, NN) for t, x in zip(t_inv, xs))
    return us, (tuple(t_inv), us)


def _tri_solve_bwd(res, dus):
    t_inv, us = res
    dxs = tuple(_bdot_raw(t, du, TN) for t, du in zip(t_inv, dus))
    dms = tuple(_bdot_raw(dx, u, NT) for dx, u in zip(dxs, us))
    return dms, dxs


_tri_solve.defvjp(_tri_solve_fwd, _tri_solve_bwd)


def _chunk_fn(ss, rs, lws, ks, vs, kks, als):
    c = rs[0].shape[0]
    n = 2 * c
    row = lax.broadcasted_iota(jnp.int32, (n, n), 0)
    col = lax.broadcasted_iota(jnp.int32, (n, n), 1)
    incl = (row % c) >= (col % c)
    strict = (row % c) > (col % c)
    lane = lax.broadcasted_iota(jnp.int32, (1, LANES), 1)
    m_lo = jnp.where(lane < HEAD_DIM, 1.0, 0.0)
    m_hi = 1.0 - m_lo

    def stack(a):
        return jnp.concatenate([a * m_lo, a * m_hi], axis=0)

    cums = [_cumsum_rows(lw) for lw in lws]
    totals = [jnp.sum(lw, axis=0, keepdims=True) for lw in lws]
    bs = [kk * al for kk, al in zip(kks, als)]
    grows = [jnp.exp(-cum) for cum in cums]
    a_s = [stack(-kk * jnp.exp(cum - lw)) for kk, cum, lw in zip(kks, cums, lws)]
    b_s = [stack(b * g) for b, g in zip(bs, grows)]
    k_s = [stack(k * g) for k, g in zip(ks, grows)]
    r_s = [stack(r * jnp.exp(cum)) for r, cum in zip(rs, cums)]
    v_s = [stack(v) for v in vs]
    m_ab = [jnp.where(strict, _bdot(a, b, NT), 0.0) for a, b in zip(a_s, b_s)]
    m_ak = [jnp.where(strict, _bdot(a, k, NT), 0.0) for a, k in zip(a_s, k_s)]
    m_rb = [jnp.where(incl, _bdot(r, b, NT), 0.0) for r, b in zip(r_s, b_s)]
    m_rk = [jnp.where(incl, _bdot(r, k, NT), 0.0) for r, k in zip(r_s, k_s)]
    x = [_bdot(a, s, NT) + _bdot(m, v) for a, s, m, v in zip(a_s, ss, m_ak, v_s)]
    u = _tri_solve(tuple(m_ab), tuple(x))
    y = [_fold_rows(_bdot(r, s, NT) + _bdot(mb, uu) + _bdot(mk, v))
         for r, s, mb, uu, mk, v in zip(r_s, ss, m_rb, u, m_rk, v_s)]
    tails = [jnp.exp(tot - cum) for tot, cum in zip(totals, cums)]
    s_new = [s * jnp.exp(tot) + _bdot(uu, stack(b * tl), TN) + _bdot(v, stack(k * tl), TN)
             for s, tot, uu, b, tl, v, k in zip(ss, totals, u, bs, tails, v_s, ks)]
    return tuple(y), tuple(s_new)


def _chains(bsz, npair):
    return [(b, p, slice(p * LANES, (p + 1) * LANES)) for b in range(bsz) for p in range(npair)]


def _hosted_exchange(step, nstep, ex_in, ex_out, sems, scatter):
    if not ex_in:
        return

    @pl.when(step == 0)
    def _():
        _exchange_start(_exchange_copies(ex_in, ex_out, *sems, scatter))

    @pl.when(step == nstep - 1)
    def _():
        _exchange_wait(_exchange_copies(ex_in, ex_out, *sems, scatter))


def _rwkv_fwd(r, lw, k, v, kk, al, hosted=(), scatter=False):
    bsz, t, w = r.shape
    npair, nchunk = w // LANES, t // CHUNK
    chains = _chains(bsz, npair)
    nex = len(hosted)

    def body(*refs):
        r_ref, lw_ref, k_ref, v_ref, kk_ref, al_ref = refs[:6]
        ex_in = refs[6:6 + nex]
        y_ref, sall_ref = refs[6 + nex:8 + nex]
        ex_out = refs[8 + nex:8 + 2 * nex]
        s_scr = refs[8 + 2 * nex]
        step = pl.program_id(0)

        @pl.when(step == 0)
        def _():
            s_scr[...] = jnp.zeros_like(s_scr)

        _hosted_exchange(step, nchunk, ex_in, ex_out, refs[9 + 2 * nex:], scatter)
        ss = tuple(s_scr[i] for i in range(len(chains)))
        for i, s in enumerate(ss):
            sall_ref[0, i] = s
        ys, s_new = _chunk_fn(ss, *[tuple(ref[b, :, cols] for b, _, cols in chains)
                                    for ref in (r_ref, lw_ref, k_ref, v_ref, kk_ref, al_ref)])
        for i, (b, _, cols) in enumerate(chains):
            y_ref[b, :, cols] = ys[i]
            s_scr[i] = s_new[i]

    spec = pl.BlockSpec((bsz, CHUNK, w), lambda c: (0, c, 0))
    hbm = pl.BlockSpec(memory_space=pltpu.HBM)
    outs = pl.pallas_call(
        body,
        grid=(nchunk,),
        in_specs=[spec] * 6 + [hbm] * nex,
        out_specs=[spec, pl.BlockSpec((1, len(chains), LANES, LANES), lambda c: (c, 0, 0, 0))] + [hbm] * nex,
        out_shape=[jax.ShapeDtypeStruct((bsz, t, w), F32),
                   jax.ShapeDtypeStruct((nchunk, len(chains), LANES, LANES), F32)]
        + _exchange_out_shapes(hosted, scatter),
        scratch_shapes=[pltpu.VMEM((len(chains), LANES, LANES), F32)] + (_exchange_scratch(nex) if nex else []),
        compiler_params=_cparams(("arbitrary",), 48),
        name="rwkv_chunk_fwd",
    )(r, lw, k, v, kk, al, *hosted)
    return outs[0], outs[1], outs[2:]


def _rwkv_bwd(r, lw, k, v, kk, al, s_all, dy, hosted=(), scatter=False):
    bsz, t, w = r.shape
    npair, nchunk = w // LANES, t // CHUNK
    chains = _chains(bsz, npair)
    nex = len(hosted)

    def body(*refs):
        r_ref, lw_ref, k_ref, v_ref, kk_ref, al_ref, s_ref, dy_ref = refs[:8]
        ex_in = refs[8:8 + nex]
        out_refs = refs[8 + nex:14 + nex]
        ex_out = refs[14 + nex:14 + 2 * nex]
        ds_scr = refs[14 + 2 * nex]
        step = pl.program_id(0)

        @pl.when(step == 0)
        def _():
            ds_scr[...] = jnp.zeros_like(ds_scr)

        _hosted_exchange(step, nchunk, ex_in, ex_out, refs[15 + 2 * nex:], scatter)
        ss = tuple(s_ref[0, i] for i in range(len(chains)))
        _, vjp = jax.vjp(_chunk_fn, ss, *[tuple(ref[b, :, cols] for b, _, cols in chains)
                                          for ref in (r_ref, lw_ref, k_ref, v_ref, kk_ref, al_ref)])
        grads = vjp((tuple(dy_ref[b, :, cols] for b, _, cols in chains),
                     tuple(ds_scr[i] for i in range(len(chains)))))
        for i, (b, _, cols) in enumerate(chains):
            ds_scr[i] = grads[0][i]
            for ref, gval in zip(out_refs, grads[1:]):
                ref[b, :, cols] = gval[i]

    spec = pl.BlockSpec((bsz, CHUNK, w), lambda c: (0, nchunk - 1 - c, 0))
    sspec = pl.BlockSpec((1, len(chains), LANES, LANES), lambda c: (nchunk - 1 - c, 0, 0, 0))
    hbm = pl.BlockSpec(memory_space=pltpu.HBM)
    outs = pl.pallas_call(
        body,
        grid=(nchunk,),
        in_specs=[spec] * 6 + [sspec, spec] + [hbm] * nex,
        out_specs=[spec] * 6 + [hbm] * nex,
        out_shape=[jax.ShapeDtypeStruct((bsz, t, w), F32)] * 6 + _exchange_out_shapes(hosted, scatter),
        scratch_shapes=[pltpu.VMEM((len(chains), LANES, LANES), F32)] + (_exchange_scratch(nex) if nex else []),
        compiler_params=_cparams(("arbitrary",), 48),
        name="rwkv_chunk_bwd",
    )(r, lw, k, v, kk, al, s_all, dy, *hosted)
    return outs[:6], outs[6:]


def _alibi_slope(head):
    return 2.0 ** (-8.0 * (head + 1) / N_ATTN_HEADS)


def _attn_block(qs, kp, kc, vp, vc, sinks, first):
    row = lax.broadcasted_iota(jnp.int32, (BLOCK, BLOCK), 0)
    col = lax.broadcasted_iota(jnp.int32, (BLOCK, BLOCK), 1)
    lane = lax.broadcasted_iota(jnp.int32, (1, LANES), 1)
    halves = [jnp.where((lane // HEAD_DIM) == half, 1.0, 0.0) for half in range(2)]
    swap = jnp.where((row + HEAD_DIM) % LANES == col, 1.0, 0.0)
    dist_c = (row - col).astype(F32)
    dist_p = dist_c + float(BLOCK)
    valid_c = row >= col
    valid_p = jnp.logical_and(col > row, jnp.logical_not(first))
    scale = HEAD_DIM ** -0.5
    stored = (kp, kc, vp, vc)
    swapped = tuple(_bdot_nn(t, swap) for t in stored)
    heads = [(pair, half) for pair in range(len(qs)) for half in range(2)]
    kv = [stored if half == pair // 2 else swapped for pair, half in heads]
    slopes = [_alibi_slope(2 * pair + half) for pair, half in heads]
    qa = [qs[pair] * halves[half] for pair, half in heads]
    sp = [jnp.where(valid_p, _bdot_nt(q, t[0]) * scale - sl * dist_p, NEG_INF) for q, t, sl in zip(qa, kv, slopes)]
    sc = [jnp.where(valid_c, _bdot_nt(q, t[1]) * scale - sl * dist_c, NEG_INF) for q, t, sl in zip(qa, kv, slopes)]
    mx = [lax.stop_gradient(jnp.maximum(jnp.maximum(jnp.max(a, axis=-1, keepdims=True),
                                                    jnp.max(b, axis=-1, keepdims=True)), sk))
          for a, b, sk in zip(sp, sc, sinks)]
    ep = [jnp.exp(a - m) for a, m in zip(sp, mx)]
    ec = [jnp.exp(b - m) for b, m in zip(sc, mx)]
    es = [jnp.exp(sk - m) for sk, m in zip(sinks, mx)]
    inv = [1.0 / (jnp.sum(a, axis=-1, keepdims=True) + jnp.sum(b, axis=-1, keepdims=True) + s)
           for a, b, s in zip(ep, ec, es)]
    o = [_bdot_nn(a * i, t[2]) + _bdot_nn(b * i, t[3]) for a, b, i, t in zip(ep, ec, inv, kv)]
    outs = tuple(o[2 * pair] * halves[0] + o[2 * pair + 1] * halves[1] for pair in range(len(qs)))
    return outs, [lax.stop_gradient(s * i) for s, i in zip(es, inv)]


def _sink_values(sink_ref):
    return [jnp.max(sink_ref[h:h + 1, :], axis=-1, keepdims=True) for h in range(N_ATTN_HEADS)]


def _attn_fwd(z, sink_rows):
    bsz, t, _ = z.shape
    nb = t // BLOCK
    npair = ATTN_WIDTH // LANES

    def body(q_ref, kp_ref, kc_ref, vp_ref, vc_ref, sink_ref, o_ref):
        first = pl.program_id(1) == 0
        qs = tuple(q_ref[0, :, pair * LANES:(pair + 1) * LANES] for pair in range(npair))
        outs, _ = _attn_block(qs, kp_ref[0], kc_ref[0], vp_ref[0], vc_ref[0], _sink_values(sink_ref), first)
        for pair in range(npair):
            o_ref[0, :, pair * LANES:(pair + 1) * LANES] = outs[pair].astype(o_ref.dtype)

    kcol, vcol = ATTN_WIDTH // KV_WIDTH, ATTN_WIDTH // KV_WIDTH + 1
    return pl.pallas_call(
        body,
        grid=(bsz, nb),
        in_specs=[pl.BlockSpec((1, BLOCK, ATTN_WIDTH), lambda b, n: (b, n, 0)),
                  pl.BlockSpec((1, BLOCK, KV_WIDTH), lambda b, n: (b, jnp.maximum(n - 1, 0), kcol)),
                  pl.BlockSpec((1, BLOCK, KV_WIDTH), lambda b, n: (b, n, kcol)),
                  pl.BlockSpec((1, BLOCK, KV_WIDTH), lambda b, n: (b, jnp.maximum(n - 1, 0), vcol)),
                  pl.BlockSpec((1, BLOCK, KV_WIDTH), lambda b, n: (b, n, vcol)),
                  pl.BlockSpec(sink_rows.shape, lambda b, n: (0, 0))],
        out_specs=pl.BlockSpec((1, BLOCK, ATTN_WIDTH), lambda b, n: (b, n, 0)),
        out_shape=jax.ShapeDtypeStruct((bsz, t, ATTN_WIDTH), BF16),
        compiler_params=_cparams(("parallel", "arbitrary"), 48),
        name="swa_fwd",
    )(z, z, z, z, z, sink_rows)


def _attn_bwd(z, dout, sink_rows):
    bsz, t, _ = z.shape
    nb = t // BLOCK
    npair = ATTN_WIDTH // LANES

    def body(q_ref, kp_ref, kc_ref, vp_ref, vc_ref, do_ref, sink_ref, dz_ref, dsink_ref, carry):
        step = pl.program_id(1)
        n = nb - 1 - step
        first = n == 0

        @pl.when(step == 0)
        def _():
            carry[...] = jnp.zeros_like(carry)

        @pl.when(jnp.logical_and(step == 0, pl.program_id(0) == 0))
        def _():
            dsink_ref[...] = jnp.zeros_like(dsink_ref)

        lane = lax.broadcasted_iota(jnp.int32, (1, LANES), 1)
        qs = tuple(q_ref[0, :, pair * LANES:(pair + 1) * LANES] for pair in range(npair))
        dos = tuple(do_ref[0, :, pair * LANES:(pair + 1) * LANES] for pair in range(npair))
        fn = functools.partial(_attn_block, sinks=_sink_values(sink_ref), first=first)
        outs, vjp, psinks = jax.vjp(fn, qs, kp_ref[0], kc_ref[0], vp_ref[0], vc_ref[0], has_aux=True)
        dqs, dkp, dkc, dvp, dvc = vjp(dos)
        for pair in range(npair):
            dz_ref[0, :, pair * LANES:(pair + 1) * LANES] = dqs[pair].astype(dz_ref.dtype)
            for half in range(2):
                m = jnp.where((lane // HEAD_DIM) == half, 1.0, 0.0)
                delta = jnp.sum(dos[pair] * outs[pair] * m, axis=-1, keepdims=True)
                head = 2 * pair + half
                ds = -jnp.sum(psinks[head] * delta, axis=0, keepdims=True)
                dsink_ref[head:head + 1, :] += jnp.broadcast_to(ds, (1, LANES))
        dz_ref[0, :, ATTN_WIDTH:ATTN_WIDTH + KV_WIDTH] = (dkc + carry[0]).astype(dz_ref.dtype)
        dz_ref[0, :, ATTN_WIDTH + KV_WIDTH:QKV_WIDTH] = (dvc + carry[1]).astype(dz_ref.dtype)
        carry[0] = dkp
        carry[1] = dvp

    kcol, vcol = ATTN_WIDTH // KV_WIDTH, ATTN_WIDTH // KV_WIDTH + 1
    rev = lambda n: nb - 1 - n
    return pl.pallas_call(
        body,
        grid=(bsz, nb),
        in_specs=[pl.BlockSpec((1, BLOCK, ATTN_WIDTH), lambda b, n: (b, rev(n), 0)),
                  pl.BlockSpec((1, BLOCK, KV_WIDTH), lambda b, n: (b, jnp.maximum(rev(n) - 1, 0), kcol)),
                  pl.BlockSpec((1, BLOCK, KV_WIDTH), lambda b, n: (b, rev(n), kcol)),
                  pl.BlockSpec((1, BLOCK, KV_WIDTH), lambda b, n: (b, jnp.maximum(rev(n) - 1, 0), vcol)),
                  pl.BlockSpec((1, BLOCK, KV_WIDTH), lambda b, n: (b, rev(n), vcol)),
                  pl.BlockSpec((1, BLOCK, ATTN_WIDTH), lambda b, n: (b, rev(n), 0)),
                  pl.BlockSpec(sink_rows.shape, lambda b, n: (0, 0))],
        out_specs=[pl.BlockSpec((1, BLOCK, QKV_WIDTH), lambda b, n: (b, rev(n), 0)),
                   pl.BlockSpec((N_ATTN_HEADS, LANES), lambda b, n: (0, 0))],
        out_shape=[jax.ShapeDtypeStruct((bsz, t, QKV_WIDTH), BF16),
                   jax.ShapeDtypeStruct((N_ATTN_HEADS, LANES), F32)],
        scratch_shapes=[pltpu.VMEM((2, BLOCK, KV_WIDTH), F32)],
        compiler_params=_cparams(("arbitrary", "arbitrary"), 48),
        name="swa_bwd",
    )(z, z, z, z, z, dout, sink_rows)


def _exchange_out_shapes(arrays, scatter):
    return [jax.ShapeDtypeStruct((N_DEV,) + (a.shape[1:] if scatter else a.shape), a.dtype) for a in arrays]


def _exchange_scratch(n):
    return [pltpu.SemaphoreType.DMA((n, N_DEV - 1)), pltpu.SemaphoreType.DMA((n, N_DEV - 1)),
            pltpu.SemaphoreType.DMA((n,))]


def _exchange_copies(ins, outs, send_sems, recv_sems, local_sems, scatter):
    x, y, c = lax.axis_index("x"), lax.axis_index("y"), lax.axis_index("c")
    me = 4 * x + 2 * y + c
    copies = []
    for i in range(len(ins)):
        own = pltpu.make_async_copy(ins[i].at[me] if scatter else ins[i], outs[i].at[me], local_sems.at[i])
        copies.append((own, None))
        for d in range(1, N_DEV):
            px = 1 - x if d & 4 else x
            py = 1 - y if d & 2 else y
            pc = 1 - c if d & 1 else c
            peer = 4 * px + 2 * py + pc
            src = ins[i].at[peer] if scatter else ins[i]
            send = pltpu.make_async_remote_copy(src, outs[i].at[me], send_sems.at[i, d - 1], recv_sems.at[i, d - 1],
                                                device_id=(px, py, pc), device_id_type=MESH)
            recv = pltpu.make_async_remote_copy(src, outs[i].at[peer], send_sems.at[i, d - 1], recv_sems.at[i, d - 1],
                                                device_id=(px, py, pc), device_id_type=MESH)
            copies.append((send, recv))
    return copies


def _exchange_start(copies):
    for send, _ in copies:
        send.start()


def _exchange_wait(copies):
    for send, recv in copies:
        if recv is None:
            send.wait()
        else:
            send.wait_send()
            recv.wait_recv()


def _exchange(arrays, *, scatter, name):
    n = len(arrays)

    def body(*refs):
        copies = _exchange_copies(refs[:n], refs[n:2 * n], *refs[2 * n:], scatter)
        _exchange_start(copies)
        _exchange_wait(copies)

    hbm = pl.BlockSpec(memory_space=pltpu.HBM)
    return pl.pallas_call(
        body,
        in_specs=[hbm] * n,
        out_specs=[hbm] * n,
        out_shape=_exchange_out_shapes(arrays, scatter),
        scratch_shapes=_exchange_scratch(n),
        name=name,
    )(*arrays)


def _adamw(parts, w, m, v, name):
    rows, cols = w.shape
    tr = _pick(rows, (256, 128, 64, 8))
    c1 = 1.0 / (1.0 - ADAM_B1 ** ADAM_STEP)
    c2 = 1.0 / (1.0 - ADAM_B2 ** ADAM_STEP)

    def body(p_ref, w_ref, m_ref, v_ref, g_ref, d_ref, mo_ref, vo_ref):
        g = p_ref[0].astype(F32)
        for s in range(1, N_DEV):
            g = g + p_ref[s].astype(F32)
        mn = ADAM_B1 * m_ref[...] + (1.0 - ADAM_B1) * g
        vn = ADAM_B2 * v_ref[...] + (1.0 - ADAM_B2) * (g * g)
        g_ref[...] = g
        mo_ref[...] = mn
        vo_ref[...] = vn
        d_ref[...] = -ADAM_LR * ((mn * c1) / (jnp.sqrt(vn * c2) + ADAM_EPS) + ADAM_WD * w_ref[...])

    spec = pl.BlockSpec((tr, cols), lambda i: (i, 0))
    return pl.pallas_call(
        body,
        grid=(rows // tr,),
        in_specs=[pl.BlockSpec((N_DEV, tr, cols), lambda i: (0, i, 0)), spec, spec, spec],
        out_specs=[spec] * 4,
        out_shape=[jax.ShapeDtypeStruct((rows, cols), F32)] * 4,
        compiler_params=_cparams(("parallel",), 48),
        name=name,
    )(parts, w, m, v)


_VECTOR_PARAMS = ("attn_norm_g", "attn_sinks", "rwkv_mu", "w0", "a0", "k_k", "k_a", "r_k", "ln_x_w", "ln_x_b",
                  "mlp_norm_g", "final_norm_g")
_WEIGHT_NAMES = ("attn_norm_g", "w_in", "attn_sinks", "rwkv_mu", "w0", "w2", "a0", "a2", "g2", "k_k", "k_a", "r_k",
                 "ln_x_w", "ln_x_b", "w_out", "mlp_norm_g", "w_up", "w_down", "final_norm_g")


def _pack_vectors(vals):
    pieces = []
    for name in _VECTOR_PARAMS:
        flat = vals[name].reshape(1, -1)
        pad = (-flat.shape[1]) % LANES
        pieces.append(jnp.pad(flat, ((0, 0), (0, pad))) if pad else flat)
    return jnp.concatenate(pieces, axis=1)


def _unpack_vectors(packed, like):
    out, col = {}, 0
    for name in _VECTOR_PARAMS:
        size = like[name].size
        out[name] = packed[0, col:col + size].reshape(like[name].shape)
        col += size + (-size) % LANES
    return out


def kernel(x, attn_norm_g, w_in, attn_sinks, rwkv_mu, w0, w2, a0, a2, g2, k_k, k_a, r_k, ln_x_w, ln_x_b, w_out, mlp_norm_g, w_up, w_down, final_norm_g, loss_target, m_attn_norm_g, m_w_in, m_attn_sinks, m_rwkv_mu, m_w0, m_w2, m_a0, m_a2, m_g2, m_k_k, m_k_a, m_r_k, m_ln_x_w, m_ln_x_b, m_w_out, m_mlp_norm_g, m_w_up, m_w_down, m_final_norm_g, v_attn_norm_g, v_w_in, v_attn_sinks, v_rwkv_mu, v_w0, v_w2, v_a0, v_a2, v_g2, v_k_k, v_k_a, v_r_k, v_ln_x_w, v_ln_x_b, v_w_out, v_mlp_norm_g, v_w_up, v_w_down, v_final_norm_g):
    weights = dict(attn_norm_g=attn_norm_g, w_in=w_in, attn_sinks=attn_sinks, rwkv_mu=rwkv_mu, w0=w0, w2=w2, a0=a0,
                   a2=a2, g2=g2, k_k=k_k, k_a=k_a, r_k=r_k, ln_x_w=ln_x_w, ln_x_b=ln_x_b, w_out=w_out,
                   mlp_norm_g=mlp_norm_g, w_up=w_up, w_down=w_down, final_norm_g=final_norm_g)
    mom1 = dict(attn_norm_g=m_attn_norm_g, w_in=m_w_in, attn_sinks=m_attn_sinks, rwkv_mu=m_rwkv_mu, w0=m_w0, w2=m_w2,
                a0=m_a0, a2=m_a2, g2=m_g2, k_k=m_k_k, k_a=m_k_a, r_k=m_r_k, ln_x_w=m_ln_x_w, ln_x_b=m_ln_x_b,
                w_out=m_w_out, mlp_norm_g=m_mlp_norm_g, w_up=m_w_up, w_down=m_w_down, final_norm_g=m_final_norm_g)
    mom2 = dict(attn_norm_g=v_attn_norm_g, w_in=v_w_in, attn_sinks=v_attn_sinks, rwkv_mu=v_rwkv_mu, w0=v_w0, w2=v_w2,
                a0=v_a0, a2=v_a2, g2=v_g2, k_k=v_k_k, k_a=v_k_a, r_k=v_r_k, ln_x_w=v_ln_x_w, ln_x_b=v_ln_x_b,
                w_out=v_w_out, mlp_norm_g=v_mlp_norm_g, w_up=v_w_up, w_down=v_w_down, final_norm_g=v_final_norm_g)
    bsz, seq, d_model = x.shape
    rows = bsz * seq
    d_in = N_DEV * w_in.shape[2]
    d_ff = N_DEV * w_up.shape[2]

    gathered = _exchange([w_in[0].astype(BF16), w2[0], a2[0], g2[0]], scatter=False, name="gather_in_weights")
    cols_first = lambda a: a.transpose(1, 0, 2).reshape(a.shape[1], -1)
    w_in_f = cols_first(gathered[0])
    w_attn, w_rw = w_in_f[:, :QKV_WIDTH], w_in_f[:, QKV_WIDTH:]
    w2_f, a2_f, g2_f = cols_first(gathered[1]), cols_first(gathered[2]), cols_first(gathered[3])
    lora = w2_f.shape[0]
    w2p = jnp.concatenate([w2_f, jnp.zeros_like(a2_f)], axis=0)
    a2p = jnp.concatenate([jnp.zeros_like(w2_f), a2_f], axis=0)

    esum = _head_sum_matrix()
    sink_rows = jnp.broadcast_to(attn_sinks.reshape(N_ATTN_HEADS, 1), (N_ATTN_HEADS, LANES))
    prep_pars = [w0, w2p, a0, a2p, g2_f, k_k, k_a, esum]
    post_pars = [ln_x_w, ln_x_b, r_k, esum]

    x2d = x.reshape(rows, d_model)
    h1 = _norm_fwd(x2d, attn_norm_g, "attn_norm_fwd")
    z_attn = _matmul(h1, w_attn, "nn", name="in_proj_attn")
    z_rw = _matmul(h1, w_rw, "nn", name="in_proj_rwkv")
    z_attn3 = z_attn.reshape(bsz, seq, QKV_WIDTH)
    attn_out = _attn_fwd(z_attn3, sink_rows)
    r, lw, k, v, kk, al, gate = _prep_fwd(z_rw, seq, rwkv_mu, prep_pars)
    as3 = lambda a: a.reshape(bsz, seq, RWKV_WIDTH)
    y, s_all, late = _rwkv_fwd(as3(r), as3(lw), as3(k), as3(v), as3(kk), as3(al),
                               hosted=[w_out[0].astype(BF16), w_up[0].astype(BF16), w_down[0].astype(BF16)])
    w_out_f = late[0].reshape(-1, d_model)
    w_up_f = cols_first(late[1])
    w_down_f = late[2].reshape(-1, d_model)
    y2 = y.reshape(rows, RWKV_WIDTH)
    rw_out = _post_fwd(y2, r, k, v, gate, post_pars)
    mix = jnp.concatenate([attn_out.reshape(rows, ATTN_WIDTH), rw_out], axis=1)
    residual = lambda acc, res: (acc + res,)
    x1 = _matmul(mix, w_out_f, "nn", name="out_proj", extras=(x2d,), epilogue=residual)
    h2 = _norm_fwd(x1, mlp_norm_g, "mlp_norm_fwd")

    def relu_sq(acc):
        pos = jnp.maximum(acc, 0.0)
        return acc, pos * pos

    u, act = _matmul(h2, w_up_f, "nn", name="mlp_up", epilogue=relu_sq, out_dtypes=(BF16, BF16))
    x2 = _matmul(act, w_down_f, "nn", name="mlp_down", extras=(x1,), epilogue=residual)
    dx2, loss_vec, g_final = _final_loss(x2, loss_target.reshape(rows, d_model), final_norm_g.reshape(1, d_model))

    g_w_down = _matmul(act, dx2, "tn", name="grad_w_down", out_dtypes=(BF16,))
    du = _matmul(dx2, w_down_f, "nt", name="mlp_down_bwd", extras=(u,), out_dtypes=(BF16,),
                 epilogue=lambda acc, uv: (acc * (2.0 * jnp.maximum(uv.astype(F32), 0.0)),))
    g_w_up = _matmul(h2, du, "tn", name="grad_w_up", out_dtypes=(BF16,))
    dh2 = _matmul(du, w_up_f, "nt", name="mlp_up_bwd")
    dx1, g_mlp_norm = _norm_bwd(x1, dh2, dx2, mlp_norm_g, "mlp_norm_bwd")
    g_w_out = _matmul(mix, dx1, "tn", name="grad_w_out", out_dtypes=(BF16,))
    dmix = _matmul(dx1, w_out_f, "nt", name="out_proj_bwd")
    dy, dr_a, dk_a, dv_a, dgate, g_ln_w, g_ln_b, g_r_k = _post_bwd(
        y2, r, k, v, gate, (dmix, RWKV_WIDTH, ATTN_WIDTH // RWKV_WIDTH), post_pars)
    by_cols = lambda a: a.reshape(a.shape[0], N_DEV, -1).transpose(1, 0, 2)
    (dr_b, dlw, dk_b, dv_b, dkk, dal), (p_w_out, p_w_up, p_w_down) = _rwkv_bwd(
        as3(r), as3(lw), as3(k), as3(v), as3(kk), as3(al), s_all, as3(dy),
        hosted=[g_w_out.reshape(N_DEV, -1, d_model), by_cols(g_w_up), g_w_down.reshape(N_DEV, -1, d_model)],
        scatter=True)
    flat = lambda a: a.reshape(rows, RWKV_WIDTH)
    (dz_rw, gmu_r, gmu_k, gmu_v, gmu_wa, gmu_g, g_w0, g_w2p, g_a0, g_a2p, g_g2, g_k_k, g_k_a) = _prep_bwd(
        z_rw, seq, [dr_a, flat(dr_b), flat(dlw), dk_a, flat(dk_b), dv_a, flat(dv_b), flat(dkk), flat(dal), dgate],
        rwkv_mu, prep_pars)
    dz_attn, g_sink_rows = _attn_bwd(z_attn3, dmix.reshape(bsz, seq, d_model), sink_rows)
    dz_attn = dz_attn.reshape(rows, QKV_WIDTH)
    g_w_in = jnp.concatenate([_matmul(h1, dz_attn, "tn", name="grad_w_in_attn", out_dtypes=(BF16,)),
                              _matmul(h1, dz_rw, "tn", name="grad_w_in_rwkv", out_dtypes=(BF16,))], axis=1)
    dh1 = _matmul(dz_attn, w_attn, "nt", name="in_proj_attn_bwd")
    dh1 = _matmul(dz_rw, w_rw, "nt", name="in_proj_rwkv_bwd", extras=(dh1,), epilogue=residual)
    dx, g_attn_norm = _norm_bwd(x2d, dh1, dx1, attn_norm_g, "attn_norm_bwd")

    lora_grads = jnp.concatenate([g_w2p[:lora], g_a2p[lora:], g_g2], axis=0)
    vec_grads = dict(attn_norm_g=g_attn_norm, attn_sinks=g_sink_rows[:, 0], rwkv_mu=jnp.concatenate(
        [gmu_r, gmu_k, gmu_v, gmu_wa, gmu_g], axis=1), w0=g_w0, a0=g_a0, k_k=g_k_k, k_a=g_k_a, r_k=g_r_k,
        ln_x_w=g_ln_w, ln_x_b=g_ln_b, mlp_norm_g=g_mlp_norm, final_norm_g=g_final)
    packed = _pack_vectors(vec_grads)
    p_w_in, p_lora, vec_parts = _exchange(
        [by_cols(g_w_in), by_cols(lora_grads), jnp.broadcast_to(packed[None], (N_DEV,) + packed.shape)],
        scatter=True, name="scatter_in_grads")

    grads, delta, new_m, new_v = {}, {}, {}, {}

    def update(name, part, shape2d):
        res = _adamw(part, weights[name].reshape(shape2d), mom1[name].reshape(shape2d), mom2[name].reshape(shape2d),
                     "adamw_" + name)
        for store, val in zip((grads, delta, new_m, new_v), res):
            store[name] = val.reshape(weights[name].shape)

    update("w_in", p_w_in, w_in.shape[1:])
    update("w_out", p_w_out, w_out.shape[1:])
    update("w_up", p_w_up, w_up.shape[1:])
    update("w_down", p_w_down, w_down.shape[1:])
    stack = lambda d: jnp.concatenate([d["w2"][0], d["a2"][0], d["g2"][0]], axis=0)
    lora_res = _adamw(p_lora, stack(weights), stack(mom1), stack(mom2), "adamw_lora")
    for store, val in zip((grads, delta, new_m, new_v), lora_res):
        store["w2"], store["a2"], store["g2"] = val[None, :lora], val[None, lora:2 * lora], val[None, 2 * lora:]
    vec_res = _adamw(vec_parts, _pack_vectors(weights), _pack_vectors(mom1), _pack_vectors(mom2), "adamw_vectors")
    for store, val in zip((grads, delta, new_m, new_v), vec_res):
        store.update(_unpack_vectors(val, weights))

    loss = lax.psum(loss_vec[0, 0], ("x", "y", "c"))
    return (loss, dx.reshape(x.shape), *[grads[n] for n in _WEIGHT_NAMES], *[delta[n] for n in _WEIGHT_NAMES],
            *[new_m[n] for n in _WEIGHT_NAMES], *[new_v[n] for n in _WEIGHT_NAMES])
```

```python
import functools
import math

import jax
import jax.numpy as jnp
from jax import lax
from jax.experimental import pallas as pl
from jax.experimental.pallas import tpu as pltpu

F32 = jnp.float32
BF16 = jnp.bfloat16

N_DEV = 8
HEAD_DIM = 64
LANES = 128
N_ATTN_HEADS = 8
ATTN_WIDTH = 512
KV_WIDTH = 128
QKV_WIDTH = ATTN_WIDTH + 2 * KV_WIDTH
RWKV_WIDTH = 512
LORA_WA = 128
GATE_LORA = 128
RWKV_SHIFT_WIDTH = 3 * RWKV_WIDTH + LORA_WA + GATE_LORA
BLOCK = 128
CHUNK = 64
RMS_EPS = 1e-6
GN_EPS = 64e-5
L2_EPS = 1e-12
NEG_INF = -1e30
DECAY_SCALE = math.exp(-0.5)
ADAM_LR, ADAM_B1, ADAM_B2, ADAM_EPS, ADAM_WD, ADAM_STEP = 0.001, 0.9, 0.999, 1e-08, 0.01, 10

NN = (((1,), (0,)), ((), ()))
NT = (((1,), (1,)), ((), ()))
TN = (((0,), (0,)), ((), ()))
MESH = pl.DeviceIdType.MESH


def _dot(a, b, dn=NN, precision=None):
    return lax.dot_general(a, b, dn, precision=precision, preferred_element_type=F32)


def _bdot_raw(a, b, dn):
    return lax.dot_general(a.astype(BF16), b.astype(BF16), dn, preferred_element_type=F32)


@functools.partial(jax.custom_vjp, nondiff_argnums=(2, 3))
def _bdot_c(a, b, ca, cb):
    return _bdot_raw(a, b, (((ca,), (cb,)), ((), ())))


def _bdot_c_fwd(a, b, ca, cb):
    return _bdot_c(a, b, ca, cb), (a, b)


def _bdot_c_bwd(ca, cb, res, ct):
    a, b = res
    fa, fb = 1 - ca, 1 - cb
    da = _bdot_raw(ct, b, (((1,), (fb,)), ((), ()))) if ca == 1 else _bdot_raw(b, ct, (((fb,), (1,)), ((), ())))
    db = _bdot_raw(a, ct, (((fa,), (0,)), ((), ()))) if cb == 0 else _bdot_raw(ct, a, (((0,), (fa,)), ((), ())))
    return da, db


_bdot_c.defvjp(_bdot_c_fwd, _bdot_c_bwd)


def _bdot(a, b, dn=NN):
    return _bdot_c(a, b, dn[0][0][0], dn[0][1][0])


def _bdot_nn(a, b):
    return _bdot(a, b, NN)


def _bdot_nt(a, b):
    return _bdot(a, b, NT)


def _split3(x):
    hi = x.astype(BF16)
    rest = x - hi.astype(F32)
    mid = rest.astype(BF16)
    return hi, mid, (rest - mid.astype(F32)).astype(BF16)


def _running_sum(x, dn):
    c = x.shape[0]
    row = lax.broadcasted_iota(jnp.int32, (c, c), 0)
    col = lax.broadcasted_iota(jnp.int32, (c, c), 1)
    tri = jnp.where(row >= col, 1.0, 0.0).astype(BF16)
    w = x.shape[1]
    parts = lax.dot_general(tri, jnp.concatenate(_split3(x), axis=1), dn, preferred_element_type=F32)
    return parts[:, :w] + parts[:, w:2 * w] + parts[:, 2 * w:]


@jax.custom_vjp
def _cumsum_rows(x):
    return _running_sum(x, NN)


_cumsum_rows.defvjp(lambda x: (_running_sum(x, NN), None), lambda _, ct: (_running_sum(ct, TN),))


@jax.custom_vjp
def _fold_rows(x):
    c = x.shape[0] // 2
    return x[:c] + x[c:]


_fold_rows.defvjp(lambda x: (_fold_rows(x), None), lambda _, ct: (jnp.concatenate([ct, ct], axis=0),))


@jax.custom_vjp
def _halves(x):
    n = x.shape[0] // 2
    return x[:n], x[n:]


_halves.defvjp(lambda x: (_halves(x), None), lambda _, cts: (jnp.concatenate(cts, axis=0),))


@jax.custom_vjp
def _quarters(x):
    n = x.shape[0] // 2
    return x[:n, :n], x[:n, n:], x[n:, :n], x[n:, n:]


_quarters.defvjp(lambda x: (_quarters(x), None),
                 lambda _, cts: (jnp.concatenate([jnp.concatenate(cts[:2], axis=1),
                                                  jnp.concatenate(cts[2:], axis=1)], axis=0),))


def _sigmoid(x):
    return 1.0 / (1.0 + jnp.exp(-x))


def _pick(n, cands):
    for c in cands:
        if n % c == 0:
            return c
    return n


def _cparams(sem, vmem_mb=None):
    kw = dict(dimension_semantics=sem)
    if vmem_mb is not None:
        kw["vmem_limit_bytes"] = vmem_mb * 1024 * 1024
    return pltpu.CompilerParams(**kw)


def _matmul(a, b, mode, *, name, extras=(), epilogue=None, out_dtypes=(F32,), tm=1024, tn=1024, tk=1024,
            hosted=(), scatter=False):
    if mode == "nn":
        (M, K), (_, N) = a.shape, b.shape
    elif mode == "tn":
        (K, M), (_, N) = a.shape, b.shape
    else:
        (M, K), (N, _) = a.shape, b.shape
    tm = _pick(M, (tm, 512, 256, 128))
    tn = _pick(N, (tn, 896, 768, 512, 384, 256, 128))
    tk = _pick(K, (tk, 512, 256, 128))
    nk = K // tk
    ne, nout = len(extras), len(out_dtypes)
    if mode == "nn":
        a_spec = pl.BlockSpec((tm, tk), lambda i, j, k: (i, k))
        b_spec = pl.BlockSpec((tk, tn), lambda i, j, k: (k, j))
        dn = NN
    elif mode == "tn":
        a_spec = pl.BlockSpec((tk, tm), lambda i, j, k: (k, i))
        b_spec = pl.BlockSpec((tk, tn), lambda i, j, k: (k, j))
        dn = TN
    else:
        a_spec = pl.BlockSpec((tm, tk), lambda i, j, k: (i, k))
        b_spec = pl.BlockSpec((tn, tk), lambda i, j, k: (j, k))
        dn = NT
    o_spec = pl.BlockSpec((tm, tn), lambda i, j, k: (i, j))
    grid = (M // tm, N // tn, nk)
    nex = len(hosted)

    def body(*refs):
        a_ref, b_ref = refs[:2]
        e_refs = refs[2:2 + ne]
        ex_in = refs[2 + ne:2 + ne + nex]
        o_refs = refs[2 + ne + nex:2 + ne + nex + nout]
        ex_out = refs[2 + ne + nex + nout:2 + ne + 2 * nex + nout]
        scratch = refs[2 + ne + 2 * nex + nout:]
        kstep = pl.program_id(2)
        if nex:
            at = [pl.program_id(d) for d in range(3)]
            first = jnp.logical_and(jnp.logical_and(at[0] == 0, at[1] == 0), at[2] == 0)
            last = jnp.logical_and(jnp.logical_and(at[0] == grid[0] - 1, at[1] == grid[1] - 1), at[2] == grid[2] - 1)
            _hosted_exchange(first, last, ex_in, ex_out, scratch[-3:], scatter)

        def finish(total):
            outs = (total,) if epilogue is None else epilogue(total, *[e[...] for e in e_refs])
            for o_ref, o in zip(o_refs, outs):
                o_ref[...] = o.astype(o_ref.dtype)

        if nk == 1:
            finish(_bdot_raw(a_ref[...], b_ref[...], dn))
            return
        acc = scratch[0]

        @pl.when(kstep == 0)
        def _():
            acc[...] = jnp.zeros_like(acc)

        acc[...] += _bdot_raw(a_ref[...], b_ref[...], dn)

        @pl.when(kstep == nk - 1)
        def _():
            finish(acc[...])

    hbm = pl.BlockSpec(memory_space=pltpu.HBM)
    outs = pl.pallas_call(
        body,
        grid=grid,
        in_specs=[a_spec, b_spec] + [o_spec] * ne + [hbm] * nex,
        out_specs=[o_spec] * nout + [hbm] * nex,
        out_shape=[jax.ShapeDtypeStruct((M, N), dt) for dt in out_dtypes] + _exchange_out_shapes(hosted, scatter),
        scratch_shapes=([pltpu.VMEM((tm, tn), F32)] if nk > 1 else []) + (_exchange_scratch(nex) if nex else []),
        compiler_params=_cparams(("arbitrary",) * 3 if nex else ("parallel", "parallel", "arbitrary"), 56),
        name=name,
    )(a, b, *extras, *hosted)
    if nex:
        return outs[:nout], outs[nout:]
    return outs[0] if nout == 1 else outs


def _rowwise(fn, rows, pars, out_rows, out_accs, *, tile, name):
    rows = [r if isinstance(r, tuple) else (r, r.shape[1], 0) for r in rows]
    R = rows[0][0].shape[0]
    tile = min(tile, R)
    nr, npar, nor, noa = len(rows), len(pars), len(out_rows), len(out_accs)

    def body(*refs):
        rin = refs[:nr]
        pin = refs[nr:nr + npar]
        orow = refs[nr + npar:nr + npar + nor]
        oacc = refs[nr + npar + nor:]
        outs = fn(*[r[...] for r in rin], *[p[...] for p in pin])
        for ref, o in zip(orow, outs[:nor]):
            if isinstance(o, (tuple, list)):
                col = 0
                for piece in o:
                    ref[:, col:col + piece.shape[1]] = piece.astype(ref.dtype)
                    col += piece.shape[1]
            else:
                ref[...] = o.astype(ref.dtype)
        step = pl.program_id(0)

        def accumulate(ref, o):
            @pl.when(step == 0)
            def _():
                ref[...] = o

            @pl.when(step > 0)
            def _():
                ref[...] += o

        for ref, o in zip(oacc, outs[nor:]):
            accumulate(ref, o)

    def colspec(width, cb):
        return pl.BlockSpec((tile, width), lambda i: (i, cb))

    return pl.pallas_call(
        body,
        grid=(R // tile,),
        in_specs=[colspec(w, cb) for (_, w, cb) in rows]
        + [pl.BlockSpec(p.shape, lambda i: (0, 0)) for p in pars],
        out_specs=[colspec(w, 0) for (w, _) in out_rows]
        + [pl.BlockSpec(s, lambda i: (0, 0)) for s in out_accs],
        out_shape=[jax.ShapeDtypeStruct((R, w), dt) for (w, dt) in out_rows]
        + [jax.ShapeDtypeStruct(s, F32) for s in out_accs],
        compiler_params=_cparams(("arbitrary",), 56),
        name=name,
    )(*[r[0] for r in rows], *pars)


def _rms_fn(x, g):
    return x * lax.rsqrt(jnp.mean(x * x, axis=-1, keepdims=True) + RMS_EPS) * g


def _norm_fwd(x, g, name):
    return _rowwise(lambda xv, gv: (_rms_fn(xv, gv),), [x], [g], [(x.shape[1], BF16)], [], tile=512, name=name)[0]


def _norm_bwd(x, dh, dres, g, name):
    def fn(xv, dhv, dresv, gv):
        _, vjp = jax.vjp(_rms_fn, xv, gv)
        dx, dg = vjp(dhv)
        return dx + dresv, dg

    return _rowwise(fn, [x, dh, dres], [g], [(x.shape[1], F32)], [g.shape], tile=256, name=name)


def _final_loss(x, tgt, g):
    d = x.shape[1]

    def fn(xv, tv, gv):
        y, vjp = jax.vjp(_rms_fn, xv, gv)
        err = y - tv
        loss = 0.5 * jnp.sum(jnp.sum(err * err, axis=-1, keepdims=True), axis=0, keepdims=True) / d
        dx, dg = vjp(err / d)
        return dx, jnp.broadcast_to(loss, (1, LANES)), dg

    return _rowwise(fn, [x, tgt], [g], [(d, F32)], [(1, LANES), g.shape], tile=256, name="final_norm_loss")


def _head_sum_matrix():
    i = lax.broadcasted_iota(jnp.int32, (RWKV_WIDTH, RWKV_WIDTH), 0) // HEAD_DIM
    j = lax.broadcasted_iota(jnp.int32, (RWKV_WIDTH, RWKV_WIDTH), 1) // HEAD_DIM
    return (i == j).astype(BF16)


def _head_sums_raw(x, esum):
    hi = x.astype(BF16)
    lo = (x - hi.astype(F32)).astype(BF16)
    return _dot(hi, esum) + _dot(lo, esum)


@jax.custom_vjp
def _head_sums(x, esum):
    return _head_sums_raw(x, esum)


_head_sums.defvjp(lambda x, esum: (_head_sums_raw(x, esum), esum),
                  lambda esum, ct: (_head_sums_raw(ct, esum), jnp.zeros_like(esum)))


def _prep_core(xr, xk, xv, xwa, xg, w0, w2p, a0, a2p, g2, k_k, k_a, esum):
    lw = -DECAY_SCALE * _sigmoid(w0 + _bdot_nn(jnp.tanh(xwa), w2p))
    a = _sigmoid(a0 + _bdot_nn(xwa, a2p))
    g = _bdot_nn(_sigmoid(xg), g2)
    kk0 = xk * k_k
    kk = kk0 / jnp.maximum(jnp.sqrt(_head_sums(kk0 * kk0, esum)), L2_EPS)
    k = xk * (1.0 + (a - 1.0) * k_a)
    return xr, lw, k, xv, kk, a, g


_SEGS = ((0, 512), (512, 1024), (1024, 1536), (1536, 1664), (1664, 1792))


PREP_TILE = 256
SUBLANES = 8


def _shifted_tokens(z_ref, zprev_ref, tile_index, seq):
    zc = z_ref[...]
    start = (tile_index * PREP_TILE) % seq == 0
    before = jnp.where(start, 0.0, zprev_ref[SUBLANES - 1:SUBLANES, :])
    rowid = lax.broadcasted_iota(jnp.int32, zc.shape, 0)
    return zc, jnp.where(rowid == 0, before, pltpu.roll(zc, 1, 0))


def _prep_specs(z, mu, pars, index):
    width = z.shape[1]
    per = PREP_TILE // SUBLANES
    return ([pl.BlockSpec((PREP_TILE, width), lambda i: (index(i), 0)),
             pl.BlockSpec((SUBLANES, width), lambda i: (jnp.maximum(index(i) * per - 1, 0), 0))],
            [pl.BlockSpec(p.shape, lambda i: (0, 0)) for p in (mu, *pars)])


def _prep_fwd(z, seq, mu, pars):
    rows = z.shape[0]
    npar = len(pars)

    def body(z_ref, zprev_ref, mu_ref, *rest):
        par_refs, out_refs = rest[:npar], rest[npar:]
        zc, zp = _shifted_tokens(z_ref, zprev_ref, pl.program_id(0), seq)
        zs = zc + (zp - zc) * mu_ref[...]
        outs = _prep_core(*[zs[:, a:b] for a, b in _SEGS], *[p[...] for p in par_refs])
        for ref, o in zip(out_refs, outs):
            ref[...] = o

    zspecs, pspecs = _prep_specs(z, mu, pars, lambda i: i)
    return pl.pallas_call(
        body,
        grid=(rows // PREP_TILE,),
        in_specs=zspecs + pspecs,
        out_specs=[pl.BlockSpec((PREP_TILE, RWKV_WIDTH), lambda i: (i, 0))] * 7,
        out_shape=[jax.ShapeDtypeStruct((rows, RWKV_WIDTH), F32)] * 7,
        compiler_params=_cparams(("parallel",), 56),
        name="rwkv_prep_fwd",
    )(z, z, mu, *pars)


def _prep_bwd(z, seq, cts, mu, pars):
    rows, width = z.shape
    ntile = rows // PREP_TILE
    npar, nct = len(pars), len(cts)
    acc_shapes = [(1, b - a) for a, b in _SEGS] + [p.shape for p in pars[:-1]]

    def body(z_ref, zprev_ref, *rest):
        ct_refs = rest[:nct]
        mu_ref = rest[nct]
        par_refs = rest[nct + 1:nct + 1 + npar]
        dz_ref = rest[nct + 1 + npar]
        acc_refs = rest[nct + 2 + npar:-1]
        carry = rest[-1]
        step = pl.program_id(0)
        tile_index = ntile - 1 - step

        @pl.when(step == 0)
        def _():
            carry[...] = jnp.zeros_like(carry)

        zc, zp = _shifted_tokens(z_ref, zprev_ref, tile_index, seq)
        mu_v = mu_ref[...]
        diff = zp - zc
        zs = zc + diff * mu_v
        dra, drb, dlw, dka, dkb, dva, dvb, dkk, da, dg = [c[...] for c in ct_refs]
        pv = [p[...] for p in par_refs]
        _, vjp = jax.vjp(lambda *args: _prep_core(*args, pv[-1]), *[zs[:, a:b] for a, b in _SEGS], *pv[:-1])
        grads = vjp((dra + drb, dlw, dka + dkb, dva + dvb, dkk, da, dg))
        dsegs, dpars = grads[:5], grads[5:]
        last_of_sequence = ((tile_index + 1) * PREP_TILE) % seq == 0
        accs = []
        for ds, (a, b) in zip(dsegs, _SEGS):
            mu_s = mu_v[:, a:b]
            dzp = ds * mu_s
            after = jnp.where(last_of_sequence, 0.0, carry[0:1, a:b])
            rowid = lax.broadcasted_iota(jnp.int32, dzp.shape, 0)
            from_next = jnp.where(rowid == PREP_TILE - 1, after, pltpu.roll(dzp, PREP_TILE - 1, 0))
            dz_ref[:, a:b] = (ds * (1.0 - mu_s) + from_next).astype(dz_ref.dtype)
            carry[:, a:b] = dzp[0:SUBLANES, :]
            accs.append(jnp.sum(ds * diff[:, a:b], axis=0, keepdims=True))
        accs.extend(dpars)

        def accumulate(ref, o):
            @pl.when(step == 0)
            def _():
                ref[...] = o

            @pl.when(step > 0)
            def _():
                ref[...] += o

        for ref, o in zip(acc_refs, accs):
            accumulate(ref, o)

    rev = lambda i: ntile - 1 - i
    zspecs, pspecs = _prep_specs(z, mu, pars, rev)
    return pl.pallas_call(
        body,
        grid=(ntile,),
        in_specs=zspecs + [pl.BlockSpec((PREP_TILE, RWKV_WIDTH), lambda i: (rev(i), 0))] * nct + pspecs,
        out_specs=[pl.BlockSpec((PREP_TILE, width), lambda i: (rev(i), 0))]
        + [pl.BlockSpec(s, lambda i: (0, 0)) for s in acc_shapes],
        out_shape=[jax.ShapeDtypeStruct((rows, width), BF16)] + [jax.ShapeDtypeStruct(s, F32) for s in acc_shapes],
        scratch_shapes=[pltpu.VMEM((SUBLANES, width), F32)],
        compiler_params=_cparams(("arbitrary",), 56),
        name="rwkv_prep_bwd",
    )(z, z, *cts, mu, *pars)


def _post_fn(y, r, k, v, g, ln_w, ln_b, r_k, esum):
    mean = _head_sums(y, esum) * (1.0 / HEAD_DIM)
    yc = y - mean
    var = _head_sums(yc * yc, esum) * (1.0 / HEAD_DIM)
    yn = yc * lax.rsqrt(var + GN_EPS) * ln_w + ln_b
    bonus = _head_sums(r * k * r_k, esum) * v
    return (yn + bonus) * g


def _post_fwd(y, r, k, v, g, pars):
    return _rowwise(lambda *a: (_post_fn(*a),), [y, r, k, v, g], pars, [(RWKV_WIDTH, BF16)], [],
                    tile=256, name="rwkv_post_fwd")[0]


def _post_bwd(y, r, k, v, g, dout, pars):
    def fn(yv, rv, kv, vv, gv, dv_, ln_w, ln_b, r_k, esum):
        _, vjp = jax.vjp(lambda *a: _post_fn(*a, esum), yv, rv, kv, vv, gv, ln_w, ln_b, r_k)
        return vjp(dv_)

    return _rowwise(fn, [y, r, k, v, g, dout], pars, [(RWKV_WIDTH, F32)] * 5, [p.shape for p in pars[:-1]],
                    tile=256, name="rwkv_post_bwd")


def _tri_inverses(ms):
    n = ms[0].shape[0]
    row = lax.broadcasted_iota(jnp.int32, (n, n), 0)
    col = lax.broadcasted_iota(jnp.int32, (n, n), 1)
    eye = jnp.where(row == col, 1.0, 0.0)
    t_inv = [eye + m for m in ms]
    power = [_bdot_raw(m, m, NN) for m in ms]
    steps = int(math.log2(n // 2)) - 1
    for step in range(steps):
        if step < steps - 1:
            both = [_bdot_raw(jnp.concatenate([p, t], axis=0), p, NN) for p, t in zip(power, t_inv)]
            power = [b[:n] for b in both]
            t_inv = [t + b[n:] for t, b in zip(t_inv, both)]
        else:
            t_inv = [t + _bdot_raw(t, p, NN) for t, p in zip(t_inv, power)]
    return t_inv


@jax.custom_vjp
def _tri_solve(ms, xs):
    return tuple(_bdot_raw(t, x, NN) for t, x in zip(_tri_inverses(ms), xs))


def _tri_solve_fwd(ms, xs):
    t_inv = _tri_inverses(ms)
    us = tuple(_bdot_raw(t, x, NN) for t, x in zip(t_inv, xs))
    return us, (tuple(t_inv), us)


def _tri_solve_bwd(res, dus):
    t_inv, us = res
    dxs = tuple(_bdot_raw(t, du, TN) for t, du in zip(t_inv, dus))
    dms = tuple(_bdot_raw(dx, u, NT) for dx, u in zip(dxs, us))
    return dms, dxs


_tri_solve.defvjp(_tri_solve_fwd, _tri_solve_bwd)


def _chunk_fn(ss, rs, lws, ks, vs, kks, als):
    c = rs[0].shape[0]
    n = 2 * c
    row = lax.broadcasted_iota(jnp.int32, (n, n), 0)
    col = lax.broadcasted_iota(jnp.int32, (n, n), 1)
    incl = (row % c) >= (col % c)
    strict = (row % c) > (col % c)
    lane = lax.broadcasted_iota(jnp.int32, (1, LANES), 1)
    m_lo = jnp.where(lane < HEAD_DIM, 1.0, 0.0)
    m_hi = 1.0 - m_lo

    def stack(a):
        return jnp.concatenate([a * m_lo, a * m_hi], axis=0)

    cums = [_cumsum_rows(lw) for lw in lws]
    totals = [jnp.sum(lw, axis=0, keepdims=True) for lw in lws]
    bs = [kk * al for kk, al in zip(kks, als)]
    grows = [jnp.exp(-cum) for cum in cums]
    a_s = [stack(-kk * jnp.exp(cum - lw)) for kk, cum, lw in zip(kks, cums, lws)]
    b_s = [stack(b * g) for b, g in zip(bs, grows)]
    k_s = [stack(k * g) for k, g in zip(ks, grows)]
    r_s = [stack(r * jnp.exp(cum)) for r, cum in zip(rs, cums)]
    v_s = [stack(v) for v in vs]
    pair = lambda p, q: jnp.concatenate([p, q], axis=0)
    ar_s = [pair(a, r) for a, r in zip(a_s, r_s)]
    blocks = [_quarters(_bdot(ar, pair(b, k), NT)) for ar, b, k in zip(ar_s, b_s, k_s)]
    m_ab = [jnp.where(strict, q[0], 0.0) for q in blocks]
    m_ak = [jnp.where(strict, q[1], 0.0) for q in blocks]
    m_rb = [jnp.where(incl, q[2], 0.0) for q in blocks]
    m_rk = [jnp.where(incl, q[3], 0.0) for q in blocks]
    from_state = [_halves(_bdot(ar, s, NT)) for ar, s in zip(ar_s, ss)]
    from_v = [_halves(_bdot(pair(mk, mr), v)) for mk, mr, v in zip(m_ak, m_rk, v_s)]
    u = _tri_solve(tuple(m_ab), tuple(fs[0] + fv[0] for fs, fv in zip(from_state, from_v)))
    y = [_fold_rows(fs[1] + _bdot(mb, uu) + fv[1]) for fs, mb, uu, fv in zip(from_state, m_rb, u, from_v)]
    tails = [jnp.exp(tot - cum) for tot, cum in zip(totals, cums)]
    s_new = [s * jnp.exp(tot) + _bdot(pair(uu, v), pair(stack(b * tl), stack(k * tl)), TN)
             for s, tot, uu, b, tl, v, k in zip(ss, totals, u, bs, tails, v_s, ks)]
    return tuple(y), tuple(s_new)


def _chains(bsz, npair):
    return [(b, p, slice(p * LANES, (p + 1) * LANES)) for b in range(bsz) for p in range(npair)]


def _hosted_exchange(first, last, ex_in, ex_out, sems, scatter):
    if not ex_in:
        return

    @pl.when(first)
    def _():
        _exchange_start(_exchange_copies(ex_in, ex_out, *sems, scatter, arrivals=False))

    @pl.when(last)
    def _():
        _exchange_wait(_exchange_copies(ex_in, ex_out, *sems, scatter, arrivals=True))


def _rwkv_fwd(r, lw, k, v, kk, al, hosted=(), scatter=False):
    bsz, t, w = r.shape
    npair, nchunk = w // LANES, t // CHUNK
    chains = _chains(bsz, npair)
    nex = len(hosted)

    def body(*refs):
        r_ref, lw_ref, k_ref, v_ref, kk_ref, al_ref = refs[:6]
        ex_in = refs[6:6 + nex]
        y_ref, sall_ref = refs[6 + nex:8 + nex]
        ex_out = refs[8 + nex:8 + 2 * nex]
        s_scr = refs[8 + 2 * nex]
        step = pl.program_id(0)

        @pl.when(step == 0)
        def _():
            s_scr[...] = jnp.zeros_like(s_scr)

        _hosted_exchange(step == 0, step == nchunk - 1, ex_in, ex_out, refs[9 + 2 * nex:], scatter)
        ss = tuple(s_scr[i] for i in range(len(chains)))
        for i, s in enumerate(ss):
            sall_ref[0, i] = s
        ys, s_new = _chunk_fn(ss, *[tuple(ref[b, :, cols] for b, _, cols in chains)
                                    for ref in (r_ref, lw_ref, k_ref, v_ref, kk_ref, al_ref)])
        for i, (b, _, cols) in enumerate(chains):
            y_ref[b, :, cols] = ys[i]
            s_scr[i] = s_new[i]

    spec = pl.BlockSpec((bsz, CHUNK, w), lambda c: (0, c, 0))
    hbm = pl.BlockSpec(memory_space=pltpu.HBM)
    outs = pl.pallas_call(
        body,
        grid=(nchunk,),
        in_specs=[spec] * 6 + [hbm] * nex,
        out_specs=[spec, pl.BlockSpec((1, len(chains), LANES, LANES), lambda c: (c, 0, 0, 0))] + [hbm] * nex,
        out_shape=[jax.ShapeDtypeStruct((bsz, t, w), F32),
                   jax.ShapeDtypeStruct((nchunk, len(chains), LANES, LANES), F32)]
        + _exchange_out_shapes(hosted, scatter),
        scratch_shapes=[pltpu.VMEM((len(chains), LANES, LANES), F32)] + (_exchange_scratch(nex) if nex else []),
        compiler_params=_cparams(("arbitrary",), 48),
        name="rwkv_chunk_fwd",
    )(r, lw, k, v, kk, al, *hosted)
    return outs[0], outs[1], outs[2:]


def _rwkv_bwd(r, lw, k, v, kk, al, s_all, dy, hosted=(), scatter=False):
    bsz, t, w = r.shape
    npair, nchunk = w // LANES, t // CHUNK
    chains = _chains(bsz, npair)
    nex = len(hosted)

    def body(*refs):
        r_ref, lw_ref, k_ref, v_ref, kk_ref, al_ref, s_ref, dy_ref = refs[:8]
        ex_in = refs[8:8 + nex]
        out_refs = refs[8 + nex:14 + nex]
        ex_out = refs[14 + nex:14 + 2 * nex]
        ds_scr = refs[14 + 2 * nex]
        step = pl.program_id(0)

        @pl.when(step == 0)
        def _():
            ds_scr[...] = jnp.zeros_like(ds_scr)

        _hosted_exchange(step == 0, step == nchunk - 1, ex_in, ex_out, refs[15 + 2 * nex:], scatter)
        ss = tuple(s_ref[0, i] for i in range(len(chains)))
        _, vjp = jax.vjp(_chunk_fn, ss, *[tuple(ref[b, :, cols] for b, _, cols in chains)
                                          for ref in (r_ref, lw_ref, k_ref, v_ref, kk_ref, al_ref)])
        grads = vjp((tuple(dy_ref[b, :, cols] for b, _, cols in chains),
                     tuple(ds_scr[i] for i in range(len(chains)))))
        for i, (b, _, cols) in enumerate(chains):
            ds_scr[i] = grads[0][i]
            for ref, gval in zip(out_refs, grads[1:]):
                ref[b, :, cols] = gval[i]

    spec = pl.BlockSpec((bsz, CHUNK, w), lambda c: (0, nchunk - 1 - c, 0))
    sspec = pl.BlockSpec((1, len(chains), LANES, LANES), lambda c: (nchunk - 1 - c, 0, 0, 0))
    hbm = pl.BlockSpec(memory_space=pltpu.HBM)
    outs = pl.pallas_call(
        body,
        grid=(nchunk,),
        in_specs=[spec] * 6 + [sspec, spec] + [hbm] * nex,
        out_specs=[spec] * 6 + [hbm] * nex,
        out_shape=[jax.ShapeDtypeStruct((bsz, t, w), F32)] * 6 + _exchange_out_shapes(hosted, scatter),
        scratch_shapes=[pltpu.VMEM((len(chains), LANES, LANES), F32)] + (_exchange_scratch(nex) if nex else []),
        compiler_params=_cparams(("arbitrary",), 48),
        name="rwkv_chunk_bwd",
    )(r, lw, k, v, kk, al, s_all, dy, *hosted)
    return outs[:6], outs[6:]


def _alibi_slope(head):
    return 2.0 ** (-8.0 * (head + 1) / N_ATTN_HEADS)


def _attn_block(qs, kp, kc, vp, vc, sinks, first):
    row = lax.broadcasted_iota(jnp.int32, (BLOCK, BLOCK), 0)
    col = lax.broadcasted_iota(jnp.int32, (BLOCK, BLOCK), 1)
    lane = lax.broadcasted_iota(jnp.int32, (1, LANES), 1)
    halves = [jnp.where((lane // HEAD_DIM) == half, 1.0, 0.0) for half in range(2)]
    swap = jnp.where((row + HEAD_DIM) % LANES == col, 1.0, 0.0)
    dist_c = (row - col).astype(F32)
    dist_p = dist_c + float(BLOCK)
    valid_c = row >= col
    valid_p = jnp.logical_and(col > row, jnp.logical_not(first))
    scale = HEAD_DIM ** -0.5
    stored = (kp, kc, vp, vc)
    swapped = tuple(_bdot_nn(t, swap) for t in stored)
    heads = [(pair, half) for pair in range(len(qs)) for half in range(2)]
    kv = [stored if half == pair // 2 else swapped for pair, half in heads]
    slopes = [_alibi_slope(2 * pair + half) for pair, half in heads]
    qa = [qs[pair] * halves[half] for pair, half in heads]
    sp = [jnp.where(valid_p, _bdot_nt(q, t[0]) * scale - sl * dist_p, NEG_INF) for q, t, sl in zip(qa, kv, slopes)]
    sc = [jnp.where(valid_c, _bdot_nt(q, t[1]) * scale - sl * dist_c, NEG_INF) for q, t, sl in zip(qa, kv, slopes)]
    mx = [lax.stop_gradient(jnp.maximum(jnp.maximum(jnp.max(a, axis=-1, keepdims=True),
                                                    jnp.max(b, axis=-1, keepdims=True)), sk))
          for a, b, sk in zip(sp, sc, sinks)]
    ep = [jnp.exp(a - m) for a, m in zip(sp, mx)]
    ec = [jnp.exp(b - m) for b, m in zip(sc, mx)]
    es = [jnp.exp(sk - m) for sk, m in zip(sinks, mx)]
    inv = [1.0 / (jnp.sum(a, axis=-1, keepdims=True) + jnp.sum(b, axis=-1, keepdims=True) + s)
           for a, b, s in zip(ep, ec, es)]
    o = [_bdot_nn(a * i, t[2]) + _bdot_nn(b * i, t[3]) for a, b, i, t in zip(ep, ec, inv, kv)]
    outs = tuple(o[2 * pair] * halves[0] + o[2 * pair + 1] * halves[1] for pair in range(len(qs)))
    return outs, [lax.stop_gradient(s * i) for s, i in zip(es, inv)]


def _sink_values(sink_ref):
    return [jnp.max(sink_ref[h:h + 1, :], axis=-1, keepdims=True) for h in range(N_ATTN_HEADS)]


def _attn_fwd(z, sink_rows):
    bsz, t, _ = z.shape
    nb = t // BLOCK
    npair = ATTN_WIDTH // LANES

    def body(q_ref, kp_ref, kc_ref, vp_ref, vc_ref, sink_ref, o_ref):
        first = pl.program_id(1) == 0
        qs = tuple(q_ref[0, :, pair * LANES:(pair + 1) * LANES] for pair in range(npair))
        outs, _ = _attn_block(qs, kp_ref[0], kc_ref[0], vp_ref[0], vc_ref[0], _sink_values(sink_ref), first)
        for pair in range(npair):
            o_ref[0, :, pair * LANES:(pair + 1) * LANES] = outs[pair].astype(o_ref.dtype)

    kcol, vcol = ATTN_WIDTH // KV_WIDTH, ATTN_WIDTH // KV_WIDTH + 1
    return pl.pallas_call(
        body,
        grid=(bsz, nb),
        in_specs=[pl.BlockSpec((1, BLOCK, ATTN_WIDTH), lambda b, n: (b, n, 0)),
                  pl.BlockSpec((1, BLOCK, KV_WIDTH), lambda b, n: (b, jnp.maximum(n - 1, 0), kcol)),
                  pl.BlockSpec((1, BLOCK, KV_WIDTH), lambda b, n: (b, n, kcol)),
                  pl.BlockSpec((1, BLOCK, KV_WIDTH), lambda b, n: (b, jnp.maximum(n - 1, 0), vcol)),
                  pl.BlockSpec((1, BLOCK, KV_WIDTH), lambda b, n: (b, n, vcol)),
                  pl.BlockSpec(sink_rows.shape, lambda b, n: (0, 0))],
        out_specs=pl.BlockSpec((1, BLOCK, ATTN_WIDTH), lambda b, n: (b, n, 0)),
        out_shape=jax.ShapeDtypeStruct((bsz, t, ATTN_WIDTH), BF16),
        compiler_params=_cparams(("parallel", "arbitrary"), 48),
        name="swa_fwd",
    )(z, z, z, z, z, sink_rows)


def _attn_bwd(z, dout, sink_rows):
    bsz, t, _ = z.shape
    nb = t // BLOCK
    npair = ATTN_WIDTH // LANES

    def body(q_ref, kp_ref, kc_ref, vp_ref, vc_ref, do_ref, sink_ref, dz_ref, dsink_ref, carry):
        step = pl.program_id(1)
        n = nb - 1 - step
        first = n == 0

        @pl.when(step == 0)
        def _():
            carry[...] = jnp.zeros_like(carry)

        @pl.when(jnp.logical_and(step == 0, pl.program_id(0) == 0))
        def _():
            dsink_ref[...] = jnp.zeros_like(dsink_ref)

        lane = lax.broadcasted_iota(jnp.int32, (1, LANES), 1)
        qs = tuple(q_ref[0, :, pair * LANES:(pair + 1) * LANES] for pair in range(npair))
        dos = tuple(do_ref[0, :, pair * LANES:(pair + 1) * LANES] for pair in range(npair))
        fn = functools.partial(_attn_block, sinks=_sink_values(sink_ref), first=first)
        outs, vjp, psinks = jax.vjp(fn, qs, kp_ref[0], kc_ref[0], vp_ref[0], vc_ref[0], has_aux=True)
        dqs, dkp, dkc, dvp, dvc = vjp(dos)
        for pair in range(npair):
            dz_ref[0, :, pair * LANES:(pair + 1) * LANES] = dqs[pair].astype(dz_ref.dtype)
            for half in range(2):
                m = jnp.where((lane // HEAD_DIM) == half, 1.0, 0.0)
                delta = jnp.sum(dos[pair] * outs[pair] * m, axis=-1, keepdims=True)
                head = 2 * pair + half
                ds = -jnp.sum(psinks[head] * delta, axis=0, keepdims=True)
                dsink_ref[head:head + 1, :] += jnp.broadcast_to(ds, (1, LANES))
        dz_ref[0, :, ATTN_WIDTH:ATTN_WIDTH + KV_WIDTH] = (dkc + carry[0]).astype(dz_ref.dtype)
        dz_ref[0, :, ATTN_WIDTH + KV_WIDTH:QKV_WIDTH] = (dvc + carry[1]).astype(dz_ref.dtype)
        carry[0] = dkp
        carry[1] = dvp

    kcol, vcol = ATTN_WIDTH // KV_WIDTH, ATTN_WIDTH // KV_WIDTH + 1
    rev = lambda n: nb - 1 - n
    return pl.pallas_call(
        body,
        grid=(bsz, nb),
        in_specs=[pl.BlockSpec((1, BLOCK, ATTN_WIDTH), lambda b, n: (b, rev(n), 0)),
                  pl.BlockSpec((1, BLOCK, KV_WIDTH), lambda b, n: (b, jnp.maximum(rev(n) - 1, 0), kcol)),
                  pl.BlockSpec((1, BLOCK, KV_WIDTH), lambda b, n: (b, rev(n), kcol)),
                  pl.BlockSpec((1, BLOCK, KV_WIDTH), lambda b, n: (b, jnp.maximum(rev(n) - 1, 0), vcol)),
                  pl.BlockSpec((1, BLOCK, KV_WIDTH), lambda b, n: (b, rev(n), vcol)),
                  pl.BlockSpec((1, BLOCK, ATTN_WIDTH), lambda b, n: (b, rev(n), 0)),
                  pl.BlockSpec(sink_rows.shape, lambda b, n: (0, 0))],
        out_specs=[pl.BlockSpec((1, BLOCK, QKV_WIDTH), lambda b, n: (b, rev(n), 0)),
                   pl.BlockSpec((N_ATTN_HEADS, LANES), lambda b, n: (0, 0))],
        out_shape=[jax.ShapeDtypeStruct((bsz, t, QKV_WIDTH), BF16),
                   jax.ShapeDtypeStruct((N_ATTN_HEADS, LANES), F32)],
        scratch_shapes=[pltpu.VMEM((2, BLOCK, KV_WIDTH), F32)],
        compiler_params=_cparams(("arbitrary", "arbitrary"), 48),
        name="swa_bwd",
    )(z, z, z, z, z, dout, sink_rows)


def _exchange_out_shapes(arrays, scatter):
    return [jax.ShapeDtypeStruct((N_DEV,) + (a.shape[1:] if scatter else a.shape), a.dtype) for a in arrays]


def _exchange_scratch(n):
    return [pltpu.SemaphoreType.DMA((n, N_DEV - 1)), pltpu.SemaphoreType.DMA((n, N_DEV - 1)),
            pltpu.SemaphoreType.DMA((n,))]


def _exchange_copies(ins, outs, send_sems, recv_sems, local_sems, scatter, arrivals=True):
    x, y, c = lax.axis_index("x"), lax.axis_index("y"), lax.axis_index("c")
    me = 4 * x + 2 * y + c
    copies = []
    for i in range(len(ins)):
        own = pltpu.make_async_copy(ins[i].at[me] if scatter else ins[i], outs[i].at[me], local_sems.at[i])
        copies.append((own, None, True))
        for d in range(1, N_DEV):
            px = 1 - x if d & 4 else x
            py = 1 - y if d & 2 else y
            pc = 1 - c if d & 1 else c
            peer = 4 * px + 2 * py + pc
            src = ins[i].at[peer] if scatter else ins[i]
            send = pltpu.make_async_remote_copy(src, outs[i].at[me], send_sems.at[i, d - 1], recv_sems.at[i, d - 1],
                                                device_id=(px, py, pc), device_id_type=MESH)
            recv = pltpu.make_async_remote_copy(src, outs[i].at[peer], send_sems.at[i, d - 1], recv_sems.at[i, d - 1],
                                                device_id=(px, py, pc), device_id_type=MESH) if arrivals else None
            copies.append((send, recv, False))
    return copies


def _exchange_start(copies):
    for send, _, _ in copies:
        send.start()


def _exchange_wait(copies):
    for send, recv, local in copies:
        if local:
            send.wait()
        else:
            send.wait_send()
            recv.wait_recv()


def _exchange(arrays, *, scatter, name):
    n = len(arrays)

    def body(*refs):
        copies = _exchange_copies(refs[:n], refs[n:2 * n], *refs[2 * n:], scatter)
        _exchange_start(copies)
        _exchange_wait(copies)

    hbm = pl.BlockSpec(memory_space=pltpu.HBM)
    return pl.pallas_call(
        body,
        in_specs=[hbm] * n,
        out_specs=[hbm] * n,
        out_shape=_exchange_out_shapes(arrays, scatter),
        scratch_shapes=_exchange_scratch(n),
        name=name,
    )(*arrays)


def _adamw(parts, w, m, v, name):
    rows, cols = w.shape
    tr = _pick(rows, (256, 128, 64, 8))
    c1 = 1.0 / (1.0 - ADAM_B1 ** ADAM_STEP)
    c2 = 1.0 / (1.0 - ADAM_B2 ** ADAM_STEP)

    def body(p_ref, w_ref, m_ref, v_ref, g_ref, d_ref, mo_ref, vo_ref):
        g = p_ref[0].astype(F32)
        for s in range(1, N_DEV):
            g = g + p_ref[s].astype(F32)
        mn = ADAM_B1 * m_ref[...] + (1.0 - ADAM_B1) * g
        vn = ADAM_B2 * v_ref[...] + (1.0 - ADAM_B2) * (g * g)
        g_ref[...] = g
        mo_ref[...] = mn
        vo_ref[...] = vn
        d_ref[...] = -ADAM_LR * ((mn * c1) / (jnp.sqrt(vn * c2) + ADAM_EPS) + ADAM_WD * w_ref[...])

    spec = pl.BlockSpec((tr, cols), lambda i: (i, 0))
    return pl.pallas_call(
        body,
        grid=(rows // tr,),
        in_specs=[pl.BlockSpec((N_DEV, tr, cols), lambda i: (0, i, 0)), spec, spec, spec],
        out_specs=[spec] * 4,
        out_shape=[jax.ShapeDtypeStruct((rows, cols), F32)] * 4,
        compiler_params=_cparams(("parallel",), 48),
        name=name,
    )(parts, w, m, v)


_VECTOR_PARAMS = ("attn_norm_g", "attn_sinks", "rwkv_mu", "w0", "a0", "k_k", "k_a", "r_k", "ln_x_w", "ln_x_b",
                  "mlp_norm_g", "final_norm_g")
_WEIGHT_NAMES = ("attn_norm_g", "w_in", "attn_sinks", "rwkv_mu", "w0", "w2", "a0", "a2", "g2", "k_k", "k_a", "r_k",
                 "ln_x_w", "ln_x_b", "w_out", "mlp_norm_g", "w_up", "w_down", "final_norm_g")


def _pack_vectors(vals):
    pieces = []
    for name in _VECTOR_PARAMS:
        flat = vals[name].reshape(1, -1)
        pad = (-flat.shape[1]) % LANES
        pieces.append(jnp.pad(flat, ((0, 0), (0, pad))) if pad else flat)
    return jnp.concatenate(pieces, axis=1)


def _unpack_vectors(packed, like):
    out, col = {}, 0
    for name in _VECTOR_PARAMS:
        size = like[name].size
        out[name] = packed[0, col:col + size].reshape(like[name].shape)
        col += size + (-size) % LANES
    return out


def kernel(x, attn_norm_g, w_in, attn_sinks, rwkv_mu, w0, w2, a0, a2, g2, k_k, k_a, r_k, ln_x_w, ln_x_b, w_out, mlp_norm_g, w_up, w_down, final_norm_g, loss_target, m_attn_norm_g, m_w_in, m_attn_sinks, m_rwkv_mu, m_w0, m_w2, m_a0, m_a2, m_g2, m_k_k, m_k_a, m_r_k, m_ln_x_w, m_ln_x_b, m_w_out, m_mlp_norm_g, m_w_up, m_w_down, m_final_norm_g, v_attn_norm_g, v_w_in, v_attn_sinks, v_rwkv_mu, v_w0, v_w2, v_a0, v_a2, v_g2, v_k_k, v_k_a, v_r_k, v_ln_x_w, v_ln_x_b, v_w_out, v_mlp_norm_g, v_w_up, v_w_down, v_final_norm_g):
    weights = dict(attn_norm_g=attn_norm_g, w_in=w_in, attn_sinks=attn_sinks, rwkv_mu=rwkv_mu, w0=w0, w2=w2, a0=a0,
                   a2=a2, g2=g2, k_k=k_k, k_a=k_a, r_k=r_k, ln_x_w=ln_x_w, ln_x_b=ln_x_b, w_out=w_out,
                   mlp_norm_g=mlp_norm_g, w_up=w_up, w_down=w_down, final_norm_g=final_norm_g)
    mom1 = dict(attn_norm_g=m_attn_norm_g, w_in=m_w_in, attn_sinks=m_attn_sinks, rwkv_mu=m_rwkv_mu, w0=m_w0, w2=m_w2,
                a0=m_a0, a2=m_a2, g2=m_g2, k_k=m_k_k, k_a=m_k_a, r_k=m_r_k, ln_x_w=m_ln_x_w, ln_x_b=m_ln_x_b,
                w_out=m_w_out, mlp_norm_g=m_mlp_norm_g, w_up=m_w_up, w_down=m_w_down, final_norm_g=m_final_norm_g)
    mom2 = dict(attn_norm_g=v_attn_norm_g, w_in=v_w_in, attn_sinks=v_attn_sinks, rwkv_mu=v_rwkv_mu, w0=v_w0, w2=v_w2,
                a0=v_a0, a2=v_a2, g2=v_g2, k_k=v_k_k, k_a=v_k_a, r_k=v_r_k, ln_x_w=v_ln_x_w, ln_x_b=v_ln_x_b,
                w_out=v_w_out, mlp_norm_g=v_mlp_norm_g, w_up=v_w_up, w_down=v_w_down, final_norm_g=v_final_norm_g)
    bsz, seq, d_model = x.shape
    rows = bsz * seq
    d_in = N_DEV * w_in.shape[2]
    d_ff = N_DEV * w_up.shape[2]

    gathered = _exchange([w_in[0].astype(BF16), w2[0], a2[0], g2[0]], scatter=False, name="gather_in_weights")
    cols_first = lambda a: a.transpose(1, 0, 2).reshape(a.shape[1], -1)
    w_in_f = cols_first(gathered[0])
    w_attn, w_rw = w_in_f[:, :QKV_WIDTH], w_in_f[:, QKV_WIDTH:]
    w2_f, a2_f, g2_f = cols_first(gathered[1]), cols_first(gathered[2]), cols_first(gathered[3])
    lora = w2_f.shape[0]
    w2p = jnp.concatenate([w2_f, jnp.zeros_like(a2_f)], axis=0)
    a2p = jnp.concatenate([jnp.zeros_like(w2_f), a2_f], axis=0)

    esum = _head_sum_matrix()
    sink_rows = jnp.broadcast_to(attn_sinks.reshape(N_ATTN_HEADS, 1), (N_ATTN_HEADS, LANES))
    prep_pars = [w0, w2p, a0, a2p, g2_f, k_k, k_a, esum]
    post_pars = [ln_x_w, ln_x_b, r_k, esum]

    x2d = x.reshape(rows, d_model)
    h1 = _norm_fwd(x2d, attn_norm_g, "attn_norm_fwd")
    z_attn = _matmul(h1, w_attn, "nn", name="in_proj_attn")
    z_rw = _matmul(h1, w_rw, "nn", name="in_proj_rwkv")
    z_attn3 = z_attn.reshape(bsz, seq, QKV_WIDTH)
    attn_out = _attn_fwd(z_attn3, sink_rows)
    r, lw, k, v, kk, al, gate = _prep_fwd(z_rw, seq, rwkv_mu, prep_pars)
    as3 = lambda a: a.reshape(bsz, seq, RWKV_WIDTH)
    y, s_all, late = _rwkv_fwd(as3(r), as3(lw), as3(k), as3(v), as3(kk), as3(al),
                               hosted=[w_out[0].astype(BF16), w_up[0].astype(BF16), w_down[0].astype(BF16)])
    w_out_f = late[0].reshape(-1, d_model)
    w_up_f = cols_first(late[1])
    w_down_f = late[2].reshape(-1, d_model)
    y2 = y.reshape(rows, RWKV_WIDTH)
    rw_out = _post_fwd(y2, r, k, v, gate, post_pars)
    mix = jnp.concatenate([attn_out.reshape(rows, ATTN_WIDTH), rw_out], axis=1)
    residual = lambda acc, res: (acc + res,)
    x1 = _matmul(mix, w_out_f, "nn", name="out_proj", extras=(x2d,), epilogue=residual)
    h2 = _norm_fwd(x1, mlp_norm_g, "mlp_norm_fwd")

    def relu_sq(acc):
        pos = jnp.maximum(acc, 0.0)
        return acc, pos * pos

    u, act = _matmul(h2, w_up_f, "nn", name="mlp_up", epilogue=relu_sq, out_dtypes=(BF16, BF16))
    x2 = _matmul(act, w_down_f, "nn", name="mlp_down", extras=(x1,), epilogue=residual)
    dx2, loss_vec, g_final = _final_loss(x2, loss_target.reshape(rows, d_model), final_norm_g.reshape(1, d_model))

    g_w_down = _matmul(act, dx2, "tn", name="grad_w_down", out_dtypes=(BF16,))
    du = _matmul(dx2, w_down_f, "nt", name="mlp_down_bwd", extras=(u,), out_dtypes=(BF16,),
                 epilogue=lambda acc, uv: (acc * (2.0 * jnp.maximum(uv.astype(F32), 0.0)),))
    g_w_up = _matmul(h2, du, "tn", name="grad_w_up", out_dtypes=(BF16,))
    dh2 = _matmul(du, w_up_f, "nt", name="mlp_up_bwd")
    dx1, g_mlp_norm = _norm_bwd(x1, dh2, dx2, mlp_norm_g, "mlp_norm_bwd")
    g_w_out = _matmul(mix, dx1, "tn", name="grad_w_out", out_dtypes=(BF16,))
    dmix = _matmul(dx1, w_out_f, "nt", name="out_proj_bwd")
    dy, dr_a, dk_a, dv_a, dgate, g_ln_w, g_ln_b, g_r_k = _post_bwd(
        y2, r, k, v, gate, (dmix, RWKV_WIDTH, ATTN_WIDTH // RWKV_WIDTH), post_pars)
    by_cols = lambda a: a.reshape(a.shape[0], N_DEV, -1).transpose(1, 0, 2)
    (dr_b, dlw, dk_b, dv_b, dkk, dal), (p_w_out, p_w_up, p_w_down) = _rwkv_bwd(
        as3(r), as3(lw), as3(k), as3(v), as3(kk), as3(al), s_all, as3(dy),
        hosted=[g_w_out.reshape(N_DEV, -1, d_model), by_cols(g_w_up), g_w_down.reshape(N_DEV, -1, d_model)],
        scatter=True)
    flat = lambda a: a.reshape(rows, RWKV_WIDTH)
    (dz_rw, gmu_r, gmu_k, gmu_v, gmu_wa, gmu_g, g_w0, g_w2p, g_a0, g_a2p, g_g2, g_k_k, g_k_a) = _prep_bwd(
        z_rw, seq, [dr_a, flat(dr_b), flat(dlw), dk_a, flat(dk_b), dv_a, flat(dv_b), flat(dkk), flat(dal), dgate],
        rwkv_mu, prep_pars)
    dz_attn, g_sink_rows = _attn_bwd(z_attn3, dmix.reshape(bsz, seq, d_model), sink_rows)
    dz_attn = dz_attn.reshape(rows, QKV_WIDTH)
    g_w_in = jnp.concatenate([_matmul(h1, dz_attn, "tn", name="grad_w_in_attn", out_dtypes=(BF16,)),
                              _matmul(h1, dz_rw, "tn", name="grad_w_in_rwkv", out_dtypes=(BF16,))], axis=1)
    dh1 = _matmul(dz_attn, w_attn, "nt", name="in_proj_attn_bwd")
    lora_grads = jnp.concatenate([g_w2p[:lora], g_a2p[lora:], g_g2], axis=0)
    (dh1,), (p_w_in, p_lora) = _matmul(dz_rw, w_rw, "nt", name="in_proj_rwkv_bwd", extras=(dh1,), epilogue=residual,
                                       hosted=[by_cols(g_w_in), by_cols(lora_grads)], scatter=True)
    dx, g_attn_norm = _norm_bwd(x2d, dh1, dx1, attn_norm_g, "attn_norm_bwd")

    vec_grads = dict(attn_norm_g=g_attn_norm, attn_sinks=g_sink_rows[:, 0], rwkv_mu=jnp.concatenate(
        [gmu_r, gmu_k, gmu_v, gmu_wa, gmu_g], axis=1), w0=g_w0, a0=g_a0, k_k=g_k_k, k_a=g_k_a, r_k=g_r_k,
        ln_x_w=g_ln_w, ln_x_b=g_ln_b, mlp_norm_g=g_mlp_norm, final_norm_g=g_final)
    packed = _pack_vectors(vec_grads)
    nvec = packed.shape[1]
    everyone = _exchange([jnp.concatenate([packed, loss_vec], axis=1)], scatter=False, name="gather_vector_grads")[0]
    vec_parts = everyone[:, :, :nvec]
    loss = jnp.sum(everyone[:, 0, nvec])

    grads, delta, new_m, new_v = {}, {}, {}, {}

    def update(name, part, shape2d):
        res = _adamw(part, weights[name].reshape(shape2d), mom1[name].reshape(shape2d), mom2[name].reshape(shape2d),
                     "adamw_" + name)
        for store, val in zip((grads, delta, new_m, new_v), res):
            store[name] = val.reshape(weights[name].shape)

    update("w_in", p_w_in, w_in.shape[1:])
    update("w_out", p_w_out, w_out.shape[1:])
    update("w_up", p_w_up, w_up.shape[1:])
    update("w_down", p_w_down, w_down.shape[1:])
    stack = lambda d: jnp.concatenate([d["w2"][0], d["a2"][0], d["g2"][0]], axis=0)
    lora_res = _adamw(p_lora, stack(weights), stack(mom1), stack(mom2), "adamw_lora")
    for store, val in zip((grads, delta, new_m, new_v), lora_res):
        store["w2"], store["a2"], store["g2"] = val[None, :lora], val[None, lora:2 * lora], val[None, 2 * lora:]
    vec_res = _adamw(vec_parts, _pack_vectors(weights), _pack_vectors(mom1), _pack_vectors(mom2), "adamw_vectors")
    for store, val in zip((grads, delta, new_m, new_v), vec_res):
        store.update(_unpack_vectors(val, weights))

    return (loss, dx.reshape(x.shape), *[grads[n] for n in _WEIGHT_NAMES], *[delta[n] for n in _WEIGHT_NAMES],
            *[new_m[n] for n in _WEIGHT_NAMES], *[new_v[n] for n in _WEIGHT_NAMES])
```

```python
import functools
import math

import jax
import jax.numpy as jnp
from jax import lax
from jax.experimental import pallas as pl
from jax.experimental.pallas import tpu as pltpu

F32 = jnp.float32
BF16 = jnp.bfloat16

N_DEV = 8
HEAD_DIM = 64
LANES = 128
N_ATTN_HEADS = 8
ATTN_WIDTH = 512
KV_WIDTH = 128
QKV_WIDTH = ATTN_WIDTH + 2 * KV_WIDTH
RWKV_WIDTH = 512
LORA_WA = 128
GATE_LORA = 128
RWKV_SHIFT_WIDTH = 3 * RWKV_WIDTH + LORA_WA + GATE_LORA
BLOCK = 128
CHUNK = 64
RMS_EPS = 1e-6
GN_EPS = 64e-5
L2_EPS = 1e-12
NEG_INF = -1e30
DECAY_SCALE = math.exp(-0.5)
ADAM_LR, ADAM_B1, ADAM_B2, ADAM_EPS, ADAM_WD, ADAM_STEP = 0.001, 0.9, 0.999, 1e-08, 0.01, 10

NN = (((1,), (0,)), ((), ()))
NT = (((1,), (1,)), ((), ()))
TN = (((0,), (0,)), ((), ()))
MESH = pl.DeviceIdType.MESH


def _dot(a, b, dn=NN, precision=None):
    return lax.dot_general(a, b, dn, precision=precision, preferred_element_type=F32)


def _bdot_raw(a, b, dn):
    return lax.dot_general(a.astype(BF16), b.astype(BF16), dn, preferred_element_type=F32)


@functools.partial(jax.custom_vjp, nondiff_argnums=(2, 3))
def _bdot_c(a, b, ca, cb):
    return _bdot_raw(a, b, (((ca,), (cb,)), ((), ())))


def _bdot_c_fwd(a, b, ca, cb):
    return _bdot_c(a, b, ca, cb), (a, b)


def _bdot_c_bwd(ca, cb, res, ct):
    a, b = res
    fa, fb = 1 - ca, 1 - cb
    da = _bdot_raw(ct, b, (((1,), (fb,)), ((), ()))) if ca == 1 else _bdot_raw(b, ct, (((fb,), (1,)), ((), ())))
    db = _bdot_raw(a, ct, (((fa,), (0,)), ((), ()))) if cb == 0 else _bdot_raw(ct, a, (((0,), (fa,)), ((), ())))
    return da, db


_bdot_c.defvjp(_bdot_c_fwd, _bdot_c_bwd)


def _bdot(a, b, dn=NN):
    return _bdot_c(a, b, dn[0][0][0], dn[0][1][0])


def _bdot_nn(a, b):
    return _bdot(a, b, NN)


def _bdot_nt(a, b):
    return _bdot(a, b, NT)


def _split3(x):
    hi = x.astype(BF16)
    rest = x - hi.astype(F32)
    mid = rest.astype(BF16)
    return hi, mid, (rest - mid.astype(F32)).astype(BF16)


def _running_sum(x, dn):
    c = x.shape[0]
    row = lax.broadcasted_iota(jnp.int32, (c, c), 0)
    col = lax.broadcasted_iota(jnp.int32, (c, c), 1)
    tri = jnp.where(row >= col, 1.0, 0.0).astype(BF16)
    w = x.shape[1]
    parts = lax.dot_general(tri, jnp.concatenate(_split3(x), axis=1), dn, preferred_element_type=F32)
    return parts[:, :w] + parts[:, w:2 * w] + parts[:, 2 * w:]


@jax.custom_vjp
def _cumsum_rows(x):
    return _running_sum(x, NN)


_cumsum_rows.defvjp(lambda x: (_running_sum(x, NN), None), lambda _, ct: (_running_sum(ct, TN),))


@jax.custom_vjp
def _fold_rows(x):
    c = x.shape[0] // 2
    return x[:c] + x[c:]


_fold_rows.defvjp(lambda x: (_fold_rows(x), None), lambda _, ct: (jnp.concatenate([ct, ct], axis=0),))


@jax.custom_vjp
def _halves(x):
    n = x.shape[0] // 2
    return x[:n], x[n:]


_halves.defvjp(lambda x: (_halves(x), None), lambda _, cts: (jnp.concatenate(cts, axis=0),))


@jax.custom_vjp
def _quarters(x):
    n = x.shape[0] // 2
    return x[:n, :n], x[:n, n:], x[n:, :n], x[n:, n:]


_quarters.defvjp(lambda x: (_quarters(x), None),
                 lambda _, cts: (jnp.concatenate([jnp.concatenate(cts[:2], axis=1),
                                                  jnp.concatenate(cts[2:], axis=1)], axis=0),))


def _sigmoid(x):
    return 1.0 / (1.0 + jnp.exp(-x))


def _pick(n, cands):
    for c in cands:
        if n % c == 0:
            return c
    return n


def _cparams(sem, vmem_mb=None):
    kw = dict(dimension_semantics=sem)
    if vmem_mb is not None:
        kw["vmem_limit_bytes"] = vmem_mb * 1024 * 1024
    return pltpu.CompilerParams(**kw)


def _matmul(a, b, mode, *, name, extras=(), epilogue=None, out_dtypes=(F32,), tm=1024, tn=1024, tk=1024,
            hosted=(), scatter=False):
    if mode == "nn":
        (M, K), (_, N) = a.shape, b.shape
    elif mode == "tn":
        (K, M), (_, N) = a.shape, b.shape
    else:
        (M, K), (N, _) = a.shape, b.shape
    tm = _pick(M, (tm, 512, 256, 128))
    tn = _pick(N, (tn, 896, 768, 512, 384, 256, 128))
    tk = _pick(K, (tk, 512, 256, 128))
    nk = K // tk
    ne, nout = len(extras), len(out_dtypes)
    if mode == "nn":
        a_spec = pl.BlockSpec((tm, tk), lambda i, j, k: (i, k))
        b_spec = pl.BlockSpec((tk, tn), lambda i, j, k: (k, j))
        dn = NN
    elif mode == "tn":
        a_spec = pl.BlockSpec((tk, tm), lambda i, j, k: (k, i))
        b_spec = pl.BlockSpec((tk, tn), lambda i, j, k: (k, j))
        dn = TN
    else:
        a_spec = pl.BlockSpec((tm, tk), lambda i, j, k: (i, k))
        b_spec = pl.BlockSpec((tn, tk), lambda i, j, k: (j, k))
        dn = NT
    o_spec = pl.BlockSpec((tm, tn), lambda i, j, k: (i, j))
    grid = (M // tm, N // tn, nk)
    nex = len(hosted)

    def body(*refs):
        a_ref, b_ref = refs[:2]
        e_refs = refs[2:2 + ne]
        ex_in = refs[2 + ne:2 + ne + nex]
        o_refs = refs[2 + ne + nex:2 + ne + nex + nout]
        ex_out = refs[2 + ne + nex + nout:2 + ne + 2 * nex + nout]
        scratch = refs[2 + ne + 2 * nex + nout:]
        kstep = pl.program_id(2)
        if nex:
            at = [pl.program_id(d) for d in range(3)]
            first = jnp.logical_and(jnp.logical_and(at[0] == 0, at[1] == 0), at[2] == 0)
            last = jnp.logical_and(jnp.logical_and(at[0] == grid[0] - 1, at[1] == grid[1] - 1), at[2] == grid[2] - 1)
            _hosted_exchange(first, last, ex_in, ex_out, scratch[-3:], scatter)

        def finish(total):
            outs = (total,) if epilogue is None else epilogue(total, *[e[...] for e in e_refs])
            for o_ref, o in zip(o_refs, outs):
                o_ref[...] = o.astype(o_ref.dtype)

        if nk == 1:
            finish(_bdot_raw(a_ref[...], b_ref[...], dn))
            return
        acc = scratch[0]

        @pl.when(kstep == 0)
        def _():
            acc[...] = jnp.zeros_like(acc)

        acc[...] += _bdot_raw(a_ref[...], b_ref[...], dn)

        @pl.when(kstep == nk - 1)
        def _():
            finish(acc[...])

    hbm = pl.BlockSpec(memory_space=pltpu.HBM)
    outs = pl.pallas_call(
        body,
        grid=grid,
        in_specs=[a_spec, b_spec] + [o_spec] * ne + [hbm] * nex,
        out_specs=[o_spec] * nout + [hbm] * nex,
        out_shape=[jax.ShapeDtypeStruct((M, N), dt) for dt in out_dtypes] + _exchange_out_shapes(hosted, scatter),
        scratch_shapes=([pltpu.VMEM((tm, tn), F32)] if nk > 1 else []) + (_exchange_scratch(nex) if nex else []),
        compiler_params=_cparams(("arbitrary",) * 3 if nex else ("parallel", "parallel", "arbitrary"), 56),
        name=name,
    )(a, b, *extras, *hosted)
    if nex:
        return outs[:nout], outs[nout:]
    return outs[0] if nout == 1 else outs


def _rowwise(fn, rows, pars, out_rows, out_accs, *, tile, name, nsub=1, hosted=(), scatter=False):
    rows = [r if isinstance(r, tuple) else (r, r.shape[1], 0) for r in rows]
    R = rows[0][0].shape[0]
    tile = min(tile, R)
    chunk = tile // nsub
    ntile = R // tile
    nr, npar, nor, noa, nex = len(rows), len(pars), len(out_rows), len(out_accs), len(hosted)

    def body(*refs):
        rin = refs[:nr]
        pin = refs[nr:nr + npar]
        ex_in = refs[nr + npar:nr + npar + nex]
        orow = refs[nr + npar + nex:nr + npar + nex + nor]
        oacc = refs[nr + npar + nex + nor:nr + npar + nex + nor + noa]
        ex_out = refs[nr + npar + nex + nor + noa:nr + npar + 2 * nex + nor + noa]
        step = pl.program_id(0)
        _hosted_exchange(step == 0, step == ntile - 1, ex_in, ex_out, refs[nr + npar + 2 * nex + nor + noa:], scatter)
        pvals = [p[...] for p in pin]
        totals = []
        for sub in range(nsub):
            at = slice(sub * chunk, (sub + 1) * chunk)
            outs = fn(*[r[at, :] for r in rin], *pvals)
            for ref, o in zip(orow, outs[:nor]):
                if isinstance(o, (tuple, list)):
                    col = 0
                    for piece in o:
                        ref[at, col:col + piece.shape[1]] = piece.astype(ref.dtype)
                        col += piece.shape[1]
                else:
                    ref[at, :] = o.astype(ref.dtype)
            accs = list(outs[nor:])
            totals = accs if sub == 0 else [t + a for t, a in zip(totals, accs)]

        def accumulate(ref, o):
            @pl.when(step == 0)
            def _():
                ref[...] = o

            @pl.when(step > 0)
            def _():
                ref[...] += o

        for ref, o in zip(oacc, totals):
            accumulate(ref, o)

    def colspec(width, cb):
        return pl.BlockSpec((tile, width), lambda i: (i, cb))

    hbm = pl.BlockSpec(memory_space=pltpu.HBM)
    outs = pl.pallas_call(
        body,
        grid=(ntile,),
        in_specs=[colspec(w, cb) for (_, w, cb) in rows]
        + [pl.BlockSpec(p.shape, lambda i: (0, 0)) for p in pars] + [hbm] * nex,
        out_specs=[colspec(w, 0) for (w, _) in out_rows]
        + [pl.BlockSpec(s, lambda i: (0, 0)) for s in out_accs] + [hbm] * nex,
        out_shape=[jax.ShapeDtypeStruct((R, w), dt) for (w, dt) in out_rows]
        + [jax.ShapeDtypeStruct(s, F32) for s in out_accs] + _exchange_out_shapes(hosted, scatter),
        scratch_shapes=_exchange_scratch(nex) if nex else [],
        compiler_params=_cparams(("arbitrary",), 56),
        name=name,
    )(*[r[0] for r in rows], *pars, *hosted)
    return (outs[:nor + noa], outs[nor + noa:]) if nex else outs


def _rms_fn(x, g):
    return x * lax.rsqrt(jnp.mean(x * x, axis=-1, keepdims=True) + RMS_EPS) * g


FUSED_TILE = 512
FUSED_CHUNKS = 2


def _norm_in_proj(x, g, w_attn, w_rw):
    def fn(xv, gv, wa, wr):
        h = _rms_fn(xv, gv)
        return h, _bdot_raw(h, wa, NN), _bdot_raw(h, wr, NN)

    return _rowwise(fn, [x], [g, w_attn, w_rw], [(x.shape[1], BF16), (w_attn.shape[1], F32), (w_rw.shape[1], F32)], [],
                    tile=FUSED_TILE, nsub=FUSED_CHUNKS, name="attn_norm_in_proj")


def _out_proj_norm(mix, w_out, x, g):
    def fn(mv, xv, wv, gv):
        x1 = xv + _bdot_raw(mv, wv, NN)
        return x1, _rms_fn(x1, gv)

    return _rowwise(fn, [mix, x], [w_out, g], [(x.shape[1], F32), (x.shape[1], BF16)], [],
                    tile=FUSED_TILE, nsub=FUSED_CHUNKS, name="out_proj_mlp_norm")


def _down_proj_loss(act, w_down, x1, tgt, g):
    d = x1.shape[1]

    def fn(av, xv, tv, wv, gv):
        x2 = xv + _bdot_raw(av, wv, NN)
        y, vjp = jax.vjp(_rms_fn, x2, gv)
        err = y - tv
        loss = 0.5 * jnp.sum(jnp.sum(err * err, axis=-1, keepdims=True), axis=0, keepdims=True) / d
        dx, dg = vjp(err / d)
        return dx, jnp.broadcast_to(loss, (1, LANES)), dg

    return _rowwise(fn, [act, x1, tgt], [w_down, g], [(d, F32)], [(1, LANES), g.shape],
                    tile=FUSED_TILE, nsub=FUSED_CHUNKS, name="mlp_down_final_norm_loss")


def _proj_bwd_norm_bwd(cts, weights_t, x, dres, g, name, hosted=(), scatter=False):
    n = len(cts)

    def fn(*vals):
        ctv, (xv, dresv), wv, gv = vals[:n], vals[n:n + 2], vals[n + 2:2 * n + 2], vals[-1]
        dh = _bdot_raw(ctv[0], wv[0], NT)
        for c, w in zip(ctv[1:], wv[1:]):
            dh = dh + _bdot_raw(c, w, NT)
        _, vjp = jax.vjp(_rms_fn, xv, gv)
        dx, dg = vjp(dh)
        return dx + dresv, dg

    return _rowwise(fn, [*cts, x, dres], [*weights_t, g], [(x.shape[1], F32)], [g.shape],
                    tile=FUSED_TILE, nsub=FUSED_CHUNKS, name=name, hosted=hosted, scatter=scatter)


def _head_sum_matrix():
    i = lax.broadcasted_iota(jnp.int32, (RWKV_WIDTH, RWKV_WIDTH), 0) // HEAD_DIM
    j = lax.broadcasted_iota(jnp.int32, (RWKV_WIDTH, RWKV_WIDTH), 1) // HEAD_DIM
    return (i == j).astype(BF16)


def _head_sums_raw(x, esum):
    hi = x.astype(BF16)
    lo = (x - hi.astype(F32)).astype(BF16)
    return _dot(hi, esum) + _dot(lo, esum)


@jax.custom_vjp
def _head_sums(x, esum):
    return _head_sums_raw(x, esum)


_head_sums.defvjp(lambda x, esum: (_head_sums_raw(x, esum), esum),
                  lambda esum, ct: (_head_sums_raw(ct, esum), jnp.zeros_like(esum)))


def _prep_core(xr, xk, xv, xwa, xg, w0, w2p, a0, a2p, g2, k_k, k_a, esum):
    lw = -DECAY_SCALE * _sigmoid(w0 + _bdot_nn(jnp.tanh(xwa), w2p))
    a = _sigmoid(a0 + _bdot_nn(xwa, a2p))
    g = _bdot_nn(_sigmoid(xg), g2)
    kk0 = xk * k_k
    kk = kk0 / jnp.maximum(jnp.sqrt(_head_sums(kk0 * kk0, esum)), L2_EPS)
    k = xk * (1.0 + (a - 1.0) * k_a)
    return xr, lw, k, xv, kk, a, g


_SEGS = ((0, 512), (512, 1024), (1024, 1536), (1536, 1664), (1664, 1792))


PREP_TILE = 256
SUBLANES = 8


def _shifted_tokens(z_ref, zprev_ref, tile_index, seq):
    zc = z_ref[...]
    start = (tile_index * PREP_TILE) % seq == 0
    before = jnp.where(start, 0.0, zprev_ref[SUBLANES - 1:SUBLANES, :])
    rowid = lax.broadcasted_iota(jnp.int32, zc.shape, 0)
    return zc, jnp.where(rowid == 0, before, pltpu.roll(zc, 1, 0))


def _prep_specs(z, mu, pars, index):
    width = z.shape[1]
    per = PREP_TILE // SUBLANES
    return ([pl.BlockSpec((PREP_TILE, width), lambda i: (index(i), 0)),
             pl.BlockSpec((SUBLANES, width), lambda i: (jnp.maximum(index(i) * per - 1, 0), 0))],
            [pl.BlockSpec(p.shape, lambda i: (0, 0)) for p in (mu, *pars)])


def _prep_fwd(z, seq, mu, pars):
    rows = z.shape[0]
    npar = len(pars)

    def body(z_ref, zprev_ref, mu_ref, *rest):
        par_refs, out_refs = rest[:npar], rest[npar:]
        zc, zp = _shifted_tokens(z_ref, zprev_ref, pl.program_id(0), seq)
        zs = zc + (zp - zc) * mu_ref[...]
        outs = _prep_core(*[zs[:, a:b] for a, b in _SEGS], *[p[...] for p in par_refs])
        for ref, o in zip(out_refs, outs):
            ref[...] = o

    zspecs, pspecs = _prep_specs(z, mu, pars, lambda i: i)
    return pl.pallas_call(
        body,
        grid=(rows // PREP_TILE,),
        in_specs=zspecs + pspecs,
        out_specs=[pl.BlockSpec((PREP_TILE, RWKV_WIDTH), lambda i: (i, 0))] * 7,
        out_shape=[jax.ShapeDtypeStruct((rows, RWKV_WIDTH), F32)] * 7,
        compiler_params=_cparams(("parallel",), 56),
        name="rwkv_prep_fwd",
    )(z, z, mu, *pars)


def _prep_bwd(z, seq, cts, mu, pars):
    rows, width = z.shape
    ntile = rows // PREP_TILE
    npar, nct = len(pars), len(cts)
    acc_shapes = [(1, b - a) for a, b in _SEGS] + [p.shape for p in pars[:-1]]

    def body(z_ref, zprev_ref, *rest):
        ct_refs = rest[:nct]
        mu_ref = rest[nct]
        par_refs = rest[nct + 1:nct + 1 + npar]
        dz_ref = rest[nct + 1 + npar]
        acc_refs = rest[nct + 2 + npar:-1]
        carry = rest[-1]
        step = pl.program_id(0)
        tile_index = ntile - 1 - step

        @pl.when(step == 0)
        def _():
            carry[...] = jnp.zeros_like(carry)

        zc, zp = _shifted_tokens(z_ref, zprev_ref, tile_index, seq)
        mu_v = mu_ref[...]
        diff = zp - zc
        zs = zc + diff * mu_v
        dra, drb, dlw, dka, dkb, dva, dvb, dkk, da, dg = [c[...] for c in ct_refs]
        pv = [p[...] for p in par_refs]
        _, vjp = jax.vjp(lambda *args: _prep_core(*args, pv[-1]), *[zs[:, a:b] for a, b in _SEGS], *pv[:-1])
        grads = vjp((dra + drb, dlw, dka + dkb, dva + dvb, dkk, da, dg))
        dsegs, dpars = grads[:5], grads[5:]
        last_of_sequence = ((tile_index + 1) * PREP_TILE) % seq == 0
        accs = []
        for ds, (a, b) in zip(dsegs, _SEGS):
            mu_s = mu_v[:, a:b]
            dzp = ds * mu_s
            after = jnp.where(last_of_sequence, 0.0, carry[0:1, a:b])
            rowid = lax.broadcasted_iota(jnp.int32, dzp.shape, 0)
            from_next = jnp.where(rowid == PREP_TILE - 1, after, pltpu.roll(dzp, PREP_TILE - 1, 0))
            dz_ref[:, a:b] = (ds * (1.0 - mu_s) + from_next).astype(dz_ref.dtype)
            carry[:, a:b] = dzp[0:SUBLANES, :]
            accs.append(jnp.sum(ds * diff[:, a:b], axis=0, keepdims=True))
        accs.extend(dpars)

        def accumulate(ref, o):
            @pl.when(step == 0)
            def _():
                ref[...] = o

            @pl.when(step > 0)
            def _():
                ref[...] += o

        for ref, o in zip(acc_refs, accs):
            accumulate(ref, o)

    rev = lambda i: ntile - 1 - i
    zspecs, pspecs = _prep_specs(z, mu, pars, rev)
    return pl.pallas_call(
        body,
        grid=(ntile,),
        in_specs=zspecs + [pl.BlockSpec((PREP_TILE, RWKV_WIDTH), lambda i: (rev(i), 0))] * nct + pspecs,
        out_specs=[pl.BlockSpec((PREP_TILE, width), lambda i: (rev(i), 0))]
        + [pl.BlockSpec(s, lambda i: (0, 0)) for s in acc_shapes],
        out_shape=[jax.ShapeDtypeStruct((rows, width), BF16)] + [jax.ShapeDtypeStruct(s, F32) for s in acc_shapes],
        scratch_shapes=[pltpu.VMEM((SUBLANES, width), F32)],
        compiler_params=_cparams(("arbitrary",), 56),
        name="rwkv_prep_bwd",
    )(z, z, *cts, mu, *pars)


def _post_fn(y, r, k, v, g, ln_w, ln_b, r_k, esum):
    mean = _head_sums(y, esum) * (1.0 / HEAD_DIM)
    yc = y - mean
    var = _head_sums(yc * yc, esum) * (1.0 / HEAD_DIM)
    yn = yc * lax.rsqrt(var + GN_EPS) * ln_w + ln_b
    bonus = _head_sums(r * k * r_k, esum) * v
    return (yn + bonus) * g


def _post_fwd(y, r, k, v, g, pars):
    return _rowwise(lambda *a: (_post_fn(*a),), [y, r, k, v, g], pars, [(RWKV_WIDTH, BF16)], [],
                    tile=256, name="rwkv_post_fwd")[0]


def _post_bwd(y, r, k, v, g, dout, pars):
    def fn(yv, rv, kv, vv, gv, dv_, ln_w, ln_b, r_k, esum):
        _, vjp = jax.vjp(lambda *a: _post_fn(*a, esum), yv, rv, kv, vv, gv, ln_w, ln_b, r_k)
        return vjp(dv_)

    return _rowwise(fn, [y, r, k, v, g, dout], pars, [(RWKV_WIDTH, F32)] * 5, [p.shape for p in pars[:-1]],
                    tile=256, name="rwkv_post_bwd")


def _tri_inverses(ms):
    n = ms[0].shape[0]
    row = lax.broadcasted_iota(jnp.int32, (n, n), 0)
    col = lax.broadcasted_iota(jnp.int32, (n, n), 1)
    eye = jnp.where(row == col, 1.0, 0.0)
    t_inv = [eye + m for m in ms]
    power = [_bdot_raw(m, m, NN) for m in ms]
    steps = int(math.log2(n // 2)) - 1
    for step in range(steps):
        if step < steps - 1:
            both = [_bdot_raw(jnp.concatenate([p, t], axis=0), p, NN) for p, t in zip(power, t_inv)]
            power = [b[:n] for b in both]
            t_inv = [t + b[n:] for t, b in zip(t_inv, both)]
        else:
            t_inv = [t + _bdot_raw(t, p, NN) for t, p in zip(t_inv, power)]
    return t_inv


@jax.custom_vjp
def _tri_solve(ms, xs):
    return tuple(_bdot_raw(t, x, NN) for t, x in zip(_tri_inverses(ms), xs))


def _tri_solve_fwd(ms, xs):
    t_inv = _tri_inverses(ms)
    us = tuple(_bdot_raw(t, x, NN) for t, x in zip(t_inv, xs))
    return us, (tuple(t_inv), us)


def _tri_solve_bwd(res, dus):
    t_inv, us = res
    dxs = tuple(_bdot_raw(t, du, TN) for t, du in zip(t_inv, dus))
    dms = tuple(_bdot_raw(dx, u, NT) for dx, u in zip(dxs, us))
    return dms, dxs


_tri_solve.defvjp(_tri_solve_fwd, _tri_solve_bwd)


def _chunk_fn(ss, rs, lws, ks, vs, kks, als):
    c = rs[0].shape[0]
    n = 2 * c
    row = lax.broadcasted_iota(jnp.int32, (n, n), 0)
    col = lax.broadcasted_iota(jnp.int32, (n, n), 1)
    incl = (row % c) >= (col % c)
    strict = (row % c) > (col % c)
    lane = lax.broadcasted_iota(jnp.int32, (1, LANES), 1)
    m_lo = jnp.where(lane < HEAD_DIM, 1.0, 0.0)
    m_hi = 1.0 - m_lo

    def stack(a):
        return jnp.concatenate([a * m_lo, a * m_hi], axis=0)

    cums = [_cumsum_rows(lw) for lw in lws]
    totals = [jnp.sum(lw, axis=0, keepdims=True) for lw in lws]
    bs = [kk * al for kk, al in zip(kks, als)]
    grows = [jnp.exp(-cum) for cum in cums]
    a_s = [stack(-kk * jnp.exp(cum - lw)) for kk, cum, lw in zip(kks, cums, lws)]
    b_s = [stack(b * g) for b, g in zip(bs, grows)]
    k_s = [stack(k * g) for k, g in zip(ks, grows)]
    r_s = [stack(r * jnp.exp(cum)) for r, cum in zip(rs, cums)]
    v_s = [stack(v) for v in vs]
    pair = lambda p, q: jnp.concatenate([p, q], axis=0)
    ar_s = [pair(a, r) for a, r in zip(a_s, r_s)]
    blocks = [_quarters(_bdot(ar, pair(b, k), NT)) for ar, b, k in zip(ar_s, b_s, k_s)]
    m_ab = [jnp.where(strict, q[0], 0.0) for q in blocks]
    m_ak = [jnp.where(strict, q[1], 0.0) for q in blocks]
    m_rb = [jnp.where(incl, q[2], 0.0) for q in blocks]
    m_rk = [jnp.where(incl, q[3], 0.0) for q in blocks]
    from_state = [_halves(_bdot(ar, s, NT)) for ar, s in zip(ar_s, ss)]
    from_v = [_halves(_bdot(pair(mk, mr), v)) for mk, mr, v in zip(m_ak, m_rk, v_s)]
    u = _tri_solve(tuple(m_ab), tuple(fs[0] + fv[0] for fs, fv in zip(from_state, from_v)))
    y = [_fold_rows(fs[1] + _bdot(mb, uu) + fv[1]) for fs, mb, uu, fv in zip(from_state, m_rb, u, from_v)]
    tails = [jnp.exp(tot - cum) for tot, cum in zip(totals, cums)]
    s_new = [s * jnp.exp(tot) + _bdot(pair(uu, v), pair(stack(b * tl), stack(k * tl)), TN)
             for s, tot, uu, b, tl, v, k in zip(ss, totals, u, bs, tails, v_s, ks)]
    return tuple(y), tuple(s_new)


def _chains(bsz, npair):
    return [(b, p, slice(p * LANES, (p + 1) * LANES)) for b in range(bsz) for p in range(npair)]


def _hosted_exchange(first, last, ex_in, ex_out, sems, scatter):
    if not ex_in:
        return

    @pl.when(first)
    def _():
        _exchange_start(_exchange_copies(ex_in, ex_out, *sems, scatter, arrivals=False))

    @pl.when(last)
    def _():
        _exchange_wait(_exchange_copies(ex_in, ex_out, *sems, scatter, arrivals=True))


def _rwkv_fwd(r, lw, k, v, kk, al, hosted=(), scatter=False):
    bsz, t, w = r.shape
    npair, nchunk = w // LANES, t // CHUNK
    chains = _chains(bsz, npair)
    nex = len(hosted)

    def body(*refs):
        r_ref, lw_ref, k_ref, v_ref, kk_ref, al_ref = refs[:6]
        ex_in = refs[6:6 + nex]
        y_ref, sall_ref = refs[6 + nex:8 + nex]
        ex_out = refs[8 + nex:8 + 2 * nex]
        s_scr = refs[8 + 2 * nex]
        step = pl.program_id(0)

        @pl.when(step == 0)
        def _():
            s_scr[...] = jnp.zeros_like(s_scr)

        _hosted_exchange(step == 0, step == nchunk - 1, ex_in, ex_out, refs[9 + 2 * nex:], scatter)
        ss = tuple(s_scr[i] for i in range(len(chains)))
        for i, s in enumerate(ss):
            sall_ref[0, i] = s
        ys, s_new = _chunk_fn(ss, *[tuple(ref[b, :, cols] for b, _, cols in chains)
                                    for ref in (r_ref, lw_ref, k_ref, v_ref, kk_ref, al_ref)])
        for i, (b, _, cols) in enumerate(chains):
            y_ref[b, :, cols] = ys[i]
            s_scr[i] = s_new[i]

    spec = pl.BlockSpec((bsz, CHUNK, w), lambda c: (0, c, 0))
    hbm = pl.BlockSpec(memory_space=pltpu.HBM)
    outs = pl.pallas_call(
        body,
        grid=(nchunk,),
        in_specs=[spec] * 6 + [hbm] * nex,
        out_specs=[spec, pl.BlockSpec((1, len(chains), LANES, LANES), lambda c: (c, 0, 0, 0))] + [hbm] * nex,
        out_shape=[jax.ShapeDtypeStruct((bsz, t, w), F32),
                   jax.ShapeDtypeStruct((nchunk, len(chains), LANES, LANES), F32)]
        + _exchange_out_shapes(hosted, scatter),
        scratch_shapes=[pltpu.VMEM((len(chains), LANES, LANES), F32)] + (_exchange_scratch(nex) if nex else []),
        compiler_params=_cparams(("arbitrary",), 48),
        name="rwkv_chunk_fwd",
    )(r, lw, k, v, kk, al, *hosted)
    return outs[0], outs[1], outs[2:]


def _rwkv_bwd(r, lw, k, v, kk, al, s_all, dy, hosted=(), scatter=False):
    bsz, t, w = r.shape
    npair, nchunk = w // LANES, t // CHUNK
    chains = _chains(bsz, npair)
    nex = len(hosted)

    def body(*refs):
        r_ref, lw_ref, k_ref, v_ref, kk_ref, al_ref, s_ref, dy_ref = refs[:8]
        ex_in = refs[8:8 + nex]
        out_refs = refs[8 + nex:14 + nex]
        ex_out = refs[14 + nex:14 + 2 * nex]
        ds_scr = refs[14 + 2 * nex]
        step = pl.program_id(0)

        @pl.when(step == 0)
        def _():
            ds_scr[...] = jnp.zeros_like(ds_scr)

        _hosted_exchange(step == 0, step == nchunk - 1, ex_in, ex_out, refs[15 + 2 * nex:], scatter)
        ss = tuple(s_ref[0, i] for i in range(len(chains)))
        _, vjp = jax.vjp(_chunk_fn, ss, *[tuple(ref[b, :, cols] for b, _, cols in chains)
                                          for ref in (r_ref, lw_ref, k_ref, v_ref, kk_ref, al_ref)])
        grads = vjp((tuple(dy_ref[b, :, cols] for b, _, cols in chains),
                     tuple(ds_scr[i] for i in range(len(chains)))))
        for i, (b, _, cols) in enumerate(chains):
            ds_scr[i] = grads[0][i]
            for ref, gval in zip(out_refs, grads[1:]):
                ref[b, :, cols] = gval[i]

    spec = pl.BlockSpec((bsz, CHUNK, w), lambda c: (0, nchunk - 1 - c, 0))
    sspec = pl.BlockSpec((1, len(chains), LANES, LANES), lambda c: (nchunk - 1 - c, 0, 0, 0))
    hbm = pl.BlockSpec(memory_space=pltpu.HBM)
    outs = pl.pallas_call(
        body,
        grid=(nchunk,),
        in_specs=[spec] * 6 + [sspec, spec] + [hbm] * nex,
        out_specs=[spec] * 6 + [hbm] * nex,
        out_shape=[jax.ShapeDtypeStruct((bsz, t, w), F32)] * 6 + _exchange_out_shapes(hosted, scatter),
        scratch_shapes=[pltpu.VMEM((len(chains), LANES, LANES), F32)] + (_exchange_scratch(nex) if nex else []),
        compiler_params=_cparams(("arbitrary",), 48),
        name="rwkv_chunk_bwd",
    )(r, lw, k, v, kk, al, s_all, dy, *hosted)
    return outs[:6], outs[6:]


def _alibi_slope(head):
    return 2.0 ** (-8.0 * (head + 1) / N_ATTN_HEADS)


def _attn_block(qs, kp, kc, vp, vc, sinks, first):
    row = lax.broadcasted_iota(jnp.int32, (BLOCK, BLOCK), 0)
    col = lax.broadcasted_iota(jnp.int32, (BLOCK, BLOCK), 1)
    lane = lax.broadcasted_iota(jnp.int32, (1, LANES), 1)
    halves = [jnp.where((lane // HEAD_DIM) == half, 1.0, 0.0) for half in range(2)]
    swap = jnp.where((row + HEAD_DIM) % LANES == col, 1.0, 0.0)
    dist_c = (row - col).astype(F32)
    dist_p = dist_c + float(BLOCK)
    valid_c = row >= col
    valid_p = jnp.logical_and(col > row, jnp.logical_not(first))
    scale = HEAD_DIM ** -0.5
    stored = (kp, kc, vp, vc)
    swapped = tuple(_bdot_nn(t, swap) for t in stored)
    heads = [(pair, half) for pair in range(len(qs)) for half in range(2)]
    kv = [stored if half == pair // 2 else swapped for pair, half in heads]
    slopes = [_alibi_slope(2 * pair + half) for pair, half in heads]
    qa = [qs[pair] * halves[half] for pair, half in heads]
    sp = [jnp.where(valid_p, _bdot_nt(q, t[0]) * scale - sl * dist_p, NEG_INF) for q, t, sl in zip(qa, kv, slopes)]
    sc = [jnp.where(valid_c, _bdot_nt(q, t[1]) * scale - sl * dist_c, NEG_INF) for q, t, sl in zip(qa, kv, slopes)]
    mx = [lax.stop_gradient(jnp.maximum(jnp.maximum(jnp.max(a, axis=-1, keepdims=True),
                                                    jnp.max(b, axis=-1, keepdims=True)), sk))
          for a, b, sk in zip(sp, sc, sinks)]
    ep = [jnp.exp(a - m) for a, m in zip(sp, mx)]
    ec = [jnp.exp(b - m) for b, m in zip(sc, mx)]
    es = [jnp.exp(sk - m) for sk, m in zip(sinks, mx)]
    inv = [1.0 / (jnp.sum(a, axis=-1, keepdims=True) + jnp.sum(b, axis=-1, keepdims=True) + s)
           for a, b, s in zip(ep, ec, es)]
    o = [_bdot_nn(a * i, t[2]) + _bdot_nn(b * i, t[3]) for a, b, i, t in zip(ep, ec, inv, kv)]
    outs = tuple(o[2 * pair] * halves[0] + o[2 * pair + 1] * halves[1] for pair in range(len(qs)))
    return outs, [lax.stop_gradient(s * i) for s, i in zip(es, inv)]


def _sink_values(sink_ref):
    return [jnp.max(sink_ref[h:h + 1, :], axis=-1, keepdims=True) for h in range(N_ATTN_HEADS)]


def _attn_fwd(z, sink_rows):
    bsz, t, _ = z.shape
    nb = t // BLOCK
    npair = ATTN_WIDTH // LANES

    def body(q_ref, kp_ref, kc_ref, vp_ref, vc_ref, sink_ref, o_ref):
        first = pl.program_id(1) == 0
        qs = tuple(q_ref[0, :, pair * LANES:(pair + 1) * LANES] for pair in range(npair))
        outs, _ = _attn_block(qs, kp_ref[0], kc_ref[0], vp_ref[0], vc_ref[0], _sink_values(sink_ref), first)
        for pair in range(npair):
            o_ref[0, :, pair * LANES:(pair + 1) * LANES] = outs[pair].astype(o_ref.dtype)

    kcol, vcol = ATTN_WIDTH // KV_WIDTH, ATTN_WIDTH // KV_WIDTH + 1
    return pl.pallas_call(
        body,
        grid=(bsz, nb),
        in_specs=[pl.BlockSpec((1, BLOCK, ATTN_WIDTH), lambda b, n: (b, n, 0)),
                  pl.BlockSpec((1, BLOCK, KV_WIDTH), lambda b, n: (b, jnp.maximum(n - 1, 0), kcol)),
                  pl.BlockSpec((1, BLOCK, KV_WIDTH), lambda b, n: (b, n, kcol)),
                  pl.BlockSpec((1, BLOCK, KV_WIDTH), lambda b, n: (b, jnp.maximum(n - 1, 0), vcol)),
                  pl.BlockSpec((1, BLOCK, KV_WIDTH), lambda b, n: (b, n, vcol)),
                  pl.BlockSpec(sink_rows.shape, lambda b, n: (0, 0))],
        out_specs=pl.BlockSpec((1, BLOCK, ATTN_WIDTH), lambda b, n: (b, n, 0)),
        out_shape=jax.ShapeDtypeStruct((bsz, t, ATTN_WIDTH), BF16),
        compiler_params=_cparams(("parallel", "arbitrary"), 48),
        name="swa_fwd",
    )(z, z, z, z, z, sink_rows)


def _attn_bwd(z, dout, sink_rows):
    bsz, t, _ = z.shape
    nb = t // BLOCK
    npair = ATTN_WIDTH // LANES

    def body(q_ref, kp_ref, kc_ref, vp_ref, vc_ref, do_ref, sink_ref, dz_ref, dsink_ref, carry):
        step = pl.program_id(1)
        n = nb - 1 - step
        first = n == 0

        @pl.when(step == 0)
        def _():
            carry[...] = jnp.zeros_like(carry)

        @pl.when(jnp.logical_and(step == 0, pl.program_id(0) == 0))
        def _():
            dsink_ref[...] = jnp.zeros_like(dsink_ref)

        lane = lax.broadcasted_iota(jnp.int32, (1, LANES), 1)
        qs = tuple(q_ref[0, :, pair * LANES:(pair + 1) * LANES] for pair in range(npair))
        dos = tuple(do_ref[0, :, pair * LANES:(pair + 1) * LANES] for pair in range(npair))
        fn = functools.partial(_attn_block, sinks=_sink_values(sink_ref), first=first)
        outs, vjp, psinks = jax.vjp(fn, qs, kp_ref[0], kc_ref[0], vp_ref[0], vc_ref[0], has_aux=True)
        dqs, dkp, dkc, dvp, dvc = vjp(dos)
        for pair in range(npair):
            dz_ref[0, :, pair * LANES:(pair + 1) * LANES] = dqs[pair].astype(dz_ref.dtype)
            for half in range(2):
                m = jnp.where((lane // HEAD_DIM) == half, 1.0, 0.0)
                delta = jnp.sum(dos[pair] * outs[pair] * m, axis=-1, keepdims=True)
                head = 2 * pair + half
                ds = -jnp.sum(psinks[head] * delta, axis=0, keepdims=True)
                dsink_ref[head:head + 1, :] += jnp.broadcast_to(ds, (1, LANES))
        dz_ref[0, :, ATTN_WIDTH:ATTN_WIDTH + KV_WIDTH] = (dkc + carry[0]).astype(dz_ref.dtype)
        dz_ref[0, :, ATTN_WIDTH + KV_WIDTH:QKV_WIDTH] = (dvc + carry[1]).astype(dz_ref.dtype)
        carry[0] = dkp
        carry[1] = dvp

    kcol, vcol = ATTN_WIDTH // KV_WIDTH, ATTN_WIDTH // KV_WIDTH + 1
    rev = lambda n: nb - 1 - n
    return pl.pallas_call(
        body,
        grid=(bsz, nb),
        in_specs=[pl.BlockSpec((1, BLOCK, ATTN_WIDTH), lambda b, n: (b, rev(n), 0)),
                  pl.BlockSpec((1, BLOCK, KV_WIDTH), lambda b, n: (b, jnp.maximum(rev(n) - 1, 0), kcol)),
                  pl.BlockSpec((1, BLOCK, KV_WIDTH), lambda b, n: (b, rev(n), kcol)),
                  pl.BlockSpec((1, BLOCK, KV_WIDTH), lambda b, n: (b, jnp.maximum(rev(n) - 1, 0), vcol)),
                  pl.BlockSpec((1, BLOCK, KV_WIDTH), lambda b, n: (b, rev(n), vcol)),
                  pl.BlockSpec((1, BLOCK, ATTN_WIDTH), lambda b, n: (b, rev(n), 0)),
                  pl.BlockSpec(sink_rows.shape, lambda b, n: (0, 0))],
        out_specs=[pl.BlockSpec((1, BLOCK, QKV_WIDTH), lambda b, n: (b, rev(n), 0)),
                   pl.BlockSpec((N_ATTN_HEADS, LANES), lambda b, n: (0, 0))],
        out_shape=[jax.ShapeDtypeStruct((bsz, t, QKV_WIDTH), BF16),
                   jax.ShapeDtypeStruct((N_ATTN_HEADS, LANES), F32)],
        scratch_shapes=[pltpu.VMEM((2, BLOCK, KV_WIDTH), F32)],
        compiler_params=_cparams(("arbitrary", "arbitrary"), 48),
        name="swa_bwd",
    )(z, z, z, z, z, dout, sink_rows)


def _exchange_out_shapes(arrays, scatter):
    return [jax.ShapeDtypeStruct((N_DEV,) + (a.shape[1:] if scatter else a.shape), a.dtype) for a in arrays]


def _exchange_scratch(n):
    return [pltpu.SemaphoreType.DMA((n, N_DEV - 1)), pltpu.SemaphoreType.DMA((n, N_DEV - 1)),
            pltpu.SemaphoreType.DMA((n,))]


def _exchange_copies(ins, outs, send_sems, recv_sems, local_sems, scatter, arrivals=True):
    x, y, c = lax.axis_index("x"), lax.axis_index("y"), lax.axis_index("c")
    me = 4 * x + 2 * y + c
    copies = []
    for i in range(len(ins)):
        own = pltpu.make_async_copy(ins[i].at[me] if scatter else ins[i], outs[i].at[me], local_sems.at[i])
        copies.append((own, None, True))
        for d in range(1, N_DEV):
            px = 1 - x if d & 4 else x
            py = 1 - y if d & 2 else y
            pc = 1 - c if d & 1 else c
            peer = 4 * px + 2 * py + pc
            src = ins[i].at[peer] if scatter else ins[i]
            send = pltpu.make_async_remote_copy(src, outs[i].at[me], send_sems.at[i, d - 1], recv_sems.at[i, d - 1],
                                                device_id=(px, py, pc), device_id_type=MESH)
            recv = pltpu.make_async_remote_copy(src, outs[i].at[peer], send_sems.at[i, d - 1], recv_sems.at[i, d - 1],
                                                device_id=(px, py, pc), device_id_type=MESH) if arrivals else None
            copies.append((send, recv, False))
    return copies


def _exchange_start(copies):
    for send, _, _ in copies:
        send.start()


def _exchange_wait(copies):
    for send, recv, local in copies:
        if local:
            send.wait()
        else:
            send.wait_send()
            recv.wait_recv()


def _exchange(arrays, *, scatter, name):
    n = len(arrays)

    def body(*refs):
        copies = _exchange_copies(refs[:n], refs[n:2 * n], *refs[2 * n:], scatter)
        _exchange_start(copies)
        _exchange_wait(copies)

    hbm = pl.BlockSpec(memory_space=pltpu.HBM)
    return pl.pallas_call(
        body,
        in_specs=[hbm] * n,
        out_specs=[hbm] * n,
        out_shape=_exchange_out_shapes(arrays, scatter),
        scratch_shapes=_exchange_scratch(n),
        name=name,
    )(*arrays)


def _adamw(parts, w, m, v, name):
    rows, cols = w.shape
    tr = _pick(rows, (256, 128, 64, 8))
    c1 = 1.0 / (1.0 - ADAM_B1 ** ADAM_STEP)
    c2 = 1.0 / (1.0 - ADAM_B2 ** ADAM_STEP)

    def body(p_ref, w_ref, m_ref, v_ref, g_ref, d_ref, mo_ref, vo_ref):
        g = p_ref[0].astype(F32)
        for s in range(1, N_DEV):
            g = g + p_ref[s].astype(F32)
        mn = ADAM_B1 * m_ref[...] + (1.0 - ADAM_B1) * g
        vn = ADAM_B2 * v_ref[...] + (1.0 - ADAM_B2) * (g * g)
        g_ref[...] = g
        mo_ref[...] = mn
        vo_ref[...] = vn
        d_ref[...] = -ADAM_LR * ((mn * c1) / (jnp.sqrt(vn * c2) + ADAM_EPS) + ADAM_WD * w_ref[...])

    spec = pl.BlockSpec((tr, cols), lambda i: (i, 0))
    return pl.pallas_call(
        body,
        grid=(rows // tr,),
        in_specs=[pl.BlockSpec((N_DEV, tr, cols), lambda i: (0, i, 0)), spec, spec, spec],
        out_specs=[spec] * 4,
        out_shape=[jax.ShapeDtypeStruct((rows, cols), F32)] * 4,
        compiler_params=_cparams(("parallel",), 48),
        name=name,
    )(parts, w, m, v)


_VECTOR_PARAMS = ("attn_norm_g", "attn_sinks", "rwkv_mu", "w0", "a0", "k_k", "k_a", "r_k", "ln_x_w", "ln_x_b",
                  "mlp_norm_g", "final_norm_g")
_WEIGHT_NAMES = ("attn_norm_g", "w_in", "attn_sinks", "rwkv_mu", "w0", "w2", "a0", "a2", "g2", "k_k", "k_a", "r_k",
                 "ln_x_w", "ln_x_b", "w_out", "mlp_norm_g", "w_up", "w_down", "final_norm_g")


def _pack_vectors(vals):
    pieces = []
    for name in _VECTOR_PARAMS:
        flat = vals[name].reshape(1, -1)
        pad = (-flat.shape[1]) % LANES
        pieces.append(jnp.pad(flat, ((0, 0), (0, pad))) if pad else flat)
    return jnp.concatenate(pieces, axis=1)


def _unpack_vectors(packed, like):
    out, col = {}, 0
    for name in _VECTOR_PARAMS:
        size = like[name].size
        out[name] = packed[0, col:col + size].reshape(like[name].shape)
        col += size + (-size) % LANES
    return out


def kernel(x, attn_norm_g, w_in, attn_sinks, rwkv_mu, w0, w2, a0, a2, g2, k_k, k_a, r_k, ln_x_w, ln_x_b, w_out, mlp_norm_g, w_up, w_down, final_norm_g, loss_target, m_attn_norm_g, m_w_in, m_attn_sinks, m_rwkv_mu, m_w0, m_w2, m_a0, m_a2, m_g2, m_k_k, m_k_a, m_r_k, m_ln_x_w, m_ln_x_b, m_w_out, m_mlp_norm_g, m_w_up, m_w_down, m_final_norm_g, v_attn_norm_g, v_w_in, v_attn_sinks, v_rwkv_mu, v_w0, v_w2, v_a0, v_a2, v_g2, v_k_k, v_k_a, v_r_k, v_ln_x_w, v_ln_x_b, v_w_out, v_mlp_norm_g, v_w_up, v_w_down, v_final_norm_g):
    weights = dict(attn_norm_g=attn_norm_g, w_in=w_in, attn_sinks=attn_sinks, rwkv_mu=rwkv_mu, w0=w0, w2=w2, a0=a0,
                   a2=a2, g2=g2, k_k=k_k, k_a=k_a, r_k=r_k, ln_x_w=ln_x_w, ln_x_b=ln_x_b, w_out=w_out,
                   mlp_norm_g=mlp_norm_g, w_up=w_up, w_down=w_down, final_norm_g=final_norm_g)
    mom1 = dict(attn_norm_g=m_attn_norm_g, w_in=m_w_in, attn_sinks=m_attn_sinks, rwkv_mu=m_rwkv_mu, w0=m_w0, w2=m_w2,
                a0=m_a0, a2=m_a2, g2=m_g2, k_k=m_k_k, k_a=m_k_a, r_k=m_r_k, ln_x_w=m_ln_x_w, ln_x_b=m_ln_x_b,
                w_out=m_w_out, mlp_norm_g=m_mlp_norm_g, w_up=m_w_up, w_down=m_w_down, final_norm_g=m_final_norm_g)
    mom2 = dict(attn_norm_g=v_attn_norm_g, w_in=v_w_in, attn_sinks=v_attn_sinks, rwkv_mu=v_rwkv_mu, w0=v_w0, w2=v_w2,
                a0=v_a0, a2=v_a2, g2=v_g2, k_k=v_k_k, k_a=v_k_a, r_k=v_r_k, ln_x_w=v_ln_x_w, ln_x_b=v_ln_x_b,
                w_out=v_w_out, mlp_norm_g=v_mlp_norm_g, w_up=v_w_up, w_down=v_w_down, final_norm_g=v_final_norm_g)
    bsz, seq, d_model = x.shape
    rows = bsz * seq
    d_in = N_DEV * w_in.shape[2]
    d_ff = N_DEV * w_up.shape[2]

    gathered = _exchange([w_in[0].astype(BF16), w2[0], a2[0], g2[0]], scatter=False, name="gather_in_weights")
    cols_first = lambda a: a.transpose(1, 0, 2).reshape(a.shape[1], -1)
    w_in_f = cols_first(gathered[0])
    w_attn, w_rw = w_in_f[:, :QKV_WIDTH], w_in_f[:, QKV_WIDTH:]
    w2_f, a2_f, g2_f = cols_first(gathered[1]), cols_first(gathered[2]), cols_first(gathered[3])
    lora = w2_f.shape[0]
    w2p = jnp.concatenate([w2_f, jnp.zeros_like(a2_f)], axis=0)
    a2p = jnp.concatenate([jnp.zeros_like(w2_f), a2_f], axis=0)

    esum = _head_sum_matrix()
    sink_rows = jnp.broadcast_to(attn_sinks.reshape(N_ATTN_HEADS, 1), (N_ATTN_HEADS, LANES))
    prep_pars = [w0, w2p, a0, a2p, g2_f, k_k, k_a, esum]
    post_pars = [ln_x_w, ln_x_b, r_k, esum]

    x2d = x.reshape(rows, d_model)
    h1, z_attn, z_rw = _norm_in_proj(x2d, attn_norm_g, w_attn, w_rw)
    z_attn3 = z_attn.reshape(bsz, seq, QKV_WIDTH)
    attn_out = _attn_fwd(z_attn3, sink_rows)
    r, lw, k, v, kk, al, gate = _prep_fwd(z_rw, seq, rwkv_mu, prep_pars)
    as3 = lambda a: a.reshape(bsz, seq, RWKV_WIDTH)
    y, s_all, late = _rwkv_fwd(as3(r), as3(lw), as3(k), as3(v), as3(kk), as3(al),
                               hosted=[w_out[0].astype(BF16), w_up[0].astype(BF16), w_down[0].astype(BF16)])
    w_out_f = late[0].reshape(-1, d_model)
    w_up_f = cols_first(late[1])
    w_down_f = late[2].reshape(-1, d_model)
    y2 = y.reshape(rows, RWKV_WIDTH)
    rw_out = _post_fwd(y2, r, k, v, gate, post_pars)
    mix = jnp.concatenate([attn_out.reshape(rows, ATTN_WIDTH), rw_out], axis=1)
    x1, h2 = _out_proj_norm(mix, w_out_f, x2d, mlp_norm_g)

    def relu_sq(acc):
        pos = jnp.maximum(acc, 0.0)
        return acc, pos * pos

    u, act = _matmul(h2, w_up_f, "nn", name="mlp_up", epilogue=relu_sq, out_dtypes=(BF16, BF16))
    dx2, loss_vec, g_final = _down_proj_loss(act, w_down_f, x1, loss_target.reshape(rows, d_model),
                                             final_norm_g.reshape(1, d_model))

    g_w_down = _matmul(act, dx2, "tn", name="grad_w_down", out_dtypes=(BF16,))
    du = _matmul(dx2, w_down_f, "nt", name="mlp_down_bwd", extras=(u,), out_dtypes=(BF16,),
                 epilogue=lambda acc, uv: (acc * (2.0 * jnp.maximum(uv.astype(F32), 0.0)),))
    g_w_up = _matmul(h2, du, "tn", name="grad_w_up", out_dtypes=(BF16,))
    dx1, g_mlp_norm = _proj_bwd_norm_bwd([du], [w_up_f], x1, dx2, mlp_norm_g, "mlp_up_bwd_norm_bwd")
    g_w_out = _matmul(mix, dx1, "tn", name="grad_w_out", out_dtypes=(BF16,))
    dmix = _matmul(dx1, w_out_f, "nt", name="out_proj_bwd")
    dy, dr_a, dk_a, dv_a, dgate, g_ln_w, g_ln_b, g_r_k = _post_bwd(
        y2, r, k, v, gate, (dmix, RWKV_WIDTH, ATTN_WIDTH // RWKV_WIDTH), post_pars)
    by_cols = lambda a: a.reshape(a.shape[0], N_DEV, -1).transpose(1, 0, 2)
    (dr_b, dlw, dk_b, dv_b, dkk, dal), (p_w_out, p_w_up, p_w_down) = _rwkv_bwd(
        as3(r), as3(lw), as3(k), as3(v), as3(kk), as3(al), s_all, as3(dy),
        hosted=[g_w_out.reshape(N_DEV, -1, d_model), by_cols(g_w_up), g_w_down.reshape(N_DEV, -1, d_model)],
        scatter=True)
    flat = lambda a: a.reshape(rows, RWKV_WIDTH)
    (dz_rw, gmu_r, gmu_k, gmu_v, gmu_wa, gmu_g, g_w0, g_w2p, g_a0, g_a2p, g_g2, g_k_k, g_k_a) = _prep_bwd(
        z_rw, seq, [dr_a, flat(dr_b), flat(dlw), dk_a, flat(dk_b), dv_a, flat(dv_b), flat(dkk), flat(dal), dgate],
        rwkv_mu, prep_pars)
    dz_attn, g_sink_rows = _attn_bwd(z_attn3, dmix.reshape(bsz, seq, d_model), sink_rows)
    dz_attn = dz_attn.reshape(rows, QKV_WIDTH)
    g_w_in = jnp.concatenate([_matmul(h1, dz_attn, "tn", name="grad_w_in_attn", out_dtypes=(BF16,)),
                              _matmul(h1, dz_rw, "tn", name="grad_w_in_rwkv", out_dtypes=(BF16,))], axis=1)
    lora_grads = jnp.concatenate([g_w2p[:lora], g_a2p[lora:], g_g2], axis=0)
    (dx, g_attn_norm), (p_w_in, p_lora) = _proj_bwd_norm_bwd(
        [dz_attn, dz_rw], [w_attn, w_rw], x2d, dx1, attn_norm_g, "in_proj_bwd_norm_bwd",
        hosted=[by_cols(g_w_in), by_cols(lora_grads)], scatter=True)

    vec_grads = dict(attn_norm_g=g_attn_norm, attn_sinks=g_sink_rows[:, 0], rwkv_mu=jnp.concatenate(
        [gmu_r, gmu_k, gmu_v, gmu_wa, gmu_g], axis=1), w0=g_w0, a0=g_a0, k_k=g_k_k, k_a=g_k_a, r_k=g_r_k,
        ln_x_w=g_ln_w, ln_x_b=g_ln_b, mlp_norm_g=g_mlp_norm, final_norm_g=g_final)
    packed = _pack_vectors(vec_grads)
    nvec = packed.shape[1]
    everyone = _exchange([jnp.concatenate([packed, loss_vec], axis=1)], scatter=False, name="gather_vector_grads")[0]
    vec_parts = everyone[:, :, :nvec]
    loss = jnp.sum(everyone[:, 0, nvec])

    grads, delta, new_m, new_v = {}, {}, {}, {}

    def update(name, part, shape2d):
        res = _adamw(part, weights[name].reshape(shape2d), mom1[name].reshape(shape2d), mom2[name].reshape(shape2d),
                     "adamw_" + name)
        for store, val in zip((grads, delta, new_m, new_v), res):
            store[name] = val.reshape(weights[name].shape)

    update("w_in", p_w_in, w_in.shape[1:])
    update("w_out", p_w_out, w_out.shape[1:])
    update("w_up", p_w_up, w_up.shape[1:])
    update("w_down", p_w_down, w_down.shape[1:])
    stack = lambda d: jnp.concatenate([d["w2"][0], d["a2"][0], d["g2"][0]], axis=0)
    lora_res = _adamw(p_lora, stack(weights), stack(mom1), stack(mom2), "adamw_lora")
    for store, val in zip((grads, delta, new_m, new_v), lora_res):
        store["w2"], store["a2"], store["g2"] = val[None, :lora], val[None, lora:2 * lora], val[None, 2 * lora:]
    vec_res = _adamw(vec_parts, _pack_vectors(weights), _pack_vectors(mom1), _pack_vectors(mom2), "adamw_vectors")
    for store, val in zip((grads, delta, new_m, new_v), vec_res):
        store.update(_unpack_vectors(val, weights))

    return (loss, dx.reshape(x.shape), *[grads[n] for n in _WEIGHT_NAMES], *[delta[n] for n in _WEIGHT_NAMES],
            *[new_m[n] for n in _WEIGHT_NAMES], *[new_v[n] for n in _WEIGHT_NAMES])
```

```python
import functools
import math

import jax
import jax.numpy as jnp
from jax import lax
from jax.experimental import pallas as pl
from jax.experimental.pallas import tpu as pltpu

F32 = jnp.float32
BF16 = jnp.bfloat16

N_DEV = 8
HEAD_DIM = 64
LANES = 128
N_ATTN_HEADS = 8
ATTN_WIDTH = 512
KV_WIDTH = 128
QKV_WIDTH = ATTN_WIDTH + 2 * KV_WIDTH
RWKV_WIDTH = 512
LORA_WA = 128
GATE_LORA = 128
RWKV_SHIFT_WIDTH = 3 * RWKV_WIDTH + LORA_WA + GATE_LORA
BLOCK = 128
CHUNK = 64
RMS_EPS = 1e-6
GN_EPS = 64e-5
L2_EPS = 1e-12
NEG_INF = -1e30
DECAY_SCALE = math.exp(-0.5)
ADAM_LR, ADAM_B1, ADAM_B2, ADAM_EPS, ADAM_WD, ADAM_STEP = 0.001, 0.9, 0.999, 1e-08, 0.01, 10

NN = (((1,), (0,)), ((), ()))
NT = (((1,), (1,)), ((), ()))
TN = (((0,), (0,)), ((), ()))
MESH = pl.DeviceIdType.MESH


def _dot(a, b, dn=NN, precision=None):
    return lax.dot_general(a, b, dn, precision=precision, preferred_element_type=F32)


def _bdot_raw(a, b, dn):
    return lax.dot_general(a.astype(BF16), b.astype(BF16), dn, preferred_element_type=F32)


@functools.partial(jax.custom_vjp, nondiff_argnums=(2, 3))
def _bdot_c(a, b, ca, cb):
    return _bdot_raw(a, b, (((ca,), (cb,)), ((), ())))


def _bdot_c_fwd(a, b, ca, cb):
    return _bdot_c(a, b, ca, cb), (a, b)


def _bdot_c_bwd(ca, cb, res, ct):
    a, b = res
    fa, fb = 1 - ca, 1 - cb
    da = _bdot_raw(ct, b, (((1,), (fb,)), ((), ()))) if ca == 1 else _bdot_raw(b, ct, (((fb,), (1,)), ((), ())))
    db = _bdot_raw(a, ct, (((fa,), (0,)), ((), ()))) if cb == 0 else _bdot_raw(ct, a, (((0,), (fa,)), ((), ())))
    return da, db


_bdot_c.defvjp(_bdot_c_fwd, _bdot_c_bwd)


def _bdot(a, b, dn=NN):
    return _bdot_c(a, b, dn[0][0][0], dn[0][1][0])


def _bdot_nn(a, b):
    return _bdot(a, b, NN)


def _bdot_nt(a, b):
    return _bdot(a, b, NT)


def _split3(x):
    hi = x.astype(BF16)
    rest = x - hi.astype(F32)
    mid = rest.astype(BF16)
    return hi, mid, (rest - mid.astype(F32)).astype(BF16)


def _running_sum(x, dn):
    c = x.shape[0]
    row = lax.broadcasted_iota(jnp.int32, (c, c), 0)
    col = lax.broadcasted_iota(jnp.int32, (c, c), 1)
    tri = jnp.where(row >= col, 1.0, 0.0).astype(BF16)
    w = x.shape[1]
    parts = lax.dot_general(tri, jnp.concatenate(_split3(x), axis=1), dn, preferred_element_type=F32)
    return parts[:, :w] + parts[:, w:2 * w] + parts[:, 2 * w:]


@jax.custom_vjp
def _cumsum_rows(x):
    return _running_sum(x, NN)


_cumsum_rows.defvjp(lambda x: (_running_sum(x, NN), None), lambda _, ct: (_running_sum(ct, TN),))


@jax.custom_vjp
def _fold_rows(x):
    c = x.shape[0] // 2
    return x[:c] + x[c:]


_fold_rows.defvjp(lambda x: (_fold_rows(x), None), lambda _, ct: (jnp.concatenate([ct, ct], axis=0),))


@jax.custom_vjp
def _halves(x):
    n = x.shape[0] // 2
    return x[:n], x[n:]


_halves.defvjp(lambda x: (_halves(x), None), lambda _, cts: (jnp.concatenate(cts, axis=0),))


@jax.custom_vjp
def _quarters(x):
    n = x.shape[0] // 2
    return x[:n, :n], x[:n, n:], x[n:, :n], x[n:, n:]


_quarters.defvjp(lambda x: (_quarters(x), None),
                 lambda _, cts: (jnp.concatenate([jnp.concatenate(cts[:2], axis=1),
                                                  jnp.concatenate(cts[2:], axis=1)], axis=0),))


@jax.custom_vjp
def _sigmoid(x):
    return 1.0 / (1.0 + jnp.exp(-x))


def _sigmoid_fwd(x):
    s = _sigmoid(x)
    return s, s


_sigmoid.defvjp(_sigmoid_fwd, lambda s, ct: (ct * s * (1.0 - s),))


def _pick(n, cands):
    for c in cands:
        if n % c == 0:
            return c
    return n


def _cparams(sem, vmem_mb=None):
    kw = dict(dimension_semantics=sem)
    if vmem_mb is not None:
        kw["vmem_limit_bytes"] = vmem_mb * 1024 * 1024
    return pltpu.CompilerParams(**kw)


def _matmul(a, b, mode, *, name, extras=(), epilogue=None, out_dtypes=(F32,), tm=1024, tn=1024, tk=1024,
            hosted=(), scatter=False):
    if mode == "nn":
        (M, K), (_, N) = a.shape, b.shape
    elif mode == "tn":
        (K, M), (_, N) = a.shape, b.shape
    else:
        (M, K), (N, _) = a.shape, b.shape
    tm = _pick(M, (tm, 512, 256, 128))
    tn = _pick(N, (tn, 896, 768, 512, 384, 256, 128))
    tk = _pick(K, (tk, 512, 256, 128))
    nk = K // tk
    ne, nout = len(extras), len(out_dtypes)
    if mode == "nn":
        a_spec = pl.BlockSpec((tm, tk), lambda i, j, k: (i, k))
        b_spec = pl.BlockSpec((tk, tn), lambda i, j, k: (k, j))
        dn = NN
    elif mode == "tn":
        a_spec = pl.BlockSpec((tk, tm), lambda i, j, k: (k, i))
        b_spec = pl.BlockSpec((tk, tn), lambda i, j, k: (k, j))
        dn = TN
    else:
        a_spec = pl.BlockSpec((tm, tk), lambda i, j, k: (i, k))
        b_spec = pl.BlockSpec((tn, tk), lambda i, j, k: (j, k))
        dn = NT
    o_spec = pl.BlockSpec((tm, tn), lambda i, j, k: (i, j))
    grid = (M // tm, N // tn, nk)
    nex = len(hosted)

    def body(*refs):
        a_ref, b_ref = refs[:2]
        e_refs = refs[2:2 + ne]
        ex_in = refs[2 + ne:2 + ne + nex]
        o_refs = refs[2 + ne + nex:2 + ne + nex + nout]
        ex_out = refs[2 + ne + nex + nout:2 + ne + 2 * nex + nout]
        scratch = refs[2 + ne + 2 * nex + nout:]
        kstep = pl.program_id(2)
        if nex:
            at = [pl.program_id(d) for d in range(3)]
            first = jnp.logical_and(jnp.logical_and(at[0] == 0, at[1] == 0), at[2] == 0)
            last = jnp.logical_and(jnp.logical_and(at[0] == grid[0] - 1, at[1] == grid[1] - 1), at[2] == grid[2] - 1)
            _hosted_exchange(first, last, ex_in, ex_out, scratch[-3:], scatter)

        def finish(total):
            outs = (total,) if epilogue is None else epilogue(total, *[e[...] for e in e_refs])
            for o_ref, o in zip(o_refs, outs):
                o_ref[...] = o.astype(o_ref.dtype)

        if nk == 1:
            finish(_bdot_raw(a_ref[...], b_ref[...], dn))
            return
        acc = scratch[0]

        @pl.when(kstep == 0)
        def _():
            acc[...] = jnp.zeros_like(acc)

        acc[...] += _bdot_raw(a_ref[...], b_ref[...], dn)

        @pl.when(kstep == nk - 1)
        def _():
            finish(acc[...])

    hbm = pl.BlockSpec(memory_space=pltpu.HBM)
    outs = pl.pallas_call(
        body,
        grid=grid,
        in_specs=[a_spec, b_spec] + [o_spec] * ne + [hbm] * nex,
        out_specs=[o_spec] * nout + [hbm] * nex,
        out_shape=[jax.ShapeDtypeStruct((M, N), dt) for dt in out_dtypes] + _exchange_out_shapes(hosted, scatter),
        scratch_shapes=([pltpu.VMEM((tm, tn), F32)] if nk > 1 else []) + (_exchange_scratch(nex) if nex else []),
        compiler_params=_cparams(("arbitrary",) * 3 if nex else ("parallel", "parallel", "arbitrary"), 56),
        name=name,
    )(a, b, *extras, *hosted)
    if nex:
        return outs[:nout], outs[nout:]
    return outs[0] if nout == 1 else outs


def _rowwise(fn, rows, pars, out_rows, out_accs, *, tile, name, nsub=1, hosted=(), scatter=False):
    rows = [r if isinstance(r, tuple) else (r, r.shape[1], 0) for r in rows]
    R = rows[0][0].shape[0]
    tile = min(tile, R)
    chunk = tile // nsub
    ntile = R // tile
    nr, npar, nor, noa, nex = len(rows), len(pars), len(out_rows), len(out_accs), len(hosted)

    def body(*refs):
        rin = refs[:nr]
        pin = refs[nr:nr + npar]
        ex_in = refs[nr + npar:nr + npar + nex]
        orow = refs[nr + npar + nex:nr + npar + nex + nor]
        oacc = refs[nr + npar + nex + nor:nr + npar + nex + nor + noa]
        ex_out = refs[nr + npar + nex + nor + noa:nr + npar + 2 * nex + nor + noa]
        step = pl.program_id(0)
        _hosted_exchange(step == 0, step == ntile - 1, ex_in, ex_out, refs[nr + npar + 2 * nex + nor + noa:], scatter)
        pvals = [p[...] for p in pin]
        totals = []
        for sub in range(nsub):
            at = slice(sub * chunk, (sub + 1) * chunk)
            outs = fn(*[r[at, :] for r in rin], *pvals)
            for ref, o in zip(orow, outs[:nor]):
                if isinstance(o, (tuple, list)):
                    col = 0
                    for piece in o:
                        ref[at, col:col + piece.shape[1]] = piece.astype(ref.dtype)
                        col += piece.shape[1]
                else:
                    ref[at, :] = o.astype(ref.dtype)
            accs = list(outs[nor:])
            totals = accs if sub == 0 else [t + a for t, a in zip(totals, accs)]

        def accumulate(ref, o):
            @pl.when(step == 0)
            def _():
                ref[...] = o

            @pl.when(step > 0)
            def _():
                ref[...] += o

        for ref, o in zip(oacc, totals):
            accumulate(ref, o)

    def colspec(width, cb):
        return pl.BlockSpec((tile, width), lambda i: (i, cb))

    hbm = pl.BlockSpec(memory_space=pltpu.HBM)
    outs = pl.pallas_call(
        body,
        grid=(ntile,),
        in_specs=[colspec(w, cb) for (_, w, cb) in rows]
        + [pl.BlockSpec(p.shape, lambda i: (0, 0)) for p in pars] + [hbm] * nex,
        out_specs=[colspec(w, 0) for (w, _) in out_rows]
        + [pl.BlockSpec(s, lambda i: (0, 0)) for s in out_accs] + [hbm] * nex,
        out_shape=[jax.ShapeDtypeStruct((R, w), dt) for (w, dt) in out_rows]
        + [jax.ShapeDtypeStruct(s, F32) for s in out_accs] + _exchange_out_shapes(hosted, scatter),
        scratch_shapes=_exchange_scratch(nex) if nex else [],
        compiler_params=_cparams(("arbitrary",), 56),
        name=name,
    )(*[r[0] for r in rows], *pars, *hosted)
    return (outs[:nor + noa], outs[nor + noa:]) if nex else outs


def _rms_fn(x, g):
    return x * lax.rsqrt(jnp.mean(x * x, axis=-1, keepdims=True) + RMS_EPS) * g


FUSED_TILE = 512
FUSED_CHUNKS = 2


def _norm_in_proj(x, g, w_attn, w_rw):
    def fn(xv, gv, wa, wr):
        h = _rms_fn(xv, gv)
        return h, _bdot_raw(h, wa, NN), _bdot_raw(h, wr, NN)

    return _rowwise(fn, [x], [g, w_attn, w_rw], [(x.shape[1], BF16), (w_attn.shape[1], F32), (w_rw.shape[1], F32)], [],
                    tile=FUSED_TILE, nsub=FUSED_CHUNKS, name="attn_norm_in_proj")


def _down_proj_loss(act, w_down, x1, tgt, g):
    d = x1.shape[1]

    def fn(av, xv, tv, wv, gv):
        x2 = xv + _bdot_raw(av, wv, NN)
        y, vjp = jax.vjp(_rms_fn, x2, gv)
        err = y - tv
        loss = 0.5 * jnp.sum(jnp.sum(err * err, axis=-1, keepdims=True), axis=0, keepdims=True) / d
        dx, dg = vjp(err / d)
        return dx, jnp.broadcast_to(loss, (1, LANES)), dg

    return _rowwise(fn, [act, x1, tgt], [w_down, g], [(d, F32)], [(1, LANES), g.shape],
                    tile=FUSED_TILE, nsub=FUSED_CHUNKS, name="mlp_down_final_norm_loss")


def _proj_bwd_norm_bwd(cts, weights_t, x, dres, g, name, hosted=(), scatter=False):
    n = len(cts)

    def fn(*vals):
        ctv, (xv, dresv), wv, gv = vals[:n], vals[n:n + 2], vals[n + 2:2 * n + 2], vals[-1]
        dh = _bdot_raw(ctv[0], wv[0], NT)
        for c, w in zip(ctv[1:], wv[1:]):
            dh = dh + _bdot_raw(c, w, NT)
        _, vjp = jax.vjp(_rms_fn, xv, gv)
        dx, dg = vjp(dh)
        return dx + dresv, dg

    return _rowwise(fn, [*cts, x, dres], [*weights_t, g], [(x.shape[1], F32)], [g.shape],
                    tile=FUSED_TILE, nsub=FUSED_CHUNKS, name=name, hosted=hosted, scatter=scatter)


def _head_sum_matrix():
    i = lax.broadcasted_iota(jnp.int32, (RWKV_WIDTH, RWKV_WIDTH), 0) // HEAD_DIM
    j = lax.broadcasted_iota(jnp.int32, (RWKV_WIDTH, RWKV_WIDTH), 1) // HEAD_DIM
    return (i == j).astype(BF16)


def _head_sums_raw(x, esum):
    hi = x.astype(BF16)
    lo = (x - hi.astype(F32)).astype(BF16)
    return _dot(hi, esum) + _dot(lo, esum)


@jax.custom_vjp
def _head_sums(x, esum):
    return _head_sums_raw(x, esum)


_head_sums.defvjp(lambda x, esum: (_head_sums_raw(x, esum), esum),
                  lambda esum, ct: (_head_sums_raw(ct, esum), jnp.zeros_like(esum)))


def _prep_core(xr, xk, xv, xwa, xg, w0, w2p, a0, a2p, g2, k_k, k_a, esum):
    lw = -DECAY_SCALE * _sigmoid(w0 + _bdot_nn(jnp.tanh(xwa), w2p))
    a = _sigmoid(a0 + _bdot_nn(xwa, a2p))
    g = _bdot_nn(_sigmoid(xg), g2)
    kk0 = xk * k_k
    kk = kk0 * jnp.minimum(lax.rsqrt(_head_sums(kk0 * kk0, esum)), 1.0 / L2_EPS)
    k = xk * (1.0 + (a - 1.0) * k_a)
    return xr, lw, k, xv, kk, a, g


_SEGS = ((0, 512), (512, 1024), (1024, 1536), (1536, 1664), (1664, 1792))


PREP_TILE = 256
SUBLANES = 8


def _shifted_tokens(z_ref, zprev_ref, tile_index, seq):
    zc = z_ref[...]
    start = (tile_index * PREP_TILE) % seq == 0
    before = jnp.where(start, 0.0, zprev_ref[SUBLANES - 1:SUBLANES, :])
    rowid = lax.broadcasted_iota(jnp.int32, zc.shape, 0)
    return zc, jnp.where(rowid == 0, before, pltpu.roll(zc, 1, 0))


def _prep_specs(z, mu, pars, index):
    width = z.shape[1]
    per = PREP_TILE // SUBLANES
    return ([pl.BlockSpec((PREP_TILE, width), lambda i: (index(i), 0)),
             pl.BlockSpec((SUBLANES, width), lambda i: (jnp.maximum(index(i) * per - 1, 0), 0))],
            [pl.BlockSpec(p.shape, lambda i: (0, 0)) for p in (mu, *pars)])


def _prep_fwd(z, seq, mu, pars):
    rows = z.shape[0]
    npar = len(pars)

    def body(z_ref, zprev_ref, mu_ref, *rest):
        par_refs, out_refs = rest[:npar], rest[npar:]
        zc, zp = _shifted_tokens(z_ref, zprev_ref, pl.program_id(0), seq)
        zs = zc + (zp - zc) * mu_ref[...]
        outs = _prep_core(*[zs[:, a:b] for a, b in _SEGS], *[p[...] for p in par_refs])
        for ref, o in zip(out_refs, outs):
            ref[...] = o

    zspecs, pspecs = _prep_specs(z, mu, pars, lambda i: i)
    return pl.pallas_call(
        body,
        grid=(rows // PREP_TILE,),
        in_specs=zspecs + pspecs,
        out_specs=[pl.BlockSpec((PREP_TILE, RWKV_WIDTH), lambda i: (i, 0))] * 7,
        out_shape=[jax.ShapeDtypeStruct((rows, RWKV_WIDTH), F32)] * 7,
        compiler_params=_cparams(("parallel",), 56),
        name="rwkv_prep_fwd",
    )(z, z, mu, *pars)


def _prep_bwd(z, seq, cts, mu, pars):
    rows, width = z.shape
    ntile = rows // PREP_TILE
    npar, nct = len(pars), len(cts)
    acc_shapes = [(1, b - a) for a, b in _SEGS] + [p.shape for p in pars[:-1]]

    def body(z_ref, zprev_ref, *rest):
        ct_refs = rest[:nct]
        mu_ref = rest[nct]
        par_refs = rest[nct + 1:nct + 1 + npar]
        dz_ref = rest[nct + 1 + npar]
        acc_refs = rest[nct + 2 + npar:-1]
        carry = rest[-1]
        step = pl.program_id(0)
        tile_index = ntile - 1 - step

        @pl.when(step == 0)
        def _():
            carry[...] = jnp.zeros_like(carry)

        zc, zp = _shifted_tokens(z_ref, zprev_ref, tile_index, seq)
        mu_v = mu_ref[...]
        diff = zp - zc
        zs = zc + diff * mu_v
        dra, drb, dlw, dka, dkb, dva, dvb, dkk, da, dg = [c[...] for c in ct_refs]
        pv = [p[...] for p in par_refs]
        _, vjp = jax.vjp(lambda *args: _prep_core(*args, pv[-1]), *[zs[:, a:b] for a, b in _SEGS], *pv[:-1])
        grads = vjp((dra + drb, dlw, dka + dkb, dva + dvb, dkk, da, dg))
        dsegs, dpars = grads[:5], grads[5:]
        last_of_sequence = ((tile_index + 1) * PREP_TILE) % seq == 0
        accs = []
        for ds, (a, b) in zip(dsegs, _SEGS):
            mu_s = mu_v[:, a:b]
            dzp = ds * mu_s
            after = jnp.where(last_of_sequence, 0.0, carry[0:1, a:b])
            rowid = lax.broadcasted_iota(jnp.int32, dzp.shape, 0)
            from_next = jnp.where(rowid == PREP_TILE - 1, after, pltpu.roll(dzp, PREP_TILE - 1, 0))
            dz_ref[:, a:b] = (ds * (1.0 - mu_s) + from_next).astype(dz_ref.dtype)
            carry[:, a:b] = dzp[0:SUBLANES, :]
            accs.append(jnp.sum(ds * diff[:, a:b], axis=0, keepdims=True))
        accs.extend(dpars)

        def accumulate(ref, o):
            @pl.when(step == 0)
            def _():
                ref[...] = o

            @pl.when(step > 0)
            def _():
                ref[...] += o

        for ref, o in zip(acc_refs, accs):
            accumulate(ref, o)

    rev = lambda i: ntile - 1 - i
    zspecs, pspecs = _prep_specs(z, mu, pars, rev)
    return pl.pallas_call(
        body,
        grid=(ntile,),
        in_specs=zspecs + [pl.BlockSpec((PREP_TILE, RWKV_WIDTH), lambda i: (rev(i), 0))] * nct + pspecs,
        out_specs=[pl.BlockSpec((PREP_TILE, width), lambda i: (rev(i), 0))]
        + [pl.BlockSpec(s, lambda i: (0, 0)) for s in acc_shapes],
        out_shape=[jax.ShapeDtypeStruct((rows, width), BF16)] + [jax.ShapeDtypeStruct(s, F32) for s in acc_shapes],
        scratch_shapes=[pltpu.VMEM((SUBLANES, width), F32)],
        compiler_params=_cparams(("arbitrary",), 56),
        name="rwkv_prep_bwd",
    )(z, z, *cts, mu, *pars)


def _post_fn(y, r, k, v, g, ln_w, ln_b, r_k, esum):
    mean = _head_sums(y, esum) * (1.0 / HEAD_DIM)
    yc = y - mean
    var = _head_sums(yc * yc, esum) * (1.0 / HEAD_DIM)
    yn = yc * lax.rsqrt(var + GN_EPS) * ln_w + ln_b
    bonus = _head_sums(r * k * r_k, esum) * v
    return (yn + bonus) * g


def _post_out_proj_norm(y, r, k, v, g, attn_out, x, pars, w_attn_rows, w_rwkv_rows, g_norm):
    npar = len(pars)

    def fn(yv, rv, kv, vv, gv, av, xv, *rest):
        wa, wr, gn = rest[npar:]
        rw = _post_fn(yv, rv, kv, vv, gv, *rest[:npar])
        x1 = xv + _bdot_raw(av, wa, NN) + _bdot_raw(rw, wr, NN)
        return rw, x1, _rms_fn(x1, gn)

    d = x.shape[1]
    return _rowwise(fn, [y, r, k, v, g, attn_out, x], [*pars, w_attn_rows, w_rwkv_rows, g_norm],
                    [(RWKV_WIDTH, BF16), (d, F32), (d, BF16)], [],
                    tile=FUSED_TILE, nsub=FUSED_CHUNKS, name="rwkv_post_out_proj_mlp_norm")


def _out_proj_bwd_post_bwd(dx1, y, r, k, v, g, pars, w_attn_rows, w_rwkv_rows):
    npar = len(pars)

    def fn(dxv, yv, rv, kv, vv, gv, *rest):
        wa, wr = rest[npar:]
        esum = rest[npar - 1]
        d_attn = _bdot_raw(dxv, wa, NT)
        d_rw = _bdot_raw(dxv, wr, NT)
        _, vjp = jax.vjp(lambda *a: _post_fn(*a, esum), yv, rv, kv, vv, gv, *rest[:npar - 1])
        return (d_attn, *vjp(d_rw))

    return _rowwise(fn, [dx1, y, r, k, v, g], [*pars, w_attn_rows, w_rwkv_rows], [(RWKV_WIDTH, F32)] * 6,
                    [p.shape for p in pars[:-1]], tile=FUSED_TILE, nsub=FUSED_CHUNKS, name="out_proj_bwd_rwkv_post_bwd")


def _tri_inverses(ms):
    n = ms[0].shape[0]
    row = lax.broadcasted_iota(jnp.int32, (n, n), 0)
    col = lax.broadcasted_iota(jnp.int32, (n, n), 1)
    eye = jnp.where(row == col, 1.0, 0.0)
    t_inv = [eye + m for m in ms]
    power = [_bdot_raw(m, m, NN) for m in ms]
    steps = int(math.log2(n // 2)) - 1
    for step in range(steps):
        if step < steps - 1:
            both = [_bdot_raw(jnp.concatenate([p, t], axis=0), p, NN) for p, t in zip(power, t_inv)]
            power = [b[:n] for b in both]
            t_inv = [t + b[n:] for t, b in zip(t_inv, both)]
        else:
            t_inv = [t + _bdot_raw(t, p, NN) for t, p in zip(t_inv, power)]
    return t_inv


@jax.custom_vjp
def _tri_solve(ms, xs):
    return tuple(_bdot_raw(t, x, NN) for t, x in zip(_tri_inverses(ms), xs))


def _tri_solve_fwd(ms, xs):
    t_inv = _tri_inverses(ms)
    us = tuple(_bdot_raw(t, x, NN) for t, x in zip(t_inv, xs))
    return us, (tuple(t_inv), us)


def _tri_solve_bwd(res, dus):
    t_inv, us = res
    dxs = tuple(_bdot_raw(t, du, TN) for t, du in zip(t_inv, dus))
    dms = tuple(_bdot_raw(dx, u, NT) for dx, u in zip(dxs, us))
    return dms, dxs


_tri_solve.defvjp(_tri_solve_fwd, _tri_solve_bwd)


def _chunk_fn(ss, rs, lws, ks, vs, kks, als):
    c = rs[0].shape[0]
    n = 2 * c
    row = lax.broadcasted_iota(jnp.int32, (n, n), 0)
    col = lax.broadcasted_iota(jnp.int32, (n, n), 1)
    incl = (row % c) >= (col % c)
    strict = (row % c) > (col % c)
    lane = lax.broadcasted_iota(jnp.int32, (1, LANES), 1)
    m_lo = jnp.where(lane < HEAD_DIM, 1.0, 0.0)
    m_hi = 1.0 - m_lo

    def stack(a):
        return jnp.concatenate([a * m_lo, a * m_hi], axis=0)

    cums = [_cumsum_rows(lw) for lw in lws]
    totals = [jnp.sum(lw, axis=0, keepdims=True) for lw in lws]
    bs = [kk * al for kk, al in zip(kks, als)]
    grows = [jnp.exp(-cum) for cum in cums]
    a_s = [stack(-kk * jnp.exp(cum - lw)) for kk, cum, lw in zip(kks, cums, lws)]
    b_s = [stack(b * g) for b, g in zip(bs, grows)]
    k_s = [stack(k * g) for k, g in zip(ks, grows)]
    r_s = [stack(r * jnp.exp(cum)) for r, cum in zip(rs, cums)]
    v_s = [stack(v) for v in vs]
    pair = lambda p, q: jnp.concatenate([p, q], axis=0)
    ar_s = [pair(a, r) for a, r in zip(a_s, r_s)]
    blocks = [_quarters(_bdot(ar, pair(b, k), NT)) for ar, b, k in zip(ar_s, b_s, k_s)]
    m_ab = [jnp.where(strict, q[0], 0.0) for q in blocks]
    m_ak = [jnp.where(strict, q[1], 0.0) for q in blocks]
    m_rb = [jnp.where(incl, q[2], 0.0) for q in blocks]
    m_rk = [jnp.where(incl, q[3], 0.0) for q in blocks]
    from_state = [_halves(_bdot(ar, s, NT)) for ar, s in zip(ar_s, ss)]
    from_v = [_halves(_bdot(pair(mk, mr), v)) for mk, mr, v in zip(m_ak, m_rk, v_s)]
    u = _tri_solve(tuple(m_ab), tuple(fs[0] + fv[0] for fs, fv in zip(from_state, from_v)))
    y = [_fold_rows(fs[1] + _bdot(mb, uu) + fv[1]) for fs, mb, uu, fv in zip(from_state, m_rb, u, from_v)]
    tails = [jnp.exp(tot - cum) for tot, cum in zip(totals, cums)]
    s_new = [s * jnp.exp(tot) + _bdot(pair(uu, v), pair(stack(b * tl), stack(k * tl)), TN)
             for s, tot, uu, b, tl, v, k in zip(ss, totals, u, bs, tails, v_s, ks)]
    return tuple(y), tuple(s_new)


def _chains(bsz, npair):
    return [(b, p, slice(p * LANES, (p + 1) * LANES)) for b in range(bsz) for p in range(npair)]


def _hosted_exchange(first, last, ex_in, ex_out, sems, scatter):
    if not ex_in:
        return

    @pl.when(first)
    def _():
        _exchange_start(_exchange_copies(ex_in, ex_out, *sems, scatter, arrivals=False))

    @pl.when(last)
    def _():
        _exchange_wait(_exchange_copies(ex_in, ex_out, *sems, scatter, arrivals=True))


def _rwkv_fwd(r, lw, k, v, kk, al, hosted=(), scatter=False):
    bsz, t, w = r.shape
    npair, nchunk = w // LANES, t // CHUNK
    chains = _chains(bsz, npair)
    nex = len(hosted)

    def body(*refs):
        r_ref, lw_ref, k_ref, v_ref, kk_ref, al_ref = refs[:6]
        ex_in = refs[6:6 + nex]
        y_ref, sall_ref = refs[6 + nex:8 + nex]
        ex_out = refs[8 + nex:8 + 2 * nex]
        s_scr = refs[8 + 2 * nex]
        step = pl.program_id(0)

        @pl.when(step == 0)
        def _():
            s_scr[...] = jnp.zeros_like(s_scr)

        _hosted_exchange(step == 0, step == nchunk - 1, ex_in, ex_out, refs[9 + 2 * nex:], scatter)
        ss = tuple(s_scr[i] for i in range(len(chains)))
        for i, s in enumerate(ss):
            sall_ref[0, i] = s
        ys, s_new = _chunk_fn(ss, *[tuple(ref[b, :, cols] for b, _, cols in chains)
                                    for ref in (r_ref, lw_ref, k_ref, v_ref, kk_ref, al_ref)])
        for i, (b, _, cols) in enumerate(chains):
            y_ref[b, :, cols] = ys[i]
            s_scr[i] = s_new[i]

    spec = pl.BlockSpec((bsz, CHUNK, w), lambda c: (0, c, 0))
    hbm = pl.BlockSpec(memory_space=pltpu.HBM)
    outs = pl.pallas_call(
        body,
        grid=(nchunk,),
        in_specs=[spec] * 6 + [hbm] * nex,
        out_specs=[spec, pl.BlockSpec((1, len(chains), LANES, LANES), lambda c: (c, 0, 0, 0))] + [hbm] * nex,
        out_shape=[jax.ShapeDtypeStruct((bsz, t, w), F32),
                   jax.ShapeDtypeStruct((nchunk, len(chains), LANES, LANES), F32)]
        + _exchange_out_shapes(hosted, scatter),
        scratch_shapes=[pltpu.VMEM((len(chains), LANES, LANES), F32)] + (_exchange_scratch(nex) if nex else []),
        compiler_params=_cparams(("arbitrary",), 48),
        name="rwkv_chunk_fwd",
    )(r, lw, k, v, kk, al, *hosted)
    return outs[0], outs[1], outs[2:]


def _rwkv_bwd(r, lw, k, v, kk, al, s_all, dy, hosted=(), scatter=False):
    bsz, t, w = r.shape
    npair, nchunk = w // LANES, t // CHUNK
    chains = _chains(bsz, npair)
    nex = len(hosted)

    def body(*refs):
        r_ref, lw_ref, k_ref, v_ref, kk_ref, al_ref, s_ref, dy_ref = refs[:8]
        ex_in = refs[8:8 + nex]
        out_refs = refs[8 + nex:14 + nex]
        ex_out = refs[14 + nex:14 + 2 * nex]
        ds_scr = refs[14 + 2 * nex]
        step = pl.program_id(0)

        @pl.when(step == 0)
        def _():
            ds_scr[...] = jnp.zeros_like(ds_scr)

        _hosted_exchange(step == 0, step == nchunk - 1, ex_in, ex_out, refs[15 + 2 * nex:], scatter)
        ss = tuple(s_ref[0, i] for i in range(len(chains)))
        _, vjp = jax.vjp(_chunk_fn, ss, *[tuple(ref[b, :, cols] for b, _, cols in chains)
                                          for ref in (r_ref, lw_ref, k_ref, v_ref, kk_ref, al_ref)])
        grads = vjp((tuple(dy_ref[b, :, cols] for b, _, cols in chains),
                     tuple(ds_scr[i] for i in range(len(chains)))))
        for i, (b, _, cols) in enumerate(chains):
            ds_scr[i] = grads[0][i]
            for ref, gval in zip(out_refs, grads[1:]):
                ref[b, :, cols] = gval[i]

    spec = pl.BlockSpec((bsz, CHUNK, w), lambda c: (0, nchunk - 1 - c, 0))
    sspec = pl.BlockSpec((1, len(chains), LANES, LANES), lambda c: (nchunk - 1 - c, 0, 0, 0))
    hbm = pl.BlockSpec(memory_space=pltpu.HBM)
    outs = pl.pallas_call(
        body,
        grid=(nchunk,),
        in_specs=[spec] * 6 + [sspec, spec] + [hbm] * nex,
        out_specs=[spec] * 6 + [hbm] * nex,
        out_shape=[jax.ShapeDtypeStruct((bsz, t, w), F32)] * 6 + _exchange_out_shapes(hosted, scatter),
        scratch_shapes=[pltpu.VMEM((len(chains), LANES, LANES), F32)] + (_exchange_scratch(nex) if nex else []),
        compiler_params=_cparams(("arbitrary",), 48),
        name="rwkv_chunk_bwd",
    )(r, lw, k, v, kk, al, s_all, dy, *hosted)
    return outs[:6], outs[6:]


def _alibi_slope(head):
    return 2.0 ** (-8.0 * (head + 1) / N_ATTN_HEADS)


def _attn_block(qs, kp, kc, vp, vc, sinks, first):
    row = lax.broadcasted_iota(jnp.int32, (BLOCK, 2 * BLOCK), 0)
    col = lax.broadcasted_iota(jnp.int32, (BLOCK, 2 * BLOCK), 1)
    lane = lax.broadcasted_iota(jnp.int32, (1, LANES), 1)
    halves = [jnp.where((lane // HEAD_DIM) == half, 1.0, 0.0) for half in range(2)]
    srow = lax.broadcasted_iota(jnp.int32, (LANES, LANES), 0)
    scol = lax.broadcasted_iota(jnp.int32, (LANES, LANES), 1)
    swap = jnp.where((srow + HEAD_DIM) % LANES == scol, 1.0, 0.0)
    dist = row - col + BLOCK
    valid = jnp.logical_and(jnp.logical_and(dist >= 0, dist < BLOCK),
                            jnp.logical_or(col >= BLOCK, jnp.logical_not(first)))
    dist = dist.astype(F32)
    scale = HEAD_DIM ** -0.5
    stored = (jnp.concatenate([kp, kc], axis=0), jnp.concatenate([vp, vc], axis=0))
    swapped = tuple(_bdot_nn(t, swap) for t in stored)
    heads = [(pair, half) for pair in range(len(qs)) for half in range(2)]
    kv = [stored if half == pair // 2 else swapped for pair, half in heads]
    slopes = [_alibi_slope(2 * pair + half) for pair, half in heads]
    qa = [qs[pair] * halves[half] for pair, half in heads]
    s = [jnp.where(valid, _bdot_nt(q, t[0]) * scale - sl * dist, NEG_INF) for q, t, sl in zip(qa, kv, slopes)]
    mx = [lax.stop_gradient(jnp.maximum(jnp.max(a, axis=-1, keepdims=True), sk)) for a, sk in zip(s, sinks)]
    e = [jnp.exp(a - m) for a, m in zip(s, mx)]
    es = [jnp.exp(sk - m) for sk, m in zip(sinks, mx)]
    inv = [1.0 / (jnp.sum(a, axis=-1, keepdims=True) + b) for a, b in zip(e, es)]
    o = [_bdot_nn(a * i, t[1]) for a, i, t in zip(e, inv, kv)]
    outs = tuple(o[2 * pair] * halves[0] + o[2 * pair + 1] * halves[1] for pair in range(len(qs)))
    return outs, [lax.stop_gradient(b * i) for b, i in zip(es, inv)]


def _sink_values(sink_ref):
    return [jnp.max(sink_ref[h:h + 1, :], axis=-1, keepdims=True) for h in range(N_ATTN_HEADS)]


def _attn_fwd(z, sink_rows):
    bsz, t, _ = z.shape
    nb = t // BLOCK
    npair = ATTN_WIDTH // LANES

    def body(q_ref, kp_ref, kc_ref, vp_ref, vc_ref, sink_ref, o_ref):
        first = pl.program_id(1) == 0
        qs = tuple(q_ref[0, :, pair * LANES:(pair + 1) * LANES] for pair in range(npair))
        outs, _ = _attn_block(qs, kp_ref[0], kc_ref[0], vp_ref[0], vc_ref[0], _sink_values(sink_ref), first)
        for pair in range(npair):
            o_ref[0, :, pair * LANES:(pair + 1) * LANES] = outs[pair].astype(o_ref.dtype)

    kcol, vcol = ATTN_WIDTH // KV_WIDTH, ATTN_WIDTH // KV_WIDTH + 1
    return pl.pallas_call(
        body,
        grid=(bsz, nb),
        in_specs=[pl.BlockSpec((1, BLOCK, ATTN_WIDTH), lambda b, n: (b, n, 0)),
                  pl.BlockSpec((1, BLOCK, KV_WIDTH), lambda b, n: (b, jnp.maximum(n - 1, 0), kcol)),
                  pl.BlockSpec((1, BLOCK, KV_WIDTH), lambda b, n: (b, n, kcol)),
                  pl.BlockSpec((1, BLOCK, KV_WIDTH), lambda b, n: (b, jnp.maximum(n - 1, 0), vcol)),
                  pl.BlockSpec((1, BLOCK, KV_WIDTH), lambda b, n: (b, n, vcol)),
                  pl.BlockSpec(sink_rows.shape, lambda b, n: (0, 0))],
        out_specs=pl.BlockSpec((1, BLOCK, ATTN_WIDTH), lambda b, n: (b, n, 0)),
        out_shape=jax.ShapeDtypeStruct((bsz, t, ATTN_WIDTH), BF16),
        compiler_params=_cparams(("parallel", "arbitrary"), 48),
        name="swa_fwd",
    )(z, z, z, z, z, sink_rows)


def _attn_bwd(z, dout, sink_rows):
    bsz, t, _ = z.shape
    nb = t // BLOCK
    npair = ATTN_WIDTH // LANES

    def body(q_ref, kp_ref, kc_ref, vp_ref, vc_ref, do_ref, sink_ref, dz_ref, dsink_ref, carry):
        step = pl.program_id(1)
        n = nb - 1 - step
        first = n == 0

        @pl.when(step == 0)
        def _():
            carry[...] = jnp.zeros_like(carry)

        @pl.when(jnp.logical_and(step == 0, pl.program_id(0) == 0))
        def _():
            dsink_ref[...] = jnp.zeros_like(dsink_ref)

        lane = lax.broadcasted_iota(jnp.int32, (1, LANES), 1)
        qs = tuple(q_ref[0, :, pair * LANES:(pair + 1) * LANES] for pair in range(npair))
        dos = tuple(do_ref[0, :, pair * LANES:(pair + 1) * LANES] for pair in range(npair))
        fn = functools.partial(_attn_block, sinks=_sink_values(sink_ref), first=first)
        outs, vjp, psinks = jax.vjp(fn, qs, kp_ref[0], kc_ref[0], vp_ref[0], vc_ref[0], has_aux=True)
        dqs, dkp, dkc, dvp, dvc = vjp(dos)
        for pair in range(npair):
            dz_ref[0, :, pair * LANES:(pair + 1) * LANES] = dqs[pair].astype(dz_ref.dtype)
            for half in range(2):
                m = jnp.where((lane // HEAD_DIM) == half, 1.0, 0.0)
                delta = jnp.sum(dos[pair] * outs[pair] * m, axis=-1, keepdims=True)
                head = 2 * pair + half
                ds = -jnp.sum(psinks[head] * delta, axis=0, keepdims=True)
                dsink_ref[head:head + 1, :] += jnp.broadcast_to(ds, (1, LANES))
        dz_ref[0, :, ATTN_WIDTH:ATTN_WIDTH + KV_WIDTH] = (dkc + carry[0]).astype(dz_ref.dtype)
        dz_ref[0, :, ATTN_WIDTH + KV_WIDTH:QKV_WIDTH] = (dvc + carry[1]).astype(dz_ref.dtype)
        carry[0] = dkp
        carry[1] = dvp

    kcol, vcol = ATTN_WIDTH // KV_WIDTH, ATTN_WIDTH // KV_WIDTH + 1
    rev = lambda n: nb - 1 - n
    return pl.pallas_call(
        body,
        grid=(bsz, nb),
        in_specs=[pl.BlockSpec((1, BLOCK, ATTN_WIDTH), lambda b, n: (b, rev(n), 0)),
                  pl.BlockSpec((1, BLOCK, KV_WIDTH), lambda b, n: (b, jnp.maximum(rev(n) - 1, 0), kcol)),
                  pl.BlockSpec((1, BLOCK, KV_WIDTH), lambda b, n: (b, rev(n), kcol)),
                  pl.BlockSpec((1, BLOCK, KV_WIDTH), lambda b, n: (b, jnp.maximum(rev(n) - 1, 0), vcol)),
                  pl.BlockSpec((1, BLOCK, KV_WIDTH), lambda b, n: (b, rev(n), vcol)),
                  pl.BlockSpec((1, BLOCK, ATTN_WIDTH), lambda b, n: (b, rev(n), 0)),
                  pl.BlockSpec(sink_rows.shape, lambda b, n: (0, 0))],
        out_specs=[pl.BlockSpec((1, BLOCK, QKV_WIDTH), lambda b, n: (b, rev(n), 0)),
                   pl.BlockSpec((N_ATTN_HEADS, LANES), lambda b, n: (0, 0))],
        out_shape=[jax.ShapeDtypeStruct((bsz, t, QKV_WIDTH), BF16),
                   jax.ShapeDtypeStruct((N_ATTN_HEADS, LANES), F32)],
        scratch_shapes=[pltpu.VMEM((2, BLOCK, KV_WIDTH), F32)],
        compiler_params=_cparams(("arbitrary", "arbitrary"), 48),
        name="swa_bwd",
    )(z, z, z, z, z, dout, sink_rows)


def _exchange_out_shapes(arrays, scatter):
    return [jax.ShapeDtypeStruct((N_DEV,) + (a.shape[1:] if scatter else a.shape), a.dtype) for a in arrays]


def _exchange_scratch(n):
    return [pltpu.SemaphoreType.DMA((n, N_DEV - 1)), pltpu.SemaphoreType.DMA((n, N_DEV - 1)),
            pltpu.SemaphoreType.DMA((n,))]


def _exchange_copies(ins, outs, send_sems, recv_sems, local_sems, scatter, arrivals=True):
    x, y, c = lax.axis_index("x"), lax.axis_index("y"), lax.axis_index("c")
    me = 4 * x + 2 * y + c
    copies = []
    for i in range(len(ins)):
        own = pltpu.make_async_copy(ins[i].at[me] if scatter else ins[i], outs[i].at[me], local_sems.at[i])
        copies.append((own, None, True))
        for d in range(1, N_DEV):
            px = 1 - x if d & 4 else x
            py = 1 - y if d & 2 else y
            pc = 1 - c if d & 1 else c
            peer = 4 * px + 2 * py + pc
            src = ins[i].at[peer] if scatter else ins[i]
            send = pltpu.make_async_remote_copy(src, outs[i].at[me], send_sems.at[i, d - 1], recv_sems.at[i, d - 1],
                                                device_id=(px, py, pc), device_id_type=MESH)
            recv = pltpu.make_async_remote_copy(src, outs[i].at[peer], send_sems.at[i, d - 1], recv_sems.at[i, d - 1],
                                                device_id=(px, py, pc), device_id_type=MESH) if arrivals else None
            copies.append((send, recv, False))
    return copies


def _exchange_start(copies):
    for send, _, _ in copies:
        send.start()


def _exchange_wait(copies):
    for send, recv, local in copies:
        if local:
            send.wait()
        else:
            send.wait_send()
            recv.wait_recv()


def _exchange(arrays, *, scatter, name):
    n = len(arrays)

    def body(*refs):
        copies = _exchange_copies(refs[:n], refs[n:2 * n], *refs[2 * n:], scatter)
        _exchange_start(copies)
        _exchange_wait(copies)

    hbm = pl.BlockSpec(memory_space=pltpu.HBM)
    return pl.pallas_call(
        body,
        in_specs=[hbm] * n,
        out_specs=[hbm] * n,
        out_shape=_exchange_out_shapes(arrays, scatter),
        scratch_shapes=_exchange_scratch(n),
        name=name,
    )(*arrays)


def _adamw(parts, w, m, v, name):
    rows, cols = w.shape
    tr = _pick(rows, (256, 128, 64, 8))
    c1 = 1.0 / (1.0 - ADAM_B1 ** ADAM_STEP)
    c2 = 1.0 / (1.0 - ADAM_B2 ** ADAM_STEP)

    def body(p_ref, w_ref, m_ref, v_ref, g_ref, d_ref, mo_ref, vo_ref):
        g = p_ref[0].astype(F32)
        for s in range(1, N_DEV):
            g = g + p_ref[s].astype(F32)
        mn = ADAM_B1 * m_ref[...] + (1.0 - ADAM_B1) * g
        vn = ADAM_B2 * v_ref[...] + (1.0 - ADAM_B2) * (g * g)
        g_ref[...] = g
        mo_ref[...] = mn
        vo_ref[...] = vn
        d_ref[...] = -ADAM_LR * ((mn * c1) / (jnp.sqrt(vn * c2) + ADAM_EPS) + ADAM_WD * w_ref[...])

    spec = pl.BlockSpec((tr, cols), lambda i: (i, 0))
    return pl.pallas_call(
        body,
        grid=(rows // tr,),
        in_specs=[pl.BlockSpec((N_DEV, tr, cols), lambda i: (0, i, 0)), spec, spec, spec],
        out_specs=[spec] * 4,
        out_shape=[jax.ShapeDtypeStruct((rows, cols), F32)] * 4,
        compiler_params=_cparams(("parallel",), 48),
        name=name,
    )(parts, w, m, v)


_VECTOR_PARAMS = ("attn_norm_g", "attn_sinks", "rwkv_mu", "w0", "a0", "k_k", "k_a", "r_k", "ln_x_w", "ln_x_b",
                  "mlp_norm_g", "final_norm_g")
_WEIGHT_NAMES = ("attn_norm_g", "w_in", "attn_sinks", "rwkv_mu", "w0", "w2", "a0", "a2", "g2", "k_k", "k_a", "r_k",
                 "ln_x_w", "ln_x_b", "w_out", "mlp_norm_g", "w_up", "w_down", "final_norm_g")


def _pack_vectors(vals):
    pieces = []
    for name in _VECTOR_PARAMS:
        flat = vals[name].reshape(1, -1)
        pad = (-flat.shape[1]) % LANES
        pieces.append(jnp.pad(flat, ((0, 0), (0, pad))) if pad else flat)
    return jnp.concatenate(pieces, axis=1)


def _unpack_vectors(packed, like):
    out, col = {}, 0
    for name in _VECTOR_PARAMS:
        size = like[name].size
        out[name] = packed[0, col:col + size].reshape(like[name].shape)
        col += size + (-size) % LANES
    return out


def kernel(x, attn_norm_g, w_in, attn_sinks, rwkv_mu, w0, w2, a0, a2, g2, k_k, k_a, r_k, ln_x_w, ln_x_b, w_out, mlp_norm_g, w_up, w_down, final_norm_g, loss_target, m_attn_norm_g, m_w_in, m_attn_sinks, m_rwkv_mu, m_w0, m_w2, m_a0, m_a2, m_g2, m_k_k, m_k_a, m_r_k, m_ln_x_w, m_ln_x_b, m_w_out, m_mlp_norm_g, m_w_up, m_w_down, m_final_norm_g, v_attn_norm_g, v_w_in, v_attn_sinks, v_rwkv_mu, v_w0, v_w2, v_a0, v_a2, v_g2, v_k_k, v_k_a, v_r_k, v_ln_x_w, v_ln_x_b, v_w_out, v_mlp_norm_g, v_w_up, v_w_down, v_final_norm_g):
    weights = dict(attn_norm_g=attn_norm_g, w_in=w_in, attn_sinks=attn_sinks, rwkv_mu=rwkv_mu, w0=w0, w2=w2, a0=a0,
                   a2=a2, g2=g2, k_k=k_k, k_a=k_a, r_k=r_k, ln_x_w=ln_x_w, ln_x_b=ln_x_b, w_out=w_out,
                   mlp_norm_g=mlp_norm_g, w_up=w_up, w_down=w_down, final_norm_g=final_norm_g)
    mom1 = dict(attn_norm_g=m_attn_norm_g, w_in=m_w_in, attn_sinks=m_attn_sinks, rwkv_mu=m_rwkv_mu, w0=m_w0, w2=m_w2,
                a0=m_a0, a2=m_a2, g2=m_g2, k_k=m_k_k, k_a=m_k_a, r_k=m_r_k, ln_x_w=m_ln_x_w, ln_x_b=m_ln_x_b,
                w_out=m_w_out, mlp_norm_g=m_mlp_norm_g, w_up=m_w_up, w_down=m_w_down, final_norm_g=m_final_norm_g)
    mom2 = dict(attn_norm_g=v_attn_norm_g, w_in=v_w_in, attn_sinks=v_attn_sinks, rwkv_mu=v_rwkv_mu, w0=v_w0, w2=v_w2,
                a0=v_a0, a2=v_a2, g2=v_g2, k_k=v_k_k, k_a=v_k_a, r_k=v_r_k, ln_x_w=v_ln_x_w, ln_x_b=v_ln_x_b,
                w_out=v_w_out, mlp_norm_g=v_mlp_norm_g, w_up=v_w_up, w_down=v_w_down, final_norm_g=v_final_norm_g)
    bsz, seq, d_model = x.shape
    rows = bsz * seq
    d_in = N_DEV * w_in.shape[2]
    d_ff = N_DEV * w_up.shape[2]

    gathered = _exchange([w_in[0].astype(BF16), w2[0], a2[0], g2[0]], scatter=False, name="gather_in_weights")
    cols_first = lambda a: a.transpose(1, 0, 2).reshape(a.shape[1], -1)
    w_in_f = cols_first(gathered[0])
    w_attn, w_rw = w_in_f[:, :QKV_WIDTH], w_in_f[:, QKV_WIDTH:]
    w2_f, a2_f, g2_f = cols_first(gathered[1]), cols_first(gathered[2]), cols_first(gathered[3])
    lora = w2_f.shape[0]
    w2p = jnp.concatenate([w2_f, jnp.zeros_like(a2_f)], axis=0)
    a2p = jnp.concatenate([jnp.zeros_like(w2_f), a2_f], axis=0)

    esum = _head_sum_matrix()
    sink_rows = jnp.broadcast_to(attn_sinks.reshape(N_ATTN_HEADS, 1), (N_ATTN_HEADS, LANES))
    prep_pars = [w0, w2p, a0, a2p, g2_f, k_k, k_a, esum]
    post_pars = [ln_x_w, ln_x_b, r_k, esum]

    x2d = x.reshape(rows, d_model)
    h1, z_attn, z_rw = _norm_in_proj(x2d, attn_norm_g, w_attn, w_rw)
    z_attn3 = z_attn.reshape(bsz, seq, QKV_WIDTH)
    attn_out = _attn_fwd(z_attn3, sink_rows)
    r, lw, k, v, kk, al, gate = _prep_fwd(z_rw, seq, rwkv_mu, prep_pars)
    as3 = lambda a: a.reshape(bsz, seq, RWKV_WIDTH)
    y, s_all, late = _rwkv_fwd(as3(r), as3(lw), as3(k), as3(v), as3(kk), as3(al),
                               hosted=[w_out[0].astype(BF16), w_up[0].astype(BF16), w_down[0].astype(BF16)])
    w_out_f = late[0].reshape(-1, d_model)
    w_up_f = cols_first(late[1])
    w_down_f = late[2].reshape(-1, d_model)
    y2 = y.reshape(rows, RWKV_WIDTH)
    attn_out2d = attn_out.reshape(rows, ATTN_WIDTH)
    w_out_attn, w_out_rw = w_out_f[:ATTN_WIDTH], w_out_f[ATTN_WIDTH:]
    rw_out, x1, h2 = _post_out_proj_norm(y2, r, k, v, gate, attn_out2d, x2d, post_pars, w_out_attn, w_out_rw,
                                         mlp_norm_g)

    def relu_sq(acc):
        pos = jnp.maximum(acc, 0.0)
        return acc, pos * pos

    u, act = _matmul(h2, w_up_f, "nn", name="mlp_up", epilogue=relu_sq, out_dtypes=(BF16, BF16))
    dx2, loss_vec, g_final = _down_proj_loss(act, w_down_f, x1, loss_target.reshape(rows, d_model),
                                             final_norm_g.reshape(1, d_model))

    g_w_down = _matmul(act, dx2, "tn", name="grad_w_down", out_dtypes=(BF16,))
    du = _matmul(dx2, w_down_f, "nt", name="mlp_down_bwd", extras=(u,), out_dtypes=(BF16,),
                 epilogue=lambda acc, uv: (acc * (2.0 * jnp.maximum(uv.astype(F32), 0.0)),))
    g_w_up = _matmul(h2, du, "tn", name="grad_w_up", out_dtypes=(BF16,))
    dx1, g_mlp_norm = _proj_bwd_norm_bwd([du], [w_up_f], x1, dx2, mlp_norm_g, "mlp_up_bwd_norm_bwd")
    g_w_out = jnp.concatenate([_matmul(attn_out2d, dx1, "tn", name="grad_w_out_attn", out_dtypes=(BF16,)),
                               _matmul(rw_out, dx1, "tn", name="grad_w_out_rwkv", out_dtypes=(BF16,))], axis=0)
    d_attn_out, dy, dr_a, dk_a, dv_a, dgate, g_ln_w, g_ln_b, g_r_k = _out_proj_bwd_post_bwd(
        dx1, y2, r, k, v, gate, post_pars, w_out_attn, w_out_rw)
    by_cols = lambda a: a.reshape(a.shape[0], N_DEV, -1).transpose(1, 0, 2)
    (dr_b, dlw, dk_b, dv_b, dkk, dal), (p_w_out, p_w_up, p_w_down) = _rwkv_bwd(
        as3(r), as3(lw), as3(k), as3(v), as3(kk), as3(al), s_all, as3(dy),
        hosted=[g_w_out.reshape(N_DEV, -1, d_model), by_cols(g_w_up), g_w_down.reshape(N_DEV, -1, d_model)],
        scatter=True)
    flat = lambda a: a.reshape(rows, RWKV_WIDTH)
    (dz_rw, gmu_r, gmu_k, gmu_v, gmu_wa, gmu_g, g_w0, g_w2p, g_a0, g_a2p, g_g2, g_k_k, g_k_a) = _prep_bwd(
        z_rw, seq, [dr_a, flat(dr_b), flat(dlw), dk_a, flat(dk_b), dv_a, flat(dv_b), flat(dkk), flat(dal), dgate],
        rwkv_mu, prep_pars)
    dz_attn, g_sink_rows = _attn_bwd(z_attn3, d_attn_out.reshape(bsz, seq, ATTN_WIDTH), sink_rows)
    dz_attn = dz_attn.reshape(rows, QKV_WIDTH)
    g_w_in = jnp.concatenate([_matmul(h1, dz_attn, "tn", name="grad_w_in_attn", out_dtypes=(BF16,)),
                              _matmul(h1, dz_rw, "tn", name="grad_w_in_rwkv", out_dtypes=(BF16,))], axis=1)
    lora_grads = jnp.concatenate([g_w2p[:lora], g_a2p[lora:], g_g2], axis=0)
    (dx, g_attn_norm), (p_w_in, p_lora) = _proj_bwd_norm_bwd(
        [dz_attn, dz_rw], [w_attn, w_rw], x2d, dx1, attn_norm_g, "in_proj_bwd_norm_bwd",
        hosted=[by_cols(g_w_in), by_cols(lora_grads)], scatter=True)

    vec_grads = dict(attn_norm_g=g_attn_norm, attn_sinks=g_sink_rows[:, 0], rwkv_mu=jnp.concatenate(
        [gmu_r, gmu_k, gmu_v, gmu_wa, gmu_g], axis=1), w0=g_w0, a0=g_a0, k_k=g_k_k, k_a=g_k_a, r_k=g_r_k,
        ln_x_w=g_ln_w, ln_x_b=g_ln_b, mlp_norm_g=g_mlp_norm, final_norm_g=g_final)
    packed = _pack_vectors(vec_grads)
    nvec = packed.shape[1]
    everyone = _exchange([jnp.concatenate([packed, loss_vec], axis=1)], scatter=False, name="gather_vector_grads")[0]
    vec_parts = everyone[:, :, :nvec]
    loss = jnp.sum(everyone[:, 0, nvec])

    grads, delta, new_m, new_v = {}, {}, {}, {}

    def update(name, part, shape2d):
        res = _adamw(part, weights[name].reshape(shape2d), mom1[name].reshape(shape2d), mom2[name].reshape(shape2d),
                     "adamw_" + name)
        for store, val in zip((grads, delta, new_m, new_v), res):
            store[name] = val.reshape(weights[name].shape)

    update("w_in", p_w_in, w_in.shape[1:])
    update("w_out", p_w_out, w_out.shape[1:])
    update("w_up", p_w_up, w_up.shape[1:])
    update("w_down", p_w_down, w_down.shape[1:])
    stack = lambda d: jnp.concatenate([d["w2"][0], d["a2"][0], d["g2"][0]], axis=0)
    lora_res = _adamw(p_lora, stack(weights), stack(mom1), stack(mom2), "adamw_lora")
    for store, val in zip((grads, delta, new_m, new_v), lora_res):
        store["w2"], store["a2"], store["g2"] = val[None, :lora], val[None, lora:2 * lora], val[None, 2 * lora:]
    vec_res = _adamw(vec_parts, _pack_vectors(weights), _pack_vectors(mom1), _pack_vectors(mom2), "adamw_vectors")
    for store, val in zip((grads, delta, new_m, new_v), vec_res):
        store.update(_unpack_vectors(val, weights))

    return (loss, dx.reshape(x.shape), *[grads[n] for n in _WEIGHT_NAMES], *[delta[n] for n in _WEIGHT_NAMES],
            *[new_m[n] for n in _WEIGHT_NAMES], *[new_v[n] for n in _WEIGHT_NAMES])
```

```python
import functools
import math

import jax
import jax.numpy as jnp
from jax import lax
from jax.experimental import pallas as pl
from jax.experimental.pallas import tpu as pltpu

F32 = jnp.float32
BF16 = jnp.bfloat16

N_DEV = 8
HEAD_DIM = 64
LANES = 128
N_ATTN_HEADS = 8
ATTN_WIDTH = 512
KV_WIDTH = 128
QKV_WIDTH = ATTN_WIDTH + 2 * KV_WIDTH
RWKV_WIDTH = 512
LORA_WA = 128
GATE_LORA = 128
RWKV_SHIFT_WIDTH = 3 * RWKV_WIDTH + LORA_WA + GATE_LORA
BLOCK = 128
CHUNK = 64
RMS_EPS = 1e-6
GN_EPS = 64e-5
L2_EPS = 1e-12
NEG_INF = -1e30
DECAY_SCALE = math.exp(-0.5)
ADAM_LR, ADAM_B1, ADAM_B2, ADAM_EPS, ADAM_WD, ADAM_STEP = 0.001, 0.9, 0.999, 1e-08, 0.01, 10

NN = (((1,), (0,)), ((), ()))
NT = (((1,), (1,)), ((), ()))
TN = (((0,), (0,)), ((), ()))
MESH = pl.DeviceIdType.MESH


def _dot(a, b, dn=NN, precision=None):
    return lax.dot_general(a, b, dn, precision=precision, preferred_element_type=F32)


def _bdot_raw(a, b, dn):
    return lax.dot_general(a.astype(BF16), b.astype(BF16), dn, preferred_element_type=F32)


@functools.partial(jax.custom_vjp, nondiff_argnums=(2, 3))
def _bdot_c(a, b, ca, cb):
    return _bdot_raw(a, b, (((ca,), (cb,)), ((), ())))


def _bdot_c_fwd(a, b, ca, cb):
    return _bdot_c(a, b, ca, cb), (a, b)


def _bdot_c_bwd(ca, cb, res, ct):
    a, b = res
    fa, fb = 1 - ca, 1 - cb
    da = _bdot_raw(ct, b, (((1,), (fb,)), ((), ()))) if ca == 1 else _bdot_raw(b, ct, (((fb,), (1,)), ((), ())))
    db = _bdot_raw(a, ct, (((fa,), (0,)), ((), ()))) if cb == 0 else _bdot_raw(ct, a, (((0,), (fa,)), ((), ())))
    return da, db


_bdot_c.defvjp(_bdot_c_fwd, _bdot_c_bwd)


def _bdot(a, b, dn=NN):
    return _bdot_c(a, b, dn[0][0][0], dn[0][1][0])


def _bdot_nn(a, b):
    return _bdot(a, b, NN)


def _bdot_nt(a, b):
    return _bdot(a, b, NT)


def _split3(x):
    hi = x.astype(BF16)
    rest = x - hi.astype(F32)
    mid = rest.astype(BF16)
    return hi, mid, (rest - mid.astype(F32)).astype(BF16)


def _running_sum(x, dn):
    c = x.shape[0]
    row = lax.broadcasted_iota(jnp.int32, (c, c), 0)
    col = lax.broadcasted_iota(jnp.int32, (c, c), 1)
    tri = jnp.where(row >= col, 1.0, 0.0).astype(BF16)
    w = x.shape[1]
    parts = lax.dot_general(tri, jnp.concatenate(_split3(x), axis=1), dn, preferred_element_type=F32)
    return parts[:, :w] + parts[:, w:2 * w] + parts[:, 2 * w:]


@jax.custom_vjp
def _cumsum_rows(x):
    return _running_sum(x, NN)


_cumsum_rows.defvjp(lambda x: (_running_sum(x, NN), None), lambda _, ct: (_running_sum(ct, TN),))


@jax.custom_vjp
def _fold_rows(x):
    c = x.shape[0] // 2
    return x[:c] + x[c:]


_fold_rows.defvjp(lambda x: (_fold_rows(x), None), lambda _, ct: (jnp.concatenate([ct, ct], axis=0),))


@jax.custom_vjp
def _halves(x):
    n = x.shape[0] // 2
    return x[:n], x[n:]


_halves.defvjp(lambda x: (_halves(x), None), lambda _, cts: (jnp.concatenate(cts, axis=0),))


@jax.custom_vjp
def _quarters(x):
    n = x.shape[0] // 2
    return x[:n, :n], x[:n, n:], x[n:, :n], x[n:, n:]


_quarters.defvjp(lambda x: (_quarters(x), None),
                 lambda _, cts: (jnp.concatenate([jnp.concatenate(cts[:2], axis=1),
                                                  jnp.concatenate(cts[2:], axis=1)], axis=0),))


@jax.custom_vjp
def _sigmoid(x):
    return 1.0 / (1.0 + jnp.exp(-x))


def _sigmoid_fwd(x):
    s = _sigmoid(x)
    return s, s


_sigmoid.defvjp(_sigmoid_fwd, lambda s, ct: (ct * s * (1.0 - s),))


def _pick(n, cands):
    for c in cands:
        if n % c == 0:
            return c
    return n


def _cparams(sem, vmem_mb=None):
    kw = dict(dimension_semantics=sem)
    if vmem_mb is not None:
        kw["vmem_limit_bytes"] = vmem_mb * 1024 * 1024
    return pltpu.CompilerParams(**kw)


def _matmul(a, b, mode, *, name, extras=(), epilogue=None, out_dtypes=(F32,), tm=1024, tn=1024, tk=1024,
            hosted=(), scatter=False):
    if mode == "nn":
        (M, K), (_, N) = a.shape, b.shape
    elif mode == "tn":
        (K, M), (_, N) = a.shape, b.shape
    else:
        (M, K), (N, _) = a.shape, b.shape
    tm = _pick(M, (tm, 512, 256, 128))
    tn = _pick(N, (tn, 896, 768, 512, 384, 256, 128))
    tk = _pick(K, (tk, 512, 256, 128))
    nk = K // tk
    ne, nout = len(extras), len(out_dtypes)
    if mode == "nn":
        a_spec = pl.BlockSpec((tm, tk), lambda i, j, k: (i, k))
        b_spec = pl.BlockSpec((tk, tn), lambda i, j, k: (k, j))
        dn = NN
    elif mode == "tn":
        a_spec = pl.BlockSpec((tk, tm), lambda i, j, k: (k, i))
        b_spec = pl.BlockSpec((tk, tn), lambda i, j, k: (k, j))
        dn = TN
    else:
        a_spec = pl.BlockSpec((tm, tk), lambda i, j, k: (i, k))
        b_spec = pl.BlockSpec((tn, tk), lambda i, j, k: (j, k))
        dn = NT
    o_spec = pl.BlockSpec((tm, tn), lambda i, j, k: (i, j))
    grid = (M // tm, N // tn, nk)
    nex = len(hosted)

    def body(*refs):
        a_ref, b_ref = refs[:2]
        e_refs = refs[2:2 + ne]
        ex_in = refs[2 + ne:2 + ne + nex]
        o_refs = refs[2 + ne + nex:2 + ne + nex + nout]
        ex_out = refs[2 + ne + nex + nout:2 + ne + 2 * nex + nout]
        scratch = refs[2 + ne + 2 * nex + nout:]
        kstep = pl.program_id(2)
        if nex:
            at = [pl.program_id(d) for d in range(3)]
            first = jnp.logical_and(jnp.logical_and(at[0] == 0, at[1] == 0), at[2] == 0)
            last = jnp.logical_and(jnp.logical_and(at[0] == grid[0] - 1, at[1] == grid[1] - 1), at[2] == grid[2] - 1)
            _hosted_exchange(first, last, ex_in, ex_out, scratch[-3:], scatter)

        def finish(total):
            outs = (total,) if epilogue is None else epilogue(total, *[e[...] for e in e_refs])
            for o_ref, o in zip(o_refs, outs):
                o_ref[...] = o.astype(o_ref.dtype)

        if nk == 1:
            finish(_bdot_raw(a_ref[...], b_ref[...], dn))
            return
        acc = scratch[0]

        @pl.when(kstep == 0)
        def _():
            acc[...] = jnp.zeros_like(acc)

        acc[...] += _bdot_raw(a_ref[...], b_ref[...], dn)

        @pl.when(kstep == nk - 1)
        def _():
            finish(acc[...])

    hbm = pl.BlockSpec(memory_space=pltpu.HBM)
    outs = pl.pallas_call(
        body,
        grid=grid,
        in_specs=[a_spec, b_spec] + [o_spec] * ne + [hbm] * nex,
        out_specs=[o_spec] * nout + [hbm] * nex,
        out_shape=[jax.ShapeDtypeStruct((M, N), dt) for dt in out_dtypes] + _exchange_out_shapes(hosted, scatter),
        scratch_shapes=([pltpu.VMEM((tm, tn), F32)] if nk > 1 else []) + (_exchange_scratch(nex) if nex else []),
        compiler_params=_cparams(("arbitrary",) * 3 if nex else ("parallel", "parallel", "arbitrary"), 56),
        name=name,
    )(a, b, *extras, *hosted)
    if nex:
        return outs[:nout], outs[nout:]
    return outs[0] if nout == 1 else outs


def _rowwise(fn, rows, pars, out_rows, out_accs, *, tile, name, nsub=1, hosted=(), scatter=False):
    rows = [r if isinstance(r, tuple) else (r, r.shape[1], 0) for r in rows]
    R = rows[0][0].shape[0]
    tile = min(tile, R)
    chunk = tile // nsub
    ntile = R // tile
    nr, npar, nor, noa, nex = len(rows), len(pars), len(out_rows), len(out_accs), len(hosted)

    def body(*refs):
        rin = refs[:nr]
        pin = refs[nr:nr + npar]
        ex_in = refs[nr + npar:nr + npar + nex]
        orow = refs[nr + npar + nex:nr + npar + nex + nor]
        oacc = refs[nr + npar + nex + nor:nr + npar + nex + nor + noa]
        ex_out = refs[nr + npar + nex + nor + noa:nr + npar + 2 * nex + nor + noa]
        step = pl.program_id(0)
        _hosted_exchange(step == 0, step == ntile - 1, ex_in, ex_out, refs[nr + npar + 2 * nex + nor + noa:], scatter)
        pvals = [p[...] for p in pin]
        totals = []
        for sub in range(nsub):
            at = slice(sub * chunk, (sub + 1) * chunk)
            outs = fn(*[r[at, :] for r in rin], *pvals)
            for ref, o in zip(orow, outs[:nor]):
                if isinstance(o, (tuple, list)):
                    col = 0
                    for piece in o:
                        ref[at, col:col + piece.shape[1]] = piece.astype(ref.dtype)
                        col += piece.shape[1]
                else:
                    ref[at, :] = o.astype(ref.dtype)
            accs = list(outs[nor:])
            totals = accs if sub == 0 else [t + a for t, a in zip(totals, accs)]

        def accumulate(ref, o):
            @pl.when(step == 0)
            def _():
                ref[...] = o

            @pl.when(step > 0)
            def _():
                ref[...] += o

        for ref, o in zip(oacc, totals):
            accumulate(ref, o)

    def colspec(width, cb):
        return pl.BlockSpec((tile, width), lambda i: (i, cb))

    hbm = pl.BlockSpec(memory_space=pltpu.HBM)
    outs = pl.pallas_call(
        body,
        grid=(ntile,),
        in_specs=[colspec(w, cb) for (_, w, cb) in rows]
        + [pl.BlockSpec(p.shape, lambda i: (0, 0), pipeline_mode=pl.Buffered(1)) for p in pars] + [hbm] * nex,
        out_specs=[colspec(w, 0) for (w, _) in out_rows]
        + [pl.BlockSpec(s, lambda i: (0, 0)) for s in out_accs] + [hbm] * nex,
        out_shape=[jax.ShapeDtypeStruct((R, w), dt) for (w, dt) in out_rows]
        + [jax.ShapeDtypeStruct(s, F32) for s in out_accs] + _exchange_out_shapes(hosted, scatter),
        scratch_shapes=_exchange_scratch(nex) if nex else [],
        compiler_params=_cparams(("arbitrary",), 56),
        name=name,
    )(*[r[0] for r in rows], *pars, *hosted)
    return (outs[:nor + noa], outs[nor + noa:]) if nex else outs


def _rms_fn(x, g):
    return x * lax.rsqrt(jnp.mean(x * x, axis=-1, keepdims=True) + RMS_EPS) * g


FUSED_TILE = 512
FUSED_CHUNKS = 2


def _down_proj_loss(act, w_down, x1, tgt, g):
    d = x1.shape[1]

    def fn(av, xv, tv, wv, gv):
        x2 = xv + _bdot_raw(av, wv, NN)
        y, vjp = jax.vjp(_rms_fn, x2, gv)
        err = y - tv
        loss = 0.5 * jnp.sum(jnp.sum(err * err, axis=-1, keepdims=True), axis=0, keepdims=True) / d
        dx, dg = vjp(err / d)
        return dx, jnp.broadcast_to(loss, (1, LANES)), dg

    return _rowwise(fn, [act, x1, tgt], [w_down, g], [(d, F32)], [(1, LANES), g.shape],
                    tile=FUSED_TILE, nsub=FUSED_CHUNKS, name="mlp_down_final_norm_loss")


def _proj_bwd_norm_bwd(cts, weights_t, x, dres, g, name, hosted=(), scatter=False):
    n = len(cts)

    def fn(*vals):
        ctv, (xv, dresv), wv, gv = vals[:n], vals[n:n + 2], vals[n + 2:2 * n + 2], vals[-1]
        dh = _bdot_raw(ctv[0], wv[0], NT)
        for c, w in zip(ctv[1:], wv[1:]):
            dh = dh + _bdot_raw(c, w, NT)
        _, vjp = jax.vjp(_rms_fn, xv, gv)
        dx, dg = vjp(dh)
        return dx + dresv, dg

    return _rowwise(fn, [*cts, x, dres], [*weights_t, g], [(x.shape[1], F32)], [g.shape],
                    tile=FUSED_TILE, nsub=FUSED_CHUNKS, name=name, hosted=hosted, scatter=scatter)


def _head_sum_matrix():
    i = lax.broadcasted_iota(jnp.int32, (RWKV_WIDTH, RWKV_WIDTH), 0) // HEAD_DIM
    j = lax.broadcasted_iota(jnp.int32, (RWKV_WIDTH, RWKV_WIDTH), 1) // HEAD_DIM
    return (i == j).astype(BF16)


def _head_sums_raw(x, esum):
    hi = x.astype(BF16)
    lo = (x - hi.astype(F32)).astype(BF16)
    return _dot(hi, esum) + _dot(lo, esum)


@jax.custom_vjp
def _head_sums(x, esum):
    return _head_sums_raw(x, esum)


_head_sums.defvjp(lambda x, esum: (_head_sums_raw(x, esum), esum),
                  lambda esum, ct: (_head_sums_raw(ct, esum), jnp.zeros_like(esum)))


def _prep_core(xr, xk, xv, xwa, xg, w0, w2p, a0, a2p, g2, k_k, k_a, esum):
    lw = -DECAY_SCALE * _sigmoid(w0 + _bdot_nn(jnp.tanh(xwa), w2p))
    a = _sigmoid(a0 + _bdot_nn(xwa, a2p))
    g = _bdot_nn(_sigmoid(xg), g2)
    kk0 = xk * k_k
    kk = kk0 * jnp.minimum(lax.rsqrt(_head_sums(kk0 * kk0, esum)), 1.0 / L2_EPS)
    k = xk * (1.0 + (a - 1.0) * k_a)
    return xr, lw, k, xv, kk, a, g


_SEGS = ((0, 512), (512, 1024), (1024, 1536), (1536, 1664), (1664, 1792))


PREP_TILE = 256
SUBLANES = 8


def _shifted_tokens(z_ref, zprev_ref, tile_index, seq):
    zc = z_ref[...]
    start = (tile_index * PREP_TILE) % seq == 0
    before = jnp.where(start, 0.0, zprev_ref[SUBLANES - 1:SUBLANES, :])
    rowid = lax.broadcasted_iota(jnp.int32, zc.shape, 0)
    return zc, jnp.where(rowid == 0, before, pltpu.roll(zc, 1, 0))


def _prep_specs(z, mu, pars, index):
    width = z.shape[1]
    per = PREP_TILE // SUBLANES
    return ([pl.BlockSpec((PREP_TILE, width), lambda i: (index(i), 0)),
             pl.BlockSpec((SUBLANES, width), lambda i: (jnp.maximum(index(i) * per - 1, 0), 0))],
            [pl.BlockSpec(p.shape, lambda i: (0, 0)) for p in (mu, *pars)])


def _norm_in_proj_prep(x, seq, g, w_attn, w_rw, mu, pars):
    rows, d = x.shape
    chunk = FUSED_TILE // FUSED_CHUNKS
    npar = len(pars)
    wa_width, wr_width = w_attn.shape[1], w_rw.shape[1]

    def body(x_ref, g_ref, wa_ref, wr_ref, mu_ref, *rest):
        par_refs = rest[:npar]
        h_ref, za_ref, zr_ref = rest[npar:npar + 3]
        out_refs, carry = rest[npar + 3:-1], rest[-1]
        step = pl.program_id(0)

        @pl.when(step == 0)
        def _():
            carry[...] = jnp.zeros_like(carry)

        pv = [p[...] for p in par_refs]
        for sub in range(FUSED_CHUNKS):
            at = slice(sub * chunk, (sub + 1) * chunk)
            h = _rms_fn(x_ref[at, :], g_ref[...])
            h_ref[at, :] = h.astype(h_ref.dtype)
            za_ref[at, :] = _bdot_raw(h, wa_ref[...], NN)
            zc = _bdot_raw(h, wr_ref[...], NN)
            zr_ref[at, :] = zc
            start = (step * FUSED_TILE + sub * chunk) % seq == 0
            before = jnp.where(start, 0.0, carry[SUBLANES - 1:SUBLANES, :])
            rowid = lax.broadcasted_iota(jnp.int32, zc.shape, 0)
            zp = jnp.where(rowid == 0, before, pltpu.roll(zc, 1, 0))
            carry[...] = zc[chunk - SUBLANES:chunk, :]
            zs = zc + (zp - zc) * mu_ref[...]
            outs = _prep_core(*[zs[:, a:b] for a, b in _SEGS], *pv)
            for ref, o in zip(out_refs, outs):
                ref[at, :] = o

    tiled = lambda width: pl.BlockSpec((FUSED_TILE, width), lambda i: (i, 0))
    resident = lambda a: pl.BlockSpec(a.shape, lambda i: (0, 0), pipeline_mode=pl.Buffered(1))
    return pl.pallas_call(
        body,
        grid=(rows // FUSED_TILE,),
        in_specs=[tiled(d)] + [resident(a) for a in (g, w_attn, w_rw, mu, *pars)],
        out_specs=[tiled(d), tiled(wa_width), tiled(wr_width)] + [tiled(RWKV_WIDTH)] * 7,
        out_shape=[jax.ShapeDtypeStruct((rows, d), BF16), jax.ShapeDtypeStruct((rows, wa_width), F32),
                   jax.ShapeDtypeStruct((rows, wr_width), F32)] + [jax.ShapeDtypeStruct((rows, RWKV_WIDTH), F32)] * 7,
        scratch_shapes=[pltpu.VMEM((SUBLANES, wr_width), F32)],
        compiler_params=_cparams(("arbitrary",), 56),
        name="attn_norm_in_proj_rwkv_prep",
    )(x, g, w_attn, w_rw, mu, *pars)


def _prep_bwd(z, seq, cts, mu, pars):
    rows, width = z.shape
    ntile = rows // PREP_TILE
    npar, nct = len(pars), len(cts)
    acc_shapes = [(1, b - a) for a, b in _SEGS] + [p.shape for p in pars[:-1]]

    def body(z_ref, zprev_ref, *rest):
        ct_refs = rest[:nct]
        mu_ref = rest[nct]
        par_refs = rest[nct + 1:nct + 1 + npar]
        dz_ref = rest[nct + 1 + npar]
        acc_refs = rest[nct + 2 + npar:-1]
        carry = rest[-1]
        step = pl.program_id(0)
        tile_index = ntile - 1 - step

        @pl.when(step == 0)
        def _():
            carry[...] = jnp.zeros_like(carry)

        zc, zp = _shifted_tokens(z_ref, zprev_ref, tile_index, seq)
        mu_v = mu_ref[...]
        diff = zp - zc
        zs = zc + diff * mu_v
        dra, drb, dlw, dka, dkb, dva, dvb, dkk, da, dg = [c[...] for c in ct_refs]
        pv = [p[...] for p in par_refs]
        _, vjp = jax.vjp(lambda *args: _prep_core(*args, pv[-1]), *[zs[:, a:b] for a, b in _SEGS], *pv[:-1])
        grads = vjp((dra + drb, dlw, dka + dkb, dva + dvb, dkk, da, dg))
        dsegs, dpars = grads[:5], grads[5:]
        last_of_sequence = ((tile_index + 1) * PREP_TILE) % seq == 0
        accs = []
        for ds, (a, b) in zip(dsegs, _SEGS):
            mu_s = mu_v[:, a:b]
            dzp = ds * mu_s
            after = jnp.where(last_of_sequence, 0.0, carry[0:1, a:b])
            rowid = lax.broadcasted_iota(jnp.int32, dzp.shape, 0)
            from_next = jnp.where(rowid == PREP_TILE - 1, after, pltpu.roll(dzp, PREP_TILE - 1, 0))
            dz_ref[:, a:b] = (ds * (1.0 - mu_s) + from_next).astype(dz_ref.dtype)
            carry[:, a:b] = dzp[0:SUBLANES, :]
            accs.append(jnp.sum(ds * diff[:, a:b], axis=0, keepdims=True))
        accs.extend(dpars)

        def accumulate(ref, o):
            @pl.when(step == 0)
            def _():
                ref[...] = o

            @pl.when(step > 0)
            def _():
                ref[...] += o

        for ref, o in zip(acc_refs, accs):
            accumulate(ref, o)

    rev = lambda i: ntile - 1 - i
    zspecs, pspecs = _prep_specs(z, mu, pars, rev)
    return pl.pallas_call(
        body,
        grid=(ntile,),
        in_specs=zspecs + [pl.BlockSpec((PREP_TILE, RWKV_WIDTH), lambda i: (rev(i), 0))] * nct + pspecs,
        out_specs=[pl.BlockSpec((PREP_TILE, width), lambda i: (rev(i), 0))]
        + [pl.BlockSpec(s, lambda i: (0, 0)) for s in acc_shapes],
        out_shape=[jax.ShapeDtypeStruct((rows, width), BF16)] + [jax.ShapeDtypeStruct(s, F32) for s in acc_shapes],
        scratch_shapes=[pltpu.VMEM((SUBLANES, width), F32)],
        compiler_params=_cparams(("arbitrary",), 56),
        name="rwkv_prep_bwd",
    )(z, z, *cts, mu, *pars)


def _post_fn(y, r, k, v, g, ln_w, ln_b, r_k, esum):
    mean = _head_sums(y, esum) * (1.0 / HEAD_DIM)
    yc = y - mean
    var = _head_sums(yc * yc, esum) * (1.0 / HEAD_DIM)
    yn = yc * lax.rsqrt(var + GN_EPS) * ln_w + ln_b
    bonus = _head_sums(r * k * r_k, esum) * v
    return (yn + bonus) * g


def _post_out_proj_norm(y, r, k, v, g, attn_out, x, pars, w_attn_rows, w_rwkv_rows, g_norm):
    npar = len(pars)

    def fn(yv, rv, kv, vv, gv, av, xv, *rest):
        wa, wr, gn = rest[npar:]
        rw = _post_fn(yv, rv, kv, vv, gv, *rest[:npar])
        x1 = xv + _bdot_raw(av, wa, NN) + _bdot_raw(rw, wr, NN)
        return rw, x1, _rms_fn(x1, gn)

    d = x.shape[1]
    return _rowwise(fn, [y, r, k, v, g, attn_out, x], [*pars, w_attn_rows, w_rwkv_rows, g_norm],
                    [(RWKV_WIDTH, BF16), (d, F32), (d, BF16)], [],
                    tile=FUSED_TILE, nsub=FUSED_CHUNKS, name="rwkv_post_out_proj_mlp_norm")


def _out_proj_bwd_post_bwd(dx1, y, r, k, v, g, pars, w_attn_rows, w_rwkv_rows):
    npar = len(pars)

    def fn(dxv, yv, rv, kv, vv, gv, *rest):
        wa, wr = rest[npar:]
        esum = rest[npar - 1]
        d_attn = _bdot_raw(dxv, wa, NT)
        d_rw = _bdot_raw(dxv, wr, NT)
        _, vjp = jax.vjp(lambda *a: _post_fn(*a, esum), yv, rv, kv, vv, gv, *rest[:npar - 1])
        return (d_attn, *vjp(d_rw))

    return _rowwise(fn, [dx1, y, r, k, v, g], [*pars, w_attn_rows, w_rwkv_rows], [(RWKV_WIDTH, F32)] * 6,
                    [p.shape for p in pars[:-1]], tile=FUSED_TILE, nsub=FUSED_CHUNKS, name="out_proj_bwd_rwkv_post_bwd")


def _tri_inverses(ms):
    n = ms[0].shape[0]
    row = lax.broadcasted_iota(jnp.int32, (n, n), 0)
    col = lax.broadcasted_iota(jnp.int32, (n, n), 1)
    eye = jnp.where(row == col, 1.0, 0.0)
    t_inv = [eye + m for m in ms]
    power = [_bdot_raw(m, m, NN) for m in ms]
    steps = int(math.log2(n // 2)) - 1
    for step in range(steps):
        if step < steps - 1:
            both = [_bdot_raw(jnp.concatenate([p, t], axis=0), p, NN) for p, t in zip(power, t_inv)]
            power = [b[:n] for b in both]
            t_inv = [t + b[n:] for t, b in zip(t_inv, both)]
        else:
            t_inv = [t + _bdot_raw(t, p, NN) for t, p in zip(t_inv, power)]
    return t_inv


@jax.custom_vjp
def _tri_solve(ms, xs):
    return tuple(_bdot_raw(t, x, NN) for t, x in zip(_tri_inverses(ms), xs))


def _tri_solve_fwd(ms, xs):
    t_inv = _tri_inverses(ms)
    us = tuple(_bdot_raw(t, x, NN) for t, x in zip(t_inv, xs))
    return us, (tuple(t_inv), us)


def _tri_solve_bwd(res, dus):
    t_inv, us = res
    dxs = tuple(_bdot_raw(t, du, TN) for t, du in zip(t_inv, dus))
    dms = tuple(_bdot_raw(dx, u, NT) for dx, u in zip(dxs, us))
    return dms, dxs


_tri_solve.defvjp(_tri_solve_fwd, _tri_solve_bwd)


def _chunk_fn(ss, rs, lws, ks, vs, kks, als):
    c = rs[0].shape[0]
    n = 2 * c
    row = lax.broadcasted_iota(jnp.int32, (n, n), 0)
    col = lax.broadcasted_iota(jnp.int32, (n, n), 1)
    incl = (row % c) >= (col % c)
    strict = (row % c) > (col % c)
    lane = lax.broadcasted_iota(jnp.int32, (1, LANES), 1)
    m_lo = jnp.where(lane < HEAD_DIM, 1.0, 0.0)
    m_hi = 1.0 - m_lo

    def stack(a):
        return jnp.concatenate([a * m_lo, a * m_hi], axis=0)

    cums = [_cumsum_rows(lw) for lw in lws]
    totals = [jnp.sum(lw, axis=0, keepdims=True) for lw in lws]
    bs = [kk * al for kk, al in zip(kks, als)]
    grows = [jnp.exp(-cum) for cum in cums]
    a_s = [stack(-kk * jnp.exp(cum - lw)) for kk, cum, lw in zip(kks, cums, lws)]
    b_s = [stack(b * g) for b, g in zip(bs, grows)]
    k_s = [stack(k * g) for k, g in zip(ks, grows)]
    r_s = [stack(r * jnp.exp(cum)) for r, cum in zip(rs, cums)]
    v_s = [stack(v) for v in vs]
    pair = lambda p, q: jnp.concatenate([p, q], axis=0)
    ar_s = [pair(a, r) for a, r in zip(a_s, r_s)]
    blocks = [_quarters(_bdot(ar, pair(b, k), NT)) for ar, b, k in zip(ar_s, b_s, k_s)]
    m_ab = [jnp.where(strict, q[0], 0.0) for q in blocks]
    m_ak = [jnp.where(strict, q[1], 0.0) for q in blocks]
    m_rb = [jnp.where(incl, q[2], 0.0) for q in blocks]
    m_rk = [jnp.where(incl, q[3], 0.0) for q in blocks]
    from_state = [_halves(_bdot(ar, s, NT)) for ar, s in zip(ar_s, ss)]
    from_v = [_halves(_bdot(pair(mk, mr), v)) for mk, mr, v in zip(m_ak, m_rk, v_s)]
    u = _tri_solve(tuple(m_ab), tuple(fs[0] + fv[0] for fs, fv in zip(from_state, from_v)))
    y = [_fold_rows(fs[1] + _bdot(mb, uu) + fv[1]) for fs, mb, uu, fv in zip(from_state, m_rb, u, from_v)]
    tails = [jnp.exp(tot - cum) for tot, cum in zip(totals, cums)]
    s_new = [s * jnp.exp(tot) + _bdot(pair(uu, v), pair(stack(b * tl), stack(k * tl)), TN)
             for s, tot, uu, b, tl, v, k in zip(ss, totals, u, bs, tails, v_s, ks)]
    return tuple(y), tuple(s_new)


def _chains(bsz, npair):
    return [(b, p, slice(p * LANES, (p + 1) * LANES)) for b in range(bsz) for p in range(npair)]


def _hosted_exchange(first, last, ex_in, ex_out, sems, scatter):
    if not ex_in:
        return

    @pl.when(first)
    def _():
        _exchange_start(_exchange_copies(ex_in, ex_out, *sems, scatter, arrivals=False))

    @pl.when(last)
    def _():
        _exchange_wait(_exchange_copies(ex_in, ex_out, *sems, scatter, arrivals=True))


def _rwkv_fwd(r, lw, k, v, kk, al, hosted=(), scatter=False):
    bsz, t, w = r.shape
    npair, nchunk = w // LANES, t // CHUNK
    chains = _chains(bsz, npair)
    nex = len(hosted)

    def body(*refs):
        r_ref, lw_ref, k_ref, v_ref, kk_ref, al_ref = refs[:6]
        ex_in = refs[6:6 + nex]
        y_ref, sall_ref = refs[6 + nex:8 + nex]
        ex_out = refs[8 + nex:8 + 2 * nex]
        s_scr = refs[8 + 2 * nex]
        step = pl.program_id(0)

        @pl.when(step == 0)
        def _():
            s_scr[...] = jnp.zeros_like(s_scr)

        _hosted_exchange(step == 0, step == nchunk - 1, ex_in, ex_out, refs[9 + 2 * nex:], scatter)
        ss = tuple(s_scr[i] for i in range(len(chains)))
        for i, s in enumerate(ss):
            sall_ref[0, i] = s
        ys, s_new = _chunk_fn(ss, *[tuple(ref[b, :, cols] for b, _, cols in chains)
                                    for ref in (r_ref, lw_ref, k_ref, v_ref, kk_ref, al_ref)])
        for i, (b, _, cols) in enumerate(chains):
            y_ref[b, :, cols] = ys[i]
            s_scr[i] = s_new[i]

    spec = pl.BlockSpec((bsz, CHUNK, w), lambda c: (0, c, 0))
    hbm = pl.BlockSpec(memory_space=pltpu.HBM)
    outs = pl.pallas_call(
        body,
        grid=(nchunk,),
        in_specs=[spec] * 6 + [hbm] * nex,
        out_specs=[spec, pl.BlockSpec((1, len(chains), LANES, LANES), lambda c: (c, 0, 0, 0))] + [hbm] * nex,
        out_shape=[jax.ShapeDtypeStruct((bsz, t, w), F32),
                   jax.ShapeDtypeStruct((nchunk, len(chains), LANES, LANES), F32)]
        + _exchange_out_shapes(hosted, scatter),
        scratch_shapes=[pltpu.VMEM((len(chains), LANES, LANES), F32)] + (_exchange_scratch(nex) if nex else []),
        compiler_params=_cparams(("arbitrary",), 48),
        name="rwkv_chunk_fwd",
    )(r, lw, k, v, kk, al, *hosted)
    return outs[0], outs[1], outs[2:]


def _rwkv_bwd(r, lw, k, v, kk, al, s_all, dy, hosted=(), scatter=False):
    bsz, t, w = r.shape
    npair, nchunk = w // LANES, t // CHUNK
    chains = _chains(bsz, npair)
    nex = len(hosted)

    def body(*refs):
        r_ref, lw_ref, k_ref, v_ref, kk_ref, al_ref, s_ref, dy_ref = refs[:8]
        ex_in = refs[8:8 + nex]
        out_refs = refs[8 + nex:14 + nex]
        ex_out = refs[14 + nex:14 + 2 * nex]
        ds_scr = refs[14 + 2 * nex]
        step = pl.program_id(0)

        @pl.when(step == 0)
        def _():
            ds_scr[...] = jnp.zeros_like(ds_scr)

        _hosted_exchange(step == 0, step == nchunk - 1, ex_in, ex_out, refs[15 + 2 * nex:], scatter)
        ss = tuple(s_ref[0, i] for i in range(len(chains)))
        _, vjp = jax.vjp(_chunk_fn, ss, *[tuple(ref[b, :, cols] for b, _, cols in chains)
                                          for ref in (r_ref, lw_ref, k_ref, v_ref, kk_ref, al_ref)])
        grads = vjp((tuple(dy_ref[b, :, cols] for b, _, cols in chains),
                     tuple(ds_scr[i] for i in range(len(chains)))))
        for i, (b, _, cols) in enumerate(chains):
            ds_scr[i] = grads[0][i]
            for ref, gval in zip(out_refs, grads[1:]):
                ref[b, :, cols] = gval[i]

    spec = pl.BlockSpec((bsz, CHUNK, w), lambda c: (0, nchunk - 1 - c, 0))
    sspec = pl.BlockSpec((1, len(chains), LANES, LANES), lambda c: (nchunk - 1 - c, 0, 0, 0))
    hbm = pl.BlockSpec(memory_space=pltpu.HBM)
    outs = pl.pallas_call(
        body,
        grid=(nchunk,),
        in_specs=[spec] * 6 + [sspec, spec] + [hbm] * nex,
        out_specs=[spec] * 6 + [hbm] * nex,
        out_shape=[jax.ShapeDtypeStruct((bsz, t, w), F32)] * 6 + _exchange_out_shapes(hosted, scatter),
        scratch_shapes=[pltpu.VMEM((len(chains), LANES, LANES), F32)] + (_exchange_scratch(nex) if nex else []),
        compiler_params=_cparams(("arbitrary",), 48),
        name="rwkv_chunk_bwd",
    )(r, lw, k, v, kk, al, s_all, dy, *hosted)
    return outs[:6], outs[6:]


def _alibi_slope(head):
    return 2.0 ** (-8.0 * (head + 1) / N_ATTN_HEADS)


def _attn_block(qs, kp, kc, vp, vc, sinks, first):
    row = lax.broadcasted_iota(jnp.int32, (BLOCK, 2 * BLOCK), 0)
    col = lax.broadcasted_iota(jnp.int32, (BLOCK, 2 * BLOCK), 1)
    lane = lax.broadcasted_iota(jnp.int32, (1, LANES), 1)
    halves = [jnp.where((lane // HEAD_DIM) == half, 1.0, 0.0) for half in range(2)]
    srow = lax.broadcasted_iota(jnp.int32, (LANES, LANES), 0)
    scol = lax.broadcasted_iota(jnp.int32, (LANES, LANES), 1)
    swap = jnp.where((srow + HEAD_DIM) % LANES == scol, 1.0, 0.0)
    dist = row - col + BLOCK
    valid = jnp.logical_and(jnp.logical_and(dist >= 0, dist < BLOCK),
                            jnp.logical_or(col >= BLOCK, jnp.logical_not(first)))
    dist = dist.astype(F32)
    scale = HEAD_DIM ** -0.5
    stored = (jnp.concatenate([kp, kc], axis=0), jnp.concatenate([vp, vc], axis=0))
    swapped = tuple(_bdot_nn(t, swap) for t in stored)
    heads = [(pair, half) for pair in range(len(qs)) for half in range(2)]
    kv = [stored if half == pair // 2 else swapped for pair, half in heads]
    slopes = [_alibi_slope(2 * pair + half) for pair, half in heads]
    qa = [qs[pair] * halves[half] for pair, half in heads]
    s = [jnp.where(valid, _bdot_nt(q, t[0]) * scale - sl * dist, NEG_INF) for q, t, sl in zip(qa, kv, slopes)]
    mx = [lax.stop_gradient(jnp.maximum(jnp.max(a, axis=-1, keepdims=True), sk)) for a, sk in zip(s, sinks)]
    e = [jnp.exp(a - m) for a, m in zip(s, mx)]
    es = [jnp.exp(sk - m) for sk, m in zip(sinks, mx)]
    inv = [1.0 / (jnp.sum(a, axis=-1, keepdims=True) + b) for a, b in zip(e, es)]
    o = [_bdot_nn(a * i, t[1]) for a, i, t in zip(e, inv, kv)]
    outs = tuple(o[2 * pair] * halves[0] + o[2 * pair + 1] * halves[1] for pair in range(len(qs)))
    return outs, [lax.stop_gradient(b * i) for b, i in zip(es, inv)]


def _sink_values(sink_ref):
    return [jnp.max(sink_ref[h:h + 1, :], axis=-1, keepdims=True) for h in range(N_ATTN_HEADS)]


def _attn_fwd(z, sink_rows):
    bsz, t, _ = z.shape
    nb = t // BLOCK
    npair = ATTN_WIDTH // LANES

    def body(q_ref, kp_ref, kc_ref, vp_ref, vc_ref, sink_ref, o_ref):
        first = pl.program_id(1) == 0
        qs = tuple(q_ref[0, :, pair * LANES:(pair + 1) * LANES] for pair in range(npair))
        outs, _ = _attn_block(qs, kp_ref[0], kc_ref[0], vp_ref[0], vc_ref[0], _sink_values(sink_ref), first)
        for pair in range(npair):
            o_ref[0, :, pair * LANES:(pair + 1) * LANES] = outs[pair].astype(o_ref.dtype)

    kcol, vcol = ATTN_WIDTH // KV_WIDTH, ATTN_WIDTH // KV_WIDTH + 1
    return pl.pallas_call(
        body,
        grid=(bsz, nb),
        in_specs=[pl.BlockSpec((1, BLOCK, ATTN_WIDTH), lambda b, n: (b, n, 0)),
                  pl.BlockSpec((1, BLOCK, KV_WIDTH), lambda b, n: (b, jnp.maximum(n - 1, 0), kcol)),
                  pl.BlockSpec((1, BLOCK, KV_WIDTH), lambda b, n: (b, n, kcol)),
                  pl.BlockSpec((1, BLOCK, KV_WIDTH), lambda b, n: (b, jnp.maximum(n - 1, 0), vcol)),
                  pl.BlockSpec((1, BLOCK, KV_WIDTH), lambda b, n: (b, n, vcol)),
                  pl.BlockSpec(sink_rows.shape, lambda b, n: (0, 0))],
        out_specs=pl.BlockSpec((1, BLOCK, ATTN_WIDTH), lambda b, n: (b, n, 0)),
        out_shape=jax.ShapeDtypeStruct((bsz, t, ATTN_WIDTH), BF16),
        compiler_params=_cparams(("parallel", "arbitrary"), 48),
        name="swa_fwd",
    )(z, z, z, z, z, sink_rows)


def _attn_bwd(z, dout, sink_rows):
    bsz, t, _ = z.shape
    nb = t // BLOCK
    npair = ATTN_WIDTH // LANES

    def body(q_ref, kp_ref, kc_ref, vp_ref, vc_ref, do_ref, sink_ref, dz_ref, dsink_ref, carry):
        step = pl.program_id(1)
        n = nb - 1 - step
        first = n == 0

        @pl.when(step == 0)
        def _():
            carry[...] = jnp.zeros_like(carry)

        @pl.when(jnp.logical_and(step == 0, pl.program_id(0) == 0))
        def _():
            dsink_ref[...] = jnp.zeros_like(dsink_ref)

        lane = lax.broadcasted_iota(jnp.int32, (1, LANES), 1)
        qs = tuple(q_ref[0, :, pair * LANES:(pair + 1) * LANES] for pair in range(npair))
        dos = tuple(do_ref[0, :, pair * LANES:(pair + 1) * LANES] for pair in range(npair))
        fn = functools.partial(_attn_block, sinks=_sink_values(sink_ref), first=first)
        outs, vjp, psinks = jax.vjp(fn, qs, kp_ref[0], kc_ref[0], vp_ref[0], vc_ref[0], has_aux=True)
        dqs, dkp, dkc, dvp, dvc = vjp(dos)
        for pair in range(npair):
            dz_ref[0, :, pair * LANES:(pair + 1) * LANES] = dqs[pair].astype(dz_ref.dtype)
            for half in range(2):
                m = jnp.where((lane // HEAD_DIM) == half, 1.0, 0.0)
                delta = jnp.sum(dos[pair] * outs[pair] * m, axis=-1, keepdims=True)
                head = 2 * pair + half
                ds = -jnp.sum(psinks[head] * delta, axis=0, keepdims=True)
                dsink_ref[head:head + 1, :] += jnp.broadcast_to(ds, (1, LANES))
        dz_ref[0, :, ATTN_WIDTH:ATTN_WIDTH + KV_WIDTH] = (dkc + carry[0]).astype(dz_ref.dtype)
        dz_ref[0, :, ATTN_WIDTH + KV_WIDTH:QKV_WIDTH] = (dvc + carry[1]).astype(dz_ref.dtype)
        carry[0] = dkp
        carry[1] = dvp

    kcol, vcol = ATTN_WIDTH // KV_WIDTH, ATTN_WIDTH // KV_WIDTH + 1
    rev = lambda n: nb - 1 - n
    return pl.pallas_call(
        body,
        grid=(bsz, nb),
        in_specs=[pl.BlockSpec((1, BLOCK, ATTN_WIDTH), lambda b, n: (b, rev(n), 0)),
                  pl.BlockSpec((1, BLOCK, KV_WIDTH), lambda b, n: (b, jnp.maximum(rev(n) - 1, 0), kcol)),
                  pl.BlockSpec((1, BLOCK, KV_WIDTH), lambda b, n: (b, rev(n), kcol)),
                  pl.BlockSpec((1, BLOCK, KV_WIDTH), lambda b, n: (b, jnp.maximum(rev(n) - 1, 0), vcol)),
                  pl.BlockSpec((1, BLOCK, KV_WIDTH), lambda b, n: (b, rev(n), vcol)),
                  pl.BlockSpec((1, BLOCK, ATTN_WIDTH), lambda b, n: (b, rev(n), 0)),
                  pl.BlockSpec(sink_rows.shape, lambda b, n: (0, 0))],
        out_specs=[pl.BlockSpec((1, BLOCK, QKV_WIDTH), lambda b, n: (b, rev(n), 0)),
                   pl.BlockSpec((N_ATTN_HEADS, LANES), lambda b, n: (0, 0))],
        out_shape=[jax.ShapeDtypeStruct((bsz, t, QKV_WIDTH), BF16),
                   jax.ShapeDtypeStruct((N_ATTN_HEADS, LANES), F32)],
        scratch_shapes=[pltpu.VMEM((2, BLOCK, KV_WIDTH), F32)],
        compiler_params=_cparams(("arbitrary", "arbitrary"), 48),
        name="swa_bwd",
    )(z, z, z, z, z, dout, sink_rows)


def _exchange_out_shapes(arrays, scatter):
    return [jax.ShapeDtypeStruct((N_DEV,) + (a.shape[1:] if scatter else a.shape), a.dtype) for a in arrays]


def _exchange_scratch(n):
    return [pltpu.SemaphoreType.DMA((n, N_DEV - 1)), pltpu.SemaphoreType.DMA((n, N_DEV - 1)),
            pltpu.SemaphoreType.DMA((n,))]


def _exchange_copies(ins, outs, send_sems, recv_sems, local_sems, scatter, arrivals=True):
    x, y, c = lax.axis_index("x"), lax.axis_index("y"), lax.axis_index("c")
    me = 4 * x + 2 * y + c
    copies = []
    for i in range(len(ins)):
        own = pltpu.make_async_copy(ins[i].at[me] if scatter else ins[i], outs[i].at[me], local_sems.at[i])
        copies.append((own, None, True))
        for d in range(1, N_DEV):
            px = 1 - x if d & 4 else x
            py = 1 - y if d & 2 else y
            pc = 1 - c if d & 1 else c
            peer = 4 * px + 2 * py + pc
            src = ins[i].at[peer] if scatter else ins[i]
            send = pltpu.make_async_remote_copy(src, outs[i].at[me], send_sems.at[i, d - 1], recv_sems.at[i, d - 1],
                                                device_id=(px, py, pc), device_id_type=MESH)
            recv = pltpu.make_async_remote_copy(src, outs[i].at[peer], send_sems.at[i, d - 1], recv_sems.at[i, d - 1],
                                                device_id=(px, py, pc), device_id_type=MESH) if arrivals else None
            copies.append((send, recv, False))
    return copies


def _exchange_start(copies):
    for send, _, _ in copies:
        send.start()


def _exchange_wait(copies):
    for send, recv, local in copies:
        if local:
            send.wait()
        else:
            send.wait_send()
            recv.wait_recv()


def _gather_two_level(arrays, name):
    n = len(arrays)

    def body(*refs):
        ins, outs = refs[:n], refs[n:2 * n]
        send_sems, recv_sems, local_sems = refs[2 * n:]
        x, y, c = lax.axis_index("x"), lax.axis_index("y"), lax.axis_index("c")
        index = lambda px, py, pc: 4 * px + 2 * py + pc
        sibling = (x, y, 1 - c)
        chips = [(1 - x, y), (x, 1 - y), (1 - x, 1 - y)]

        def copy(i, k, block, to, src=None):
            slot = outs[i].at[index(*block)]
            return pltpu.make_async_remote_copy(slot if src is None else src, slot, send_sems.at[i, k],
                                                recv_sems.at[i, k], device_id=to, device_id_type=MESH)

        local, sends = [], []
        for i in range(n):
            own = pltpu.make_async_copy(ins[i], outs[i].at[index(x, y, c)], local_sems.at[i])
            own.start()
            local.append(own)
            first = [copy(i, 0, (x, y, c), sibling, src=ins[i])]
            first += [copy(i, 1 + j, (x, y, c), (*chip, c), src=ins[i]) for j, chip in enumerate(chips)]
            for cp in first:
                cp.start()
            sends += first
        for i in range(n):
            for j, chip in enumerate(chips):
                copy(i, 1 + j, (*chip, c), (x, y, c)).wait_recv()
                onward = copy(i, 4 + j, (*chip, c), sibling)
                onward.start()
                sends.append(onward)
        for i in range(n):
            copy(i, 0, sibling, (x, y, c)).wait_recv()
            for j, chip in enumerate(chips):
                copy(i, 4 + j, (*chip, 1 - c), (x, y, c)).wait_recv()
        for cp in sends:
            cp.wait_send()
        for cp in local:
            cp.wait()

    hbm = pl.BlockSpec(memory_space=pltpu.HBM)
    return pl.pallas_call(
        body,
        in_specs=[hbm] * n,
        out_specs=[hbm] * n,
        out_shape=_exchange_out_shapes(arrays, False),
        scratch_shapes=_exchange_scratch(n),
        name=name,
    )(*arrays)


def _exchange(arrays, *, scatter, name):
    n = len(arrays)

    def body(*refs):
        copies = _exchange_copies(refs[:n], refs[n:2 * n], *refs[2 * n:], scatter)
        _exchange_start(copies)
        _exchange_wait(copies)

    hbm = pl.BlockSpec(memory_space=pltpu.HBM)
    return pl.pallas_call(
        body,
        in_specs=[hbm] * n,
        out_specs=[hbm] * n,
        out_shape=_exchange_out_shapes(arrays, scatter),
        scratch_shapes=_exchange_scratch(n),
        name=name,
    )(*arrays)


def _adamw(parts, w, m, v, name):
    rows, cols = w.shape
    tr = _pick(rows, (256, 128, 64, 8))
    c1 = 1.0 / (1.0 - ADAM_B1 ** ADAM_STEP)
    c2 = 1.0 / (1.0 - ADAM_B2 ** ADAM_STEP)

    def body(p_ref, w_ref, m_ref, v_ref, g_ref, d_ref, mo_ref, vo_ref):
        g = p_ref[0].astype(F32)
        for s in range(1, N_DEV):
            g = g + p_ref[s].astype(F32)
        mn = ADAM_B1 * m_ref[...] + (1.0 - ADAM_B1) * g
        vn = ADAM_B2 * v_ref[...] + (1.0 - ADAM_B2) * (g * g)
        g_ref[...] = g
        mo_ref[...] = mn
        vo_ref[...] = vn
        d_ref[...] = -ADAM_LR * ((mn * c1) / (jnp.sqrt(vn * c2) + ADAM_EPS) + ADAM_WD * w_ref[...])

    spec = pl.BlockSpec((tr, cols), lambda i: (i, 0))
    return pl.pallas_call(
        body,
        grid=(rows // tr,),
        in_specs=[pl.BlockSpec((N_DEV, tr, cols), lambda i: (0, i, 0)), spec, spec, spec],
        out_specs=[spec] * 4,
        out_shape=[jax.ShapeDtypeStruct((rows, cols), F32)] * 4,
        compiler_params=_cparams(("parallel",), 48),
        name=name,
    )(parts, w, m, v)


_VECTOR_PARAMS = ("attn_norm_g", "attn_sinks", "rwkv_mu", "w0", "a0", "k_k", "k_a", "r_k", "ln_x_w", "ln_x_b",
                  "mlp_norm_g", "final_norm_g")
_WEIGHT_NAMES = ("attn_norm_g", "w_in", "attn_sinks", "rwkv_mu", "w0", "w2", "a0", "a2", "g2", "k_k", "k_a", "r_k",
                 "ln_x_w", "ln_x_b", "w_out", "mlp_norm_g", "w_up", "w_down", "final_norm_g")


def _pack_vectors(vals):
    pieces = []
    for name in _VECTOR_PARAMS:
        flat = vals[name].reshape(1, -1)
        pad = (-flat.shape[1]) % LANES
        pieces.append(jnp.pad(flat, ((0, 0), (0, pad))) if pad else flat)
    return jnp.concatenate(pieces, axis=1)


def _unpack_vectors(packed, like):
    out, col = {}, 0
    for name in _VECTOR_PARAMS:
        size = like[name].size
        out[name] = packed[0, col:col + size].reshape(like[name].shape)
        col += size + (-size) % LANES
    return out


def kernel(x, attn_norm_g, w_in, attn_sinks, rwkv_mu, w0, w2, a0, a2, g2, k_k, k_a, r_k, ln_x_w, ln_x_b, w_out, mlp_norm_g, w_up, w_down, final_norm_g, loss_target, m_attn_norm_g, m_w_in, m_attn_sinks, m_rwkv_mu, m_w0, m_w2, m_a0, m_a2, m_g2, m_k_k, m_k_a, m_r_k, m_ln_x_w, m_ln_x_b, m_w_out, m_mlp_norm_g, m_w_up, m_w_down, m_final_norm_g, v_attn_norm_g, v_w_in, v_attn_sinks, v_rwkv_mu, v_w0, v_w2, v_a0, v_a2, v_g2, v_k_k, v_k_a, v_r_k, v_ln_x_w, v_ln_x_b, v_w_out, v_mlp_norm_g, v_w_up, v_w_down, v_final_norm_g):
    weights = dict(attn_norm_g=attn_norm_g, w_in=w_in, attn_sinks=attn_sinks, rwkv_mu=rwkv_mu, w0=w0, w2=w2, a0=a0,
                   a2=a2, g2=g2, k_k=k_k, k_a=k_a, r_k=r_k, ln_x_w=ln_x_w, ln_x_b=ln_x_b, w_out=w_out,
                   mlp_norm_g=mlp_norm_g, w_up=w_up, w_down=w_down, final_norm_g=final_norm_g)
    mom1 = dict(attn_norm_g=m_attn_norm_g, w_in=m_w_in, attn_sinks=m_attn_sinks, rwkv_mu=m_rwkv_mu, w0=m_w0, w2=m_w2,
                a0=m_a0, a2=m_a2, g2=m_g2, k_k=m_k_k, k_a=m_k_a, r_k=m_r_k, ln_x_w=m_ln_x_w, ln_x_b=m_ln_x_b,
                w_out=m_w_out, mlp_norm_g=m_mlp_norm_g, w_up=m_w_up, w_down=m_w_down, final_norm_g=m_final_norm_g)
    mom2 = dict(attn_norm_g=v_attn_norm_g, w_in=v_w_in, attn_sinks=v_attn_sinks, rwkv_mu=v_rwkv_mu, w0=v_w0, w2=v_w2,
                a0=v_a0, a2=v_a2, g2=v_g2, k_k=v_k_k, k_a=v_k_a, r_k=v_r_k, ln_x_w=v_ln_x_w, ln_x_b=v_ln_x_b,
                w_out=v_w_out, mlp_norm_g=v_mlp_norm_g, w_up=v_w_up, w_down=v_w_down, final_norm_g=v_final_norm_g)
    bsz, seq, d_model = x.shape
    rows = bsz * seq
    d_in = N_DEV * w_in.shape[2]
    d_ff = N_DEV * w_up.shape[2]

    gathered = _gather_two_level([w_in[0].astype(BF16), w2[0], a2[0], g2[0]], name="gather_in_weights")
    cols_first = lambda a: a.transpose(1, 0, 2).reshape(a.shape[1], -1)
    w_in_f = cols_first(gathered[0])
    w_attn, w_rw = w_in_f[:, :QKV_WIDTH], w_in_f[:, QKV_WIDTH:]
    w2_f, a2_f, g2_f = cols_first(gathered[1]), cols_first(gathered[2]), cols_first(gathered[3])
    lora = w2_f.shape[0]
    w2p = jnp.concatenate([w2_f, jnp.zeros_like(a2_f)], axis=0)
    a2p = jnp.concatenate([jnp.zeros_like(w2_f), a2_f], axis=0)

    esum = _head_sum_matrix()
    sink_rows = jnp.broadcast_to(attn_sinks.reshape(N_ATTN_HEADS, 1), (N_ATTN_HEADS, LANES))
    prep_pars = [w0, w2p, a0, a2p, g2_f, k_k, k_a, esum]
    post_pars = [ln_x_w, ln_x_b, r_k, esum]

    x2d = x.reshape(rows, d_model)
    h1, z_attn, z_rw, r, lw, k, v, kk, al, gate = _norm_in_proj_prep(x2d, seq, attn_norm_g, w_attn, w_rw, rwkv_mu,
                                                                       prep_pars)
    z_attn3 = z_attn.reshape(bsz, seq, QKV_WIDTH)
    attn_out = _attn_fwd(z_attn3, sink_rows)
    as3 = lambda a: a.reshape(bsz, seq, RWKV_WIDTH)
    y, s_all, late = _rwkv_fwd(as3(r), as3(lw), as3(k), as3(v), as3(kk), as3(al),
                               hosted=[w_out[0].astype(BF16), w_up[0].astype(BF16), w_down[0].astype(BF16)])
    w_out_f = late[0].reshape(-1, d_model)
    w_up_f = cols_first(late[1])
    w_down_f = late[2].reshape(-1, d_model)
    y2 = y.reshape(rows, RWKV_WIDTH)
    attn_out2d = attn_out.reshape(rows, ATTN_WIDTH)
    w_out_attn, w_out_rw = w_out_f[:ATTN_WIDTH], w_out_f[ATTN_WIDTH:]
    rw_out, x1, h2 = _post_out_proj_norm(y2, r, k, v, gate, attn_out2d, x2d, post_pars, w_out_attn, w_out_rw,
                                         mlp_norm_g)

    def relu_sq(acc):
        pos = jnp.maximum(acc, 0.0)
        return acc, pos * pos

    u, act = _matmul(h2, w_up_f, "nn", name="mlp_up", epilogue=relu_sq, out_dtypes=(BF16, BF16))
    dx2, loss_vec, g_final = _down_proj_loss(act, w_down_f, x1, loss_target.reshape(rows, d_model),
                                             final_norm_g.reshape(1, d_model))

    g_w_down = _matmul(act, dx2, "tn", name="grad_w_down", out_dtypes=(BF16,))
    du = _matmul(dx2, w_down_f, "nt", name="mlp_down_bwd", extras=(u,), out_dtypes=(BF16,),
                 epilogue=lambda acc, uv: (acc * (2.0 * jnp.maximum(uv.astype(F32), 0.0)),))
    g_w_up = _matmul(h2, du, "tn", name="grad_w_up", out_dtypes=(BF16,))
    dx1, g_mlp_norm = _proj_bwd_norm_bwd([du], [w_up_f], x1, dx2, mlp_norm_g, "mlp_up_bwd_norm_bwd")
    g_w_out = jnp.concatenate([_matmul(attn_out2d, dx1, "tn", name="grad_w_out_attn", out_dtypes=(BF16,)),
                               _matmul(rw_out, dx1, "tn", name="grad_w_out_rwkv", out_dtypes=(BF16,))], axis=0)
    d_attn_out, dy, dr_a, dk_a, dv_a, dgate, g_ln_w, g_ln_b, g_r_k = _out_proj_bwd_post_bwd(
        dx1, y2, r, k, v, gate, post_pars, w_out_attn, w_out_rw)
    by_cols = lambda a: a.reshape(a.shape[0], N_DEV, -1).transpose(1, 0, 2)
    (dr_b, dlw, dk_b, dv_b, dkk, dal), (p_w_out, p_w_up, p_w_down) = _rwkv_bwd(
        as3(r), as3(lw), as3(k), as3(v), as3(kk), as3(al), s_all, as3(dy),
        hosted=[g_w_out.reshape(N_DEV, -1, d_model), by_cols(g_w_up), g_w_down.reshape(N_DEV, -1, d_model)],
        scatter=True)
    flat = lambda a: a.reshape(rows, RWKV_WIDTH)
    (dz_rw, gmu_r, gmu_k, gmu_v, gmu_wa, gmu_g, g_w0, g_w2p, g_a0, g_a2p, g_g2, g_k_k, g_k_a) = _prep_bwd(
        z_rw, seq, [dr_a, flat(dr_b), flat(dlw), dk_a, flat(dk_b), dv_a, flat(dv_b), flat(dkk), flat(dal), dgate],
        rwkv_mu, prep_pars)
    dz_attn, g_sink_rows = _attn_bwd(z_attn3, d_attn_out.reshape(bsz, seq, ATTN_WIDTH), sink_rows)
    dz_attn = dz_attn.reshape(rows, QKV_WIDTH)
    g_w_in = jnp.concatenate([_matmul(h1, dz_attn, "tn", name="grad_w_in_attn", out_dtypes=(BF16,)),
                              _matmul(h1, dz_rw, "tn", name="grad_w_in_rwkv", out_dtypes=(BF16,))], axis=1)
    lora_grads = jnp.concatenate([g_w2p[:lora], g_a2p[lora:], g_g2], axis=0)
    (dx, g_attn_norm), (p_w_in, p_lora) = _proj_bwd_norm_bwd(
        [dz_attn, dz_rw], [w_attn, w_rw], x2d, dx1, attn_norm_g, "in_proj_bwd_norm_bwd",
        hosted=[by_cols(g_w_in), by_cols(lora_grads)], scatter=True)

    vec_grads = dict(attn_norm_g=g_attn_norm, attn_sinks=g_sink_rows[:, 0], rwkv_mu=jnp.concatenate(
        [gmu_r, gmu_k, gmu_v, gmu_wa, gmu_g], axis=1), w0=g_w0, a0=g_a0, k_k=g_k_k, k_a=g_k_a, r_k=g_r_k,
        ln_x_w=g_ln_w, ln_x_b=g_ln_b, mlp_norm_g=g_mlp_norm, final_norm_g=g_final)
    packed = _pack_vectors(vec_grads)
    nvec = packed.shape[1]
    everyone = _exchange([jnp.concatenate([packed, loss_vec], axis=1)], scatter=False, name="gather_vector_grads")[0]
    vec_parts = everyone[:, :, :nvec]
    loss = jnp.sum(everyone[:, 0, nvec])

    grads, delta, new_m, new_v = {}, {}, {}, {}

    def update(name, part, shape2d):
        res = _adamw(part, weights[name].reshape(shape2d), mom1[name].reshape(shape2d), mom2[name].reshape(shape2d),
                     "adamw_" + name)
        for store, val in zip((grads, delta, new_m, new_v), res):
            store[name] = val.reshape(weights[name].shape)

    update("w_in", p_w_in, w_in.shape[1:])
    update("w_out", p_w_out, w_out.shape[1:])
    update("w_up", p_w_up, w_up.shape[1:])
    update("w_down", p_w_down, w_down.shape[1:])
    stack = lambda d: jnp.concatenate([d["w2"][0], d["a2"][0], d["g2"][0]], axis=0)
    lora_res = _adamw(p_lora, stack(weights), stack(mom1), stack(mom2), "adamw_lora")
    for store, val in zip((grads, delta, new_m, new_v), lora_res):
        store["w2"], store["a2"], store["g2"] = val[None, :lora], val[None, lora:2 * lora], val[None, 2 * lora:]
    vec_res = _adamw(vec_parts, _pack_vectors(weights), _pack_vectors(mom1), _pack_vectors(mom2), "adamw_vectors")
    for store, val in zip((grads, delta, new_m, new_v), vec_res):
        store.update(_unpack_vectors(val, weights))

    return (loss, dx.reshape(x.shape), *[grads[n] for n in _WEIGHT_NAMES], *[delta[n] for n in _WEIGHT_NAMES],
            *[new_m[n] for n in _WEIGHT_NAMES], *[new_v[n] for n in _WEIGHT_NAMES])
```

```python
import functools
import math

import jax
import jax.numpy as jnp
from jax import lax
from jax.experimental import pallas as pl
from jax.experimental.pallas import tpu as pltpu

F32 = jnp.float32
BF16 = jnp.bfloat16

N_DEV = 8
HEAD_DIM = 64
LANES = 128
N_ATTN_HEADS = 8
ATTN_WIDTH = 512
KV_WIDTH = 128
QKV_WIDTH = ATTN_WIDTH + 2 * KV_WIDTH
RWKV_WIDTH = 512
LORA_WA = 128
GATE_LORA = 128
RWKV_SHIFT_WIDTH = 3 * RWKV_WIDTH + LORA_WA + GATE_LORA
BLOCK = 128
CHUNK = 64
RMS_EPS = 1e-6
GN_EPS = 64e-5
L2_EPS = 1e-12
NEG_INF = -1e30
DECAY_SCALE = math.exp(-0.5)
ADAM_LR, ADAM_B1, ADAM_B2, ADAM_EPS, ADAM_WD, ADAM_STEP = 0.001, 0.9, 0.999, 1e-08, 0.01, 10

NN = (((1,), (0,)), ((), ()))
NT = (((1,), (1,)), ((), ()))
TN = (((0,), (0,)), ((), ()))
MESH = pl.DeviceIdType.MESH


def _dot(a, b, dn=NN, precision=None):
    return lax.dot_general(a, b, dn, precision=precision, preferred_element_type=F32)


def _bdot_raw(a, b, dn):
    return lax.dot_general(a.astype(BF16), b.astype(BF16), dn, preferred_element_type=F32)


@functools.partial(jax.custom_vjp, nondiff_argnums=(2, 3))
def _bdot_c(a, b, ca, cb):
    return _bdot_raw(a, b, (((ca,), (cb,)), ((), ())))


def _bdot_c_fwd(a, b, ca, cb):
    return _bdot_c(a, b, ca, cb), (a, b)


def _bdot_c_bwd(ca, cb, res, ct):
    a, b = res
    fa, fb = 1 - ca, 1 - cb
    da = _bdot_raw(ct, b, (((1,), (fb,)), ((), ()))) if ca == 1 else _bdot_raw(b, ct, (((fb,), (1,)), ((), ())))
    db = _bdot_raw(a, ct, (((fa,), (0,)), ((), ()))) if cb == 0 else _bdot_raw(ct, a, (((0,), (fa,)), ((), ())))
    return da, db


_bdot_c.defvjp(_bdot_c_fwd, _bdot_c_bwd)


def _bdot(a, b, dn=NN):
    return _bdot_c(a, b, dn[0][0][0], dn[0][1][0])


def _bdot_nn(a, b):
    return _bdot(a, b, NN)


def _bdot_nt(a, b):
    return _bdot(a, b, NT)


def _split3(x):
    hi = x.astype(BF16)
    rest = x - hi.astype(F32)
    mid = rest.astype(BF16)
    return hi, mid, (rest - mid.astype(F32)).astype(BF16)


def _running_sum(x, dn):
    c = x.shape[0]
    row = lax.broadcasted_iota(jnp.int32, (c, c), 0)
    col = lax.broadcasted_iota(jnp.int32, (c, c), 1)
    tri = jnp.where(row >= col, 1.0, 0.0).astype(BF16)
    w = x.shape[1]
    parts = lax.dot_general(tri, jnp.concatenate(_split3(x), axis=1), dn, preferred_element_type=F32)
    return parts[:, :w] + parts[:, w:2 * w] + parts[:, 2 * w:]


@jax.custom_vjp
def _cumsum_rows(x):
    return _running_sum(x, NN)


_cumsum_rows.defvjp(lambda x: (_running_sum(x, NN), None), lambda _, ct: (_running_sum(ct, TN),))


@jax.custom_vjp
def _fold_rows(x):
    c = x.shape[0] // 2
    return x[:c] + x[c:]


_fold_rows.defvjp(lambda x: (_fold_rows(x), None), lambda _, ct: (jnp.concatenate([ct, ct], axis=0),))


@jax.custom_vjp
def _halves(x):
    n = x.shape[0] // 2
    return x[:n], x[n:]


_halves.defvjp(lambda x: (_halves(x), None), lambda _, cts: (jnp.concatenate(cts, axis=0),))


@jax.custom_vjp
def _quarters(x):
    n = x.shape[0] // 2
    return x[:n, :n], x[:n, n:], x[n:, :n], x[n:, n:]


_quarters.defvjp(lambda x: (_quarters(x), None),
                 lambda _, cts: (jnp.concatenate([jnp.concatenate(cts[:2], axis=1),
                                                  jnp.concatenate(cts[2:], axis=1)], axis=0),))


@jax.custom_vjp
def _sigmoid(x):
    return 1.0 / (1.0 + jnp.exp(-x))


def _sigmoid_fwd(x):
    s = _sigmoid(x)
    return s, s


_sigmoid.defvjp(_sigmoid_fwd, lambda s, ct: (ct * s * (1.0 - s),))


def _pick(n, cands):
    for c in cands:
        if n % c == 0:
            return c
    return n


def _cparams(sem, vmem_mb=None):
    kw = dict(dimension_semantics=sem)
    if vmem_mb is not None:
        kw["vmem_limit_bytes"] = vmem_mb * 1024 * 1024
    return pltpu.CompilerParams(**kw)


def _matmul(a, b, mode, *, name, extras=(), epilogue=None, out_dtypes=(F32,), tm=1024, tn=1024, tk=1024,
            hosted=(), scatter=False):
    if mode == "nn":
        (M, K), (_, N) = a.shape, b.shape
    elif mode == "tn":
        (K, M), (_, N) = a.shape, b.shape
    else:
        (M, K), (N, _) = a.shape, b.shape
    tm = _pick(M, (tm, 512, 256, 128))
    tn = _pick(N, (tn, 896, 768, 512, 384, 256, 128))
    tk = _pick(K, (tk, 512, 256, 128))
    nk = K // tk
    ne, nout = len(extras), len(out_dtypes)
    if mode == "nn":
        a_spec = pl.BlockSpec((tm, tk), lambda i, j, k: (i, k))
        b_spec = pl.BlockSpec((tk, tn), lambda i, j, k: (k, j))
        dn = NN
    elif mode == "tn":
        a_spec = pl.BlockSpec((tk, tm), lambda i, j, k: (k, i))
        b_spec = pl.BlockSpec((tk, tn), lambda i, j, k: (k, j))
        dn = TN
    else:
        a_spec = pl.BlockSpec((tm, tk), lambda i, j, k: (i, k))
        b_spec = pl.BlockSpec((tn, tk), lambda i, j, k: (j, k))
        dn = NT
    o_spec = pl.BlockSpec((tm, tn), lambda i, j, k: (i, j))
    grid = (M // tm, N // tn, nk)
    nex = len(hosted)

    def body(*refs):
        a_ref, b_ref = refs[:2]
        e_refs = refs[2:2 + ne]
        ex_in = refs[2 + ne:2 + ne + nex]
        o_refs = refs[2 + ne + nex:2 + ne + nex + nout]
        ex_out = refs[2 + ne + nex + nout:2 + ne + 2 * nex + nout]
        scratch = refs[2 + ne + 2 * nex + nout:]
        kstep = pl.program_id(2)
        if nex:
            at = [pl.program_id(d) for d in range(3)]
            first = jnp.logical_and(jnp.logical_and(at[0] == 0, at[1] == 0), at[2] == 0)
            last = jnp.logical_and(jnp.logical_and(at[0] == grid[0] - 1, at[1] == grid[1] - 1), at[2] == grid[2] - 1)
            _hosted_exchange(first, last, ex_in, ex_out, scratch[-3:], scatter)

        def finish(total):
            outs = (total,) if epilogue is None else epilogue(total, *[e[...] for e in e_refs])
            for o_ref, o in zip(o_refs, outs):
                o_ref[...] = o.astype(o_ref.dtype)

        if nk == 1:
            finish(_bdot_raw(a_ref[...], b_ref[...], dn))
            return
        acc = scratch[0]

        @pl.when(kstep == 0)
        def _():
            acc[...] = jnp.zeros_like(acc)

        acc[...] += _bdot_raw(a_ref[...], b_ref[...], dn)

        @pl.when(kstep == nk - 1)
        def _():
            finish(acc[...])

    hbm = pl.BlockSpec(memory_space=pltpu.HBM)
    outs = pl.pallas_call(
        body,
        grid=grid,
        in_specs=[a_spec, b_spec] + [o_spec] * ne + [hbm] * nex,
        out_specs=[o_spec] * nout + [hbm] * nex,
        out_shape=[jax.ShapeDtypeStruct((M, N), dt) for dt in out_dtypes] + _exchange_out_shapes(hosted, scatter),
        scratch_shapes=([pltpu.VMEM((tm, tn), F32)] if nk > 1 else []) + (_exchange_scratch(nex) if nex else []),
        compiler_params=_cparams(("arbitrary",) * 3 if nex else ("parallel", "parallel", "arbitrary"), 56),
        name=name,
    )(a, b, *extras, *hosted)
    if nex:
        return outs[:nout], outs[nout:]
    return outs[0] if nout == 1 else outs


def _rowwise(fn, rows, pars, out_rows, out_accs, *, tile, name, nsub=1, hosted=(), scatter=False):
    rows = [r if isinstance(r, tuple) else (r, r.shape[1], 0) for r in rows]
    R = rows[0][0].shape[0]
    tile = min(tile, R)
    chunk = tile // nsub
    ntile = R // tile
    nr, npar, nor, noa, nex = len(rows), len(pars), len(out_rows), len(out_accs), len(hosted)

    def body(*refs):
        rin = refs[:nr]
        pin = refs[nr:nr + npar]
        ex_in = refs[nr + npar:nr + npar + nex]
        orow = refs[nr + npar + nex:nr + npar + nex + nor]
        oacc = refs[nr + npar + nex + nor:nr + npar + nex + nor + noa]
        ex_out = refs[nr + npar + nex + nor + noa:nr + npar + 2 * nex + nor + noa]
        step = pl.program_id(0)
        _hosted_exchange(step == 0, step == ntile - 1, ex_in, ex_out, refs[nr + npar + 2 * nex + nor + noa:], scatter)
        pvals = [p[...] for p in pin]
        totals = []
        for sub in range(nsub):
            at = slice(sub * chunk, (sub + 1) * chunk)
            outs = fn(*[r[at, :] for r in rin], *pvals)
            for ref, o in zip(orow, outs[:nor]):
                if isinstance(o, (tuple, list)):
                    col = 0
                    for piece in o:
                        ref[at, col:col + piece.shape[1]] = piece.astype(ref.dtype)
                        col += piece.shape[1]
                else:
                    ref[at, :] = o.astype(ref.dtype)
            accs = list(outs[nor:])
            totals = accs if sub == 0 else [t + a for t, a in zip(totals, accs)]

        def accumulate(ref, o):
            @pl.when(step == 0)
            def _():
                ref[...] = o

            @pl.when(step > 0)
            def _():
                ref[...] += o

        for ref, o in zip(oacc, totals):
            accumulate(ref, o)

    def colspec(width, cb):
        return pl.BlockSpec((tile, width), lambda i: (i, cb))

    hbm = pl.BlockSpec(memory_space=pltpu.HBM)
    outs = pl.pallas_call(
        body,
        grid=(ntile,),
        in_specs=[colspec(w, cb) for (_, w, cb) in rows]
        + [pl.BlockSpec(p.shape, lambda i: (0, 0), pipeline_mode=pl.Buffered(1)) for p in pars] + [hbm] * nex,
        out_specs=[colspec(w, 0) for (w, _) in out_rows]
        + [pl.BlockSpec(s, lambda i: (0, 0)) for s in out_accs] + [hbm] * nex,
        out_shape=[jax.ShapeDtypeStruct((R, w), dt) for (w, dt) in out_rows]
        + [jax.ShapeDtypeStruct(s, F32) for s in out_accs] + _exchange_out_shapes(hosted, scatter),
        scratch_shapes=_exchange_scratch(nex) if nex else [],
        compiler_params=_cparams(("arbitrary",), 56),
        name=name,
    )(*[r[0] for r in rows], *pars, *hosted)
    return (outs[:nor + noa], outs[nor + noa:]) if nex else outs


def _rms_fn(x, g):
    return x * lax.rsqrt(jnp.mean(x * x, axis=-1, keepdims=True) + RMS_EPS) * g


FUSED_TILE = 512
FUSED_CHUNKS = 2


def _down_proj_loss(act, w_down, x1, tgt, g):
    d = x1.shape[1]

    def fn(av, xv, tv, wv, gv):
        x2 = xv + _bdot_raw(av, wv, NN)
        y, vjp = jax.vjp(_rms_fn, x2, gv)
        err = y - tv
        loss = 0.5 * jnp.sum(jnp.sum(err * err, axis=-1, keepdims=True), axis=0, keepdims=True) / d
        dx, dg = vjp(err / d)
        return dx, jnp.broadcast_to(loss, (1, LANES)), dg

    return _rowwise(fn, [act, x1, tgt], [w_down, g], [(d, F32)], [(1, LANES), g.shape],
                    tile=FUSED_TILE, nsub=FUSED_CHUNKS, name="mlp_down_final_norm_loss")


def _proj_bwd_norm_bwd(cts, weights_t, x, dres, g, name, hosted=(), scatter=False):
    n = len(cts)

    def fn(*vals):
        ctv, (xv, dresv), wv, gv = vals[:n], vals[n:n + 2], vals[n + 2:2 * n + 2], vals[-1]
        dh = _bdot_raw(ctv[0], wv[0], NT)
        for c, w in zip(ctv[1:], wv[1:]):
            dh = dh + _bdot_raw(c, w, NT)
        _, vjp = jax.vjp(_rms_fn, xv, gv)
        dx, dg = vjp(dh)
        return dx + dresv, dg

    return _rowwise(fn, [*cts, x, dres], [*weights_t, g], [(x.shape[1], F32)], [g.shape],
                    tile=FUSED_TILE, nsub=FUSED_CHUNKS, name=name, hosted=hosted, scatter=scatter)


def _head_sum_matrix():
    i = lax.broadcasted_iota(jnp.int32, (RWKV_WIDTH, RWKV_WIDTH), 0) // HEAD_DIM
    j = lax.broadcasted_iota(jnp.int32, (RWKV_WIDTH, RWKV_WIDTH), 1) // HEAD_DIM
    return (i == j).astype(BF16)


def _head_sums_raw(x, esum):
    hi = x.astype(BF16)
    lo = (x - hi.astype(F32)).astype(BF16)
    return _dot(hi, esum) + _dot(lo, esum)


@jax.custom_vjp
def _head_sums(x, esum):
    return _head_sums_raw(x, esum)


_head_sums.defvjp(lambda x, esum: (_head_sums_raw(x, esum), esum),
                  lambda esum, ct: (_head_sums_raw(ct, esum), jnp.zeros_like(esum)))


def _prep_core(xr, xk, xv, xwa, xg, w0, w2p, a0, a2p, g2, k_k, k_a, esum):
    lw = -DECAY_SCALE * _sigmoid(w0 + _bdot_nn(jnp.tanh(xwa), w2p))
    a = _sigmoid(a0 + _bdot_nn(xwa, a2p))
    g = _bdot_nn(_sigmoid(xg), g2)
    kk0 = xk * k_k
    kk = kk0 * jnp.minimum(lax.rsqrt(_head_sums(kk0 * kk0, esum)), 1.0 / L2_EPS)
    k = xk * (1.0 + (a - 1.0) * k_a)
    return xr, lw, k, xv, kk, a, g


_SEGS = ((0, 512), (512, 1024), (1024, 1536), (1536, 1664), (1664, 1792))


PREP_TILE = 256
SUBLANES = 8


def _shifted_tokens(z_ref, zprev_ref, tile_index, seq):
    zc = z_ref[...]
    start = (tile_index * PREP_TILE) % seq == 0
    before = jnp.where(start, 0.0, zprev_ref[SUBLANES - 1:SUBLANES, :])
    rowid = lax.broadcasted_iota(jnp.int32, zc.shape, 0)
    return zc, jnp.where(rowid == 0, before, pltpu.roll(zc, 1, 0))


def _prep_specs(z, mu, pars, index):
    width = z.shape[1]
    per = PREP_TILE // SUBLANES
    return ([pl.BlockSpec((PREP_TILE, width), lambda i: (index(i), 0)),
             pl.BlockSpec((SUBLANES, width), lambda i: (jnp.maximum(index(i) * per - 1, 0), 0))],
            [pl.BlockSpec(p.shape, lambda i: (0, 0)) for p in (mu, *pars)])


def _norm_in_proj_prep(x, seq, g, w_attn, w_rw, mu, pars):
    rows, d = x.shape
    chunk = FUSED_TILE // FUSED_CHUNKS
    npar = len(pars)
    wa_width, wr_width = w_attn.shape[1], w_rw.shape[1]

    def body(x_ref, g_ref, wa_ref, wr_ref, mu_ref, *rest):
        par_refs = rest[:npar]
        h_ref, za_ref, zr_ref = rest[npar:npar + 3]
        out_refs, carry = rest[npar + 3:-1], rest[-1]
        step = pl.program_id(0)

        @pl.when(step == 0)
        def _():
            carry[...] = jnp.zeros_like(carry)

        pv = [p[...] for p in par_refs]
        for sub in range(FUSED_CHUNKS):
            at = slice(sub * chunk, (sub + 1) * chunk)
            h = _rms_fn(x_ref[at, :], g_ref[...])
            h_ref[at, :] = h.astype(h_ref.dtype)
            za_ref[at, :] = _bdot_raw(h, wa_ref[...], NN)
            zc = _bdot_raw(h, wr_ref[...], NN)
            zr_ref[at, :] = zc
            start = (step * FUSED_TILE + sub * chunk) % seq == 0
            before = jnp.where(start, 0.0, carry[SUBLANES - 1:SUBLANES, :])
            rowid = lax.broadcasted_iota(jnp.int32, zc.shape, 0)
            zp = jnp.where(rowid == 0, before, pltpu.roll(zc, 1, 0))
            carry[...] = zc[chunk - SUBLANES:chunk, :]
            zs = zc + (zp - zc) * mu_ref[...]
            outs = _prep_core(*[zs[:, a:b] for a, b in _SEGS], *pv)
            for ref, o in zip(out_refs, outs):
                ref[at, :] = o

    tiled = lambda width: pl.BlockSpec((FUSED_TILE, width), lambda i: (i, 0))
    resident = lambda a: pl.BlockSpec(a.shape, lambda i: (0, 0), pipeline_mode=pl.Buffered(1))
    return pl.pallas_call(
        body,
        grid=(rows // FUSED_TILE,),
        in_specs=[tiled(d)] + [resident(a) for a in (g, w_attn, w_rw, mu, *pars)],
        out_specs=[tiled(d), tiled(wa_width), tiled(wr_width)] + [tiled(RWKV_WIDTH)] * 7,
        out_shape=[jax.ShapeDtypeStruct((rows, d), BF16), jax.ShapeDtypeStruct((rows, wa_width), F32),
                   jax.ShapeDtypeStruct((rows, wr_width), F32)] + [jax.ShapeDtypeStruct((rows, RWKV_WIDTH), F32)] * 7,
        scratch_shapes=[pltpu.VMEM((SUBLANES, wr_width), F32)],
        compiler_params=_cparams(("arbitrary",), 56),
        name="attn_norm_in_proj_rwkv_prep",
    )(x, g, w_attn, w_rw, mu, *pars)


def _prep_bwd(z, seq, cts, mu, pars):
    rows, width = z.shape
    ntile = rows // PREP_TILE
    npar, nct = len(pars), len(cts)
    acc_shapes = [(1, b - a) for a, b in _SEGS] + [p.shape for p in pars[:-1]]

    def body(z_ref, zprev_ref, *rest):
        ct_refs = rest[:nct]
        mu_ref = rest[nct]
        par_refs = rest[nct + 1:nct + 1 + npar]
        dz_ref = rest[nct + 1 + npar]
        acc_refs = rest[nct + 2 + npar:-1]
        carry = rest[-1]
        step = pl.program_id(0)
        tile_index = ntile - 1 - step

        @pl.when(step == 0)
        def _():
            carry[...] = jnp.zeros_like(carry)

        zc, zp = _shifted_tokens(z_ref, zprev_ref, tile_index, seq)
        mu_v = mu_ref[...]
        diff = zp - zc
        zs = zc + diff * mu_v
        dra, drb, dlw, dka, dkb, dva, dvb, dkk, da, dg = [c[...] for c in ct_refs]
        pv = [p[...] for p in par_refs]
        _, vjp = jax.vjp(lambda *args: _prep_core(*args, pv[-1]), *[zs[:, a:b] for a, b in _SEGS], *pv[:-1])
        grads = vjp((dra + drb, dlw, dka + dkb, dva + dvb, dkk, da, dg))
        dsegs, dpars = grads[:5], grads[5:]
        last_of_sequence = ((tile_index + 1) * PREP_TILE) % seq == 0
        accs = []
        for ds, (a, b) in zip(dsegs, _SEGS):
            mu_s = mu_v[:, a:b]
            dzp = ds * mu_s
            after = jnp.where(last_of_sequence, 0.0, carry[0:1, a:b])
            rowid = lax.broadcasted_iota(jnp.int32, dzp.shape, 0)
            from_next = jnp.where(rowid == PREP_TILE - 1, after, pltpu.roll(dzp, PREP_TILE - 1, 0))
            dz_ref[:, a:b] = (ds * (1.0 - mu_s) + from_next).astype(dz_ref.dtype)
            carry[:, a:b] = dzp[0:SUBLANES, :]
            accs.append(jnp.sum(ds * diff[:, a:b], axis=0, keepdims=True))
        accs.extend(dpars)

        def accumulate(ref, o):
            @pl.when(step == 0)
            def _():
                ref[...] = o

            @pl.when(step > 0)
            def _():
                ref[...] += o

        for ref, o in zip(acc_refs, accs):
            accumulate(ref, o)

    rev = lambda i: ntile - 1 - i
    zspecs, pspecs = _prep_specs(z, mu, pars, rev)
    return pl.pallas_call(
        body,
        grid=(ntile,),
        in_specs=zspecs + [pl.BlockSpec((PREP_TILE, RWKV_WIDTH), lambda i: (rev(i), 0))] * nct + pspecs,
        out_specs=[pl.BlockSpec((PREP_TILE, width), lambda i: (rev(i), 0))]
        + [pl.BlockSpec(s, lambda i: (0, 0)) for s in acc_shapes],
        out_shape=[jax.ShapeDtypeStruct((rows, width), BF16)] + [jax.ShapeDtypeStruct(s, F32) for s in acc_shapes],
        scratch_shapes=[pltpu.VMEM((SUBLANES, width), F32)],
        compiler_params=_cparams(("arbitrary",), 56),
        name="rwkv_prep_bwd",
    )(z, z, *cts, mu, *pars)


def _post_fn(y, r, k, v, g, ln_w, ln_b, r_k, esum):
    mean = _head_sums(y, esum) * (1.0 / HEAD_DIM)
    yc = y - mean
    var = _head_sums(yc * yc, esum) * (1.0 / HEAD_DIM)
    yn = yc * lax.rsqrt(var + GN_EPS) * ln_w + ln_b
    bonus = _head_sums(r * k * r_k, esum) * v
    return (yn + bonus) * g


def _post_out_proj_norm(y, r, k, v, g, attn_out, x, pars, w_attn_rows, w_rwkv_rows, g_norm):
    npar = len(pars)

    def fn(yv, rv, kv, vv, gv, av, xv, *rest):
        wa, wr, gn = rest[npar:]
        rw = _post_fn(yv, rv, kv, vv, gv, *rest[:npar])
        x1 = xv + _bdot_raw(av, wa, NN) + _bdot_raw(rw, wr, NN)
        return rw, x1, _rms_fn(x1, gn)

    d = x.shape[1]
    return _rowwise(fn, [y, r, k, v, g, attn_out, x], [*pars, w_attn_rows, w_rwkv_rows, g_norm],
                    [(RWKV_WIDTH, BF16), (d, F32), (d, BF16)], [],
                    tile=FUSED_TILE, nsub=FUSED_CHUNKS, name="rwkv_post_out_proj_mlp_norm")


def _out_proj_bwd_post_bwd(dx1, y, r, k, v, g, pars, w_attn_rows, w_rwkv_rows):
    npar = len(pars)

    def fn(dxv, yv, rv, kv, vv, gv, *rest):
        wa, wr = rest[npar:]
        esum = rest[npar - 1]
        d_attn = _bdot_raw(dxv, wa, NT)
        d_rw = _bdot_raw(dxv, wr, NT)
        _, vjp = jax.vjp(lambda *a: _post_fn(*a, esum), yv, rv, kv, vv, gv, *rest[:npar - 1])
        return (d_attn, *vjp(d_rw))

    return _rowwise(fn, [dx1, y, r, k, v, g], [*pars, w_attn_rows, w_rwkv_rows], [(RWKV_WIDTH, F32)] * 6,
                    [p.shape for p in pars[:-1]], tile=FUSED_TILE, nsub=FUSED_CHUNKS, name="out_proj_bwd_rwkv_post_bwd")


def _tri_inverses(ms):
    n = ms[0].shape[0]
    row = lax.broadcasted_iota(jnp.int32, (n, n), 0)
    col = lax.broadcasted_iota(jnp.int32, (n, n), 1)
    eye = jnp.where(row == col, 1.0, 0.0)
    t_inv = [eye + m for m in ms]
    power = [_bdot_raw(m, m, NN) for m in ms]
    steps = int(math.log2(n // 2)) - 1
    for step in range(steps):
        if step < steps - 1:
            both = [_bdot_raw(jnp.concatenate([p, t], axis=0), p, NN) for p, t in zip(power, t_inv)]
            power = [b[:n] for b in both]
            t_inv = [t + b[n:] for t, b in zip(t_inv, both)]
        else:
            t_inv = [t + _bdot_raw(t, p, NN) for t, p in zip(t_inv, power)]
    return t_inv


@jax.custom_vjp
def _tri_solve(ms, xs):
    return tuple(_bdot_raw(t, x, NN) for t, x in zip(_tri_inverses(ms), xs))


def _tri_solve_fwd(ms, xs):
    t_inv = _tri_inverses(ms)
    us = tuple(_bdot_raw(t, x, NN) for t, x in zip(t_inv, xs))
    return us, (tuple(t_inv), us)


def _tri_solve_bwd(res, dus):
    t_inv, us = res
    dxs = tuple(_bdot_raw(t, du, TN) for t, du in zip(t_inv, dus))
    dms = tuple(_bdot_raw(dx, u, NT) for dx, u in zip(dxs, us))
    return dms, dxs


_tri_solve.defvjp(_tri_solve_fwd, _tri_solve_bwd)


@functools.partial(jax.custom_vjp, nondiff_argnums=(3, 4))
def _kept_bdot_c(a, b, kept, ca, cb):
    return kept.astype(F32)


def _kept_bdot_c_fwd(a, b, kept, ca, cb):
    return kept.astype(F32), (a, b, kept)


def _kept_bdot_c_bwd(ca, cb, res, ct):
    a, b, kept = res
    return (*_bdot_c_bwd(ca, cb, (a, b), ct), jnp.zeros_like(kept))


_kept_bdot_c.defvjp(_kept_bdot_c_fwd, _kept_bdot_c_bwd)


def _kept_bdot(a, b, kept, dn):
    return _kept_bdot_c(a, b, kept, dn[0][0][0], dn[0][1][0])


@jax.custom_vjp
def _kept_tri_solve(ms, xs, t_inv, us):
    return tuple(u.astype(F32) for u in us)


def _kept_tri_solve_fwd(ms, xs, t_inv, us):
    return tuple(u.astype(F32) for u in us), (t_inv, us)


def _kept_tri_solve_bwd(res, dus):
    t_inv, us = res
    dms, dxs = _tri_solve_bwd(res, dus)
    return dms, dxs, tuple(jnp.zeros_like(t) for t in t_inv), tuple(jnp.zeros_like(u) for u in us)


_kept_tri_solve.defvjp(_kept_tri_solve_fwd, _kept_tri_solve_bwd)


def _chunk_fn(ss, rs, lws, ks, vs, kks, als, kept=None):
    c = rs[0].shape[0]
    n = 2 * c
    row = lax.broadcasted_iota(jnp.int32, (n, n), 0)
    col = lax.broadcasted_iota(jnp.int32, (n, n), 1)
    incl = (row % c) >= (col % c)
    strict = (row % c) > (col % c)
    lane = lax.broadcasted_iota(jnp.int32, (1, LANES), 1)
    m_lo = jnp.where(lane < HEAD_DIM, 1.0, 0.0)
    m_hi = 1.0 - m_lo

    def stack(a):
        return jnp.concatenate([a * m_lo, a * m_hi], axis=0)

    cums = [_cumsum_rows(lw) for lw in lws]
    totals = [jnp.sum(lw, axis=0, keepdims=True) for lw in lws]
    bs = [kk * al for kk, al in zip(kks, als)]
    grows = [jnp.exp(-cum) for cum in cums]
    a_s = [stack(-kk * jnp.exp(cum - lw)) for kk, cum, lw in zip(kks, cums, lws)]
    b_s = [stack(b * g) for b, g in zip(bs, grows)]
    k_s = [stack(k * g) for k, g in zip(ks, grows)]
    r_s = [stack(r * jnp.exp(cum)) for r, cum in zip(rs, cums)]
    v_s = [stack(v) for v in vs]
    pair = lambda p, q: jnp.concatenate([p, q], axis=0)
    ar_s = [pair(a, r) for a, r in zip(a_s, r_s)]
    if kept is None:
        products = [_bdot_raw(ar, pair(b, k), NT) for ar, b, k in zip(ar_s, b_s, k_s)]
    else:
        products = [_kept_bdot(ar, pair(b, k), kp, NT) for ar, b, k, kp in zip(ar_s, b_s, k_s, kept[0])]
    blocks = [_quarters(p) for p in products]
    m_ab = [jnp.where(strict, q[0], 0.0) for q in blocks]
    m_ak = [jnp.where(strict, q[1], 0.0) for q in blocks]
    m_rb = [jnp.where(incl, q[2], 0.0) for q in blocks]
    m_rk = [jnp.where(incl, q[3], 0.0) for q in blocks]
    from_state = [_halves(_bdot(ar, s, NT)) for ar, s in zip(ar_s, ss)]
    from_v = [_halves(_bdot(pair(mk, mr), v)) for mk, mr, v in zip(m_ak, m_rk, v_s)]
    x = tuple(fs[0] + fv[0] for fs, fv in zip(from_state, from_v))
    if kept is None:
        t_inv = _tri_inverses(m_ab)
        u = [_bdot_raw(t, xx, NN) for t, xx in zip(t_inv, x)]
    else:
        u = _kept_tri_solve(tuple(m_ab), x, kept[1], kept[2])
    y = [_fold_rows(fs[1] + _bdot(mb, uu) + fv[1]) for fs, mb, uu, fv in zip(from_state, m_rb, u, from_v)]
    tails = [jnp.exp(tot - cum) for tot, cum in zip(totals, cums)]
    s_new = [s * jnp.exp(tot) + _bdot(pair(uu, v), pair(stack(b * tl), stack(k * tl)), TN)
             for s, tot, uu, b, tl, v, k in zip(ss, totals, u, bs, tails, v_s, ks)]
    if kept is None:
        keep = lambda vals: tuple(v.astype(BF16) for v in vals)
        return tuple(y), tuple(s_new), (keep(products), keep(t_inv), keep(u))
    return tuple(y), tuple(s_new)


def _chains(bsz, npair):
    return [(b, p, slice(p * LANES, (p + 1) * LANES)) for b in range(bsz) for p in range(npair)]


def _hosted_exchange(first, last, ex_in, ex_out, sems, scatter):
    if not ex_in:
        return

    @pl.when(first)
    def _():
        _exchange_start(_exchange_copies(ex_in, ex_out, *sems, scatter, arrivals=False))

    @pl.when(last)
    def _():
        _exchange_wait(_exchange_copies(ex_in, ex_out, *sems, scatter, arrivals=True))


def _rwkv_fwd(r, lw, k, v, kk, al, hosted=(), scatter=False):
    bsz, t, w = r.shape
    npair, nchunk = w // LANES, t // CHUNK
    chains = _chains(bsz, npair)
    nex = len(hosted)

    def body(*refs):
        r_ref, lw_ref, k_ref, v_ref, kk_ref, al_ref = refs[:6]
        ex_in = refs[6:6 + nex]
        y_ref, sall_ref, prod_ref, tinv_ref, u_ref = refs[6 + nex:11 + nex]
        ex_out = refs[11 + nex:11 + 2 * nex]
        s_scr = refs[11 + 2 * nex]
        step = pl.program_id(0)

        @pl.when(step == 0)
        def _():
            s_scr[...] = jnp.zeros_like(s_scr)

        _hosted_exchange(step == 0, step == nchunk - 1, ex_in, ex_out, refs[12 + 2 * nex:], scatter)
        ss = tuple(s_scr[i] for i in range(len(chains)))
        for i, s in enumerate(ss):
            sall_ref[0, i] = s
        ys, s_new, kept = _chunk_fn(ss, *[tuple(ref[b, :, cols] for b, _, cols in chains)
                                          for ref in (r_ref, lw_ref, k_ref, v_ref, kk_ref, al_ref)])
        for i, (b, _, cols) in enumerate(chains):
            y_ref[b, :, cols] = ys[i]
            s_scr[i] = s_new[i]
            prod_ref[0, i], tinv_ref[0, i], u_ref[0, i] = kept[0][i], kept[1][i], kept[2][i]

    spec = pl.BlockSpec((bsz, CHUNK, w), lambda c: (0, c, 0))
    hbm = pl.BlockSpec(memory_space=pltpu.HBM)
    per_chunk = lambda n: pl.BlockSpec((1, len(chains), n, n), lambda c: (c, 0, 0, 0))
    kept_shape = lambda n: jax.ShapeDtypeStruct((nchunk, len(chains), n, n), BF16)
    outs = pl.pallas_call(
        body,
        grid=(nchunk,),
        in_specs=[spec] * 6 + [hbm] * nex,
        out_specs=[spec, per_chunk(LANES), per_chunk(4 * CHUNK), per_chunk(2 * CHUNK), per_chunk(2 * CHUNK)]
        + [hbm] * nex,
        out_shape=[jax.ShapeDtypeStruct((bsz, t, w), F32),
                   jax.ShapeDtypeStruct((nchunk, len(chains), LANES, LANES), F32),
                   kept_shape(4 * CHUNK), kept_shape(2 * CHUNK), kept_shape(2 * CHUNK)]
        + _exchange_out_shapes(hosted, scatter),
        scratch_shapes=[pltpu.VMEM((len(chains), LANES, LANES), F32)] + (_exchange_scratch(nex) if nex else []),
        compiler_params=_cparams(("arbitrary",), 48),
        name="rwkv_chunk_fwd",
    )(r, lw, k, v, kk, al, *hosted)
    return outs[0], outs[1:5], outs[5:]


def _rwkv_bwd(r, lw, k, v, kk, al, from_fwd, dy, hosted=(), scatter=False):
    bsz, t, w = r.shape
    npair, nchunk = w // LANES, t // CHUNK
    chains = _chains(bsz, npair)
    nex = len(hosted)

    def body(*refs):
        r_ref, lw_ref, k_ref, v_ref, kk_ref, al_ref, s_ref, prod_ref, tinv_ref, u_ref, dy_ref = refs[:11]
        ex_in = refs[11:11 + nex]
        out_refs = refs[11 + nex:17 + nex]
        ex_out = refs[17 + nex:17 + 2 * nex]
        ds_scr = refs[17 + 2 * nex]
        step = pl.program_id(0)

        @pl.when(step == 0)
        def _():
            ds_scr[...] = jnp.zeros_like(ds_scr)

        _hosted_exchange(step == 0, step == nchunk - 1, ex_in, ex_out, refs[18 + 2 * nex:], scatter)
        ss = tuple(s_ref[0, i] for i in range(len(chains)))
        kept = tuple(tuple(ref[0, i] for i in range(len(chains))) for ref in (prod_ref, tinv_ref, u_ref))
        _, vjp = jax.vjp(functools.partial(_chunk_fn, kept=kept), ss,
                         *[tuple(ref[b, :, cols] for b, _, cols in chains)
                           for ref in (r_ref, lw_ref, k_ref, v_ref, kk_ref, al_ref)])
        grads = vjp((tuple(dy_ref[b, :, cols] for b, _, cols in chains),
                     tuple(ds_scr[i] for i in range(len(chains)))))
        for i, (b, _, cols) in enumerate(chains):
            ds_scr[i] = grads[0][i]
            for ref, gval in zip(out_refs, grads[1:]):
                ref[b, :, cols] = gval[i]

    spec = pl.BlockSpec((bsz, CHUNK, w), lambda c: (0, nchunk - 1 - c, 0))
    per_chunk = lambda n: pl.BlockSpec((1, len(chains), n, n), lambda c: (nchunk - 1 - c, 0, 0, 0))
    hbm = pl.BlockSpec(memory_space=pltpu.HBM)
    outs = pl.pallas_call(
        body,
        grid=(nchunk,),
        in_specs=[spec] * 6 + [per_chunk(LANES), per_chunk(4 * CHUNK), per_chunk(2 * CHUNK), per_chunk(2 * CHUNK), spec]
        + [hbm] * nex,
        out_specs=[spec] * 6 + [hbm] * nex,
        out_shape=[jax.ShapeDtypeStruct((bsz, t, w), F32)] * 6 + _exchange_out_shapes(hosted, scatter),
        scratch_shapes=[pltpu.VMEM((len(chains), LANES, LANES), F32)] + (_exchange_scratch(nex) if nex else []),
        compiler_params=_cparams(("arbitrary",), 48),
        name="rwkv_chunk_bwd",
    )(r, lw, k, v, kk, al, *from_fwd, dy, *hosted)
    return outs[:6], outs[6:]


def _alibi_slope(head):
    return 2.0 ** (-8.0 * (head + 1) / N_ATTN_HEADS)


def _attn_block(qs, kp, kc, vp, vc, sinks, first):
    row = lax.broadcasted_iota(jnp.int32, (BLOCK, 2 * BLOCK), 0)
    col = lax.broadcasted_iota(jnp.int32, (BLOCK, 2 * BLOCK), 1)
    lane = lax.broadcasted_iota(jnp.int32, (1, LANES), 1)
    halves = [jnp.where((lane // HEAD_DIM) == half, 1.0, 0.0) for half in range(2)]
    srow = lax.broadcasted_iota(jnp.int32, (LANES, LANES), 0)
    scol = lax.broadcasted_iota(jnp.int32, (LANES, LANES), 1)
    swap = jnp.where((srow + HEAD_DIM) % LANES == scol, 1.0, 0.0)
    dist = row - col + BLOCK
    valid = jnp.logical_and(jnp.logical_and(dist >= 0, dist < BLOCK),
                            jnp.logical_or(col >= BLOCK, jnp.logical_not(first)))
    dist = dist.astype(F32)
    scale = HEAD_DIM ** -0.5
    stored = (jnp.concatenate([kp, kc], axis=0), jnp.concatenate([vp, vc], axis=0))
    swapped = tuple(_bdot_nn(t, swap) for t in stored)
    heads = [(pair, half) for pair in range(len(qs)) for half in range(2)]
    kv = [stored if half == pair // 2 else swapped for pair, half in heads]
    slopes = [_alibi_slope(2 * pair + half) for pair, half in heads]
    qa = [qs[pair] * halves[half] for pair, half in heads]
    s = [jnp.where(valid, _bdot_nt(q, t[0]) * scale - sl * dist, NEG_INF) for q, t, sl in zip(qa, kv, slopes)]
    mx = [lax.stop_gradient(jnp.maximum(jnp.max(a, axis=-1, keepdims=True), sk)) for a, sk in zip(s, sinks)]
    e = [jnp.exp(a - m) for a, m in zip(s, mx)]
    es = [jnp.exp(sk - m) for sk, m in zip(sinks, mx)]
    inv = [1.0 / (jnp.sum(a, axis=-1, keepdims=True) + b) for a, b in zip(e, es)]
    o = [_bdot_nn(a * i, t[1]) for a, i, t in zip(e, inv, kv)]
    outs = tuple(o[2 * pair] * halves[0] + o[2 * pair + 1] * halves[1] for pair in range(len(qs)))
    return outs, [lax.stop_gradient(b * i) for b, i in zip(es, inv)]


def _sink_values(sink_ref):
    return [jnp.max(sink_ref[h:h + 1, :], axis=-1, keepdims=True) for h in range(N_ATTN_HEADS)]


def _attn_fwd(z, sink_rows):
    bsz, t, _ = z.shape
    nb = t // BLOCK
    npair = ATTN_WIDTH // LANES

    def body(q_ref, kp_ref, kc_ref, vp_ref, vc_ref, sink_ref, o_ref):
        first = pl.program_id(1) == 0
        qs = tuple(q_ref[0, :, pair * LANES:(pair + 1) * LANES] for pair in range(npair))
        outs, _ = _attn_block(qs, kp_ref[0], kc_ref[0], vp_ref[0], vc_ref[0], _sink_values(sink_ref), first)
        for pair in range(npair):
            o_ref[0, :, pair * LANES:(pair + 1) * LANES] = outs[pair].astype(o_ref.dtype)

    kcol, vcol = ATTN_WIDTH // KV_WIDTH, ATTN_WIDTH // KV_WIDTH + 1
    return pl.pallas_call(
        body,
        grid=(bsz, nb),
        in_specs=[pl.BlockSpec((1, BLOCK, ATTN_WIDTH), lambda b, n: (b, n, 0)),
                  pl.BlockSpec((1, BLOCK, KV_WIDTH), lambda b, n: (b, jnp.maximum(n - 1, 0), kcol)),
                  pl.BlockSpec((1, BLOCK, KV_WIDTH), lambda b, n: (b, n, kcol)),
                  pl.BlockSpec((1, BLOCK, KV_WIDTH), lambda b, n: (b, jnp.maximum(n - 1, 0), vcol)),
                  pl.BlockSpec((1, BLOCK, KV_WIDTH), lambda b, n: (b, n, vcol)),
                  pl.BlockSpec(sink_rows.shape, lambda b, n: (0, 0))],
        out_specs=pl.BlockSpec((1, BLOCK, ATTN_WIDTH), lambda b, n: (b, n, 0)),
        out_shape=jax.ShapeDtypeStruct((bsz, t, ATTN_WIDTH), BF16),
        compiler_params=_cparams(("parallel", "arbitrary"), 48),
        name="swa_fwd",
    )(z, z, z, z, z, sink_rows)


def _attn_bwd(z, dout, sink_rows):
    bsz, t, _ = z.shape
    nb = t // BLOCK
    npair = ATTN_WIDTH // LANES

    def body(q_ref, kp_ref, kc_ref, vp_ref, vc_ref, do_ref, sink_ref, dz_ref, dsink_ref, carry):
        step = pl.program_id(1)
        n = nb - 1 - step
        first = n == 0

        @pl.when(step == 0)
        def _():
            carry[...] = jnp.zeros_like(carry)

        @pl.when(jnp.logical_and(step == 0, pl.program_id(0) == 0))
        def _():
            dsink_ref[...] = jnp.zeros_like(dsink_ref)

        lane = lax.broadcasted_iota(jnp.int32, (1, LANES), 1)
        qs = tuple(q_ref[0, :, pair * LANES:(pair + 1) * LANES] for pair in range(npair))
        dos = tuple(do_ref[0, :, pair * LANES:(pair + 1) * LANES] for pair in range(npair))
        fn = functools.partial(_attn_block, sinks=_sink_values(sink_ref), first=first)
        outs, vjp, psinks = jax.vjp(fn, qs, kp_ref[0], kc_ref[0], vp_ref[0], vc_ref[0], has_aux=True)
        dqs, dkp, dkc, dvp, dvc = vjp(dos)
        for pair in range(npair):
            dz_ref[0, :, pair * LANES:(pair + 1) * LANES] = dqs[pair].astype(dz_ref.dtype)
            for half in range(2):
                m = jnp.where((lane // HEAD_DIM) == half, 1.0, 0.0)
                delta = jnp.sum(dos[pair] * outs[pair] * m, axis=-1, keepdims=True)
                head = 2 * pair + half
                ds = -jnp.sum(psinks[head] * delta, axis=0, keepdims=True)
                dsink_ref[head:head + 1, :] += jnp.broadcast_to(ds, (1, LANES))
        dz_ref[0, :, ATTN_WIDTH:ATTN_WIDTH + KV_WIDTH] = (dkc + carry[0]).astype(dz_ref.dtype)
        dz_ref[0, :, ATTN_WIDTH + KV_WIDTH:QKV_WIDTH] = (dvc + carry[1]).astype(dz_ref.dtype)
        carry[0] = dkp
        carry[1] = dvp

    kcol, vcol = ATTN_WIDTH // KV_WIDTH, ATTN_WIDTH // KV_WIDTH + 1
    rev = lambda n: nb - 1 - n
    return pl.pallas_call(
        body,
        grid=(bsz, nb),
        in_specs=[pl.BlockSpec((1, BLOCK, ATTN_WIDTH), lambda b, n: (b, rev(n), 0)),
                  pl.BlockSpec((1, BLOCK, KV_WIDTH), lambda b, n: (b, jnp.maximum(rev(n) - 1, 0), kcol)),
                  pl.BlockSpec((1, BLOCK, KV_WIDTH), lambda b, n: (b, rev(n), kcol)),
                  pl.BlockSpec((1, BLOCK, KV_WIDTH), lambda b, n: (b, jnp.maximum(rev(n) - 1, 0), vcol)),
                  pl.BlockSpec((1, BLOCK, KV_WIDTH), lambda b, n: (b, rev(n), vcol)),
                  pl.BlockSpec((1, BLOCK, ATTN_WIDTH), lambda b, n: (b, rev(n), 0)),
                  pl.BlockSpec(sink_rows.shape, lambda b, n: (0, 0))],
        out_specs=[pl.BlockSpec((1, BLOCK, QKV_WIDTH), lambda b, n: (b, rev(n), 0)),
                   pl.BlockSpec((N_ATTN_HEADS, LANES), lambda b, n: (0, 0))],
        out_shape=[jax.ShapeDtypeStruct((bsz, t, QKV_WIDTH), BF16),
                   jax.ShapeDtypeStruct((N_ATTN_HEADS, LANES), F32)],
        scratch_shapes=[pltpu.VMEM((2, BLOCK, KV_WIDTH), F32)],
        compiler_params=_cparams(("arbitrary", "arbitrary"), 48),
        name="swa_bwd",
    )(z, z, z, z, z, dout, sink_rows)


def _exchange_out_shapes(arrays, scatter):
    return [jax.ShapeDtypeStruct((N_DEV,) + (a.shape[1:] if scatter else a.shape), a.dtype) for a in arrays]


def _exchange_scratch(n):
    return [pltpu.SemaphoreType.DMA((n, N_DEV - 1)), pltpu.SemaphoreType.DMA((n, N_DEV - 1)),
            pltpu.SemaphoreType.DMA((n,))]


def _exchange_copies(ins, outs, send_sems, recv_sems, local_sems, scatter, arrivals=True):
    x, y, c = lax.axis_index("x"), lax.axis_index("y"), lax.axis_index("c")
    me = 4 * x + 2 * y + c
    copies = []
    for i in range(len(ins)):
        own = pltpu.make_async_copy(ins[i].at[me] if scatter else ins[i], outs[i].at[me], local_sems.at[i])
        copies.append((own, None, True))
        for d in range(1, N_DEV):
            px = 1 - x if d & 4 else x
            py = 1 - y if d & 2 else y
            pc = 1 - c if d & 1 else c
            peer = 4 * px + 2 * py + pc
            src = ins[i].at[peer] if scatter else ins[i]
            send = pltpu.make_async_remote_copy(src, outs[i].at[me], send_sems.at[i, d - 1], recv_sems.at[i, d - 1],
                                                device_id=(px, py, pc), device_id_type=MESH)
            recv = pltpu.make_async_remote_copy(src, outs[i].at[peer], send_sems.at[i, d - 1], recv_sems.at[i, d - 1],
                                                device_id=(px, py, pc), device_id_type=MESH) if arrivals else None
            copies.append((send, recv, False))
    return copies


def _exchange_start(copies):
    for send, _, _ in copies:
        send.start()


def _exchange_wait(copies):
    for send, recv, local in copies:
        if local:
            send.wait()
        else:
            send.wait_send()
            recv.wait_recv()


def _gather_two_level(arrays, name):
    n = len(arrays)

    def body(*refs):
        ins, outs = refs[:n], refs[n:2 * n]
        send_sems, recv_sems, local_sems = refs[2 * n:]
        x, y, c = lax.axis_index("x"), lax.axis_index("y"), lax.axis_index("c")
        index = lambda px, py, pc: 4 * px + 2 * py + pc
        sibling = (x, y, 1 - c)
        chips = [(1 - x, y), (x, 1 - y), (1 - x, 1 - y)]

        def copy(i, k, block, to, src=None):
            slot = outs[i].at[index(*block)]
            return pltpu.make_async_remote_copy(slot if src is None else src, slot, send_sems.at[i, k],
                                                recv_sems.at[i, k], device_id=to, device_id_type=MESH)

        local, sends = [], []
        for i in range(n):
            own = pltpu.make_async_copy(ins[i], outs[i].at[index(x, y, c)], local_sems.at[i])
            own.start()
            local.append(own)
            first = [copy(i, 0, (x, y, c), sibling, src=ins[i])]
            first += [copy(i, 1 + j, (x, y, c), (*chip, c), src=ins[i]) for j, chip in enumerate(chips)]
            for cp in first:
                cp.start()
            sends += first
        for i in range(n):
            for j, chip in enumerate(chips):
                copy(i, 1 + j, (*chip, c), (x, y, c)).wait_recv()
                onward = copy(i, 4 + j, (*chip, c), sibling)
                onward.start()
                sends.append(onward)
        for i in range(n):
            copy(i, 0, sibling, (x, y, c)).wait_recv()
            for j, chip in enumerate(chips):
                copy(i, 4 + j, (*chip, 1 - c), (x, y, c)).wait_recv()
        for cp in sends:
            cp.wait_send()
        for cp in local:
            cp.wait()

    hbm = pl.BlockSpec(memory_space=pltpu.HBM)
    return pl.pallas_call(
        body,
        in_specs=[hbm] * n,
        out_specs=[hbm] * n,
        out_shape=_exchange_out_shapes(arrays, False),
        scratch_shapes=_exchange_scratch(n),
        name=name,
    )(*arrays)


def _exchange(arrays, *, scatter, name):
    n = len(arrays)

    def body(*refs):
        copies = _exchange_copies(refs[:n], refs[n:2 * n], *refs[2 * n:], scatter)
        _exchange_start(copies)
        _exchange_wait(copies)

    hbm = pl.BlockSpec(memory_space=pltpu.HBM)
    return pl.pallas_call(
        body,
        in_specs=[hbm] * n,
        out_specs=[hbm] * n,
        out_shape=_exchange_out_shapes(arrays, scatter),
        scratch_shapes=_exchange_scratch(n),
        name=name,
    )(*arrays)


def _adamw(parts, w, m, v, name):
    rows, cols = w.shape
    tr = _pick(rows, (256, 128, 64, 8))
    c1 = 1.0 / (1.0 - ADAM_B1 ** ADAM_STEP)
    c2 = 1.0 / (1.0 - ADAM_B2 ** ADAM_STEP)

    def body(p_ref, w_ref, m_ref, v_ref, g_ref, d_ref, mo_ref, vo_ref):
        g = p_ref[0].astype(F32)
        for s in range(1, N_DEV):
            g = g + p_ref[s].astype(F32)
        mn = ADAM_B1 * m_ref[...] + (1.0 - ADAM_B1) * g
        vn = ADAM_B2 * v_ref[...] + (1.0 - ADAM_B2) * (g * g)
        g_ref[...] = g
        mo_ref[...] = mn
        vo_ref[...] = vn
        d_ref[...] = -ADAM_LR * ((mn * c1) / (jnp.sqrt(vn * c2) + ADAM_EPS) + ADAM_WD * w_ref[...])

    spec = pl.BlockSpec((tr, cols), lambda i: (i, 0))
    return pl.pallas_call(
        body,
        grid=(rows // tr,),
        in_specs=[pl.BlockSpec((N_DEV, tr, cols), lambda i: (0, i, 0)), spec, spec, spec],
        out_specs=[spec] * 4,
        out_shape=[jax.ShapeDtypeStruct((rows, cols), F32)] * 4,
        compiler_params=_cparams(("parallel",), 48),
        name=name,
    )(parts, w, m, v)


_VECTOR_PARAMS = ("attn_norm_g", "attn_sinks", "rwkv_mu", "w0", "a0", "k_k", "k_a", "r_k", "ln_x_w", "ln_x_b",
                  "mlp_norm_g", "final_norm_g")
_WEIGHT_NAMES = ("attn_norm_g", "w_in", "attn_sinks", "rwkv_mu", "w0", "w2", "a0", "a2", "g2", "k_k", "k_a", "r_k",
                 "ln_x_w", "ln_x_b", "w_out", "mlp_norm_g", "w_up", "w_down", "final_norm_g")


def _pack_vectors(vals):
    pieces = []
    for name in _VECTOR_PARAMS:
        flat = vals[name].reshape(1, -1)
        pad = (-flat.shape[1]) % LANES
        pieces.append(jnp.pad(flat, ((0, 0), (0, pad))) if pad else flat)
    return jnp.concatenate(pieces, axis=1)


def _unpack_vectors(packed, like):
    out, col = {}, 0
    for name in _VECTOR_PARAMS:
        size = like[name].size
        out[name] = packed[0, col:col + size].reshape(like[name].shape)
        col += size + (-size) % LANES
    return out


def kernel(x, attn_norm_g, w_in, attn_sinks, rwkv_mu, w0, w2, a0, a2, g2, k_k, k_a, r_k, ln_x_w, ln_x_b, w_out, mlp_norm_g, w_up, w_down, final_norm_g, loss_target, m_attn_norm_g, m_w_in, m_attn_sinks, m_rwkv_mu, m_w0, m_w2, m_a0, m_a2, m_g2, m_k_k, m_k_a, m_r_k, m_ln_x_w, m_ln_x_b, m_w_out, m_mlp_norm_g, m_w_up, m_w_down, m_final_norm_g, v_attn_norm_g, v_w_in, v_attn_sinks, v_rwkv_mu, v_w0, v_w2, v_a0, v_a2, v_g2, v_k_k, v_k_a, v_r_k, v_ln_x_w, v_ln_x_b, v_w_out, v_mlp_norm_g, v_w_up, v_w_down, v_final_norm_g):
    weights = dict(attn_norm_g=attn_norm_g, w_in=w_in, attn_sinks=attn_sinks, rwkv_mu=rwkv_mu, w0=w0, w2=w2, a0=a0,
                   a2=a2, g2=g2, k_k=k_k, k_a=k_a, r_k=r_k, ln_x_w=ln_x_w, ln_x_b=ln_x_b, w_out=w_out,
                   mlp_norm_g=mlp_norm_g, w_up=w_up, w_down=w_down, final_norm_g=final_norm_g)
    mom1 = dict(attn_norm_g=m_attn_norm_g, w_in=m_w_in, attn_sinks=m_attn_sinks, rwkv_mu=m_rwkv_mu, w0=m_w0, w2=m_w2,
                a0=m_a0, a2=m_a2, g2=m_g2, k_k=m_k_k, k_a=m_k_a, r_k=m_r_k, ln_x_w=m_ln_x_w, ln_x_b=m_ln_x_b,
                w_out=m_w_out, mlp_norm_g=m_mlp_norm_g, w_up=m_w_up, w_down=m_w_down, final_norm_g=m_final_norm_g)
    mom2 = dict(attn_norm_g=v_attn_norm_g, w_in=v_w_in, attn_sinks=v_attn_sinks, rwkv_mu=v_rwkv_mu, w0=v_w0, w2=v_w2,
                a0=v_a0, a2=v_a2, g2=v_g2, k_k=v_k_k, k_a=v_k_a, r_k=v_r_k, ln_x_w=v_ln_x_w, ln_x_b=v_ln_x_b,
                w_out=v_w_out, mlp_norm_g=v_mlp_norm_g, w_up=v_w_up, w_down=v_w_down, final_norm_g=v_final_norm_g)
    bsz, seq, d_model = x.shape
    rows = bsz * seq
    d_in = N_DEV * w_in.shape[2]
    d_ff = N_DEV * w_up.shape[2]

    gathered = _gather_two_level([w_in[0].astype(BF16), w2[0], a2[0], g2[0]], name="gather_in_weights")
    cols_first = lambda a: a.transpose(1, 0, 2).reshape(a.shape[1], -1)
    w_in_f = cols_first(gathered[0])
    w_attn, w_rw = w_in_f[:, :QKV_WIDTH], w_in_f[:, QKV_WIDTH:]
    w2_f, a2_f, g2_f = cols_first(gathered[1]), cols_first(gathered[2]), cols_first(gathered[3])
    lora = w2_f.shape[0]
    w2p = jnp.concatenate([w2_f, jnp.zeros_like(a2_f)], axis=0)
    a2p = jnp.concatenate([jnp.zeros_like(w2_f), a2_f], axis=0)

    esum = _head_sum_matrix()
    sink_rows = jnp.broadcast_to(attn_sinks.reshape(N_ATTN_HEADS, 1), (N_ATTN_HEADS, LANES))
    prep_pars = [w0, w2p, a0, a2p, g2_f, k_k, k_a, esum]
    post_pars = [ln_x_w, ln_x_b, r_k, esum]

    x2d = x.reshape(rows, d_model)
    h1, z_attn, z_rw, r, lw, k, v, kk, al, gate = _norm_in_proj_prep(x2d, seq, attn_norm_g, w_attn, w_rw, rwkv_mu,
                                                                       prep_pars)
    z_attn3 = z_attn.reshape(bsz, seq, QKV_WIDTH)
    attn_out = _attn_fwd(z_attn3, sink_rows)
    as3 = lambda a: a.reshape(bsz, seq, RWKV_WIDTH)
    y, from_fwd, late = _rwkv_fwd(as3(r), as3(lw), as3(k), as3(v), as3(kk), as3(al),
                               hosted=[w_out[0].astype(BF16), w_up[0].astype(BF16), w_down[0].astype(BF16)])
    w_out_f = late[0].reshape(-1, d_model)
    w_up_f = cols_first(late[1])
    w_down_f = late[2].reshape(-1, d_model)
    y2 = y.reshape(rows, RWKV_WIDTH)
    attn_out2d = attn_out.reshape(rows, ATTN_WIDTH)
    w_out_attn, w_out_rw = w_out_f[:ATTN_WIDTH], w_out_f[ATTN_WIDTH:]
    rw_out, x1, h2 = _post_out_proj_norm(y2, r, k, v, gate, attn_out2d, x2d, post_pars, w_out_attn, w_out_rw,
                                         mlp_norm_g)

    def relu_sq(acc):
        pos = jnp.maximum(acc, 0.0)
        return acc, pos * pos

    u, act = _matmul(h2, w_up_f, "nn", name="mlp_up", epilogue=relu_sq, out_dtypes=(BF16, BF16))
    dx2, loss_vec, g_final = _down_proj_loss(act, w_down_f, x1, loss_target.reshape(rows, d_model),
                                             final_norm_g.reshape(1, d_model))

    g_w_down = _matmul(act, dx2, "tn", name="grad_w_down", out_dtypes=(BF16,))
    du = _matmul(dx2, w_down_f, "nt", name="mlp_down_bwd", extras=(u,), out_dtypes=(BF16,),
                 epilogue=lambda acc, uv: (acc * (2.0 * jnp.maximum(uv.astype(F32), 0.0)),))
    g_w_up = _matmul(h2, du, "tn", name="grad_w_up", out_dtypes=(BF16,))
    dx1, g_mlp_norm = _proj_bwd_norm_bwd([du], [w_up_f], x1, dx2, mlp_norm_g, "mlp_up_bwd_norm_bwd")
    g_w_out = jnp.concatenate([_matmul(attn_out2d, dx1, "tn", name="grad_w_out_attn", out_dtypes=(BF16,)),
                               _matmul(rw_out, dx1, "tn", name="grad_w_out_rwkv", out_dtypes=(BF16,))], axis=0)
    d_attn_out, dy, dr_a, dk_a, dv_a, dgate, g_ln_w, g_ln_b, g_r_k = _out_proj_bwd_post_bwd(
        dx1, y2, r, k, v, gate, post_pars, w_out_attn, w_out_rw)
    by_cols = lambda a: a.reshape(a.shape[0], N_DEV, -1).transpose(1, 0, 2)
    (dr_b, dlw, dk_b, dv_b, dkk, dal), (p_w_out, p_w_up, p_w_down) = _rwkv_bwd(
        as3(r), as3(lw), as3(k), as3(v), as3(kk), as3(al), from_fwd, as3(dy),
        hosted=[g_w_out.reshape(N_DEV, -1, d_model), by_cols(g_w_up), g_w_down.reshape(N_DEV, -1, d_model)],
        scatter=True)
    flat = lambda a: a.reshape(rows, RWKV_WIDTH)
    (dz_rw, gmu_r, gmu_k, gmu_v, gmu_wa, gmu_g, g_w0, g_w2p, g_a0, g_a2p, g_g2, g_k_k, g_k_a) = _prep_bwd(
        z_rw, seq, [dr_a, flat(dr_b), flat(dlw), dk_a, flat(dk_b), dv_a, flat(dv_b), flat(dkk), flat(dal), dgate],
        rwkv_mu, prep_pars)
    dz_attn, g_sink_rows = _attn_bwd(z_attn3, d_attn_out.reshape(bsz, seq, ATTN_WIDTH), sink_rows)
    dz_attn = dz_attn.reshape(rows, QKV_WIDTH)
    g_w_in = jnp.concatenate([_matmul(h1, dz_attn, "tn", name="grad_w_in_attn", out_dtypes=(BF16,)),
                              _matmul(h1, dz_rw, "tn", name="grad_w_in_rwkv", out_dtypes=(BF16,))], axis=1)
    lora_grads = jnp.concatenate([g_w2p[:lora], g_a2p[lora:], g_g2], axis=0)
    (dx, g_attn_norm), (p_w_in, p_lora) = _proj_bwd_norm_bwd(
        [dz_attn, dz_rw], [w_attn, w_rw], x2d, dx1, attn_norm_g, "in_proj_bwd_norm_bwd",
        hosted=[by_cols(g_w_in), by_cols(lora_grads)], scatter=True)

    vec_grads = dict(attn_norm_g=g_attn_norm, attn_sinks=g_sink_rows[:, 0], rwkv_mu=jnp.concatenate(
        [gmu_r, gmu_k, gmu_v, gmu_wa, gmu_g], axis=1), w0=g_w0, a0=g_a0, k_k=g_k_k, k_a=g_k_a, r_k=g_r_k,
        ln_x_w=g_ln_w, ln_x_b=g_ln_b, mlp_norm_g=g_mlp_norm, final_norm_g=g_final)
    packed = _pack_vectors(vec_grads)
    nvec = packed.shape[1]
    everyone = _exchange([jnp.concatenate([packed, loss_vec], axis=1)], scatter=False, name="gather_vector_grads")[0]
    vec_parts = everyone[:, :, :nvec]
    loss = jnp.sum(everyone[:, 0, nvec])

    grads, delta, new_m, new_v = {}, {}, {}, {}

    def update(name, part, shape2d):
        res = _adamw(part, weights[name].reshape(shape2d), mom1[name].reshape(shape2d), mom2[name].reshape(shape2d),
                     "adamw_" + name)
        for store, val in zip((grads, delta, new_m, new_v), res):
            store[name] = val.reshape(weights[name].shape)

    update("w_in", p_w_in, w_in.shape[1:])
    update("w_out", p_w_out, w_out.shape[1:])
    update("w_up", p_w_up, w_up.shape[1:])
    update("w_down", p_w_down, w_down.shape[1:])
    stack = lambda d: jnp.concatenate([d["w2"][0], d["a2"][0], d["g2"][0]], axis=0)
    lora_res = _adamw(p_lora, stack(weights), stack(mom1), stack(mom2), "adamw_lora")
    for store, val in zip((grads, delta, new_m, new_v), lora_res):
        store["w2"], store["a2"], store["g2"] = val[None, :lora], val[None, lora:2 * lora], val[None, 2 * lora:]
    vec_res = _adamw(vec_parts, _pack_vectors(weights), _pack_vectors(mom1), _pack_vectors(mom2), "adamw_vectors")
    for store, val in zip((grads, delta, new_m, new_v), vec_res):
        store.update(_unpack_vectors(val, weights))

    return (loss, dx.reshape(x.shape), *[grads[n] for n in _WEIGHT_NAMES], *[delta[n] for n in _WEIGHT_NAMES],
            *[new_m[n] for n in _WEIGHT_NAMES], *[new_v[n] for n in _WEIGHT_NAMES])
```

```python
import functools
import math

import jax
import jax.numpy as jnp
from jax import lax
from jax.experimental import pallas as pl
from jax.experimental.pallas import tpu as pltpu

F32 = jnp.float32
BF16 = jnp.bfloat16

N_DEV = 8
HEAD_DIM = 64
LANES = 128
N_ATTN_HEADS = 8
ATTN_WIDTH = 512
KV_WIDTH = 128
QKV_WIDTH = ATTN_WIDTH + 2 * KV_WIDTH
RWKV_WIDTH = 512
LORA_WA = 128
GATE_LORA = 128
RWKV_SHIFT_WIDTH = 3 * RWKV_WIDTH + LORA_WA + GATE_LORA
BLOCK = 128
CHUNK = 64
RMS_EPS = 1e-6
GN_EPS = 64e-5
L2_EPS = 1e-12
NEG_INF = -1e30
DECAY_SCALE = math.exp(-0.5)
ADAM_LR, ADAM_B1, ADAM_B2, ADAM_EPS, ADAM_WD, ADAM_STEP = 0.001, 0.9, 0.999, 1e-08, 0.01, 10

NN = (((1,), (0,)), ((), ()))
NT = (((1,), (1,)), ((), ()))
TN = (((0,), (0,)), ((), ()))
MESH = pl.DeviceIdType.MESH


def _dot(a, b, dn=NN, precision=None):
    return lax.dot_general(a, b, dn, precision=precision, preferred_element_type=F32)


def _bdot_raw(a, b, dn):
    return lax.dot_general(a.astype(BF16), b.astype(BF16), dn, preferred_element_type=F32)


@functools.partial(jax.custom_vjp, nondiff_argnums=(2, 3))
def _bdot_c(a, b, ca, cb):
    return _bdot_raw(a, b, (((ca,), (cb,)), ((), ())))


def _bdot_c_fwd(a, b, ca, cb):
    return _bdot_c(a, b, ca, cb), (a, b)


def _bdot_c_bwd(ca, cb, res, ct):
    a, b = res
    fa, fb = 1 - ca, 1 - cb
    da = _bdot_raw(ct, b, (((1,), (fb,)), ((), ()))) if ca == 1 else _bdot_raw(b, ct, (((fb,), (1,)), ((), ())))
    db = _bdot_raw(a, ct, (((fa,), (0,)), ((), ()))) if cb == 0 else _bdot_raw(ct, a, (((0,), (fa,)), ((), ())))
    return da, db


_bdot_c.defvjp(_bdot_c_fwd, _bdot_c_bwd)


def _bdot(a, b, dn=NN):
    return _bdot_c(a, b, dn[0][0][0], dn[0][1][0])


def _bdot_nn(a, b):
    return _bdot(a, b, NN)


def _bdot_nt(a, b):
    return _bdot(a, b, NT)


def _split3(x):
    hi = x.astype(BF16)
    rest = x - hi.astype(F32)
    mid = rest.astype(BF16)
    return hi, mid, (rest - mid.astype(F32)).astype(BF16)


def _running_sum(x, dn):
    c = x.shape[0]
    row = lax.broadcasted_iota(jnp.int32, (c, c), 0)
    col = lax.broadcasted_iota(jnp.int32, (c, c), 1)
    tri = jnp.where(row >= col, 1.0, 0.0).astype(BF16)
    w = x.shape[1]
    parts = lax.dot_general(tri, jnp.concatenate(_split3(x), axis=1), dn, preferred_element_type=F32)
    return parts[:, :w] + parts[:, w:2 * w] + parts[:, 2 * w:]


@jax.custom_vjp
def _cumsum_rows(x):
    return _running_sum(x, NN)


_cumsum_rows.defvjp(lambda x: (_running_sum(x, NN), None), lambda _, ct: (_running_sum(ct, TN),))


@jax.custom_vjp
def _fold_rows(x):
    c = x.shape[0] // 2
    return x[:c] + x[c:]


_fold_rows.defvjp(lambda x: (_fold_rows(x), None), lambda _, ct: (jnp.concatenate([ct, ct], axis=0),))


@jax.custom_vjp
def _halves(x):
    n = x.shape[0] // 2
    return x[:n], x[n:]


_halves.defvjp(lambda x: (_halves(x), None), lambda _, cts: (jnp.concatenate(cts, axis=0),))


@jax.custom_vjp
def _quarters(x):
    n = x.shape[0] // 2
    return x[:n, :n], x[:n, n:], x[n:, :n], x[n:, n:]


_quarters.defvjp(lambda x: (_quarters(x), None),
                 lambda _, cts: (jnp.concatenate([jnp.concatenate(cts[:2], axis=1),
                                                  jnp.concatenate(cts[2:], axis=1)], axis=0),))


@jax.custom_vjp
def _sigmoid(x):
    return 1.0 / (1.0 + jnp.exp(-x))


def _sigmoid_fwd(x):
    s = _sigmoid(x)
    return s, s


_sigmoid.defvjp(_sigmoid_fwd, lambda s, ct: (ct * s * (1.0 - s),))


def _pick(n, cands):
    for c in cands:
        if n % c == 0:
            return c
    return n


def _cparams(sem, vmem_mb=None):
    kw = dict(dimension_semantics=sem)
    if vmem_mb is not None:
        kw["vmem_limit_bytes"] = vmem_mb * 1024 * 1024
    return pltpu.CompilerParams(**kw)


def _matmul(a, b, mode, *, name, extras=(), epilogue=None, out_dtypes=(F32,), tm=1024, tn=1024, tk=1024,
            hosted=(), scatter=False):
    if mode == "nn":
        (M, K), (_, N) = a.shape, b.shape
    elif mode == "tn":
        (K, M), (_, N) = a.shape, b.shape
    else:
        (M, K), (N, _) = a.shape, b.shape
    tm = _pick(M, (tm, 512, 256, 128))
    tn = _pick(N, (tn, 896, 768, 512, 384, 256, 128))
    tk = _pick(K, (tk, 512, 256, 128))
    nk = K // tk
    ne, nout = len(extras), len(out_dtypes)
    if mode == "nn":
        a_spec = pl.BlockSpec((tm, tk), lambda i, j, k: (i, k))
        b_spec = pl.BlockSpec((tk, tn), lambda i, j, k: (k, j))
        dn = NN
    elif mode == "tn":
        a_spec = pl.BlockSpec((tk, tm), lambda i, j, k: (k, i))
        b_spec = pl.BlockSpec((tk, tn), lambda i, j, k: (k, j))
        dn = TN
    else:
        a_spec = pl.BlockSpec((tm, tk), lambda i, j, k: (i, k))
        b_spec = pl.BlockSpec((tn, tk), lambda i, j, k: (j, k))
        dn = NT
    o_spec = pl.BlockSpec((tm, tn), lambda i, j, k: (i, j))
    grid = (M // tm, N // tn, nk)
    nex = len(hosted)

    def body(*refs):
        a_ref, b_ref = refs[:2]
        e_refs = refs[2:2 + ne]
        ex_in = refs[2 + ne:2 + ne + nex]
        o_refs = refs[2 + ne + nex:2 + ne + nex + nout]
        ex_out = refs[2 + ne + nex + nout:2 + ne + 2 * nex + nout]
        scratch = refs[2 + ne + 2 * nex + nout:]
        kstep = pl.program_id(2)
        if nex:
            at = [pl.program_id(d) for d in range(3)]
            first = jnp.logical_and(jnp.logical_and(at[0] == 0, at[1] == 0), at[2] == 0)
            last = jnp.logical_and(jnp.logical_and(at[0] == grid[0] - 1, at[1] == grid[1] - 1), at[2] == grid[2] - 1)
            _hosted_exchange(first, last, ex_in, ex_out, scratch[-3:], scatter)

        def finish(total):
            outs = (total,) if epilogue is None else epilogue(total, *[e[...] for e in e_refs])
            for o_ref, o in zip(o_refs, outs):
                o_ref[...] = o.astype(o_ref.dtype)

        if nk == 1:
            finish(_bdot_raw(a_ref[...], b_ref[...], dn))
            return
        acc = scratch[0]

        @pl.when(kstep == 0)
        def _():
            acc[...] = jnp.zeros_like(acc)

        acc[...] += _bdot_raw(a_ref[...], b_ref[...], dn)

        @pl.when(kstep == nk - 1)
        def _():
            finish(acc[...])

    hbm = pl.BlockSpec(memory_space=pltpu.HBM)
    outs = pl.pallas_call(
        body,
        grid=grid,
        in_specs=[a_spec, b_spec] + [o_spec] * ne + [hbm] * nex,
        out_specs=[o_spec] * nout + [hbm] * nex,
        out_shape=[jax.ShapeDtypeStruct((M, N), dt) for dt in out_dtypes] + _exchange_out_shapes(hosted, scatter),
        scratch_shapes=([pltpu.VMEM((tm, tn), F32)] if nk > 1 else []) + (_exchange_scratch(nex) if nex else []),
        compiler_params=_cparams(("arbitrary",) * 3 if nex else ("parallel", "parallel", "arbitrary"), 56),
        name=name,
    )(a, b, *extras, *hosted)
    if nex:
        return outs[:nout], outs[nout:]
    return outs[0] if nout == 1 else outs


def _rowwise(fn, rows, pars, out_rows, out_accs, *, tile, name, nsub=1, hosted=(), scatter=False):
    rows = [r if isinstance(r, tuple) else (r, r.shape[1], 0) for r in rows]
    R = rows[0][0].shape[0]
    tile = min(tile, R)
    chunk = tile // nsub
    ntile = R // tile
    nr, npar, nor, noa, nex = len(rows), len(pars), len(out_rows), len(out_accs), len(hosted)

    def body(*refs):
        rin = refs[:nr]
        pin = refs[nr:nr + npar]
        ex_in = refs[nr + npar:nr + npar + nex]
        orow = refs[nr + npar + nex:nr + npar + nex + nor]
        oacc = refs[nr + npar + nex + nor:nr + npar + nex + nor + noa]
        ex_out = refs[nr + npar + nex + nor + noa:nr + npar + 2 * nex + nor + noa]
        step = pl.program_id(0)
        _hosted_exchange(step == 0, step == ntile - 1, ex_in, ex_out, refs[nr + npar + 2 * nex + nor + noa:], scatter)
        pvals = [p[...] for p in pin]
        totals = []
        for sub in range(nsub):
            at = slice(sub * chunk, (sub + 1) * chunk)
            outs = fn(*[r[at, :] for r in rin], *pvals)
            for ref, o in zip(orow, outs[:nor]):
                if isinstance(o, (tuple, list)):
                    col = 0
                    for piece in o:
                        ref[at, col:col + piece.shape[1]] = piece.astype(ref.dtype)
                        col += piece.shape[1]
                else:
                    ref[at, :] = o.astype(ref.dtype)
            accs = list(outs[nor:])
            totals = accs if sub == 0 else [t + a for t, a in zip(totals, accs)]

        def accumulate(ref, o):
            @pl.when(step == 0)
            def _():
                ref[...] = o

            @pl.when(step > 0)
            def _():
                ref[...] += o

        for ref, o in zip(oacc, totals):
            accumulate(ref, o)

    def colspec(width, cb):
        return pl.BlockSpec((tile, width), lambda i: (i, cb))

    hbm = pl.BlockSpec(memory_space=pltpu.HBM)
    outs = pl.pallas_call(
        body,
        grid=(ntile,),
        in_specs=[colspec(w, cb) for (_, w, cb) in rows]
        + [pl.BlockSpec(p.shape, lambda i: (0, 0), pipeline_mode=pl.Buffered(1)) for p in pars] + [hbm] * nex,
        out_specs=[colspec(w, 0) for (w, _) in out_rows]
        + [pl.BlockSpec(s, lambda i: (0, 0)) for s in out_accs] + [hbm] * nex,
        out_shape=[jax.ShapeDtypeStruct((R, w), dt) for (w, dt) in out_rows]
        + [jax.ShapeDtypeStruct(s, F32) for s in out_accs] + _exchange_out_shapes(hosted, scatter),
        scratch_shapes=_exchange_scratch(nex) if nex else [],
        compiler_params=_cparams(("arbitrary",), 56),
        name=name,
    )(*[r[0] for r in rows], *pars, *hosted)
    return (outs[:nor + noa], outs[nor + noa:]) if nex else outs


def _rms_fn(x, g):
    return x * lax.rsqrt(jnp.mean(x * x, axis=-1, keepdims=True) + RMS_EPS) * g


FUSED_TILE = 512
FUSED_CHUNKS = 2


def _down_proj_loss(act, w_down, x1, tgt, g):
    d = x1.shape[1]

    def fn(av, xv, tv, wv, gv):
        x2 = xv + _bdot_raw(av, wv, NN)
        y, vjp = jax.vjp(_rms_fn, x2, gv)
        err = y - tv
        loss = 0.5 * jnp.sum(jnp.sum(err * err, axis=-1, keepdims=True), axis=0, keepdims=True) / d
        dx, dg = vjp(err / d)
        return dx, jnp.broadcast_to(loss, (1, LANES)), dg

    return _rowwise(fn, [act, x1, tgt], [w_down, g], [(d, F32)], [(1, LANES), g.shape],
                    tile=FUSED_TILE, nsub=FUSED_CHUNKS, name="mlp_down_final_norm_loss")


def _proj_bwd_norm_bwd(cts, weights_t, x, dres, g, name, hosted=(), scatter=False):
    n = len(cts)

    def fn(*vals):
        ctv, (xv, dresv), wv, gv = vals[:n], vals[n:n + 2], vals[n + 2:2 * n + 2], vals[-1]
        dh = _bdot_raw(ctv[0], wv[0], NT)
        for c, w in zip(ctv[1:], wv[1:]):
            dh = dh + _bdot_raw(c, w, NT)
        _, vjp = jax.vjp(_rms_fn, xv, gv)
        dx, dg = vjp(dh)
        return dx + dresv, dg

    return _rowwise(fn, [*cts, x, dres], [*weights_t, g], [(x.shape[1], F32)], [g.shape],
                    tile=FUSED_TILE, nsub=FUSED_CHUNKS, name=name, hosted=hosted, scatter=scatter)


def _head_sum_matrix():
    i = lax.broadcasted_iota(jnp.int32, (RWKV_WIDTH, RWKV_WIDTH), 0) // HEAD_DIM
    j = lax.broadcasted_iota(jnp.int32, (RWKV_WIDTH, RWKV_WIDTH), 1) // HEAD_DIM
    return (i == j).astype(BF16)


def _head_sums_raw(x, esum):
    hi = x.astype(BF16)
    lo = (x - hi.astype(F32)).astype(BF16)
    return _dot(hi, esum) + _dot(lo, esum)


@jax.custom_vjp
def _head_sums(x, esum):
    return _head_sums_raw(x, esum)


_head_sums.defvjp(lambda x, esum: (_head_sums_raw(x, esum), esum),
                  lambda esum, ct: (_head_sums_raw(ct, esum), jnp.zeros_like(esum)))


def _prep_core(xr, xk, xv, xwa, xg, w0, w2p, a0, a2p, g2, k_k, k_a, esum):
    lw = -DECAY_SCALE * _sigmoid(w0 + _bdot_nn(jnp.tanh(xwa), w2p))
    a = _sigmoid(a0 + _bdot_nn(xwa, a2p))
    g = _bdot_nn(_sigmoid(xg), g2)
    kk0 = xk * k_k
    kk = kk0 * jnp.minimum(lax.rsqrt(_head_sums(kk0 * kk0, esum)), 1.0 / L2_EPS)
    k = xk * (1.0 + (a - 1.0) * k_a)
    return xr, lw, k, xv, kk, a, g


_SEGS = ((0, 512), (512, 1024), (1024, 1536), (1536, 1664), (1664, 1792))


PREP_TILE = 256
SUBLANES = 8


def _shifted_tokens(z_ref, zprev_ref, tile_index, seq):
    zc = z_ref[...]
    start = (tile_index * PREP_TILE) % seq == 0
    before = jnp.where(start, 0.0, zprev_ref[SUBLANES - 1:SUBLANES, :])
    rowid = lax.broadcasted_iota(jnp.int32, zc.shape, 0)
    return zc, jnp.where(rowid == 0, before, pltpu.roll(zc, 1, 0))


def _prep_specs(z, mu, pars, index):
    width = z.shape[1]
    per = PREP_TILE // SUBLANES
    return ([pl.BlockSpec((PREP_TILE, width), lambda i: (index(i), 0)),
             pl.BlockSpec((SUBLANES, width), lambda i: (jnp.maximum(index(i) * per - 1, 0), 0))],
            [pl.BlockSpec(p.shape, lambda i: (0, 0)) for p in (mu, *pars)])


def _norm_in_proj_prep(x, seq, g, w_attn, w_rw, mu, pars):
    rows, d = x.shape
    chunk = FUSED_TILE // FUSED_CHUNKS
    npar = len(pars)
    wa_width, wr_width = w_attn.shape[1], w_rw.shape[1]

    def body(x_ref, g_ref, wa_ref, wr_ref, mu_ref, *rest):
        par_refs = rest[:npar]
        h_ref, za_ref, zr_ref = rest[npar:npar + 3]
        out_refs, carry = rest[npar + 3:-1], rest[-1]
        step = pl.program_id(0)

        @pl.when(step == 0)
        def _():
            carry[...] = jnp.zeros_like(carry)

        pv = [p[...] for p in par_refs]
        for sub in range(FUSED_CHUNKS):
            at = slice(sub * chunk, (sub + 1) * chunk)
            h = _rms_fn(x_ref[at, :], g_ref[...])
            h_ref[at, :] = h.astype(h_ref.dtype)
            za_ref[at, :] = _bdot_raw(h, wa_ref[...], NN)
            zc = _bdot_raw(h, wr_ref[...], NN)
            zr_ref[at, :] = zc
            start = (step * FUSED_TILE + sub * chunk) % seq == 0
            before = jnp.where(start, 0.0, carry[SUBLANES - 1:SUBLANES, :])
            rowid = lax.broadcasted_iota(jnp.int32, zc.shape, 0)
            zp = jnp.where(rowid == 0, before, pltpu.roll(zc, 1, 0))
            carry[...] = zc[chunk - SUBLANES:chunk, :]
            zs = zc + (zp - zc) * mu_ref[...]
            outs = _prep_core(*[zs[:, a:b] for a, b in _SEGS], *pv)
            for ref, o in zip(out_refs, outs):
                ref[at, :] = o

    tiled = lambda width: pl.BlockSpec((FUSED_TILE, width), lambda i: (i, 0))
    resident = lambda a: pl.BlockSpec(a.shape, lambda i: (0, 0), pipeline_mode=pl.Buffered(1))
    return pl.pallas_call(
        body,
        grid=(rows // FUSED_TILE,),
        in_specs=[tiled(d)] + [resident(a) for a in (g, w_attn, w_rw, mu, *pars)],
        out_specs=[tiled(d), tiled(wa_width), tiled(wr_width)] + [tiled(RWKV_WIDTH)] * 7,
        out_shape=[jax.ShapeDtypeStruct((rows, d), BF16), jax.ShapeDtypeStruct((rows, wa_width), F32),
                   jax.ShapeDtypeStruct((rows, wr_width), F32)] + [jax.ShapeDtypeStruct((rows, RWKV_WIDTH), F32)] * 7,
        scratch_shapes=[pltpu.VMEM((SUBLANES, wr_width), F32)],
        compiler_params=_cparams(("arbitrary",), 56),
        name="attn_norm_in_proj_rwkv_prep",
    )(x, g, w_attn, w_rw, mu, *pars)


def _prep_bwd(z, seq, cts, mu, pars):
    rows, width = z.shape
    ntile = rows // PREP_TILE
    npar, nct = len(pars), len(cts)
    acc_shapes = [(1, b - a) for a, b in _SEGS] + [p.shape for p in pars[:-1]]

    def body(z_ref, zprev_ref, *rest):
        ct_refs = rest[:nct]
        mu_ref = rest[nct]
        par_refs = rest[nct + 1:nct + 1 + npar]
        dz_ref = rest[nct + 1 + npar]
        acc_refs = rest[nct + 2 + npar:-1]
        carry = rest[-1]
        step = pl.program_id(0)
        tile_index = ntile - 1 - step

        @pl.when(step == 0)
        def _():
            carry[...] = jnp.zeros_like(carry)

        zc, zp = _shifted_tokens(z_ref, zprev_ref, tile_index, seq)
        mu_v = mu_ref[...]
        diff = zp - zc
        zs = zc + diff * mu_v
        dra, drb, dlw, dka, dkb, dva, dvb, dkk, da, dg = [c[...] for c in ct_refs]
        pv = [p[...] for p in par_refs]
        _, vjp = jax.vjp(lambda *args: _prep_core(*args, pv[-1]), *[zs[:, a:b] for a, b in _SEGS], *pv[:-1])
        grads = vjp((dra + drb, dlw, dka + dkb, dva + dvb, dkk, da, dg))
        dsegs, dpars = grads[:5], grads[5:]
        last_of_sequence = ((tile_index + 1) * PREP_TILE) % seq == 0
        accs = []
        for ds, (a, b) in zip(dsegs, _SEGS):
            mu_s = mu_v[:, a:b]
            dzp = ds * mu_s
            after = jnp.where(last_of_sequence, 0.0, carry[0:1, a:b])
            rowid = lax.broadcasted_iota(jnp.int32, dzp.shape, 0)
            from_next = jnp.where(rowid == PREP_TILE - 1, after, pltpu.roll(dzp, PREP_TILE - 1, 0))
            dz_ref[:, a:b] = (ds * (1.0 - mu_s) + from_next).astype(dz_ref.dtype)
            carry[:, a:b] = dzp[0:SUBLANES, :]
            accs.append(jnp.sum(ds * diff[:, a:b], axis=0, keepdims=True))
        accs.extend(dpars)

        def accumulate(ref, o):
            @pl.when(step == 0)
            def _():
                ref[...] = o

            @pl.when(step > 0)
            def _():
                ref[...] += o

        for ref, o in zip(acc_refs, accs):
            accumulate(ref, o)

    rev = lambda i: ntile - 1 - i
    zspecs, pspecs = _prep_specs(z, mu, pars, rev)
    return pl.pallas_call(
        body,
        grid=(ntile,),
        in_specs=zspecs + [pl.BlockSpec((PREP_TILE, RWKV_WIDTH), lambda i: (rev(i), 0))] * nct + pspecs,
        out_specs=[pl.BlockSpec((PREP_TILE, width), lambda i: (rev(i), 0))]
        + [pl.BlockSpec(s, lambda i: (0, 0)) for s in acc_shapes],
        out_shape=[jax.ShapeDtypeStruct((rows, width), BF16)] + [jax.ShapeDtypeStruct(s, F32) for s in acc_shapes],
        scratch_shapes=[pltpu.VMEM((SUBLANES, width), F32)],
        compiler_params=_cparams(("arbitrary",), 56),
        name="rwkv_prep_bwd",
    )(z, z, *cts, mu, *pars)


def _post_fn(y, r, k, v, g, ln_w, ln_b, r_k, esum):
    mean = _head_sums(y, esum) * (1.0 / HEAD_DIM)
    yc = y - mean
    var = _head_sums(yc * yc, esum) * (1.0 / HEAD_DIM)
    yn = yc * lax.rsqrt(var + GN_EPS) * ln_w + ln_b
    bonus = _head_sums(r * k * r_k, esum) * v
    return (yn + bonus) * g


def _post_out_proj_norm(y, r, k, v, g, attn_out, x, pars, w_attn_rows, w_rwkv_rows, g_norm):
    npar = len(pars)

    def fn(yv, rv, kv, vv, gv, av, xv, *rest):
        wa, wr, gn = rest[npar:]
        rw = _post_fn(yv, rv, kv, vv, gv, *rest[:npar])
        x1 = xv + _bdot_raw(av, wa, NN) + _bdot_raw(rw, wr, NN)
        return rw, x1, _rms_fn(x1, gn)

    d = x.shape[1]
    return _rowwise(fn, [y, r, k, v, g, attn_out, x], [*pars, w_attn_rows, w_rwkv_rows, g_norm],
                    [(RWKV_WIDTH, BF16), (d, F32), (d, BF16)], [],
                    tile=FUSED_TILE, nsub=FUSED_CHUNKS, name="rwkv_post_out_proj_mlp_norm")


def _out_proj_bwd_post_bwd(dx1, y, r, k, v, g, pars, w_attn_rows, w_rwkv_rows):
    npar = len(pars)

    def fn(dxv, yv, rv, kv, vv, gv, *rest):
        wa, wr = rest[npar:]
        esum = rest[npar - 1]
        d_attn = _bdot_raw(dxv, wa, NT)
        d_rw = _bdot_raw(dxv, wr, NT)
        _, vjp = jax.vjp(lambda *a: _post_fn(*a, esum), yv, rv, kv, vv, gv, *rest[:npar - 1])
        return (d_attn, *vjp(d_rw))

    return _rowwise(fn, [dx1, y, r, k, v, g], [*pars, w_attn_rows, w_rwkv_rows], [(RWKV_WIDTH, F32)] * 6,
                    [p.shape for p in pars[:-1]], tile=FUSED_TILE, nsub=FUSED_CHUNKS, name="out_proj_bwd_rwkv_post_bwd")


def _tri_inverses(ms):
    n = ms[0].shape[0]
    row = lax.broadcasted_iota(jnp.int32, (n, n), 0)
    col = lax.broadcasted_iota(jnp.int32, (n, n), 1)
    eye = jnp.where(row == col, 1.0, 0.0)
    t_inv = [eye + m for m in ms]
    power = [_bdot_raw(m, m, NN) for m in ms]
    steps = int(math.log2(n // 2)) - 1
    for step in range(steps):
        if step < steps - 1:
            both = [_bdot_raw(jnp.concatenate([p, t], axis=0), p, NN) for p, t in zip(power, t_inv)]
            power = [b[:n] for b in both]
            t_inv = [t + b[n:] for t, b in zip(t_inv, both)]
        else:
            t_inv = [t + _bdot_raw(t, p, NN) for t, p in zip(t_inv, power)]
    return t_inv


@jax.custom_vjp
def _tri_solve(ms, xs):
    return tuple(_bdot_raw(t, x, NN) for t, x in zip(_tri_inverses(ms), xs))


def _tri_solve_fwd(ms, xs):
    t_inv = _tri_inverses(ms)
    us = tuple(_bdot_raw(t, x, NN) for t, x in zip(t_inv, xs))
    return us, (tuple(t_inv), us)


def _tri_solve_bwd(res, dus):
    t_inv, us = res
    dxs = tuple(_bdot_raw(t, du, TN) for t, du in zip(t_inv, dus))
    dms = tuple(_bdot_raw(dx, u, NT) for dx, u in zip(dxs, us))
    return dms, dxs


_tri_solve.defvjp(_tri_solve_fwd, _tri_solve_bwd)


@functools.partial(jax.custom_vjp, nondiff_argnums=(3, 4))
def _kept_bdot_c(a, b, kept, ca, cb):
    return kept.astype(F32)


def _kept_bdot_c_fwd(a, b, kept, ca, cb):
    return kept.astype(F32), (a, b, kept)


def _kept_bdot_c_bwd(ca, cb, res, ct):
    a, b, kept = res
    return (*_bdot_c_bwd(ca, cb, (a, b), ct), jnp.zeros_like(kept))


_kept_bdot_c.defvjp(_kept_bdot_c_fwd, _kept_bdot_c_bwd)


def _kept_bdot(a, b, kept, dn):
    return _kept_bdot_c(a, b, kept, dn[0][0][0], dn[0][1][0])


@jax.custom_vjp
def _kept_tri_solve(ms, xs, t_inv, us):
    return tuple(u.astype(F32) for u in us)


def _kept_tri_solve_fwd(ms, xs, t_inv, us):
    return tuple(u.astype(F32) for u in us), (t_inv, us)


def _kept_tri_solve_bwd(res, dus):
    t_inv, us = res
    dms, dxs = _tri_solve_bwd(res, dus)
    return dms, dxs, tuple(jnp.zeros_like(t) for t in t_inv), tuple(jnp.zeros_like(u) for u in us)


_kept_tri_solve.defvjp(_kept_tri_solve_fwd, _kept_tri_solve_bwd)


def _chunk_fn(ss, rs, lws, ks, vs, kks, als, kept=None):
    c = rs[0].shape[0]
    n = 2 * c
    row = lax.broadcasted_iota(jnp.int32, (n, n), 0)
    col = lax.broadcasted_iota(jnp.int32, (n, n), 1)
    incl = (row % c) >= (col % c)
    strict = (row % c) > (col % c)
    lane = lax.broadcasted_iota(jnp.int32, (1, LANES), 1)
    m_lo = jnp.where(lane < HEAD_DIM, 1.0, 0.0)
    m_hi = 1.0 - m_lo

    def stack(a):
        return jnp.concatenate([a * m_lo, a * m_hi], axis=0)

    cums = [_cumsum_rows(lw) for lw in lws]
    totals = [jnp.sum(lw, axis=0, keepdims=True) for lw in lws]
    bs = [kk * al for kk, al in zip(kks, als)]
    grows = [jnp.exp(-cum) for cum in cums]
    a_s = [stack(-kk * jnp.exp(cum - lw)) for kk, cum, lw in zip(kks, cums, lws)]
    b_s = [stack(b * g) for b, g in zip(bs, grows)]
    k_s = [stack(k * g) for k, g in zip(ks, grows)]
    r_s = [stack(r * jnp.exp(cum)) for r, cum in zip(rs, cums)]
    v_s = [stack(v) for v in vs]
    pair = lambda p, q: jnp.concatenate([p, q], axis=0)
    ar_s = [pair(a, r) for a, r in zip(a_s, r_s)]
    if kept is None:
        products = [_bdot_raw(ar, pair(b, k), NT) for ar, b, k in zip(ar_s, b_s, k_s)]
    else:
        products = [_kept_bdot(ar, pair(b, k), kp, NT) for ar, b, k, kp in zip(ar_s, b_s, k_s, kept[0])]
    blocks = [_quarters(p) for p in products]
    m_ab = [jnp.where(strict, q[0], 0.0) for q in blocks]
    m_ak = [jnp.where(strict, q[1], 0.0) for q in blocks]
    m_rb = [jnp.where(incl, q[2], 0.0) for q in blocks]
    m_rk = [jnp.where(incl, q[3], 0.0) for q in blocks]
    from_state = [_halves(_bdot(ar, s, NT)) for ar, s in zip(ar_s, ss)]
    from_v = [_halves(_bdot(pair(mk, mr), v)) for mk, mr, v in zip(m_ak, m_rk, v_s)]
    x = tuple(fs[0] + fv[0] for fs, fv in zip(from_state, from_v))
    if kept is None:
        t_inv = _tri_inverses(m_ab)
        u = [_bdot_raw(t, xx, NN) for t, xx in zip(t_inv, x)]
    else:
        u = _kept_tri_solve(tuple(m_ab), x, kept[1], kept[2])
    y = [_fold_rows(fs[1] + _bdot(mb, uu) + fv[1]) for fs, mb, uu, fv in zip(from_state, m_rb, u, from_v)]
    tails = [jnp.exp(tot - cum) for tot, cum in zip(totals, cums)]
    s_new = [s * jnp.exp(tot) + _bdot(pair(uu, v), pair(stack(b * tl), stack(k * tl)), TN)
             for s, tot, uu, b, tl, v, k in zip(ss, totals, u, bs, tails, v_s, ks)]
    if kept is None:
        keep = lambda vals: tuple(v.astype(BF16) for v in vals)
        return tuple(y), tuple(s_new), (keep(products), keep(t_inv), keep(u))
    return tuple(y), tuple(s_new)


def _chains(bsz, npair):
    return [(b, p, slice(p * LANES, (p + 1) * LANES)) for b in range(bsz) for p in range(npair)]


def _hosted_exchange(first, last, ex_in, ex_out, sems, scatter):
    if not ex_in:
        return

    @pl.when(first)
    def _():
        _exchange_start(_exchange_copies(ex_in, ex_out, *sems, scatter, arrivals=False))

    @pl.when(last)
    def _():
        _exchange_wait(_exchange_copies(ex_in, ex_out, *sems, scatter, arrivals=True))


def _rwkv_fwd(r, lw, k, v, kk, al, hosted=(), scatter=False):
    bsz, t, w = r.shape
    npair, nchunk = w // LANES, t // CHUNK
    chains = _chains(bsz, npair)
    nex = len(hosted)

    def body(*refs):
        r_ref, lw_ref, k_ref, v_ref, kk_ref, al_ref = refs[:6]
        ex_in = refs[6:6 + nex]
        y_ref, sall_ref, prod_ref, tinv_ref, u_ref = refs[6 + nex:11 + nex]
        ex_out = refs[11 + nex:11 + 2 * nex]
        s_scr = refs[11 + 2 * nex]
        step = pl.program_id(0)

        @pl.when(step == 0)
        def _():
            s_scr[...] = jnp.zeros_like(s_scr)

        _hosted_exchange(step == 0, step == nchunk - 1, ex_in, ex_out, refs[12 + 2 * nex:], scatter)
        ss = tuple(s_scr[i] for i in range(len(chains)))
        for i, s in enumerate(ss):
            sall_ref[0, i] = s
        ys, s_new, kept = _chunk_fn(ss, *[tuple(ref[b, :, cols] for b, _, cols in chains)
                                          for ref in (r_ref, lw_ref, k_ref, v_ref, kk_ref, al_ref)])
        for i, (b, _, cols) in enumerate(chains):
            y_ref[b, :, cols] = ys[i]
            s_scr[i] = s_new[i]
            prod_ref[0, i], tinv_ref[0, i], u_ref[0, i] = kept[0][i], kept[1][i], kept[2][i]

    spec = pl.BlockSpec((bsz, CHUNK, w), lambda c: (0, c, 0))
    hbm = pl.BlockSpec(memory_space=pltpu.HBM)
    per_chunk = lambda n: pl.BlockSpec((1, len(chains), n, n), lambda c: (c, 0, 0, 0))
    kept_shape = lambda n: jax.ShapeDtypeStruct((nchunk, len(chains), n, n), BF16)
    outs = pl.pallas_call(
        body,
        grid=(nchunk,),
        in_specs=[spec] * 6 + [hbm] * nex,
        out_specs=[spec, per_chunk(LANES), per_chunk(4 * CHUNK), per_chunk(2 * CHUNK), per_chunk(2 * CHUNK)]
        + [hbm] * nex,
        out_shape=[jax.ShapeDtypeStruct((bsz, t, w), F32),
                   jax.ShapeDtypeStruct((nchunk, len(chains), LANES, LANES), F32),
                   kept_shape(4 * CHUNK), kept_shape(2 * CHUNK), kept_shape(2 * CHUNK)]
        + _exchange_out_shapes(hosted, scatter),
        scratch_shapes=[pltpu.VMEM((len(chains), LANES, LANES), F32)] + (_exchange_scratch(nex) if nex else []),
        compiler_params=_cparams(("arbitrary",), 48),
        name="rwkv_chunk_fwd",
    )(r, lw, k, v, kk, al, *hosted)
    return outs[0], outs[1:5], outs[5:]


def _rwkv_bwd(r, lw, k, v, kk, al, from_fwd, dy, hosted=(), scatter=False):
    bsz, t, w = r.shape
    npair, nchunk = w // LANES, t // CHUNK
    chains = _chains(bsz, npair)
    nex = len(hosted)

    def body(*refs):
        r_ref, lw_ref, k_ref, v_ref, kk_ref, al_ref, s_ref, prod_ref, tinv_ref, u_ref, dy_ref = refs[:11]
        ex_in = refs[11:11 + nex]
        out_refs = refs[11 + nex:17 + nex]
        ex_out = refs[17 + nex:17 + 2 * nex]
        ds_scr = refs[17 + 2 * nex]
        step = pl.program_id(0)

        @pl.when(step == 0)
        def _():
            ds_scr[...] = jnp.zeros_like(ds_scr)

        _hosted_exchange(step == 0, step == nchunk - 1, ex_in, ex_out, refs[18 + 2 * nex:], scatter)
        ss = tuple(s_ref[0, i] for i in range(len(chains)))
        kept = tuple(tuple(ref[0, i] for i in range(len(chains))) for ref in (prod_ref, tinv_ref, u_ref))
        _, vjp = jax.vjp(functools.partial(_chunk_fn, kept=kept), ss,
                         *[tuple(ref[b, :, cols] for b, _, cols in chains)
                           for ref in (r_ref, lw_ref, k_ref, v_ref, kk_ref, al_ref)])
        grads = vjp((tuple(dy_ref[b, :, cols] for b, _, cols in chains),
                     tuple(ds_scr[i] for i in range(len(chains)))))
        for i, (b, _, cols) in enumerate(chains):
            ds_scr[i] = grads[0][i]
            for ref, gval in zip(out_refs, grads[1:]):
                ref[b, :, cols] = gval[i]

    spec = pl.BlockSpec((bsz, CHUNK, w), lambda c: (0, nchunk - 1 - c, 0))
    per_chunk = lambda n: pl.BlockSpec((1, len(chains), n, n), lambda c: (nchunk - 1 - c, 0, 0, 0))
    hbm = pl.BlockSpec(memory_space=pltpu.HBM)
    outs = pl.pallas_call(
        body,
        grid=(nchunk,),
        in_specs=[spec] * 6 + [per_chunk(LANES), per_chunk(4 * CHUNK), per_chunk(2 * CHUNK), per_chunk(2 * CHUNK), spec]
        + [hbm] * nex,
        out_specs=[spec] * 6 + [hbm] * nex,
        out_shape=[jax.ShapeDtypeStruct((bsz, t, w), F32)] * 6 + _exchange_out_shapes(hosted, scatter),
        scratch_shapes=[pltpu.VMEM((len(chains), LANES, LANES), F32)] + (_exchange_scratch(nex) if nex else []),
        compiler_params=_cparams(("arbitrary",), 48),
        name="rwkv_chunk_bwd",
    )(r, lw, k, v, kk, al, *from_fwd, dy, *hosted)
    return outs[:6], outs[6:]


def _alibi_slope(head):
    return 2.0 ** (-8.0 * (head + 1) / N_ATTN_HEADS)


def _attn_setup(first):
    row = lax.broadcasted_iota(jnp.int32, (BLOCK, 2 * BLOCK), 0)
    col = lax.broadcasted_iota(jnp.int32, (BLOCK, 2 * BLOCK), 1)
    lane = lax.broadcasted_iota(jnp.int32, (1, LANES), 1)
    halves = [jnp.where((lane // HEAD_DIM) == half, 1.0, 0.0) for half in range(2)]
    srow = lax.broadcasted_iota(jnp.int32, (LANES, LANES), 0)
    scol = lax.broadcasted_iota(jnp.int32, (LANES, LANES), 1)
    swap = jnp.where((srow + HEAD_DIM) % LANES == scol, 1.0, 0.0)
    dist = row - col + BLOCK
    valid = jnp.logical_and(jnp.logical_and(dist >= 0, dist < BLOCK),
                            jnp.logical_or(col >= BLOCK, jnp.logical_not(first)))
    return halves, swap, dist.astype(F32), valid, HEAD_DIM ** -0.5


def _attn_keys_values(kp, kc, vp, vc, swap, npair):
    stored = (jnp.concatenate([kp, kc], axis=0), jnp.concatenate([vp, vc], axis=0))
    swapped = tuple(_bdot_raw(t, swap, NN) for t in stored)
    heads = [(pair, half) for pair in range(npair) for half in range(2)]
    return heads, [half == pair // 2 for pair, half in heads], stored, swapped


def _attn_block(qs, kp, kc, vp, vc, sinks, first):
    halves, swap, dist, valid, scale = _attn_setup(first)
    heads, as_stored, stored, swapped = _attn_keys_values(kp, kc, vp, vc, swap, len(qs))
    kv = [stored if own else swapped for own in as_stored]
    slopes = [_alibi_slope(2 * pair + half) for pair, half in heads]
    qa = [qs[pair] * halves[half] for pair, half in heads]
    s = [jnp.where(valid, _bdot_raw(q, t[0], NT) * scale - sl * dist, NEG_INF) for q, t, sl in zip(qa, kv, slopes)]
    mx = [jnp.maximum(jnp.max(a, axis=-1, keepdims=True), sk) for a, sk in zip(s, sinks)]
    e = [jnp.exp(a - m) for a, m in zip(s, mx)]
    es = [jnp.exp(sk - m) for sk, m in zip(sinks, mx)]
    inv = [1.0 / (jnp.sum(a, axis=-1, keepdims=True) + b) for a, b in zip(e, es)]
    probs = [a * i for a, i in zip(e, inv)]
    o = [_bdot_raw(p, t[1], NN) for p, t in zip(probs, kv)]
    outs = tuple(o[2 * pair] * halves[0] + o[2 * pair + 1] * halves[1] for pair in range(len(qs)))
    return outs, probs, [b * i for b, i in zip(es, inv)]


def _attn_block_bwd(qs, kp, kc, vp, vc, dos, probs, first):
    halves, swap, _, _, scale = _attn_setup(first)
    heads, as_stored, stored, swapped = _attn_keys_values(kp, kc, vp, vc, swap, len(qs))
    kv = [stored if own else swapped for own in as_stored]
    qa = [qs[pair] * halves[half] for pair, half in heads]
    do = [dos[pair] * halves[half] for pair, half in heads]
    dp = [_bdot_raw(d, t[1], NT) for d, t in zip(do, kv)]
    delta = [jnp.sum(p * d, axis=-1, keepdims=True) for p, d in zip(probs, dp)]
    ds = [p * (d - dl) for p, d, dl in zip(probs, dp, delta)]
    dq = [_bdot_raw(g, t[0], NN) * (scale * halves[half]) for g, t, (_, half) in zip(ds, kv, heads)]
    dk = [_bdot_raw(g, q, TN) * scale for g, q in zip(ds, qa)]
    dv = [_bdot_raw(p, d, TN) for p, d in zip(probs, do)]
    dqs = tuple(dq[2 * pair] + dq[2 * pair + 1] for pair in range(len(qs)))

    def total(parts):
        direct = sum(g for g, own in zip(parts, as_stored) if own)
        return direct + _bdot_raw(sum(g for g, own in zip(parts, as_stored) if not own), swap, NN)

    dk_all, dv_all = total(dk), total(dv)
    return dqs, dk_all[:BLOCK], dk_all[BLOCK:], dv_all[:BLOCK], dv_all[BLOCK:], delta


def _sink_values(sink_ref):
    return [jnp.max(sink_ref[h:h + 1, :], axis=-1, keepdims=True) for h in range(N_ATTN_HEADS)]


def _head_columns(cols):
    lane = lax.broadcasted_iota(jnp.int32, (1, LANES), 1)
    return sum(c * jnp.where(lane == h, 1.0, 0.0) for h, c in enumerate(cols))


def _attn_fwd(z, sink_rows):
    bsz, t, _ = z.shape
    nb = t // BLOCK
    npair = ATTN_WIDTH // LANES

    def body(q_ref, kp_ref, kc_ref, vp_ref, vc_ref, sink_ref, o_ref, p_ref, ps_ref):
        first = pl.program_id(1) == 0
        qs = tuple(q_ref[0, :, pair * LANES:(pair + 1) * LANES] for pair in range(npair))
        outs, probs, psinks = _attn_block(qs, kp_ref[0], kc_ref[0], vp_ref[0], vc_ref[0], _sink_values(sink_ref),
                                          first)
        for pair in range(npair):
            o_ref[0, :, pair * LANES:(pair + 1) * LANES] = outs[pair].astype(o_ref.dtype)
        for h, p in enumerate(probs):
            p_ref[0, 0, h] = p
        ps_ref[0] = _head_columns(psinks)

    kcol, vcol = ATTN_WIDTH // KV_WIDTH, ATTN_WIDTH // KV_WIDTH + 1
    return pl.pallas_call(
        body,
        grid=(bsz, nb),
        in_specs=[pl.BlockSpec((1, BLOCK, ATTN_WIDTH), lambda b, n: (b, n, 0)),
                  pl.BlockSpec((1, BLOCK, KV_WIDTH), lambda b, n: (b, jnp.maximum(n - 1, 0), kcol)),
                  pl.BlockSpec((1, BLOCK, KV_WIDTH), lambda b, n: (b, n, kcol)),
                  pl.BlockSpec((1, BLOCK, KV_WIDTH), lambda b, n: (b, jnp.maximum(n - 1, 0), vcol)),
                  pl.BlockSpec((1, BLOCK, KV_WIDTH), lambda b, n: (b, n, vcol)),
                  pl.BlockSpec(sink_rows.shape, lambda b, n: (0, 0))],
        out_specs=[pl.BlockSpec((1, BLOCK, ATTN_WIDTH), lambda b, n: (b, n, 0)),
                   pl.BlockSpec((1, 1, N_ATTN_HEADS, BLOCK, 2 * BLOCK), lambda b, n: (b, n, 0, 0, 0)),
                   pl.BlockSpec((1, BLOCK, LANES), lambda b, n: (b, n, 0))],
        out_shape=[jax.ShapeDtypeStruct((bsz, t, ATTN_WIDTH), BF16),
                   jax.ShapeDtypeStruct((bsz, nb, N_ATTN_HEADS, BLOCK, 2 * BLOCK), F32),
                   jax.ShapeDtypeStruct((bsz, t, LANES), F32)],
        compiler_params=_cparams(("parallel", "arbitrary"), 48),
        name="swa_fwd",
    )(z, z, z, z, z, sink_rows)


def _attn_bwd(z, dout, probs, psinks):
    bsz, t, _ = z.shape
    nb = t // BLOCK
    npair = ATTN_WIDTH // LANES

    def body(q_ref, kp_ref, kc_ref, vp_ref, vc_ref, do_ref, p_ref, ps_ref, dz_ref, dsink_ref, carry):
        step = pl.program_id(1)
        n = nb - 1 - step

        @pl.when(step == 0)
        def _():
            carry[...] = jnp.zeros_like(carry)

        @pl.when(jnp.logical_and(step == 0, pl.program_id(0) == 0))
        def _():
            dsink_ref[...] = jnp.zeros_like(dsink_ref)

        qs = tuple(q_ref[0, :, pair * LANES:(pair + 1) * LANES] for pair in range(npair))
        dos = tuple(do_ref[0, :, pair * LANES:(pair + 1) * LANES] for pair in range(npair))
        kept = [p_ref[0, 0, h] for h in range(N_ATTN_HEADS)]
        dqs, dkp, dkc, dvp, dvc, deltas = _attn_block_bwd(qs, kp_ref[0], kc_ref[0], vp_ref[0], vc_ref[0], dos, kept,
                                                          n == 0)
        for pair in range(npair):
            dz_ref[0, :, pair * LANES:(pair + 1) * LANES] = dqs[pair].astype(dz_ref.dtype)
        dsink_ref[...] -= jnp.sum(ps_ref[0] * _head_columns(deltas), axis=0, keepdims=True)
        dz_ref[0, :, ATTN_WIDTH:ATTN_WIDTH + KV_WIDTH] = (dkc + carry[0]).astype(dz_ref.dtype)
        dz_ref[0, :, ATTN_WIDTH + KV_WIDTH:QKV_WIDTH] = (dvc + carry[1]).astype(dz_ref.dtype)
        carry[0] = dkp
        carry[1] = dvp

    kcol, vcol = ATTN_WIDTH // KV_WIDTH, ATTN_WIDTH // KV_WIDTH + 1
    rev = lambda n: nb - 1 - n
    return pl.pallas_call(
        body,
        grid=(bsz, nb),
        in_specs=[pl.BlockSpec((1, BLOCK, ATTN_WIDTH), lambda b, n: (b, rev(n), 0)),
                  pl.BlockSpec((1, BLOCK, KV_WIDTH), lambda b, n: (b, jnp.maximum(rev(n) - 1, 0), kcol)),
                  pl.BlockSpec((1, BLOCK, KV_WIDTH), lambda b, n: (b, rev(n), kcol)),
                  pl.BlockSpec((1, BLOCK, KV_WIDTH), lambda b, n: (b, jnp.maximum(rev(n) - 1, 0), vcol)),
                  pl.BlockSpec((1, BLOCK, KV_WIDTH), lambda b, n: (b, rev(n), vcol)),
                  pl.BlockSpec((1, BLOCK, ATTN_WIDTH), lambda b, n: (b, rev(n), 0)),
                  pl.BlockSpec((1, 1, N_ATTN_HEADS, BLOCK, 2 * BLOCK), lambda b, n: (b, rev(n), 0, 0, 0)),
                  pl.BlockSpec((1, BLOCK, LANES), lambda b, n: (b, rev(n), 0))],
        out_specs=[pl.BlockSpec((1, BLOCK, QKV_WIDTH), lambda b, n: (b, rev(n), 0)),
                   pl.BlockSpec((1, LANES), lambda b, n: (0, 0))],
        out_shape=[jax.ShapeDtypeStruct((bsz, t, QKV_WIDTH), BF16),
                   jax.ShapeDtypeStruct((1, LANES), F32)],
        scratch_shapes=[pltpu.VMEM((2, BLOCK, KV_WIDTH), F32)],
        compiler_params=_cparams(("arbitrary", "arbitrary"), 48),
        name="swa_bwd",
    )(z, z, z, z, z, dout, probs, psinks)


def _exchange_out_shapes(arrays, scatter):
    return [jax.ShapeDtypeStruct((N_DEV,) + (a.shape[1:] if scatter else a.shape), a.dtype) for a in arrays]


def _exchange_scratch(n):
    return [pltpu.SemaphoreType.DMA((n, N_DEV - 1)), pltpu.SemaphoreType.DMA((n, N_DEV - 1)),
            pltpu.SemaphoreType.DMA((n,))]


def _exchange_copies(ins, outs, send_sems, recv_sems, local_sems, scatter, arrivals=True):
    x, y, c = lax.axis_index("x"), lax.axis_index("y"), lax.axis_index("c")
    me = 4 * x + 2 * y + c
    copies = []
    for i in range(len(ins)):
        own = pltpu.make_async_copy(ins[i].at[me] if scatter else ins[i], outs[i].at[me], local_sems.at[i])
        copies.append((own, None, True))
        for d in range(1, N_DEV):
            px = 1 - x if d & 4 else x
            py = 1 - y if d & 2 else y
            pc = 1 - c if d & 1 else c
            peer = 4 * px + 2 * py + pc
            src = ins[i].at[peer] if scatter else ins[i]
            send = pltpu.make_async_remote_copy(src, outs[i].at[me], send_sems.at[i, d - 1], recv_sems.at[i, d - 1],
                                                device_id=(px, py, pc), device_id_type=MESH)
            recv = pltpu.make_async_remote_copy(src, outs[i].at[peer], send_sems.at[i, d - 1], recv_sems.at[i, d - 1],
                                                device_id=(px, py, pc), device_id_type=MESH) if arrivals else None
            copies.append((send, recv, False))
    return copies


def _exchange_start(copies):
    for send, _, _ in copies:
        send.start()


def _exchange_wait(copies):
    for send, recv, local in copies:
        if local:
            send.wait()
        else:
            send.wait_send()
            recv.wait_recv()


def _gather_two_level(arrays, name):
    n = len(arrays)

    def body(*refs):
        ins, outs = refs[:n], refs[n:2 * n]
        send_sems, recv_sems, local_sems = refs[2 * n:]
        x, y, c = lax.axis_index("x"), lax.axis_index("y"), lax.axis_index("c")
        index = lambda px, py, pc: 4 * px + 2 * py + pc
        sibling = (x, y, 1 - c)
        chips = [(1 - x, y), (x, 1 - y), (1 - x, 1 - y)]

        def copy(i, k, block, to, src=None):
            slot = outs[i].at[index(*block)]
            return pltpu.make_async_remote_copy(slot if src is None else src, slot, send_sems.at[i, k],
                                                recv_sems.at[i, k], device_id=to, device_id_type=MESH)

        local, sends = [], []
        for i in range(n):
            own = pltpu.make_async_copy(ins[i], outs[i].at[index(x, y, c)], local_sems.at[i])
            own.start()
            local.append(own)
            first = [copy(i, 0, (x, y, c), sibling, src=ins[i])]
            first += [copy(i, 1 + j, (x, y, c), (*chip, c), src=ins[i]) for j, chip in enumerate(chips)]
            for cp in first:
                cp.start()
            sends += first
        for i in range(n):
            for j, chip in enumerate(chips):
                copy(i, 1 + j, (*chip, c), (x, y, c)).wait_recv()
                onward = copy(i, 4 + j, (*chip, c), sibling)
                onward.start()
                sends.append(onward)
        for i in range(n):
            copy(i, 0, sibling, (x, y, c)).wait_recv()
            for j, chip in enumerate(chips):
                copy(i, 4 + j, (*chip, 1 - c), (x, y, c)).wait_recv()
        for cp in sends:
            cp.wait_send()
        for cp in local:
            cp.wait()

    hbm = pl.BlockSpec(memory_space=pltpu.HBM)
    return pl.pallas_call(
        body,
        in_specs=[hbm] * n,
        out_specs=[hbm] * n,
        out_shape=_exchange_out_shapes(arrays, False),
        scratch_shapes=_exchange_scratch(n),
        name=name,
    )(*arrays)


def _exchange(arrays, *, scatter, name):
    n = len(arrays)

    def body(*refs):
        copies = _exchange_copies(refs[:n], refs[n:2 * n], *refs[2 * n:], scatter)
        _exchange_start(copies)
        _exchange_wait(copies)

    hbm = pl.BlockSpec(memory_space=pltpu.HBM)
    return pl.pallas_call(
        body,
        in_specs=[hbm] * n,
        out_specs=[hbm] * n,
        out_shape=_exchange_out_shapes(arrays, scatter),
        scratch_shapes=_exchange_scratch(n),
        name=name,
    )(*arrays)


def _adamw(parts, w, m, v, name):
    rows, cols = w.shape
    tr = _pick(rows, (256, 128, 64, 8))
    c1 = 1.0 / (1.0 - ADAM_B1 ** ADAM_STEP)
    c2 = 1.0 / (1.0 - ADAM_B2 ** ADAM_STEP)

    def body(p_ref, w_ref, m_ref, v_ref, g_ref, d_ref, mo_ref, vo_ref):
        g = p_ref[0].astype(F32)
        for s in range(1, N_DEV):
            g = g + p_ref[s].astype(F32)
        mn = ADAM_B1 * m_ref[...] + (1.0 - ADAM_B1) * g
        vn = ADAM_B2 * v_ref[...] + (1.0 - ADAM_B2) * (g * g)
        g_ref[...] = g
        mo_ref[...] = mn
        vo_ref[...] = vn
        d_ref[...] = -ADAM_LR * ((mn * c1) / (jnp.sqrt(vn * c2) + ADAM_EPS) + ADAM_WD * w_ref[...])

    spec = pl.BlockSpec((tr, cols), lambda i: (i, 0))
    return pl.pallas_call(
        body,
        grid=(rows // tr,),
        in_specs=[pl.BlockSpec((N_DEV, tr, cols), lambda i: (0, i, 0)), spec, spec, spec],
        out_specs=[spec] * 4,
        out_shape=[jax.ShapeDtypeStruct((rows, cols), F32)] * 4,
        compiler_params=_cparams(("parallel",), 48),
        name=name,
    )(parts, w, m, v)


_VECTOR_PARAMS = ("attn_norm_g", "attn_sinks", "rwkv_mu", "w0", "a0", "k_k", "k_a", "r_k", "ln_x_w", "ln_x_b",
                  "mlp_norm_g", "final_norm_g")
_WEIGHT_NAMES = ("attn_norm_g", "w_in", "attn_sinks", "rwkv_mu", "w0", "w2", "a0", "a2", "g2", "k_k", "k_a", "r_k",
                 "ln_x_w", "ln_x_b", "w_out", "mlp_norm_g", "w_up", "w_down", "final_norm_g")


def _pack_vectors(vals):
    pieces = []
    for name in _VECTOR_PARAMS:
        flat = vals[name].reshape(1, -1)
        pad = (-flat.shape[1]) % LANES
        pieces.append(jnp.pad(flat, ((0, 0), (0, pad))) if pad else flat)
    return jnp.concatenate(pieces, axis=1)


def _unpack_vectors(packed, like):
    out, col = {}, 0
    for name in _VECTOR_PARAMS:
        size = like[name].size
        out[name] = packed[0, col:col + size].reshape(like[name].shape)
        col += size + (-size) % LANES
    return out


def kernel(x, attn_norm_g, w_in, attn_sinks, rwkv_mu, w0, w2, a0, a2, g2, k_k, k_a, r_k, ln_x_w, ln_x_b, w_out, mlp_norm_g, w_up, w_down, final_norm_g, loss_target, m_attn_norm_g, m_w_in, m_attn_sinks, m_rwkv_mu, m_w0, m_w2, m_a0, m_a2, m_g2, m_k_k, m_k_a, m_r_k, m_ln_x_w, m_ln_x_b, m_w_out, m_mlp_norm_g, m_w_up, m_w_down, m_final_norm_g, v_attn_norm_g, v_w_in, v_attn_sinks, v_rwkv_mu, v_w0, v_w2, v_a0, v_a2, v_g2, v_k_k, v_k_a, v_r_k, v_ln_x_w, v_ln_x_b, v_w_out, v_mlp_norm_g, v_w_up, v_w_down, v_final_norm_g):
    weights = dict(attn_norm_g=attn_norm_g, w_in=w_in, attn_sinks=attn_sinks, rwkv_mu=rwkv_mu, w0=w0, w2=w2, a0=a0,
                   a2=a2, g2=g2, k_k=k_k, k_a=k_a, r_k=r_k, ln_x_w=ln_x_w, ln_x_b=ln_x_b, w_out=w_out,
                   mlp_norm_g=mlp_norm_g, w_up=w_up, w_down=w_down, final_norm_g=final_norm_g)
    mom1 = dict(attn_norm_g=m_attn_norm_g, w_in=m_w_in, attn_sinks=m_attn_sinks, rwkv_mu=m_rwkv_mu, w0=m_w0, w2=m_w2,
                a0=m_a0, a2=m_a2, g2=m_g2, k_k=m_k_k, k_a=m_k_a, r_k=m_r_k, ln_x_w=m_ln_x_w, ln_x_b=m_ln_x_b,
                w_out=m_w_out, mlp_norm_g=m_mlp_norm_g, w_up=m_w_up, w_down=m_w_down, final_norm_g=m_final_norm_g)
    mom2 = dict(attn_norm_g=v_attn_norm_g, w_in=v_w_in, attn_sinks=v_attn_sinks, rwkv_mu=v_rwkv_mu, w0=v_w0, w2=v_w2,
                a0=v_a0, a2=v_a2, g2=v_g2, k_k=v_k_k, k_a=v_k_a, r_k=v_r_k, ln_x_w=v_ln_x_w, ln_x_b=v_ln_x_b,
                w_out=v_w_out, mlp_norm_g=v_mlp_norm_g, w_up=v_w_up, w_down=v_w_down, final_norm_g=v_final_norm_g)
    bsz, seq, d_model = x.shape
    rows = bsz * seq
    d_in = N_DEV * w_in.shape[2]
    d_ff = N_DEV * w_up.shape[2]

    gathered = _gather_two_level([w_in[0].astype(BF16), w2[0], a2[0], g2[0]], name="gather_in_weights")
    cols_first = lambda a: a.transpose(1, 0, 2).reshape(a.shape[1], -1)
    w_in_f = cols_first(gathered[0])
    w_attn, w_rw = w_in_f[:, :QKV_WIDTH], w_in_f[:, QKV_WIDTH:]
    w2_f, a2_f, g2_f = cols_first(gathered[1]), cols_first(gathered[2]), cols_first(gathered[3])
    lora = w2_f.shape[0]
    w2p = jnp.concatenate([w2_f, jnp.zeros_like(a2_f)], axis=0)
    a2p = jnp.concatenate([jnp.zeros_like(w2_f), a2_f], axis=0)

    esum = _head_sum_matrix()
    sink_rows = jnp.broadcast_to(attn_sinks.reshape(N_ATTN_HEADS, 1), (N_ATTN_HEADS, LANES))
    prep_pars = [w0, w2p, a0, a2p, g2_f, k_k, k_a, esum]
    post_pars = [ln_x_w, ln_x_b, r_k, esum]

    x2d = x.reshape(rows, d_model)
    h1, z_attn, z_rw, r, lw, k, v, kk, al, gate = _norm_in_proj_prep(x2d, seq, attn_norm_g, w_attn, w_rw, rwkv_mu,
                                                                       prep_pars)
    z_attn3 = z_attn.reshape(bsz, seq, QKV_WIDTH)
    attn_out, attn_probs, attn_psinks = _attn_fwd(z_attn3, sink_rows)
    as3 = lambda a: a.reshape(bsz, seq, RWKV_WIDTH)
    y, from_fwd, late = _rwkv_fwd(as3(r), as3(lw), as3(k), as3(v), as3(kk), as3(al),
                               hosted=[w_out[0].astype(BF16), w_up[0].astype(BF16), w_down[0].astype(BF16)])
    w_out_f = late[0].reshape(-1, d_model)
    w_up_f = cols_first(late[1])
    w_down_f = late[2].reshape(-1, d_model)
    y2 = y.reshape(rows, RWKV_WIDTH)
    attn_out2d = attn_out.reshape(rows, ATTN_WIDTH)
    w_out_attn, w_out_rw = w_out_f[:ATTN_WIDTH], w_out_f[ATTN_WIDTH:]
    rw_out, x1, h2 = _post_out_proj_norm(y2, r, k, v, gate, attn_out2d, x2d, post_pars, w_out_attn, w_out_rw,
                                         mlp_norm_g)

    def relu_sq(acc):
        pos = jnp.maximum(acc, 0.0)
        return acc, pos * pos

    u, act = _matmul(h2, w_up_f, "nn", name="mlp_up", epilogue=relu_sq, out_dtypes=(BF16, BF16))
    dx2, loss_vec, g_final = _down_proj_loss(act, w_down_f, x1, loss_target.reshape(rows, d_model),
                                             final_norm_g.reshape(1, d_model))

    g_w_down = _matmul(act, dx2, "tn", name="grad_w_down", out_dtypes=(BF16,))
    du = _matmul(dx2, w_down_f, "nt", name="mlp_down_bwd", extras=(u,), out_dtypes=(BF16,),
                 epilogue=lambda acc, uv: (acc * (2.0 * jnp.maximum(uv.astype(F32), 0.0)),))
    g_w_up = _matmul(h2, du, "tn", name="grad_w_up", out_dtypes=(BF16,))
    dx1, g_mlp_norm = _proj_bwd_norm_bwd([du], [w_up_f], x1, dx2, mlp_norm_g, "mlp_up_bwd_norm_bwd")
    g_w_out = jnp.concatenate([_matmul(attn_out2d, dx1, "tn", name="grad_w_out_attn", out_dtypes=(BF16,)),
                               _matmul(rw_out, dx1, "tn", name="grad_w_out_rwkv", out_dtypes=(BF16,))], axis=0)
    d_attn_out, dy, dr_a, dk_a, dv_a, dgate, g_ln_w, g_ln_b, g_r_k = _out_proj_bwd_post_bwd(
        dx1, y2, r, k, v, gate, post_pars, w_out_attn, w_out_rw)
    by_cols = lambda a: a.reshape(a.shape[0], N_DEV, -1).transpose(1, 0, 2)
    (dr_b, dlw, dk_b, dv_b, dkk, dal), (p_w_out, p_w_up, p_w_down) = _rwkv_bwd(
        as3(r), as3(lw), as3(k), as3(v), as3(kk), as3(al), from_fwd, as3(dy),
        hosted=[g_w_out.reshape(N_DEV, -1, d_model), by_cols(g_w_up), g_w_down.reshape(N_DEV, -1, d_model)],
        scatter=True)
    flat = lambda a: a.reshape(rows, RWKV_WIDTH)
    (dz_rw, gmu_r, gmu_k, gmu_v, gmu_wa, gmu_g, g_w0, g_w2p, g_a0, g_a2p, g_g2, g_k_k, g_k_a) = _prep_bwd(
        z_rw, seq, [dr_a, flat(dr_b), flat(dlw), dk_a, flat(dk_b), dv_a, flat(dv_b), flat(dkk), flat(dal), dgate],
        rwkv_mu, prep_pars)
    dz_attn, g_sink_lanes = _attn_bwd(z_attn3, d_attn_out.reshape(bsz, seq, ATTN_WIDTH), attn_probs, attn_psinks)
    dz_attn = dz_attn.reshape(rows, QKV_WIDTH)
    g_w_in = jnp.concatenate([_matmul(h1, dz_attn, "tn", name="grad_w_in_attn", out_dtypes=(BF16,)),
                              _matmul(h1, dz_rw, "tn", name="grad_w_in_rwkv", out_dtypes=(BF16,))], axis=1)
    lora_grads = jnp.concatenate([g_w2p[:lora], g_a2p[lora:], g_g2], axis=0)
    (dx, g_attn_norm), (p_w_in, p_lora) = _proj_bwd_norm_bwd(
        [dz_attn, dz_rw], [w_attn, w_rw], x2d, dx1, attn_norm_g, "in_proj_bwd_norm_bwd",
        hosted=[by_cols(g_w_in), by_cols(lora_grads)], scatter=True)

    vec_grads = dict(attn_norm_g=g_attn_norm, attn_sinks=g_sink_lanes[0, :N_ATTN_HEADS], rwkv_mu=jnp.concatenate(
        [gmu_r, gmu_k, gmu_v, gmu_wa, gmu_g], axis=1), w0=g_w0, a0=g_a0, k_k=g_k_k, k_a=g_k_a, r_k=g_r_k,
        ln_x_w=g_ln_w, ln_x_b=g_ln_b, mlp_norm_g=g_mlp_norm, final_norm_g=g_final)
    packed = _pack_vectors(vec_grads)
    nvec = packed.shape[1]
    everyone = _exchange([jnp.concatenate([packed, loss_vec], axis=1)], scatter=False, name="gather_vector_grads")[0]
    vec_parts = everyone[:, :, :nvec]
    loss = jnp.sum(everyone[:, 0, nvec])

    grads, delta, new_m, new_v = {}, {}, {}, {}

    def update(name, part, shape2d):
        res = _adamw(part, weights[name].reshape(shape2d), mom1[name].reshape(shape2d), mom2[name].reshape(shape2d),
                     "adamw_" + name)
        for store, val in zip((grads, delta, new_m, new_v), res):
            store[name] = val.reshape(weights[name].shape)

    update("w_in", p_w_in, w_in.shape[1:])
    update("w_out", p_w_out, w_out.shape[1:])
    update("w_up", p_w_up, w_up.shape[1:])
    update("w_down", p_w_down, w_down.shape[1:])
    stack = lambda d: jnp.concatenate([d["w2"][0], d["a2"][0], d["g2"][0]], axis=0)
    lora_res = _adamw(p_lora, stack(weights), stack(mom1), stack(mom2), "adamw_lora")
    for store, val in zip((grads, delta, new_m, new_v), lora_res):
        store["w2"], store["a2"], store["g2"] = val[None, :lora], val[None, lora:2 * lora], val[None, 2 * lora:]
    vec_res = _adamw(vec_parts, _pack_vectors(weights), _pack_vectors(mom1), _pack_vectors(mom2), "adamw_vectors")
    for store, val in zip((grads, delta, new_m, new_v), vec_res):
        store.update(_unpack_vectors(val, weights))

    return (loss, dx.reshape(x.shape), *[grads[n] for n in _WEIGHT_NAMES], *[delta[n] for n in _WEIGHT_NAMES],
            *[new_m[n] for n in _WEIGHT_NAMES], *[new_v[n] for n in _WEIGHT_NAMES])
```

```python
import functools
import math

import jax
import jax.numpy as jnp
from jax import lax
from jax.experimental import pallas as pl
from jax.experimental.pallas import tpu as pltpu

F32 = jnp.float32
BF16 = jnp.bfloat16

N_DEV = 8
HEAD_DIM = 64
LANES = 128
N_ATTN_HEADS = 8
ATTN_WIDTH = 512
KV_WIDTH = 128
QKV_WIDTH = ATTN_WIDTH + 2 * KV_WIDTH
RWKV_WIDTH = 512
LORA_WA = 128
GATE_LORA = 128
RWKV_SHIFT_WIDTH = 3 * RWKV_WIDTH + LORA_WA + GATE_LORA
BLOCK = 128
CHUNK = 64
RMS_EPS = 1e-6
GN_EPS = 64e-5
L2_EPS = 1e-12
NEG_INF = -1e30
DECAY_SCALE = math.exp(-0.5)
ADAM_LR, ADAM_B1, ADAM_B2, ADAM_EPS, ADAM_WD, ADAM_STEP = 0.001, 0.9, 0.999, 1e-08, 0.01, 10

NN = (((1,), (0,)), ((), ()))
NT = (((1,), (1,)), ((), ()))
TN = (((0,), (0,)), ((), ()))
MESH = pl.DeviceIdType.MESH


def _dot(a, b, dn=NN, precision=None):
    return lax.dot_general(a, b, dn, precision=precision, preferred_element_type=F32)


def _bdot_raw(a, b, dn):
    return lax.dot_general(a.astype(BF16), b.astype(BF16), dn, preferred_element_type=F32)


@functools.partial(jax.custom_vjp, nondiff_argnums=(2, 3))
def _bdot_c(a, b, ca, cb):
    return _bdot_raw(a, b, (((ca,), (cb,)), ((), ())))


def _bdot_c_fwd(a, b, ca, cb):
    return _bdot_c(a, b, ca, cb), (a, b)


def _bdot_c_bwd(ca, cb, res, ct):
    a, b = res
    fa, fb = 1 - ca, 1 - cb
    da = _bdot_raw(ct, b, (((1,), (fb,)), ((), ()))) if ca == 1 else _bdot_raw(b, ct, (((fb,), (1,)), ((), ())))
    db = _bdot_raw(a, ct, (((fa,), (0,)), ((), ()))) if cb == 0 else _bdot_raw(ct, a, (((0,), (fa,)), ((), ())))
    return da, db


_bdot_c.defvjp(_bdot_c_fwd, _bdot_c_bwd)


def _bdot(a, b, dn=NN):
    return _bdot_c(a, b, dn[0][0][0], dn[0][1][0])


def _bdot_nn(a, b):
    return _bdot(a, b, NN)


def _bdot_nt(a, b):
    return _bdot(a, b, NT)


def _split3(x):
    hi = x.astype(BF16)
    rest = x - hi.astype(F32)
    mid = rest.astype(BF16)
    return hi, mid, (rest - mid.astype(F32)).astype(BF16)


def _running_sum(x, dn):
    c = x.shape[0]
    row = lax.broadcasted_iota(jnp.int32, (c, c), 0)
    col = lax.broadcasted_iota(jnp.int32, (c, c), 1)
    tri = jnp.where(row >= col, 1.0, 0.0).astype(BF16)
    w = x.shape[1]
    parts = lax.dot_general(tri, jnp.concatenate(_split3(x), axis=1), dn, preferred_element_type=F32)
    return parts[:, :w] + parts[:, w:2 * w] + parts[:, 2 * w:]


@jax.custom_vjp
def _cumsum_rows(x):
    return _running_sum(x, NN)


_cumsum_rows.defvjp(lambda x: (_running_sum(x, NN), None), lambda _, ct: (_running_sum(ct, TN),))


@jax.custom_vjp
def _fold_rows(x):
    c = x.shape[0] // 2
    return x[:c] + x[c:]


_fold_rows.defvjp(lambda x: (_fold_rows(x), None), lambda _, ct: (jnp.concatenate([ct, ct], axis=0),))


@jax.custom_vjp
def _halves(x):
    n = x.shape[0] // 2
    return x[:n], x[n:]


_halves.defvjp(lambda x: (_halves(x), None), lambda _, cts: (jnp.concatenate(cts, axis=0),))


@jax.custom_vjp
def _quarters(x):
    n = x.shape[0] // 2
    return x[:n, :n], x[:n, n:], x[n:, :n], x[n:, n:]


_quarters.defvjp(lambda x: (_quarters(x), None),
                 lambda _, cts: (jnp.concatenate([jnp.concatenate(cts[:2], axis=1),
                                                  jnp.concatenate(cts[2:], axis=1)], axis=0),))


@jax.custom_vjp
def _sigmoid(x):
    return 1.0 / (1.0 + jnp.exp(-x))


def _sigmoid_fwd(x):
    s = _sigmoid(x)
    return s, s


_sigmoid.defvjp(_sigmoid_fwd, lambda s, ct: (ct * s * (1.0 - s),))


def _pick(n, cands):
    for c in cands:
        if n % c == 0:
            return c
    return n


def _cparams(sem, vmem_mb=None):
    kw = dict(dimension_semantics=sem)
    if vmem_mb is not None:
        kw["vmem_limit_bytes"] = vmem_mb * 1024 * 1024
    return pltpu.CompilerParams(**kw)


def _matmul(a, b, mode, *, name, extras=(), epilogue=None, out_dtypes=(F32,), tm=1024, tn=1024, tk=1024,
            hosted=(), scatter=False):
    if mode == "nn":
        (M, K), (_, N) = a.shape, b.shape
    elif mode == "tn":
        (K, M), (_, N) = a.shape, b.shape
    else:
        (M, K), (N, _) = a.shape, b.shape
    tm = _pick(M, (tm, 512, 256, 128))
    tn = _pick(N, (tn, 896, 768, 512, 384, 256, 128))
    tk = _pick(K, (tk, 512, 256, 128))
    nk = K // tk
    ne, nout = len(extras), len(out_dtypes)
    if mode == "nn":
        a_spec = pl.BlockSpec((tm, tk), lambda i, j, k: (i, k))
        b_spec = pl.BlockSpec((tk, tn), lambda i, j, k: (k, j))
        dn = NN
    elif mode == "tn":
        a_spec = pl.BlockSpec((tk, tm), lambda i, j, k: (k, i))
        b_spec = pl.BlockSpec((tk, tn), lambda i, j, k: (k, j))
        dn = TN
    else:
        a_spec = pl.BlockSpec((tm, tk), lambda i, j, k: (i, k))
        b_spec = pl.BlockSpec((tn, tk), lambda i, j, k: (j, k))
        dn = NT
    o_spec = pl.BlockSpec((tm, tn), lambda i, j, k: (i, j))
    grid = (M // tm, N // tn, nk)
    nex = len(hosted)

    def body(*refs):
        a_ref, b_ref = refs[:2]
        e_refs = refs[2:2 + ne]
        ex_in = refs[2 + ne:2 + ne + nex]
        o_refs = refs[2 + ne + nex:2 + ne + nex + nout]
        ex_out = refs[2 + ne + nex + nout:2 + ne + 2 * nex + nout]
        scratch = refs[2 + ne + 2 * nex + nout:]
        kstep = pl.program_id(2)
        if nex:
            at = [pl.program_id(d) for d in range(3)]
            first = jnp.logical_and(jnp.logical_and(at[0] == 0, at[1] == 0), at[2] == 0)
            last = jnp.logical_and(jnp.logical_and(at[0] == grid[0] - 1, at[1] == grid[1] - 1), at[2] == grid[2] - 1)
            _hosted_exchange(first, last, ex_in, ex_out, scratch[-3:], scatter)

        def finish(total):
            outs = (total,) if epilogue is None else epilogue(total, *[e[...] for e in e_refs])
            for o_ref, o in zip(o_refs, outs):
                o_ref[...] = o.astype(o_ref.dtype)

        if nk == 1:
            finish(_bdot_raw(a_ref[...], b_ref[...], dn))
            return
        acc = scratch[0]

        @pl.when(kstep == 0)
        def _():
            acc[...] = jnp.zeros_like(acc)

        acc[...] += _bdot_raw(a_ref[...], b_ref[...], dn)

        @pl.when(kstep == nk - 1)
        def _():
            finish(acc[...])

    hbm = pl.BlockSpec(memory_space=pltpu.HBM)
    outs = pl.pallas_call(
        body,
        grid=grid,
        in_specs=[a_spec, b_spec] + [o_spec] * ne + [hbm] * nex,
        out_specs=[o_spec] * nout + [hbm] * nex,
        out_shape=[jax.ShapeDtypeStruct((M, N), dt) for dt in out_dtypes] + _exchange_out_shapes(hosted, scatter),
        scratch_shapes=([pltpu.VMEM((tm, tn), F32)] if nk > 1 else []) + (_exchange_scratch(nex) if nex else []),
        compiler_params=_cparams(("arbitrary",) * 3 if nex else ("parallel", "parallel", "arbitrary"), 56),
        name=name,
    )(a, b, *extras, *hosted)
    if nex:
        return outs[:nout], outs[nout:]
    return outs[0] if nout == 1 else outs


def _rowwise(fn, rows, pars, out_rows, out_accs, *, tile, name, nsub=1, hosted=(), scatter=False):
    rows = [r if isinstance(r, tuple) else (r, r.shape[1], 0) for r in rows]
    R = rows[0][0].shape[0]
    tile = min(tile, R)
    chunk = tile // nsub
    ntile = R // tile
    nr, npar, nor, noa, nex = len(rows), len(pars), len(out_rows), len(out_accs), len(hosted)

    def body(*refs):
        rin = refs[:nr]
        pin = refs[nr:nr + npar]
        ex_in = refs[nr + npar:nr + npar + nex]
        orow = refs[nr + npar + nex:nr + npar + nex + nor]
        oacc = refs[nr + npar + nex + nor:nr + npar + nex + nor + noa]
        ex_out = refs[nr + npar + nex + nor + noa:nr + npar + 2 * nex + nor + noa]
        step = pl.program_id(0)
        _hosted_exchange(step == 0, step == ntile - 1, ex_in, ex_out, refs[nr + npar + 2 * nex + nor + noa:], scatter)
        pvals = [p[...] for p in pin]
        totals = []
        for sub in range(nsub):
            at = slice(sub * chunk, (sub + 1) * chunk)
            outs = fn(*[r[at, :] for r in rin], *pvals)
            for ref, o in zip(orow, outs[:nor]):
                if isinstance(o, (tuple, list)):
                    col = 0
                    for piece in o:
                        ref[at, col:col + piece.shape[1]] = piece.astype(ref.dtype)
                        col += piece.shape[1]
                else:
                    ref[at, :] = o.astype(ref.dtype)
            accs = list(outs[nor:])
            totals = accs if sub == 0 else [t + a for t, a in zip(totals, accs)]

        def accumulate(ref, o):
            @pl.when(step == 0)
            def _():
                ref[...] = o

            @pl.when(step > 0)
            def _():
                ref[...] += o

        for ref, o in zip(oacc, totals):
            accumulate(ref, o)

    def colspec(width, cb):
        return pl.BlockSpec((tile, width), lambda i: (i, cb))

    hbm = pl.BlockSpec(memory_space=pltpu.HBM)
    outs = pl.pallas_call(
        body,
        grid=(ntile,),
        in_specs=[colspec(w, cb) for (_, w, cb) in rows]
        + [pl.BlockSpec(p.shape, lambda i: (0, 0), pipeline_mode=pl.Buffered(1)) for p in pars] + [hbm] * nex,
        out_specs=[colspec(w, 0) for (w, _) in out_rows]
        + [pl.BlockSpec(s, lambda i: (0, 0)) for s in out_accs] + [hbm] * nex,
        out_shape=[jax.ShapeDtypeStruct((R, w), dt) for (w, dt) in out_rows]
        + [jax.ShapeDtypeStruct(s, F32) for s in out_accs] + _exchange_out_shapes(hosted, scatter),
        scratch_shapes=_exchange_scratch(nex) if nex else [],
        compiler_params=_cparams(("arbitrary",), 56),
        name=name,
    )(*[r[0] for r in rows], *pars, *hosted)
    return (outs[:nor + noa], outs[nor + noa:]) if nex else outs


def _rms_fn(x, g):
    return x * lax.rsqrt(jnp.mean(x * x, axis=-1, keepdims=True) + RMS_EPS) * g


FUSED_TILE = 512
FUSED_CHUNKS = 2


def _down_proj_loss(act, w_down, x1, tgt, g):
    d = x1.shape[1]

    def fn(av, xv, tv, wv, gv):
        x2 = xv + _bdot_raw(av, wv, NN)
        y, vjp = jax.vjp(_rms_fn, x2, gv)
        err = y - tv
        loss = 0.5 * jnp.sum(jnp.sum(err * err, axis=-1, keepdims=True), axis=0, keepdims=True) / d
        dx, dg = vjp(err / d)
        return dx, jnp.broadcast_to(loss, (1, LANES)), dg

    return _rowwise(fn, [act, x1, tgt], [w_down, g], [(d, F32)], [(1, LANES), g.shape],
                    tile=FUSED_TILE, nsub=FUSED_CHUNKS, name="mlp_down_final_norm_loss")


def _proj_bwd_norm_bwd(cts, weights_t, x, dres, g, name, hosted=(), scatter=False):
    n = len(cts)

    def fn(*vals):
        ctv, (xv, dresv), wv, gv = vals[:n], vals[n:n + 2], vals[n + 2:2 * n + 2], vals[-1]
        dh = _bdot_raw(ctv[0], wv[0], NT)
        for c, w in zip(ctv[1:], wv[1:]):
            dh = dh + _bdot_raw(c, w, NT)
        _, vjp = jax.vjp(_rms_fn, xv, gv)
        dx, dg = vjp(dh)
        return dx + dresv, dg

    return _rowwise(fn, [*cts, x, dres], [*weights_t, g], [(x.shape[1], F32)], [g.shape],
                    tile=FUSED_TILE, nsub=FUSED_CHUNKS, name=name, hosted=hosted, scatter=scatter)


def _head_sum_matrix():
    i = lax.broadcasted_iota(jnp.int32, (RWKV_WIDTH, RWKV_WIDTH), 0) // HEAD_DIM
    j = lax.broadcasted_iota(jnp.int32, (RWKV_WIDTH, RWKV_WIDTH), 1) // HEAD_DIM
    return (i == j).astype(BF16)


def _head_sums_raw(x, esum):
    hi = x.astype(BF16)
    lo = (x - hi.astype(F32)).astype(BF16)
    return _dot(hi, esum) + _dot(lo, esum)


@jax.custom_vjp
def _head_sums(x, esum):
    return _head_sums_raw(x, esum)


_head_sums.defvjp(lambda x, esum: (_head_sums_raw(x, esum), esum),
                  lambda esum, ct: (_head_sums_raw(ct, esum), jnp.zeros_like(esum)))


def _prep_core(xr, xk, xv, xwa, xg, w0, w2p, a0, a2p, g2, k_k, k_a, esum):
    lw = -DECAY_SCALE * _sigmoid(w0 + _bdot_nn(jnp.tanh(xwa), w2p))
    a = _sigmoid(a0 + _bdot_nn(xwa, a2p))
    g = _bdot_nn(_sigmoid(xg), g2)
    kk0 = xk * k_k
    kk = kk0 * jnp.minimum(lax.rsqrt(_head_sums(kk0 * kk0, esum)), 1.0 / L2_EPS)
    k = xk * (1.0 + (a - 1.0) * k_a)
    return xr, lw, k, xv, kk, a, g


_SEGS = ((0, 512), (512, 1024), (1024, 1536), (1536, 1664), (1664, 1792))


PREP_TILE = 256
SUBLANES = 8


def _shifted_tokens(z_ref, zprev_ref, tile_index, seq):
    zc = z_ref[...]
    start = (tile_index * PREP_TILE) % seq == 0
    before = jnp.where(start, 0.0, zprev_ref[SUBLANES - 1:SUBLANES, :])
    rowid = lax.broadcasted_iota(jnp.int32, zc.shape, 0)
    return zc, jnp.where(rowid == 0, before, pltpu.roll(zc, 1, 0))


def _prep_specs(z, mu, pars, index):
    width = z.shape[1]
    per = PREP_TILE // SUBLANES
    return ([pl.BlockSpec((PREP_TILE, width), lambda i: (index(i), 0)),
             pl.BlockSpec((SUBLANES, width), lambda i: (jnp.maximum(index(i) * per - 1, 0), 0))],
            [pl.BlockSpec(p.shape, lambda i: (0, 0)) for p in (mu, *pars)])


def _norm_in_proj_prep(x, seq, g, w_attn, w_rw, mu, pars):
    rows, d = x.shape
    chunk = FUSED_TILE // FUSED_CHUNKS
    npar = len(pars)
    wa_width, wr_width = w_attn.shape[1], w_rw.shape[1]

    def body(x_ref, g_ref, wa_ref, wr_ref, mu_ref, *rest):
        par_refs = rest[:npar]
        h_ref, za_ref, zr_ref = rest[npar:npar + 3]
        out_refs, carry = rest[npar + 3:-1], rest[-1]
        step = pl.program_id(0)

        @pl.when(step == 0)
        def _():
            carry[...] = jnp.zeros_like(carry)

        pv = [p[...] for p in par_refs]
        for sub in range(FUSED_CHUNKS):
            at = slice(sub * chunk, (sub + 1) * chunk)
            h = _rms_fn(x_ref[at, :], g_ref[...])
            h_ref[at, :] = h.astype(h_ref.dtype)
            za_ref[at, :] = _bdot_raw(h, wa_ref[...], NN)
            zc = _bdot_raw(h, wr_ref[...], NN)
            zr_ref[at, :] = zc
            start = (step * FUSED_TILE + sub * chunk) % seq == 0
            before = jnp.where(start, 0.0, carry[SUBLANES - 1:SUBLANES, :])
            rowid = lax.broadcasted_iota(jnp.int32, zc.shape, 0)
            zp = jnp.where(rowid == 0, before, pltpu.roll(zc, 1, 0))
            carry[...] = zc[chunk - SUBLANES:chunk, :]
            zs = zc + (zp - zc) * mu_ref[...]
            outs = _prep_core(*[zs[:, a:b] for a, b in _SEGS], *pv)
            for ref, o in zip(out_refs, outs):
                ref[at, :] = o

    tiled = lambda width: pl.BlockSpec((FUSED_TILE, width), lambda i: (i, 0))
    resident = lambda a: pl.BlockSpec(a.shape, lambda i: (0, 0), pipeline_mode=pl.Buffered(1))
    return pl.pallas_call(
        body,
        grid=(rows // FUSED_TILE,),
        in_specs=[tiled(d)] + [resident(a) for a in (g, w_attn, w_rw, mu, *pars)],
        out_specs=[tiled(d), tiled(wa_width), tiled(wr_width)] + [tiled(RWKV_WIDTH)] * 7,
        out_shape=[jax.ShapeDtypeStruct((rows, d), BF16), jax.ShapeDtypeStruct((rows, wa_width), F32),
                   jax.ShapeDtypeStruct((rows, wr_width), F32)] + [jax.ShapeDtypeStruct((rows, RWKV_WIDTH), F32)] * 7,
        scratch_shapes=[pltpu.VMEM((SUBLANES, wr_width), F32)],
        compiler_params=_cparams(("arbitrary",), 56),
        name="attn_norm_in_proj_rwkv_prep",
    )(x, g, w_attn, w_rw, mu, *pars)


def _prep_bwd(z, seq, cts, mu, pars):
    rows, width = z.shape
    ntile = rows // PREP_TILE
    npar, nct = len(pars), len(cts)
    acc_shapes = [(1, b - a) for a, b in _SEGS] + [p.shape for p in pars[:-1]]

    def body(z_ref, zprev_ref, *rest):
        ct_refs = rest[:nct]
        mu_ref = rest[nct]
        par_refs = rest[nct + 1:nct + 1 + npar]
        dz_ref = rest[nct + 1 + npar]
        acc_refs = rest[nct + 2 + npar:-1]
        carry = rest[-1]
        step = pl.program_id(0)
        tile_index = ntile - 1 - step

        @pl.when(step == 0)
        def _():
            carry[...] = jnp.zeros_like(carry)

        zc, zp = _shifted_tokens(z_ref, zprev_ref, tile_index, seq)
        mu_v = mu_ref[...]
        diff = zp - zc
        zs = zc + diff * mu_v
        dra, drb, dlw, dka, dkb, dva, dvb, dkk, da, dg = [c[...] for c in ct_refs]
        pv = [p[...] for p in par_refs]
        _, vjp = jax.vjp(lambda *args: _prep_core(*args, pv[-1]), *[zs[:, a:b] for a, b in _SEGS], *pv[:-1])
        grads = vjp((dra + drb, dlw, dka + dkb, dva + dvb, dkk, da, dg))
        dsegs, dpars = grads[:5], grads[5:]
        last_of_sequence = ((tile_index + 1) * PREP_TILE) % seq == 0
        accs = []
        for ds, (a, b) in zip(dsegs, _SEGS):
            mu_s = mu_v[:, a:b]
            dzp = ds * mu_s
            after = jnp.where(last_of_sequence, 0.0, carry[0:1, a:b])
            rowid = lax.broadcasted_iota(jnp.int32, dzp.shape, 0)
            from_next = jnp.where(rowid == PREP_TILE - 1, after, pltpu.roll(dzp, PREP_TILE - 1, 0))
            dz_ref[:, a:b] = (ds * (1.0 - mu_s) + from_next).astype(dz_ref.dtype)
            carry[:, a:b] = dzp[0:SUBLANES, :]
            accs.append(jnp.sum(ds * diff[:, a:b], axis=0, keepdims=True))
        accs.extend(dpars)

        def accumulate(ref, o):
            @pl.when(step == 0)
            def _():
                ref[...] = o

            @pl.when(step > 0)
            def _():
                ref[...] += o

        for ref, o in zip(acc_refs, accs):
            accumulate(ref, o)

    rev = lambda i: ntile - 1 - i
    zspecs, pspecs = _prep_specs(z, mu, pars, rev)
    return pl.pallas_call(
        body,
        grid=(ntile,),
        in_specs=zspecs + [pl.BlockSpec((PREP_TILE, RWKV_WIDTH), lambda i: (rev(i), 0))] * nct + pspecs,
        out_specs=[pl.BlockSpec((PREP_TILE, width), lambda i: (rev(i), 0))]
        + [pl.BlockSpec(s, lambda i: (0, 0)) for s in acc_shapes],
        out_shape=[jax.ShapeDtypeStruct((rows, width), BF16)] + [jax.ShapeDtypeStruct(s, F32) for s in acc_shapes],
        scratch_shapes=[pltpu.VMEM((SUBLANES, width), F32)],
        compiler_params=_cparams(("arbitrary",), 56),
        name="rwkv_prep_bwd",
    )(z, z, *cts, mu, *pars)


def _post_fn(y, r, k, v, g, ln_w, ln_b, r_k, esum):
    mean = _head_sums(y, esum) * (1.0 / HEAD_DIM)
    yc = y - mean
    var = _head_sums(yc * yc, esum) * (1.0 / HEAD_DIM)
    yn = yc * lax.rsqrt(var + GN_EPS) * ln_w + ln_b
    bonus = _head_sums(r * k * r_k, esum) * v
    return (yn + bonus) * g


def _post_out_proj_norm(y, r, k, v, g, attn_out, x, pars, w_attn_rows, w_rwkv_rows, g_norm):
    npar = len(pars)

    def fn(yv, rv, kv, vv, gv, av, xv, *rest):
        wa, wr, gn = rest[npar:]
        rw = _post_fn(yv, rv, kv, vv, gv, *rest[:npar])
        x1 = xv + _bdot_raw(av, wa, NN) + _bdot_raw(rw, wr, NN)
        return rw, x1, _rms_fn(x1, gn)

    d = x.shape[1]
    return _rowwise(fn, [y, r, k, v, g, attn_out, x], [*pars, w_attn_rows, w_rwkv_rows, g_norm],
                    [(RWKV_WIDTH, BF16), (d, F32), (d, BF16)], [],
                    tile=FUSED_TILE, nsub=FUSED_CHUNKS, name="rwkv_post_out_proj_mlp_norm")


def _out_proj_bwd_post_bwd(dx1, y, r, k, v, g, pars, w_attn_rows, w_rwkv_rows):
    npar = len(pars)

    def fn(dxv, yv, rv, kv, vv, gv, *rest):
        wa, wr = rest[npar:]
        esum = rest[npar - 1]
        d_attn = _bdot_raw(dxv, wa, NT)
        d_rw = _bdot_raw(dxv, wr, NT)
        _, vjp = jax.vjp(lambda *a: _post_fn(*a, esum), yv, rv, kv, vv, gv, *rest[:npar - 1])
        return (d_attn, *vjp(d_rw))

    return _rowwise(fn, [dx1, y, r, k, v, g], [*pars, w_attn_rows, w_rwkv_rows], [(RWKV_WIDTH, F32)] * 6,
                    [p.shape for p in pars[:-1]], tile=FUSED_TILE, nsub=FUSED_CHUNKS, name="out_proj_bwd_rwkv_post_bwd")


def _tri_inverses(ms, tick=lambda: None):
    n = ms[0].shape[0]
    row = lax.broadcasted_iota(jnp.int32, (n, n), 0)
    col = lax.broadcasted_iota(jnp.int32, (n, n), 1)
    eye = jnp.where(row == col, 1.0, 0.0)
    t_inv = [eye + m for m in ms]
    power = [_bdot_raw(m, m, NN) for m in ms]
    steps = int(math.log2(n // 2)) - 1
    for step in range(steps):
        if step < steps - 1:
            both = [_bdot_raw(jnp.concatenate([p, t], axis=0), p, NN) for p, t in zip(power, t_inv)]
            power = [b[:n] for b in both]
            t_inv = [t + b[n:] for t, b in zip(t_inv, both)]
        else:
            t_inv = [t + _bdot_raw(t, p, NN) for t, p in zip(t_inv, power)]
        tick()
    return t_inv


@jax.custom_vjp
def _tri_solve(ms, xs):
    return tuple(_bdot_raw(t, x, NN) for t, x in zip(_tri_inverses(ms), xs))


def _tri_solve_fwd(ms, xs):
    t_inv = _tri_inverses(ms)
    us = tuple(_bdot_raw(t, x, NN) for t, x in zip(t_inv, xs))
    return us, (tuple(t_inv), us)


def _tri_solve_bwd(res, dus):
    t_inv, us = res
    dxs = tuple(_bdot_raw(t, du, TN) for t, du in zip(t_inv, dus))
    dms = tuple(_bdot_raw(dx, u, NT) for dx, u in zip(dxs, us))
    return dms, dxs


_tri_solve.defvjp(_tri_solve_fwd, _tri_solve_bwd)


@functools.partial(jax.custom_vjp, nondiff_argnums=(3, 4))
def _kept_bdot_c(a, b, kept, ca, cb):
    return kept.astype(F32)


def _kept_bdot_c_fwd(a, b, kept, ca, cb):
    return kept.astype(F32), (a, b, kept)


def _kept_bdot_c_bwd(ca, cb, res, ct):
    a, b, kept = res
    return (*_bdot_c_bwd(ca, cb, (a, b), ct), jnp.zeros_like(kept))


_kept_bdot_c.defvjp(_kept_bdot_c_fwd, _kept_bdot_c_bwd)


def _kept_bdot(a, b, kept, dn):
    return _kept_bdot_c(a, b, kept, dn[0][0][0], dn[0][1][0])


@jax.custom_vjp
def _kept_tri_solve(ms, xs, t_inv, us):
    return tuple(u.astype(F32) for u in us)


def _kept_tri_solve_fwd(ms, xs, t_inv, us):
    return tuple(u.astype(F32) for u in us), (t_inv, us)


def _kept_tri_solve_bwd(res, dus):
    t_inv, us = res
    dms, dxs = _tri_solve_bwd(res, dus)
    return dms, dxs, tuple(jnp.zeros_like(t) for t in t_inv), tuple(jnp.zeros_like(u) for u in us)


_kept_tri_solve.defvjp(_kept_tri_solve_fwd, _kept_tri_solve_bwd)


def _chunk_fn(ss, rs, lws, ks, vs, kks, als, kept=None, tick=lambda: None):
    c = rs[0].shape[0]
    n = 2 * c
    row = lax.broadcasted_iota(jnp.int32, (n, n), 0)
    col = lax.broadcasted_iota(jnp.int32, (n, n), 1)
    incl = (row % c) >= (col % c)
    strict = (row % c) > (col % c)
    lane = lax.broadcasted_iota(jnp.int32, (1, LANES), 1)
    m_lo = jnp.where(lane < HEAD_DIM, 1.0, 0.0)
    m_hi = 1.0 - m_lo

    def stack(a):
        return jnp.concatenate([a * m_lo, a * m_hi], axis=0)

    cums = [_cumsum_rows(lw) for lw in lws]
    totals = [jnp.sum(lw, axis=0, keepdims=True) for lw in lws]
    bs = [kk * al for kk, al in zip(kks, als)]
    grows = [jnp.exp(-cum) for cum in cums]
    a_s = [stack(-kk * jnp.exp(cum - lw)) for kk, cum, lw in zip(kks, cums, lws)]
    b_s = [stack(b * g) for b, g in zip(bs, grows)]
    k_s = [stack(k * g) for k, g in zip(ks, grows)]
    r_s = [stack(r * jnp.exp(cum)) for r, cum in zip(rs, cums)]
    v_s = [stack(v) for v in vs]
    tick()
    pair = lambda p, q: jnp.concatenate([p, q], axis=0)
    ar_s = [pair(a, r) for a, r in zip(a_s, r_s)]
    if kept is None:
        products = [_bdot_raw(ar, pair(b, k), NT) for ar, b, k in zip(ar_s, b_s, k_s)]
    else:
        products = [_kept_bdot(ar, pair(b, k), kp, NT) for ar, b, k, kp in zip(ar_s, b_s, k_s, kept[0])]
    tick()
    blocks = [_quarters(p) for p in products]
    m_ab = [jnp.where(strict, q[0], 0.0) for q in blocks]
    m_ak = [jnp.where(strict, q[1], 0.0) for q in blocks]
    m_rb = [jnp.where(incl, q[2], 0.0) for q in blocks]
    m_rk = [jnp.where(incl, q[3], 0.0) for q in blocks]
    from_state = [_halves(_bdot(ar, s, NT)) for ar, s in zip(ar_s, ss)]
    tick()
    from_v = [_halves(_bdot(pair(mk, mr), v)) for mk, mr, v in zip(m_ak, m_rk, v_s)]
    tick()
    x = tuple(fs[0] + fv[0] for fs, fv in zip(from_state, from_v))
    if kept is None:
        t_inv = _tri_inverses(m_ab, tick)
        u = [_bdot_raw(t, xx, NN) for t, xx in zip(t_inv, x)]
        tick()
    else:
        u = _kept_tri_solve(tuple(m_ab), x, kept[1], kept[2])
    y = [_fold_rows(fs[1] + _bdot(mb, uu) + fv[1]) for fs, mb, uu, fv in zip(from_state, m_rb, u, from_v)]
    tick()
    tails = [jnp.exp(tot - cum) for tot, cum in zip(totals, cums)]
    s_new = [s * jnp.exp(tot) + _bdot(pair(uu, v), pair(stack(b * tl), stack(k * tl)), TN)
             for s, tot, uu, b, tl, v, k in zip(ss, totals, u, bs, tails, v_s, ks)]
    if kept is None:
        keep = lambda vals: tuple(v.astype(BF16) for v in vals)
        return tuple(y), tuple(s_new), (keep(products), keep(t_inv), keep(u))
    return tuple(y), tuple(s_new)


def _chains(bsz, npair):
    return [(b, p, slice(p * LANES, (p + 1) * LANES)) for b in range(bsz) for p in range(npair)]


def _hosted_exchange(first, last, ex_in, ex_out, sems, scatter):
    if not ex_in:
        return

    @pl.when(first)
    def _():
        _exchange_start(_exchange_copies(ex_in, ex_out, *sems, scatter, arrivals=False))

    @pl.when(last)
    def _():
        _exchange_wait(_exchange_copies(ex_in, ex_out, *sems, scatter, arrivals=True))


def _rwkv_attn_fwd(r, lw, k, v, kk, al, z, sink_rows, hosted=(), scatter=False):
    bsz, t, w = r.shape
    npair, nchunk = w // LANES, t // CHUNK
    nb = t // BLOCK
    assert bsz * nb == nchunk
    chains = _chains(bsz, npair)
    nex = len(hosted)
    apair = ATTN_WIDTH // LANES

    def body(*refs):
        r_ref, lw_ref, k_ref, v_ref, kk_ref, al_ref = refs[:6]
        q_ref, kp_ref, kc_ref, vp_ref, vc_ref, sink_ref = refs[6:12]
        ex_in = refs[12:12 + nex]
        y_ref, sall_ref, prod_ref, tinv_ref, u_ref, o_ref, p_ref, ps_ref = refs[12 + nex:20 + nex]
        ex_out = refs[20 + nex:20 + 2 * nex]
        s_scr = refs[20 + 2 * nex]
        step = pl.program_id(0)

        @pl.when(step == 0)
        def _():
            s_scr[...] = jnp.zeros_like(s_scr)

        _hosted_exchange(step == 0, step == nchunk - 1, ex_in, ex_out, refs[21 + 2 * nex:], scatter)
        qs = tuple(q_ref[0, :, pair * LANES:(pair + 1) * LANES] for pair in range(apair))
        attn = []
        stages = _attn_block_stages(qs, kp_ref[0], kc_ref[0], vp_ref[0], vc_ref[0], _sink_values(sink_ref),
                                    step % nb == 0, attn)
        ss = tuple(s_scr[i] for i in range(len(chains)))
        for i, s in enumerate(ss):
            sall_ref[0, i] = s
        ys, s_new, kept = _chunk_fn(ss, *[tuple(ref[b, :, cols] for b, _, cols in chains)
                                          for ref in (r_ref, lw_ref, k_ref, v_ref, kk_ref, al_ref)],
                                    tick=lambda: next(stages, None))
        for _ in stages:
            pass
        for i, (b, _, cols) in enumerate(chains):
            y_ref[b, :, cols] = ys[i]
            s_scr[i] = s_new[i]
            prod_ref[0, i], tinv_ref[0, i], u_ref[0, i] = kept[0][i], kept[1][i], kept[2][i]
        outs, probs, psinks = attn
        for pair in range(apair):
            o_ref[0, :, pair * LANES:(pair + 1) * LANES] = outs[pair].astype(o_ref.dtype)
        for h, p in enumerate(probs):
            p_ref[0, 0, h] = p
        ps_ref[0] = _head_columns(psinks)

    spec = pl.BlockSpec((bsz, CHUNK, w), lambda c: (0, c, 0))
    hbm = pl.BlockSpec(memory_space=pltpu.HBM)
    per_chunk = lambda n: pl.BlockSpec((1, len(chains), n, n), lambda c: (c, 0, 0, 0))
    kept_shape = lambda n: jax.ShapeDtypeStruct((nchunk, len(chains), n, n), BF16)
    kcol, vcol = ATTN_WIDTH // KV_WIDTH, ATTN_WIDTH // KV_WIDTH + 1
    before = lambda c: jnp.maximum(c % nb - 1, 0)
    outs = pl.pallas_call(
        body,
        grid=(nchunk,),
        in_specs=[spec] * 6
        + [pl.BlockSpec((1, BLOCK, ATTN_WIDTH), lambda c: (c // nb, c % nb, 0)),
           pl.BlockSpec((1, BLOCK, KV_WIDTH), lambda c: (c // nb, before(c), kcol)),
           pl.BlockSpec((1, BLOCK, KV_WIDTH), lambda c: (c // nb, c % nb, kcol)),
           pl.BlockSpec((1, BLOCK, KV_WIDTH), lambda c: (c // nb, before(c), vcol)),
           pl.BlockSpec((1, BLOCK, KV_WIDTH), lambda c: (c // nb, c % nb, vcol)),
           pl.BlockSpec(sink_rows.shape, lambda c: (0, 0))]
        + [hbm] * nex,
        out_specs=[spec, per_chunk(LANES), per_chunk(4 * CHUNK), per_chunk(2 * CHUNK), per_chunk(2 * CHUNK),
                   pl.BlockSpec((1, BLOCK, ATTN_WIDTH), lambda c: (c // nb, c % nb, 0)),
                   pl.BlockSpec((1, 1, N_ATTN_HEADS, BLOCK, 2 * BLOCK), lambda c: (c // nb, c % nb, 0, 0, 0)),
                   pl.BlockSpec((1, BLOCK, LANES), lambda c: (c // nb, c % nb, 0))]
        + [hbm] * nex,
        out_shape=[jax.ShapeDtypeStruct((bsz, t, w), F32),
                   jax.ShapeDtypeStruct((nchunk, len(chains), LANES, LANES), F32),
                   kept_shape(4 * CHUNK), kept_shape(2 * CHUNK), kept_shape(2 * CHUNK),
                   jax.ShapeDtypeStruct((bsz, t, ATTN_WIDTH), BF16),
                   jax.ShapeDtypeStruct((bsz, nb, N_ATTN_HEADS, BLOCK, 2 * BLOCK), F32),
                   jax.ShapeDtypeStruct((bsz, t, LANES), F32)]
        + _exchange_out_shapes(hosted, scatter),
        scratch_shapes=[pltpu.VMEM((len(chains), LANES, LANES), F32)] + (_exchange_scratch(nex) if nex else []),
        compiler_params=_cparams(("arbitrary",), 48),
        name="rwkv_chunk_swa_fwd",
    )(r, lw, k, v, kk, al, z, z, z, z, z, sink_rows, *hosted)
    return outs[0], outs[1:5], outs[5:8], outs[8:]


def _rwkv_bwd(r, lw, k, v, kk, al, from_fwd, dy, hosted=(), scatter=False):
    bsz, t, w = r.shape
    npair, nchunk = w // LANES, t // CHUNK
    chains = _chains(bsz, npair)
    nex = len(hosted)

    def body(*refs):
        r_ref, lw_ref, k_ref, v_ref, kk_ref, al_ref, s_ref, prod_ref, tinv_ref, u_ref, dy_ref = refs[:11]
        ex_in = refs[11:11 + nex]
        out_refs = refs[11 + nex:17 + nex]
        ex_out = refs[17 + nex:17 + 2 * nex]
        ds_scr = refs[17 + 2 * nex]
        step = pl.program_id(0)

        @pl.when(step == 0)
        def _():
            ds_scr[...] = jnp.zeros_like(ds_scr)

        _hosted_exchange(step == 0, step == nchunk - 1, ex_in, ex_out, refs[18 + 2 * nex:], scatter)
        ss = tuple(s_ref[0, i] for i in range(len(chains)))
        kept = tuple(tuple(ref[0, i] for i in range(len(chains))) for ref in (prod_ref, tinv_ref, u_ref))
        _, vjp = jax.vjp(functools.partial(_chunk_fn, kept=kept), ss,
                         *[tuple(ref[b, :, cols] for b, _, cols in chains)
                           for ref in (r_ref, lw_ref, k_ref, v_ref, kk_ref, al_ref)])
        grads = vjp((tuple(dy_ref[b, :, cols] for b, _, cols in chains),
                     tuple(ds_scr[i] for i in range(len(chains)))))
        for i, (b, _, cols) in enumerate(chains):
            ds_scr[i] = grads[0][i]
            for ref, gval in zip(out_refs, grads[1:]):
                ref[b, :, cols] = gval[i]

    spec = pl.BlockSpec((bsz, CHUNK, w), lambda c: (0, nchunk - 1 - c, 0))
    per_chunk = lambda n: pl.BlockSpec((1, len(chains), n, n), lambda c: (nchunk - 1 - c, 0, 0, 0))
    hbm = pl.BlockSpec(memory_space=pltpu.HBM)
    outs = pl.pallas_call(
        body,
        grid=(nchunk,),
        in_specs=[spec] * 6 + [per_chunk(LANES), per_chunk(4 * CHUNK), per_chunk(2 * CHUNK), per_chunk(2 * CHUNK), spec]
        + [hbm] * nex,
        out_specs=[spec] * 6 + [hbm] * nex,
        out_shape=[jax.ShapeDtypeStruct((bsz, t, w), F32)] * 6 + _exchange_out_shapes(hosted, scatter),
        scratch_shapes=[pltpu.VMEM((len(chains), LANES, LANES), F32)] + (_exchange_scratch(nex) if nex else []),
        compiler_params=_cparams(("arbitrary",), 48),
        name="rwkv_chunk_bwd",
    )(r, lw, k, v, kk, al, *from_fwd, dy, *hosted)
    return outs[:6], outs[6:]


def _alibi_slope(head):
    return 2.0 ** (-8.0 * (head + 1) / N_ATTN_HEADS)


def _attn_setup(first):
    row = lax.broadcasted_iota(jnp.int32, (BLOCK, 2 * BLOCK), 0)
    col = lax.broadcasted_iota(jnp.int32, (BLOCK, 2 * BLOCK), 1)
    lane = lax.broadcasted_iota(jnp.int32, (1, LANES), 1)
    halves = [jnp.where((lane // HEAD_DIM) == half, 1.0, 0.0) for half in range(2)]
    srow = lax.broadcasted_iota(jnp.int32, (LANES, LANES), 0)
    scol = lax.broadcasted_iota(jnp.int32, (LANES, LANES), 1)
    swap = jnp.where((srow + HEAD_DIM) % LANES == scol, 1.0, 0.0)
    dist = row - col + BLOCK
    valid = jnp.logical_and(jnp.logical_and(dist >= 0, dist < BLOCK),
                            jnp.logical_or(col >= BLOCK, jnp.logical_not(first)))
    return halves, swap, dist.astype(F32), valid, HEAD_DIM ** -0.5


def _attn_keys_values(kp, kc, vp, vc, swap, npair):
    stored = (jnp.concatenate([kp, kc], axis=0), jnp.concatenate([vp, vc], axis=0))
    swapped = tuple(_bdot_raw(t, swap, NN) for t in stored)
    heads = [(pair, half) for pair in range(npair) for half in range(2)]
    return heads, [half == pair // 2 for pair, half in heads], stored, swapped


def _attn_block_stages(qs, kp, kc, vp, vc, sinks, first, result):
    halves, swap, dist, valid, scale = _attn_setup(first)
    heads, as_stored, stored, swapped = _attn_keys_values(kp, kc, vp, vc, swap, len(qs))
    kv = [stored if own else swapped for own in as_stored]
    slopes = [_alibi_slope(2 * pair + half) for pair, half in heads]
    qa = [qs[pair] * halves[half] for pair, half in heads]
    yield
    s = [jnp.where(valid, _bdot_raw(q, t[0], NT) * scale - sl * dist, NEG_INF) for q, t, sl in zip(qa, kv, slopes)]
    yield
    mx = [jnp.maximum(jnp.max(a, axis=-1, keepdims=True), sk) for a, sk in zip(s, sinks)]
    yield
    e = [jnp.exp(a - m) for a, m in zip(s, mx)]
    yield
    es = [jnp.exp(sk - m) for sk, m in zip(sinks, mx)]
    inv = [1.0 / (jnp.sum(a, axis=-1, keepdims=True) + b) for a, b in zip(e, es)]
    yield
    probs = [a * i for a, i in zip(e, inv)]
    yield
    o = [_bdot_raw(p, t[1], NN) for p, t in zip(probs, kv)]
    yield
    outs = tuple(o[2 * pair] * halves[0] + o[2 * pair + 1] * halves[1] for pair in range(len(qs)))
    result.extend([outs, probs, [b * i for b, i in zip(es, inv)]])


def _attn_block_bwd(qs, kp, kc, vp, vc, dos, probs, first):
    halves, swap, _, _, scale = _attn_setup(first)
    heads, as_stored, stored, swapped = _attn_keys_values(kp, kc, vp, vc, swap, len(qs))
    kv = [stored if own else swapped for own in as_stored]
    qa = [qs[pair] * halves[half] for pair, half in heads]
    do = [dos[pair] * halves[half] for pair, half in heads]
    dp = [_bdot_raw(d, t[1], NT) for d, t in zip(do, kv)]
    delta = [jnp.sum(p * d, axis=-1, keepdims=True) for p, d in zip(probs, dp)]
    ds = [p * (d - dl) for p, d, dl in zip(probs, dp, delta)]
    dq = [_bdot_raw(g, t[0], NN) * (scale * halves[half]) for g, t, (_, half) in zip(ds, kv, heads)]
    dk = [_bdot_raw(g, q, TN) * scale for g, q in zip(ds, qa)]
    dv = [_bdot_raw(p, d, TN) for p, d in zip(probs, do)]
    dqs = tuple(dq[2 * pair] + dq[2 * pair + 1] for pair in range(len(qs)))

    def total(parts):
        direct = sum(g for g, own in zip(parts, as_stored) if own)
        return direct + _bdot_raw(sum(g for g, own in zip(parts, as_stored) if not own), swap, NN)

    dk_all, dv_all = total(dk), total(dv)
    return dqs, dk_all[:BLOCK], dk_all[BLOCK:], dv_all[:BLOCK], dv_all[BLOCK:], delta


def _sink_values(sink_ref):
    return [jnp.max(sink_ref[h:h + 1, :], axis=-1, keepdims=True) for h in range(N_ATTN_HEADS)]


def _head_columns(cols):
    lane = lax.broadcasted_iota(jnp.int32, (1, LANES), 1)
    return sum(c * jnp.where(lane == h, 1.0, 0.0) for h, c in enumerate(cols))


def _attn_bwd(z, dout, probs, psinks):
    bsz, t, _ = z.shape
    nb = t // BLOCK
    npair = ATTN_WIDTH // LANES

    def body(q_ref, kp_ref, kc_ref, vp_ref, vc_ref, do_ref, p_ref, ps_ref, dz_ref, dsink_ref, carry):
        step = pl.program_id(1)
        n = nb - 1 - step

        @pl.when(step == 0)
        def _():
            carry[...] = jnp.zeros_like(carry)

        @pl.when(jnp.logical_and(step == 0, pl.program_id(0) == 0))
        def _():
            dsink_ref[...] = jnp.zeros_like(dsink_ref)

        qs = tuple(q_ref[0, :, pair * LANES:(pair + 1) * LANES] for pair in range(npair))
        dos = tuple(do_ref[0, :, pair * LANES:(pair + 1) * LANES] for pair in range(npair))
        kept = [p_ref[0, 0, h] for h in range(N_ATTN_HEADS)]
        dqs, dkp, dkc, dvp, dvc, deltas = _attn_block_bwd(qs, kp_ref[0], kc_ref[0], vp_ref[0], vc_ref[0], dos, kept,
                                                          n == 0)
        for pair in range(npair):
            dz_ref[0, :, pair * LANES:(pair + 1) * LANES] = dqs[pair].astype(dz_ref.dtype)
        dsink_ref[...] -= jnp.sum(ps_ref[0] * _head_columns(deltas), axis=0, keepdims=True)
        dz_ref[0, :, ATTN_WIDTH:ATTN_WIDTH + KV_WIDTH] = (dkc + carry[0]).astype(dz_ref.dtype)
        dz_ref[0, :, ATTN_WIDTH + KV_WIDTH:QKV_WIDTH] = (dvc + carry[1]).astype(dz_ref.dtype)
        carry[0] = dkp
        carry[1] = dvp

    kcol, vcol = ATTN_WIDTH // KV_WIDTH, ATTN_WIDTH // KV_WIDTH + 1
    rev = lambda n: nb - 1 - n
    return pl.pallas_call(
        body,
        grid=(bsz, nb),
        in_specs=[pl.BlockSpec((1, BLOCK, ATTN_WIDTH), lambda b, n: (b, rev(n), 0)),
                  pl.BlockSpec((1, BLOCK, KV_WIDTH), lambda b, n: (b, jnp.maximum(rev(n) - 1, 0), kcol)),
                  pl.BlockSpec((1, BLOCK, KV_WIDTH), lambda b, n: (b, rev(n), kcol)),
                  pl.BlockSpec((1, BLOCK, KV_WIDTH), lambda b, n: (b, jnp.maximum(rev(n) - 1, 0), vcol)),
                  pl.BlockSpec((1, BLOCK, KV_WIDTH), lambda b, n: (b, rev(n), vcol)),
                  pl.BlockSpec((1, BLOCK, ATTN_WIDTH), lambda b, n: (b, rev(n), 0)),
                  pl.BlockSpec((1, 1, N_ATTN_HEADS, BLOCK, 2 * BLOCK), lambda b, n: (b, rev(n), 0, 0, 0)),
                  pl.BlockSpec((1, BLOCK, LANES), lambda b, n: (b, rev(n), 0))],
        out_specs=[pl.BlockSpec((1, BLOCK, QKV_WIDTH), lambda b, n: (b, rev(n), 0)),
                   pl.BlockSpec((1, LANES), lambda b, n: (0, 0))],
        out_shape=[jax.ShapeDtypeStruct((bsz, t, QKV_WIDTH), BF16),
                   jax.ShapeDtypeStruct((1, LANES), F32)],
        scratch_shapes=[pltpu.VMEM((2, BLOCK, KV_WIDTH), F32)],
        compiler_params=_cparams(("arbitrary", "arbitrary"), 48),
        name="swa_bwd",
    )(z, z, z, z, z, dout, probs, psinks)


def _exchange_out_shapes(arrays, scatter):
    return [jax.ShapeDtypeStruct((N_DEV,) + (a.shape[1:] if scatter else a.shape), a.dtype) for a in arrays]


def _exchange_scratch(n):
    return [pltpu.SemaphoreType.DMA((n, N_DEV - 1)), pltpu.SemaphoreType.DMA((n, N_DEV - 1)),
            pltpu.SemaphoreType.DMA((n,))]


def _exchange_copies(ins, outs, send_sems, recv_sems, local_sems, scatter, arrivals=True):
    x, y, c = lax.axis_index("x"), lax.axis_index("y"), lax.axis_index("c")
    me = 4 * x + 2 * y + c
    copies = []
    for i in range(len(ins)):
        own = pltpu.make_async_copy(ins[i].at[me] if scatter else ins[i], outs[i].at[me], local_sems.at[i])
        copies.append((own, None, True))
        for d in range(1, N_DEV):
            px = 1 - x if d & 4 else x
            py = 1 - y if d & 2 else y
            pc = 1 - c if d & 1 else c
            peer = 4 * px + 2 * py + pc
            src = ins[i].at[peer] if scatter else ins[i]
            send = pltpu.make_async_remote_copy(src, outs[i].at[me], send_sems.at[i, d - 1], recv_sems.at[i, d - 1],
                                                device_id=(px, py, pc), device_id_type=MESH)
            recv = pltpu.make_async_remote_copy(src, outs[i].at[peer], send_sems.at[i, d - 1], recv_sems.at[i, d - 1],
                                                device_id=(px, py, pc), device_id_type=MESH) if arrivals else None
            copies.append((send, recv, False))
    return copies


def _exchange_start(copies):
    for send, _, _ in copies:
        send.start()


def _exchange_wait(copies):
    for send, recv, local in copies:
        if local:
            send.wait()
        else:
            send.wait_send()
            recv.wait_recv()


def _gather_two_level(arrays, name):
    n = len(arrays)

    def body(*refs):
        ins, outs = refs[:n], refs[n:2 * n]
        send_sems, recv_sems, local_sems = refs[2 * n:]
        x, y, c = lax.axis_index("x"), lax.axis_index("y"), lax.axis_index("c")
        index = lambda px, py, pc: 4 * px + 2 * py + pc
        sibling = (x, y, 1 - c)
        chips = [(1 - x, y), (x, 1 - y), (1 - x, 1 - y)]

        def copy(i, k, block, to, src=None):
            slot = outs[i].at[index(*block)]
            return pltpu.make_async_remote_copy(slot if src is None else src, slot, send_sems.at[i, k],
                                                recv_sems.at[i, k], device_id=to, device_id_type=MESH)

        local, sends = [], []
        for i in range(n):
            own = pltpu.make_async_copy(ins[i], outs[i].at[index(x, y, c)], local_sems.at[i])
            own.start()
            local.append(own)
            first = [copy(i, 0, (x, y, c), sibling, src=ins[i])]
            first += [copy(i, 1 + j, (x, y, c), (*chip, c), src=ins[i]) for j, chip in enumerate(chips)]
            for cp in first:
                cp.start()
            sends += first
        for i in range(n):
            for j, chip in enumerate(chips):
                copy(i, 1 + j, (*chip, c), (x, y, c)).wait_recv()
                onward = copy(i, 4 + j, (*chip, c), sibling)
                onward.start()
                sends.append(onward)
        for i in range(n):
            copy(i, 0, sibling, (x, y, c)).wait_recv()
            for j, chip in enumerate(chips):
                copy(i, 4 + j, (*chip, 1 - c), (x, y, c)).wait_recv()
        for cp in sends:
            cp.wait_send()
        for cp in local:
            cp.wait()

    hbm = pl.BlockSpec(memory_space=pltpu.HBM)
    return pl.pallas_call(
        body,
        in_specs=[hbm] * n,
        out_specs=[hbm] * n,
        out_shape=_exchange_out_shapes(arrays, False),
        scratch_shapes=_exchange_scratch(n),
        name=name,
    )(*arrays)


def _exchange(arrays, *, scatter, name):
    n = len(arrays)

    def body(*refs):
        copies = _exchange_copies(refs[:n], refs[n:2 * n], *refs[2 * n:], scatter)
        _exchange_start(copies)
        _exchange_wait(copies)

    hbm = pl.BlockSpec(memory_space=pltpu.HBM)
    return pl.pallas_call(
        body,
        in_specs=[hbm] * n,
        out_specs=[hbm] * n,
        out_shape=_exchange_out_shapes(arrays, scatter),
        scratch_shapes=_exchange_scratch(n),
        name=name,
    )(*arrays)


def _adamw(parts, w, m, v, name):
    rows, cols = w.shape
    tr = _pick(rows, (256, 128, 64, 8))
    c1 = 1.0 / (1.0 - ADAM_B1 ** ADAM_STEP)
    c2 = 1.0 / (1.0 - ADAM_B2 ** ADAM_STEP)

    def body(p_ref, w_ref, m_ref, v_ref, g_ref, d_ref, mo_ref, vo_ref):
        g = p_ref[0].astype(F32)
        for s in range(1, N_DEV):
            g = g + p_ref[s].astype(F32)
        mn = ADAM_B1 * m_ref[...] + (1.0 - ADAM_B1) * g
        vn = ADAM_B2 * v_ref[...] + (1.0 - ADAM_B2) * (g * g)
        g_ref[...] = g
        mo_ref[...] = mn
        vo_ref[...] = vn
        d_ref[...] = -ADAM_LR * ((mn * c1) / (jnp.sqrt(vn * c2) + ADAM_EPS) + ADAM_WD * w_ref[...])

    spec = pl.BlockSpec((tr, cols), lambda i: (i, 0))
    return pl.pallas_call(
        body,
        grid=(rows // tr,),
        in_specs=[pl.BlockSpec((N_DEV, tr, cols), lambda i: (0, i, 0)), spec, spec, spec],
        out_specs=[spec] * 4,
        out_shape=[jax.ShapeDtypeStruct((rows, cols), F32)] * 4,
        compiler_params=_cparams(("parallel",), 48),
        name=name,
    )(parts, w, m, v)


_VECTOR_PARAMS = ("attn_norm_g", "attn_sinks", "rwkv_mu", "w0", "a0", "k_k", "k_a", "r_k", "ln_x_w", "ln_x_b",
                  "mlp_norm_g", "final_norm_g")
_WEIGHT_NAMES = ("attn_norm_g", "w_in", "attn_sinks", "rwkv_mu", "w0", "w2", "a0", "a2", "g2", "k_k", "k_a", "r_k",
                 "ln_x_w", "ln_x_b", "w_out", "mlp_norm_g", "w_up", "w_down", "final_norm_g")


def _pack_vectors(vals):
    pieces = []
    for name in _VECTOR_PARAMS:
        flat = vals[name].reshape(1, -1)
        pad = (-flat.shape[1]) % LANES
        pieces.append(jnp.pad(flat, ((0, 0), (0, pad))) if pad else flat)
    return jnp.concatenate(pieces, axis=1)


def _unpack_vectors(packed, like):
    out, col = {}, 0
    for name in _VECTOR_PARAMS:
        size = like[name].size
        out[name] = packed[0, col:col + size].reshape(like[name].shape)
        col += size + (-size) % LANES
    return out


def kernel(x, attn_norm_g, w_in, attn_sinks, rwkv_mu, w0, w2, a0, a2, g2, k_k, k_a, r_k, ln_x_w, ln_x_b, w_out, mlp_norm_g, w_up, w_down, final_norm_g, loss_target, m_attn_norm_g, m_w_in, m_attn_sinks, m_rwkv_mu, m_w0, m_w2, m_a0, m_a2, m_g2, m_k_k, m_k_a, m_r_k, m_ln_x_w, m_ln_x_b, m_w_out, m_mlp_norm_g, m_w_up, m_w_down, m_final_norm_g, v_attn_norm_g, v_w_in, v_attn_sinks, v_rwkv_mu, v_w0, v_w2, v_a0, v_a2, v_g2, v_k_k, v_k_a, v_r_k, v_ln_x_w, v_ln_x_b, v_w_out, v_mlp_norm_g, v_w_up, v_w_down, v_final_norm_g):
    weights = dict(attn_norm_g=attn_norm_g, w_in=w_in, attn_sinks=attn_sinks, rwkv_mu=rwkv_mu, w0=w0, w2=w2, a0=a0,
                   a2=a2, g2=g2, k_k=k_k, k_a=k_a, r_k=r_k, ln_x_w=ln_x_w, ln_x_b=ln_x_b, w_out=w_out,
                   mlp_norm_g=mlp_norm_g, w_up=w_up, w_down=w_down, final_norm_g=final_norm_g)
    mom1 = dict(attn_norm_g=m_attn_norm_g, w_in=m_w_in, attn_sinks=m_attn_sinks, rwkv_mu=m_rwkv_mu, w0=m_w0, w2=m_w2,
                a0=m_a0, a2=m_a2, g2=m_g2, k_k=m_k_k, k_a=m_k_a, r_k=m_r_k, ln_x_w=m_ln_x_w, ln_x_b=m_ln_x_b,
                w_out=m_w_out, mlp_norm_g=m_mlp_norm_g, w_up=m_w_up, w_down=m_w_down, final_norm_g=m_final_norm_g)
    mom2 = dict(attn_norm_g=v_attn_norm_g, w_in=v_w_in, attn_sinks=v_attn_sinks, rwkv_mu=v_rwkv_mu, w0=v_w0, w2=v_w2,
                a0=v_a0, a2=v_a2, g2=v_g2, k_k=v_k_k, k_a=v_k_a, r_k=v_r_k, ln_x_w=v_ln_x_w, ln_x_b=v_ln_x_b,
                w_out=v_w_out, mlp_norm_g=v_mlp_norm_g, w_up=v_w_up, w_down=v_w_down, final_norm_g=v_final_norm_g)
    bsz, seq, d_model = x.shape
    rows = bsz * seq
    d_in = N_DEV * w_in.shape[2]
    d_ff = N_DEV * w_up.shape[2]

    gathered = _gather_two_level([w_in[0].astype(BF16), w2[0], a2[0], g2[0]], name="gather_in_weights")
    cols_first = lambda a: a.transpose(1, 0, 2).reshape(a.shape[1], -1)
    w_in_f = cols_first(gathered[0])
    w_attn, w_rw = w_in_f[:, :QKV_WIDTH], w_in_f[:, QKV_WIDTH:]
    w2_f, a2_f, g2_f = cols_first(gathered[1]), cols_first(gathered[2]), cols_first(gathered[3])
    lora = w2_f.shape[0]
    w2p = jnp.concatenate([w2_f, jnp.zeros_like(a2_f)], axis=0)
    a2p = jnp.concatenate([jnp.zeros_like(w2_f), a2_f], axis=0)

    esum = _head_sum_matrix()
    sink_rows = jnp.broadcast_to(attn_sinks.reshape(N_ATTN_HEADS, 1), (N_ATTN_HEADS, LANES))
    prep_pars = [w0, w2p, a0, a2p, g2_f, k_k, k_a, esum]
    post_pars = [ln_x_w, ln_x_b, r_k, esum]

    x2d = x.reshape(rows, d_model)
    h1, z_attn, z_rw, r, lw, k, v, kk, al, gate = _norm_in_proj_prep(x2d, seq, attn_norm_g, w_attn, w_rw, rwkv_mu,
                                                                       prep_pars)
    z_attn3 = z_attn.reshape(bsz, seq, QKV_WIDTH)
    as3 = lambda a: a.reshape(bsz, seq, RWKV_WIDTH)
    y, from_fwd, (attn_out, attn_probs, attn_psinks), late = _rwkv_attn_fwd(
        as3(r), as3(lw), as3(k), as3(v), as3(kk), as3(al), z_attn3, sink_rows,
        hosted=[w_out[0].astype(BF16), w_up[0].astype(BF16), w_down[0].astype(BF16)])
    w_out_f = late[0].reshape(-1, d_model)
    w_up_f = cols_first(late[1])
    w_down_f = late[2].reshape(-1, d_model)
    y2 = y.reshape(rows, RWKV_WIDTH)
    attn_out2d = attn_out.reshape(rows, ATTN_WIDTH)
    w_out_attn, w_out_rw = w_out_f[:ATTN_WIDTH], w_out_f[ATTN_WIDTH:]
    rw_out, x1, h2 = _post_out_proj_norm(y2, r, k, v, gate, attn_out2d, x2d, post_pars, w_out_attn, w_out_rw,
                                         mlp_norm_g)

    def relu_sq(acc):
        pos = jnp.maximum(acc, 0.0)
        return acc, pos * pos

    u, act = _matmul(h2, w_up_f, "nn", name="mlp_up", epilogue=relu_sq, out_dtypes=(BF16, BF16))
    dx2, loss_vec, g_final = _down_proj_loss(act, w_down_f, x1, loss_target.reshape(rows, d_model),
                                             final_norm_g.reshape(1, d_model))

    g_w_down = _matmul(act, dx2, "tn", name="grad_w_down", out_dtypes=(BF16,))
    du = _matmul(dx2, w_down_f, "nt", name="mlp_down_bwd", extras=(u,), out_dtypes=(BF16,),
                 epilogue=lambda acc, uv: (acc * (2.0 * jnp.maximum(uv.astype(F32), 0.0)),))
    g_w_up = _matmul(h2, du, "tn", name="grad_w_up", out_dtypes=(BF16,))
    dx1, g_mlp_norm = _proj_bwd_norm_bwd([du], [w_up_f], x1, dx2, mlp_norm_g, "mlp_up_bwd_norm_bwd")
    g_w_out = jnp.concatenate([_matmul(attn_out2d, dx1, "tn", name="grad_w_out_attn", out_dtypes=(BF16,)),
                               _matmul(rw_out, dx1, "tn", name="grad_w_out_rwkv", out_dtypes=(BF16,))], axis=0)
    d_attn_out, dy, dr_a, dk_a, dv_a, dgate, g_ln_w, g_ln_b, g_r_k = _out_proj_bwd_post_bwd(
        dx1, y2, r, k, v, gate, post_pars, w_out_attn, w_out_rw)
    by_cols = lambda a: a.reshape(a.shape[0], N_DEV, -1).transpose(1, 0, 2)
    (dr_b, dlw, dk_b, dv_b, dkk, dal), (p_w_out, p_w_up, p_w_down) = _rwkv_bwd(
        as3(r), as3(lw), as3(k), as3(v), as3(kk), as3(al), from_fwd, as3(dy),
        hosted=[g_w_out.reshape(N_DEV, -1, d_model), by_cols(g_w_up), g_w_down.reshape(N_DEV, -1, d_model)],
        scatter=True)
    flat = lambda a: a.reshape(rows, RWKV_WIDTH)
    (dz_rw, gmu_r, gmu_k, gmu_v, gmu_wa, gmu_g, g_w0, g_w2p, g_a0, g_a2p, g_g2, g_k_k, g_k_a) = _prep_bwd(
        z_rw, seq, [dr_a, flat(dr_b), flat(dlw), dk_a, flat(dk_b), dv_a, flat(dv_b), flat(dkk), flat(dal), dgate],
        rwkv_mu, prep_pars)
    dz_attn, g_sink_lanes = _attn_bwd(z_attn3, d_attn_out.reshape(bsz, seq, ATTN_WIDTH), attn_probs, attn_psinks)
    dz_attn = dz_attn.reshape(rows, QKV_WIDTH)
    g_w_in = jnp.concatenate([_matmul(h1, dz_attn, "tn", name="grad_w_in_attn", out_dtypes=(BF16,)),
                              _matmul(h1, dz_rw, "tn", name="grad_w_in_rwkv", out_dtypes=(BF16,))], axis=1)
    lora_grads = jnp.concatenate([g_w2p[:lora], g_a2p[lora:], g_g2], axis=0)
    (dx, g_attn_norm), (p_w_in, p_lora) = _proj_bwd_norm_bwd(
        [dz_attn, dz_rw], [w_attn, w_rw], x2d, dx1, attn_norm_g, "in_proj_bwd_norm_bwd",
        hosted=[by_cols(g_w_in), by_cols(lora_grads)], scatter=True)

    vec_grads = dict(attn_norm_g=g_attn_norm, attn_sinks=g_sink_lanes[0, :N_ATTN_HEADS], rwkv_mu=jnp.concatenate(
        [gmu_r, gmu_k, gmu_v, gmu_wa, gmu_g], axis=1), w0=g_w0, a0=g_a0, k_k=g_k_k, k_a=g_k_a, r_k=g_r_k,
        ln_x_w=g_ln_w, ln_x_b=g_ln_b, mlp_norm_g=g_mlp_norm, final_norm_g=g_final)
    packed = _pack_vectors(vec_grads)
    nvec = packed.shape[1]
    everyone = _exchange([jnp.concatenate([packed, loss_vec], axis=1)], scatter=False, name="gather_vector_grads")[0]
    vec_parts = everyone[:, :, :nvec]
    loss = jnp.sum(everyone[:, 0, nvec])

    grads, delta, new_m, new_v = {}, {}, {}, {}

    def update(name, part, shape2d):
        res = _adamw(part, weights[name].reshape(shape2d), mom1[name].reshape(shape2d), mom2[name].reshape(shape2d),
                     "adamw_" + name)
        for store, val in zip((grads, delta, new_m, new_v), res):
            store[name] = val.reshape(weights[name].shape)

    update("w_in", p_w_in, w_in.shape[1:])
    update("w_out", p_w_out, w_out.shape[1:])
    update("w_up", p_w_up, w_up.shape[1:])
    update("w_down", p_w_down, w_down.shape[1:])
    stack = lambda d: jnp.concatenate([d["w2"][0], d["a2"][0], d["g2"][0]], axis=0)
    lora_res = _adamw(p_lora, stack(weights), stack(mom1), stack(mom2), "adamw_lora")
    for store, val in zip((grads, delta, new_m, new_v), lora_res):
        store["w2"], store["a2"], store["g2"] = val[None, :lora], val[None, lora:2 * lora], val[None, 2 * lora:]
    vec_res = _adamw(vec_parts, _pack_vectors(weights), _pack_vectors(mom1), _pack_vectors(mom2), "adamw_vectors")
    for store, val in zip((grads, delta, new_m, new_v), vec_res):
        store.update(_unpack_vectors(val, weights))

    return (loss, dx.reshape(x.shape), *[grads[n] for n in _WEIGHT_NAMES], *[delta[n] for n in _WEIGHT_NAMES],
            *[new_m[n] for n in _WEIGHT_NAMES], *[new_v[n] for n in _WEIGHT_NAMES])
```

```python
import functools
import math

import jax
import jax.numpy as jnp
from jax import lax
from jax.experimental import pallas as pl
from jax.experimental.pallas import tpu as pltpu

F32 = jnp.float32
BF16 = jnp.bfloat16

N_DEV = 8
HEAD_DIM = 64
LANES = 128
N_ATTN_HEADS = 8
ATTN_WIDTH = 512
KV_WIDTH = 128
QKV_WIDTH = ATTN_WIDTH + 2 * KV_WIDTH
RWKV_WIDTH = 512
LORA_WA = 128
GATE_LORA = 128
RWKV_SHIFT_WIDTH = 3 * RWKV_WIDTH + LORA_WA + GATE_LORA
BLOCK = 128
CHUNK = 64
RMS_EPS = 1e-6
GN_EPS = 64e-5
L2_EPS = 1e-12
NEG_INF = -1e30
DECAY_SCALE = math.exp(-0.5)
ADAM_LR, ADAM_B1, ADAM_B2, ADAM_EPS, ADAM_WD, ADAM_STEP = 0.001, 0.9, 0.999, 1e-08, 0.01, 10

NN = (((1,), (0,)), ((), ()))
NT = (((1,), (1,)), ((), ()))
TN = (((0,), (0,)), ((), ()))
MESH = pl.DeviceIdType.MESH


def _dot(a, b, dn=NN, precision=None):
    return lax.dot_general(a, b, dn, precision=precision, preferred_element_type=F32)


def _bdot_raw(a, b, dn):
    return lax.dot_general(a.astype(BF16), b.astype(BF16), dn, preferred_element_type=F32)


@functools.partial(jax.custom_vjp, nondiff_argnums=(2, 3))
def _bdot_c(a, b, ca, cb):
    return _bdot_raw(a, b, (((ca,), (cb,)), ((), ())))


def _bdot_c_fwd(a, b, ca, cb):
    return _bdot_c(a, b, ca, cb), (a, b)


_BESIDE_BACKWARD = [None]


def _beside_backward():
    if _BESIDE_BACKWARD[0] is not None:
        _BESIDE_BACKWARD[0]()


def _bdot_c_bwd(ca, cb, res, ct):
    _beside_backward()
    a, b = res
    fa, fb = 1 - ca, 1 - cb
    da = _bdot_raw(ct, b, (((1,), (fb,)), ((), ()))) if ca == 1 else _bdot_raw(b, ct, (((fb,), (1,)), ((), ())))
    db = _bdot_raw(a, ct, (((fa,), (0,)), ((), ()))) if cb == 0 else _bdot_raw(ct, a, (((0,), (fa,)), ((), ())))
    return da, db


_bdot_c.defvjp(_bdot_c_fwd, _bdot_c_bwd)


def _bdot(a, b, dn=NN):
    return _bdot_c(a, b, dn[0][0][0], dn[0][1][0])


def _bdot_nn(a, b):
    return _bdot(a, b, NN)


def _bdot_nt(a, b):
    return _bdot(a, b, NT)


def _split3(x):
    hi = x.astype(BF16)
    rest = x - hi.astype(F32)
    mid = rest.astype(BF16)
    return hi, mid, (rest - mid.astype(F32)).astype(BF16)


def _running_sum(x, dn):
    c = x.shape[0]
    row = lax.broadcasted_iota(jnp.int32, (c, c), 0)
    col = lax.broadcasted_iota(jnp.int32, (c, c), 1)
    tri = jnp.where(row >= col, 1.0, 0.0).astype(BF16)
    w = x.shape[1]
    parts = lax.dot_general(tri, jnp.concatenate(_split3(x), axis=1), dn, preferred_element_type=F32)
    return parts[:, :w] + parts[:, w:2 * w] + parts[:, 2 * w:]


@jax.custom_vjp
def _cumsum_rows(x):
    return _running_sum(x, NN)


_cumsum_rows.defvjp(lambda x: (_running_sum(x, NN), None), lambda _, ct: (_running_sum(ct, TN),))


@jax.custom_vjp
def _fold_rows(x):
    c = x.shape[0] // 2
    return x[:c] + x[c:]


_fold_rows.defvjp(lambda x: (_fold_rows(x), None), lambda _, ct: (jnp.concatenate([ct, ct], axis=0),))


@jax.custom_vjp
def _halves(x):
    n = x.shape[0] // 2
    return x[:n], x[n:]


_halves.defvjp(lambda x: (_halves(x), None), lambda _, cts: (jnp.concatenate(cts, axis=0),))


@jax.custom_vjp
def _quarters(x):
    n = x.shape[0] // 2
    return x[:n, :n], x[:n, n:], x[n:, :n], x[n:, n:]


_quarters.defvjp(lambda x: (_quarters(x), None),
                 lambda _, cts: (jnp.concatenate([jnp.concatenate(cts[:2], axis=1),
                                                  jnp.concatenate(cts[2:], axis=1)], axis=0),))


@jax.custom_vjp
def _sigmoid(x):
    return 1.0 / (1.0 + jnp.exp(-x))


def _sigmoid_fwd(x):
    s = _sigmoid(x)
    return s, s


_sigmoid.defvjp(_sigmoid_fwd, lambda s, ct: (ct * s * (1.0 - s),))


def _pick(n, cands):
    for c in cands:
        if n % c == 0:
            return c
    return n


def _cparams(sem, vmem_mb=None):
    kw = dict(dimension_semantics=sem)
    if vmem_mb is not None:
        kw["vmem_limit_bytes"] = vmem_mb * 1024 * 1024
    return pltpu.CompilerParams(**kw)


def _matmul(a, b, mode, *, name, extras=(), epilogue=None, out_dtypes=(F32,), tm=1024, tn=1024, tk=1024,
            hosted=(), scatter=False):
    if mode == "nn":
        (M, K), (_, N) = a.shape, b.shape
    elif mode == "tn":
        (K, M), (_, N) = a.shape, b.shape
    else:
        (M, K), (N, _) = a.shape, b.shape
    tm = _pick(M, (tm, 512, 256, 128))
    tn = _pick(N, (tn, 896, 768, 512, 384, 256, 128))
    tk = _pick(K, (tk, 512, 256, 128))
    nk = K // tk
    ne, nout = len(extras), len(out_dtypes)
    if mode == "nn":
        a_spec = pl.BlockSpec((tm, tk), lambda i, j, k: (i, k))
        b_spec = pl.BlockSpec((tk, tn), lambda i, j, k: (k, j))
        dn = NN
    elif mode == "tn":
        a_spec = pl.BlockSpec((tk, tm), lambda i, j, k: (k, i))
        b_spec = pl.BlockSpec((tk, tn), lambda i, j, k: (k, j))
        dn = TN
    else:
        a_spec = pl.BlockSpec((tm, tk), lambda i, j, k: (i, k))
        b_spec = pl.BlockSpec((tn, tk), lambda i, j, k: (j, k))
        dn = NT
    o_spec = pl.BlockSpec((tm, tn), lambda i, j, k: (i, j))
    grid = (M // tm, N // tn, nk)
    nex = len(hosted)

    def body(*refs):
        a_ref, b_ref = refs[:2]
        e_refs = refs[2:2 + ne]
        ex_in = refs[2 + ne:2 + ne + nex]
        o_refs = refs[2 + ne + nex:2 + ne + nex + nout]
        ex_out = refs[2 + ne + nex + nout:2 + ne + 2 * nex + nout]
        scratch = refs[2 + ne + 2 * nex + nout:]
        kstep = pl.program_id(2)
        if nex:
            at = [pl.program_id(d) for d in range(3)]
            first = jnp.logical_and(jnp.logical_and(at[0] == 0, at[1] == 0), at[2] == 0)
            last = jnp.logical_and(jnp.logical_and(at[0] == grid[0] - 1, at[1] == grid[1] - 1), at[2] == grid[2] - 1)
            _hosted_exchange(first, last, ex_in, ex_out, scratch[-3:], scatter)

        def finish(total):
            outs = (total,) if epilogue is None else epilogue(total, *[e[...] for e in e_refs])
            for o_ref, o in zip(o_refs, outs):
                o_ref[...] = o.astype(o_ref.dtype)

        if nk == 1:
            finish(_bdot_raw(a_ref[...], b_ref[...], dn))
            return
        acc = scratch[0]

        @pl.when(kstep == 0)
        def _():
            acc[...] = jnp.zeros_like(acc)

        acc[...] += _bdot_raw(a_ref[...], b_ref[...], dn)

        @pl.when(kstep == nk - 1)
        def _():
            finish(acc[...])

    hbm = pl.BlockSpec(memory_space=pltpu.HBM)
    outs = pl.pallas_call(
        body,
        grid=grid,
        in_specs=[a_spec, b_spec] + [o_spec] * ne + [hbm] * nex,
        out_specs=[o_spec] * nout + [hbm] * nex,
        out_shape=[jax.ShapeDtypeStruct((M, N), dt) for dt in out_dtypes] + _exchange_out_shapes(hosted, scatter),
        scratch_shapes=([pltpu.VMEM((tm, tn), F32)] if nk > 1 else []) + (_exchange_scratch(nex) if nex else []),
        compiler_params=_cparams(("arbitrary",) * 3 if nex else ("parallel", "parallel", "arbitrary"), 56),
        name=name,
    )(a, b, *extras, *hosted)
    if nex:
        return outs[:nout], outs[nout:]
    return outs[0] if nout == 1 else outs


def _rowwise(fn, rows, pars, out_rows, out_accs, *, tile, name, nsub=1, hosted=(), scatter=False):
    rows = [r if isinstance(r, tuple) else (r, r.shape[1], 0) for r in rows]
    R = rows[0][0].shape[0]
    tile = min(tile, R)
    chunk = tile // nsub
    ntile = R // tile
    nr, npar, nor, noa, nex = len(rows), len(pars), len(out_rows), len(out_accs), len(hosted)

    def body(*refs):
        rin = refs[:nr]
        pin = refs[nr:nr + npar]
        ex_in = refs[nr + npar:nr + npar + nex]
        orow = refs[nr + npar + nex:nr + npar + nex + nor]
        oacc = refs[nr + npar + nex + nor:nr + npar + nex + nor + noa]
        ex_out = refs[nr + npar + nex + nor + noa:nr + npar + 2 * nex + nor + noa]
        step = pl.program_id(0)
        _hosted_exchange(step == 0, step == ntile - 1, ex_in, ex_out, refs[nr + npar + 2 * nex + nor + noa:], scatter)
        pvals = [p[...] for p in pin]
        totals = []
        for sub in range(nsub):
            at = slice(sub * chunk, (sub + 1) * chunk)
            outs = fn(*[r[at, :] for r in rin], *pvals)
            for ref, o in zip(orow, outs[:nor]):
                if isinstance(o, (tuple, list)):
                    col = 0
                    for piece in o:
                        ref[at, col:col + piece.shape[1]] = piece.astype(ref.dtype)
                        col += piece.shape[1]
                else:
                    ref[at, :] = o.astype(ref.dtype)
            accs = list(outs[nor:])
            totals = accs if sub == 0 else [t + a for t, a in zip(totals, accs)]

        def accumulate(ref, o):
            @pl.when(step == 0)
            def _():
                ref[...] = o

            @pl.when(step > 0)
            def _():
                ref[...] += o

        for ref, o in zip(oacc, totals):
            accumulate(ref, o)

    def colspec(width, cb):
        return pl.BlockSpec((tile, width), lambda i: (i, cb))

    hbm = pl.BlockSpec(memory_space=pltpu.HBM)
    outs = pl.pallas_call(
        body,
        grid=(ntile,),
        in_specs=[colspec(w, cb) for (_, w, cb) in rows]
        + [pl.BlockSpec(p.shape, lambda i: (0, 0), pipeline_mode=pl.Buffered(1)) for p in pars] + [hbm] * nex,
        out_specs=[colspec(w, 0) for (w, _) in out_rows]
        + [pl.BlockSpec(s, lambda i: (0, 0)) for s in out_accs] + [hbm] * nex,
        out_shape=[jax.ShapeDtypeStruct((R, w), dt) for (w, dt) in out_rows]
        + [jax.ShapeDtypeStruct(s, F32) for s in out_accs] + _exchange_out_shapes(hosted, scatter),
        scratch_shapes=_exchange_scratch(nex) if nex else [],
        compiler_params=_cparams(("arbitrary",), 56),
        name=name,
    )(*[r[0] for r in rows], *pars, *hosted)
    return (outs[:nor + noa], outs[nor + noa:]) if nex else outs


def _rms_fn(x, g):
    return x * lax.rsqrt(jnp.mean(x * x, axis=-1, keepdims=True) + RMS_EPS) * g


FUSED_TILE = 512
FUSED_CHUNKS = 2


def _down_proj_loss(act, w_down, x1, tgt, g):
    d = x1.shape[1]

    def fn(av, xv, tv, wv, gv):
        x2 = xv + _bdot_raw(av, wv, NN)
        y, vjp = jax.vjp(_rms_fn, x2, gv)
        err = y - tv
        loss = 0.5 * jnp.sum(jnp.sum(err * err, axis=-1, keepdims=True), axis=0, keepdims=True) / d
        dx, dg = vjp(err / d)
        return dx, jnp.broadcast_to(loss, (1, LANES)), dg

    return _rowwise(fn, [act, x1, tgt], [w_down, g], [(d, F32)], [(1, LANES), g.shape],
                    tile=FUSED_TILE, nsub=FUSED_CHUNKS, name="mlp_down_final_norm_loss")


def _proj_bwd_norm_bwd(cts, weights_t, x, dres, g, name, hosted=(), scatter=False):
    n = len(cts)

    def fn(*vals):
        ctv, (xv, dresv), wv, gv = vals[:n], vals[n:n + 2], vals[n + 2:2 * n + 2], vals[-1]
        dh = _bdot_raw(ctv[0], wv[0], NT)
        for c, w in zip(ctv[1:], wv[1:]):
            dh = dh + _bdot_raw(c, w, NT)
        _, vjp = jax.vjp(_rms_fn, xv, gv)
        dx, dg = vjp(dh)
        return dx + dresv, dg

    return _rowwise(fn, [*cts, x, dres], [*weights_t, g], [(x.shape[1], F32)], [g.shape],
                    tile=FUSED_TILE, nsub=FUSED_CHUNKS, name=name, hosted=hosted, scatter=scatter)


def _head_sum_matrix():
    i = lax.broadcasted_iota(jnp.int32, (RWKV_WIDTH, RWKV_WIDTH), 0) // HEAD_DIM
    j = lax.broadcasted_iota(jnp.int32, (RWKV_WIDTH, RWKV_WIDTH), 1) // HEAD_DIM
    return (i == j).astype(BF16)


def _head_sums_raw(x, esum):
    hi = x.astype(BF16)
    lo = (x - hi.astype(F32)).astype(BF16)
    return _dot(hi, esum) + _dot(lo, esum)


@jax.custom_vjp
def _head_sums(x, esum):
    return _head_sums_raw(x, esum)


_head_sums.defvjp(lambda x, esum: (_head_sums_raw(x, esum), esum),
                  lambda esum, ct: (_head_sums_raw(ct, esum), jnp.zeros_like(esum)))


def _prep_core(xr, xk, xv, xwa, xg, w0, w2p, a0, a2p, g2, k_k, k_a, esum):
    lw = -DECAY_SCALE * _sigmoid(w0 + _bdot_nn(jnp.tanh(xwa), w2p))
    a = _sigmoid(a0 + _bdot_nn(xwa, a2p))
    g = _bdot_nn(_sigmoid(xg), g2)
    kk0 = xk * k_k
    kk = kk0 * jnp.minimum(lax.rsqrt(_head_sums(kk0 * kk0, esum)), 1.0 / L2_EPS)
    k = xk * (1.0 + (a - 1.0) * k_a)
    return xr, lw, k, xv, kk, a, g


_SEGS = ((0, 512), (512, 1024), (1024, 1536), (1536, 1664), (1664, 1792))


PREP_TILE = 256
SUBLANES = 8


def _shifted_tokens(z_ref, zprev_ref, tile_index, seq):
    zc = z_ref[...]
    start = (tile_index * PREP_TILE) % seq == 0
    before = jnp.where(start, 0.0, zprev_ref[SUBLANES - 1:SUBLANES, :])
    rowid = lax.broadcasted_iota(jnp.int32, zc.shape, 0)
    return zc, jnp.where(rowid == 0, before, pltpu.roll(zc, 1, 0))


def _prep_specs(z, mu, pars, index):
    width = z.shape[1]
    per = PREP_TILE // SUBLANES
    return ([pl.BlockSpec((PREP_TILE, width), lambda i: (index(i), 0)),
             pl.BlockSpec((SUBLANES, width), lambda i: (jnp.maximum(index(i) * per - 1, 0), 0))],
            [pl.BlockSpec(p.shape, lambda i: (0, 0)) for p in (mu, *pars)])


def _norm_in_proj_prep(x, seq, g, w_attn, w_rw, mu, pars):
    rows, d = x.shape
    chunk = FUSED_TILE // FUSED_CHUNKS
    npar = len(pars)
    wa_width, wr_width = w_attn.shape[1], w_rw.shape[1]

    def body(x_ref, g_ref, wa_ref, wr_ref, mu_ref, *rest):
        par_refs = rest[:npar]
        h_ref, za_ref, zr_ref = rest[npar:npar + 3]
        out_refs, carry = rest[npar + 3:-1], rest[-1]
        step = pl.program_id(0)

        @pl.when(step == 0)
        def _():
            carry[...] = jnp.zeros_like(carry)

        pv = [p[...] for p in par_refs]
        for sub in range(FUSED_CHUNKS):
            at = slice(sub * chunk, (sub + 1) * chunk)
            h = _rms_fn(x_ref[at, :], g_ref[...])
            h_ref[at, :] = h.astype(h_ref.dtype)
            za_ref[at, :] = _bdot_raw(h, wa_ref[...], NN)
            zc = _bdot_raw(h, wr_ref[...], NN)
            zr_ref[at, :] = zc
            start = (step * FUSED_TILE + sub * chunk) % seq == 0
            before = jnp.where(start, 0.0, carry[SUBLANES - 1:SUBLANES, :])
            rowid = lax.broadcasted_iota(jnp.int32, zc.shape, 0)
            zp = jnp.where(rowid == 0, before, pltpu.roll(zc, 1, 0))
            carry[...] = zc[chunk - SUBLANES:chunk, :]
            zs = zc + (zp - zc) * mu_ref[...]
            outs = _prep_core(*[zs[:, a:b] for a, b in _SEGS], *pv)
            for ref, o in zip(out_refs, outs):
                ref[at, :] = o

    tiled = lambda width: pl.BlockSpec((FUSED_TILE, width), lambda i: (i, 0))
    resident = lambda a: pl.BlockSpec(a.shape, lambda i: (0, 0), pipeline_mode=pl.Buffered(1))
    return pl.pallas_call(
        body,
        grid=(rows // FUSED_TILE,),
        in_specs=[tiled(d)] + [resident(a) for a in (g, w_attn, w_rw, mu, *pars)],
        out_specs=[tiled(d), tiled(wa_width), tiled(wr_width)] + [tiled(RWKV_WIDTH)] * 7,
        out_shape=[jax.ShapeDtypeStruct((rows, d), BF16), jax.ShapeDtypeStruct((rows, wa_width), F32),
                   jax.ShapeDtypeStruct((rows, wr_width), F32)] + [jax.ShapeDtypeStruct((rows, RWKV_WIDTH), F32)] * 7,
        scratch_shapes=[pltpu.VMEM((SUBLANES, wr_width), F32)],
        compiler_params=_cparams(("arbitrary",), 56),
        name="attn_norm_in_proj_rwkv_prep",
    )(x, g, w_attn, w_rw, mu, *pars)


def _prep_bwd(z, seq, cts, mu, pars):
    rows, width = z.shape
    ntile = rows // PREP_TILE
    npar, nct = len(pars), len(cts)
    acc_shapes = [(1, b - a) for a, b in _SEGS] + [p.shape for p in pars[:-1]]

    def body(z_ref, zprev_ref, *rest):
        ct_refs = rest[:nct]
        mu_ref = rest[nct]
        par_refs = rest[nct + 1:nct + 1 + npar]
        dz_ref = rest[nct + 1 + npar]
        acc_refs = rest[nct + 2 + npar:-1]
        carry = rest[-1]
        step = pl.program_id(0)
        tile_index = ntile - 1 - step

        @pl.when(step == 0)
        def _():
            carry[...] = jnp.zeros_like(carry)

        zc, zp = _shifted_tokens(z_ref, zprev_ref, tile_index, seq)
        mu_v = mu_ref[...]
        diff = zp - zc
        zs = zc + diff * mu_v
        dra, drb, dlw, dka, dkb, dva, dvb, dkk, da, dg = [c[...] for c in ct_refs]
        pv = [p[...] for p in par_refs]
        _, vjp = jax.vjp(lambda *args: _prep_core(*args, pv[-1]), *[zs[:, a:b] for a, b in _SEGS], *pv[:-1])
        grads = vjp((dra + drb, dlw, dka + dkb, dva + dvb, dkk, da, dg))
        dsegs, dpars = grads[:5], grads[5:]
        last_of_sequence = ((tile_index + 1) * PREP_TILE) % seq == 0
        accs = []
        for ds, (a, b) in zip(dsegs, _SEGS):
            mu_s = mu_v[:, a:b]
            dzp = ds * mu_s
            after = jnp.where(last_of_sequence, 0.0, carry[0:1, a:b])
            rowid = lax.broadcasted_iota(jnp.int32, dzp.shape, 0)
            from_next = jnp.where(rowid == PREP_TILE - 1, after, pltpu.roll(dzp, PREP_TILE - 1, 0))
            dz_ref[:, a:b] = (ds * (1.0 - mu_s) + from_next).astype(dz_ref.dtype)
            carry[:, a:b] = dzp[0:SUBLANES, :]
            accs.append(jnp.sum(ds * diff[:, a:b], axis=0, keepdims=True))
        accs.extend(dpars)

        def accumulate(ref, o):
            @pl.when(step == 0)
            def _():
                ref[...] = o

            @pl.when(step > 0)
            def _():
                ref[...] += o

        for ref, o in zip(acc_refs, accs):
            accumulate(ref, o)

    rev = lambda i: ntile - 1 - i
    zspecs, pspecs = _prep_specs(z, mu, pars, rev)
    return pl.pallas_call(
        body,
        grid=(ntile,),
        in_specs=zspecs + [pl.BlockSpec((PREP_TILE, RWKV_WIDTH), lambda i: (rev(i), 0))] * nct + pspecs,
        out_specs=[pl.BlockSpec((PREP_TILE, width), lambda i: (rev(i), 0))]
        + [pl.BlockSpec(s, lambda i: (0, 0)) for s in acc_shapes],
        out_shape=[jax.ShapeDtypeStruct((rows, width), BF16)] + [jax.ShapeDtypeStruct(s, F32) for s in acc_shapes],
        scratch_shapes=[pltpu.VMEM((SUBLANES, width), F32)],
        compiler_params=_cparams(("arbitrary",), 56),
        name="rwkv_prep_bwd",
    )(z, z, *cts, mu, *pars)


def _post_fn(y, r, k, v, g, ln_w, ln_b, r_k, esum):
    mean = _head_sums(y, esum) * (1.0 / HEAD_DIM)
    yc = y - mean
    var = _head_sums(yc * yc, esum) * (1.0 / HEAD_DIM)
    yn = yc * lax.rsqrt(var + GN_EPS) * ln_w + ln_b
    bonus = _head_sums(r * k * r_k, esum) * v
    return (yn + bonus) * g


def _post_out_proj_norm(y, r, k, v, g, attn_out, x, pars, w_attn_rows, w_rwkv_rows, g_norm):
    npar = len(pars)

    def fn(yv, rv, kv, vv, gv, av, xv, *rest):
        wa, wr, gn = rest[npar:]
        rw = _post_fn(yv, rv, kv, vv, gv, *rest[:npar])
        x1 = xv + _bdot_raw(av, wa, NN) + _bdot_raw(rw, wr, NN)
        return rw, x1, _rms_fn(x1, gn)

    d = x.shape[1]
    return _rowwise(fn, [y, r, k, v, g, attn_out, x], [*pars, w_attn_rows, w_rwkv_rows, g_norm],
                    [(RWKV_WIDTH, BF16), (d, F32), (d, BF16)], [],
                    tile=FUSED_TILE, nsub=FUSED_CHUNKS, name="rwkv_post_out_proj_mlp_norm")


def _out_proj_bwd_post_bwd(dx1, y, r, k, v, g, pars, w_attn_rows, w_rwkv_rows):
    npar = len(pars)

    def fn(dxv, yv, rv, kv, vv, gv, *rest):
        wa, wr = rest[npar:]
        esum = rest[npar - 1]
        d_attn = _bdot_raw(dxv, wa, NT)
        d_rw = _bdot_raw(dxv, wr, NT)
        _, vjp = jax.vjp(lambda *a: _post_fn(*a, esum), yv, rv, kv, vv, gv, *rest[:npar - 1])
        return (d_attn, *vjp(d_rw))

    return _rowwise(fn, [dx1, y, r, k, v, g], [*pars, w_attn_rows, w_rwkv_rows], [(RWKV_WIDTH, F32)] * 6,
                    [p.shape for p in pars[:-1]], tile=FUSED_TILE, nsub=FUSED_CHUNKS, name="out_proj_bwd_rwkv_post_bwd")


def _tri_inverses(ms, tick=lambda: None):
    n = ms[0].shape[0]
    row = lax.broadcasted_iota(jnp.int32, (n, n), 0)
    col = lax.broadcasted_iota(jnp.int32, (n, n), 1)
    eye = jnp.where(row == col, 1.0, 0.0)
    t_inv = [eye + m for m in ms]
    power = [_bdot_raw(m, m, NN) for m in ms]
    steps = int(math.log2(n // 2)) - 1
    for step in range(steps):
        if step < steps - 1:
            both = [_bdot_raw(jnp.concatenate([p, t], axis=0), p, NN) for p, t in zip(power, t_inv)]
            power = [b[:n] for b in both]
            t_inv = [t + b[n:] for t, b in zip(t_inv, both)]
        else:
            t_inv = [t + _bdot_raw(t, p, NN) for t, p in zip(t_inv, power)]
        tick()
    return t_inv


@jax.custom_vjp
def _tri_solve(ms, xs):
    return tuple(_bdot_raw(t, x, NN) for t, x in zip(_tri_inverses(ms), xs))


def _tri_solve_fwd(ms, xs):
    t_inv = _tri_inverses(ms)
    us = tuple(_bdot_raw(t, x, NN) for t, x in zip(t_inv, xs))
    return us, (tuple(t_inv), us)


def _tri_solve_bwd(res, dus):
    t_inv, us = res
    dxs = tuple(_bdot_raw(t, du, TN) for t, du in zip(t_inv, dus))
    dms = tuple(_bdot_raw(dx, u, NT) for dx, u in zip(dxs, us))
    return dms, dxs


_tri_solve.defvjp(_tri_solve_fwd, _tri_solve_bwd)


@functools.partial(jax.custom_vjp, nondiff_argnums=(3, 4))
def _kept_bdot_c(a, b, kept, ca, cb):
    return kept.astype(F32)


def _kept_bdot_c_fwd(a, b, kept, ca, cb):
    return kept.astype(F32), (a, b, kept)


def _kept_bdot_c_bwd(ca, cb, res, ct):
    a, b, kept = res
    return (*_bdot_c_bwd(ca, cb, (a, b), ct), jnp.zeros_like(kept))


_kept_bdot_c.defvjp(_kept_bdot_c_fwd, _kept_bdot_c_bwd)


def _kept_bdot(a, b, kept, dn):
    return _kept_bdot_c(a, b, kept, dn[0][0][0], dn[0][1][0])


@jax.custom_vjp
def _kept_tri_solve(ms, xs, t_inv, us):
    return tuple(u.astype(F32) for u in us)


def _kept_tri_solve_fwd(ms, xs, t_inv, us):
    return tuple(u.astype(F32) for u in us), (t_inv, us)


def _kept_tri_solve_bwd(res, dus):
    t_inv, us = res
    dms, dxs = _tri_solve_bwd(res, dus)
    return dms, dxs, tuple(jnp.zeros_like(t) for t in t_inv), tuple(jnp.zeros_like(u) for u in us)


_kept_tri_solve.defvjp(_kept_tri_solve_fwd, _kept_tri_solve_bwd)


def _chunk_fn(ss, rs, lws, ks, vs, kks, als, kept=None, tick=lambda: None):
    c = rs[0].shape[0]
    n = 2 * c
    row = lax.broadcasted_iota(jnp.int32, (n, n), 0)
    col = lax.broadcasted_iota(jnp.int32, (n, n), 1)
    incl = (row % c) >= (col % c)
    strict = (row % c) > (col % c)
    lane = lax.broadcasted_iota(jnp.int32, (1, LANES), 1)
    m_lo = jnp.where(lane < HEAD_DIM, 1.0, 0.0)
    m_hi = 1.0 - m_lo

    def stack(a):
        return jnp.concatenate([a * m_lo, a * m_hi], axis=0)

    cums = [_cumsum_rows(lw) for lw in lws]
    totals = [jnp.sum(lw, axis=0, keepdims=True) for lw in lws]
    bs = [kk * al for kk, al in zip(kks, als)]
    grows = [jnp.exp(-cum) for cum in cums]
    a_s = [stack(-kk * jnp.exp(cum - lw)) for kk, cum, lw in zip(kks, cums, lws)]
    b_s = [stack(b * g) for b, g in zip(bs, grows)]
    k_s = [stack(k * g) for k, g in zip(ks, grows)]
    r_s = [stack(r * jnp.exp(cum)) for r, cum in zip(rs, cums)]
    v_s = [stack(v) for v in vs]
    tick()
    pair = lambda p, q: jnp.concatenate([p, q], axis=0)
    ar_s = [pair(a, r) for a, r in zip(a_s, r_s)]
    if kept is None:
        products = [_bdot_raw(ar, pair(b, k), NT) for ar, b, k in zip(ar_s, b_s, k_s)]
    else:
        products = [_kept_bdot(ar, pair(b, k), kp, NT) for ar, b, k, kp in zip(ar_s, b_s, k_s, kept[0])]
    tick()
    blocks = [_quarters(p) for p in products]
    m_ab = [jnp.where(strict, q[0], 0.0) for q in blocks]
    m_ak = [jnp.where(strict, q[1], 0.0) for q in blocks]
    m_rb = [jnp.where(incl, q[2], 0.0) for q in blocks]
    m_rk = [jnp.where(incl, q[3], 0.0) for q in blocks]
    from_state = [_halves(_bdot(ar, s, NT)) for ar, s in zip(ar_s, ss)]
    tick()
    from_v = [_halves(_bdot(pair(mk, mr), v)) for mk, mr, v in zip(m_ak, m_rk, v_s)]
    tick()
    x = tuple(fs[0] + fv[0] for fs, fv in zip(from_state, from_v))
    if kept is None:
        t_inv = _tri_inverses(m_ab, tick)
        u = [_bdot_raw(t, xx, NN) for t, xx in zip(t_inv, x)]
        tick()
    else:
        u = _kept_tri_solve(tuple(m_ab), x, kept[1], kept[2])
    y = [_fold_rows(fs[1] + _bdot(mb, uu) + fv[1]) for fs, mb, uu, fv in zip(from_state, m_rb, u, from_v)]
    tick()
    tails = [jnp.exp(tot - cum) for tot, cum in zip(totals, cums)]
    s_new = [s * jnp.exp(tot) + _bdot(pair(uu, v), pair(stack(b * tl), stack(k * tl)), TN)
             for s, tot, uu, b, tl, v, k in zip(ss, totals, u, bs, tails, v_s, ks)]
    if kept is None:
        keep = lambda vals: tuple(v.astype(BF16) for v in vals)
        return tuple(y), tuple(s_new), (keep(products), keep(t_inv), keep(u))
    return tuple(y), tuple(s_new)


def _chains(bsz, npair):
    return [(b, p, slice(p * LANES, (p + 1) * LANES)) for b in range(bsz) for p in range(npair)]


def _hosted_exchange(first, last, ex_in, ex_out, sems, scatter):
    if not ex_in:
        return

    @pl.when(first)
    def _():
        _exchange_start(_exchange_copies(ex_in, ex_out, *sems, scatter, arrivals=False))

    @pl.when(last)
    def _():
        _exchange_wait(_exchange_copies(ex_in, ex_out, *sems, scatter, arrivals=True))


def _rwkv_attn_fwd(r, lw, k, v, kk, al, z, sink_rows, hosted=(), scatter=False):
    bsz, t, w = r.shape
    npair, nchunk = w // LANES, t // CHUNK
    nb = t // BLOCK
    assert bsz * nb == nchunk
    chains = _chains(bsz, npair)
    nex = len(hosted)
    apair = ATTN_WIDTH // LANES

    def body(*refs):
        r_ref, lw_ref, k_ref, v_ref, kk_ref, al_ref = refs[:6]
        q_ref, kp_ref, kc_ref, vp_ref, vc_ref, sink_ref = refs[6:12]
        ex_in = refs[12:12 + nex]
        y_ref, sall_ref, prod_ref, tinv_ref, u_ref, o_ref, p_ref, ps_ref = refs[12 + nex:20 + nex]
        ex_out = refs[20 + nex:20 + 2 * nex]
        s_scr = refs[20 + 2 * nex]
        step = pl.program_id(0)

        @pl.when(step == 0)
        def _():
            s_scr[...] = jnp.zeros_like(s_scr)

        _hosted_exchange(step == 0, step == nchunk - 1, ex_in, ex_out, refs[21 + 2 * nex:], scatter)
        qs = tuple(q_ref[0, :, pair * LANES:(pair + 1) * LANES] for pair in range(apair))
        attn = []
        stages = _attn_block_stages(qs, kp_ref[0], kc_ref[0], vp_ref[0], vc_ref[0], _sink_values(sink_ref),
                                    step % nb == 0, attn)
        ss = tuple(s_scr[i] for i in range(len(chains)))
        for i, s in enumerate(ss):
            sall_ref[0, i] = s
        ys, s_new, kept = _chunk_fn(ss, *[tuple(ref[b, :, cols] for b, _, cols in chains)
                                          for ref in (r_ref, lw_ref, k_ref, v_ref, kk_ref, al_ref)],
                                    tick=lambda: next(stages, None))
        for _ in stages:
            pass
        for i, (b, _, cols) in enumerate(chains):
            y_ref[b, :, cols] = ys[i]
            s_scr[i] = s_new[i]
            prod_ref[0, i], tinv_ref[0, i], u_ref[0, i] = kept[0][i], kept[1][i], kept[2][i]
        outs, probs, psinks = attn
        for pair in range(apair):
            o_ref[0, :, pair * LANES:(pair + 1) * LANES] = outs[pair].astype(o_ref.dtype)
        for h, p in enumerate(probs):
            p_ref[0, 0, h] = p
        ps_ref[0] = _head_columns(psinks)

    spec = pl.BlockSpec((bsz, CHUNK, w), lambda c: (0, c, 0))
    hbm = pl.BlockSpec(memory_space=pltpu.HBM)
    per_chunk = lambda n: pl.BlockSpec((1, len(chains), n, n), lambda c: (c, 0, 0, 0))
    kept_shape = lambda n: jax.ShapeDtypeStruct((nchunk, len(chains), n, n), BF16)
    kcol, vcol = ATTN_WIDTH // KV_WIDTH, ATTN_WIDTH // KV_WIDTH + 1
    before = lambda c: jnp.maximum(c % nb - 1, 0)
    outs = pl.pallas_call(
        body,
        grid=(nchunk,),
        in_specs=[spec] * 6
        + [pl.BlockSpec((1, BLOCK, ATTN_WIDTH), lambda c: (c // nb, c % nb, 0)),
           pl.BlockSpec((1, BLOCK, KV_WIDTH), lambda c: (c // nb, before(c), kcol)),
           pl.BlockSpec((1, BLOCK, KV_WIDTH), lambda c: (c // nb, c % nb, kcol)),
           pl.BlockSpec((1, BLOCK, KV_WIDTH), lambda c: (c // nb, before(c), vcol)),
           pl.BlockSpec((1, BLOCK, KV_WIDTH), lambda c: (c // nb, c % nb, vcol)),
           pl.BlockSpec(sink_rows.shape, lambda c: (0, 0))]
        + [hbm] * nex,
        out_specs=[spec, per_chunk(LANES), per_chunk(4 * CHUNK), per_chunk(2 * CHUNK), per_chunk(2 * CHUNK),
                   pl.BlockSpec((1, BLOCK, ATTN_WIDTH), lambda c: (c // nb, c % nb, 0)),
                   pl.BlockSpec((1, 1, N_ATTN_HEADS, BLOCK, 2 * BLOCK), lambda c: (c // nb, c % nb, 0, 0, 0)),
                   pl.BlockSpec((1, BLOCK, LANES), lambda c: (c // nb, c % nb, 0))]
        + [hbm] * nex,
        out_shape=[jax.ShapeDtypeStruct((bsz, t, w), F32),
                   jax.ShapeDtypeStruct((nchunk, len(chains), LANES, LANES), F32),
                   kept_shape(4 * CHUNK), kept_shape(2 * CHUNK), kept_shape(2 * CHUNK),
                   jax.ShapeDtypeStruct((bsz, t, ATTN_WIDTH), BF16),
                   jax.ShapeDtypeStruct((bsz, nb, N_ATTN_HEADS, BLOCK, 2 * BLOCK), F32),
                   jax.ShapeDtypeStruct((bsz, t, LANES), F32)]
        + _exchange_out_shapes(hosted, scatter),
        scratch_shapes=[pltpu.VMEM((len(chains), LANES, LANES), F32)] + (_exchange_scratch(nex) if nex else []),
        compiler_params=_cparams(("arbitrary",), 48),
        name="rwkv_chunk_swa_fwd",
    )(r, lw, k, v, kk, al, z, z, z, z, z, sink_rows, *hosted)
    return outs[0], outs[1:5], outs[5:8], outs[8:]


def _rwkv_attn_bwd(r, lw, k, v, kk, al, from_fwd, dy, z, dout, probs, psinks, hosted=(), scatter=False):
    bsz, t, w = r.shape
    npair, nchunk = w // LANES, t // CHUNK
    nb = t // BLOCK
    assert bsz * nb == nchunk
    chains = _chains(bsz, npair)
    nex = len(hosted)
    apair = ATTN_WIDTH // LANES

    def body(*refs):
        r_ref, lw_ref, k_ref, v_ref, kk_ref, al_ref, s_ref, prod_ref, tinv_ref, u_ref, dy_ref = refs[:11]
        q_ref, kp_ref, kc_ref, vp_ref, vc_ref, do_ref, p_ref, ps_ref = refs[11:19]
        ex_in = refs[19:19 + nex]
        out_refs = refs[19 + nex:25 + nex]
        dz_ref, dsink_ref = refs[25 + nex:27 + nex]
        ex_out = refs[27 + nex:27 + 2 * nex]
        ds_scr, carry = refs[27 + 2 * nex:29 + 2 * nex]
        step = pl.program_id(0)

        @pl.when(step == 0)
        def _():
            ds_scr[...] = jnp.zeros_like(ds_scr)
            dsink_ref[...] = jnp.zeros_like(dsink_ref)

        @pl.when(step % nb == 0)
        def _():
            carry[...] = jnp.zeros_like(carry)

        _hosted_exchange(step == 0, step == nchunk - 1, ex_in, ex_out, refs[29 + 2 * nex:], scatter)
        qs = tuple(q_ref[0, :, pair * LANES:(pair + 1) * LANES] for pair in range(apair))
        dos = tuple(do_ref[0, :, pair * LANES:(pair + 1) * LANES] for pair in range(apair))
        attn = []
        stages = _attn_block_bwd_stages(qs, kp_ref[0], kc_ref[0], vp_ref[0], vc_ref[0], dos,
                                        [p_ref[0, 0, h] for h in range(N_ATTN_HEADS)], step % nb == nb - 1, attn)
        ss = tuple(s_ref[0, i] for i in range(len(chains)))
        kept = tuple(tuple(ref[0, i] for i in range(len(chains))) for ref in (prod_ref, tinv_ref, u_ref))
        _, vjp = jax.vjp(functools.partial(_chunk_fn, kept=kept), ss,
                         *[tuple(ref[b, :, cols] for b, _, cols in chains)
                           for ref in (r_ref, lw_ref, k_ref, v_ref, kk_ref, al_ref)])
        rules = [0]

        def one_stage_per_round():
            rules[0] += 1
            if rules[0] % len(chains) == 0:
                next(stages, None)

        _BESIDE_BACKWARD[0] = one_stage_per_round
        try:
            grads = vjp((tuple(dy_ref[b, :, cols] for b, _, cols in chains),
                         tuple(ds_scr[i] for i in range(len(chains)))))
        finally:
            _BESIDE_BACKWARD[0] = None
        for _ in stages:
            pass
        for i, (b, _, cols) in enumerate(chains):
            ds_scr[i] = grads[0][i]
            for ref, gval in zip(out_refs, grads[1:]):
                ref[b, :, cols] = gval[i]
        dqs, dkp, dkc, dvp, dvc, deltas = attn
        for pair in range(apair):
            dz_ref[0, :, pair * LANES:(pair + 1) * LANES] = dqs[pair].astype(dz_ref.dtype)
        dsink_ref[...] -= jnp.sum(ps_ref[0] * _head_columns(deltas), axis=0, keepdims=True)
        dz_ref[0, :, ATTN_WIDTH:ATTN_WIDTH + KV_WIDTH] = (dkc + carry[0]).astype(dz_ref.dtype)
        dz_ref[0, :, ATTN_WIDTH + KV_WIDTH:QKV_WIDTH] = (dvc + carry[1]).astype(dz_ref.dtype)
        carry[0] = dkp
        carry[1] = dvp

    spec = pl.BlockSpec((bsz, CHUNK, w), lambda c: (0, nchunk - 1 - c, 0))
    per_chunk = lambda n: pl.BlockSpec((1, len(chains), n, n), lambda c: (nchunk - 1 - c, 0, 0, 0))
    hbm = pl.BlockSpec(memory_space=pltpu.HBM)
    kcol, vcol = ATTN_WIDTH // KV_WIDTH, ATTN_WIDTH // KV_WIDTH + 1
    seq_of = lambda c: c // nb
    blk = lambda c: nb - 1 - c % nb
    before = lambda c: jnp.maximum(blk(c) - 1, 0)
    outs = pl.pallas_call(
        body,
        grid=(nchunk,),
        in_specs=[spec] * 6 + [per_chunk(LANES), per_chunk(4 * CHUNK), per_chunk(2 * CHUNK), per_chunk(2 * CHUNK), spec]
        + [pl.BlockSpec((1, BLOCK, ATTN_WIDTH), lambda c: (seq_of(c), blk(c), 0)),
           pl.BlockSpec((1, BLOCK, KV_WIDTH), lambda c: (seq_of(c), before(c), kcol)),
           pl.BlockSpec((1, BLOCK, KV_WIDTH), lambda c: (seq_of(c), blk(c), kcol)),
           pl.BlockSpec((1, BLOCK, KV_WIDTH), lambda c: (seq_of(c), before(c), vcol)),
           pl.BlockSpec((1, BLOCK, KV_WIDTH), lambda c: (seq_of(c), blk(c), vcol)),
           pl.BlockSpec((1, BLOCK, ATTN_WIDTH), lambda c: (seq_of(c), blk(c), 0)),
           pl.BlockSpec((1, 1, N_ATTN_HEADS, BLOCK, 2 * BLOCK), lambda c: (seq_of(c), blk(c), 0, 0, 0)),
           pl.BlockSpec((1, BLOCK, LANES), lambda c: (seq_of(c), blk(c), 0))]
        + [hbm] * nex,
        out_specs=[spec] * 6
        + [pl.BlockSpec((1, BLOCK, QKV_WIDTH), lambda c: (seq_of(c), blk(c), 0)),
           pl.BlockSpec((1, LANES), lambda c: (0, 0))]
        + [hbm] * nex,
        out_shape=[jax.ShapeDtypeStruct((bsz, t, w), F32)] * 6
        + [jax.ShapeDtypeStruct((bsz, t, QKV_WIDTH), BF16), jax.ShapeDtypeStruct((1, LANES), F32)]
        + _exchange_out_shapes(hosted, scatter),
        scratch_shapes=[pltpu.VMEM((len(chains), LANES, LANES), F32), pltpu.VMEM((2, BLOCK, KV_WIDTH), F32)]
        + (_exchange_scratch(nex) if nex else []),
        compiler_params=_cparams(("arbitrary",), 48),
        name="rwkv_chunk_swa_bwd",
    )(r, lw, k, v, kk, al, *from_fwd, dy, z, z, z, z, z, dout, probs, psinks, *hosted)
    return outs[:6], outs[6:8], outs[8:]


def _alibi_slope(head):
    return 2.0 ** (-8.0 * (head + 1) / N_ATTN_HEADS)


def _attn_setup(first):
    row = lax.broadcasted_iota(jnp.int32, (BLOCK, 2 * BLOCK), 0)
    col = lax.broadcasted_iota(jnp.int32, (BLOCK, 2 * BLOCK), 1)
    lane = lax.broadcasted_iota(jnp.int32, (1, LANES), 1)
    halves = [jnp.where((lane // HEAD_DIM) == half, 1.0, 0.0) for half in range(2)]
    srow = lax.broadcasted_iota(jnp.int32, (LANES, LANES), 0)
    scol = lax.broadcasted_iota(jnp.int32, (LANES, LANES), 1)
    swap = jnp.where((srow + HEAD_DIM) % LANES == scol, 1.0, 0.0)
    dist = row - col + BLOCK
    valid = jnp.logical_and(jnp.logical_and(dist >= 0, dist < BLOCK),
                            jnp.logical_or(col >= BLOCK, jnp.logical_not(first)))
    return halves, swap, dist.astype(F32), valid, HEAD_DIM ** -0.5


def _attn_keys_values(kp, kc, vp, vc, swap, npair):
    stored = (jnp.concatenate([kp, kc], axis=0), jnp.concatenate([vp, vc], axis=0))
    swapped = tuple(_bdot_raw(t, swap, NN) for t in stored)
    heads = [(pair, half) for pair in range(npair) for half in range(2)]
    return heads, [half == pair // 2 for pair, half in heads], stored, swapped


def _attn_block_stages(qs, kp, kc, vp, vc, sinks, first, result):
    halves, swap, dist, valid, scale = _attn_setup(first)
    heads, as_stored, stored, swapped = _attn_keys_values(kp, kc, vp, vc, swap, len(qs))
    kv = [stored if own else swapped for own in as_stored]
    slopes = [_alibi_slope(2 * pair + half) for pair, half in heads]
    qa = [qs[pair] * halves[half] for pair, half in heads]
    yield
    s = [jnp.where(valid, _bdot_raw(q, t[0], NT) * scale - sl * dist, NEG_INF) for q, t, sl in zip(qa, kv, slopes)]
    yield
    mx = [jnp.maximum(jnp.max(a, axis=-1, keepdims=True), sk) for a, sk in zip(s, sinks)]
    yield
    e = [jnp.exp(a - m) for a, m in zip(s, mx)]
    yield
    es = [jnp.exp(sk - m) for sk, m in zip(sinks, mx)]
    inv = [1.0 / (jnp.sum(a, axis=-1, keepdims=True) + b) for a, b in zip(e, es)]
    yield
    probs = [a * i for a, i in zip(e, inv)]
    yield
    o = [_bdot_raw(p, t[1], NN) for p, t in zip(probs, kv)]
    yield
    outs = tuple(o[2 * pair] * halves[0] + o[2 * pair + 1] * halves[1] for pair in range(len(qs)))
    result.extend([outs, probs, [b * i for b, i in zip(es, inv)]])


def _attn_block_bwd_stages(qs, kp, kc, vp, vc, dos, probs, first, result):
    halves, swap, _, _, scale = _attn_setup(first)
    heads, as_stored, stored, swapped = _attn_keys_values(kp, kc, vp, vc, swap, len(qs))
    kv = [stored if own else swapped for own in as_stored]
    qa = [qs[pair] * halves[half] for pair, half in heads]
    do = [dos[pair] * halves[half] for pair, half in heads]
    yield
    dp = [_bdot_raw(d, t[1], NT) for d, t in zip(do, kv)]
    yield
    delta = [jnp.sum(p * d, axis=-1, keepdims=True) for p, d in zip(probs, dp)]
    yield
    ds = [p * (d - dl) for p, d, dl in zip(probs, dp, delta)]
    yield
    dq = [_bdot_raw(g, t[0], NN) * (scale * halves[half]) for g, t, (_, half) in zip(ds, kv, heads)]
    yield
    dk = [_bdot_raw(g, q, TN) * scale for g, q in zip(ds, qa)]
    yield
    dv = [_bdot_raw(p, d, TN) for p, d in zip(probs, do)]
    yield
    dqs = tuple(dq[2 * pair] + dq[2 * pair + 1] for pair in range(len(qs)))

    def total(parts):
        direct = sum(g for g, own in zip(parts, as_stored) if own)
        return direct + _bdot_raw(sum(g for g, own in zip(parts, as_stored) if not own), swap, NN)

    dk_all, dv_all = total(dk), total(dv)
    result.extend([dqs, dk_all[:BLOCK], dk_all[BLOCK:], dv_all[:BLOCK], dv_all[BLOCK:], delta])


def _sink_values(sink_ref):
    return [jnp.max(sink_ref[h:h + 1, :], axis=-1, keepdims=True) for h in range(N_ATTN_HEADS)]


def _head_columns(cols):
    lane = lax.broadcasted_iota(jnp.int32, (1, LANES), 1)
    return sum(c * jnp.where(lane == h, 1.0, 0.0) for h, c in enumerate(cols))


def _exchange_out_shapes(arrays, scatter):
    return [jax.ShapeDtypeStruct((N_DEV,) + (a.shape[1:] if scatter else a.shape), a.dtype) for a in arrays]


def _exchange_scratch(n):
    return [pltpu.SemaphoreType.DMA((n, N_DEV - 1)), pltpu.SemaphoreType.DMA((n, N_DEV - 1)),
            pltpu.SemaphoreType.DMA((n,))]


def _exchange_copies(ins, outs, send_sems, recv_sems, local_sems, scatter, arrivals=True):
    x, y, c = lax.axis_index("x"), lax.axis_index("y"), lax.axis_index("c")
    me = 4 * x + 2 * y + c
    copies = []
    for i in range(len(ins)):
        own = pltpu.make_async_copy(ins[i].at[me] if scatter else ins[i], outs[i].at[me], local_sems.at[i])
        copies.append((own, None, True))
        for d in range(1, N_DEV):
            px = 1 - x if d & 4 else x
            py = 1 - y if d & 2 else y
            pc = 1 - c if d & 1 else c
            peer = 4 * px + 2 * py + pc
            src = ins[i].at[peer] if scatter else ins[i]
            send = pltpu.make_async_remote_copy(src, outs[i].at[me], send_sems.at[i, d - 1], recv_sems.at[i, d - 1],
                                                device_id=(px, py, pc), device_id_type=MESH)
            recv = pltpu.make_async_remote_copy(src, outs[i].at[peer], send_sems.at[i, d - 1], recv_sems.at[i, d - 1],
                                                device_id=(px, py, pc), device_id_type=MESH) if arrivals else None
            copies.append((send, recv, False))
    return copies


def _exchange_start(copies):
    for send, _, _ in copies:
        send.start()


def _exchange_wait(copies):
    for send, recv, local in copies:
        if local:
            send.wait()
        else:
            send.wait_send()
            recv.wait_recv()


def _gather_two_level(arrays, name):
    n = len(arrays)

    def body(*refs):
        ins, outs = refs[:n], refs[n:2 * n]
        send_sems, recv_sems, local_sems = refs[2 * n:]
        x, y, c = lax.axis_index("x"), lax.axis_index("y"), lax.axis_index("c")
        index = lambda px, py, pc: 4 * px + 2 * py + pc
        sibling = (x, y, 1 - c)
        chips = [(1 - x, y), (x, 1 - y), (1 - x, 1 - y)]

        def copy(i, k, block, to, src=None):
            slot = outs[i].at[index(*block)]
            return pltpu.make_async_remote_copy(slot if src is None else src, slot, send_sems.at[i, k],
                                                recv_sems.at[i, k], device_id=to, device_id_type=MESH)

        local, sends = [], []
        for i in range(n):
            own = pltpu.make_async_copy(ins[i], outs[i].at[index(x, y, c)], local_sems.at[i])
            own.start()
            local.append(own)
            first = [copy(i, 0, (x, y, c), sibling, src=ins[i])]
            first += [copy(i, 1 + j, (x, y, c), (*chip, c), src=ins[i]) for j, chip in enumerate(chips)]
            for cp in first:
                cp.start()
            sends += first
        for i in range(n):
            for j, chip in enumerate(chips):
                copy(i, 1 + j, (*chip, c), (x, y, c)).wait_recv()
                onward = copy(i, 4 + j, (*chip, c), sibling)
                onward.start()
                sends.append(onward)
        for i in range(n):
            copy(i, 0, sibling, (x, y, c)).wait_recv()
            for j, chip in enumerate(chips):
                copy(i, 4 + j, (*chip, 1 - c), (x, y, c)).wait_recv()
        for cp in sends:
            cp.wait_send()
        for cp in local:
            cp.wait()

    hbm = pl.BlockSpec(memory_space=pltpu.HBM)
    return pl.pallas_call(
        body,
        in_specs=[hbm] * n,
        out_specs=[hbm] * n,
        out_shape=_exchange_out_shapes(arrays, False),
        scratch_shapes=_exchange_scratch(n),
        name=name,
    )(*arrays)


def _exchange(arrays, *, scatter, name):
    n = len(arrays)

    def body(*refs):
        copies = _exchange_copies(refs[:n], refs[n:2 * n], *refs[2 * n:], scatter)
        _exchange_start(copies)
        _exchange_wait(copies)

    hbm = pl.BlockSpec(memory_space=pltpu.HBM)
    return pl.pallas_call(
        body,
        in_specs=[hbm] * n,
        out_specs=[hbm] * n,
        out_shape=_exchange_out_shapes(arrays, scatter),
        scratch_shapes=_exchange_scratch(n),
        name=name,
    )(*arrays)


def _adamw(parts, w, m, v, name):
    rows, cols = w.shape
    tr = _pick(rows, (256, 128, 64, 8))
    c1 = 1.0 / (1.0 - ADAM_B1 ** ADAM_STEP)
    c2 = 1.0 / (1.0 - ADAM_B2 ** ADAM_STEP)

    def body(p_ref, w_ref, m_ref, v_ref, g_ref, d_ref, mo_ref, vo_ref):
        g = p_ref[0].astype(F32)
        for s in range(1, N_DEV):
            g = g + p_ref[s].astype(F32)
        mn = ADAM_B1 * m_ref[...] + (1.0 - ADAM_B1) * g
        vn = ADAM_B2 * v_ref[...] + (1.0 - ADAM_B2) * (g * g)
        g_ref[...] = g
        mo_ref[...] = mn
        vo_ref[...] = vn
        d_ref[...] = -ADAM_LR * ((mn * c1) / (jnp.sqrt(vn * c2) + ADAM_EPS) + ADAM_WD * w_ref[...])

    spec = pl.BlockSpec((tr, cols), lambda i: (i, 0))
    return pl.pallas_call(
        body,
        grid=(rows // tr,),
        in_specs=[pl.BlockSpec((N_DEV, tr, cols), lambda i: (0, i, 0)), spec, spec, spec],
        out_specs=[spec] * 4,
        out_shape=[jax.ShapeDtypeStruct((rows, cols), F32)] * 4,
        compiler_params=_cparams(("parallel",), 48),
        name=name,
    )(parts, w, m, v)


_VECTOR_PARAMS = ("attn_norm_g", "attn_sinks", "rwkv_mu", "w0", "a0", "k_k", "k_a", "r_k", "ln_x_w", "ln_x_b",
                  "mlp_norm_g", "final_norm_g")
_WEIGHT_NAMES = ("attn_norm_g", "w_in", "attn_sinks", "rwkv_mu", "w0", "w2", "a0", "a2", "g2", "k_k", "k_a", "r_k",
                 "ln_x_w", "ln_x_b", "w_out", "mlp_norm_g", "w_up", "w_down", "final_norm_g")


def _pack_vectors(vals):
    pieces = []
    for name in _VECTOR_PARAMS:
        flat = vals[name].reshape(1, -1)
        pad = (-flat.shape[1]) % LANES
        pieces.append(jnp.pad(flat, ((0, 0), (0, pad))) if pad else flat)
    return jnp.concatenate(pieces, axis=1)


def _unpack_vectors(packed, like):
    out, col = {}, 0
    for name in _VECTOR_PARAMS:
        size = like[name].size
        out[name] = packed[0, col:col + size].reshape(like[name].shape)
        col += size + (-size) % LANES
    return out


def kernel(x, attn_norm_g, w_in, attn_sinks, rwkv_mu, w0, w2, a0, a2, g2, k_k, k_a, r_k, ln_x_w, ln_x_b, w_out, mlp_norm_g, w_up, w_down, final_norm_g, loss_target, m_attn_norm_g, m_w_in, m_attn_sinks, m_rwkv_mu, m_w0, m_w2, m_a0, m_a2, m_g2, m_k_k, m_k_a, m_r_k, m_ln_x_w, m_ln_x_b, m_w_out, m_mlp_norm_g, m_w_up, m_w_down, m_final_norm_g, v_attn_norm_g, v_w_in, v_attn_sinks, v_rwkv_mu, v_w0, v_w2, v_a0, v_a2, v_g2, v_k_k, v_k_a, v_r_k, v_ln_x_w, v_ln_x_b, v_w_out, v_mlp_norm_g, v_w_up, v_w_down, v_final_norm_g):
    weights = dict(attn_norm_g=attn_norm_g, w_in=w_in, attn_sinks=attn_sinks, rwkv_mu=rwkv_mu, w0=w0, w2=w2, a0=a0,
                   a2=a2, g2=g2, k_k=k_k, k_a=k_a, r_k=r_k, ln_x_w=ln_x_w, ln_x_b=ln_x_b, w_out=w_out,
                   mlp_norm_g=mlp_norm_g, w_up=w_up, w_down=w_down, final_norm_g=final_norm_g)
    mom1 = dict(attn_norm_g=m_attn_norm_g, w_in=m_w_in, attn_sinks=m_attn_sinks, rwkv_mu=m_rwkv_mu, w0=m_w0, w2=m_w2,
                a0=m_a0, a2=m_a2, g2=m_g2, k_k=m_k_k, k_a=m_k_a, r_k=m_r_k, ln_x_w=m_ln_x_w, ln_x_b=m_ln_x_b,
                w_out=m_w_out, mlp_norm_g=m_mlp_norm_g, w_up=m_w_up, w_down=m_w_down, final_norm_g=m_final_norm_g)
    mom2 = dict(attn_norm_g=v_attn_norm_g, w_in=v_w_in, attn_sinks=v_attn_sinks, rwkv_mu=v_rwkv_mu, w0=v_w0, w2=v_w2,
                a0=v_a0, a2=v_a2, g2=v_g2, k_k=v_k_k, k_a=v_k_a, r_k=v_r_k, ln_x_w=v_ln_x_w, ln_x_b=v_ln_x_b,
                w_out=v_w_out, mlp_norm_g=v_mlp_norm_g, w_up=v_w_up, w_down=v_w_down, final_norm_g=v_final_norm_g)
    bsz, seq, d_model = x.shape
    rows = bsz * seq
    d_in = N_DEV * w_in.shape[2]
    d_ff = N_DEV * w_up.shape[2]

    gathered = _gather_two_level([w_in[0].astype(BF16), w2[0], a2[0], g2[0]], name="gather_in_weights")
    cols_first = lambda a: a.transpose(1, 0, 2).reshape(a.shape[1], -1)
    w_in_f = cols_first(gathered[0])
    w_attn, w_rw = w_in_f[:, :QKV_WIDTH], w_in_f[:, QKV_WIDTH:]
    w2_f, a2_f, g2_f = cols_first(gathered[1]), cols_first(gathered[2]), cols_first(gathered[3])
    lora = w2_f.shape[0]
    w2p = jnp.concatenate([w2_f, jnp.zeros_like(a2_f)], axis=0)
    a2p = jnp.concatenate([jnp.zeros_like(w2_f), a2_f], axis=0)

    esum = _head_sum_matrix()
    sink_rows = jnp.broadcast_to(attn_sinks.reshape(N_ATTN_HEADS, 1), (N_ATTN_HEADS, LANES))
    prep_pars = [w0, w2p, a0, a2p, g2_f, k_k, k_a, esum]
    post_pars = [ln_x_w, ln_x_b, r_k, esum]

    x2d = x.reshape(rows, d_model)
    h1, z_attn, z_rw, r, lw, k, v, kk, al, gate = _norm_in_proj_prep(x2d, seq, attn_norm_g, w_attn, w_rw, rwkv_mu,
                                                                       prep_pars)
    z_attn3 = z_attn.reshape(bsz, seq, QKV_WIDTH)
    as3 = lambda a: a.reshape(bsz, seq, RWKV_WIDTH)
    y, from_fwd, (attn_out, attn_probs, attn_psinks), late = _rwkv_attn_fwd(
        as3(r), as3(lw), as3(k), as3(v), as3(kk), as3(al), z_attn3, sink_rows,
        hosted=[w_out[0].astype(BF16), w_up[0].astype(BF16), w_down[0].astype(BF16)])
    w_out_f = late[0].reshape(-1, d_model)
    w_up_f = cols_first(late[1])
    w_down_f = late[2].reshape(-1, d_model)
    y2 = y.reshape(rows, RWKV_WIDTH)
    attn_out2d = attn_out.reshape(rows, ATTN_WIDTH)
    w_out_attn, w_out_rw = w_out_f[:ATTN_WIDTH], w_out_f[ATTN_WIDTH:]
    rw_out, x1, h2 = _post_out_proj_norm(y2, r, k, v, gate, attn_out2d, x2d, post_pars, w_out_attn, w_out_rw,
                                         mlp_norm_g)

    def relu_sq(acc):
        pos = jnp.maximum(acc, 0.0)
        return acc, pos * pos

    u, act = _matmul(h2, w_up_f, "nn", name="mlp_up", epilogue=relu_sq, out_dtypes=(BF16, BF16))
    dx2, loss_vec, g_final = _down_proj_loss(act, w_down_f, x1, loss_target.reshape(rows, d_model),
                                             final_norm_g.reshape(1, d_model))

    g_w_down = _matmul(act, dx2, "tn", name="grad_w_down", out_dtypes=(BF16,))
    du = _matmul(dx2, w_down_f, "nt", name="mlp_down_bwd", extras=(u,), out_dtypes=(BF16,),
                 epilogue=lambda acc, uv: (acc * (2.0 * jnp.maximum(uv.astype(F32), 0.0)),))
    g_w_up = _matmul(h2, du, "tn", name="grad_w_up", out_dtypes=(BF16,))
    dx1, g_mlp_norm = _proj_bwd_norm_bwd([du], [w_up_f], x1, dx2, mlp_norm_g, "mlp_up_bwd_norm_bwd")
    g_w_out = jnp.concatenate([_matmul(attn_out2d, dx1, "tn", name="grad_w_out_attn", out_dtypes=(BF16,)),
                               _matmul(rw_out, dx1, "tn", name="grad_w_out_rwkv", out_dtypes=(BF16,))], axis=0)
    d_attn_out, dy, dr_a, dk_a, dv_a, dgate, g_ln_w, g_ln_b, g_r_k = _out_proj_bwd_post_bwd(
        dx1, y2, r, k, v, gate, post_pars, w_out_attn, w_out_rw)
    by_cols = lambda a: a.reshape(a.shape[0], N_DEV, -1).transpose(1, 0, 2)
    (dr_b, dlw, dk_b, dv_b, dkk, dal), (dz_attn, g_sink_lanes), (p_w_out, p_w_up, p_w_down) = _rwkv_attn_bwd(
        as3(r), as3(lw), as3(k), as3(v), as3(kk), as3(al), from_fwd, as3(dy),
        z_attn3, d_attn_out.reshape(bsz, seq, ATTN_WIDTH), attn_probs, attn_psinks,
        hosted=[g_w_out.reshape(N_DEV, -1, d_model), by_cols(g_w_up), g_w_down.reshape(N_DEV, -1, d_model)],
        scatter=True)
    flat = lambda a: a.reshape(rows, RWKV_WIDTH)
    (dz_rw, gmu_r, gmu_k, gmu_v, gmu_wa, gmu_g, g_w0, g_w2p, g_a0, g_a2p, g_g2, g_k_k, g_k_a) = _prep_bwd(
        z_rw, seq, [dr_a, flat(dr_b), flat(dlw), dk_a, flat(dk_b), dv_a, flat(dv_b), flat(dkk), flat(dal), dgate],
        rwkv_mu, prep_pars)
    dz_attn = dz_attn.reshape(rows, QKV_WIDTH)
    g_w_in = jnp.concatenate([_matmul(h1, dz_attn, "tn", name="grad_w_in_attn", out_dtypes=(BF16,)),
                              _matmul(h1, dz_rw, "tn", name="grad_w_in_rwkv", out_dtypes=(BF16,))], axis=1)
    lora_grads = jnp.concatenate([g_w2p[:lora], g_a2p[lora:], g_g2], axis=0)
    (dx, g_attn_norm), (p_w_in, p_lora) = _proj_bwd_norm_bwd(
        [dz_attn, dz_rw], [w_attn, w_rw], x2d, dx1, attn_norm_g, "in_proj_bwd_norm_bwd",
        hosted=[by_cols(g_w_in), by_cols(lora_grads)], scatter=True)

    vec_grads = dict(attn_norm_g=g_attn_norm, attn_sinks=g_sink_lanes[0, :N_ATTN_HEADS], rwkv_mu=jnp.concatenate(
        [gmu_r, gmu_k, gmu_v, gmu_wa, gmu_g], axis=1), w0=g_w0, a0=g_a0, k_k=g_k_k, k_a=g_k_a, r_k=g_r_k,
        ln_x_w=g_ln_w, ln_x_b=g_ln_b, mlp_norm_g=g_mlp_norm, final_norm_g=g_final)
    packed = _pack_vectors(vec_grads)
    nvec = packed.shape[1]
    everyone = _exchange([jnp.concatenate([packed, loss_vec], axis=1)], scatter=False, name="gather_vector_grads")[0]
    vec_parts = everyone[:, :, :nvec]
    loss = jnp.sum(everyone[:, 0, nvec])

    grads, delta, new_m, new_v = {}, {}, {}, {}

    def update(name, part, shape2d):
        res = _adamw(part, weights[name].reshape(shape2d), mom1[name].reshape(shape2d), mom2[name].reshape(shape2d),
                     "adamw_" + name)
        for store, val in zip((grads, delta, new_m, new_v), res):
            store[name] = val.reshape(weights[name].shape)

    update("w_in", p_w_in, w_in.shape[1:])
    update("w_out", p_w_out, w_out.shape[1:])
    update("w_up", p_w_up, w_up.shape[1:])
    update("w_down", p_w_down, w_down.shape[1:])
    stack = lambda d: jnp.concatenate([d["w2"][0], d["a2"][0], d["g2"][0]], axis=0)
    lora_res = _adamw(p_lora, stack(weights), stack(mom1), stack(mom2), "adamw_lora")
    for store, val in zip((grads, delta, new_m, new_v), lora_res):
        store["w2"], store["a2"], store["g2"] = val[None, :lora], val[None, lora:2 * lora], val[None, 2 * lora:]
    vec_res = _adamw(vec_parts, _pack_vectors(weights), _pack_vectors(mom1), _pack_vectors(mom2), "adamw_vectors")
    for store, val in zip((grads, delta, new_m, new_v), vec_res):
        store.update(_unpack_vectors(val, weights))

    return (loss, dx.reshape(x.shape), *[grads[n] for n in _WEIGHT_NAMES], *[delta[n] for n in _WEIGHT_NAMES],
            *[new_m[n] for n in _WEIGHT_NAMES], *[new_v[n] for n in _WEIGHT_NAMES])
```

```python
import functools
import math

import jax
import jax.numpy as jnp
from jax import lax
from jax.experimental import pallas as pl
from jax.experimental.pallas import tpu as pltpu

F32 = jnp.float32
BF16 = jnp.bfloat16

N_DEV = 8
HEAD_DIM = 64
LANES = 128
N_ATTN_HEADS = 8
ATTN_WIDTH = 512
KV_WIDTH = 128
QKV_WIDTH = ATTN_WIDTH + 2 * KV_WIDTH
RWKV_WIDTH = 512
LORA_WA = 128
GATE_LORA = 128
RWKV_SHIFT_WIDTH = 3 * RWKV_WIDTH + LORA_WA + GATE_LORA
BLOCK = 128
CHUNK = 64
RMS_EPS = 1e-6
GN_EPS = 64e-5
L2_EPS = 1e-12
NEG_INF = -1e30
DECAY_SCALE = math.exp(-0.5)
ADAM_LR, ADAM_B1, ADAM_B2, ADAM_EPS, ADAM_WD, ADAM_STEP = 0.001, 0.9, 0.999, 1e-08, 0.01, 10

NN = (((1,), (0,)), ((), ()))
NT = (((1,), (1,)), ((), ()))
TN = (((0,), (0,)), ((), ()))
MESH = pl.DeviceIdType.MESH


def _dot(a, b, dn=NN, precision=None):
    return lax.dot_general(a, b, dn, precision=precision, preferred_element_type=F32)


def _bdot_raw(a, b, dn):
    return lax.dot_general(a.astype(BF16), b.astype(BF16), dn, preferred_element_type=F32)


@functools.partial(jax.custom_vjp, nondiff_argnums=(2, 3))
def _bdot_c(a, b, ca, cb):
    return _bdot_raw(a, b, (((ca,), (cb,)), ((), ())))


def _bdot_c_fwd(a, b, ca, cb):
    return _bdot_c(a, b, ca, cb), (a, b)


_BESIDE_BACKWARD = [None]


def _beside_backward():
    if _BESIDE_BACKWARD[0] is not None:
        _BESIDE_BACKWARD[0]()


def _bdot_c_bwd(ca, cb, res, ct):
    _beside_backward()
    a, b = res
    fa, fb = 1 - ca, 1 - cb
    da = _bdot_raw(ct, b, (((1,), (fb,)), ((), ()))) if ca == 1 else _bdot_raw(b, ct, (((fb,), (1,)), ((), ())))
    db = _bdot_raw(a, ct, (((fa,), (0,)), ((), ()))) if cb == 0 else _bdot_raw(ct, a, (((0,), (fa,)), ((), ())))
    return da, db


_bdot_c.defvjp(_bdot_c_fwd, _bdot_c_bwd)


def _bdot(a, b, dn=NN):
    return _bdot_c(a, b, dn[0][0][0], dn[0][1][0])


def _bdot_nn(a, b):
    return _bdot(a, b, NN)


def _split3(x):
    hi = x.astype(BF16)
    rest = x - hi.astype(F32)
    mid = rest.astype(BF16)
    return hi, mid, (rest - mid.astype(F32)).astype(BF16)


def _running_sum(x, dn):
    c = x.shape[0]
    row = lax.broadcasted_iota(jnp.int32, (c, c), 0)
    col = lax.broadcasted_iota(jnp.int32, (c, c), 1)
    tri = jnp.where(row >= col, 1.0, 0.0).astype(BF16)
    w = x.shape[1]
    parts = lax.dot_general(tri, jnp.concatenate(_split3(x), axis=1), dn, preferred_element_type=F32)
    return parts[:, :w] + parts[:, w:2 * w] + parts[:, 2 * w:]


@jax.custom_vjp
def _cumsum_rows(x):
    return _running_sum(x, NN)


_cumsum_rows.defvjp(lambda x: (_running_sum(x, NN), None), lambda _, ct: (_running_sum(ct, TN),))


@jax.custom_vjp
def _fold_rows(x):
    c = x.shape[0] // 2
    return x[:c] + x[c:]


_fold_rows.defvjp(lambda x: (_fold_rows(x), None), lambda _, ct: (jnp.concatenate([ct, ct], axis=0),))


@jax.custom_vjp
def _halves(x):
    n = x.shape[0] // 2
    return x[:n], x[n:]


_halves.defvjp(lambda x: (_halves(x), None), lambda _, cts: (jnp.concatenate(cts, axis=0),))


@jax.custom_vjp
def _quarters(x):
    n = x.shape[0] // 2
    return x[:n, :n], x[:n, n:], x[n:, :n], x[n:, n:]


_quarters.defvjp(lambda x: (_quarters(x), None),
                 lambda _, cts: (jnp.concatenate([jnp.concatenate(cts[:2], axis=1),
                                                  jnp.concatenate(cts[2:], axis=1)], axis=0),))


@jax.custom_vjp
def _sigmoid(x):
    return 1.0 / (1.0 + jnp.exp(-x))


def _sigmoid_fwd(x):
    s = _sigmoid(x)
    return s, s


_sigmoid.defvjp(_sigmoid_fwd, lambda s, ct: (ct * s * (1.0 - s),))


def _pick(n, cands):
    for c in cands:
        if n % c == 0:
            return c
    return n


def _cparams(sem, vmem_mb=None):
    kw = dict(dimension_semantics=sem)
    if vmem_mb is not None:
        kw["vmem_limit_bytes"] = vmem_mb * 1024 * 1024
    return pltpu.CompilerParams(**kw)


def _matmul(a, b, mode, *, name, extras=(), epilogue=None, out_dtypes=(F32,), tm=1024, tn=1024, tk=1024,
            hosted=(), scatter=False, a_map=lambda tile: tile):
    if mode == "nn":
        (M, K), (_, N) = a.shape, b.shape
    elif mode == "tn":
        (K, M), (_, N) = a.shape, b.shape
    else:
        (M, K), (N, _) = a.shape, b.shape
    tm = _pick(M, (tm, 512, 256, 128))
    tn = _pick(N, (tn, 896, 768, 512, 384, 256, 128))
    tk = _pick(K, (tk, 512, 256, 128))
    nk = K // tk
    ne, nout = len(extras), len(out_dtypes)
    if mode == "nn":
        a_spec = pl.BlockSpec((tm, tk), lambda i, j, k: (i, k))
        b_spec = pl.BlockSpec((tk, tn), lambda i, j, k: (k, j))
        dn = NN
    elif mode == "tn":
        a_spec = pl.BlockSpec((tk, tm), lambda i, j, k: (k, i))
        b_spec = pl.BlockSpec((tk, tn), lambda i, j, k: (k, j))
        dn = TN
    else:
        a_spec = pl.BlockSpec((tm, tk), lambda i, j, k: (i, k))
        b_spec = pl.BlockSpec((tn, tk), lambda i, j, k: (j, k))
        dn = NT
    o_spec = pl.BlockSpec((tm, tn), lambda i, j, k: (i, j))
    grid = (M // tm, N // tn, nk)
    nex = len(hosted)

    def body(*refs):
        a_ref, b_ref = refs[:2]
        e_refs = refs[2:2 + ne]
        ex_in = refs[2 + ne:2 + ne + nex]
        o_refs = refs[2 + ne + nex:2 + ne + nex + nout]
        ex_out = refs[2 + ne + nex + nout:2 + ne + 2 * nex + nout]
        scratch = refs[2 + ne + 2 * nex + nout:]
        kstep = pl.program_id(2)
        if nex:
            at = [pl.program_id(d) for d in range(3)]
            first = jnp.logical_and(jnp.logical_and(at[0] == 0, at[1] == 0), at[2] == 0)
            last = jnp.logical_and(jnp.logical_and(at[0] == grid[0] - 1, at[1] == grid[1] - 1), at[2] == grid[2] - 1)
            _hosted_exchange(first, last, ex_in, ex_out, scratch[-3:], scatter)

        def finish(total):
            outs = (total,) if epilogue is None else epilogue(total, *[e[...] for e in e_refs])
            for o_ref, o in zip(o_refs, outs):
                o_ref[...] = o.astype(o_ref.dtype)

        if nk == 1:
            finish(_bdot_raw(a_map(a_ref[...]), b_ref[...], dn))
            return
        acc = scratch[0]

        @pl.when(kstep == 0)
        def _():
            acc[...] = jnp.zeros_like(acc)

        acc[...] += _bdot_raw(a_map(a_ref[...]), b_ref[...], dn)

        @pl.when(kstep == nk - 1)
        def _():
            finish(acc[...])

    hbm = pl.BlockSpec(memory_space=pltpu.HBM)
    outs = pl.pallas_call(
        body,
        grid=grid,
        in_specs=[a_spec, b_spec] + [o_spec] * ne + [hbm] * nex,
        out_specs=[o_spec] * nout + [hbm] * nex,
        out_shape=[jax.ShapeDtypeStruct((M, N), dt) for dt in out_dtypes] + _exchange_out_shapes(hosted, scatter),
        scratch_shapes=([pltpu.VMEM((tm, tn), F32)] if nk > 1 else []) + (_exchange_scratch(nex) if nex else []),
        compiler_params=_cparams(("arbitrary",) * 3 if nex else ("parallel", "parallel", "arbitrary"), 56),
        name=name,
    )(a, b, *extras, *hosted)
    if nex:
        return outs[:nout], outs[nout:]
    return outs[0] if nout == 1 else outs


def _rowwise(fn, rows, pars, out_rows, out_accs, *, tile, name, nsub=1, hosted=(), scatter=False):
    rows = [r if isinstance(r, tuple) else (r, r.shape[1], 0) for r in rows]
    R = rows[0][0].shape[0]
    tile = min(tile, R)
    chunk = tile // nsub
    ntile = R // tile
    nr, npar, nor, noa, nex = len(rows), len(pars), len(out_rows), len(out_accs), len(hosted)

    def body(*refs):
        rin = refs[:nr]
        pin = refs[nr:nr + npar]
        ex_in = refs[nr + npar:nr + npar + nex]
        orow = refs[nr + npar + nex:nr + npar + nex + nor]
        oacc = refs[nr + npar + nex + nor:nr + npar + nex + nor + noa]
        ex_out = refs[nr + npar + nex + nor + noa:nr + npar + 2 * nex + nor + noa]
        step = pl.program_id(0)
        _hosted_exchange(step == 0, step == ntile - 1, ex_in, ex_out, refs[nr + npar + 2 * nex + nor + noa:], scatter)
        pvals = [p[...] for p in pin]
        totals = []
        for sub in range(nsub):
            at = slice(sub * chunk, (sub + 1) * chunk)
            outs = fn(*[r[at, :] for r in rin], *pvals)
            for ref, o in zip(orow, outs[:nor]):
                if isinstance(o, (tuple, list)):
                    col = 0
                    for piece in o:
                        ref[at, col:col + piece.shape[1]] = piece.astype(ref.dtype)
                        col += piece.shape[1]
                else:
                    ref[at, :] = o.astype(ref.dtype)
            accs = list(outs[nor:])
            totals = accs if sub == 0 else [t + a for t, a in zip(totals, accs)]

        def accumulate(ref, o):
            @pl.when(step == 0)
            def _():
                ref[...] = o

            @pl.when(step > 0)
            def _():
                ref[...] += o

        for ref, o in zip(oacc, totals):
            accumulate(ref, o)

    def colspec(width, cb):
        return pl.BlockSpec((tile, width), lambda i: (i, cb))

    hbm = pl.BlockSpec(memory_space=pltpu.HBM)
    outs = pl.pallas_call(
        body,
        grid=(ntile,),
        in_specs=[colspec(w, cb) for (_, w, cb) in rows]
        + [pl.BlockSpec(p.shape, lambda i: (0, 0), pipeline_mode=pl.Buffered(1)) for p in pars] + [hbm] * nex,
        out_specs=[colspec(w, 0) for (w, _) in out_rows]
        + [pl.BlockSpec(s, lambda i: (0, 0)) for s in out_accs] + [hbm] * nex,
        out_shape=[jax.ShapeDtypeStruct((R, w), dt) for (w, dt) in out_rows]
        + [jax.ShapeDtypeStruct(s, F32) for s in out_accs] + _exchange_out_shapes(hosted, scatter),
        scratch_shapes=_exchange_scratch(nex) if nex else [],
        compiler_params=_cparams(("arbitrary",), 56),
        name=name,
    )(*[r[0] for r in rows], *pars, *hosted)
    return (outs[:nor + noa], outs[nor + noa:]) if nex else outs


def _rms_fn(x, g):
    return x * lax.rsqrt(jnp.mean(x * x, axis=-1, keepdims=True) + RMS_EPS) * g


FUSED_TILE = 512
FUSED_CHUNKS = 2


def _relu_squared(u):
    pos = jnp.maximum(u.astype(F32), 0.0)
    return pos * pos


def _down_proj_loss(u, w_down, x1, tgt, g):
    d = x1.shape[1]

    def fn(uv, xv, tv, wv, gv):
        x2 = xv + _bdot_raw(_relu_squared(uv), wv, NN)
        y, vjp = jax.vjp(_rms_fn, x2, gv)
        err = y - tv
        loss = 0.5 * jnp.sum(jnp.sum(err * err, axis=-1, keepdims=True), axis=0, keepdims=True) / d
        dx, dg = vjp(err / d)
        return dx, dx, jnp.broadcast_to(loss, (1, LANES)), dg

    return _rowwise(fn, [u, x1, tgt], [w_down, g], [(d, F32), (d, BF16)], [(1, LANES), g.shape],
                    tile=FUSED_TILE, nsub=FUSED_CHUNKS, name="mlp_down_final_norm_loss")


def _proj_bwd_norm_bwd(cts, weights_t, x, dres, g, name, hosted=(), scatter=False):
    n = len(cts)

    def fn(*vals):
        ctv, (xv, dresv), wv, gv = vals[:n], vals[n:n + 2], vals[n + 2:2 * n + 2], vals[-1]
        dh = _bdot_raw(ctv[0], wv[0], NT)
        for c, w in zip(ctv[1:], wv[1:]):
            dh = dh + _bdot_raw(c, w, NT)
        _, vjp = jax.vjp(_rms_fn, xv, gv)
        dx, dg = vjp(dh)
        return dx + dresv, dg

    return _rowwise(fn, [*cts, x, dres], [*weights_t, g], [(x.shape[1], F32)], [g.shape],
                    tile=FUSED_TILE, nsub=FUSED_CHUNKS, name=name, hosted=hosted, scatter=scatter)


def _head_sum_matrix():
    i = lax.broadcasted_iota(jnp.int32, (RWKV_WIDTH, RWKV_WIDTH), 0) // HEAD_DIM
    j = lax.broadcasted_iota(jnp.int32, (RWKV_WIDTH, RWKV_WIDTH), 1) // HEAD_DIM
    return (i == j).astype(BF16)


def _head_sums_raw(x, esum):
    hi = x.astype(BF16)
    lo = (x - hi.astype(F32)).astype(BF16)
    return _dot(hi, esum) + _dot(lo, esum)


@jax.custom_vjp
def _head_sums(x, esum):
    return _head_sums_raw(x, esum)


_head_sums.defvjp(lambda x, esum: (_head_sums_raw(x, esum), esum),
                  lambda esum, ct: (_head_sums_raw(ct, esum), jnp.zeros_like(esum)))


def _prep_core(xr, xk, xv, xwa, xg, w0, w2p, a0, a2p, g2, k_k, k_a, esum):
    lw = -DECAY_SCALE * _sigmoid(w0 + _bdot_nn(jnp.tanh(xwa), w2p))
    a = _sigmoid(a0 + _bdot_nn(xwa, a2p))
    g = _bdot_nn(_sigmoid(xg), g2)
    kk0 = xk * k_k
    kk = kk0 * jnp.minimum(lax.rsqrt(_head_sums(kk0 * kk0, esum)), 1.0 / L2_EPS)
    k = xk * (1.0 + (a - 1.0) * k_a)
    return xr, lw, k, xv, kk, a, g


_SEGS = ((0, 512), (512, 1024), (1024, 1536), (1536, 1664), (1664, 1792))


PREP_TILE = 256
SUBLANES = 8


def _shifted_tokens(z_ref, zprev_ref, tile_index, seq):
    zc = z_ref[...]
    start = (tile_index * PREP_TILE) % seq == 0
    before = jnp.where(start, 0.0, zprev_ref[SUBLANES - 1:SUBLANES, :])
    rowid = lax.broadcasted_iota(jnp.int32, zc.shape, 0)
    return zc, jnp.where(rowid == 0, before, pltpu.roll(zc, 1, 0))


def _prep_specs(z, mu, pars, index):
    width = z.shape[1]
    per = PREP_TILE // SUBLANES
    return ([pl.BlockSpec((PREP_TILE, width), lambda i: (index(i), 0)),
             pl.BlockSpec((SUBLANES, width), lambda i: (jnp.maximum(index(i) * per - 1, 0), 0))],
            [pl.BlockSpec(p.shape, lambda i: (0, 0)) for p in (mu, *pars)])


def _norm_in_proj_prep(x, seq, g, w_attn, w_rw, mu, pars):
    rows, d = x.shape
    chunk = FUSED_TILE // FUSED_CHUNKS
    npar = len(pars)
    wa_width, wr_width = w_attn.shape[1], w_rw.shape[1]

    def body(x_ref, g_ref, wa_ref, wr_ref, mu_ref, *rest):
        par_refs = rest[:npar]
        h_ref, za_ref, zr_ref = rest[npar:npar + 3]
        out_refs, carry = rest[npar + 3:-1], rest[-1]
        step = pl.program_id(0)

        @pl.when(step == 0)
        def _():
            carry[...] = jnp.zeros_like(carry)

        pv = [p[...] for p in par_refs]
        for sub in range(FUSED_CHUNKS):
            at = slice(sub * chunk, (sub + 1) * chunk)
            h = _rms_fn(x_ref[at, :], g_ref[...])
            h_ref[at, :] = h.astype(h_ref.dtype)
            za_ref[at, :] = _bdot_raw(h, wa_ref[...], NN)
            zc = _bdot_raw(h, wr_ref[...], NN)
            zr_ref[at, :] = zc
            start = (step * FUSED_TILE + sub * chunk) % seq == 0
            before = jnp.where(start, 0.0, carry[SUBLANES - 1:SUBLANES, :])
            rowid = lax.broadcasted_iota(jnp.int32, zc.shape, 0)
            zp = jnp.where(rowid == 0, before, pltpu.roll(zc, 1, 0))
            carry[...] = zc[chunk - SUBLANES:chunk, :]
            zs = zc + (zp - zc) * mu_ref[...]
            outs = _prep_core(*[zs[:, a:b] for a, b in _SEGS], *pv)
            for ref, o in zip(out_refs, outs):
                ref[at, :] = o

    tiled = lambda width: pl.BlockSpec((FUSED_TILE, width), lambda i: (i, 0))
    resident = lambda a: pl.BlockSpec(a.shape, lambda i: (0, 0), pipeline_mode=pl.Buffered(1))
    return pl.pallas_call(
        body,
        grid=(rows // FUSED_TILE,),
        in_specs=[tiled(d)] + [resident(a) for a in (g, w_attn, w_rw, mu, *pars)],
        out_specs=[tiled(d), tiled(wa_width), tiled(wr_width)] + [tiled(RWKV_WIDTH)] * 7,
        out_shape=[jax.ShapeDtypeStruct((rows, d), BF16), jax.ShapeDtypeStruct((rows, wa_width), F32),
                   jax.ShapeDtypeStruct((rows, wr_width), F32)] + [jax.ShapeDtypeStruct((rows, RWKV_WIDTH), F32)] * 7,
        scratch_shapes=[pltpu.VMEM((SUBLANES, wr_width), F32)],
        compiler_params=_cparams(("arbitrary",), 56),
        name="attn_norm_in_proj_rwkv_prep",
    )(x, g, w_attn, w_rw, mu, *pars)


def _prep_bwd(z, seq, cts, mu, pars):
    rows, width = z.shape
    ntile = rows // PREP_TILE
    npar, nct = len(pars), len(cts)
    acc_shapes = [(1, b - a) for a, b in _SEGS] + [p.shape for p in pars[:-1]]

    def body(z_ref, zprev_ref, *rest):
        ct_refs = rest[:nct]
        mu_ref = rest[nct]
        par_refs = rest[nct + 1:nct + 1 + npar]
        dz_ref = rest[nct + 1 + npar]
        acc_refs = rest[nct + 2 + npar:-1]
        carry = rest[-1]
        step = pl.program_id(0)
        tile_index = ntile - 1 - step

        @pl.when(step == 0)
        def _():
            carry[...] = jnp.zeros_like(carry)

        zc, zp = _shifted_tokens(z_ref, zprev_ref, tile_index, seq)
        mu_v = mu_ref[...]
        diff = zp - zc
        zs = zc + diff * mu_v
        dra, drb, dlw, dka, dkb, dva, dvb, dkk, da, dg = [c[...] for c in ct_refs]
        pv = [p[...] for p in par_refs]
        _, vjp = jax.vjp(lambda *args: _prep_core(*args, pv[-1]), *[zs[:, a:b] for a, b in _SEGS], *pv[:-1])
        grads = vjp((dra + drb, dlw, dka + dkb, dva + dvb, dkk, da, dg))
        dsegs, dpars = grads[:5], grads[5:]
        last_of_sequence = ((tile_index + 1) * PREP_TILE) % seq == 0
        accs = []
        for ds, (a, b) in zip(dsegs, _SEGS):
            mu_s = mu_v[:, a:b]
            dzp = ds * mu_s
            after = jnp.where(last_of_sequence, 0.0, carry[0:1, a:b])
            rowid = lax.broadcasted_iota(jnp.int32, dzp.shape, 0)
            from_next = jnp.where(rowid == PREP_TILE - 1, after, pltpu.roll(dzp, PREP_TILE - 1, 0))
            dz_ref[:, a:b] = (ds * (1.0 - mu_s) + from_next).astype(dz_ref.dtype)
            carry[:, a:b] = dzp[0:SUBLANES, :]
            accs.append(jnp.sum(ds * diff[:, a:b], axis=0, keepdims=True))
        accs.extend(dpars)

        def accumulate(ref, o):
            @pl.when(step == 0)
            def _():
                ref[...] = o

            @pl.when(step > 0)
            def _():
                ref[...] += o

        for ref, o in zip(acc_refs, accs):
            accumulate(ref, o)

    rev = lambda i: ntile - 1 - i
    zspecs, pspecs = _prep_specs(z, mu, pars, rev)
    return pl.pallas_call(
        body,
        grid=(ntile,),
        in_specs=zspecs + [pl.BlockSpec((PREP_TILE, RWKV_WIDTH), lambda i: (rev(i), 0))] * nct + pspecs,
        out_specs=[pl.BlockSpec((PREP_TILE, width), lambda i: (rev(i), 0))]
        + [pl.BlockSpec(s, lambda i: (0, 0)) for s in acc_shapes],
        out_shape=[jax.ShapeDtypeStruct((rows, width), BF16)] + [jax.ShapeDtypeStruct(s, F32) for s in acc_shapes],
        scratch_shapes=[pltpu.VMEM((SUBLANES, width), F32)],
        compiler_params=_cparams(("arbitrary",), 56),
        name="rwkv_prep_bwd",
    )(z, z, *cts, mu, *pars)


def _post_fn(y, r, k, v, g, ln_w, ln_b, r_k, esum):
    mean = _head_sums(y, esum) * (1.0 / HEAD_DIM)
    yc = y - mean
    var = _head_sums(yc * yc, esum) * (1.0 / HEAD_DIM)
    yn = yc * lax.rsqrt(var + GN_EPS) * ln_w + ln_b
    bonus = _head_sums(r * k * r_k, esum) * v
    return (yn + bonus) * g


def _post_out_proj_norm(y, r, k, v, g, attn_out, x, pars, w_attn_rows, w_rwkv_rows, g_norm):
    npar = len(pars)

    def fn(yv, rv, kv, vv, gv, av, xv, *rest):
        wa, wr, gn = rest[npar:]
        rw = _post_fn(yv, rv, kv, vv, gv, *rest[:npar])
        x1 = xv + _bdot_raw(av, wa, NN) + _bdot_raw(rw, wr, NN)
        return rw, x1, _rms_fn(x1, gn)

    d = x.shape[1]
    return _rowwise(fn, [y, r, k, v, g, attn_out, x], [*pars, w_attn_rows, w_rwkv_rows, g_norm],
                    [(RWKV_WIDTH, BF16), (d, F32), (d, BF16)], [],
                    tile=FUSED_TILE, nsub=FUSED_CHUNKS, name="rwkv_post_out_proj_mlp_norm")


def _out_proj_bwd_post_bwd(dx1, y, r, k, v, g, pars, w_attn_rows, w_rwkv_rows):
    npar = len(pars)

    def fn(dxv, yv, rv, kv, vv, gv, *rest):
        wa, wr = rest[npar:]
        esum = rest[npar - 1]
        d_attn = _bdot_raw(dxv, wa, NT)
        d_rw = _bdot_raw(dxv, wr, NT)
        _, vjp = jax.vjp(lambda *a: _post_fn(*a, esum), yv, rv, kv, vv, gv, *rest[:npar - 1])
        return (d_attn, *vjp(d_rw))

    return _rowwise(fn, [dx1, y, r, k, v, g], [*pars, w_attn_rows, w_rwkv_rows], [(RWKV_WIDTH, F32)] * 6,
                    [p.shape for p in pars[:-1]], tile=FUSED_TILE, nsub=FUSED_CHUNKS, name="out_proj_bwd_rwkv_post_bwd")


def _tri_inverses(ms, tick=lambda: None):
    n = ms[0].shape[0]
    row = lax.broadcasted_iota(jnp.int32, (n, n), 0)
    col = lax.broadcasted_iota(jnp.int32, (n, n), 1)
    eye = jnp.where(row == col, 1.0, 0.0)
    t_inv = [eye + m for m in ms]
    power = [_bdot_raw(m, m, NN) for m in ms]
    steps = int(math.log2(n // 2)) - 1
    for step in range(steps):
        if step < steps - 1:
            both = [_bdot_raw(jnp.concatenate([p, t], axis=0), p, NN) for p, t in zip(power, t_inv)]
            power = [b[:n] for b in both]
            t_inv = [t + b[n:] for t, b in zip(t_inv, both)]
        else:
            t_inv = [t + _bdot_raw(t, p, NN) for t, p in zip(t_inv, power)]
        tick()
    return t_inv


def _tri_solve_bwd(res, dus):
    t_inv, us = res
    dxs = tuple(_bdot_raw(t, du, TN) for t, du in zip(t_inv, dus))
    dms = tuple(_bdot_raw(dx, u, NT) for dx, u in zip(dxs, us))
    return dms, dxs


@functools.partial(jax.custom_vjp, nondiff_argnums=(3, 4))
def _kept_bdot_c(a, b, kept, ca, cb):
    return kept.astype(F32)


def _kept_bdot_c_fwd(a, b, kept, ca, cb):
    return kept.astype(F32), (a, b, kept)


def _kept_bdot_c_bwd(ca, cb, res, ct):
    a, b, kept = res
    return (*_bdot_c_bwd(ca, cb, (a, b), ct), jnp.zeros_like(kept))


_kept_bdot_c.defvjp(_kept_bdot_c_fwd, _kept_bdot_c_bwd)


def _kept_bdot(a, b, kept, dn):
    return _kept_bdot_c(a, b, kept, dn[0][0][0], dn[0][1][0])


@jax.custom_vjp
def _kept_tri_solve(ms, xs, t_inv, us):
    return tuple(u.astype(F32) for u in us)


def _kept_tri_solve_fwd(ms, xs, t_inv, us):
    return tuple(u.astype(F32) for u in us), (t_inv, us)


def _kept_tri_solve_bwd(res, dus):
    t_inv, us = res
    dms, dxs = _tri_solve_bwd(res, dus)
    return dms, dxs, tuple(jnp.zeros_like(t) for t in t_inv), tuple(jnp.zeros_like(u) for u in us)


_kept_tri_solve.defvjp(_kept_tri_solve_fwd, _kept_tri_solve_bwd)


def _chunk_fn(ss, rs, lws, ks, vs, kks, als, kept=None, tick=lambda: None):
    c = rs[0].shape[0]
    n = 2 * c
    row = lax.broadcasted_iota(jnp.int32, (n, n), 0)
    col = lax.broadcasted_iota(jnp.int32, (n, n), 1)
    incl = (row % c) >= (col % c)
    strict = (row % c) > (col % c)
    lane = lax.broadcasted_iota(jnp.int32, (1, LANES), 1)
    m_lo = jnp.where(lane < HEAD_DIM, 1.0, 0.0)
    m_hi = 1.0 - m_lo

    def stack(a):
        return jnp.concatenate([a * m_lo, a * m_hi], axis=0)

    cums = [_cumsum_rows(lw) for lw in lws]
    totals = [jnp.sum(lw, axis=0, keepdims=True) for lw in lws]
    bs = [kk * al for kk, al in zip(kks, als)]
    grows = [jnp.exp(-cum) for cum in cums]
    a_s = [stack(-kk * jnp.exp(cum - lw)) for kk, cum, lw in zip(kks, cums, lws)]
    b_s = [stack(b * g) for b, g in zip(bs, grows)]
    k_s = [stack(k * g) for k, g in zip(ks, grows)]
    r_s = [stack(r * jnp.exp(cum)) for r, cum in zip(rs, cums)]
    v_s = [stack(v) for v in vs]
    tick()
    pair = lambda p, q: jnp.concatenate([p, q], axis=0)
    ar_s = [pair(a, r) for a, r in zip(a_s, r_s)]
    if kept is None:
        products = [_bdot_raw(ar, pair(b, k), NT) for ar, b, k in zip(ar_s, b_s, k_s)]
    else:
        products = [_kept_bdot(ar, pair(b, k), kp, NT) for ar, b, k, kp in zip(ar_s, b_s, k_s, kept[0])]
    tick()
    blocks = [_quarters(p) for p in products]
    m_ab = [jnp.where(strict, q[0], 0.0) for q in blocks]
    m_ak = [jnp.where(strict, q[1], 0.0) for q in blocks]
    m_rb = [jnp.where(incl, q[2], 0.0) for q in blocks]
    m_rk = [jnp.where(incl, q[3], 0.0) for q in blocks]
    from_state = [_halves(_bdot(ar, s, NT)) for ar, s in zip(ar_s, ss)]
    tick()
    from_v = [_halves(_bdot(pair(mk, mr), v)) for mk, mr, v in zip(m_ak, m_rk, v_s)]
    tick()
    x = tuple(fs[0] + fv[0] for fs, fv in zip(from_state, from_v))
    if kept is None:
        t_inv = _tri_inverses(m_ab, tick)
        u = [_bdot_raw(t, xx, NN) for t, xx in zip(t_inv, x)]
        tick()
    else:
        u = _kept_tri_solve(tuple(m_ab), x, kept[1], kept[2])
    y = [_fold_rows(fs[1] + _bdot(mb, uu) + fv[1]) for fs, mb, uu, fv in zip(from_state, m_rb, u, from_v)]
    tick()
    tails = [jnp.exp(tot - cum) for tot, cum in zip(totals, cums)]
    s_new = [s * jnp.exp(tot) + _bdot(pair(uu, v), pair(stack(b * tl), stack(k * tl)), TN)
             for s, tot, uu, b, tl, v, k in zip(ss, totals, u, bs, tails, v_s, ks)]
    if kept is None:
        keep = lambda vals: tuple(v.astype(BF16) for v in vals)
        return tuple(y), tuple(s_new), (keep(products), keep(t_inv), keep(u))
    return tuple(y), tuple(s_new)


def _chains(bsz, npair):
    return [(b, p, slice(p * LANES, (p + 1) * LANES)) for b in range(bsz) for p in range(npair)]


def _hosted_exchange(first, last, ex_in, ex_out, sems, scatter):
    if not ex_in:
        return

    @pl.when(first)
    def _():
        _exchange_start(_exchange_copies(ex_in, ex_out, *sems, scatter, arrivals=False))

    @pl.when(last)
    def _():
        _exchange_wait(_exchange_copies(ex_in, ex_out, *sems, scatter, arrivals=True))


def _rwkv_attn_fwd(r, lw, k, v, kk, al, z, sink_rows, hosted=(), scatter=False):
    bsz, t, w = r.shape
    npair, nchunk = w // LANES, t // CHUNK
    nb = t // BLOCK
    assert bsz * nb == nchunk
    chains = _chains(bsz, npair)
    nex = len(hosted)
    apair = ATTN_WIDTH // LANES

    def body(*refs):
        r_ref, lw_ref, k_ref, v_ref, kk_ref, al_ref = refs[:6]
        q_ref, kp_ref, kc_ref, vp_ref, vc_ref, sink_ref = refs[6:12]
        ex_in = refs[12:12 + nex]
        y_ref, sall_ref, prod_ref, tinv_ref, u_ref, o_ref, p_ref, ps_ref = refs[12 + nex:20 + nex]
        ex_out = refs[20 + nex:20 + 2 * nex]
        s_scr = refs[20 + 2 * nex]
        step = pl.program_id(0)

        @pl.when(step == 0)
        def _():
            s_scr[...] = jnp.zeros_like(s_scr)

        _hosted_exchange(step == 0, step == nchunk - 1, ex_in, ex_out, refs[21 + 2 * nex:], scatter)
        qs = tuple(q_ref[0, :, pair * LANES:(pair + 1) * LANES] for pair in range(apair))
        attn = []
        stages = _attn_block_stages(qs, kp_ref[0], kc_ref[0], vp_ref[0], vc_ref[0], _sink_values(sink_ref),
                                    step % nb == 0, attn)
        ss = tuple(s_scr[i] for i in range(len(chains)))
        for i, s in enumerate(ss):
            sall_ref[0, i] = s
        ys, s_new, kept = _chunk_fn(ss, *[tuple(ref[b, :, cols] for b, _, cols in chains)
                                          for ref in (r_ref, lw_ref, k_ref, v_ref, kk_ref, al_ref)],
                                    tick=lambda: next(stages, None))
        for _ in stages:
            pass
        for i, (b, _, cols) in enumerate(chains):
            y_ref[b, :, cols] = ys[i]
            s_scr[i] = s_new[i]
            prod_ref[0, i], tinv_ref[0, i], u_ref[0, i] = kept[0][i], kept[1][i], kept[2][i]
        outs, probs, psinks = attn
        for pair in range(apair):
            o_ref[0, :, pair * LANES:(pair + 1) * LANES] = outs[pair].astype(o_ref.dtype)
        for h, p in enumerate(probs):
            p_ref[0, 0, h] = p
        ps_ref[0] = _head_columns(psinks)

    spec = pl.BlockSpec((bsz, CHUNK, w), lambda c: (0, c, 0))
    hbm = pl.BlockSpec(memory_space=pltpu.HBM)
    per_chunk = lambda n: pl.BlockSpec((1, len(chains), n, n), lambda c: (c, 0, 0, 0))
    kept_shape = lambda n: jax.ShapeDtypeStruct((nchunk, len(chains), n, n), BF16)
    kcol, vcol = ATTN_WIDTH // KV_WIDTH, ATTN_WIDTH // KV_WIDTH + 1
    before = lambda c: jnp.maximum(c % nb - 1, 0)
    outs = pl.pallas_call(
        body,
        grid=(nchunk,),
        in_specs=[spec] * 6
        + [pl.BlockSpec((1, BLOCK, ATTN_WIDTH), lambda c: (c // nb, c % nb, 0)),
           pl.BlockSpec((1, BLOCK, KV_WIDTH), lambda c: (c // nb, before(c), kcol)),
           pl.BlockSpec((1, BLOCK, KV_WIDTH), lambda c: (c // nb, c % nb, kcol)),
           pl.BlockSpec((1, BLOCK, KV_WIDTH), lambda c: (c // nb, before(c), vcol)),
           pl.BlockSpec((1, BLOCK, KV_WIDTH), lambda c: (c // nb, c % nb, vcol)),
           pl.BlockSpec(sink_rows.shape, lambda c: (0, 0))]
        + [hbm] * nex,
        out_specs=[spec, per_chunk(LANES), per_chunk(4 * CHUNK), per_chunk(2 * CHUNK), per_chunk(2 * CHUNK),
                   pl.BlockSpec((1, BLOCK, ATTN_WIDTH), lambda c: (c // nb, c % nb, 0)),
                   pl.BlockSpec((1, 1, N_ATTN_HEADS, BLOCK, 2 * BLOCK), lambda c: (c // nb, c % nb, 0, 0, 0)),
                   pl.BlockSpec((1, BLOCK, LANES), lambda c: (c // nb, c % nb, 0))]
        + [hbm] * nex,
        out_shape=[jax.ShapeDtypeStruct((bsz, t, w), F32),
                   jax.ShapeDtypeStruct((nchunk, len(chains), LANES, LANES), F32),
                   kept_shape(4 * CHUNK), kept_shape(2 * CHUNK), kept_shape(2 * CHUNK),
                   jax.ShapeDtypeStruct((bsz, t, ATTN_WIDTH), BF16),
                   jax.ShapeDtypeStruct((bsz, nb, N_ATTN_HEADS, BLOCK, 2 * BLOCK), F32),
                   jax.ShapeDtypeStruct((bsz, t, LANES), F32)]
        + _exchange_out_shapes(hosted, scatter),
        scratch_shapes=[pltpu.VMEM((len(chains), LANES, LANES), F32)] + (_exchange_scratch(nex) if nex else []),
        compiler_params=_cparams(("arbitrary",), 48),
        name="rwkv_chunk_swa_fwd",
    )(r, lw, k, v, kk, al, z, z, z, z, z, sink_rows, *hosted)
    return outs[0], outs[1:5], outs[5:8], outs[8:]


def _rwkv_attn_bwd(r, lw, k, v, kk, al, from_fwd, dy, z, dout, probs, psinks, hosted=(), scatter=False):
    bsz, t, w = r.shape
    npair, nchunk = w // LANES, t // CHUNK
    nb = t // BLOCK
    assert bsz * nb == nchunk
    chains = _chains(bsz, npair)
    nex = len(hosted)
    apair = ATTN_WIDTH // LANES

    def body(*refs):
        r_ref, lw_ref, k_ref, v_ref, kk_ref, al_ref, s_ref, prod_ref, tinv_ref, u_ref, dy_ref = refs[:11]
        q_ref, kp_ref, kc_ref, vp_ref, vc_ref, do_ref, p_ref, ps_ref = refs[11:19]
        ex_in = refs[19:19 + nex]
        out_refs = refs[19 + nex:25 + nex]
        dz_ref, dsink_ref = refs[25 + nex:27 + nex]
        ex_out = refs[27 + nex:27 + 2 * nex]
        ds_scr, carry = refs[27 + 2 * nex:29 + 2 * nex]
        step = pl.program_id(0)

        @pl.when(step == 0)
        def _():
            ds_scr[...] = jnp.zeros_like(ds_scr)
            dsink_ref[...] = jnp.zeros_like(dsink_ref)

        @pl.when(step % nb == 0)
        def _():
            carry[...] = jnp.zeros_like(carry)

        _hosted_exchange(step == 0, step == nchunk - 1, ex_in, ex_out, refs[29 + 2 * nex:], scatter)
        qs = tuple(q_ref[0, :, pair * LANES:(pair + 1) * LANES] for pair in range(apair))
        dos = tuple(do_ref[0, :, pair * LANES:(pair + 1) * LANES] for pair in range(apair))
        attn = []
        stages = _attn_block_bwd_stages(qs, kp_ref[0], kc_ref[0], vp_ref[0], vc_ref[0], dos,
                                        [p_ref[0, 0, h] for h in range(N_ATTN_HEADS)], step % nb == nb - 1, attn)
        ss = tuple(s_ref[0, i] for i in range(len(chains)))
        kept = tuple(tuple(ref[0, i] for i in range(len(chains))) for ref in (prod_ref, tinv_ref, u_ref))
        _, vjp = jax.vjp(functools.partial(_chunk_fn, kept=kept), ss,
                         *[tuple(ref[b, :, cols] for b, _, cols in chains)
                           for ref in (r_ref, lw_ref, k_ref, v_ref, kk_ref, al_ref)])
        rules = [0]

        def one_stage_per_round():
            rules[0] += 1
            if rules[0] % len(chains) == 0:
                next(stages, None)

        _BESIDE_BACKWARD[0] = one_stage_per_round
        try:
            grads = vjp((tuple(dy_ref[b, :, cols] for b, _, cols in chains),
                         tuple(ds_scr[i] for i in range(len(chains)))))
        finally:
            _BESIDE_BACKWARD[0] = None
        for _ in stages:
            pass
        for i, (b, _, cols) in enumerate(chains):
            ds_scr[i] = grads[0][i]
            for ref, gval in zip(out_refs, grads[1:]):
                ref[b, :, cols] = gval[i]
        dqs, dkp, dkc, dvp, dvc, deltas = attn
        for pair in range(apair):
            dz_ref[0, :, pair * LANES:(pair + 1) * LANES] = dqs[pair].astype(dz_ref.dtype)
        dsink_ref[...] -= jnp.sum(ps_ref[0] * _head_columns(deltas), axis=0, keepdims=True)
        dz_ref[0, :, ATTN_WIDTH:ATTN_WIDTH + KV_WIDTH] = (dkc + carry[0]).astype(dz_ref.dtype)
        dz_ref[0, :, ATTN_WIDTH + KV_WIDTH:QKV_WIDTH] = (dvc + carry[1]).astype(dz_ref.dtype)
        carry[0] = dkp
        carry[1] = dvp

    spec = pl.BlockSpec((bsz, CHUNK, w), lambda c: (0, nchunk - 1 - c, 0))
    per_chunk = lambda n: pl.BlockSpec((1, len(chains), n, n), lambda c: (nchunk - 1 - c, 0, 0, 0))
    hbm = pl.BlockSpec(memory_space=pltpu.HBM)
    kcol, vcol = ATTN_WIDTH // KV_WIDTH, ATTN_WIDTH // KV_WIDTH + 1
    seq_of = lambda c: c // nb
    blk = lambda c: nb - 1 - c % nb
    before = lambda c: jnp.maximum(blk(c) - 1, 0)
    outs = pl.pallas_call(
        body,
        grid=(nchunk,),
        in_specs=[spec] * 6 + [per_chunk(LANES), per_chunk(4 * CHUNK), per_chunk(2 * CHUNK), per_chunk(2 * CHUNK), spec]
        + [pl.BlockSpec((1, BLOCK, ATTN_WIDTH), lambda c: (seq_of(c), blk(c), 0)),
           pl.BlockSpec((1, BLOCK, KV_WIDTH), lambda c: (seq_of(c), before(c), kcol)),
           pl.BlockSpec((1, BLOCK, KV_WIDTH), lambda c: (seq_of(c), blk(c), kcol)),
           pl.BlockSpec((1, BLOCK, KV_WIDTH), lambda c: (seq_of(c), before(c), vcol)),
           pl.BlockSpec((1, BLOCK, KV_WIDTH), lambda c: (seq_of(c), blk(c), vcol)),
           pl.BlockSpec((1, BLOCK, ATTN_WIDTH), lambda c: (seq_of(c), blk(c), 0)),
           pl.BlockSpec((1, 1, N_ATTN_HEADS, BLOCK, 2 * BLOCK), lambda c: (seq_of(c), blk(c), 0, 0, 0)),
           pl.BlockSpec((1, BLOCK, LANES), lambda c: (seq_of(c), blk(c), 0))]
        + [hbm] * nex,
        out_specs=[spec] * 6
        + [pl.BlockSpec((1, BLOCK, QKV_WIDTH), lambda c: (seq_of(c), blk(c), 0)),
           pl.BlockSpec((1, LANES), lambda c: (0, 0))]
        + [hbm] * nex,
        out_shape=[jax.ShapeDtypeStruct((bsz, t, w), F32)] * 6
        + [jax.ShapeDtypeStruct((bsz, t, QKV_WIDTH), BF16), jax.ShapeDtypeStruct((1, LANES), F32)]
        + _exchange_out_shapes(hosted, scatter),
        scratch_shapes=[pltpu.VMEM((len(chains), LANES, LANES), F32), pltpu.VMEM((2, BLOCK, KV_WIDTH), F32)]
        + (_exchange_scratch(nex) if nex else []),
        compiler_params=_cparams(("arbitrary",), 48),
        name="rwkv_chunk_swa_bwd",
    )(r, lw, k, v, kk, al, *from_fwd, dy, z, z, z, z, z, dout, probs, psinks, *hosted)
    return outs[:6], outs[6:8], outs[8:]


def _alibi_slope(head):
    return 2.0 ** (-8.0 * (head + 1) / N_ATTN_HEADS)


def _attn_setup(first):
    row = lax.broadcasted_iota(jnp.int32, (BLOCK, 2 * BLOCK), 0)
    col = lax.broadcasted_iota(jnp.int32, (BLOCK, 2 * BLOCK), 1)
    lane = lax.broadcasted_iota(jnp.int32, (1, LANES), 1)
    halves = [jnp.where((lane // HEAD_DIM) == half, 1.0, 0.0) for half in range(2)]
    srow = lax.broadcasted_iota(jnp.int32, (LANES, LANES), 0)
    scol = lax.broadcasted_iota(jnp.int32, (LANES, LANES), 1)
    swap = jnp.where((srow + HEAD_DIM) % LANES == scol, 1.0, 0.0)
    dist = row - col + BLOCK
    valid = jnp.logical_and(jnp.logical_and(dist >= 0, dist < BLOCK),
                            jnp.logical_or(col >= BLOCK, jnp.logical_not(first)))
    return halves, swap, dist.astype(F32), valid, HEAD_DIM ** -0.5


def _attn_keys_values(kp, kc, vp, vc, swap, npair):
    stored = (jnp.concatenate([kp, kc], axis=0), jnp.concatenate([vp, vc], axis=0))
    swapped = tuple(_bdot_raw(t, swap, NN) for t in stored)
    heads = [(pair, half) for pair in range(npair) for half in range(2)]
    return heads, [half == pair // 2 for pair, half in heads], stored, swapped


def _attn_block_stages(qs, kp, kc, vp, vc, sinks, first, result):
    halves, swap, dist, valid, scale = _attn_setup(first)
    heads, as_stored, stored, swapped = _attn_keys_values(kp, kc, vp, vc, swap, len(qs))
    kv = [stored if own else swapped for own in as_stored]
    slopes = [_alibi_slope(2 * pair + half) for pair, half in heads]
    qa = [qs[pair] * halves[half] for pair, half in heads]
    yield
    s = [jnp.where(valid, _bdot_raw(q, t[0], NT) * scale - sl * dist, NEG_INF) for q, t, sl in zip(qa, kv, slopes)]
    yield
    mx = [jnp.maximum(jnp.max(a, axis=-1, keepdims=True), sk) for a, sk in zip(s, sinks)]
    yield
    e = [jnp.exp(a - m) for a, m in zip(s, mx)]
    yield
    es = [jnp.exp(sk - m) for sk, m in zip(sinks, mx)]
    inv = [1.0 / (jnp.sum(a, axis=-1, keepdims=True) + b) for a, b in zip(e, es)]
    yield
    probs = [a * i for a, i in zip(e, inv)]
    yield
    o = [_bdot_raw(p, t[1], NN) for p, t in zip(probs, kv)]
    yield
    outs = tuple(o[2 * pair] * halves[0] + o[2 * pair + 1] * halves[1] for pair in range(len(qs)))
    result.extend([outs, probs, [b * i for b, i in zip(es, inv)]])


def _attn_block_bwd_stages(qs, kp, kc, vp, vc, dos, probs, first, result):
    halves, swap, _, _, scale = _attn_setup(first)
    heads, as_stored, stored, swapped = _attn_keys_values(kp, kc, vp, vc, swap, len(qs))
    kv = [stored if own else swapped for own in as_stored]
    qa = [qs[pair] * halves[half] for pair, half in heads]
    do = [dos[pair] * halves[half] for pair, half in heads]
    yield
    dp = [_bdot_raw(d, t[1], NT) for d, t in zip(do, kv)]
    yield
    delta = [jnp.sum(p * d, axis=-1, keepdims=True) for p, d in zip(probs, dp)]
    yield
    ds = [p * (d - dl) for p, d, dl in zip(probs, dp, delta)]
    yield
    dq = [_bdot_raw(g, t[0], NN) * (scale * halves[half]) for g, t, (_, half) in zip(ds, kv, heads)]
    yield
    dk = [_bdot_raw(g, q, TN) * scale for g, q in zip(ds, qa)]
    yield
    dv = [_bdot_raw(p, d, TN) for p, d in zip(probs, do)]
    yield
    dqs = tuple(dq[2 * pair] + dq[2 * pair + 1] for pair in range(len(qs)))

    def total(parts):
        direct = sum(g for g, own in zip(parts, as_stored) if own)
        return direct + _bdot_raw(sum(g for g, own in zip(parts, as_stored) if not own), swap, NN)

    dk_all, dv_all = total(dk), total(dv)
    result.extend([dqs, dk_all[:BLOCK], dk_all[BLOCK:], dv_all[:BLOCK], dv_all[BLOCK:], delta])


def _sink_values(sink_ref):
    return [jnp.max(sink_ref[h:h + 1, :], axis=-1, keepdims=True) for h in range(N_ATTN_HEADS)]


def _head_columns(cols):
    lane = lax.broadcasted_iota(jnp.int32, (1, LANES), 1)
    return sum(c * jnp.where(lane == h, 1.0, 0.0) for h, c in enumerate(cols))


def _exchange_out_shapes(arrays, scatter):
    return [jax.ShapeDtypeStruct((N_DEV,) + (a.shape[1:] if scatter else a.shape), a.dtype) for a in arrays]


def _exchange_scratch(n):
    return [pltpu.SemaphoreType.DMA((n, N_DEV - 1)), pltpu.SemaphoreType.DMA((n, N_DEV - 1)),
            pltpu.SemaphoreType.DMA((n,))]


def _exchange_copies(ins, outs, send_sems, recv_sems, local_sems, scatter, arrivals=True):
    x, y, c = lax.axis_index("x"), lax.axis_index("y"), lax.axis_index("c")
    me = 4 * x + 2 * y + c
    copies = []
    for i in range(len(ins)):
        own = pltpu.make_async_copy(ins[i].at[me] if scatter else ins[i], outs[i].at[me], local_sems.at[i])
        copies.append((own, None, True))
        for d in range(1, N_DEV):
            px = 1 - x if d & 4 else x
            py = 1 - y if d & 2 else y
            pc = 1 - c if d & 1 else c
            peer = 4 * px + 2 * py + pc
            src = ins[i].at[peer] if scatter else ins[i]
            send = pltpu.make_async_remote_copy(src, outs[i].at[me], send_sems.at[i, d - 1], recv_sems.at[i, d - 1],
                                                device_id=(px, py, pc), device_id_type=MESH)
            recv = pltpu.make_async_remote_copy(src, outs[i].at[peer], send_sems.at[i, d - 1], recv_sems.at[i, d - 1],
                                                device_id=(px, py, pc), device_id_type=MESH) if arrivals else None
            copies.append((send, recv, False))
    return copies


def _exchange_start(copies):
    for send, _, _ in copies:
        send.start()


def _exchange_wait(copies):
    for send, recv, local in copies:
        if local:
            send.wait()
        else:
            send.wait_send()
            recv.wait_recv()


def _gather_two_level(arrays, name):
    n = len(arrays)

    def body(*refs):
        ins, outs = refs[:n], refs[n:2 * n]
        send_sems, recv_sems, local_sems = refs[2 * n:]
        x, y, c = lax.axis_index("x"), lax.axis_index("y"), lax.axis_index("c")
        index = lambda px, py, pc: 4 * px + 2 * py + pc
        sibling = (x, y, 1 - c)
        chips = [(1 - x, y), (x, 1 - y), (1 - x, 1 - y)]

        def copy(i, k, block, to, src=None):
            slot = outs[i].at[index(*block)]
            return pltpu.make_async_remote_copy(slot if src is None else src, slot, send_sems.at[i, k],
                                                recv_sems.at[i, k], device_id=to, device_id_type=MESH)

        local, sends = [], []
        for i in range(n):
            own = pltpu.make_async_copy(ins[i], outs[i].at[index(x, y, c)], local_sems.at[i])
            own.start()
            local.append(own)
            first = [copy(i, 0, (x, y, c), sibling, src=ins[i])]
            first += [copy(i, 1 + j, (x, y, c), (*chip, c), src=ins[i]) for j, chip in enumerate(chips)]
            for cp in first:
                cp.start()
            sends += first
        for i in range(n):
            for j, chip in enumerate(chips):
                copy(i, 1 + j, (*chip, c), (x, y, c)).wait_recv()
                onward = copy(i, 4 + j, (*chip, c), sibling)
                onward.start()
                sends.append(onward)
        for i in range(n):
            copy(i, 0, sibling, (x, y, c)).wait_recv()
            for j, chip in enumerate(chips):
                copy(i, 4 + j, (*chip, 1 - c), (x, y, c)).wait_recv()
        for cp in sends:
            cp.wait_send()
        for cp in local:
            cp.wait()

    hbm = pl.BlockSpec(memory_space=pltpu.HBM)
    return pl.pallas_call(
        body,
        in_specs=[hbm] * n,
        out_specs=[hbm] * n,
        out_shape=_exchange_out_shapes(arrays, False),
        scratch_shapes=_exchange_scratch(n),
        name=name,
    )(*arrays)


def _exchange(arrays, *, scatter, name):
    n = len(arrays)

    def body(*refs):
        copies = _exchange_copies(refs[:n], refs[n:2 * n], *refs[2 * n:], scatter)
        _exchange_start(copies)
        _exchange_wait(copies)

    hbm = pl.BlockSpec(memory_space=pltpu.HBM)
    return pl.pallas_call(
        body,
        in_specs=[hbm] * n,
        out_specs=[hbm] * n,
        out_shape=_exchange_out_shapes(arrays, scatter),
        scratch_shapes=_exchange_scratch(n),
        name=name,
    )(*arrays)


def _adamw(parts, w, m, v, name):
    rows, cols = w.shape
    tr = _pick(rows, (256, 128, 64, 8))
    c1 = 1.0 / (1.0 - ADAM_B1 ** ADAM_STEP)
    c2 = 1.0 / (1.0 - ADAM_B2 ** ADAM_STEP)

    def body(p_ref, w_ref, m_ref, v_ref, g_ref, d_ref, mo_ref, vo_ref):
        g = p_ref[0].astype(F32)
        for s in range(1, N_DEV):
            g = g + p_ref[s].astype(F32)
        mn = ADAM_B1 * m_ref[...] + (1.0 - ADAM_B1) * g
        vn = ADAM_B2 * v_ref[...] + (1.0 - ADAM_B2) * (g * g)
        g_ref[...] = g
        mo_ref[...] = mn
        vo_ref[...] = vn
        d_ref[...] = -ADAM_LR * ((mn * c1) / (jnp.sqrt(vn * c2) + ADAM_EPS) + ADAM_WD * w_ref[...])

    spec = pl.BlockSpec((tr, cols), lambda i: (i, 0))
    return pl.pallas_call(
        body,
        grid=(rows // tr,),
        in_specs=[pl.BlockSpec((N_DEV, tr, cols), lambda i: (0, i, 0)), spec, spec, spec],
        out_specs=[spec] * 4,
        out_shape=[jax.ShapeDtypeStruct((rows, cols), F32)] * 4,
        compiler_params=_cparams(("parallel",), 48),
        name=name,
    )(parts, w, m, v)


_VECTOR_PARAMS = ("attn_norm_g", "attn_sinks", "rwkv_mu", "w0", "a0", "k_k", "k_a", "r_k", "ln_x_w", "ln_x_b",
                  "mlp_norm_g", "final_norm_g")
_WEIGHT_NAMES = ("attn_norm_g", "w_in", "attn_sinks", "rwkv_mu", "w0", "w2", "a0", "a2", "g2", "k_k", "k_a", "r_k",
                 "ln_x_w", "ln_x_b", "w_out", "mlp_norm_g", "w_up", "w_down", "final_norm_g")


def _pack_vectors(vals):
    pieces = []
    for name in _VECTOR_PARAMS:
        flat = vals[name].reshape(1, -1)
        pad = (-flat.shape[1]) % LANES
        pieces.append(jnp.pad(flat, ((0, 0), (0, pad))) if pad else flat)
    return jnp.concatenate(pieces, axis=1)


def _unpack_vectors(packed, like):
    out, col = {}, 0
    for name in _VECTOR_PARAMS:
        size = like[name].size
        out[name] = packed[0, col:col + size].reshape(like[name].shape)
        col += size + (-size) % LANES
    return out


def kernel(x, attn_norm_g, w_in, attn_sinks, rwkv_mu, w0, w2, a0, a2, g2, k_k, k_a, r_k, ln_x_w, ln_x_b, w_out, mlp_norm_g, w_up, w_down, final_norm_g, loss_target, m_attn_norm_g, m_w_in, m_attn_sinks, m_rwkv_mu, m_w0, m_w2, m_a0, m_a2, m_g2, m_k_k, m_k_a, m_r_k, m_ln_x_w, m_ln_x_b, m_w_out, m_mlp_norm_g, m_w_up, m_w_down, m_final_norm_g, v_attn_norm_g, v_w_in, v_attn_sinks, v_rwkv_mu, v_w0, v_w2, v_a0, v_a2, v_g2, v_k_k, v_k_a, v_r_k, v_ln_x_w, v_ln_x_b, v_w_out, v_mlp_norm_g, v_w_up, v_w_down, v_final_norm_g):
    weights = dict(attn_norm_g=attn_norm_g, w_in=w_in, attn_sinks=attn_sinks, rwkv_mu=rwkv_mu, w0=w0, w2=w2, a0=a0,
                   a2=a2, g2=g2, k_k=k_k, k_a=k_a, r_k=r_k, ln_x_w=ln_x_w, ln_x_b=ln_x_b, w_out=w_out,
                   mlp_norm_g=mlp_norm_g, w_up=w_up, w_down=w_down, final_norm_g=final_norm_g)
    mom1 = dict(attn_norm_g=m_attn_norm_g, w_in=m_w_in, attn_sinks=m_attn_sinks, rwkv_mu=m_rwkv_mu, w0=m_w0, w2=m_w2,
                a0=m_a0, a2=m_a2, g2=m_g2, k_k=m_k_k, k_a=m_k_a, r_k=m_r_k, ln_x_w=m_ln_x_w, ln_x_b=m_ln_x_b,
                w_out=m_w_out, mlp_norm_g=m_mlp_norm_g, w_up=m_w_up, w_down=m_w_down, final_norm_g=m_final_norm_g)
    mom2 = dict(attn_norm_g=v_attn_norm_g, w_in=v_w_in, attn_sinks=v_attn_sinks, rwkv_mu=v_rwkv_mu, w0=v_w0, w2=v_w2,
                a0=v_a0, a2=v_a2, g2=v_g2, k_k=v_k_k, k_a=v_k_a, r_k=v_r_k, ln_x_w=v_ln_x_w, ln_x_b=v_ln_x_b,
                w_out=v_w_out, mlp_norm_g=v_mlp_norm_g, w_up=v_w_up, w_down=v_w_down, final_norm_g=v_final_norm_g)
    bsz, seq, d_model = x.shape
    rows = bsz * seq
    d_in = N_DEV * w_in.shape[2]
    d_ff = N_DEV * w_up.shape[2]

    gathered = _gather_two_level([w_in[0].astype(BF16), w2[0], a2[0], g2[0]], name="gather_in_weights")
    cols_first = lambda a: a.transpose(1, 0, 2).reshape(a.shape[1], -1)
    w_in_f = cols_first(gathered[0])
    w_attn, w_rw = w_in_f[:, :QKV_WIDTH], w_in_f[:, QKV_WIDTH:]
    w2_f, a2_f, g2_f = cols_first(gathered[1]), cols_first(gathered[2]), cols_first(gathered[3])
    lora = w2_f.shape[0]
    w2p = jnp.concatenate([w2_f, jnp.zeros_like(a2_f)], axis=0)
    a2p = jnp.concatenate([jnp.zeros_like(w2_f), a2_f], axis=0)

    esum = _head_sum_matrix()
    sink_rows = jnp.broadcast_to(attn_sinks.reshape(N_ATTN_HEADS, 1), (N_ATTN_HEADS, LANES))
    prep_pars = [w0, w2p, a0, a2p, g2_f, k_k, k_a, esum]
    post_pars = [ln_x_w, ln_x_b, r_k, esum]

    x2d = x.reshape(rows, d_model)
    h1, z_attn, z_rw, r, lw, k, v, kk, al, gate = _norm_in_proj_prep(x2d, seq, attn_norm_g, w_attn, w_rw, rwkv_mu,
                                                                       prep_pars)
    z_attn3 = z_attn.reshape(bsz, seq, QKV_WIDTH)
    as3 = lambda a: a.reshape(bsz, seq, RWKV_WIDTH)
    y, from_fwd, (attn_out, attn_probs, attn_psinks), late = _rwkv_attn_fwd(
        as3(r), as3(lw), as3(k), as3(v), as3(kk), as3(al), z_attn3, sink_rows,
        hosted=[w_out[0].astype(BF16), w_up[0].astype(BF16), w_down[0].astype(BF16)])
    w_out_f = late[0].reshape(-1, d_model)
    w_up_f = cols_first(late[1])
    w_down_f = late[2].reshape(-1, d_model)
    y2 = y.reshape(rows, RWKV_WIDTH)
    attn_out2d = attn_out.reshape(rows, ATTN_WIDTH)
    w_out_attn, w_out_rw = w_out_f[:ATTN_WIDTH], w_out_f[ATTN_WIDTH:]
    rw_out, x1, h2 = _post_out_proj_norm(y2, r, k, v, gate, attn_out2d, x2d, post_pars, w_out_attn, w_out_rw,
                                         mlp_norm_g)

    u = _matmul(h2, w_up_f, "nn", name="mlp_up", out_dtypes=(BF16,))
    dx2, dx2_b, loss_vec, g_final = _down_proj_loss(u, w_down_f, x1, loss_target.reshape(rows, d_model),
                                             final_norm_g.reshape(1, d_model))

    g_w_down = _matmul(u, dx2_b, "tn", name="grad_w_down", out_dtypes=(BF16,), a_map=_relu_squared)
    du = _matmul(dx2_b, w_down_f, "nt", name="mlp_down_bwd", extras=(u,), out_dtypes=(BF16,),
                 epilogue=lambda acc, uv: (acc * (2.0 * jnp.maximum(uv.astype(F32), 0.0)),))
    g_w_up = _matmul(h2, du, "tn", name="grad_w_up", out_dtypes=(BF16,))
    dx1, g_mlp_norm = _proj_bwd_norm_bwd([du], [w_up_f], x1, dx2, mlp_norm_g, "mlp_up_bwd_norm_bwd")
    g_w_out = jnp.concatenate([_matmul(attn_out2d, dx1, "tn", name="grad_w_out_attn", out_dtypes=(BF16,)),
                               _matmul(rw_out, dx1, "tn", name="grad_w_out_rwkv", out_dtypes=(BF16,))], axis=0)
    d_attn_out, dy, dr_a, dk_a, dv_a, dgate, g_ln_w, g_ln_b, g_r_k = _out_proj_bwd_post_bwd(
        dx1, y2, r, k, v, gate, post_pars, w_out_attn, w_out_rw)
    by_cols = lambda a: a.reshape(a.shape[0], N_DEV, -1).transpose(1, 0, 2)
    (dr_b, dlw, dk_b, dv_b, dkk, dal), (dz_attn, g_sink_lanes), (p_w_out, p_w_up, p_w_down) = _rwkv_attn_bwd(
        as3(r), as3(lw), as3(k), as3(v), as3(kk), as3(al), from_fwd, as3(dy),
        z_attn3, d_attn_out.reshape(bsz, seq, ATTN_WIDTH), attn_probs, attn_psinks,
        hosted=[g_w_out.reshape(N_DEV, -1, d_model), by_cols(g_w_up), g_w_down.reshape(N_DEV, -1, d_model)],
        scatter=True)
    flat = lambda a: a.reshape(rows, RWKV_WIDTH)
    (dz_rw, gmu_r, gmu_k, gmu_v, gmu_wa, gmu_g, g_w0, g_w2p, g_a0, g_a2p, g_g2, g_k_k, g_k_a) = _prep_bwd(
        z_rw, seq, [dr_a, flat(dr_b), flat(dlw), dk_a, flat(dk_b), dv_a, flat(dv_b), flat(dkk), flat(dal), dgate],
        rwkv_mu, prep_pars)
    dz_attn = dz_attn.reshape(rows, QKV_WIDTH)
    g_w_in = jnp.concatenate([_matmul(h1, dz_attn, "tn", name="grad_w_in_attn", out_dtypes=(BF16,)),
                              _matmul(h1, dz_rw, "tn", name="grad_w_in_rwkv", out_dtypes=(BF16,))], axis=1)
    lora_grads = jnp.concatenate([g_w2p[:lora], g_a2p[lora:], g_g2], axis=0)
    (dx, g_attn_norm), (p_w_in, p_lora) = _proj_bwd_norm_bwd(
        [dz_attn, dz_rw], [w_attn, w_rw], x2d, dx1, attn_norm_g, "in_proj_bwd_norm_bwd",
        hosted=[by_cols(g_w_in), by_cols(lora_grads)], scatter=True)

    vec_grads = dict(attn_norm_g=g_attn_norm, attn_sinks=g_sink_lanes[0, :N_ATTN_HEADS], rwkv_mu=jnp.concatenate(
        [gmu_r, gmu_k, gmu_v, gmu_wa, gmu_g], axis=1), w0=g_w0, a0=g_a0, k_k=g_k_k, k_a=g_k_a, r_k=g_r_k,
        ln_x_w=g_ln_w, ln_x_b=g_ln_b, mlp_norm_g=g_mlp_norm, final_norm_g=g_final)
    packed = _pack_vectors(vec_grads)
    nvec = packed.shape[1]
    everyone = _exchange([jnp.concatenate([packed, loss_vec], axis=1)], scatter=False, name="gather_vector_grads")[0]
    vec_parts = everyone[:, :, :nvec]
    loss = jnp.sum(everyone[:, 0, nvec])

    grads, delta, new_m, new_v = {}, {}, {}, {}

    def update(name, part, shape2d):
        res = _adamw(part, weights[name].reshape(shape2d), mom1[name].reshape(shape2d), mom2[name].reshape(shape2d),
                     "adamw_" + name)
        for store, val in zip((grads, delta, new_m, new_v), res):
            store[name] = val.reshape(weights[name].shape)

    update("w_in", p_w_in, w_in.shape[1:])
    update("w_out", p_w_out, w_out.shape[1:])
    update("w_up", p_w_up, w_up.shape[1:])
    update("w_down", p_w_down, w_down.shape[1:])
    stack = lambda d: jnp.concatenate([d["w2"][0], d["a2"][0], d["g2"][0]], axis=0)
    lora_res = _adamw(p_lora, stack(weights), stack(mom1), stack(mom2), "adamw_lora")
    for store, val in zip((grads, delta, new_m, new_v), lora_res):
        store["w2"], store["a2"], store["g2"] = val[None, :lora], val[None, lora:2 * lora], val[None, 2 * lora:]
    vec_res = _adamw(vec_parts, _pack_vectors(weights), _pack_vectors(mom1), _pack_vectors(mom2), "adamw_vectors")
    for store, val in zip((grads, delta, new_m, new_v), vec_res):
        store.update(_unpack_vectors(val, weights))

    return (loss, dx.reshape(x.shape), *[grads[n] for n in _WEIGHT_NAMES], *[delta[n] for n in _WEIGHT_NAMES],
            *[new_m[n] for n in _WEIGHT_NAMES], *[new_v[n] for n in _WEIGHT_NAMES])
```

```python
import functools
import math

import jax
import jax.numpy as jnp
from jax import lax
from jax.experimental import pallas as pl
from jax.experimental.pallas import tpu as pltpu

F32 = jnp.float32
BF16 = jnp.bfloat16

N_DEV = 8
HEAD_DIM = 64
LANES = 128
N_ATTN_HEADS = 8
ATTN_WIDTH = 512
KV_WIDTH = 128
QKV_WIDTH = ATTN_WIDTH + 2 * KV_WIDTH
RWKV_WIDTH = 512
LORA_WA = 128
GATE_LORA = 128
RWKV_SHIFT_WIDTH = 3 * RWKV_WIDTH + LORA_WA + GATE_LORA
BLOCK = 128
CHUNK = 64
RMS_EPS = 1e-6
GN_EPS = 64e-5
L2_EPS = 1e-12
NEG_INF = -1e30
DECAY_SCALE = math.exp(-0.5)
ADAM_LR, ADAM_B1, ADAM_B2, ADAM_EPS, ADAM_WD, ADAM_STEP = 0.001, 0.9, 0.999, 1e-08, 0.01, 10

NN = (((1,), (0,)), ((), ()))
NT = (((1,), (1,)), ((), ()))
TN = (((0,), (0,)), ((), ()))
MESH = pl.DeviceIdType.MESH


def _dot(a, b, dn=NN, precision=None):
    return lax.dot_general(a, b, dn, precision=precision, preferred_element_type=F32)


def _bdot_raw(a, b, dn):
    return lax.dot_general(a.astype(BF16), b.astype(BF16), dn, preferred_element_type=F32)


@functools.partial(jax.custom_vjp, nondiff_argnums=(2, 3))
def _bdot_c(a, b, ca, cb):
    return _bdot_raw(a, b, (((ca,), (cb,)), ((), ())))


def _bdot_c_fwd(a, b, ca, cb):
    return _bdot_c(a, b, ca, cb), (a, b)


_BESIDE_BACKWARD = [None]


def _beside_backward():
    if _BESIDE_BACKWARD[0] is not None:
        _BESIDE_BACKWARD[0]()


def _bdot_c_bwd(ca, cb, res, ct):
    _beside_backward()
    a, b = res
    fa, fb = 1 - ca, 1 - cb
    da = _bdot_raw(ct, b, (((1,), (fb,)), ((), ()))) if ca == 1 else _bdot_raw(b, ct, (((fb,), (1,)), ((), ())))
    db = _bdot_raw(a, ct, (((fa,), (0,)), ((), ()))) if cb == 0 else _bdot_raw(ct, a, (((0,), (fa,)), ((), ())))
    return da, db


_bdot_c.defvjp(_bdot_c_fwd, _bdot_c_bwd)


def _bdot(a, b, dn=NN):
    return _bdot_c(a, b, dn[0][0][0], dn[0][1][0])


def _bdot_nn(a, b):
    return _bdot(a, b, NN)


def _split3(x):
    hi = x.astype(BF16)
    rest = x - hi.astype(F32)
    mid = rest.astype(BF16)
    return hi, mid, (rest - mid.astype(F32)).astype(BF16)


def _running_sum(x, dn):
    c = x.shape[0]
    row = lax.broadcasted_iota(jnp.int32, (c, c), 0)
    col = lax.broadcasted_iota(jnp.int32, (c, c), 1)
    tri = jnp.where(row >= col, 1.0, 0.0).astype(BF16)
    w = x.shape[1]
    parts = lax.dot_general(tri, jnp.concatenate(_split3(x), axis=1), dn, preferred_element_type=F32)
    return parts[:, :w] + parts[:, w:2 * w] + parts[:, 2 * w:]


@jax.custom_vjp
def _cumsum_rows(x):
    return _running_sum(x, NN)


_cumsum_rows.defvjp(lambda x: (_running_sum(x, NN), None), lambda _, ct: (_running_sum(ct, TN),))


@jax.custom_vjp
def _fold_rows(x):
    c = x.shape[0] // 2
    return x[:c] + x[c:]


_fold_rows.defvjp(lambda x: (_fold_rows(x), None), lambda _, ct: (jnp.concatenate([ct, ct], axis=0),))


@jax.custom_vjp
def _halves(x):
    n = x.shape[0] // 2
    return x[:n], x[n:]


_halves.defvjp(lambda x: (_halves(x), None), lambda _, cts: (jnp.concatenate(cts, axis=0),))


@jax.custom_vjp
def _quarters(x):
    n = x.shape[0] // 2
    return x[:n, :n], x[:n, n:], x[n:, :n], x[n:, n:]


_quarters.defvjp(lambda x: (_quarters(x), None),
                 lambda _, cts: (jnp.concatenate([jnp.concatenate(cts[:2], axis=1),
                                                  jnp.concatenate(cts[2:], axis=1)], axis=0),))


@jax.custom_vjp
def _sigmoid(x):
    return 1.0 / (1.0 + jnp.exp(-x))


def _sigmoid_fwd(x):
    s = _sigmoid(x)
    return s, s


_sigmoid.defvjp(_sigmoid_fwd, lambda s, ct: (ct * s * (1.0 - s),))


def _pick(n, cands):
    for c in cands:
        if n % c == 0:
            return c
    return n


def _cparams(sem, vmem_mb=None):
    kw = dict(dimension_semantics=sem)
    if vmem_mb is not None:
        kw["vmem_limit_bytes"] = vmem_mb * 1024 * 1024
    return pltpu.CompilerParams(**kw)


def _matmul(a, b, mode, *, name, extras=(), epilogue=None, out_dtypes=(F32,), tm=1024, tn=1024, tk=1024,
            hosted=(), scatter=False, a_map=lambda tile: tile):
    if mode == "nn":
        (M, K), (_, N) = a.shape, b.shape
    elif mode == "tn":
        (K, M), (_, N) = a.shape, b.shape
    else:
        (M, K), (N, _) = a.shape, b.shape
    tm = _pick(M, (tm, 512, 256, 128))
    tn = _pick(N, (tn, 896, 768, 512, 384, 256, 128))
    tk = _pick(K, (tk, 512, 256, 128))
    nk = K // tk
    ne, nout = len(extras), len(out_dtypes)
    if mode == "nn":
        a_spec = pl.BlockSpec((tm, tk), lambda i, j, k: (i, k))
        b_spec = pl.BlockSpec((tk, tn), lambda i, j, k: (k, j))
        dn = NN
    elif mode == "tn":
        a_spec = pl.BlockSpec((tk, tm), lambda i, j, k: (k, i))
        b_spec = pl.BlockSpec((tk, tn), lambda i, j, k: (k, j))
        dn = TN
    else:
        a_spec = pl.BlockSpec((tm, tk), lambda i, j, k: (i, k))
        b_spec = pl.BlockSpec((tn, tk), lambda i, j, k: (j, k))
        dn = NT
    o_spec = pl.BlockSpec((tm, tn), lambda i, j, k: (i, j))
    grid = (M // tm, N // tn, nk)
    nex = len(hosted)

    def body(*refs):
        a_ref, b_ref = refs[:2]
        e_refs = refs[2:2 + ne]
        ex_in = refs[2 + ne:2 + ne + nex]
        o_refs = refs[2 + ne + nex:2 + ne + nex + nout]
        ex_out = refs[2 + ne + nex + nout:2 + ne + 2 * nex + nout]
        scratch = refs[2 + ne + 2 * nex + nout:]
        kstep = pl.program_id(2)
        if nex:
            at = [pl.program_id(d) for d in range(3)]
            first = jnp.logical_and(jnp.logical_and(at[0] == 0, at[1] == 0), at[2] == 0)
            last = jnp.logical_and(jnp.logical_and(at[0] == grid[0] - 1, at[1] == grid[1] - 1), at[2] == grid[2] - 1)
            _hosted_exchange(first, last, ex_in, ex_out, scratch[-3:], scatter)

        def finish(total):
            outs = (total,) if epilogue is None else epilogue(total, *[e[...] for e in e_refs])
            for o_ref, o in zip(o_refs, outs):
                o_ref[...] = o.astype(o_ref.dtype)

        if nk == 1:
            finish(_bdot_raw(a_map(a_ref[...]), b_ref[...], dn))
            return
        acc = scratch[0]

        @pl.when(kstep == 0)
        def _():
            acc[...] = jnp.zeros_like(acc)

        acc[...] += _bdot_raw(a_map(a_ref[...]), b_ref[...], dn)

        @pl.when(kstep == nk - 1)
        def _():
            finish(acc[...])

    hbm = pl.BlockSpec(memory_space=pltpu.HBM)
    outs = pl.pallas_call(
        body,
        grid=grid,
        in_specs=[a_spec, b_spec] + [o_spec] * ne + [hbm] * nex,
        out_specs=[o_spec] * nout + [hbm] * nex,
        out_shape=[jax.ShapeDtypeStruct((M, N), dt) for dt in out_dtypes] + _exchange_out_shapes(hosted, scatter),
        scratch_shapes=([pltpu.VMEM((tm, tn), F32)] if nk > 1 else []) + (_exchange_scratch(nex) if nex else []),
        compiler_params=_cparams(("arbitrary",) * 3 if nex else ("parallel", "parallel", "arbitrary"), 56),
        name=name,
    )(a, b, *extras, *hosted)
    if nex:
        return outs[:nout], outs[nout:]
    return outs[0] if nout == 1 else outs


def _rowwise(fn, rows, pars, out_rows, out_accs, *, tile, name, nsub=1, hosted=(), scatter=False):
    rows = [r if isinstance(r, tuple) else (r, r.shape[1], 0) for r in rows]
    R = rows[0][0].shape[0]
    tile = min(tile, R)
    chunk = tile // nsub
    ntile = R // tile
    nr, npar, nor, noa, nex = len(rows), len(pars), len(out_rows), len(out_accs), len(hosted)

    def body(*refs):
        rin = refs[:nr]
        pin = refs[nr:nr + npar]
        ex_in = refs[nr + npar:nr + npar + nex]
        orow = refs[nr + npar + nex:nr + npar + nex + nor]
        oacc = refs[nr + npar + nex + nor:nr + npar + nex + nor + noa]
        ex_out = refs[nr + npar + nex + nor + noa:nr + npar + 2 * nex + nor + noa]
        step = pl.program_id(0)
        _hosted_exchange(step == 0, step == ntile - 1, ex_in, ex_out, refs[nr + npar + 2 * nex + nor + noa:], scatter)
        pvals = [p[...] for p in pin]
        totals = []
        for sub in range(nsub):
            at = slice(sub * chunk, (sub + 1) * chunk)
            outs = fn(*[r[at, :] for r in rin], *pvals)
            for ref, o in zip(orow, outs[:nor]):
                if isinstance(o, (tuple, list)):
                    col = 0
                    for piece in o:
                        ref[at, col:col + piece.shape[1]] = piece.astype(ref.dtype)
                        col += piece.shape[1]
                else:
                    ref[at, :] = o.astype(ref.dtype)
            accs = list(outs[nor:])
            totals = accs if sub == 0 else [t + a for t, a in zip(totals, accs)]

        def accumulate(ref, o):
            @pl.when(step == 0)
            def _():
                ref[...] = o

            @pl.when(step > 0)
            def _():
                ref[...] += o

        for ref, o in zip(oacc, totals):
            accumulate(ref, o)

    def colspec(width, cb):
        return pl.BlockSpec((tile, width), lambda i: (i, cb))

    hbm = pl.BlockSpec(memory_space=pltpu.HBM)
    outs = pl.pallas_call(
        body,
        grid=(ntile,),
        in_specs=[colspec(w, cb) for (_, w, cb) in rows]
        + [pl.BlockSpec(p.shape, lambda i: (0, 0), pipeline_mode=pl.Buffered(1)) for p in pars] + [hbm] * nex,
        out_specs=[colspec(w, 0) for (w, _) in out_rows]
        + [pl.BlockSpec(s, lambda i: (0, 0)) for s in out_accs] + [hbm] * nex,
        out_shape=[jax.ShapeDtypeStruct((R, w), dt) for (w, dt) in out_rows]
        + [jax.ShapeDtypeStruct(s, F32) for s in out_accs] + _exchange_out_shapes(hosted, scatter),
        scratch_shapes=_exchange_scratch(nex) if nex else [],
        compiler_params=_cparams(("arbitrary",), 56),
        name=name,
    )(*[r[0] for r in rows], *pars, *hosted)
    return (outs[:nor + noa], outs[nor + noa:]) if nex else outs


def _rms_fn(x, g):
    return x * lax.rsqrt(jnp.mean(x * x, axis=-1, keepdims=True) + RMS_EPS) * g


FUSED_TILE = 512
FUSED_CHUNKS = 2


def _relu_squared(u):
    pos = jnp.maximum(u.astype(F32), 0.0)
    return pos * pos


def _down_proj_loss(u, w_down, x1, tgt, g):
    d = x1.shape[1]

    def fn(uv, xv, tv, wv, gv):
        x2 = xv + _bdot_raw(_relu_squared(uv), wv, NN)
        y, vjp = jax.vjp(_rms_fn, x2, gv)
        err = y - tv
        loss = 0.5 * jnp.sum(jnp.sum(err * err, axis=-1, keepdims=True), axis=0, keepdims=True) / d
        dx, dg = vjp(err / d)
        return dx, dx, jnp.broadcast_to(loss, (1, LANES)), dg

    return _rowwise(fn, [u, x1, tgt], [w_down, g], [(d, F32), (d, BF16)], [(1, LANES), g.shape],
                    tile=FUSED_TILE, nsub=FUSED_CHUNKS, name="mlp_down_final_norm_loss")


def _proj_bwd_norm_bwd(cts, weights_t, x, dres, g, name, hosted=(), scatter=False):
    n = len(cts)

    def fn(*vals):
        ctv, (xv, dresv), wv, gv = vals[:n], vals[n:n + 2], vals[n + 2:2 * n + 2], vals[-1]
        dh = _bdot_raw(ctv[0], wv[0], NT)
        for c, w in zip(ctv[1:], wv[1:]):
            dh = dh + _bdot_raw(c, w, NT)
        _, vjp = jax.vjp(_rms_fn, xv, gv)
        dx, dg = vjp(dh)
        return dx + dresv, dg

    return _rowwise(fn, [*cts, x, dres], [*weights_t, g], [(x.shape[1], F32)], [g.shape],
                    tile=FUSED_TILE, nsub=FUSED_CHUNKS, name=name, hosted=hosted, scatter=scatter)


def _head_sum_matrix():
    i = lax.broadcasted_iota(jnp.int32, (RWKV_WIDTH, RWKV_WIDTH), 0) // HEAD_DIM
    j = lax.broadcasted_iota(jnp.int32, (RWKV_WIDTH, RWKV_WIDTH), 1) // HEAD_DIM
    return (i == j).astype(BF16)


def _head_sums_raw(x, esum):
    hi = x.astype(BF16)
    lo = (x - hi.astype(F32)).astype(BF16)
    return _dot(hi, esum) + _dot(lo, esum)


@jax.custom_vjp
def _head_sums(x, esum):
    return _head_sums_raw(x, esum)


_head_sums.defvjp(lambda x, esum: (_head_sums_raw(x, esum), esum),
                  lambda esum, ct: (_head_sums_raw(ct, esum), jnp.zeros_like(esum)))


def _prep_core(xr, xk, xv, xwa, xg, w0, w2p, a0, a2p, g2, k_k, k_a, esum):
    lw = -DECAY_SCALE * _sigmoid(w0 + _bdot_nn(jnp.tanh(xwa), w2p))
    a = _sigmoid(a0 + _bdot_nn(xwa, a2p))
    g = _bdot_nn(_sigmoid(xg), g2)
    kk0 = xk * k_k
    kk = kk0 * jnp.minimum(lax.rsqrt(_head_sums(kk0 * kk0, esum)), 1.0 / L2_EPS)
    k = xk * (1.0 + (a - 1.0) * k_a)
    return xr, lw, k, xv, kk, a, g


_SEGS = ((0, 512), (512, 1024), (1024, 1536), (1536, 1664), (1664, 1792))


PREP_TILE = 256
SUBLANES = 8


def _shifted_tokens(z_ref, zprev_ref, tile_index, seq):
    zc = z_ref[...]
    start = (tile_index * PREP_TILE) % seq == 0
    before = jnp.where(start, 0.0, zprev_ref[SUBLANES - 1:SUBLANES, :])
    rowid = lax.broadcasted_iota(jnp.int32, zc.shape, 0)
    return zc, jnp.where(rowid == 0, before, pltpu.roll(zc, 1, 0))


def _prep_specs(z, mu, pars, index):
    width = z.shape[1]
    per = PREP_TILE // SUBLANES
    return ([pl.BlockSpec((PREP_TILE, width), lambda i: (index(i), 0)),
             pl.BlockSpec((SUBLANES, width), lambda i: (jnp.maximum(index(i) * per - 1, 0), 0))],
            [pl.BlockSpec(p.shape, lambda i: (0, 0)) for p in (mu, *pars)])


def _norm_in_proj_prep(x, seq, g, w_attn, w_rw, mu, pars):
    rows, d = x.shape
    chunk = FUSED_TILE // FUSED_CHUNKS
    npar = len(pars)
    wa_width, wr_width = w_attn.shape[1], w_rw.shape[1]

    def body(x_ref, g_ref, wa_ref, wr_ref, mu_ref, *rest):
        par_refs = rest[:npar]
        h_ref, za_ref, zr_ref = rest[npar:npar + 3]
        out_refs, carry = rest[npar + 3:-1], rest[-1]
        step = pl.program_id(0)

        @pl.when(step == 0)
        def _():
            carry[...] = jnp.zeros_like(carry)

        pv = [p[...] for p in par_refs]
        for sub in range(FUSED_CHUNKS):
            at = slice(sub * chunk, (sub + 1) * chunk)
            h = _rms_fn(x_ref[at, :], g_ref[...])
            h_ref[at, :] = h.astype(h_ref.dtype)
            za_ref[at, :] = _bdot_raw(h, wa_ref[...], NN).astype(za_ref.dtype)
            zc = _bdot_raw(h, wr_ref[...], NN)
            zr_ref[at, :] = zc
            start = (step * FUSED_TILE + sub * chunk) % seq == 0
            before = jnp.where(start, 0.0, carry[SUBLANES - 1:SUBLANES, :])
            rowid = lax.broadcasted_iota(jnp.int32, zc.shape, 0)
            zp = jnp.where(rowid == 0, before, pltpu.roll(zc, 1, 0))
            carry[...] = zc[chunk - SUBLANES:chunk, :]
            zs = zc + (zp - zc) * mu_ref[...]
            outs = _prep_core(*[zs[:, a:b] for a, b in _SEGS], *pv)
            for ref, o in zip(out_refs, outs):
                ref[at, :] = o

    tiled = lambda width: pl.BlockSpec((FUSED_TILE, width), lambda i: (i, 0))
    resident = lambda a: pl.BlockSpec(a.shape, lambda i: (0, 0), pipeline_mode=pl.Buffered(1))
    return pl.pallas_call(
        body,
        grid=(rows // FUSED_TILE,),
        in_specs=[tiled(d)] + [resident(a) for a in (g, w_attn, w_rw, mu, *pars)],
        out_specs=[tiled(d), tiled(wa_width), tiled(wr_width)] + [tiled(RWKV_WIDTH)] * 7,
        out_shape=[jax.ShapeDtypeStruct((rows, d), BF16), jax.ShapeDtypeStruct((rows, wa_width), BF16),
                   jax.ShapeDtypeStruct((rows, wr_width), F32)] + [jax.ShapeDtypeStruct((rows, RWKV_WIDTH), F32)] * 7,
        scratch_shapes=[pltpu.VMEM((SUBLANES, wr_width), F32)],
        compiler_params=_cparams(("arbitrary",), 56),
        name="attn_norm_in_proj_rwkv_prep",
    )(x, g, w_attn, w_rw, mu, *pars)


def _prep_bwd(z, seq, cts, mu, pars):
    rows, width = z.shape
    ntile = rows // PREP_TILE
    npar, nct = len(pars), len(cts)
    acc_shapes = [(1, b - a) for a, b in _SEGS] + [p.shape for p in pars[:-1]]

    def body(z_ref, zprev_ref, *rest):
        ct_refs = rest[:nct]
        mu_ref = rest[nct]
        par_refs = rest[nct + 1:nct + 1 + npar]
        dz_ref = rest[nct + 1 + npar]
        acc_refs = rest[nct + 2 + npar:-1]
        carry = rest[-1]
        step = pl.program_id(0)
        tile_index = ntile - 1 - step

        @pl.when(step == 0)
        def _():
            carry[...] = jnp.zeros_like(carry)

        zc, zp = _shifted_tokens(z_ref, zprev_ref, tile_index, seq)
        mu_v = mu_ref[...]
        diff = zp - zc
        zs = zc + diff * mu_v
        dra, drb, dlw, dka, dkb, dva, dvb, dkk, da, dg = [c[...] for c in ct_refs]
        pv = [p[...] for p in par_refs]
        _, vjp = jax.vjp(lambda *args: _prep_core(*args, pv[-1]), *[zs[:, a:b] for a, b in _SEGS], *pv[:-1])
        grads = vjp((dra + drb, dlw, dka + dkb, dva + dvb, dkk, da, dg))
        dsegs, dpars = grads[:5], grads[5:]
        last_of_sequence = ((tile_index + 1) * PREP_TILE) % seq == 0
        accs = []
        for ds, (a, b) in zip(dsegs, _SEGS):
            mu_s = mu_v[:, a:b]
            dzp = ds * mu_s
            after = jnp.where(last_of_sequence, 0.0, carry[0:1, a:b])
            rowid = lax.broadcasted_iota(jnp.int32, dzp.shape, 0)
            from_next = jnp.where(rowid == PREP_TILE - 1, after, pltpu.roll(dzp, PREP_TILE - 1, 0))
            dz_ref[:, a:b] = (ds * (1.0 - mu_s) + from_next).astype(dz_ref.dtype)
            carry[:, a:b] = dzp[0:SUBLANES, :]
            accs.append(jnp.sum(ds * diff[:, a:b], axis=0, keepdims=True))
        accs.extend(dpars)

        def accumulate(ref, o):
            @pl.when(step == 0)
            def _():
                ref[...] = o

            @pl.when(step > 0)
            def _():
                ref[...] += o

        for ref, o in zip(acc_refs, accs):
            accumulate(ref, o)

    rev = lambda i: ntile - 1 - i
    zspecs, pspecs = _prep_specs(z, mu, pars, rev)
    return pl.pallas_call(
        body,
        grid=(ntile,),
        in_specs=zspecs + [pl.BlockSpec((PREP_TILE, RWKV_WIDTH), lambda i: (rev(i), 0))] * nct + pspecs,
        out_specs=[pl.BlockSpec((PREP_TILE, width), lambda i: (rev(i), 0))]
        + [pl.BlockSpec(s, lambda i: (0, 0)) for s in acc_shapes],
        out_shape=[jax.ShapeDtypeStruct((rows, width), BF16)] + [jax.ShapeDtypeStruct(s, F32) for s in acc_shapes],
        scratch_shapes=[pltpu.VMEM((SUBLANES, width), F32)],
        compiler_params=_cparams(("arbitrary",), 56),
        name="rwkv_prep_bwd",
    )(z, z, *cts, mu, *pars)


def _post_fn(y, r, k, v, g, ln_w, ln_b, r_k, esum):
    mean = _head_sums(y, esum) * (1.0 / HEAD_DIM)
    yc = y - mean
    var = _head_sums(yc * yc, esum) * (1.0 / HEAD_DIM)
    yn = yc * lax.rsqrt(var + GN_EPS) * ln_w + ln_b
    bonus = _head_sums(r * k * r_k, esum) * v
    return (yn + bonus) * g


def _post_out_proj_norm(y, r, k, v, g, attn_out, x, pars, w_attn_rows, w_rwkv_rows, g_norm):
    npar = len(pars)

    def fn(yv, rv, kv, vv, gv, av, xv, *rest):
        wa, wr, gn = rest[npar:]
        rw = _post_fn(yv, rv, kv, vv, gv, *rest[:npar])
        x1 = xv + _bdot_raw(av, wa, NN) + _bdot_raw(rw, wr, NN)
        return rw, x1, _rms_fn(x1, gn)

    d = x.shape[1]
    return _rowwise(fn, [y, r, k, v, g, attn_out, x], [*pars, w_attn_rows, w_rwkv_rows, g_norm],
                    [(RWKV_WIDTH, BF16), (d, F32), (d, BF16)], [],
                    tile=FUSED_TILE, nsub=FUSED_CHUNKS, name="rwkv_post_out_proj_mlp_norm")


def _out_proj_bwd_post_bwd(dx1, y, r, k, v, g, pars, w_attn_rows, w_rwkv_rows):
    npar = len(pars)

    def fn(dxv, yv, rv, kv, vv, gv, *rest):
        wa, wr = rest[npar:]
        esum = rest[npar - 1]
        d_attn = _bdot_raw(dxv, wa, NT)
        d_rw = _bdot_raw(dxv, wr, NT)
        _, vjp = jax.vjp(lambda *a: _post_fn(*a, esum), yv, rv, kv, vv, gv, *rest[:npar - 1])
        return (d_attn, *vjp(d_rw))

    return _rowwise(fn, [dx1, y, r, k, v, g], [*pars, w_attn_rows, w_rwkv_rows],
                    [(RWKV_WIDTH, BF16)] * 2 + [(RWKV_WIDTH, F32)] * 4,
                    [p.shape for p in pars[:-1]], tile=FUSED_TILE, nsub=FUSED_CHUNKS, name="out_proj_bwd_rwkv_post_bwd")


def _tri_inverses(ms, tick=lambda: None):
    n = ms[0].shape[0]
    row = lax.broadcasted_iota(jnp.int32, (n, n), 0)
    col = lax.broadcasted_iota(jnp.int32, (n, n), 1)
    eye = jnp.where(row == col, 1.0, 0.0)
    t_inv = [eye + m for m in ms]
    power = [_bdot_raw(m, m, NN) for m in ms]
    steps = int(math.log2(n // 2)) - 1
    for step in range(steps):
        if step < steps - 1:
            both = [_bdot_raw(jnp.concatenate([p, t], axis=0), p, NN) for p, t in zip(power, t_inv)]
            power = [b[:n] for b in both]
            t_inv = [t + b[n:] for t, b in zip(t_inv, both)]
        else:
            t_inv = [t + _bdot_raw(t, p, NN) for t, p in zip(t_inv, power)]
        tick()
    return t_inv


def _tri_solve_bwd(res, dus):
    t_inv, us = res
    dxs = tuple(_bdot_raw(t, du, TN) for t, du in zip(t_inv, dus))
    dms = tuple(_bdot_raw(dx, u, NT) for dx, u in zip(dxs, us))
    return dms, dxs


@functools.partial(jax.custom_vjp, nondiff_argnums=(3, 4))
def _kept_bdot_c(a, b, kept, ca, cb):
    return kept.astype(F32)


def _kept_bdot_c_fwd(a, b, kept, ca, cb):
    return kept.astype(F32), (a, b, kept)


def _kept_bdot_c_bwd(ca, cb, res, ct):
    a, b, kept = res
    return (*_bdot_c_bwd(ca, cb, (a, b), ct), jnp.zeros_like(kept))


_kept_bdot_c.defvjp(_kept_bdot_c_fwd, _kept_bdot_c_bwd)


def _kept_bdot(a, b, kept, dn):
    return _kept_bdot_c(a, b, kept, dn[0][0][0], dn[0][1][0])


@jax.custom_vjp
def _kept_tri_solve(ms, xs, t_inv, us):
    return tuple(u.astype(F32) for u in us)


def _kept_tri_solve_fwd(ms, xs, t_inv, us):
    return tuple(u.astype(F32) for u in us), (t_inv, us)


def _kept_tri_solve_bwd(res, dus):
    t_inv, us = res
    dms, dxs = _tri_solve_bwd(res, dus)
    return dms, dxs, tuple(jnp.zeros_like(t) for t in t_inv), tuple(jnp.zeros_like(u) for u in us)


_kept_tri_solve.defvjp(_kept_tri_solve_fwd, _kept_tri_solve_bwd)


def _chunk_fn(ss, rs, lws, ks, vs, kks, als, kept=None, tick=lambda: None):
    c = rs[0].shape[0]
    n = 2 * c
    row = lax.broadcasted_iota(jnp.int32, (n, n), 0)
    col = lax.broadcasted_iota(jnp.int32, (n, n), 1)
    incl = (row % c) >= (col % c)
    strict = (row % c) > (col % c)
    lane = lax.broadcasted_iota(jnp.int32, (1, LANES), 1)
    m_lo = jnp.where(lane < HEAD_DIM, 1.0, 0.0)
    m_hi = 1.0 - m_lo

    def stack(a):
        return jnp.concatenate([a * m_lo, a * m_hi], axis=0)

    cums = [_cumsum_rows(lw) for lw in lws]
    totals = [jnp.sum(lw, axis=0, keepdims=True) for lw in lws]
    bs = [kk * al for kk, al in zip(kks, als)]
    grows = [jnp.exp(-cum) for cum in cums]
    a_s = [stack(-kk * jnp.exp(cum - lw)) for kk, cum, lw in zip(kks, cums, lws)]
    b_s = [stack(b * g) for b, g in zip(bs, grows)]
    k_s = [stack(k * g) for k, g in zip(ks, grows)]
    r_s = [stack(r * jnp.exp(cum)) for r, cum in zip(rs, cums)]
    v_s = [stack(v) for v in vs]
    tick()
    pair = lambda p, q: jnp.concatenate([p, q], axis=0)
    ar_s = [pair(a, r) for a, r in zip(a_s, r_s)]
    if kept is None:
        products = [_bdot_raw(ar, pair(b, k), NT) for ar, b, k in zip(ar_s, b_s, k_s)]
    else:
        products = [_kept_bdot(ar, pair(b, k), kp, NT) for ar, b, k, kp in zip(ar_s, b_s, k_s, kept[0])]
    tick()
    blocks = [_quarters(p) for p in products]
    m_ab = [jnp.where(strict, q[0], 0.0) for q in blocks]
    m_ak = [jnp.where(strict, q[1], 0.0) for q in blocks]
    m_rb = [jnp.where(incl, q[2], 0.0) for q in blocks]
    m_rk = [jnp.where(incl, q[3], 0.0) for q in blocks]
    from_state = [_halves(_bdot(ar, s, NT)) for ar, s in zip(ar_s, ss)]
    tick()
    from_v = [_halves(_bdot(pair(mk, mr), v)) for mk, mr, v in zip(m_ak, m_rk, v_s)]
    tick()
    x = tuple(fs[0] + fv[0] for fs, fv in zip(from_state, from_v))
    if kept is None:
        t_inv = _tri_inverses(m_ab, tick)
        u = [_bdot_raw(t, xx, NN) for t, xx in zip(t_inv, x)]
        tick()
    else:
        u = _kept_tri_solve(tuple(m_ab), x, kept[1], kept[2])
    y = [_fold_rows(fs[1] + _bdot(mb, uu) + fv[1]) for fs, mb, uu, fv in zip(from_state, m_rb, u, from_v)]
    tick()
    tails = [jnp.exp(tot - cum) for tot, cum in zip(totals, cums)]
    s_new = [s * jnp.exp(tot) + _bdot(pair(uu, v), pair(stack(b * tl), stack(k * tl)), TN)
             for s, tot, uu, b, tl, v, k in zip(ss, totals, u, bs, tails, v_s, ks)]
    if kept is None:
        keep = lambda vals: tuple(v.astype(BF16) for v in vals)
        return tuple(y), tuple(s_new), (keep(products), keep(t_inv), keep(u))
    return tuple(y), tuple(s_new)


def _chains(bsz, npair):
    return [(b, p, slice(p * LANES, (p + 1) * LANES)) for b in range(bsz) for p in range(npair)]


def _hosted_exchange(first, last, ex_in, ex_out, sems, scatter):
    if not ex_in:
        return

    @pl.when(first)
    def _():
        _exchange_start(_exchange_copies(ex_in, ex_out, *sems, scatter, arrivals=False))

    @pl.when(last)
    def _():
        _exchange_wait(_exchange_copies(ex_in, ex_out, *sems, scatter, arrivals=True))


def _rwkv_attn_fwd(r, lw, k, v, kk, al, z, sink_rows, hosted=(), scatter=False):
    bsz, t, w = r.shape
    npair, nchunk = w // LANES, t // CHUNK
    nb = t // BLOCK
    assert bsz * nb == nchunk
    chains = _chains(bsz, npair)
    nex = len(hosted)
    apair = ATTN_WIDTH // LANES

    def body(*refs):
        r_ref, lw_ref, k_ref, v_ref, kk_ref, al_ref = refs[:6]
        q_ref, kp_ref, kc_ref, vp_ref, vc_ref, sink_ref = refs[6:12]
        ex_in = refs[12:12 + nex]
        y_ref, sall_ref, prod_ref, tinv_ref, u_ref, o_ref, p_ref, ps_ref = refs[12 + nex:20 + nex]
        ex_out = refs[20 + nex:20 + 2 * nex]
        s_scr = refs[20 + 2 * nex]
        step = pl.program_id(0)

        @pl.when(step == 0)
        def _():
            s_scr[...] = jnp.zeros_like(s_scr)

        _hosted_exchange(step == 0, step == nchunk - 1, ex_in, ex_out, refs[21 + 2 * nex:], scatter)
        qs = tuple(q_ref[0, :, pair * LANES:(pair + 1) * LANES] for pair in range(apair))
        attn = []
        stages = _attn_block_stages(qs, kp_ref[0], kc_ref[0], vp_ref[0], vc_ref[0], _sink_values(sink_ref),
                                    step % nb == 0, attn)
        ss = tuple(s_scr[i] for i in range(len(chains)))
        for i, s in enumerate(ss):
            sall_ref[0, i] = s
        ys, s_new, kept = _chunk_fn(ss, *[tuple(ref[b, :, cols] for b, _, cols in chains)
                                          for ref in (r_ref, lw_ref, k_ref, v_ref, kk_ref, al_ref)],
                                    tick=lambda: next(stages, None))
        for _ in stages:
            pass
        for i, (b, _, cols) in enumerate(chains):
            y_ref[b, :, cols] = ys[i]
            s_scr[i] = s_new[i]
            prod_ref[0, i], tinv_ref[0, i], u_ref[0, i] = kept[0][i], kept[1][i], kept[2][i]
        outs, probs, psinks = attn
        for pair in range(apair):
            o_ref[0, :, pair * LANES:(pair + 1) * LANES] = outs[pair].astype(o_ref.dtype)
        for h, p in enumerate(probs):
            p_ref[0, 0, h] = p
        ps_ref[0] = _head_columns(psinks)

    spec = pl.BlockSpec((bsz, CHUNK, w), lambda c: (0, c, 0))
    hbm = pl.BlockSpec(memory_space=pltpu.HBM)
    per_chunk = lambda n: pl.BlockSpec((1, len(chains), n, n), lambda c: (c, 0, 0, 0))
    kept_shape = lambda n: jax.ShapeDtypeStruct((nchunk, len(chains), n, n), BF16)
    kcol, vcol = ATTN_WIDTH // KV_WIDTH, ATTN_WIDTH // KV_WIDTH + 1
    before = lambda c: jnp.maximum(c % nb - 1, 0)
    outs = pl.pallas_call(
        body,
        grid=(nchunk,),
        in_specs=[spec] * 6
        + [pl.BlockSpec((1, BLOCK, ATTN_WIDTH), lambda c: (c // nb, c % nb, 0)),
           pl.BlockSpec((1, BLOCK, KV_WIDTH), lambda c: (c // nb, before(c), kcol)),
           pl.BlockSpec((1, BLOCK, KV_WIDTH), lambda c: (c // nb, c % nb, kcol)),
           pl.BlockSpec((1, BLOCK, KV_WIDTH), lambda c: (c // nb, before(c), vcol)),
           pl.BlockSpec((1, BLOCK, KV_WIDTH), lambda c: (c // nb, c % nb, vcol)),
           pl.BlockSpec(sink_rows.shape, lambda c: (0, 0))]
        + [hbm] * nex,
        out_specs=[spec, per_chunk(LANES), per_chunk(4 * CHUNK), per_chunk(2 * CHUNK), per_chunk(2 * CHUNK),
                   pl.BlockSpec((1, BLOCK, ATTN_WIDTH), lambda c: (c // nb, c % nb, 0)),
                   pl.BlockSpec((1, 1, N_ATTN_HEADS, BLOCK, 2 * BLOCK), lambda c: (c // nb, c % nb, 0, 0, 0)),
                   pl.BlockSpec((1, BLOCK, LANES), lambda c: (c // nb, c % nb, 0))]
        + [hbm] * nex,
        out_shape=[jax.ShapeDtypeStruct((bsz, t, w), F32),
                   jax.ShapeDtypeStruct((nchunk, len(chains), LANES, LANES), F32),
                   kept_shape(4 * CHUNK), kept_shape(2 * CHUNK), kept_shape(2 * CHUNK),
                   jax.ShapeDtypeStruct((bsz, t, ATTN_WIDTH), BF16),
                   jax.ShapeDtypeStruct((bsz, nb, N_ATTN_HEADS, BLOCK, 2 * BLOCK), F32),
                   jax.ShapeDtypeStruct((bsz, t, LANES), F32)]
        + _exchange_out_shapes(hosted, scatter),
        scratch_shapes=[pltpu.VMEM((len(chains), LANES, LANES), F32)] + (_exchange_scratch(nex) if nex else []),
        compiler_params=_cparams(("arbitrary",), 48),
        name="rwkv_chunk_swa_fwd",
    )(r, lw, k, v, kk, al, z, z, z, z, z, sink_rows, *hosted)
    return outs[0], outs[1:5], outs[5:8], outs[8:]


def _rwkv_attn_bwd(r, lw, k, v, kk, al, from_fwd, dy, z, dout, probs, psinks, hosted=(), scatter=False):
    bsz, t, w = r.shape
    npair, nchunk = w // LANES, t // CHUNK
    nb = t // BLOCK
    assert bsz * nb == nchunk
    chains = _chains(bsz, npair)
    nex = len(hosted)
    apair = ATTN_WIDTH // LANES

    def body(*refs):
        r_ref, lw_ref, k_ref, v_ref, kk_ref, al_ref, s_ref, prod_ref, tinv_ref, u_ref, dy_ref = refs[:11]
        q_ref, kp_ref, kc_ref, vp_ref, vc_ref, do_ref, p_ref, ps_ref = refs[11:19]
        ex_in = refs[19:19 + nex]
        out_refs = refs[19 + nex:25 + nex]
        dz_ref, dsink_ref = refs[25 + nex:27 + nex]
        ex_out = refs[27 + nex:27 + 2 * nex]
        ds_scr, carry = refs[27 + 2 * nex:29 + 2 * nex]
        step = pl.program_id(0)

        @pl.when(step == 0)
        def _():
            ds_scr[...] = jnp.zeros_like(ds_scr)
            dsink_ref[...] = jnp.zeros_like(dsink_ref)

        @pl.when(step % nb == 0)
        def _():
            carry[...] = jnp.zeros_like(carry)

        _hosted_exchange(step == 0, step == nchunk - 1, ex_in, ex_out, refs[29 + 2 * nex:], scatter)
        qs = tuple(q_ref[0, :, pair * LANES:(pair + 1) * LANES] for pair in range(apair))
        dos = tuple(do_ref[0, :, pair * LANES:(pair + 1) * LANES] for pair in range(apair))
        attn = []
        stages = _attn_block_bwd_stages(qs, kp_ref[0], kc_ref[0], vp_ref[0], vc_ref[0], dos,
                                        [p_ref[0, 0, h] for h in range(N_ATTN_HEADS)], step % nb == nb - 1, attn)
        ss = tuple(s_ref[0, i] for i in range(len(chains)))
        kept = tuple(tuple(ref[0, i] for i in range(len(chains))) for ref in (prod_ref, tinv_ref, u_ref))
        _, vjp = jax.vjp(functools.partial(_chunk_fn, kept=kept), ss,
                         *[tuple(ref[b, :, cols] for b, _, cols in chains)
                           for ref in (r_ref, lw_ref, k_ref, v_ref, kk_ref, al_ref)])
        rules = [0]

        def one_stage_per_round():
            rules[0] += 1
            if rules[0] % len(chains) == 0:
                next(stages, None)

        _BESIDE_BACKWARD[0] = one_stage_per_round
        try:
            grads = vjp((tuple(dy_ref[b, :, cols].astype(F32) for b, _, cols in chains),
                         tuple(ds_scr[i] for i in range(len(chains)))))
        finally:
            _BESIDE_BACKWARD[0] = None
        for _ in stages:
            pass
        for i, (b, _, cols) in enumerate(chains):
            ds_scr[i] = grads[0][i]
            for ref, gval in zip(out_refs, grads[1:]):
                ref[b, :, cols] = gval[i]
        dqs, dkp, dkc, dvp, dvc, deltas = attn
        for pair in range(apair):
            dz_ref[0, :, pair * LANES:(pair + 1) * LANES] = dqs[pair].astype(dz_ref.dtype)
        dsink_ref[...] -= jnp.sum(ps_ref[0] * _head_columns(deltas), axis=0, keepdims=True)
        dz_ref[0, :, ATTN_WIDTH:ATTN_WIDTH + KV_WIDTH] = (dkc + carry[0]).astype(dz_ref.dtype)
        dz_ref[0, :, ATTN_WIDTH + KV_WIDTH:QKV_WIDTH] = (dvc + carry[1]).astype(dz_ref.dtype)
        carry[0] = dkp
        carry[1] = dvp

    spec = pl.BlockSpec((bsz, CHUNK, w), lambda c: (0, nchunk - 1 - c, 0))
    per_chunk = lambda n: pl.BlockSpec((1, len(chains), n, n), lambda c: (nchunk - 1 - c, 0, 0, 0))
    hbm = pl.BlockSpec(memory_space=pltpu.HBM)
    kcol, vcol = ATTN_WIDTH // KV_WIDTH, ATTN_WIDTH // KV_WIDTH + 1
    seq_of = lambda c: c // nb
    blk = lambda c: nb - 1 - c % nb
    before = lambda c: jnp.maximum(blk(c) - 1, 0)
    outs = pl.pallas_call(
        body,
        grid=(nchunk,),
        in_specs=[spec] * 6 + [per_chunk(LANES), per_chunk(4 * CHUNK), per_chunk(2 * CHUNK), per_chunk(2 * CHUNK), spec]
        + [pl.BlockSpec((1, BLOCK, ATTN_WIDTH), lambda c: (seq_of(c), blk(c), 0)),
           pl.BlockSpec((1, BLOCK, KV_WIDTH), lambda c: (seq_of(c), before(c), kcol)),
           pl.BlockSpec((1, BLOCK, KV_WIDTH), lambda c: (seq_of(c), blk(c), kcol)),
           pl.BlockSpec((1, BLOCK, KV_WIDTH), lambda c: (seq_of(c), before(c), vcol)),
           pl.BlockSpec((1, BLOCK, KV_WIDTH), lambda c: (seq_of(c), blk(c), vcol)),
           pl.BlockSpec((1, BLOCK, ATTN_WIDTH), lambda c: (seq_of(c), blk(c), 0)),
           pl.BlockSpec((1, 1, N_ATTN_HEADS, BLOCK, 2 * BLOCK), lambda c: (seq_of(c), blk(c), 0, 0, 0)),
           pl.BlockSpec((1, BLOCK, LANES), lambda c: (seq_of(c), blk(c), 0))]
        + [hbm] * nex,
        out_specs=[spec] * 6
        + [pl.BlockSpec((1, BLOCK, QKV_WIDTH), lambda c: (seq_of(c), blk(c), 0)),
           pl.BlockSpec((1, LANES), lambda c: (0, 0))]
        + [hbm] * nex,
        out_shape=[jax.ShapeDtypeStruct((bsz, t, w), F32)] * 6
        + [jax.ShapeDtypeStruct((bsz, t, QKV_WIDTH), BF16), jax.ShapeDtypeStruct((1, LANES), F32)]
        + _exchange_out_shapes(hosted, scatter),
        scratch_shapes=[pltpu.VMEM((len(chains), LANES, LANES), F32), pltpu.VMEM((2, BLOCK, KV_WIDTH), F32)]
        + (_exchange_scratch(nex) if nex else []),
        compiler_params=_cparams(("arbitrary",), 48),
        name="rwkv_chunk_swa_bwd",
    )(r, lw, k, v, kk, al, *from_fwd, dy, z, z, z, z, z, dout, probs, psinks, *hosted)
    return outs[:6], outs[6:8], outs[8:]


def _alibi_slope(head):
    return 2.0 ** (-8.0 * (head + 1) / N_ATTN_HEADS)


def _attn_setup(first):
    row = lax.broadcasted_iota(jnp.int32, (BLOCK, 2 * BLOCK), 0)
    col = lax.broadcasted_iota(jnp.int32, (BLOCK, 2 * BLOCK), 1)
    lane = lax.broadcasted_iota(jnp.int32, (1, LANES), 1)
    halves = [jnp.where((lane // HEAD_DIM) == half, 1.0, 0.0) for half in range(2)]
    srow = lax.broadcasted_iota(jnp.int32, (LANES, LANES), 0)
    scol = lax.broadcasted_iota(jnp.int32, (LANES, LANES), 1)
    swap = jnp.where((srow + HEAD_DIM) % LANES == scol, 1.0, 0.0)
    dist = row - col + BLOCK
    valid = jnp.logical_and(jnp.logical_and(dist >= 0, dist < BLOCK),
                            jnp.logical_or(col >= BLOCK, jnp.logical_not(first)))
    return halves, swap, dist.astype(F32), valid, HEAD_DIM ** -0.5


def _attn_keys_values(kp, kc, vp, vc, swap, npair):
    stored = (jnp.concatenate([kp, kc], axis=0), jnp.concatenate([vp, vc], axis=0))
    swapped = tuple(_bdot_raw(t, swap, NN) for t in stored)
    heads = [(pair, half) for pair in range(npair) for half in range(2)]
    return heads, [half == pair // 2 for pair, half in heads], stored, swapped


def _attn_block_stages(qs, kp, kc, vp, vc, sinks, first, result):
    halves, swap, dist, valid, scale = _attn_setup(first)
    heads, as_stored, stored, swapped = _attn_keys_values(kp, kc, vp, vc, swap, len(qs))
    kv = [stored if own else swapped for own in as_stored]
    slopes = [_alibi_slope(2 * pair + half) for pair, half in heads]
    qa = [qs[pair] * halves[half] for pair, half in heads]
    yield
    s = [jnp.where(valid, _bdot_raw(q, t[0], NT) * scale - sl * dist, NEG_INF) for q, t, sl in zip(qa, kv, slopes)]
    yield
    mx = [jnp.maximum(jnp.max(a, axis=-1, keepdims=True), sk) for a, sk in zip(s, sinks)]
    yield
    e = [jnp.exp(a - m) for a, m in zip(s, mx)]
    yield
    es = [jnp.exp(sk - m) for sk, m in zip(sinks, mx)]
    inv = [1.0 / (jnp.sum(a, axis=-1, keepdims=True) + b) for a, b in zip(e, es)]
    yield
    probs = [a * i for a, i in zip(e, inv)]
    yield
    o = [_bdot_raw(p, t[1], NN) for p, t in zip(probs, kv)]
    yield
    outs = tuple(o[2 * pair] * halves[0] + o[2 * pair + 1] * halves[1] for pair in range(len(qs)))
    result.extend([outs, probs, [b * i for b, i in zip(es, inv)]])


def _attn_block_bwd_stages(qs, kp, kc, vp, vc, dos, probs, first, result):
    halves, swap, _, _, scale = _attn_setup(first)
    heads, as_stored, stored, swapped = _attn_keys_values(kp, kc, vp, vc, swap, len(qs))
    kv = [stored if own else swapped for own in as_stored]
    qa = [qs[pair] * halves[half] for pair, half in heads]
    do = [dos[pair] * halves[half] for pair, half in heads]
    yield
    dp = [_bdot_raw(d, t[1], NT) for d, t in zip(do, kv)]
    yield
    delta = [jnp.sum(p * d, axis=-1, keepdims=True) for p, d in zip(probs, dp)]
    yield
    ds = [p * (d - dl) for p, d, dl in zip(probs, dp, delta)]
    yield
    dq = [_bdot_raw(g, t[0], NN) * (scale * halves[half]) for g, t, (_, half) in zip(ds, kv, heads)]
    yield
    dk = [_bdot_raw(g, q, TN) * scale for g, q in zip(ds, qa)]
    yield
    dv = [_bdot_raw(p, d, TN) for p, d in zip(probs, do)]
    yield
    dqs = tuple(dq[2 * pair] + dq[2 * pair + 1] for pair in range(len(qs)))

    def total(parts):
        direct = sum(g for g, own in zip(parts, as_stored) if own)
        return direct + _bdot_raw(sum(g for g, own in zip(parts, as_stored) if not own), swap, NN)

    dk_all, dv_all = total(dk), total(dv)
    result.extend([dqs, dk_all[:BLOCK], dk_all[BLOCK:], dv_all[:BLOCK], dv_all[BLOCK:], delta])


def _sink_values(sink_ref):
    return [jnp.max(sink_ref[h:h + 1, :], axis=-1, keepdims=True) for h in range(N_ATTN_HEADS)]


def _head_columns(cols):
    lane = lax.broadcasted_iota(jnp.int32, (1, LANES), 1)
    return sum(c * jnp.where(lane == h, 1.0, 0.0) for h, c in enumerate(cols))


def _exchange_out_shapes(arrays, scatter):
    return [jax.ShapeDtypeStruct((N_DEV,) + (a.shape[1:] if scatter else a.shape), a.dtype) for a in arrays]


def _exchange_scratch(n):
    return [pltpu.SemaphoreType.DMA((n, N_DEV - 1)), pltpu.SemaphoreType.DMA((n, N_DEV - 1)),
            pltpu.SemaphoreType.DMA((n,))]


def _exchange_copies(ins, outs, send_sems, recv_sems, local_sems, scatter, arrivals=True):
    x, y, c = lax.axis_index("x"), lax.axis_index("y"), lax.axis_index("c")
    me = 4 * x + 2 * y + c
    copies = []
    for i in range(len(ins)):
        own = pltpu.make_async_copy(ins[i].at[me] if scatter else ins[i], outs[i].at[me], local_sems.at[i])
        copies.append((own, None, True))
        for d in range(1, N_DEV):
            px = 1 - x if d & 4 else x
            py = 1 - y if d & 2 else y
            pc = 1 - c if d & 1 else c
            peer = 4 * px + 2 * py + pc
            src = ins[i].at[peer] if scatter else ins[i]
            send = pltpu.make_async_remote_copy(src, outs[i].at[me], send_sems.at[i, d - 1], recv_sems.at[i, d - 1],
                                                device_id=(px, py, pc), device_id_type=MESH)
            recv = pltpu.make_async_remote_copy(src, outs[i].at[peer], send_sems.at[i, d - 1], recv_sems.at[i, d - 1],
                                                device_id=(px, py, pc), device_id_type=MESH) if arrivals else None
            copies.append((send, recv, False))
    return copies


def _exchange_start(copies):
    for send, _, _ in copies:
        send.start()


def _exchange_wait(copies):
    for send, recv, local in copies:
        if local:
            send.wait()
        else:
            send.wait_send()
            recv.wait_recv()


def _gather_two_level(arrays, name):
    n = len(arrays)

    def body(*refs):
        ins, outs = refs[:n], refs[n:2 * n]
        send_sems, recv_sems, local_sems = refs[2 * n:]
        x, y, c = lax.axis_index("x"), lax.axis_index("y"), lax.axis_index("c")
        index = lambda px, py, pc: 4 * px + 2 * py + pc
        sibling = (x, y, 1 - c)
        chips = [(1 - x, y), (x, 1 - y), (1 - x, 1 - y)]

        def copy(i, k, block, to, src=None):
            slot = outs[i].at[index(*block)]
            return pltpu.make_async_remote_copy(slot if src is None else src, slot, send_sems.at[i, k],
                                                recv_sems.at[i, k], device_id=to, device_id_type=MESH)

        local, sends = [], []
        for i in range(n):
            own = pltpu.make_async_copy(ins[i], outs[i].at[index(x, y, c)], local_sems.at[i])
            own.start()
            local.append(own)
            first = [copy(i, 0, (x, y, c), sibling, src=ins[i])]
            first += [copy(i, 1 + j, (x, y, c), (*chip, c), src=ins[i]) for j, chip in enumerate(chips)]
            for cp in first:
                cp.start()
            sends += first
        for i in range(n):
            for j, chip in enumerate(chips):
                copy(i, 1 + j, (*chip, c), (x, y, c)).wait_recv()
                onward = copy(i, 4 + j, (*chip, c), sibling)
                onward.start()
                sends.append(onward)
        for i in range(n):
            copy(i, 0, sibling, (x, y, c)).wait_recv()
            for j, chip in enumerate(chips):
                copy(i, 4 + j, (*chip, 1 - c), (x, y, c)).wait_recv()
        for cp in sends:
            cp.wait_send()
        for cp in local:
            cp.wait()

    hbm = pl.BlockSpec(memory_space=pltpu.HBM)
    return pl.pallas_call(
        body,
        in_specs=[hbm] * n,
        out_specs=[hbm] * n,
        out_shape=_exchange_out_shapes(arrays, False),
        scratch_shapes=_exchange_scratch(n),
        name=name,
    )(*arrays)


def _exchange(arrays, *, scatter, name):
    n = len(arrays)

    def body(*refs):
        copies = _exchange_copies(refs[:n], refs[n:2 * n], *refs[2 * n:], scatter)
        _exchange_start(copies)
        _exchange_wait(copies)

    hbm = pl.BlockSpec(memory_space=pltpu.HBM)
    return pl.pallas_call(
        body,
        in_specs=[hbm] * n,
        out_specs=[hbm] * n,
        out_shape=_exchange_out_shapes(arrays, scatter),
        scratch_shapes=_exchange_scratch(n),
        name=name,
    )(*arrays)


def _adamw(parts, w, m, v, name):
    rows, cols = w.shape
    tr = _pick(rows, (256, 128, 64, 8))
    c1 = 1.0 / (1.0 - ADAM_B1 ** ADAM_STEP)
    c2 = 1.0 / (1.0 - ADAM_B2 ** ADAM_STEP)

    def body(p_ref, w_ref, m_ref, v_ref, g_ref, d_ref, mo_ref, vo_ref):
        g = p_ref[0].astype(F32)
        for s in range(1, N_DEV):
            g = g + p_ref[s].astype(F32)
        mn = ADAM_B1 * m_ref[...] + (1.0 - ADAM_B1) * g
        vn = ADAM_B2 * v_ref[...] + (1.0 - ADAM_B2) * (g * g)
        g_ref[...] = g
        mo_ref[...] = mn
        vo_ref[...] = vn
        d_ref[...] = -ADAM_LR * ((mn * c1) / (jnp.sqrt(vn * c2) + ADAM_EPS) + ADAM_WD * w_ref[...])

    spec = pl.BlockSpec((tr, cols), lambda i: (i, 0))
    return pl.pallas_call(
        body,
        grid=(rows // tr,),
        in_specs=[pl.BlockSpec((N_DEV, tr, cols), lambda i: (0, i, 0)), spec, spec, spec],
        out_specs=[spec] * 4,
        out_shape=[jax.ShapeDtypeStruct((rows, cols), F32)] * 4,
        compiler_params=_cparams(("parallel",), 48),
        name=name,
    )(parts, w, m, v)


_VECTOR_PARAMS = ("attn_norm_g", "attn_sinks", "rwkv_mu", "w0", "a0", "k_k", "k_a", "r_k", "ln_x_w", "ln_x_b",
                  "mlp_norm_g", "final_norm_g")
_WEIGHT_NAMES = ("attn_norm_g", "w_in", "attn_sinks", "rwkv_mu", "w0", "w2", "a0", "a2", "g2", "k_k", "k_a", "r_k",
                 "ln_x_w", "ln_x_b", "w_out", "mlp_norm_g", "w_up", "w_down", "final_norm_g")


def _pack_vectors(vals):
    pieces = []
    for name in _VECTOR_PARAMS:
        flat = vals[name].reshape(1, -1)
        pad = (-flat.shape[1]) % LANES
        pieces.append(jnp.pad(flat, ((0, 0), (0, pad))) if pad else flat)
    return jnp.concatenate(pieces, axis=1)


def _unpack_vectors(packed, like):
    out, col = {}, 0
    for name in _VECTOR_PARAMS:
        size = like[name].size
        out[name] = packed[0, col:col + size].reshape(like[name].shape)
        col += size + (-size) % LANES
    return out


def kernel(x, attn_norm_g, w_in, attn_sinks, rwkv_mu, w0, w2, a0, a2, g2, k_k, k_a, r_k, ln_x_w, ln_x_b, w_out, mlp_norm_g, w_up, w_down, final_norm_g, loss_target, m_attn_norm_g, m_w_in, m_attn_sinks, m_rwkv_mu, m_w0, m_w2, m_a0, m_a2, m_g2, m_k_k, m_k_a, m_r_k, m_ln_x_w, m_ln_x_b, m_w_out, m_mlp_norm_g, m_w_up, m_w_down, m_final_norm_g, v_attn_norm_g, v_w_in, v_attn_sinks, v_rwkv_mu, v_w0, v_w2, v_a0, v_a2, v_g2, v_k_k, v_k_a, v_r_k, v_ln_x_w, v_ln_x_b, v_w_out, v_mlp_norm_g, v_w_up, v_w_down, v_final_norm_g):
    weights = dict(attn_norm_g=attn_norm_g, w_in=w_in, attn_sinks=attn_sinks, rwkv_mu=rwkv_mu, w0=w0, w2=w2, a0=a0,
                   a2=a2, g2=g2, k_k=k_k, k_a=k_a, r_k=r_k, ln_x_w=ln_x_w, ln_x_b=ln_x_b, w_out=w_out,
                   mlp_norm_g=mlp_norm_g, w_up=w_up, w_down=w_down, final_norm_g=final_norm_g)
    mom1 = dict(attn_norm_g=m_attn_norm_g, w_in=m_w_in, attn_sinks=m_attn_sinks, rwkv_mu=m_rwkv_mu, w0=m_w0, w2=m_w2,
                a0=m_a0, a2=m_a2, g2=m_g2, k_k=m_k_k, k_a=m_k_a, r_k=m_r_k, ln_x_w=m_ln_x_w, ln_x_b=m_ln_x_b,
                w_out=m_w_out, mlp_norm_g=m_mlp_norm_g, w_up=m_w_up, w_down=m_w_down, final_norm_g=m_final_norm_g)
    mom2 = dict(attn_norm_g=v_attn_norm_g, w_in=v_w_in, attn_sinks=v_attn_sinks, rwkv_mu=v_rwkv_mu, w0=v_w0, w2=v_w2,
                a0=v_a0, a2=v_a2, g2=v_g2, k_k=v_k_k, k_a=v_k_a, r_k=v_r_k, ln_x_w=v_ln_x_w, ln_x_b=v_ln_x_b,
                w_out=v_w_out, mlp_norm_g=v_mlp_norm_g, w_up=v_w_up, w_down=v_w_down, final_norm_g=v_final_norm_g)
    bsz, seq, d_model = x.shape
    rows = bsz * seq
    d_in = N_DEV * w_in.shape[2]
    d_ff = N_DEV * w_up.shape[2]

    gathered = _gather_two_level([w_in[0].astype(BF16), w2[0], a2[0], g2[0]], name="gather_in_weights")
    cols_first = lambda a: a.transpose(1, 0, 2).reshape(a.shape[1], -1)
    w_in_f = cols_first(gathered[0])
    w_attn, w_rw = w_in_f[:, :QKV_WIDTH], w_in_f[:, QKV_WIDTH:]
    w2_f, a2_f, g2_f = cols_first(gathered[1]), cols_first(gathered[2]), cols_first(gathered[3])
    lora = w2_f.shape[0]
    w2p = jnp.concatenate([w2_f, jnp.zeros_like(a2_f)], axis=0)
    a2p = jnp.concatenate([jnp.zeros_like(w2_f), a2_f], axis=0)

    esum = _head_sum_matrix()
    sink_rows = jnp.broadcast_to(attn_sinks.reshape(N_ATTN_HEADS, 1), (N_ATTN_HEADS, LANES))
    prep_pars = [w0, w2p, a0, a2p, g2_f, k_k, k_a, esum]
    post_pars = [ln_x_w, ln_x_b, r_k, esum]

    x2d = x.reshape(rows, d_model)
    h1, z_attn, z_rw, r, lw, k, v, kk, al, gate = _norm_in_proj_prep(x2d, seq, attn_norm_g, w_attn, w_rw, rwkv_mu,
                                                                       prep_pars)
    z_attn3 = z_attn.reshape(bsz, seq, QKV_WIDTH)
    as3 = lambda a: a.reshape(bsz, seq, RWKV_WIDTH)
    y, from_fwd, (attn_out, attn_probs, attn_psinks), late = _rwkv_attn_fwd(
        as3(r), as3(lw), as3(k), as3(v), as3(kk), as3(al), z_attn3, sink_rows,
        hosted=[w_out[0].astype(BF16), w_up[0].astype(BF16), w_down[0].astype(BF16)])
    w_out_f = late[0].reshape(-1, d_model)
    w_up_f = cols_first(late[1])
    w_down_f = late[2].reshape(-1, d_model)
    y2 = y.reshape(rows, RWKV_WIDTH)
    attn_out2d = attn_out.reshape(rows, ATTN_WIDTH)
    w_out_attn, w_out_rw = w_out_f[:ATTN_WIDTH], w_out_f[ATTN_WIDTH:]
    rw_out, x1, h2 = _post_out_proj_norm(y2, r, k, v, gate, attn_out2d, x2d, post_pars, w_out_attn, w_out_rw,
                                         mlp_norm_g)

    u = _matmul(h2, w_up_f, "nn", name="mlp_up", out_dtypes=(BF16,))
    dx2, dx2_b, loss_vec, g_final = _down_proj_loss(u, w_down_f, x1, loss_target.reshape(rows, d_model),
                                             final_norm_g.reshape(1, d_model))

    g_w_down = _matmul(u, dx2_b, "tn", name="grad_w_down", out_dtypes=(BF16,), a_map=_relu_squared)
    du = _matmul(dx2_b, w_down_f, "nt", name="mlp_down_bwd", extras=(u,), out_dtypes=(BF16,), tm=2048,
                 epilogue=lambda acc, uv: (acc * (2.0 * jnp.maximum(uv.astype(F32), 0.0)),))
    g_w_up = _matmul(h2, du, "tn", name="grad_w_up", out_dtypes=(BF16,))
    dx1, g_mlp_norm = _proj_bwd_norm_bwd([du], [w_up_f], x1, dx2, mlp_norm_g, "mlp_up_bwd_norm_bwd")
    g_w_out = jnp.concatenate([_matmul(attn_out2d, dx1, "tn", name="grad_w_out_attn", out_dtypes=(BF16,)),
                               _matmul(rw_out, dx1, "tn", name="grad_w_out_rwkv", out_dtypes=(BF16,))], axis=0)
    d_attn_out, dy, dr_a, dk_a, dv_a, dgate, g_ln_w, g_ln_b, g_r_k = _out_proj_bwd_post_bwd(
        dx1, y2, r, k, v, gate, post_pars, w_out_attn, w_out_rw)
    by_cols = lambda a: a.reshape(a.shape[0], N_DEV, -1).transpose(1, 0, 2)
    (dr_b, dlw, dk_b, dv_b, dkk, dal), (dz_attn, g_sink_lanes), (p_w_out, p_w_up, p_w_down) = _rwkv_attn_bwd(
        as3(r), as3(lw), as3(k), as3(v), as3(kk), as3(al), from_fwd, as3(dy),
        z_attn3, d_attn_out.reshape(bsz, seq, ATTN_WIDTH), attn_probs, attn_psinks,
        hosted=[g_w_out.reshape(N_DEV, -1, d_model), by_cols(g_w_up), g_w_down.reshape(N_DEV, -1, d_model)],
        scatter=True)
    flat = lambda a: a.reshape(rows, RWKV_WIDTH)
    (dz_rw, gmu_r, gmu_k, gmu_v, gmu_wa, gmu_g, g_w0, g_w2p, g_a0, g_a2p, g_g2, g_k_k, g_k_a) = _prep_bwd(
        z_rw, seq, [dr_a, flat(dr_b), flat(dlw), dk_a, flat(dk_b), dv_a, flat(dv_b), flat(dkk), flat(dal), dgate],
        rwkv_mu, prep_pars)
    dz_attn = dz_attn.reshape(rows, QKV_WIDTH)
    g_w_in = jnp.concatenate([_matmul(h1, dz_attn, "tn", name="grad_w_in_attn", out_dtypes=(BF16,)),
                              _matmul(h1, dz_rw, "tn", name="grad_w_in_rwkv", out_dtypes=(BF16,))], axis=1)
    lora_grads = jnp.concatenate([g_w2p[:lora], g_a2p[lora:], g_g2], axis=0)
    (dx, g_attn_norm), (p_w_in, p_lora) = _proj_bwd_norm_bwd(
        [dz_attn, dz_rw], [w_attn, w_rw], x2d, dx1, attn_norm_g, "in_proj_bwd_norm_bwd",
        hosted=[by_cols(g_w_in), by_cols(lora_grads)], scatter=True)

    vec_grads = dict(attn_norm_g=g_attn_norm, attn_sinks=g_sink_lanes[0, :N_ATTN_HEADS], rwkv_mu=jnp.concatenate(
        [gmu_r, gmu_k, gmu_v, gmu_wa, gmu_g], axis=1), w0=g_w0, a0=g_a0, k_k=g_k_k, k_a=g_k_a, r_k=g_r_k,
        ln_x_w=g_ln_w, ln_x_b=g_ln_b, mlp_norm_g=g_mlp_norm, final_norm_g=g_final)
    packed = _pack_vectors(vec_grads)
    nvec = packed.shape[1]
    everyone = _exchange([jnp.concatenate([packed, loss_vec], axis=1)], scatter=False, name="gather_vector_grads")[0]
    vec_parts = everyone[:, :, :nvec]
    loss = jnp.sum(everyone[:, 0, nvec])

    grads, delta, new_m, new_v = {}, {}, {}, {}

    def update(name, part, shape2d):
        res = _adamw(part, weights[name].reshape(shape2d), mom1[name].reshape(shape2d), mom2[name].reshape(shape2d),
                     "adamw_" + name)
        for store, val in zip((grads, delta, new_m, new_v), res):
            store[name] = val.reshape(weights[name].shape)

    update("w_in", p_w_in, w_in.shape[1:])
    update("w_out", p_w_out, w_out.shape[1:])
    update("w_up", p_w_up, w_up.shape[1:])
    update("w_down", p_w_down, w_down.shape[1:])
    stack = lambda d: jnp.concatenate([d["w2"][0], d["a2"][0], d["g2"][0]], axis=0)
    lora_res = _adamw(p_lora, stack(weights), stack(mom1), stack(mom2), "adamw_lora")
    for store, val in zip((grads, delta, new_m, new_v), lora_res):
        store["w2"], store["a2"], store["g2"] = val[None, :lora], val[None, lora:2 * lora], val[None, 2 * lora:]
    vec_res = _adamw(vec_parts, _pack_vectors(weights), _pack_vectors(mom1), _pack_vectors(mom2), "adamw_vectors")
    for store, val in zip((grads, delta, new_m, new_v), vec_res):
        store.update(_unpack_vectors(val, weights))

    return (loss, dx.reshape(x.shape), *[grads[n] for n in _WEIGHT_NAMES], *[delta[n] for n in _WEIGHT_NAMES],
            *[new_m[n] for n in _WEIGHT_NAMES], *[new_v[n] for n in _WEIGHT_NAMES])
```

```python
import functools
import math

import jax
import jax.numpy as jnp
from jax import lax
from jax.experimental import pallas as pl
from jax.experimental.pallas import tpu as pltpu

F32 = jnp.float32
BF16 = jnp.bfloat16

N_DEV = 8
HEAD_DIM = 64
LANES = 128
N_ATTN_HEADS = 8
ATTN_WIDTH = 512
KV_WIDTH = 128
QKV_WIDTH = ATTN_WIDTH + 2 * KV_WIDTH
RWKV_WIDTH = 512
LORA_WA = 128
GATE_LORA = 128
RWKV_SHIFT_WIDTH = 3 * RWKV_WIDTH + LORA_WA + GATE_LORA
BLOCK = 128
CHUNK = 64
RMS_EPS = 1e-6
GN_EPS = 64e-5
L2_EPS = 1e-12
NEG_INF = -1e30
DECAY_SCALE = math.exp(-0.5)
ADAM_LR, ADAM_B1, ADAM_B2, ADAM_EPS, ADAM_WD, ADAM_STEP = 0.001, 0.9, 0.999, 1e-08, 0.01, 10

NN = (((1,), (0,)), ((), ()))
NT = (((1,), (1,)), ((), ()))
TN = (((0,), (0,)), ((), ()))
MESH = pl.DeviceIdType.MESH


def _dot(a, b, dn=NN, precision=None):
    return lax.dot_general(a, b, dn, precision=precision, preferred_element_type=F32)


def _bdot_raw(a, b, dn):
    return lax.dot_general(a.astype(BF16), b.astype(BF16), dn, preferred_element_type=F32)


@functools.partial(jax.custom_vjp, nondiff_argnums=(2, 3))
def _bdot_c(a, b, ca, cb):
    return _bdot_raw(a, b, (((ca,), (cb,)), ((), ())))


def _bdot_c_fwd(a, b, ca, cb):
    return _bdot_c(a, b, ca, cb), (a, b)


_BESIDE_BACKWARD = [None]


def _beside_backward():
    if _BESIDE_BACKWARD[0] is not None:
        _BESIDE_BACKWARD[0]()


def _bdot_c_bwd(ca, cb, res, ct):
    _beside_backward()
    a, b = res
    fa, fb = 1 - ca, 1 - cb
    da = _bdot_raw(ct, b, (((1,), (fb,)), ((), ()))) if ca == 1 else _bdot_raw(b, ct, (((fb,), (1,)), ((), ())))
    db = _bdot_raw(a, ct, (((fa,), (0,)), ((), ()))) if cb == 0 else _bdot_raw(ct, a, (((0,), (fa,)), ((), ())))
    return da, db


_bdot_c.defvjp(_bdot_c_fwd, _bdot_c_bwd)


def _bdot(a, b, dn=NN):
    return _bdot_c(a, b, dn[0][0][0], dn[0][1][0])


def _bdot_nn(a, b):
    return _bdot(a, b, NN)


def _split3(x):
    hi = x.astype(BF16)
    rest = x - hi.astype(F32)
    mid = rest.astype(BF16)
    return hi, mid, (rest - mid.astype(F32)).astype(BF16)


def _running_sum(x, dn):
    c = x.shape[0]
    row = lax.broadcasted_iota(jnp.int32, (c, c), 0)
    col = lax.broadcasted_iota(jnp.int32, (c, c), 1)
    tri = jnp.where(row >= col, 1.0, 0.0).astype(BF16)
    w = x.shape[1]
    parts = lax.dot_general(tri, jnp.concatenate(_split3(x), axis=1), dn, preferred_element_type=F32)
    return parts[:, :w] + parts[:, w:2 * w] + parts[:, 2 * w:]


@jax.custom_vjp
def _cumsum_rows(x):
    return _running_sum(x, NN)


_cumsum_rows.defvjp(lambda x: (_running_sum(x, NN), None), lambda _, ct: (_running_sum(ct, TN),))


@jax.custom_vjp
def _fold_rows(x):
    c = x.shape[0] // 2
    return x[:c] + x[c:]


_fold_rows.defvjp(lambda x: (_fold_rows(x), None), lambda _, ct: (jnp.concatenate([ct, ct], axis=0),))


@jax.custom_vjp
def _halves(x):
    n = x.shape[0] // 2
    return x[:n], x[n:]


_halves.defvjp(lambda x: (_halves(x), None), lambda _, cts: (jnp.concatenate(cts, axis=0),))


@jax.custom_vjp
def _quarters(x):
    n = x.shape[0] // 2
    return x[:n, :n], x[:n, n:], x[n:, :n], x[n:, n:]


_quarters.defvjp(lambda x: (_quarters(x), None),
                 lambda _, cts: (jnp.concatenate([jnp.concatenate(cts[:2], axis=1),
                                                  jnp.concatenate(cts[2:], axis=1)], axis=0),))


@jax.custom_vjp
def _sigmoid(x):
    return 1.0 / (1.0 + jnp.exp(-x))


def _sigmoid_fwd(x):
    s = _sigmoid(x)
    return s, s


_sigmoid.defvjp(_sigmoid_fwd, lambda s, ct: (ct * s * (1.0 - s),))


def _pick(n, cands):
    for c in cands:
        if n % c == 0:
            return c
    return n


def _cparams(sem, vmem_mb=None):
    kw = dict(dimension_semantics=sem)
    if vmem_mb is not None:
        kw["vmem_limit_bytes"] = vmem_mb * 1024 * 1024
    return pltpu.CompilerParams(**kw)


def _matmul(a, b, mode, *, name, extras=(), epilogue=None, out_dtypes=(F32,), tm=1024, tn=1024, tk=1024,
            hosted=(), scatter=False, a_map=lambda tile: tile):
    if mode == "nn":
        (M, K), (_, N) = a.shape, b.shape
    elif mode == "tn":
        (K, M), (_, N) = a.shape, b.shape
    else:
        (M, K), (N, _) = a.shape, b.shape
    tm = _pick(M, (tm, 512, 256, 128))
    tn = _pick(N, (tn, 896, 768, 512, 384, 256, 128))
    tk = _pick(K, (tk, 512, 256, 128))
    nk = K // tk
    ne, nout = len(extras), len(out_dtypes)
    if mode == "nn":
        a_spec = pl.BlockSpec((tm, tk), lambda i, j, k: (i, k))
        b_spec = pl.BlockSpec((tk, tn), lambda i, j, k: (k, j))
        dn = NN
    elif mode == "tn":
        a_spec = pl.BlockSpec((tk, tm), lambda i, j, k: (k, i))
        b_spec = pl.BlockSpec((tk, tn), lambda i, j, k: (k, j))
        dn = TN
    else:
        a_spec = pl.BlockSpec((tm, tk), lambda i, j, k: (i, k))
        b_spec = pl.BlockSpec((tn, tk), lambda i, j, k: (j, k))
        dn = NT
    o_spec = pl.BlockSpec((tm, tn), lambda i, j, k: (i, j))
    grid = (M // tm, N // tn, nk)
    nex = len(hosted)

    def body(*refs):
        a_ref, b_ref = refs[:2]
        e_refs = refs[2:2 + ne]
        ex_in = refs[2 + ne:2 + ne + nex]
        o_refs = refs[2 + ne + nex:2 + ne + nex + nout]
        ex_out = refs[2 + ne + nex + nout:2 + ne + 2 * nex + nout]
        scratch = refs[2 + ne + 2 * nex + nout:]
        kstep = pl.program_id(2)
        if nex:
            at = [pl.program_id(d) for d in range(3)]
            first = jnp.logical_and(jnp.logical_and(at[0] == 0, at[1] == 0), at[2] == 0)
            last = jnp.logical_and(jnp.logical_and(at[0] == grid[0] - 1, at[1] == grid[1] - 1), at[2] == grid[2] - 1)
            _hosted_exchange(first, last, ex_in, ex_out, scratch[-3:], scatter)

        def finish(total):
            outs = (total,) if epilogue is None else epilogue(total, *[e[...] for e in e_refs])
            for o_ref, o in zip(o_refs, outs):
                o_ref[...] = o.astype(o_ref.dtype)

        if nk == 1:
            finish(_bdot_raw(a_map(a_ref[...]), b_ref[...], dn))
            return
        acc = scratch[0]

        @pl.when(kstep == 0)
        def _():
            acc[...] = jnp.zeros_like(acc)

        acc[...] += _bdot_raw(a_map(a_ref[...]), b_ref[...], dn)

        @pl.when(kstep == nk - 1)
        def _():
            finish(acc[...])

    hbm = pl.BlockSpec(memory_space=pltpu.HBM)
    outs = pl.pallas_call(
        body,
        grid=grid,
        in_specs=[a_spec, b_spec] + [o_spec] * ne + [hbm] * nex,
        out_specs=[o_spec] * nout + [hbm] * nex,
        out_shape=[jax.ShapeDtypeStruct((M, N), dt) for dt in out_dtypes] + _exchange_out_shapes(hosted, scatter),
        scratch_shapes=([pltpu.VMEM((tm, tn), F32)] if nk > 1 else []) + (_exchange_scratch(nex) if nex else []),
        compiler_params=_cparams(("arbitrary",) * 3 if nex else ("parallel", "parallel", "arbitrary"), 56),
        name=name,
    )(a, b, *extras, *hosted)
    if nex:
        return outs[:nout], outs[nout:]
    return outs[0] if nout == 1 else outs


def _rowwise(fn, rows, pars, out_rows, out_accs, *, tile, name, nsub=1, hosted=(), scatter=False):
    rows = [r if isinstance(r, tuple) else (r, r.shape[1], 0) for r in rows]
    R = rows[0][0].shape[0]
    tile = min(tile, R)
    chunk = tile // nsub
    ntile = R // tile
    nr, npar, nor, noa, nex = len(rows), len(pars), len(out_rows), len(out_accs), len(hosted)

    def body(*refs):
        rin = refs[:nr]
        pin = refs[nr:nr + npar]
        ex_in = refs[nr + npar:nr + npar + nex]
        orow = refs[nr + npar + nex:nr + npar + nex + nor]
        oacc = refs[nr + npar + nex + nor:nr + npar + nex + nor + noa]
        ex_out = refs[nr + npar + nex + nor + noa:nr + npar + 2 * nex + nor + noa]
        step = pl.program_id(0)
        _hosted_exchange(step == 0, step == ntile - 1, ex_in, ex_out, refs[nr + npar + 2 * nex + nor + noa:], scatter)
        pvals = [p[...] for p in pin]
        totals = []
        for sub in range(nsub):
            at = slice(sub * chunk, (sub + 1) * chunk)
            outs = fn(*[r[at, :] for r in rin], *pvals)
            for ref, o in zip(orow, outs[:nor]):
                if isinstance(o, (tuple, list)):
                    col = 0
                    for piece in o:
                        ref[at, col:col + piece.shape[1]] = piece.astype(ref.dtype)
                        col += piece.shape[1]
                else:
                    ref[at, :] = o.astype(ref.dtype)
            accs = list(outs[nor:])
            totals = accs if sub == 0 else [t + a for t, a in zip(totals, accs)]

        def accumulate(ref, o):
            @pl.when(step == 0)
            def _():
                ref[...] = o

            @pl.when(step > 0)
            def _():
                ref[...] += o

        for ref, o in zip(oacc, totals):
            accumulate(ref, o)

    def colspec(width, cb):
        return pl.BlockSpec((tile, width), lambda i: (i, cb))

    hbm = pl.BlockSpec(memory_space=pltpu.HBM)
    outs = pl.pallas_call(
        body,
        grid=(ntile,),
        in_specs=[colspec(w, cb) for (_, w, cb) in rows]
        + [pl.BlockSpec(p.shape, lambda i: (0, 0), pipeline_mode=pl.Buffered(1)) for p in pars] + [hbm] * nex,
        out_specs=[colspec(w, 0) for (w, _) in out_rows]
        + [pl.BlockSpec(s, lambda i: (0, 0)) for s in out_accs] + [hbm] * nex,
        out_shape=[jax.ShapeDtypeStruct((R, w), dt) for (w, dt) in out_rows]
        + [jax.ShapeDtypeStruct(s, F32) for s in out_accs] + _exchange_out_shapes(hosted, scatter),
        scratch_shapes=_exchange_scratch(nex) if nex else [],
        compiler_params=_cparams(("arbitrary",), 56),
        name=name,
    )(*[r[0] for r in rows], *pars, *hosted)
    return (outs[:nor + noa], outs[nor + noa:]) if nex else outs


def _rms_fn(x, g):
    return x * lax.rsqrt(jnp.mean(x * x, axis=-1, keepdims=True) + RMS_EPS) * g


FUSED_TILE = 512
FUSED_CHUNKS = 2


def _relu_squared(u):
    pos = jnp.maximum(u.astype(F32), 0.0)
    return pos * pos


def _down_proj_loss(u, w_down, x1, tgt, g):
    d = x1.shape[1]

    def fn(uv, xv, tv, wv, gv):
        x2 = xv + _bdot_raw(_relu_squared(uv), wv, NN)
        y, vjp = jax.vjp(_rms_fn, x2, gv)
        err = y - tv
        loss = 0.5 * jnp.sum(jnp.sum(err * err, axis=-1, keepdims=True), axis=0, keepdims=True) / d
        dx, dg = vjp(err / d)
        return dx, dx, jnp.broadcast_to(loss, (1, LANES)), dg

    return _rowwise(fn, [u, x1, tgt], [w_down, g], [(d, F32), (d, BF16)], [(1, LANES), g.shape],
                    tile=FUSED_TILE, nsub=FUSED_CHUNKS, name="mlp_down_final_norm_loss")


def _proj_bwd_norm_bwd(cts, weights_t, x, dres, g, name, hosted=(), scatter=False):
    n = len(cts)

    def fn(*vals):
        ctv, (xv, dresv), wv, gv = vals[:n], vals[n:n + 2], vals[n + 2:2 * n + 2], vals[-1]
        dh = _bdot_raw(ctv[0], wv[0], NT)
        for c, w in zip(ctv[1:], wv[1:]):
            dh = dh + _bdot_raw(c, w, NT)
        _, vjp = jax.vjp(_rms_fn, xv, gv)
        dx, dg = vjp(dh)
        return dx + dresv, dg

    return _rowwise(fn, [*cts, x, dres], [*weights_t, g], [(x.shape[1], F32)], [g.shape],
                    tile=FUSED_TILE, nsub=FUSED_CHUNKS, name=name, hosted=hosted, scatter=scatter)


def _head_sum_matrix():
    i = lax.broadcasted_iota(jnp.int32, (RWKV_WIDTH, RWKV_WIDTH), 0) // HEAD_DIM
    j = lax.broadcasted_iota(jnp.int32, (RWKV_WIDTH, RWKV_WIDTH), 1) // HEAD_DIM
    return (i == j).astype(BF16)


def _head_sums_raw(x, esum):
    hi = x.astype(BF16)
    lo = (x - hi.astype(F32)).astype(BF16)
    return _dot(hi, esum) + _dot(lo, esum)


@jax.custom_vjp
def _head_sums(x, esum):
    return _head_sums_raw(x, esum)


_head_sums.defvjp(lambda x, esum: (_head_sums_raw(x, esum), esum),
                  lambda esum, ct: (_head_sums_raw(ct, esum), jnp.zeros_like(esum)))


def _prep_core(xr, xk, xv, xwa, xg, w0, w2p, a0, a2p, g2, k_k, k_a, esum):
    lw = -DECAY_SCALE * _sigmoid(w0 + _bdot_nn(jnp.tanh(xwa), w2p))
    a = _sigmoid(a0 + _bdot_nn(xwa, a2p))
    g = _bdot_nn(_sigmoid(xg), g2)
    kk0 = xk * k_k
    kk = kk0 * jnp.minimum(lax.rsqrt(_head_sums(kk0 * kk0, esum)), 1.0 / L2_EPS)
    k = xk * (1.0 + (a - 1.0) * k_a)
    return xr, lw, k, xv, kk, a, g


_SEGS = ((0, 512), (512, 1024), (1024, 1536), (1536, 1664), (1664, 1792))


PREP_TILE = 256
SUBLANES = 8


def _shifted_tokens(z_ref, zprev_ref, tile_index, seq):
    zc = z_ref[...]
    start = (tile_index * PREP_TILE) % seq == 0
    before = jnp.where(start, 0.0, zprev_ref[SUBLANES - 1:SUBLANES, :])
    rowid = lax.broadcasted_iota(jnp.int32, zc.shape, 0)
    return zc, jnp.where(rowid == 0, before, pltpu.roll(zc, 1, 0))


def _prep_specs(z, mu, pars, index):
    width = z.shape[1]
    per = PREP_TILE // SUBLANES
    return ([pl.BlockSpec((PREP_TILE, width), lambda i: (index(i), 0)),
             pl.BlockSpec((SUBLANES, width), lambda i: (jnp.maximum(index(i) * per - 1, 0), 0))],
            [pl.BlockSpec(p.shape, lambda i: (0, 0)) for p in (mu, *pars)])


def _norm_in_proj_prep(x, seq, g, w_attn, w_rw, mu, pars):
    rows, d = x.shape
    chunk = FUSED_TILE // FUSED_CHUNKS
    npar = len(pars)
    wa_width, wr_width = w_attn.shape[1], w_rw.shape[1]

    def body(x_ref, g_ref, wa_ref, wr_ref, mu_ref, *rest):
        par_refs = rest[:npar]
        h_ref, za_ref, zr_ref = rest[npar:npar + 3]
        out_refs, carry = rest[npar + 3:-1], rest[-1]
        step = pl.program_id(0)

        @pl.when(step == 0)
        def _():
            carry[...] = jnp.zeros_like(carry)

        pv = [p[...] for p in par_refs]
        for sub in range(FUSED_CHUNKS):
            at = slice(sub * chunk, (sub + 1) * chunk)
            h = _rms_fn(x_ref[at, :], g_ref[...])
            h_ref[at, :] = h.astype(h_ref.dtype)
            za_ref[at, :] = _bdot_raw(h, wa_ref[...], NN).astype(za_ref.dtype)
            zc = _bdot_raw(h, wr_ref[...], NN)
            zr_ref[at, :] = zc
            start = (step * FUSED_TILE + sub * chunk) % seq == 0
            before = jnp.where(start, 0.0, carry[SUBLANES - 1:SUBLANES, :])
            rowid = lax.broadcasted_iota(jnp.int32, zc.shape, 0)
            zp = jnp.where(rowid == 0, before, pltpu.roll(zc, 1, 0))
            carry[...] = zc[chunk - SUBLANES:chunk, :]
            zs = zc + (zp - zc) * mu_ref[...]
            outs = _prep_core(*[zs[:, a:b] for a, b in _SEGS], *pv)
            for ref, o in zip(out_refs, outs):
                ref[at, :] = o

    tiled = lambda width: pl.BlockSpec((FUSED_TILE, width), lambda i: (i, 0))
    resident = lambda a: pl.BlockSpec(a.shape, lambda i: (0, 0), pipeline_mode=pl.Buffered(1))
    return pl.pallas_call(
        body,
        grid=(rows // FUSED_TILE,),
        in_specs=[tiled(d)] + [resident(a) for a in (g, w_attn, w_rw, mu, *pars)],
        out_specs=[tiled(d), tiled(wa_width), tiled(wr_width)] + [tiled(RWKV_WIDTH)] * 7,
        out_shape=[jax.ShapeDtypeStruct((rows, d), BF16), jax.ShapeDtypeStruct((rows, wa_width), BF16),
                   jax.ShapeDtypeStruct((rows, wr_width), F32)] + [jax.ShapeDtypeStruct((rows, RWKV_WIDTH), F32)] * 7,
        scratch_shapes=[pltpu.VMEM((SUBLANES, wr_width), F32)],
        compiler_params=_cparams(("arbitrary",), 56),
        name="attn_norm_in_proj_rwkv_prep",
    )(x, g, w_attn, w_rw, mu, *pars)


def _prep_bwd(z, seq, cts, mu, pars):
    rows, width = z.shape
    ntile = rows // PREP_TILE
    npar, nct = len(pars), len(cts)
    acc_shapes = [(1, b - a) for a, b in _SEGS] + [p.shape for p in pars[:-1]]

    def body(z_ref, zprev_ref, *rest):
        ct_refs = rest[:nct]
        mu_ref = rest[nct]
        par_refs = rest[nct + 1:nct + 1 + npar]
        dz_ref = rest[nct + 1 + npar]
        acc_refs = rest[nct + 2 + npar:-1]
        carry = rest[-1]
        step = pl.program_id(0)
        tile_index = ntile - 1 - step

        @pl.when(step == 0)
        def _():
            carry[...] = jnp.zeros_like(carry)

        zc, zp = _shifted_tokens(z_ref, zprev_ref, tile_index, seq)
        mu_v = mu_ref[...]
        diff = zp - zc
        zs = zc + diff * mu_v
        dra, drb, dlw, dka, dkb, dva, dvb, dkk, da, dg = [c[...] for c in ct_refs]
        pv = [p[...] for p in par_refs]
        _, vjp = jax.vjp(lambda *args: _prep_core(*args, pv[-1]), *[zs[:, a:b] for a, b in _SEGS], *pv[:-1])
        grads = vjp((dra + drb, dlw, dka + dkb, dva + dvb, dkk, da, dg))
        dsegs, dpars = grads[:5], grads[5:]
        last_of_sequence = ((tile_index + 1) * PREP_TILE) % seq == 0
        accs = []
        for ds, (a, b) in zip(dsegs, _SEGS):
            mu_s = mu_v[:, a:b]
            dzp = ds * mu_s
            after = jnp.where(last_of_sequence, 0.0, carry[0:1, a:b])
            rowid = lax.broadcasted_iota(jnp.int32, dzp.shape, 0)
            from_next = jnp.where(rowid == PREP_TILE - 1, after, pltpu.roll(dzp, PREP_TILE - 1, 0))
            dz_ref[:, a:b] = (ds * (1.0 - mu_s) + from_next).astype(dz_ref.dtype)
            carry[:, a:b] = dzp[0:SUBLANES, :]
            accs.append(jnp.sum(ds * diff[:, a:b], axis=0, keepdims=True))
        accs.extend(dpars)

        def accumulate(ref, o):
            @pl.when(step == 0)
            def _():
                ref[...] = o

            @pl.when(step > 0)
            def _():
                ref[...] += o

        for ref, o in zip(acc_refs, accs):
            accumulate(ref, o)

    rev = lambda i: ntile - 1 - i
    zspecs, pspecs = _prep_specs(z, mu, pars, rev)
    return pl.pallas_call(
        body,
        grid=(ntile,),
        in_specs=zspecs + [pl.BlockSpec((PREP_TILE, RWKV_WIDTH), lambda i: (rev(i), 0))] * nct + pspecs,
        out_specs=[pl.BlockSpec((PREP_TILE, width), lambda i: (rev(i), 0))]
        + [pl.BlockSpec(s, lambda i: (0, 0)) for s in acc_shapes],
        out_shape=[jax.ShapeDtypeStruct((rows, width), BF16)] + [jax.ShapeDtypeStruct(s, F32) for s in acc_shapes],
        scratch_shapes=[pltpu.VMEM((SUBLANES, width), F32)],
        compiler_params=_cparams(("arbitrary",), 56),
        name="rwkv_prep_bwd",
    )(z, z, *cts, mu, *pars)


def _post_fn(y, r, k, v, g, ln_w, ln_b, r_k, esum):
    mean = _head_sums(y, esum) * (1.0 / HEAD_DIM)
    yc = y - mean
    var = _head_sums(yc * yc, esum) * (1.0 / HEAD_DIM)
    yn = yc * lax.rsqrt(var + GN_EPS) * ln_w + ln_b
    bonus = _head_sums(r * k * r_k, esum) * v
    return (yn + bonus) * g


def _post_out_proj_norm(y, r, k, v, g, attn_out, x, pars, w_attn_rows, w_rwkv_rows, g_norm):
    npar = len(pars)

    def fn(yv, rv, kv, vv, gv, av, xv, *rest):
        wa, wr, gn = rest[npar:]
        rw = _post_fn(yv, rv, kv, vv, gv, *rest[:npar])
        x1 = xv + _bdot_raw(av, wa, NN) + _bdot_raw(rw, wr, NN)
        return rw, x1, _rms_fn(x1, gn)

    d = x.shape[1]
    return _rowwise(fn, [y, r, k, v, g, attn_out, x], [*pars, w_attn_rows, w_rwkv_rows, g_norm],
                    [(RWKV_WIDTH, BF16), (d, F32), (d, BF16)], [],
                    tile=FUSED_TILE, nsub=FUSED_CHUNKS, name="rwkv_post_out_proj_mlp_norm")


def _out_proj_bwd_post_bwd(dx1, y, r, k, v, g, pars, w_attn_rows, w_rwkv_rows):
    npar = len(pars)

    def fn(dxv, yv, rv, kv, vv, gv, *rest):
        wa, wr = rest[npar:]
        esum = rest[npar - 1]
        d_attn = _bdot_raw(dxv, wa, NT)
        d_rw = _bdot_raw(dxv, wr, NT)
        _, vjp = jax.vjp(lambda *a: _post_fn(*a, esum), yv, rv, kv, vv, gv, *rest[:npar - 1])
        return (d_attn, *vjp(d_rw))

    return _rowwise(fn, [dx1, y, r, k, v, g], [*pars, w_attn_rows, w_rwkv_rows],
                    [(RWKV_WIDTH, BF16)] * 2 + [(RWKV_WIDTH, F32)] * 4,
                    [p.shape for p in pars[:-1]], tile=FUSED_TILE, nsub=FUSED_CHUNKS, name="out_proj_bwd_rwkv_post_bwd")


def _tri_inverses(ms, tick=lambda: None):
    n = ms[0].shape[0]
    row = lax.broadcasted_iota(jnp.int32, (n, n), 0)
    col = lax.broadcasted_iota(jnp.int32, (n, n), 1)
    eye = jnp.where(row == col, 1.0, 0.0)
    t_inv = [eye + m for m in ms]
    power = [_bdot_raw(m, m, NN) for m in ms]
    steps = int(math.log2(n // 2)) - 1
    for step in range(steps):
        if step < steps - 1:
            both = [_bdot_raw(jnp.concatenate([p, t], axis=0), p, NN) for p, t in zip(power, t_inv)]
            power = [b[:n] for b in both]
            t_inv = [t + b[n:] for t, b in zip(t_inv, both)]
        else:
            t_inv = [t + _bdot_raw(t, p, NN) for t, p in zip(t_inv, power)]
        tick()
    return t_inv


def _tri_solve_bwd(res, dus):
    t_inv, us = res
    dxs = tuple(_bdot_raw(t, du, TN) for t, du in zip(t_inv, dus))
    dms = tuple(_bdot_raw(dx, u, NT) for dx, u in zip(dxs, us))
    return dms, dxs


@functools.partial(jax.custom_vjp, nondiff_argnums=(3, 4))
def _kept_bdot_c(a, b, kept, ca, cb):
    return kept.astype(F32)


def _kept_bdot_c_fwd(a, b, kept, ca, cb):
    return kept.astype(F32), (a, b, kept)


def _kept_bdot_c_bwd(ca, cb, res, ct):
    a, b, kept = res
    return (*_bdot_c_bwd(ca, cb, (a, b), ct), jnp.zeros_like(kept))


_kept_bdot_c.defvjp(_kept_bdot_c_fwd, _kept_bdot_c_bwd)


def _kept_bdot(a, b, kept, dn):
    return _kept_bdot_c(a, b, kept, dn[0][0][0], dn[0][1][0])


@jax.custom_vjp
def _kept_tri_solve(ms, xs, t_inv, us):
    return tuple(u.astype(F32) for u in us)


def _kept_tri_solve_fwd(ms, xs, t_inv, us):
    return tuple(u.astype(F32) for u in us), (t_inv, us)


def _kept_tri_solve_bwd(res, dus):
    t_inv, us = res
    dms, dxs = _tri_solve_bwd(res, dus)
    return dms, dxs, tuple(jnp.zeros_like(t) for t in t_inv), tuple(jnp.zeros_like(u) for u in us)


_kept_tri_solve.defvjp(_kept_tri_solve_fwd, _kept_tri_solve_bwd)


def _chunk_fn(ss, rs, lws, ks, vs, kks, als, kept=None, tick=lambda: None):
    c = rs[0].shape[0]
    n = 2 * c
    row = lax.broadcasted_iota(jnp.int32, (n, n), 0)
    col = lax.broadcasted_iota(jnp.int32, (n, n), 1)
    incl = (row % c) >= (col % c)
    strict = (row % c) > (col % c)
    lane = lax.broadcasted_iota(jnp.int32, (1, LANES), 1)
    m_lo = jnp.where(lane < HEAD_DIM, 1.0, 0.0)
    m_hi = 1.0 - m_lo

    def stack(a):
        return jnp.concatenate([a * m_lo, a * m_hi], axis=0)

    cums = [_cumsum_rows(lw) for lw in lws]
    totals = [jnp.sum(lw, axis=0, keepdims=True) for lw in lws]
    bs = [kk * al for kk, al in zip(kks, als)]
    grows = [jnp.exp(-cum) for cum in cums]
    a_s = [stack(-kk * jnp.exp(cum - lw)) for kk, cum, lw in zip(kks, cums, lws)]
    b_s = [stack(b * g) for b, g in zip(bs, grows)]
    k_s = [stack(k * g) for k, g in zip(ks, grows)]
    r_s = [stack(r * jnp.exp(cum)) for r, cum in zip(rs, cums)]
    v_s = [stack(v) for v in vs]
    tick()
    pair = lambda p, q: jnp.concatenate([p, q], axis=0)
    ar_s = [pair(a, r) for a, r in zip(a_s, r_s)]
    if kept is None:
        products = [_bdot_raw(ar, pair(b, k), NT) for ar, b, k in zip(ar_s, b_s, k_s)]
    else:
        products = [_kept_bdot(ar, pair(b, k), kp, NT) for ar, b, k, kp in zip(ar_s, b_s, k_s, kept[0])]
    tick()
    blocks = [_quarters(p) for p in products]
    m_ab = [jnp.where(strict, q[0], 0.0) for q in blocks]
    m_ak = [jnp.where(strict, q[1], 0.0) for q in blocks]
    m_rb = [jnp.where(incl, q[2], 0.0) for q in blocks]
    m_rk = [jnp.where(incl, q[3], 0.0) for q in blocks]
    from_state = [_halves(_bdot(ar, s, NT)) for ar, s in zip(ar_s, ss)]
    tick()
    from_v = [_halves(_bdot(pair(mk, mr), v)) for mk, mr, v in zip(m_ak, m_rk, v_s)]
    tick()
    x = tuple(fs[0] + fv[0] for fs, fv in zip(from_state, from_v))
    if kept is None:
        t_inv = _tri_inverses(m_ab, tick)
        u = [_bdot_raw(t, xx, NN) for t, xx in zip(t_inv, x)]
        tick()
    else:
        u = _kept_tri_solve(tuple(m_ab), x, kept[1], kept[2])
    y = [_fold_rows(fs[1] + _bdot(mb, uu) + fv[1]) for fs, mb, uu, fv in zip(from_state, m_rb, u, from_v)]
    tick()
    tails = [jnp.exp(tot - cum) for tot, cum in zip(totals, cums)]
    s_new = [s * jnp.exp(tot) + _bdot(pair(uu, v), pair(stack(b * tl), stack(k * tl)), TN)
             for s, tot, uu, b, tl, v, k in zip(ss, totals, u, bs, tails, v_s, ks)]
    if kept is None:
        keep = lambda vals: tuple(v.astype(BF16) for v in vals)
        return tuple(y), tuple(s_new), (keep(products), keep(t_inv), keep(u))
    return tuple(y), tuple(s_new)


def _chains(bsz, npair):
    return [(b, p, slice(p * LANES, (p + 1) * LANES)) for b in range(bsz) for p in range(npair)]


def _hosted_exchange(first, last, ex_in, ex_out, sems, scatter):
    if not ex_in:
        return

    @pl.when(first)
    def _():
        _exchange_start(_exchange_copies(ex_in, ex_out, *sems, scatter, arrivals=False))

    @pl.when(last)
    def _():
        _exchange_wait(_exchange_copies(ex_in, ex_out, *sems, scatter, arrivals=True))


def _rwkv_attn_fwd(r, lw, k, v, kk, al, z, sink_rows, hosted=(), scatter=False):
    bsz, t, w = r.shape
    npair, nchunk = w // LANES, t // CHUNK
    nb = t // BLOCK
    assert bsz * nb == nchunk
    chains = _chains(bsz, npair)
    nex = len(hosted)
    apair = ATTN_WIDTH // LANES

    def body(*refs):
        r_ref, lw_ref, k_ref, v_ref, kk_ref, al_ref = refs[:6]
        q_ref, kp_ref, kc_ref, vp_ref, vc_ref, sink_ref = refs[6:12]
        ex_in = refs[12:12 + nex]
        y_ref, sall_ref, prod_ref, tinv_ref, u_ref, o_ref, p_ref, ps_ref = refs[12 + nex:20 + nex]
        ex_out = refs[20 + nex:20 + 2 * nex]
        s_scr = refs[20 + 2 * nex]
        step = pl.program_id(0)

        @pl.when(step == 0)
        def _():
            s_scr[...] = jnp.zeros_like(s_scr)

        _hosted_exchange(step == 0, step == nchunk - 1, ex_in, ex_out, refs[21 + 2 * nex:], scatter)
        qs = tuple(q_ref[0, :, pair * LANES:(pair + 1) * LANES] for pair in range(apair))
        attn = []
        stages = _attn_block_stages(qs, kp_ref[0], kc_ref[0], vp_ref[0], vc_ref[0], _sink_values(sink_ref),
                                    step % nb == 0, attn)
        ss = tuple(s_scr[i] for i in range(len(chains)))
        for i, s in enumerate(ss):
            sall_ref[0, i] = s
        ys, s_new, kept = _chunk_fn(ss, *[tuple(ref[b, :, cols] for b, _, cols in chains)
                                          for ref in (r_ref, lw_ref, k_ref, v_ref, kk_ref, al_ref)],
                                    tick=lambda: next(stages, None))
        for _ in stages:
            pass
        for i, (b, _, cols) in enumerate(chains):
            y_ref[b, :, cols] = ys[i]
            s_scr[i] = s_new[i]
            prod_ref[0, i], tinv_ref[0, i], u_ref[0, i] = kept[0][i], kept[1][i], kept[2][i]
        outs, probs, psinks = attn
        for pair in range(apair):
            o_ref[0, :, pair * LANES:(pair + 1) * LANES] = outs[pair].astype(o_ref.dtype)
        for h, p in enumerate(probs):
            p_ref[0, 0, h] = p
        ps_ref[0] = _head_columns(psinks)

    spec = pl.BlockSpec((bsz, CHUNK, w), lambda c: (0, c, 0))
    hbm = pl.BlockSpec(memory_space=pltpu.HBM)
    per_chunk = lambda n: pl.BlockSpec((1, len(chains), n, n), lambda c: (c, 0, 0, 0))
    kept_shape = lambda n: jax.ShapeDtypeStruct((nchunk, len(chains), n, n), BF16)
    kcol, vcol = ATTN_WIDTH // KV_WIDTH, ATTN_WIDTH // KV_WIDTH + 1
    before = lambda c: jnp.maximum(c % nb - 1, 0)
    outs = pl.pallas_call(
        body,
        grid=(nchunk,),
        in_specs=[spec] * 6
        + [pl.BlockSpec((1, BLOCK, ATTN_WIDTH), lambda c: (c // nb, c % nb, 0)),
           pl.BlockSpec((1, BLOCK, KV_WIDTH), lambda c: (c // nb, before(c), kcol)),
           pl.BlockSpec((1, BLOCK, KV_WIDTH), lambda c: (c // nb, c % nb, kcol)),
           pl.BlockSpec((1, BLOCK, KV_WIDTH), lambda c: (c // nb, before(c), vcol)),
           pl.BlockSpec((1, BLOCK, KV_WIDTH), lambda c: (c // nb, c % nb, vcol)),
           pl.BlockSpec(sink_rows.shape, lambda c: (0, 0))]
        + [hbm] * nex,
        out_specs=[spec, per_chunk(LANES), per_chunk(4 * CHUNK), per_chunk(2 * CHUNK), per_chunk(2 * CHUNK),
                   pl.BlockSpec((1, BLOCK, ATTN_WIDTH), lambda c: (c // nb, c % nb, 0)),
                   pl.BlockSpec((1, 1, N_ATTN_HEADS, BLOCK, 2 * BLOCK), lambda c: (c // nb, c % nb, 0, 0, 0)),
                   pl.BlockSpec((1, BLOCK, LANES), lambda c: (c // nb, c % nb, 0))]
        + [hbm] * nex,
        out_shape=[jax.ShapeDtypeStruct((bsz, t, w), F32),
                   jax.ShapeDtypeStruct((nchunk, len(chains), LANES, LANES), F32),
                   kept_shape(4 * CHUNK), kept_shape(2 * CHUNK), kept_shape(2 * CHUNK),
                   jax.ShapeDtypeStruct((bsz, t, ATTN_WIDTH), BF16),
                   jax.ShapeDtypeStruct((bsz, nb, N_ATTN_HEADS, BLOCK, 2 * BLOCK), F32),
                   jax.ShapeDtypeStruct((bsz, t, LANES), F32)]
        + _exchange_out_shapes(hosted, scatter),
        scratch_shapes=[pltpu.VMEM((len(chains), LANES, LANES), F32)] + (_exchange_scratch(nex) if nex else []),
        compiler_params=_cparams(("arbitrary",), 48),
        name="rwkv_chunk_swa_fwd",
    )(r, lw, k, v, kk, al, z, z, z, z, z, sink_rows, *hosted)
    return outs[0], outs[1:5], outs[5:8], outs[8:]


def _rwkv_attn_bwd(r, lw, k, v, kk, al, from_fwd, dy, z, dout, probs, psinks, hosted=(), scatter=False):
    bsz, t, w = r.shape
    npair, nchunk = w // LANES, t // CHUNK
    nb = t // BLOCK
    assert bsz * nb == nchunk
    chains = _chains(bsz, npair)
    nex = len(hosted)
    apair = ATTN_WIDTH // LANES

    def body(*refs):
        r_ref, lw_ref, k_ref, v_ref, kk_ref, al_ref, s_ref, prod_ref, tinv_ref, u_ref, dy_ref = refs[:11]
        q_ref, kp_ref, kc_ref, vp_ref, vc_ref, do_ref, p_ref, ps_ref = refs[11:19]
        ex_in = refs[19:19 + nex]
        out_refs = refs[19 + nex:25 + nex]
        dz_ref, dsink_ref = refs[25 + nex:27 + nex]
        ex_out = refs[27 + nex:27 + 2 * nex]
        ds_scr, carry = refs[27 + 2 * nex:29 + 2 * nex]
        step = pl.program_id(0)

        @pl.when(step == 0)
        def _():
            ds_scr[...] = jnp.zeros_like(ds_scr)
            dsink_ref[...] = jnp.zeros_like(dsink_ref)

        @pl.when(step % nb == 0)
        def _():
            carry[...] = jnp.zeros_like(carry)

        _hosted_exchange(step == 0, step == nchunk - 1, ex_in, ex_out, refs[29 + 2 * nex:], scatter)
        qs = tuple(q_ref[0, :, pair * LANES:(pair + 1) * LANES] for pair in range(apair))
        dos = tuple(do_ref[0, :, pair * LANES:(pair + 1) * LANES] for pair in range(apair))
        attn = []
        stages = _attn_block_bwd_stages(qs, kp_ref[0], kc_ref[0], vp_ref[0], vc_ref[0], dos,
                                        [p_ref[0, 0, h] for h in range(N_ATTN_HEADS)], step % nb == nb - 1, attn)
        ss = tuple(s_ref[0, i] for i in range(len(chains)))
        kept = tuple(tuple(ref[0, i] for i in range(len(chains))) for ref in (prod_ref, tinv_ref, u_ref))
        _, vjp = jax.vjp(functools.partial(_chunk_fn, kept=kept), ss,
                         *[tuple(ref[b, :, cols] for b, _, cols in chains)
                           for ref in (r_ref, lw_ref, k_ref, v_ref, kk_ref, al_ref)])
        rules = [0]

        def one_stage_per_round():
            rules[0] += 1
            if rules[0] % len(chains) == 0:
                next(stages, None)

        _BESIDE_BACKWARD[0] = one_stage_per_round
        try:
            grads = vjp((tuple(dy_ref[b, :, cols].astype(F32) for b, _, cols in chains),
                         tuple(ds_scr[i] for i in range(len(chains)))))
        finally:
            _BESIDE_BACKWARD[0] = None
        for _ in stages:
            pass
        for i, (b, _, cols) in enumerate(chains):
            ds_scr[i] = grads[0][i]
            for ref, gval in zip(out_refs, grads[1:]):
                ref[b, :, cols] = gval[i]
        dqs, dkp, dkc, dvp, dvc, deltas = attn
        for pair in range(apair):
            dz_ref[0, :, pair * LANES:(pair + 1) * LANES] = dqs[pair].astype(dz_ref.dtype)
        dsink_ref[...] -= jnp.sum(ps_ref[0] * _head_columns(deltas), axis=0, keepdims=True)
        dz_ref[0, :, ATTN_WIDTH:ATTN_WIDTH + KV_WIDTH] = (dkc + carry[0]).astype(dz_ref.dtype)
        dz_ref[0, :, ATTN_WIDTH + KV_WIDTH:QKV_WIDTH] = (dvc + carry[1]).astype(dz_ref.dtype)
        carry[0] = dkp
        carry[1] = dvp

    spec = pl.BlockSpec((bsz, CHUNK, w), lambda c: (0, nchunk - 1 - c, 0))
    per_chunk = lambda n: pl.BlockSpec((1, len(chains), n, n), lambda c: (nchunk - 1 - c, 0, 0, 0))
    hbm = pl.BlockSpec(memory_space=pltpu.HBM)
    kcol, vcol = ATTN_WIDTH // KV_WIDTH, ATTN_WIDTH // KV_WIDTH + 1
    seq_of = lambda c: c // nb
    blk = lambda c: nb - 1 - c % nb
    before = lambda c: jnp.maximum(blk(c) - 1, 0)
    outs = pl.pallas_call(
        body,
        grid=(nchunk,),
        in_specs=[spec] * 6 + [per_chunk(LANES), per_chunk(4 * CHUNK), per_chunk(2 * CHUNK), per_chunk(2 * CHUNK), spec]
        + [pl.BlockSpec((1, BLOCK, ATTN_WIDTH), lambda c: (seq_of(c), blk(c), 0)),
           pl.BlockSpec((1, BLOCK, KV_WIDTH), lambda c: (seq_of(c), before(c), kcol)),
           pl.BlockSpec((1, BLOCK, KV_WIDTH), lambda c: (seq_of(c), blk(c), kcol)),
           pl.BlockSpec((1, BLOCK, KV_WIDTH), lambda c: (seq_of(c), before(c), vcol)),
           pl.BlockSpec((1, BLOCK, KV_WIDTH), lambda c: (seq_of(c), blk(c), vcol)),
           pl.BlockSpec((1, BLOCK, ATTN_WIDTH), lambda c: (seq_of(c), blk(c), 0)),
           pl.BlockSpec((1, 1, N_ATTN_HEADS, BLOCK, 2 * BLOCK), lambda c: (seq_of(c), blk(c), 0, 0, 0)),
           pl.BlockSpec((1, BLOCK, LANES), lambda c: (seq_of(c), blk(c), 0))]
        + [hbm] * nex,
        out_specs=[spec] * 6
        + [pl.BlockSpec((1, BLOCK, QKV_WIDTH), lambda c: (seq_of(c), blk(c), 0)),
           pl.BlockSpec((1, LANES), lambda c: (0, 0))]
        + [hbm] * nex,
        out_shape=[jax.ShapeDtypeStruct((bsz, t, w), F32)] * 6
        + [jax.ShapeDtypeStruct((bsz, t, QKV_WIDTH), BF16), jax.ShapeDtypeStruct((1, LANES), F32)]
        + _exchange_out_shapes(hosted, scatter),
        scratch_shapes=[pltpu.VMEM((len(chains), LANES, LANES), F32), pltpu.VMEM((2, BLOCK, KV_WIDTH), F32)]
        + (_exchange_scratch(nex) if nex else []),
        compiler_params=_cparams(("arbitrary",), 48),
        name="rwkv_chunk_swa_bwd",
    )(r, lw, k, v, kk, al, *from_fwd, dy, z, z, z, z, z, dout, probs, psinks, *hosted)
    return outs[:6], outs[6:8], outs[8:]


def _alibi_slope(head):
    return 2.0 ** (-8.0 * (head + 1) / N_ATTN_HEADS)


def _attn_setup(first):
    row = lax.broadcasted_iota(jnp.int32, (BLOCK, 2 * BLOCK), 0)
    col = lax.broadcasted_iota(jnp.int32, (BLOCK, 2 * BLOCK), 1)
    lane = lax.broadcasted_iota(jnp.int32, (1, LANES), 1)
    halves = [jnp.where((lane // HEAD_DIM) == half, 1.0, 0.0) for half in range(2)]
    srow = lax.broadcasted_iota(jnp.int32, (LANES, LANES), 0)
    scol = lax.broadcasted_iota(jnp.int32, (LANES, LANES), 1)
    swap = jnp.where((srow + HEAD_DIM) % LANES == scol, 1.0, 0.0)
    dist = row - col + BLOCK
    valid = jnp.logical_and(jnp.logical_and(dist >= 0, dist < BLOCK),
                            jnp.logical_or(col >= BLOCK, jnp.logical_not(first)))
    return halves, swap, dist.astype(F32), valid, HEAD_DIM ** -0.5


def _attn_keys_values(kp, kc, vp, vc, swap, npair):
    stored = (jnp.concatenate([kp, kc], axis=0), jnp.concatenate([vp, vc], axis=0))
    swapped = tuple(_bdot_raw(t, swap, NN) for t in stored)
    heads = [(pair, half) for pair in range(npair) for half in range(2)]
    return heads, [half == pair // 2 for pair, half in heads], stored, swapped


def _attn_block_stages(qs, kp, kc, vp, vc, sinks, first, result):
    halves, swap, dist, valid, scale = _attn_setup(first)
    heads, as_stored, stored, swapped = _attn_keys_values(kp, kc, vp, vc, swap, len(qs))
    kv = [stored if own else swapped for own in as_stored]
    slopes = [_alibi_slope(2 * pair + half) for pair, half in heads]
    qa = [qs[pair] * halves[half] for pair, half in heads]
    yield
    s = [jnp.where(valid, _bdot_raw(q, t[0], NT) * scale - sl * dist, NEG_INF) for q, t, sl in zip(qa, kv, slopes)]
    yield
    mx = [jnp.maximum(jnp.max(a, axis=-1, keepdims=True), sk) for a, sk in zip(s, sinks)]
    yield
    e = [jnp.exp(a - m) for a, m in zip(s, mx)]
    yield
    es = [jnp.exp(sk - m) for sk, m in zip(sinks, mx)]
    inv = [1.0 / (jnp.sum(a, axis=-1, keepdims=True) + b) for a, b in zip(e, es)]
    yield
    probs = [a * i for a, i in zip(e, inv)]
    yield
    o = [_bdot_raw(p, t[1], NN) for p, t in zip(probs, kv)]
    yield
    outs = tuple(o[2 * pair] * halves[0] + o[2 * pair + 1] * halves[1] for pair in range(len(qs)))
    result.extend([outs, probs, [b * i for b, i in zip(es, inv)]])


def _attn_block_bwd_stages(qs, kp, kc, vp, vc, dos, probs, first, result):
    halves, swap, _, _, scale = _attn_setup(first)
    heads, as_stored, stored, swapped = _attn_keys_values(kp, kc, vp, vc, swap, len(qs))
    kv = [stored if own else swapped for own in as_stored]
    qa = [qs[pair] * halves[half] for pair, half in heads]
    do = [dos[pair] * halves[half] for pair, half in heads]
    yield
    dp = [_bdot_raw(d, t[1], NT) for d, t in zip(do, kv)]
    yield
    delta = [jnp.sum(p * d, axis=-1, keepdims=True) for p, d in zip(probs, dp)]
    yield
    ds = [p * (d - dl) for p, d, dl in zip(probs, dp, delta)]
    yield
    dq = [_bdot_raw(g, t[0], NN) * (scale * halves[half]) for g, t, (_, half) in zip(ds, kv, heads)]
    yield
    dk = [_bdot_raw(g, q, TN) * scale for g, q in zip(ds, qa)]
    yield
    dv = [_bdot_raw(p, d, TN) for p, d in zip(probs, do)]
    yield
    dqs = tuple(dq[2 * pair] + dq[2 * pair + 1] for pair in range(len(qs)))

    def total(parts):
        direct = sum(g for g, own in zip(parts, as_stored) if own)
        return direct + _bdot_raw(sum(g for g, own in zip(parts, as_stored) if not own), swap, NN)

    dk_all, dv_all = total(dk), total(dv)
    result.extend([dqs, dk_all[:BLOCK], dk_all[BLOCK:], dv_all[:BLOCK], dv_all[BLOCK:], delta])


def _sink_values(sink_ref):
    return [jnp.max(sink_ref[h:h + 1, :], axis=-1, keepdims=True) for h in range(N_ATTN_HEADS)]


def _head_columns(cols):
    lane = lax.broadcasted_iota(jnp.int32, (1, LANES), 1)
    return sum(c * jnp.where(lane == h, 1.0, 0.0) for h, c in enumerate(cols))


def _exchange_out_shapes(arrays, scatter):
    return [jax.ShapeDtypeStruct((N_DEV,) + (a.shape[1:] if scatter else a.shape), a.dtype) for a in arrays]


def _exchange_scratch(n):
    return [pltpu.SemaphoreType.DMA((n, N_DEV - 1)), pltpu.SemaphoreType.DMA((n, N_DEV - 1)),
            pltpu.SemaphoreType.DMA((n,))]


def _exchange_copies(ins, outs, send_sems, recv_sems, local_sems, scatter, arrivals=True):
    x, y, c = lax.axis_index("x"), lax.axis_index("y"), lax.axis_index("c")
    me = 4 * x + 2 * y + c
    copies = []
    for i in range(len(ins)):
        own = pltpu.make_async_copy(ins[i].at[me] if scatter else ins[i], outs[i].at[me], local_sems.at[i])
        copies.append((own, None, True))
        for d in range(1, N_DEV):
            px = 1 - x if d & 4 else x
            py = 1 - y if d & 2 else y
            pc = 1 - c if d & 1 else c
            peer = 4 * px + 2 * py + pc
            src = ins[i].at[peer] if scatter else ins[i]
            send = pltpu.make_async_remote_copy(src, outs[i].at[me], send_sems.at[i, d - 1], recv_sems.at[i, d - 1],
                                                device_id=(px, py, pc), device_id_type=MESH)
            recv = pltpu.make_async_remote_copy(src, outs[i].at[peer], send_sems.at[i, d - 1], recv_sems.at[i, d - 1],
                                                device_id=(px, py, pc), device_id_type=MESH) if arrivals else None
            copies.append((send, recv, False))
    return copies


def _exchange_start(copies):
    for send, _, _ in copies:
        send.start()


def _exchange_wait(copies):
    for send, recv, local in copies:
        if local:
            send.wait()
        else:
            send.wait_send()
            recv.wait_recv()


def _gather_two_level(arrays, name):
    n = len(arrays)

    def body(*refs):
        ins, outs = refs[:n], refs[n:2 * n]
        send_sems, recv_sems, local_sems = refs[2 * n:]
        x, y, c = lax.axis_index("x"), lax.axis_index("y"), lax.axis_index("c")
        index = lambda px, py, pc: 4 * px + 2 * py + pc
        sibling = (x, y, 1 - c)
        chips = [(1 - x, y), (x, 1 - y), (1 - x, 1 - y)]

        def copy(i, k, block, to, src=None):
            slot = outs[i].at[index(*block)]
            return pltpu.make_async_remote_copy(slot if src is None else src, slot, send_sems.at[i, k],
                                                recv_sems.at[i, k], device_id=to, device_id_type=MESH)

        local, sends = [], []
        for i in range(n):
            own = pltpu.make_async_copy(ins[i], outs[i].at[index(x, y, c)], local_sems.at[i])
            own.start()
            local.append(own)
            first = [copy(i, 0, (x, y, c), sibling, src=ins[i])]
            first += [copy(i, 1 + j, (x, y, c), (*chip, c), src=ins[i]) for j, chip in enumerate(chips)]
            for cp in first:
                cp.start()
            sends += first
        for i in range(n):
            for j, chip in enumerate(chips):
                copy(i, 1 + j, (*chip, c), (x, y, c)).wait_recv()
                onward = copy(i, 4 + j, (*chip, c), sibling)
                onward.start()
                sends.append(onward)
        for i in range(n):
            copy(i, 0, sibling, (x, y, c)).wait_recv()
            for j, chip in enumerate(chips):
                copy(i, 4 + j, (*chip, 1 - c), (x, y, c)).wait_recv()
        for cp in sends:
            cp.wait_send()
        for cp in local:
            cp.wait()

    hbm = pl.BlockSpec(memory_space=pltpu.HBM)
    return pl.pallas_call(
        body,
        in_specs=[hbm] * n,
        out_specs=[hbm] * n,
        out_shape=_exchange_out_shapes(arrays, False),
        scratch_shapes=_exchange_scratch(n),
        name=name,
    )(*arrays)


def _exchange(arrays, *, scatter, name):
    n = len(arrays)

    def body(*refs):
        copies = _exchange_copies(refs[:n], refs[n:2 * n], *refs[2 * n:], scatter)
        _exchange_start(copies)
        _exchange_wait(copies)

    hbm = pl.BlockSpec(memory_space=pltpu.HBM)
    return pl.pallas_call(
        body,
        in_specs=[hbm] * n,
        out_specs=[hbm] * n,
        out_shape=_exchange_out_shapes(arrays, scatter),
        scratch_shapes=_exchange_scratch(n),
        name=name,
    )(*arrays)


def _adamw(parts, w, m, v, name):
    rows, cols = w.shape
    tr = _pick(rows, (256, 128, 64, 8))
    c1 = 1.0 / (1.0 - ADAM_B1 ** ADAM_STEP)
    c2 = 1.0 / (1.0 - ADAM_B2 ** ADAM_STEP)

    def body(p_ref, w_ref, m_ref, v_ref, g_ref, d_ref, mo_ref, vo_ref):
        g = p_ref[0].astype(F32)
        for s in range(1, N_DEV):
            g = g + p_ref[s].astype(F32)
        mn = ADAM_B1 * m_ref[...] + (1.0 - ADAM_B1) * g
        vn = ADAM_B2 * v_ref[...] + (1.0 - ADAM_B2) * (g * g)
        g_ref[...] = g
        mo_ref[...] = mn
        vo_ref[...] = vn
        d_ref[...] = -ADAM_LR * ((mn * c1) / (jnp.sqrt(vn * c2) + ADAM_EPS) + ADAM_WD * w_ref[...])

    spec = pl.BlockSpec((tr, cols), lambda i: (i, 0))
    return pl.pallas_call(
        body,
        grid=(rows // tr,),
        in_specs=[pl.BlockSpec((N_DEV, tr, cols), lambda i: (0, i, 0)), spec, spec, spec],
        out_specs=[spec] * 4,
        out_shape=[jax.ShapeDtypeStruct((rows, cols), F32)] * 4,
        compiler_params=_cparams(("parallel",), 48),
        name=name,
    )(parts, w, m, v)


_VECTOR_PARAMS = ("attn_norm_g", "attn_sinks", "rwkv_mu", "w0", "a0", "k_k", "k_a", "r_k", "ln_x_w", "ln_x_b",
                  "mlp_norm_g", "final_norm_g")
_WEIGHT_NAMES = ("attn_norm_g", "w_in", "attn_sinks", "rwkv_mu", "w0", "w2", "a0", "a2", "g2", "k_k", "k_a", "r_k",
                 "ln_x_w", "ln_x_b", "w_out", "mlp_norm_g", "w_up", "w_down", "final_norm_g")


def _pack_vectors(vals):
    pieces = []
    for name in _VECTOR_PARAMS:
        flat = vals[name].reshape(1, -1)
        pad = (-flat.shape[1]) % LANES
        pieces.append(jnp.pad(flat, ((0, 0), (0, pad))) if pad else flat)
    return jnp.concatenate(pieces, axis=1)


def _unpack_vectors(packed, like):
    out, col = {}, 0
    for name in _VECTOR_PARAMS:
        size = like[name].size
        out[name] = packed[0, col:col + size].reshape(like[name].shape)
        col += size + (-size) % LANES
    return out


def kernel(x, attn_norm_g, w_in, attn_sinks, rwkv_mu, w0, w2, a0, a2, g2, k_k, k_a, r_k, ln_x_w, ln_x_b, w_out, mlp_norm_g, w_up, w_down, final_norm_g, loss_target, m_attn_norm_g, m_w_in, m_attn_sinks, m_rwkv_mu, m_w0, m_w2, m_a0, m_a2, m_g2, m_k_k, m_k_a, m_r_k, m_ln_x_w, m_ln_x_b, m_w_out, m_mlp_norm_g, m_w_up, m_w_down, m_final_norm_g, v_attn_norm_g, v_w_in, v_attn_sinks, v_rwkv_mu, v_w0, v_w2, v_a0, v_a2, v_g2, v_k_k, v_k_a, v_r_k, v_ln_x_w, v_ln_x_b, v_w_out, v_mlp_norm_g, v_w_up, v_w_down, v_final_norm_g):
    weights = dict(attn_norm_g=attn_norm_g, w_in=w_in, attn_sinks=attn_sinks, rwkv_mu=rwkv_mu, w0=w0, w2=w2, a0=a0,
                   a2=a2, g2=g2, k_k=k_k, k_a=k_a, r_k=r_k, ln_x_w=ln_x_w, ln_x_b=ln_x_b, w_out=w_out,
                   mlp_norm_g=mlp_norm_g, w_up=w_up, w_down=w_down, final_norm_g=final_norm_g)
    mom1 = dict(attn_norm_g=m_attn_norm_g, w_in=m_w_in, attn_sinks=m_attn_sinks, rwkv_mu=m_rwkv_mu, w0=m_w0, w2=m_w2,
                a0=m_a0, a2=m_a2, g2=m_g2, k_k=m_k_k, k_a=m_k_a, r_k=m_r_k, ln_x_w=m_ln_x_w, ln_x_b=m_ln_x_b,
                w_out=m_w_out, mlp_norm_g=m_mlp_norm_g, w_up=m_w_up, w_down=m_w_down, final_norm_g=m_final_norm_g)
    mom2 = dict(attn_norm_g=v_attn_norm_g, w_in=v_w_in, attn_sinks=v_attn_sinks, rwkv_mu=v_rwkv_mu, w0=v_w0, w2=v_w2,
                a0=v_a0, a2=v_a2, g2=v_g2, k_k=v_k_k, k_a=v_k_a, r_k=v_r_k, ln_x_w=v_ln_x_w, ln_x_b=v_ln_x_b,
                w_out=v_w_out, mlp_norm_g=v_mlp_norm_g, w_up=v_w_up, w_down=v_w_down, final_norm_g=v_final_norm_g)
    bsz, seq, d_model = x.shape
    rows = bsz * seq
    d_in = N_DEV * w_in.shape[2]
    d_ff = N_DEV * w_up.shape[2]

    gathered = _gather_two_level([w_in[0].astype(BF16), w2[0], a2[0], g2[0]], name="gather_in_weights")
    cols_first = lambda a: a.transpose(1, 0, 2).reshape(a.shape[1], -1)
    w_in_f = cols_first(gathered[0])
    w_attn, w_rw = w_in_f[:, :QKV_WIDTH], w_in_f[:, QKV_WIDTH:]
    w2_f, a2_f, g2_f = cols_first(gathered[1]), cols_first(gathered[2]), cols_first(gathered[3])
    lora = w2_f.shape[0]
    w2p = jnp.concatenate([w2_f, jnp.zeros_like(a2_f)], axis=0)
    a2p = jnp.concatenate([jnp.zeros_like(w2_f), a2_f], axis=0)

    esum = _head_sum_matrix()
    sink_rows = jnp.broadcast_to(attn_sinks.reshape(N_ATTN_HEADS, 1), (N_ATTN_HEADS, LANES))
    prep_pars = [w0, w2p, a0, a2p, g2_f, k_k, k_a, esum]
    post_pars = [ln_x_w, ln_x_b, r_k, esum]

    x2d = x.reshape(rows, d_model)
    h1, z_attn, z_rw, r, lw, k, v, kk, al, gate = _norm_in_proj_prep(x2d, seq, attn_norm_g, w_attn, w_rw, rwkv_mu,
                                                                       prep_pars)
    z_attn3 = z_attn.reshape(bsz, seq, QKV_WIDTH)
    as3 = lambda a: a.reshape(bsz, seq, RWKV_WIDTH)
    y, from_fwd, (attn_out, attn_probs, attn_psinks), late = _rwkv_attn_fwd(
        as3(r), as3(lw), as3(k), as3(v), as3(kk), as3(al), z_attn3, sink_rows,
        hosted=[w_out[0].astype(BF16), w_up[0].astype(BF16), w_down[0].astype(BF16)])
    w_out_f = late[0].reshape(-1, d_model)
    w_up_f = cols_first(late[1])
    w_down_f = late[2].reshape(-1, d_model)
    y2 = y.reshape(rows, RWKV_WIDTH)
    attn_out2d = attn_out.reshape(rows, ATTN_WIDTH)
    w_out_attn, w_out_rw = w_out_f[:ATTN_WIDTH], w_out_f[ATTN_WIDTH:]
    rw_out, x1, h2 = _post_out_proj_norm(y2, r, k, v, gate, attn_out2d, x2d, post_pars, w_out_attn, w_out_rw,
                                         mlp_norm_g)

    u = _matmul(h2, w_up_f, "nn", name="mlp_up", out_dtypes=(BF16,), tm=2048)
    dx2, dx2_b, loss_vec, g_final = _down_proj_loss(u, w_down_f, x1, loss_target.reshape(rows, d_model),
                                             final_norm_g.reshape(1, d_model))

    g_w_down = _matmul(u, dx2_b, "tn", name="grad_w_down", out_dtypes=(BF16,), a_map=_relu_squared, tm=2048)
    du = _matmul(dx2_b, w_down_f, "nt", name="mlp_down_bwd", extras=(u,), out_dtypes=(BF16,), tm=2048,
                 epilogue=lambda acc, uv: (acc * (2.0 * jnp.maximum(uv.astype(F32), 0.0)),))
    g_w_up = _matmul(h2, du, "tn", name="grad_w_up", out_dtypes=(BF16,), tn=2048)
    dx1, g_mlp_norm = _proj_bwd_norm_bwd([du], [w_up_f], x1, dx2, mlp_norm_g, "mlp_up_bwd_norm_bwd")
    g_w_out = jnp.concatenate([_matmul(attn_out2d, dx1, "tn", name="grad_w_out_attn", out_dtypes=(BF16,)),
                               _matmul(rw_out, dx1, "tn", name="grad_w_out_rwkv", out_dtypes=(BF16,))], axis=0)
    d_attn_out, dy, dr_a, dk_a, dv_a, dgate, g_ln_w, g_ln_b, g_r_k = _out_proj_bwd_post_bwd(
        dx1, y2, r, k, v, gate, post_pars, w_out_attn, w_out_rw)
    by_cols = lambda a: a.reshape(a.shape[0], N_DEV, -1).transpose(1, 0, 2)
    (dr_b, dlw, dk_b, dv_b, dkk, dal), (dz_attn, g_sink_lanes), (p_w_out, p_w_up, p_w_down) = _rwkv_attn_bwd(
        as3(r), as3(lw), as3(k), as3(v), as3(kk), as3(al), from_fwd, as3(dy),
        z_attn3, d_attn_out.reshape(bsz, seq, ATTN_WIDTH), attn_probs, attn_psinks,
        hosted=[g_w_out.reshape(N_DEV, -1, d_model), by_cols(g_w_up), g_w_down.reshape(N_DEV, -1, d_model)],
        scatter=True)
    flat = lambda a: a.reshape(rows, RWKV_WIDTH)
    (dz_rw, gmu_r, gmu_k, gmu_v, gmu_wa, gmu_g, g_w0, g_w2p, g_a0, g_a2p, g_g2, g_k_k, g_k_a) = _prep_bwd(
        z_rw, seq, [dr_a, flat(dr_b), flat(dlw), dk_a, flat(dk_b), dv_a, flat(dv_b), flat(dkk), flat(dal), dgate],
        rwkv_mu, prep_pars)
    dz_attn = dz_attn.reshape(rows, QKV_WIDTH)
    g_w_in = jnp.concatenate([_matmul(h1, dz_attn, "tn", name="grad_w_in_attn", out_dtypes=(BF16,)),
                              _matmul(h1, dz_rw, "tn", name="grad_w_in_rwkv", out_dtypes=(BF16,))], axis=1)
    lora_grads = jnp.concatenate([g_w2p[:lora], g_a2p[lora:], g_g2], axis=0)
    (dx, g_attn_norm), (p_w_in, p_lora) = _proj_bwd_norm_bwd(
        [dz_attn, dz_rw], [w_attn, w_rw], x2d, dx1, attn_norm_g, "in_proj_bwd_norm_bwd",
        hosted=[by_cols(g_w_in), by_cols(lora_grads)], scatter=True)

    vec_grads = dict(attn_norm_g=g_attn_norm, attn_sinks=g_sink_lanes[0, :N_ATTN_HEADS], rwkv_mu=jnp.concatenate(
        [gmu_r, gmu_k, gmu_v, gmu_wa, gmu_g], axis=1), w0=g_w0, a0=g_a0, k_k=g_k_k, k_a=g_k_a, r_k=g_r_k,
        ln_x_w=g_ln_w, ln_x_b=g_ln_b, mlp_norm_g=g_mlp_norm, final_norm_g=g_final)
    packed = _pack_vectors(vec_grads)
    nvec = packed.shape[1]
    everyone = _exchange([jnp.concatenate([packed, loss_vec], axis=1)], scatter=False, name="gather_vector_grads")[0]
    vec_parts = everyone[:, :, :nvec]
    loss = jnp.sum(everyone[:, 0, nvec])

    grads, delta, new_m, new_v = {}, {}, {}, {}

    def update(name, part, shape2d):
        res = _adamw(part, weights[name].reshape(shape2d), mom1[name].reshape(shape2d), mom2[name].reshape(shape2d),
                     "adamw_" + name)
        for store, val in zip((grads, delta, new_m, new_v), res):
            store[name] = val.reshape(weights[name].shape)

    update("w_in", p_w_in, w_in.shape[1:])
    update("w_out", p_w_out, w_out.shape[1:])
    update("w_up", p_w_up, w_up.shape[1:])
    update("w_down", p_w_down, w_down.shape[1:])
    stack = lambda d: jnp.concatenate([d["w2"][0], d["a2"][0], d["g2"][0]], axis=0)
    lora_res = _adamw(p_lora, stack(weights), stack(mom1), stack(mom2), "adamw_lora")
    for store, val in zip((grads, delta, new_m, new_v), lora_res):
        store["w2"], store["a2"], store["g2"] = val[None, :lora], val[None, lora:2 * lora], val[None, 2 * lora:]
    vec_res = _adamw(vec_parts, _pack_vectors(weights), _pack_vectors(mom1), _pack_vectors(mom2), "adamw_vectors")
    for store, val in zip((grads, delta, new_m, new_v), vec_res):
        store.update(_unpack_vectors(val, weights))

    return (loss, dx.reshape(x.shape), *[grads[n] for n in _WEIGHT_NAMES], *[delta[n] for n in _WEIGHT_NAMES],
            *[new_m[n] for n in _WEIGHT_NAMES], *[new_v[n] for n in _WEIGHT_NAMES])
```

```python
import functools
import math

import jax
import jax.numpy as jnp
from jax import lax
from jax.experimental import pallas as pl
from jax.experimental.pallas import tpu as pltpu

F32 = jnp.float32
BF16 = jnp.bfloat16

N_DEV = 8
HEAD_DIM = 64
LANES = 128
N_ATTN_HEADS = 8
ATTN_WIDTH = 512
KV_WIDTH = 128
QKV_WIDTH = ATTN_WIDTH + 2 * KV_WIDTH
RWKV_WIDTH = 512
LORA_WA = 128
GATE_LORA = 128
RWKV_SHIFT_WIDTH = 3 * RWKV_WIDTH + LORA_WA + GATE_LORA
BLOCK = 128
CHUNK = 64
RMS_EPS = 1e-6
GN_EPS = 64e-5
L2_EPS = 1e-12
NEG_INF = -1e30
DECAY_SCALE = math.exp(-0.5)
ADAM_LR, ADAM_B1, ADAM_B2, ADAM_EPS, ADAM_WD, ADAM_STEP = 0.001, 0.9, 0.999, 1e-08, 0.01, 10

NN = (((1,), (0,)), ((), ()))
NT = (((1,), (1,)), ((), ()))
TN = (((0,), (0,)), ((), ()))
MESH = pl.DeviceIdType.MESH


def _dot(a, b, dn=NN, precision=None):
    return lax.dot_general(a, b, dn, precision=precision, preferred_element_type=F32)


def _bdot_raw(a, b, dn):
    return lax.dot_general(a.astype(BF16), b.astype(BF16), dn, preferred_element_type=F32)


@functools.partial(jax.custom_vjp, nondiff_argnums=(2, 3))
def _bdot_c(a, b, ca, cb):
    return _bdot_raw(a, b, (((ca,), (cb,)), ((), ())))


def _bdot_c_fwd(a, b, ca, cb):
    return _bdot_c(a, b, ca, cb), (a, b)


_BESIDE_BACKWARD = [None]


def _beside_backward():
    if _BESIDE_BACKWARD[0] is not None:
        _BESIDE_BACKWARD[0]()


def _bdot_c_bwd(ca, cb, res, ct):
    _beside_backward()
    a, b = res
    fa, fb = 1 - ca, 1 - cb
    da = _bdot_raw(ct, b, (((1,), (fb,)), ((), ()))) if ca == 1 else _bdot_raw(b, ct, (((fb,), (1,)), ((), ())))
    db = _bdot_raw(a, ct, (((fa,), (0,)), ((), ()))) if cb == 0 else _bdot_raw(ct, a, (((0,), (fa,)), ((), ())))
    return da, db


_bdot_c.defvjp(_bdot_c_fwd, _bdot_c_bwd)


def _bdot(a, b, dn=NN):
    return _bdot_c(a, b, dn[0][0][0], dn[0][1][0])


def _bdot_nn(a, b):
    return _bdot(a, b, NN)


def _split3(x):
    hi = x.astype(BF16)
    rest = x - hi.astype(F32)
    mid = rest.astype(BF16)
    return hi, mid, (rest - mid.astype(F32)).astype(BF16)


def _running_sum(x, dn):
    c = x.shape[0]
    row = lax.broadcasted_iota(jnp.int32, (c, c), 0)
    col = lax.broadcasted_iota(jnp.int32, (c, c), 1)
    tri = jnp.where(row >= col, 1.0, 0.0).astype(BF16)
    w = x.shape[1]
    parts = lax.dot_general(tri, jnp.concatenate(_split3(x), axis=1), dn, preferred_element_type=F32)
    return parts[:, :w] + parts[:, w:2 * w] + parts[:, 2 * w:]


@jax.custom_vjp
def _cumsum_rows(x):
    return _running_sum(x, NN)


_cumsum_rows.defvjp(lambda x: (_running_sum(x, NN), None), lambda _, ct: (_running_sum(ct, TN),))


@jax.custom_vjp
def _fold_rows(x):
    c = x.shape[0] // 2
    return x[:c] + x[c:]


_fold_rows.defvjp(lambda x: (_fold_rows(x), None), lambda _, ct: (jnp.concatenate([ct, ct], axis=0),))


@jax.custom_vjp
def _halves(x):
    n = x.shape[0] // 2
    return x[:n], x[n:]


_halves.defvjp(lambda x: (_halves(x), None), lambda _, cts: (jnp.concatenate(cts, axis=0),))


@jax.custom_vjp
def _quarters(x):
    n = x.shape[0] // 2
    return x[:n, :n], x[:n, n:], x[n:, :n], x[n:, n:]


_quarters.defvjp(lambda x: (_quarters(x), None),
                 lambda _, cts: (jnp.concatenate([jnp.concatenate(cts[:2], axis=1),
                                                  jnp.concatenate(cts[2:], axis=1)], axis=0),))


@jax.custom_vjp
def _sigmoid(x):
    return 1.0 / (1.0 + jnp.exp(-x))


def _sigmoid_fwd(x):
    s = _sigmoid(x)
    return s, s


_sigmoid.defvjp(_sigmoid_fwd, lambda s, ct: (ct * s * (1.0 - s),))


def _pick(n, cands):
    for c in cands:
        if n % c == 0:
            return c
    return n


def _cparams(sem, vmem_mb=None):
    kw = dict(dimension_semantics=sem)
    if vmem_mb is not None:
        kw["vmem_limit_bytes"] = vmem_mb * 1024 * 1024
    return pltpu.CompilerParams(**kw)


def _matmul(a, b, mode, *, name, extras=(), epilogue=None, out_dtypes=(F32,), tm=1024, tn=1024, tk=1024,
            hosted=(), scatter=False, a_map=lambda tile: tile):
    if mode == "nn":
        (M, K), (_, N) = a.shape, b.shape
    elif mode == "tn":
        (K, M), (_, N) = a.shape, b.shape
    else:
        (M, K), (N, _) = a.shape, b.shape
    tm = _pick(M, (tm, 512, 256, 128))
    tn = _pick(N, (tn, 896, 768, 512, 384, 256, 128))
    tk = _pick(K, (tk, 512, 256, 128))
    nk = K // tk
    ne, nout = len(extras), len(out_dtypes)
    if mode == "nn":
        a_spec = pl.BlockSpec((tm, tk), lambda i, j, k: (i, k))
        b_spec = pl.BlockSpec((tk, tn), lambda i, j, k: (k, j))
        dn = NN
    elif mode == "tn":
        a_spec = pl.BlockSpec((tk, tm), lambda i, j, k: (k, i))
        b_spec = pl.BlockSpec((tk, tn), lambda i, j, k: (k, j))
        dn = TN
    else:
        a_spec = pl.BlockSpec((tm, tk), lambda i, j, k: (i, k))
        b_spec = pl.BlockSpec((tn, tk), lambda i, j, k: (j, k))
        dn = NT
    o_spec = pl.BlockSpec((tm, tn), lambda i, j, k: (i, j))
    grid = (M // tm, N // tn, nk)
    nex = len(hosted)

    def body(*refs):
        a_ref, b_ref = refs[:2]
        e_refs = refs[2:2 + ne]
        ex_in = refs[2 + ne:2 + ne + nex]
        o_refs = refs[2 + ne + nex:2 + ne + nex + nout]
        ex_out = refs[2 + ne + nex + nout:2 + ne + 2 * nex + nout]
        scratch = refs[2 + ne + 2 * nex + nout:]
        kstep = pl.program_id(2)
        if nex:
            at = [pl.program_id(d) for d in range(3)]
            first = jnp.logical_and(jnp.logical_and(at[0] == 0, at[1] == 0), at[2] == 0)
            last = jnp.logical_and(jnp.logical_and(at[0] == grid[0] - 1, at[1] == grid[1] - 1), at[2] == grid[2] - 1)
            _hosted_exchange(first, last, ex_in, ex_out, scratch[-3:], scatter)

        def finish(total):
            outs = (total,) if epilogue is None else epilogue(total, *[e[...] for e in e_refs])
            for o_ref, o in zip(o_refs, outs):
                o_ref[...] = o.astype(o_ref.dtype)

        if nk == 1:
            finish(_bdot_raw(a_map(a_ref[...]), b_ref[...], dn))
            return
        acc = scratch[0]

        @pl.when(kstep == 0)
        def _():
            acc[...] = jnp.zeros_like(acc)

        acc[...] += _bdot_raw(a_map(a_ref[...]), b_ref[...], dn)

        @pl.when(kstep == nk - 1)
        def _():
            finish(acc[...])

    hbm = pl.BlockSpec(memory_space=pltpu.HBM)
    outs = pl.pallas_call(
        body,
        grid=grid,
        in_specs=[a_spec, b_spec] + [o_spec] * ne + [hbm] * nex,
        out_specs=[o_spec] * nout + [hbm] * nex,
        out_shape=[jax.ShapeDtypeStruct((M, N), dt) for dt in out_dtypes] + _exchange_out_shapes(hosted, scatter),
        scratch_shapes=([pltpu.VMEM((tm, tn), F32)] if nk > 1 else []) + (_exchange_scratch(nex) if nex else []),
        compiler_params=_cparams(("arbitrary",) * 3 if nex else ("parallel", "parallel", "arbitrary"), 56),
        name=name,
    )(a, b, *extras, *hosted)
    if nex:
        return outs[:nout], outs[nout:]
    return outs[0] if nout == 1 else outs


def _rowwise(fn, rows, pars, out_rows, out_accs, *, tile, name, nsub=1, hosted=(), scatter=False, owners=None):
    rows = [r if isinstance(r, tuple) else (r, r.shape[1], 0) for r in rows]
    R = rows[0][0].shape[0]
    tile = min(tile, R)
    chunk = tile // nsub
    ntile = R // tile
    nr, npar, nor, noa, nex = len(rows), len(pars), len(out_rows), len(out_accs), len(hosted)

    def body(*refs):
        rin = refs[:nr]
        pin = refs[nr:nr + npar]
        ex_in = refs[nr + npar:nr + npar + nex]
        orow = refs[nr + npar + nex:nr + npar + nex + nor]
        oacc = refs[nr + npar + nex + nor:nr + npar + nex + nor + noa]
        ex_out = refs[nr + npar + nex + nor + noa:nr + npar + 2 * nex + nor + noa]
        step = pl.program_id(0)
        _hosted_exchange(step == 0, step == ntile - 1, ex_in, ex_out, refs[nr + npar + 2 * nex + nor + noa:], scatter,
                         owners)
        pvals = [p[...] for p in pin]
        totals = []
        for sub in range(nsub):
            at = slice(sub * chunk, (sub + 1) * chunk)
            outs = fn(*[r[at, :] for r in rin], *pvals)
            for ref, o in zip(orow, outs[:nor]):
                if isinstance(o, (tuple, list)):
                    col = 0
                    for piece in o:
                        ref[at, col:col + piece.shape[1]] = piece.astype(ref.dtype)
                        col += piece.shape[1]
                else:
                    ref[at, :] = o.astype(ref.dtype)
            accs = list(outs[nor:])
            totals = accs if sub == 0 else [t + a for t, a in zip(totals, accs)]

        def accumulate(ref, o):
            @pl.when(step == 0)
            def _():
                ref[...] = o

            @pl.when(step > 0)
            def _():
                ref[...] += o

        for ref, o in zip(oacc, totals):
            accumulate(ref, o)

    def colspec(width, cb):
        return pl.BlockSpec((tile, width), lambda i: (i, cb))

    hbm = pl.BlockSpec(memory_space=pltpu.HBM)
    outs = pl.pallas_call(
        body,
        grid=(ntile,),
        in_specs=[colspec(w, cb) for (_, w, cb) in rows]
        + [pl.BlockSpec(p.shape, lambda i: (0, 0), pipeline_mode=pl.Buffered(1)) for p in pars] + [hbm] * nex,
        out_specs=[colspec(w, 0) for (w, _) in out_rows]
        + [pl.BlockSpec(s, lambda i: (0, 0)) for s in out_accs] + [hbm] * nex,
        out_shape=[jax.ShapeDtypeStruct((R, w), dt) for (w, dt) in out_rows]
        + [jax.ShapeDtypeStruct(s, F32) for s in out_accs] + _exchange_out_shapes(hosted, scatter),
        scratch_shapes=_exchange_scratch(nex) if nex else [],
        compiler_params=_cparams(("arbitrary",), 56),
        name=name,
    )(*[r[0] for r in rows], *pars, *hosted)
    return (outs[:nor + noa], outs[nor + noa:]) if nex else outs


def _rms_fn(x, g):
    return x * lax.rsqrt(jnp.mean(x * x, axis=-1, keepdims=True) + RMS_EPS) * g


FUSED_TILE = 512
FUSED_CHUNKS = 2


def _relu_squared(u):
    pos = jnp.maximum(u.astype(F32), 0.0)
    return pos * pos


def _down_proj_loss(u, w_down, x1, tgt, g):
    d = x1.shape[1]

    def fn(uv, xv, tv, wv, gv):
        x2 = xv + _bdot_raw(_relu_squared(uv), wv, NN)
        y, vjp = jax.vjp(_rms_fn, x2, gv)
        err = y - tv
        loss = 0.5 * jnp.sum(jnp.sum(err * err, axis=-1, keepdims=True), axis=0, keepdims=True) / d
        dx, dg = vjp(err / d)
        return dx, dx, jnp.broadcast_to(loss, (1, LANES)), dg

    return _rowwise(fn, [u, x1, tgt], [w_down, g], [(d, F32), (d, BF16)], [(1, LANES), g.shape],
                    tile=FUSED_TILE, nsub=FUSED_CHUNKS, name="mlp_down_final_norm_loss")


def _proj_bwd_norm_bwd(cts, weights_t, x, dres, g, name, hosted=(), scatter=False, owners=None):
    n = len(cts)

    def fn(*vals):
        ctv, (xv, dresv), wv, gv = vals[:n], vals[n:n + 2], vals[n + 2:2 * n + 2], vals[-1]
        dh = _bdot_raw(ctv[0], wv[0], NT)
        for c, w in zip(ctv[1:], wv[1:]):
            dh = dh + _bdot_raw(c, w, NT)
        _, vjp = jax.vjp(_rms_fn, xv, gv)
        dx, dg = vjp(dh)
        return dx + dresv, dg

    return _rowwise(fn, [*cts, x, dres], [*weights_t, g], [(x.shape[1], F32)], [g.shape],
                    tile=FUSED_TILE, nsub=FUSED_CHUNKS, name=name, hosted=hosted, scatter=scatter, owners=owners)


def _head_sum_matrix():
    i = lax.broadcasted_iota(jnp.int32, (RWKV_WIDTH, RWKV_WIDTH), 0) // HEAD_DIM
    j = lax.broadcasted_iota(jnp.int32, (RWKV_WIDTH, RWKV_WIDTH), 1) // HEAD_DIM
    return (i == j).astype(BF16)


def _head_sums_raw(x, esum):
    hi = x.astype(BF16)
    lo = (x - hi.astype(F32)).astype(BF16)
    return _dot(hi, esum) + _dot(lo, esum)


@jax.custom_vjp
def _head_sums(x, esum):
    return _head_sums_raw(x, esum)


_head_sums.defvjp(lambda x, esum: (_head_sums_raw(x, esum), esum),
                  lambda esum, ct: (_head_sums_raw(ct, esum), jnp.zeros_like(esum)))


def _prep_core(xr, xk, xv, xwa, xg, w0, w2p, a0, a2p, g2, k_k, k_a, esum):
    lw = -DECAY_SCALE * _sigmoid(w0 + _bdot_nn(jnp.tanh(xwa), w2p))
    a = _sigmoid(a0 + _bdot_nn(xwa, a2p))
    g = _bdot_nn(_sigmoid(xg), g2)
    kk0 = xk * k_k
    kk = kk0 * jnp.minimum(lax.rsqrt(_head_sums(kk0 * kk0, esum)), 1.0 / L2_EPS)
    k = xk * (1.0 + (a - 1.0) * k_a)
    return xr, lw, k, xv, kk, a, g


_SEGS = ((0, 512), (512, 1024), (1024, 1536), (1536, 1664), (1664, 1792))


PREP_TILE = 256
SUBLANES = 8


def _shifted_tokens(z_ref, zprev_ref, tile_index, seq):
    zc = z_ref[...]
    start = (tile_index * PREP_TILE) % seq == 0
    before = jnp.where(start, 0.0, zprev_ref[SUBLANES - 1:SUBLANES, :])
    rowid = lax.broadcasted_iota(jnp.int32, zc.shape, 0)
    return zc, jnp.where(rowid == 0, before, pltpu.roll(zc, 1, 0))


def _prep_specs(z, mu, pars, index):
    width = z.shape[1]
    per = PREP_TILE // SUBLANES
    return ([pl.BlockSpec((PREP_TILE, width), lambda i: (index(i), 0)),
             pl.BlockSpec((SUBLANES, width), lambda i: (jnp.maximum(index(i) * per - 1, 0), 0))],
            [pl.BlockSpec(p.shape, lambda i: (0, 0)) for p in (mu, *pars)])


def _norm_in_proj_prep(x, seq, g, w_attn, w_rw, mu, pars):
    rows, d = x.shape
    chunk = FUSED_TILE // FUSED_CHUNKS
    npar = len(pars)
    wa_width, wr_width = w_attn.shape[1], w_rw.shape[1]

    def body(x_ref, g_ref, wa_ref, wr_ref, mu_ref, *rest):
        par_refs = rest[:npar]
        h_ref, za_ref, zr_ref = rest[npar:npar + 3]
        out_refs, carry = rest[npar + 3:-1], rest[-1]
        step = pl.program_id(0)

        @pl.when(step == 0)
        def _():
            carry[...] = jnp.zeros_like(carry)

        pv = [p[...] for p in par_refs]
        for sub in range(FUSED_CHUNKS):
            at = slice(sub * chunk, (sub + 1) * chunk)
            h = _rms_fn(x_ref[at, :], g_ref[...])
            h_ref[at, :] = h.astype(h_ref.dtype)
            za_ref[at, :] = _bdot_raw(h, wa_ref[...], NN).astype(za_ref.dtype)
            zc = _bdot_raw(h, wr_ref[...], NN)
            zr_ref[at, :] = zc
            start = (step * FUSED_TILE + sub * chunk) % seq == 0
            before = jnp.where(start, 0.0, carry[SUBLANES - 1:SUBLANES, :])
            rowid = lax.broadcasted_iota(jnp.int32, zc.shape, 0)
            zp = jnp.where(rowid == 0, before, pltpu.roll(zc, 1, 0))
            carry[...] = zc[chunk - SUBLANES:chunk, :]
            zs = zc + (zp - zc) * mu_ref[...]
            outs = _prep_core(*[zs[:, a:b] for a, b in _SEGS], *pv)
            for ref, o in zip(out_refs, outs):
                ref[at, :] = o

    tiled = lambda width: pl.BlockSpec((FUSED_TILE, width), lambda i: (i, 0))
    resident = lambda a: pl.BlockSpec(a.shape, lambda i: (0, 0), pipeline_mode=pl.Buffered(1))
    return pl.pallas_call(
        body,
        grid=(rows // FUSED_TILE,),
        in_specs=[tiled(d)] + [resident(a) for a in (g, w_attn, w_rw, mu, *pars)],
        out_specs=[tiled(d), tiled(wa_width), tiled(wr_width)] + [tiled(RWKV_WIDTH)] * 7,
        out_shape=[jax.ShapeDtypeStruct((rows, d), BF16), jax.ShapeDtypeStruct((rows, wa_width), BF16),
                   jax.ShapeDtypeStruct((rows, wr_width), F32)] + [jax.ShapeDtypeStruct((rows, RWKV_WIDTH), F32)] * 7,
        scratch_shapes=[pltpu.VMEM((SUBLANES, wr_width), F32)],
        compiler_params=_cparams(("arbitrary",), 56),
        name="attn_norm_in_proj_rwkv_prep",
    )(x, g, w_attn, w_rw, mu, *pars)


def _prep_bwd(z, seq, cts, mu, pars, hosted=(), owners=None):
    rows, width = z.shape
    ntile = rows // PREP_TILE
    npar, nct, nex = len(pars), len(cts), len(hosted)
    acc_shapes = [(1, b - a) for a, b in _SEGS] + [p.shape for p in pars[:-1]]

    def body(z_ref, zprev_ref, *rest):
        ct_refs = rest[:nct]
        mu_ref = rest[nct]
        par_refs = rest[nct + 1:nct + 1 + npar]
        ex_in = rest[nct + 1 + npar:nct + 1 + npar + nex]
        dz_ref = rest[nct + 1 + npar + nex]
        acc_refs = rest[nct + 2 + npar + nex:nct + 2 + npar + nex + len(acc_shapes)]
        ex_out = rest[nct + 2 + npar + nex + len(acc_shapes):nct + 2 + npar + 2 * nex + len(acc_shapes)]
        carry = rest[nct + 2 + npar + 2 * nex + len(acc_shapes)]
        step = pl.program_id(0)
        tile_index = ntile - 1 - step
        _hosted_exchange(step == 0, step == ntile - 1, ex_in, ex_out,
                         rest[nct + 3 + npar + 2 * nex + len(acc_shapes):], True, owners)

        @pl.when(step == 0)
        def _():
            carry[...] = jnp.zeros_like(carry)

        zc, zp = _shifted_tokens(z_ref, zprev_ref, tile_index, seq)
        mu_v = mu_ref[...]
        diff = zp - zc
        zs = zc + diff * mu_v
        dra, drb, dlw, dka, dkb, dva, dvb, dkk, da, dg = [c[...] for c in ct_refs]
        pv = [p[...] for p in par_refs]
        _, vjp = jax.vjp(lambda *args: _prep_core(*args, pv[-1]), *[zs[:, a:b] for a, b in _SEGS], *pv[:-1])
        grads = vjp((dra + drb, dlw, dka + dkb, dva + dvb, dkk, da, dg))
        dsegs, dpars = grads[:5], grads[5:]
        last_of_sequence = ((tile_index + 1) * PREP_TILE) % seq == 0
        accs = []
        for ds, (a, b) in zip(dsegs, _SEGS):
            mu_s = mu_v[:, a:b]
            dzp = ds * mu_s
            after = jnp.where(last_of_sequence, 0.0, carry[0:1, a:b])
            rowid = lax.broadcasted_iota(jnp.int32, dzp.shape, 0)
            from_next = jnp.where(rowid == PREP_TILE - 1, after, pltpu.roll(dzp, PREP_TILE - 1, 0))
            dz_ref[:, a:b] = (ds * (1.0 - mu_s) + from_next).astype(dz_ref.dtype)
            carry[:, a:b] = dzp[0:SUBLANES, :]
            accs.append(jnp.sum(ds * diff[:, a:b], axis=0, keepdims=True))
        accs.extend(dpars)

        def accumulate(ref, o):
            @pl.when(step == 0)
            def _():
                ref[...] = o

            @pl.when(step > 0)
            def _():
                ref[...] += o

        for ref, o in zip(acc_refs, accs):
            accumulate(ref, o)

    rev = lambda i: ntile - 1 - i
    hbm = pl.BlockSpec(memory_space=pltpu.HBM)
    zspecs, pspecs = _prep_specs(z, mu, pars, rev)
    return pl.pallas_call(
        body,
        grid=(ntile,),
        in_specs=zspecs + [pl.BlockSpec((PREP_TILE, RWKV_WIDTH), lambda i: (rev(i), 0))] * nct + pspecs + [hbm] * nex,
        out_specs=[pl.BlockSpec((PREP_TILE, width), lambda i: (rev(i), 0))]
        + [pl.BlockSpec(s, lambda i: (0, 0)) for s in acc_shapes] + [hbm] * nex,
        out_shape=[jax.ShapeDtypeStruct((rows, width), BF16)] + [jax.ShapeDtypeStruct(s, F32) for s in acc_shapes]
        + _exchange_out_shapes(hosted, True),
        scratch_shapes=[pltpu.VMEM((SUBLANES, width), F32)] + (_exchange_scratch(nex) if nex else []),
        compiler_params=_cparams(("arbitrary",), 56),
        name="rwkv_prep_bwd",
    )(z, z, *cts, mu, *pars, *hosted)


def _post_fn(y, r, k, v, g, ln_w, ln_b, r_k, esum):
    mean = _head_sums(y, esum) * (1.0 / HEAD_DIM)
    yc = y - mean
    var = _head_sums(yc * yc, esum) * (1.0 / HEAD_DIM)
    yn = yc * lax.rsqrt(var + GN_EPS) * ln_w + ln_b
    bonus = _head_sums(r * k * r_k, esum) * v
    return (yn + bonus) * g


def _post_out_proj_norm(y, r, k, v, g, attn_out, x, pars, w_attn_rows, w_rwkv_rows, g_norm):
    npar = len(pars)

    def fn(yv, rv, kv, vv, gv, av, xv, *rest):
        wa, wr, gn = rest[npar:]
        rw = _post_fn(yv, rv, kv, vv, gv, *rest[:npar])
        x1 = xv + _bdot_raw(av, wa, NN) + _bdot_raw(rw, wr, NN)
        return rw, x1, _rms_fn(x1, gn)

    d = x.shape[1]
    return _rowwise(fn, [y, r, k, v, g, attn_out, x], [*pars, w_attn_rows, w_rwkv_rows, g_norm],
                    [(RWKV_WIDTH, BF16), (d, F32), (d, BF16)], [],
                    tile=FUSED_TILE, nsub=FUSED_CHUNKS, name="rwkv_post_out_proj_mlp_norm")


def _out_proj_bwd_post_bwd(dx1, y, r, k, v, g, pars, w_attn_rows, w_rwkv_rows):
    npar = len(pars)

    def fn(dxv, yv, rv, kv, vv, gv, *rest):
        wa, wr = rest[npar:]
        esum = rest[npar - 1]
        d_attn = _bdot_raw(dxv, wa, NT)
        d_rw = _bdot_raw(dxv, wr, NT)
        _, vjp = jax.vjp(lambda *a: _post_fn(*a, esum), yv, rv, kv, vv, gv, *rest[:npar - 1])
        return (d_attn, *vjp(d_rw))

    return _rowwise(fn, [dx1, y, r, k, v, g], [*pars, w_attn_rows, w_rwkv_rows],
                    [(RWKV_WIDTH, BF16)] * 2 + [(RWKV_WIDTH, F32)] * 4,
                    [p.shape for p in pars[:-1]], tile=FUSED_TILE, nsub=FUSED_CHUNKS, name="out_proj_bwd_rwkv_post_bwd")


def _tri_inverses(ms, tick=lambda: None):
    n = ms[0].shape[0]
    row = lax.broadcasted_iota(jnp.int32, (n, n), 0)
    col = lax.broadcasted_iota(jnp.int32, (n, n), 1)
    eye = jnp.where(row == col, 1.0, 0.0)
    t_inv = [eye + m for m in ms]
    power = [_bdot_raw(m, m, NN) for m in ms]
    steps = int(math.log2(n // 2)) - 1
    for step in range(steps):
        if step < steps - 1:
            both = [_bdot_raw(jnp.concatenate([p, t], axis=0), p, NN) for p, t in zip(power, t_inv)]
            power = [b[:n] for b in both]
            t_inv = [t + b[n:] for t, b in zip(t_inv, both)]
        else:
            t_inv = [t + _bdot_raw(t, p, NN) for t, p in zip(t_inv, power)]
        tick()
    return t_inv


def _tri_solve_bwd(res, dus):
    t_inv, us = res
    dxs = tuple(_bdot_raw(t, du, TN) for t, du in zip(t_inv, dus))
    dms = tuple(_bdot_raw(dx, u, NT) for dx, u in zip(dxs, us))
    return dms, dxs


@functools.partial(jax.custom_vjp, nondiff_argnums=(3, 4))
def _kept_bdot_c(a, b, kept, ca, cb):
    return kept.astype(F32)


def _kept_bdot_c_fwd(a, b, kept, ca, cb):
    return kept.astype(F32), (a, b, kept)


def _kept_bdot_c_bwd(ca, cb, res, ct):
    a, b, kept = res
    return (*_bdot_c_bwd(ca, cb, (a, b), ct), jnp.zeros_like(kept))


_kept_bdot_c.defvjp(_kept_bdot_c_fwd, _kept_bdot_c_bwd)


def _kept_bdot(a, b, kept, dn):
    return _kept_bdot_c(a, b, kept, dn[0][0][0], dn[0][1][0])


@jax.custom_vjp
def _kept_tri_solve(ms, xs, t_inv, us):
    return tuple(u.astype(F32) for u in us)


def _kept_tri_solve_fwd(ms, xs, t_inv, us):
    return tuple(u.astype(F32) for u in us), (t_inv, us)


def _kept_tri_solve_bwd(res, dus):
    t_inv, us = res
    dms, dxs = _tri_solve_bwd(res, dus)
    return dms, dxs, tuple(jnp.zeros_like(t) for t in t_inv), tuple(jnp.zeros_like(u) for u in us)


_kept_tri_solve.defvjp(_kept_tri_solve_fwd, _kept_tri_solve_bwd)


def _chunk_fn(ss, rs, lws, ks, vs, kks, als, kept=None, tick=lambda: None):
    c = rs[0].shape[0]
    n = 2 * c
    row = lax.broadcasted_iota(jnp.int32, (n, n), 0)
    col = lax.broadcasted_iota(jnp.int32, (n, n), 1)
    incl = (row % c) >= (col % c)
    strict = (row % c) > (col % c)
    lane = lax.broadcasted_iota(jnp.int32, (1, LANES), 1)
    m_lo = jnp.where(lane < HEAD_DIM, 1.0, 0.0)
    m_hi = 1.0 - m_lo

    def stack(a):
        return jnp.concatenate([a * m_lo, a * m_hi], axis=0)

    cums = [_cumsum_rows(lw) for lw in lws]
    totals = [jnp.sum(lw, axis=0, keepdims=True) for lw in lws]
    bs = [kk * al for kk, al in zip(kks, als)]
    grows = [jnp.exp(-cum) for cum in cums]
    a_s = [stack(-kk * jnp.exp(cum - lw)) for kk, cum, lw in zip(kks, cums, lws)]
    b_s = [stack(b * g) for b, g in zip(bs, grows)]
    k_s = [stack(k * g) for k, g in zip(ks, grows)]
    r_s = [stack(r * jnp.exp(cum)) for r, cum in zip(rs, cums)]
    v_s = [stack(v) for v in vs]
    tick()
    pair = lambda p, q: jnp.concatenate([p, q], axis=0)
    ar_s = [pair(a, r) for a, r in zip(a_s, r_s)]
    if kept is None:
        products = [_bdot_raw(ar, pair(b, k), NT) for ar, b, k in zip(ar_s, b_s, k_s)]
    else:
        products = [_kept_bdot(ar, pair(b, k), kp, NT) for ar, b, k, kp in zip(ar_s, b_s, k_s, kept[0])]
    tick()
    blocks = [_quarters(p) for p in products]
    m_ab = [jnp.where(strict, q[0], 0.0) for q in blocks]
    m_ak = [jnp.where(strict, q[1], 0.0) for q in blocks]
    m_rb = [jnp.where(incl, q[2], 0.0) for q in blocks]
    m_rk = [jnp.where(incl, q[3], 0.0) for q in blocks]
    from_state = [_halves(_bdot(ar, s, NT)) for ar, s in zip(ar_s, ss)]
    tick()
    from_v = [_halves(_bdot(pair(mk, mr), v)) for mk, mr, v in zip(m_ak, m_rk, v_s)]
    tick()
    x = tuple(fs[0] + fv[0] for fs, fv in zip(from_state, from_v))
    if kept is None:
        t_inv = _tri_inverses(m_ab, tick)
        u = [_bdot_raw(t, xx, NN) for t, xx in zip(t_inv, x)]
        tick()
    else:
        u = _kept_tri_solve(tuple(m_ab), x, kept[1], kept[2])
    y = [_fold_rows(fs[1] + _bdot(mb, uu) + fv[1]) for fs, mb, uu, fv in zip(from_state, m_rb, u, from_v)]
    tick()
    tails = [jnp.exp(tot - cum) for tot, cum in zip(totals, cums)]
    s_new = [s * jnp.exp(tot) + _bdot(pair(uu, v), pair(stack(b * tl), stack(k * tl)), TN)
             for s, tot, uu, b, tl, v, k in zip(ss, totals, u, bs, tails, v_s, ks)]
    if kept is None:
        keep = lambda vals: tuple(v.astype(BF16) for v in vals)
        return tuple(y), tuple(s_new), (keep(products), keep(t_inv), keep(u))
    return tuple(y), tuple(s_new)


def _chains(bsz, npair):
    return [(b, p, slice(p * LANES, (p + 1) * LANES)) for b in range(bsz) for p in range(npair)]


def _hosted_exchange(first, last, ex_in, ex_out, sems, scatter, owners=None):
    if not ex_in:
        return

    @pl.when(first)
    def _():
        _exchange_start(_exchange_copies(ex_in, ex_out, *sems, scatter, arrivals=False, owners=owners))

    @pl.when(last)
    def _():
        _exchange_wait(_exchange_copies(ex_in, ex_out, *sems, scatter, arrivals=True, owners=owners))


def _rwkv_attn_fwd(r, lw, k, v, kk, al, z, sink_rows, hosted=(), scatter=False):
    bsz, t, w = r.shape
    npair, nchunk = w // LANES, t // CHUNK
    nb = t // BLOCK
    assert bsz * nb == nchunk
    chains = _chains(bsz, npair)
    nex = len(hosted)
    apair = ATTN_WIDTH // LANES

    def body(*refs):
        r_ref, lw_ref, k_ref, v_ref, kk_ref, al_ref = refs[:6]
        q_ref, kp_ref, kc_ref, vp_ref, vc_ref, sink_ref = refs[6:12]
        ex_in = refs[12:12 + nex]
        y_ref, sall_ref, prod_ref, tinv_ref, u_ref, o_ref, p_ref, ps_ref = refs[12 + nex:20 + nex]
        ex_out = refs[20 + nex:20 + 2 * nex]
        s_scr = refs[20 + 2 * nex]
        step = pl.program_id(0)

        @pl.when(step == 0)
        def _():
            s_scr[...] = jnp.zeros_like(s_scr)

        _hosted_exchange(step == 0, step == nchunk - 1, ex_in, ex_out, refs[21 + 2 * nex:], scatter)
        qs = tuple(q_ref[0, :, pair * LANES:(pair + 1) * LANES] for pair in range(apair))
        attn = []
        stages = _attn_block_stages(qs, kp_ref[0], kc_ref[0], vp_ref[0], vc_ref[0], _sink_values(sink_ref),
                                    step % nb == 0, attn)
        ss = tuple(s_scr[i] for i in range(len(chains)))
        for i, s in enumerate(ss):
            sall_ref[0, i] = s
        ys, s_new, kept = _chunk_fn(ss, *[tuple(ref[b, :, cols] for b, _, cols in chains)
                                          for ref in (r_ref, lw_ref, k_ref, v_ref, kk_ref, al_ref)],
                                    tick=lambda: next(stages, None))
        for _ in stages:
            pass
        for i, (b, _, cols) in enumerate(chains):
            y_ref[b, :, cols] = ys[i]
            s_scr[i] = s_new[i]
            prod_ref[0, i], tinv_ref[0, i], u_ref[0, i] = kept[0][i], kept[1][i], kept[2][i]
        outs, probs, psinks = attn
        for pair in range(apair):
            o_ref[0, :, pair * LANES:(pair + 1) * LANES] = outs[pair].astype(o_ref.dtype)
        for h, p in enumerate(probs):
            p_ref[0, 0, h] = p
        ps_ref[0] = _head_columns(psinks)

    spec = pl.BlockSpec((bsz, CHUNK, w), lambda c: (0, c, 0))
    hbm = pl.BlockSpec(memory_space=pltpu.HBM)
    per_chunk = lambda n: pl.BlockSpec((1, len(chains), n, n), lambda c: (c, 0, 0, 0))
    kept_shape = lambda n: jax.ShapeDtypeStruct((nchunk, len(chains), n, n), BF16)
    kcol, vcol = ATTN_WIDTH // KV_WIDTH, ATTN_WIDTH // KV_WIDTH + 1
    before = lambda c: jnp.maximum(c % nb - 1, 0)
    outs = pl.pallas_call(
        body,
        grid=(nchunk,),
        in_specs=[spec] * 6
        + [pl.BlockSpec((1, BLOCK, ATTN_WIDTH), lambda c: (c // nb, c % nb, 0)),
           pl.BlockSpec((1, BLOCK, KV_WIDTH), lambda c: (c // nb, before(c), kcol)),
           pl.BlockSpec((1, BLOCK, KV_WIDTH), lambda c: (c // nb, c % nb, kcol)),
           pl.BlockSpec((1, BLOCK, KV_WIDTH), lambda c: (c // nb, before(c), vcol)),
           pl.BlockSpec((1, BLOCK, KV_WIDTH), lambda c: (c // nb, c % nb, vcol)),
           pl.BlockSpec(sink_rows.shape, lambda c: (0, 0))]
        + [hbm] * nex,
        out_specs=[spec, per_chunk(LANES), per_chunk(4 * CHUNK), per_chunk(2 * CHUNK), per_chunk(2 * CHUNK),
                   pl.BlockSpec((1, BLOCK, ATTN_WIDTH), lambda c: (c // nb, c % nb, 0)),
                   pl.BlockSpec((1, 1, N_ATTN_HEADS, BLOCK, 2 * BLOCK), lambda c: (c // nb, c % nb, 0, 0, 0)),
                   pl.BlockSpec((1, BLOCK, LANES), lambda c: (c // nb, c % nb, 0))]
        + [hbm] * nex,
        out_shape=[jax.ShapeDtypeStruct((bsz, t, w), F32),
                   jax.ShapeDtypeStruct((nchunk, len(chains), LANES, LANES), F32),
                   kept_shape(4 * CHUNK), kept_shape(2 * CHUNK), kept_shape(2 * CHUNK),
                   jax.ShapeDtypeStruct((bsz, t, ATTN_WIDTH), BF16),
                   jax.ShapeDtypeStruct((bsz, nb, N_ATTN_HEADS, BLOCK, 2 * BLOCK), F32),
                   jax.ShapeDtypeStruct((bsz, t, LANES), F32)]
        + _exchange_out_shapes(hosted, scatter),
        scratch_shapes=[pltpu.VMEM((len(chains), LANES, LANES), F32)] + (_exchange_scratch(nex) if nex else []),
        compiler_params=_cparams(("arbitrary",), 48),
        name="rwkv_chunk_swa_fwd",
    )(r, lw, k, v, kk, al, z, z, z, z, z, sink_rows, *hosted)
    return outs[0], outs[1:5], outs[5:8], outs[8:]


def _rwkv_attn_bwd(r, lw, k, v, kk, al, from_fwd, dy, z, dout, probs, psinks, hosted=(), scatter=False):
    bsz, t, w = r.shape
    npair, nchunk = w // LANES, t // CHUNK
    nb = t // BLOCK
    assert bsz * nb == nchunk
    chains = _chains(bsz, npair)
    nex = len(hosted)
    apair = ATTN_WIDTH // LANES

    def body(*refs):
        r_ref, lw_ref, k_ref, v_ref, kk_ref, al_ref, s_ref, prod_ref, tinv_ref, u_ref, dy_ref = refs[:11]
        q_ref, kp_ref, kc_ref, vp_ref, vc_ref, do_ref, p_ref, ps_ref = refs[11:19]
        ex_in = refs[19:19 + nex]
        out_refs = refs[19 + nex:25 + nex]
        dz_ref, dsink_ref = refs[25 + nex:27 + nex]
        ex_out = refs[27 + nex:27 + 2 * nex]
        ds_scr, carry = refs[27 + 2 * nex:29 + 2 * nex]
        step = pl.program_id(0)

        @pl.when(step == 0)
        def _():
            ds_scr[...] = jnp.zeros_like(ds_scr)
            dsink_ref[...] = jnp.zeros_like(dsink_ref)

        @pl.when(step % nb == 0)
        def _():
            carry[...] = jnp.zeros_like(carry)

        _hosted_exchange(step == 0, step == nchunk - 1, ex_in, ex_out, refs[29 + 2 * nex:], scatter)
        qs = tuple(q_ref[0, :, pair * LANES:(pair + 1) * LANES] for pair in range(apair))
        dos = tuple(do_ref[0, :, pair * LANES:(pair + 1) * LANES] for pair in range(apair))
        attn = []
        stages = _attn_block_bwd_stages(qs, kp_ref[0], kc_ref[0], vp_ref[0], vc_ref[0], dos,
                                        [p_ref[0, 0, h] for h in range(N_ATTN_HEADS)], step % nb == nb - 1, attn)
        ss = tuple(s_ref[0, i] for i in range(len(chains)))
        kept = tuple(tuple(ref[0, i] for i in range(len(chains))) for ref in (prod_ref, tinv_ref, u_ref))
        _, vjp = jax.vjp(functools.partial(_chunk_fn, kept=kept), ss,
                         *[tuple(ref[b, :, cols] for b, _, cols in chains)
                           for ref in (r_ref, lw_ref, k_ref, v_ref, kk_ref, al_ref)])
        rules = [0]

        def one_stage_per_round():
            rules[0] += 1
            if rules[0] % len(chains) == 0:
                next(stages, None)

        _BESIDE_BACKWARD[0] = one_stage_per_round
        try:
            grads = vjp((tuple(dy_ref[b, :, cols].astype(F32) for b, _, cols in chains),
                         tuple(ds_scr[i] for i in range(len(chains)))))
        finally:
            _BESIDE_BACKWARD[0] = None
        for _ in stages:
            pass
        for i, (b, _, cols) in enumerate(chains):
            ds_scr[i] = grads[0][i]
            for ref, gval in zip(out_refs, grads[1:]):
                ref[b, :, cols] = gval[i]
        dqs, dkp, dkc, dvp, dvc, deltas = attn
        for pair in range(apair):
            dz_ref[0, :, pair * LANES:(pair + 1) * LANES] = dqs[pair].astype(dz_ref.dtype)
        dsink_ref[...] -= jnp.sum(ps_ref[0] * _head_columns(deltas), axis=0, keepdims=True)
        dz_ref[0, :, ATTN_WIDTH:ATTN_WIDTH + KV_WIDTH] = (dkc + carry[0]).astype(dz_ref.dtype)
        dz_ref[0, :, ATTN_WIDTH + KV_WIDTH:QKV_WIDTH] = (dvc + carry[1]).astype(dz_ref.dtype)
        carry[0] = dkp
        carry[1] = dvp

    spec = pl.BlockSpec((bsz, CHUNK, w), lambda c: (0, nchunk - 1 - c, 0))
    per_chunk = lambda n: pl.BlockSpec((1, len(chains), n, n), lambda c: (nchunk - 1 - c, 0, 0, 0))
    hbm = pl.BlockSpec(memory_space=pltpu.HBM)
    kcol, vcol = ATTN_WIDTH // KV_WIDTH, ATTN_WIDTH // KV_WIDTH + 1
    seq_of = lambda c: c // nb
    blk = lambda c: nb - 1 - c % nb
    before = lambda c: jnp.maximum(blk(c) - 1, 0)
    outs = pl.pallas_call(
        body,
        grid=(nchunk,),
        in_specs=[spec] * 6 + [per_chunk(LANES), per_chunk(4 * CHUNK), per_chunk(2 * CHUNK), per_chunk(2 * CHUNK), spec]
        + [pl.BlockSpec((1, BLOCK, ATTN_WIDTH), lambda c: (seq_of(c), blk(c), 0)),
           pl.BlockSpec((1, BLOCK, KV_WIDTH), lambda c: (seq_of(c), before(c), kcol)),
           pl.BlockSpec((1, BLOCK, KV_WIDTH), lambda c: (seq_of(c), blk(c), kcol)),
           pl.BlockSpec((1, BLOCK, KV_WIDTH), lambda c: (seq_of(c), before(c), vcol)),
           pl.BlockSpec((1, BLOCK, KV_WIDTH), lambda c: (seq_of(c), blk(c), vcol)),
           pl.BlockSpec((1, BLOCK, ATTN_WIDTH), lambda c: (seq_of(c), blk(c), 0)),
           pl.BlockSpec((1, 1, N_ATTN_HEADS, BLOCK, 2 * BLOCK), lambda c: (seq_of(c), blk(c), 0, 0, 0)),
           pl.BlockSpec((1, BLOCK, LANES), lambda c: (seq_of(c), blk(c), 0))]
        + [hbm] * nex,
        out_specs=[spec] * 6
        + [pl.BlockSpec((1, BLOCK, QKV_WIDTH), lambda c: (seq_of(c), blk(c), 0)),
           pl.BlockSpec((1, LANES), lambda c: (0, 0))]
        + [hbm] * nex,
        out_shape=[jax.ShapeDtypeStruct((bsz, t, w), F32)] * 6
        + [jax.ShapeDtypeStruct((bsz, t, QKV_WIDTH), BF16), jax.ShapeDtypeStruct((1, LANES), F32)]
        + _exchange_out_shapes(hosted, scatter),
        scratch_shapes=[pltpu.VMEM((len(chains), LANES, LANES), F32), pltpu.VMEM((2, BLOCK, KV_WIDTH), F32)]
        + (_exchange_scratch(nex) if nex else []),
        compiler_params=_cparams(("arbitrary",), 48),
        name="rwkv_chunk_swa_bwd",
    )(r, lw, k, v, kk, al, *from_fwd, dy, z, z, z, z, z, dout, probs, psinks, *hosted)
    return outs[:6], outs[6:8], outs[8:]


def _alibi_slope(head):
    return 2.0 ** (-8.0 * (head + 1) / N_ATTN_HEADS)


def _attn_setup(first):
    row = lax.broadcasted_iota(jnp.int32, (BLOCK, 2 * BLOCK), 0)
    col = lax.broadcasted_iota(jnp.int32, (BLOCK, 2 * BLOCK), 1)
    lane = lax.broadcasted_iota(jnp.int32, (1, LANES), 1)
    halves = [jnp.where((lane // HEAD_DIM) == half, 1.0, 0.0) for half in range(2)]
    srow = lax.broadcasted_iota(jnp.int32, (LANES, LANES), 0)
    scol = lax.broadcasted_iota(jnp.int32, (LANES, LANES), 1)
    swap = jnp.where((srow + HEAD_DIM) % LANES == scol, 1.0, 0.0)
    dist = row - col + BLOCK
    valid = jnp.logical_and(jnp.logical_and(dist >= 0, dist < BLOCK),
                            jnp.logical_or(col >= BLOCK, jnp.logical_not(first)))
    return halves, swap, dist.astype(F32), valid, HEAD_DIM ** -0.5


def _attn_keys_values(kp, kc, vp, vc, swap, npair):
    stored = (jnp.concatenate([kp, kc], axis=0), jnp.concatenate([vp, vc], axis=0))
    swapped = tuple(_bdot_raw(t, swap, NN) for t in stored)
    heads = [(pair, half) for pair in range(npair) for half in range(2)]
    return heads, [half == pair // 2 for pair, half in heads], stored, swapped


def _attn_block_stages(qs, kp, kc, vp, vc, sinks, first, result):
    halves, swap, dist, valid, scale = _attn_setup(first)
    heads, as_stored, stored, swapped = _attn_keys_values(kp, kc, vp, vc, swap, len(qs))
    kv = [stored if own else swapped for own in as_stored]
    slopes = [_alibi_slope(2 * pair + half) for pair, half in heads]
    qa = [qs[pair] * halves[half] for pair, half in heads]
    yield
    s = [jnp.where(valid, _bdot_raw(q, t[0], NT) * scale - sl * dist, NEG_INF) for q, t, sl in zip(qa, kv, slopes)]
    yield
    mx = [jnp.maximum(jnp.max(a, axis=-1, keepdims=True), sk) for a, sk in zip(s, sinks)]
    yield
    e = [jnp.exp(a - m) for a, m in zip(s, mx)]
    yield
    es = [jnp.exp(sk - m) for sk, m in zip(sinks, mx)]
    inv = [1.0 / (jnp.sum(a, axis=-1, keepdims=True) + b) for a, b in zip(e, es)]
    yield
    probs = [a * i for a, i in zip(e, inv)]
    yield
    o = [_bdot_raw(p, t[1], NN) for p, t in zip(probs, kv)]
    yield
    outs = tuple(o[2 * pair] * halves[0] + o[2 * pair + 1] * halves[1] for pair in range(len(qs)))
    result.extend([outs, probs, [b * i for b, i in zip(es, inv)]])


def _attn_block_bwd_stages(qs, kp, kc, vp, vc, dos, probs, first, result):
    halves, swap, _, _, scale = _attn_setup(first)
    heads, as_stored, stored, swapped = _attn_keys_values(kp, kc, vp, vc, swap, len(qs))
    kv = [stored if own else swapped for own in as_stored]
    qa = [qs[pair] * halves[half] for pair, half in heads]
    do = [dos[pair] * halves[half] for pair, half in heads]
    yield
    dp = [_bdot_raw(d, t[1], NT) for d, t in zip(do, kv)]
    yield
    delta = [jnp.sum(p * d, axis=-1, keepdims=True) for p, d in zip(probs, dp)]
    yield
    ds = [p * (d - dl) for p, d, dl in zip(probs, dp, delta)]
    yield
    dq = [_bdot_raw(g, t[0], NN) * (scale * halves[half]) for g, t, (_, half) in zip(ds, kv, heads)]
    yield
    dk = [_bdot_raw(g, q, TN) * scale for g, q in zip(ds, qa)]
    yield
    dv = [_bdot_raw(p, d, TN) for p, d in zip(probs, do)]
    yield
    dqs = tuple(dq[2 * pair] + dq[2 * pair + 1] for pair in range(len(qs)))

    def total(parts):
        direct = sum(g for g, own in zip(parts, as_stored) if own)
        return direct + _bdot_raw(sum(g for g, own in zip(parts, as_stored) if not own), swap, NN)

    dk_all, dv_all = total(dk), total(dv)
    result.extend([dqs, dk_all[:BLOCK], dk_all[BLOCK:], dv_all[:BLOCK], dv_all[BLOCK:], delta])


def _sink_values(sink_ref):
    return [jnp.max(sink_ref[h:h + 1, :], axis=-1, keepdims=True) for h in range(N_ATTN_HEADS)]


def _head_columns(cols):
    lane = lax.broadcasted_iota(jnp.int32, (1, LANES), 1)
    return sum(c * jnp.where(lane == h, 1.0, 0.0) for h, c in enumerate(cols))


def _exchange_out_shapes(arrays, scatter):
    return [jax.ShapeDtypeStruct((N_DEV,) + (a.shape[1:] if scatter else a.shape), a.dtype) for a in arrays]


def _exchange_scratch(n):
    return [pltpu.SemaphoreType.DMA((n, N_DEV - 1)), pltpu.SemaphoreType.DMA((n, N_DEV - 1)),
            pltpu.SemaphoreType.DMA((n,))]


def _exchange_copies(ins, outs, send_sems, recv_sems, local_sems, scatter, arrivals=True, owners=None):
    x, y, c = lax.axis_index("x"), lax.axis_index("y"), lax.axis_index("c")
    me = 4 * x + 2 * y + c
    copies = []
    for i in range(len(ins)):
        lo, hi = owners[i] if owners is not None and owners[i] is not None else (0, N_DEV)
        everyone = (lo, hi) == (0, N_DEV)

        def chosen(p, lo=lo, hi=hi, everyone=everyone):
            return None if everyone else jnp.logical_and(p >= lo, p < hi)

        def block(p, i=i, lo=lo, hi=hi):
            return ins[i].at[jnp.clip(p - lo, 0, hi - lo - 1)] if scatter else ins[i]

        own = pltpu.make_async_copy(block(me), outs[i].at[me], local_sems.at[i])
        copies.append((own, None, True, chosen(me), chosen(me)))
        for d in range(1, N_DEV):
            px = 1 - x if d & 4 else x
            py = 1 - y if d & 2 else y
            pc = 1 - c if d & 1 else c
            peer = 4 * px + 2 * py + pc
            src = block(peer)
            send = pltpu.make_async_remote_copy(src, outs[i].at[me], send_sems.at[i, d - 1], recv_sems.at[i, d - 1],
                                                device_id=(px, py, pc), device_id_type=MESH)
            recv = pltpu.make_async_remote_copy(src, outs[i].at[peer], send_sems.at[i, d - 1], recv_sems.at[i, d - 1],
                                                device_id=(px, py, pc), device_id_type=MESH) if arrivals else None
            copies.append((send, recv, False, chosen(peer), chosen(me)))
    return copies


def _when(pred, action):
    if pred is None:
        action()
    else:
        pl.when(pred)(action)


def _exchange_start(copies):
    for send, _, _, send_ok, _ in copies:
        _when(send_ok, send.start)


def _exchange_wait(copies):
    for send, recv, local, send_ok, recv_ok in copies:
        if local:
            _when(send_ok, send.wait)
        else:
            _when(send_ok, send.wait_send)
            _when(recv_ok, recv.wait_recv)


def _gather_two_level(arrays, name):
    n = len(arrays)

    def body(*refs):
        ins, outs = refs[:n], refs[n:2 * n]
        send_sems, recv_sems, local_sems = refs[2 * n:]
        x, y, c = lax.axis_index("x"), lax.axis_index("y"), lax.axis_index("c")
        index = lambda px, py, pc: 4 * px + 2 * py + pc
        sibling = (x, y, 1 - c)
        chips = [(1 - x, y), (x, 1 - y), (1 - x, 1 - y)]

        def copy(i, k, block, to, src=None):
            slot = outs[i].at[index(*block)]
            return pltpu.make_async_remote_copy(slot if src is None else src, slot, send_sems.at[i, k],
                                                recv_sems.at[i, k], device_id=to, device_id_type=MESH)

        local, sends = [], []
        for i in range(n):
            own = pltpu.make_async_copy(ins[i], outs[i].at[index(x, y, c)], local_sems.at[i])
            own.start()
            local.append(own)
            first = [copy(i, 0, (x, y, c), sibling, src=ins[i])]
            first += [copy(i, 1 + j, (x, y, c), (*chip, c), src=ins[i]) for j, chip in enumerate(chips)]
            for cp in first:
                cp.start()
            sends += first
        for i in range(n):
            for j, chip in enumerate(chips):
                copy(i, 1 + j, (*chip, c), (x, y, c)).wait_recv()
                onward = copy(i, 4 + j, (*chip, c), sibling)
                onward.start()
                sends.append(onward)
        for i in range(n):
            copy(i, 0, sibling, (x, y, c)).wait_recv()
            for j, chip in enumerate(chips):
                copy(i, 4 + j, (*chip, 1 - c), (x, y, c)).wait_recv()
        for cp in sends:
            cp.wait_send()
        for cp in local:
            cp.wait()

    hbm = pl.BlockSpec(memory_space=pltpu.HBM)
    return pl.pallas_call(
        body,
        in_specs=[hbm] * n,
        out_specs=[hbm] * n,
        out_shape=_exchange_out_shapes(arrays, False),
        scratch_shapes=_exchange_scratch(n),
        name=name,
    )(*arrays)


def _exchange(arrays, *, scatter, name):
    n = len(arrays)

    def body(*refs):
        copies = _exchange_copies(refs[:n], refs[n:2 * n], *refs[2 * n:], scatter)
        _exchange_start(copies)
        _exchange_wait(copies)

    hbm = pl.BlockSpec(memory_space=pltpu.HBM)
    return pl.pallas_call(
        body,
        in_specs=[hbm] * n,
        out_specs=[hbm] * n,
        out_shape=_exchange_out_shapes(arrays, scatter),
        scratch_shapes=_exchange_scratch(n),
        name=name,
    )(*arrays)


def _adamw(parts, w, m, v, name, early=None):
    rows, cols = w.shape
    tr = _pick(rows, (256, 128, 64, 8))
    c1 = 1.0 / (1.0 - ADAM_B1 ** ADAM_STEP)
    c2 = 1.0 / (1.0 - ADAM_B2 ** ADAM_STEP)

    def slot_sum(ref):
        total = ref[0].astype(F32)
        for s in range(1, N_DEV):
            total = total + ref[s].astype(F32)
        return total

    def body(p_ref, *rest):
        if early is not None:
            e_ref, rest = rest[0], rest[1:]
        w_ref, m_ref, v_ref, g_ref, d_ref, mo_ref, vo_ref = rest
        g = slot_sum(p_ref)
        if early is not None:
            me = 4 * lax.axis_index("x") + 2 * lax.axis_index("y") + lax.axis_index("c")
            g = jnp.where(jnp.logical_and(me >= early[1], me < early[2]), slot_sum(e_ref), g)
        mn = ADAM_B1 * m_ref[...] + (1.0 - ADAM_B1) * g
        vn = ADAM_B2 * v_ref[...] + (1.0 - ADAM_B2) * (g * g)
        g_ref[...] = g
        mo_ref[...] = mn
        vo_ref[...] = vn
        d_ref[...] = -ADAM_LR * ((mn * c1) / (jnp.sqrt(vn * c2) + ADAM_EPS) + ADAM_WD * w_ref[...])

    spec = pl.BlockSpec((tr, cols), lambda i: (i, 0))
    return pl.pallas_call(
        body,
        grid=(rows // tr,),
        in_specs=[pl.BlockSpec((N_DEV, tr, cols), lambda i: (0, i, 0))] * (1 if early is None else 2) + [spec, spec, spec],
        out_specs=[spec] * 4,
        out_shape=[jax.ShapeDtypeStruct((rows, cols), F32)] * 4,
        compiler_params=_cparams(("parallel",), 48),
        name=name,
    )(parts, *(() if early is None else (early[0],)), w, m, v)


_VECTOR_PARAMS = ("attn_norm_g", "attn_sinks", "rwkv_mu", "w0", "a0", "k_k", "k_a", "r_k", "ln_x_w", "ln_x_b",
                  "mlp_norm_g", "final_norm_g")
_WEIGHT_NAMES = ("attn_norm_g", "w_in", "attn_sinks", "rwkv_mu", "w0", "w2", "a0", "a2", "g2", "k_k", "k_a", "r_k",
                 "ln_x_w", "ln_x_b", "w_out", "mlp_norm_g", "w_up", "w_down", "final_norm_g")


def _pack_vectors(vals):
    pieces = []
    for name in _VECTOR_PARAMS:
        flat = vals[name].reshape(1, -1)
        pad = (-flat.shape[1]) % LANES
        pieces.append(jnp.pad(flat, ((0, 0), (0, pad))) if pad else flat)
    return jnp.concatenate(pieces, axis=1)


def _unpack_vectors(packed, like):
    out, col = {}, 0
    for name in _VECTOR_PARAMS:
        size = like[name].size
        out[name] = packed[0, col:col + size].reshape(like[name].shape)
        col += size + (-size) % LANES
    return out


def kernel(x, attn_norm_g, w_in, attn_sinks, rwkv_mu, w0, w2, a0, a2, g2, k_k, k_a, r_k, ln_x_w, ln_x_b, w_out, mlp_norm_g, w_up, w_down, final_norm_g, loss_target, m_attn_norm_g, m_w_in, m_attn_sinks, m_rwkv_mu, m_w0, m_w2, m_a0, m_a2, m_g2, m_k_k, m_k_a, m_r_k, m_ln_x_w, m_ln_x_b, m_w_out, m_mlp_norm_g, m_w_up, m_w_down, m_final_norm_g, v_attn_norm_g, v_w_in, v_attn_sinks, v_rwkv_mu, v_w0, v_w2, v_a0, v_a2, v_g2, v_k_k, v_k_a, v_r_k, v_ln_x_w, v_ln_x_b, v_w_out, v_mlp_norm_g, v_w_up, v_w_down, v_final_norm_g):
    weights = dict(attn_norm_g=attn_norm_g, w_in=w_in, attn_sinks=attn_sinks, rwkv_mu=rwkv_mu, w0=w0, w2=w2, a0=a0,
                   a2=a2, g2=g2, k_k=k_k, k_a=k_a, r_k=r_k, ln_x_w=ln_x_w, ln_x_b=ln_x_b, w_out=w_out,
                   mlp_norm_g=mlp_norm_g, w_up=w_up, w_down=w_down, final_norm_g=final_norm_g)
    mom1 = dict(attn_norm_g=m_attn_norm_g, w_in=m_w_in, attn_sinks=m_attn_sinks, rwkv_mu=m_rwkv_mu, w0=m_w0, w2=m_w2,
                a0=m_a0, a2=m_a2, g2=m_g2, k_k=m_k_k, k_a=m_k_a, r_k=m_r_k, ln_x_w=m_ln_x_w, ln_x_b=m_ln_x_b,
                w_out=m_w_out, mlp_norm_g=m_mlp_norm_g, w_up=m_w_up, w_down=m_w_down, final_norm_g=m_final_norm_g)
    mom2 = dict(attn_norm_g=v_attn_norm_g, w_in=v_w_in, attn_sinks=v_attn_sinks, rwkv_mu=v_rwkv_mu, w0=v_w0, w2=v_w2,
                a0=v_a0, a2=v_a2, g2=v_g2, k_k=v_k_k, k_a=v_k_a, r_k=v_r_k, ln_x_w=v_ln_x_w, ln_x_b=v_ln_x_b,
                w_out=v_w_out, mlp_norm_g=v_mlp_norm_g, w_up=v_w_up, w_down=v_w_down, final_norm_g=v_final_norm_g)
    bsz, seq, d_model = x.shape
    rows = bsz * seq
    d_in = N_DEV * w_in.shape[2]
    d_ff = N_DEV * w_up.shape[2]

    gathered = _gather_two_level([w_in[0].astype(BF16), w2[0], a2[0], g2[0]], name="gather_in_weights")
    cols_first = lambda a: a.transpose(1, 0, 2).reshape(a.shape[1], -1)
    w_in_f = cols_first(gathered[0])
    w_attn, w_rw = w_in_f[:, :QKV_WIDTH], w_in_f[:, QKV_WIDTH:]
    w2_f, a2_f, g2_f = cols_first(gathered[1]), cols_first(gathered[2]), cols_first(gathered[3])
    lora = w2_f.shape[0]
    w2p = jnp.concatenate([w2_f, jnp.zeros_like(a2_f)], axis=0)
    a2p = jnp.concatenate([jnp.zeros_like(w2_f), a2_f], axis=0)

    esum = _head_sum_matrix()
    sink_rows = jnp.broadcast_to(attn_sinks.reshape(N_ATTN_HEADS, 1), (N_ATTN_HEADS, LANES))
    prep_pars = [w0, w2p, a0, a2p, g2_f, k_k, k_a, esum]
    post_pars = [ln_x_w, ln_x_b, r_k, esum]

    x2d = x.reshape(rows, d_model)
    h1, z_attn, z_rw, r, lw, k, v, kk, al, gate = _norm_in_proj_prep(x2d, seq, attn_norm_g, w_attn, w_rw, rwkv_mu,
                                                                       prep_pars)
    z_attn3 = z_attn.reshape(bsz, seq, QKV_WIDTH)
    as3 = lambda a: a.reshape(bsz, seq, RWKV_WIDTH)
    y, from_fwd, (attn_out, attn_probs, attn_psinks), late = _rwkv_attn_fwd(
        as3(r), as3(lw), as3(k), as3(v), as3(kk), as3(al), z_attn3, sink_rows,
        hosted=[w_out[0].astype(BF16), w_up[0].astype(BF16), w_down[0].astype(BF16)])
    w_out_f = late[0].reshape(-1, d_model)
    w_up_f = cols_first(late[1])
    w_down_f = late[2].reshape(-1, d_model)
    y2 = y.reshape(rows, RWKV_WIDTH)
    attn_out2d = attn_out.reshape(rows, ATTN_WIDTH)
    w_out_attn, w_out_rw = w_out_f[:ATTN_WIDTH], w_out_f[ATTN_WIDTH:]
    rw_out, x1, h2 = _post_out_proj_norm(y2, r, k, v, gate, attn_out2d, x2d, post_pars, w_out_attn, w_out_rw,
                                         mlp_norm_g)

    u = _matmul(h2, w_up_f, "nn", name="mlp_up", out_dtypes=(BF16,), tm=2048)
    dx2, dx2_b, loss_vec, g_final = _down_proj_loss(u, w_down_f, x1, loss_target.reshape(rows, d_model),
                                             final_norm_g.reshape(1, d_model))

    g_w_down = _matmul(u, dx2_b, "tn", name="grad_w_down", out_dtypes=(BF16,), a_map=_relu_squared, tm=2048)
    du = _matmul(dx2_b, w_down_f, "nt", name="mlp_down_bwd", extras=(u,), out_dtypes=(BF16,), tm=2048,
                 epilogue=lambda acc, uv: (acc * (2.0 * jnp.maximum(uv.astype(F32), 0.0)),))
    g_w_up = _matmul(h2, du, "tn", name="grad_w_up", out_dtypes=(BF16,), tn=2048)
    dx1, g_mlp_norm = _proj_bwd_norm_bwd([du], [w_up_f], x1, dx2, mlp_norm_g, "mlp_up_bwd_norm_bwd")
    g_w_out = jnp.concatenate([_matmul(attn_out2d, dx1, "tn", name="grad_w_out_attn", out_dtypes=(BF16,)),
                               _matmul(rw_out, dx1, "tn", name="grad_w_out_rwkv", out_dtypes=(BF16,))], axis=0)
    d_attn_out, dy, dr_a, dk_a, dv_a, dgate, g_ln_w, g_ln_b, g_r_k = _out_proj_bwd_post_bwd(
        dx1, y2, r, k, v, gate, post_pars, w_out_attn, w_out_rw)
    by_cols = lambda a: a.reshape(a.shape[0], N_DEV, -1).transpose(1, 0, 2)
    (dr_b, dlw, dk_b, dv_b, dkk, dal), (dz_attn, g_sink_lanes), (p_w_out, p_w_up, p_w_down) = _rwkv_attn_bwd(
        as3(r), as3(lw), as3(k), as3(v), as3(kk), as3(al), from_fwd, as3(dy),
        z_attn3, d_attn_out.reshape(bsz, seq, ATTN_WIDTH), attn_probs, attn_psinks,
        hosted=[g_w_out.reshape(N_DEV, -1, d_model), by_cols(g_w_up), g_w_down.reshape(N_DEV, -1, d_model)],
        scatter=True)
    flat = lambda a: a.reshape(rows, RWKV_WIDTH)
    dz_attn = dz_attn.reshape(rows, QKV_WIDTH)
    g_w_in_attn = _matmul(h1, dz_attn, "tn", name="grad_w_in_attn", out_dtypes=(BF16,))
    shard = w_in.shape[2]
    n_early = QKV_WIDTH // shard
    early_blocks = g_w_in_attn[:, :n_early * shard].reshape(-1, n_early, shard).transpose(1, 0, 2)
    (dz_rw, gmu_r, gmu_k, gmu_v, gmu_wa, gmu_g, g_w0, g_w2p, g_a0, g_a2p, g_g2, g_k_k, g_k_a,
     p_w_in_early) = _prep_bwd(
        z_rw, seq, [dr_a, flat(dr_b), flat(dlw), dk_a, flat(dk_b), dv_a, flat(dv_b), flat(dkk), flat(dal), dgate],
        rwkv_mu, prep_pars, hosted=[early_blocks], owners=[(0, n_early)])
    late_blocks = jnp.concatenate([g_w_in_attn[:, n_early * shard:],
                                   _matmul(h1, dz_rw, "tn", name="grad_w_in_rwkv", out_dtypes=(BF16,))], axis=1)
    late_blocks = late_blocks.reshape(late_blocks.shape[0], N_DEV - n_early, shard).transpose(1, 0, 2)
    lora_grads = jnp.concatenate([g_w2p[:lora], g_a2p[lora:], g_g2], axis=0)
    (dx, g_attn_norm), (p_w_in, p_lora) = _proj_bwd_norm_bwd(
        [dz_attn, dz_rw], [w_attn, w_rw], x2d, dx1, attn_norm_g, "in_proj_bwd_norm_bwd",
        hosted=[late_blocks, by_cols(lora_grads)], scatter=True, owners=[(n_early, N_DEV), None])

    vec_grads = dict(attn_norm_g=g_attn_norm, attn_sinks=g_sink_lanes[0, :N_ATTN_HEADS], rwkv_mu=jnp.concatenate(
        [gmu_r, gmu_k, gmu_v, gmu_wa, gmu_g], axis=1), w0=g_w0, a0=g_a0, k_k=g_k_k, k_a=g_k_a, r_k=g_r_k,
        ln_x_w=g_ln_w, ln_x_b=g_ln_b, mlp_norm_g=g_mlp_norm, final_norm_g=g_final)
    packed = _pack_vectors(vec_grads)
    nvec = packed.shape[1]
    everyone = _exchange([jnp.concatenate([packed, loss_vec], axis=1)], scatter=False, name="gather_vector_grads")[0]
    vec_parts = everyone[:, :, :nvec]
    loss = jnp.sum(everyone[:, 0, nvec])

    grads, delta, new_m, new_v = {}, {}, {}, {}

    def update(name, part, shape2d):
        res = _adamw(part, weights[name].reshape(shape2d), mom1[name].reshape(shape2d), mom2[name].reshape(shape2d),
                     "adamw_" + name)
        for store, val in zip((grads, delta, new_m, new_v), res):
            store[name] = val.reshape(weights[name].shape)

    res = _adamw(p_w_in, w_in[0], m_w_in[0], v_w_in[0], "adamw_w_in", early=(p_w_in_early, 0, n_early))
    for store, val in zip((grads, delta, new_m, new_v), res):
        store["w_in"] = val[None]
    update("w_out", p_w_out, w_out.shape[1:])
    update("w_up", p_w_up, w_up.shape[1:])
    update("w_down", p_w_down, w_down.shape[1:])
    stack = lambda d: jnp.concatenate([d["w2"][0], d["a2"][0], d["g2"][0]], axis=0)
    lora_res = _adamw(p_lora, stack(weights), stack(mom1), stack(mom2), "adamw_lora")
    for store, val in zip((grads, delta, new_m, new_v), lora_res):
        store["w2"], store["a2"], store["g2"] = val[None, :lora], val[None, lora:2 * lora], val[None, 2 * lora:]
    vec_res = _adamw(vec_parts, _pack_vectors(weights), _pack_vectors(mom1), _pack_vectors(mom2), "adamw_vectors")
    for store, val in zip((grads, delta, new_m, new_v), vec_res):
        store.update(_unpack_vectors(val, weights))

    return (loss, dx.reshape(x.shape), *[grads[n] for n in _WEIGHT_NAMES], *[delta[n] for n in _WEIGHT_NAMES],
            *[new_m[n] for n in _WEIGHT_NAMES], *[new_v[n] for n in _WEIGHT_NAMES])
```

```python
import functools
import math

import jax
import jax.numpy as jnp
from jax import lax
from jax.experimental import pallas as pl
from jax.experimental.pallas import tpu as pltpu

F32 = jnp.float32
BF16 = jnp.bfloat16

N_DEV = 8
HEAD_DIM = 64
LANES = 128
N_ATTN_HEADS = 8
ATTN_WIDTH = 512
KV_WIDTH = 128
QKV_WIDTH = ATTN_WIDTH + 2 * KV_WIDTH
RWKV_WIDTH = 512
LORA_WA = 128
GATE_LORA = 128
RWKV_SHIFT_WIDTH = 3 * RWKV_WIDTH + LORA_WA + GATE_LORA
BLOCK = 128
CHUNK = 64
RMS_EPS = 1e-6
GN_EPS = 64e-5
L2_EPS = 1e-12
NEG_INF = -1e30
DECAY_SCALE = math.exp(-0.5)
ADAM_LR, ADAM_B1, ADAM_B2, ADAM_EPS, ADAM_WD, ADAM_STEP = 0.001, 0.9, 0.999, 1e-08, 0.01, 10

NN = (((1,), (0,)), ((), ()))
NT = (((1,), (1,)), ((), ()))
TN = (((0,), (0,)), ((), ()))
MESH = pl.DeviceIdType.MESH


def _dot(a, b, dn=NN, precision=None):
    return lax.dot_general(a, b, dn, precision=precision, preferred_element_type=F32)


def _bdot_raw(a, b, dn):
    return lax.dot_general(a.astype(BF16), b.astype(BF16), dn, preferred_element_type=F32)


@functools.partial(jax.custom_vjp, nondiff_argnums=(2, 3))
def _bdot_c(a, b, ca, cb):
    return _bdot_raw(a, b, (((ca,), (cb,)), ((), ())))


def _bdot_c_fwd(a, b, ca, cb):
    return _bdot_c(a, b, ca, cb), (a, b)


_BESIDE_BACKWARD = [None]


def _beside_backward():
    if _BESIDE_BACKWARD[0] is not None:
        _BESIDE_BACKWARD[0]()


def _bdot_c_bwd(ca, cb, res, ct):
    _beside_backward()
    a, b = res
    fa, fb = 1 - ca, 1 - cb
    da = _bdot_raw(ct, b, (((1,), (fb,)), ((), ()))) if ca == 1 else _bdot_raw(b, ct, (((fb,), (1,)), ((), ())))
    db = _bdot_raw(a, ct, (((fa,), (0,)), ((), ()))) if cb == 0 else _bdot_raw(ct, a, (((0,), (fa,)), ((), ())))
    return da, db


_bdot_c.defvjp(_bdot_c_fwd, _bdot_c_bwd)


def _bdot(a, b, dn=NN):
    return _bdot_c(a, b, dn[0][0][0], dn[0][1][0])


def _bdot_nn(a, b):
    return _bdot(a, b, NN)


def _split3(x):
    hi = x.astype(BF16)
    rest = x - hi.astype(F32)
    mid = rest.astype(BF16)
    return hi, mid, (rest - mid.astype(F32)).astype(BF16)


def _running_sum(x, dn):
    c = x.shape[0]
    row = lax.broadcasted_iota(jnp.int32, (c, c), 0)
    col = lax.broadcasted_iota(jnp.int32, (c, c), 1)
    tri = jnp.where(row >= col, 1.0, 0.0).astype(BF16)
    w = x.shape[1]
    parts = lax.dot_general(tri, jnp.concatenate(_split3(x), axis=1), dn, preferred_element_type=F32)
    return parts[:, :w] + parts[:, w:2 * w] + parts[:, 2 * w:]


@jax.custom_vjp
def _cumsum_rows(x):
    return _running_sum(x, NN)


_cumsum_rows.defvjp(lambda x: (_running_sum(x, NN), None), lambda _, ct: (_running_sum(ct, TN),))


@jax.custom_vjp
def _fold_rows(x):
    c = x.shape[0] // 2
    return x[:c] + x[c:]


_fold_rows.defvjp(lambda x: (_fold_rows(x), None), lambda _, ct: (jnp.concatenate([ct, ct], axis=0),))


@jax.custom_vjp
def _halves(x):
    n = x.shape[0] // 2
    return x[:n], x[n:]


_halves.defvjp(lambda x: (_halves(x), None), lambda _, cts: (jnp.concatenate(cts, axis=0),))


@jax.custom_vjp
def _quarters(x):
    n = x.shape[0] // 2
    return x[:n, :n], x[:n, n:], x[n:, :n], x[n:, n:]


_quarters.defvjp(lambda x: (_quarters(x), None),
                 lambda _, cts: (jnp.concatenate([jnp.concatenate(cts[:2], axis=1),
                                                  jnp.concatenate(cts[2:], axis=1)], axis=0),))


@jax.custom_vjp
def _sigmoid(x):
    return 1.0 / (1.0 + jnp.exp(-x))


def _sigmoid_fwd(x):
    s = _sigmoid(x)
    return s, s


_sigmoid.defvjp(_sigmoid_fwd, lambda s, ct: (ct * s * (1.0 - s),))


def _pick(n, cands):
    for c in cands:
        if n % c == 0:
            return c
    return n


def _cparams(sem, vmem_mb=None):
    kw = dict(dimension_semantics=sem)
    if vmem_mb is not None:
        kw["vmem_limit_bytes"] = vmem_mb * 1024 * 1024
    return pltpu.CompilerParams(**kw)


def _matmul(a, b, mode, *, name, extras=(), epilogue=None, out_dtypes=(F32,), tm=1024, tn=1024, tk=1024,
            hosted=(), scatter=False, a_map=lambda tile: tile):
    if mode == "nn":
        (M, K), (_, N) = a.shape, b.shape
    elif mode == "tn":
        (K, M), (_, N) = a.shape, b.shape
    else:
        (M, K), (N, _) = a.shape, b.shape
    tm = _pick(M, (tm, 512, 256, 128))
    tn = _pick(N, (tn, 896, 768, 512, 384, 256, 128))
    tk = _pick(K, (tk, 512, 256, 128))
    nk = K // tk
    ne, nout = len(extras), len(out_dtypes)
    if mode == "nn":
        a_spec = pl.BlockSpec((tm, tk), lambda i, j, k: (i, k))
        b_spec = pl.BlockSpec((tk, tn), lambda i, j, k: (k, j))
        dn = NN
    elif mode == "tn":
        a_spec = pl.BlockSpec((tk, tm), lambda i, j, k: (k, i))
        b_spec = pl.BlockSpec((tk, tn), lambda i, j, k: (k, j))
        dn = TN
    else:
        a_spec = pl.BlockSpec((tm, tk), lambda i, j, k: (i, k))
        b_spec = pl.BlockSpec((tn, tk), lambda i, j, k: (j, k))
        dn = NT
    o_spec = pl.BlockSpec((tm, tn), lambda i, j, k: (i, j))
    grid = (M // tm, N // tn, nk)
    nex = len(hosted)

    def body(*refs):
        a_ref, b_ref = refs[:2]
        e_refs = refs[2:2 + ne]
        ex_in = refs[2 + ne:2 + ne + nex]
        o_refs = refs[2 + ne + nex:2 + ne + nex + nout]
        ex_out = refs[2 + ne + nex + nout:2 + ne + 2 * nex + nout]
        scratch = refs[2 + ne + 2 * nex + nout:]
        kstep = pl.program_id(2)
        if nex:
            at = [pl.program_id(d) for d in range(3)]
            first = jnp.logical_and(jnp.logical_and(at[0] == 0, at[1] == 0), at[2] == 0)
            last = jnp.logical_and(jnp.logical_and(at[0] == grid[0] - 1, at[1] == grid[1] - 1), at[2] == grid[2] - 1)
            _hosted_exchange(first, last, ex_in, ex_out, scratch[-3:], scatter)

        def finish(total):
            outs = (total,) if epilogue is None else epilogue(total, *[e[...] for e in e_refs])
            for o_ref, o in zip(o_refs, outs):
                o_ref[...] = o.astype(o_ref.dtype)

        if nk == 1:
            finish(_bdot_raw(a_map(a_ref[...]), b_ref[...], dn))
            return
        acc = scratch[0]

        @pl.when(kstep == 0)
        def _():
            acc[...] = jnp.zeros_like(acc)

        acc[...] += _bdot_raw(a_map(a_ref[...]), b_ref[...], dn)

        @pl.when(kstep == nk - 1)
        def _():
            finish(acc[...])

    hbm = pl.BlockSpec(memory_space=pltpu.HBM)
    outs = pl.pallas_call(
        body,
        grid=grid,
        in_specs=[a_spec, b_spec] + [o_spec] * ne + [hbm] * nex,
        out_specs=[o_spec] * nout + [hbm] * nex,
        out_shape=[jax.ShapeDtypeStruct((M, N), dt) for dt in out_dtypes] + _exchange_out_shapes(hosted, scatter),
        scratch_shapes=([pltpu.VMEM((tm, tn), F32)] if nk > 1 else []) + (_exchange_scratch(nex) if nex else []),
        compiler_params=_cparams(("arbitrary",) * 3 if nex else ("parallel", "parallel", "arbitrary"), 56),
        name=name,
    )(a, b, *extras, *hosted)
    if nex:
        return outs[:nout], outs[nout:]
    return outs[0] if nout == 1 else outs


def _rowwise(fn, rows, pars, out_rows, out_accs, *, tile, name, nsub=1, hosted=(), scatter=False):
    rows = [r if isinstance(r, tuple) else (r, r.shape[1], 0) for r in rows]
    R = rows[0][0].shape[0]
    tile = min(tile, R)
    chunk = tile // nsub
    ntile = R // tile
    nr, npar, nor, noa, nex = len(rows), len(pars), len(out_rows), len(out_accs), len(hosted)

    def body(*refs):
        rin = refs[:nr]
        pin = refs[nr:nr + npar]
        ex_in = refs[nr + npar:nr + npar + nex]
        orow = refs[nr + npar + nex:nr + npar + nex + nor]
        oacc = refs[nr + npar + nex + nor:nr + npar + nex + nor + noa]
        ex_out = refs[nr + npar + nex + nor + noa:nr + npar + 2 * nex + nor + noa]
        step = pl.program_id(0)
        _hosted_exchange(step == 0, step == ntile - 1, ex_in, ex_out, refs[nr + npar + 2 * nex + nor + noa:], scatter)
        pvals = [p[...] for p in pin]
        totals = []
        for sub in range(nsub):
            at = slice(sub * chunk, (sub + 1) * chunk)
            outs = fn(*[r[at, :] for r in rin], *pvals)
            for ref, o in zip(orow, outs[:nor]):
                if isinstance(o, (tuple, list)):
                    col = 0
                    for piece in o:
                        ref[at, col:col + piece.shape[1]] = piece.astype(ref.dtype)
                        col += piece.shape[1]
                else:
                    ref[at, :] = o.astype(ref.dtype)
            accs = list(outs[nor:])
            totals = accs if sub == 0 else [t + a for t, a in zip(totals, accs)]

        def accumulate(ref, o):
            @pl.when(step == 0)
            def _():
                ref[...] = o

            @pl.when(step > 0)
            def _():
                ref[...] += o

        for ref, o in zip(oacc, totals):
            accumulate(ref, o)

    def colspec(width, cb):
        return pl.BlockSpec((tile, width), lambda i: (i, cb))

    hbm = pl.BlockSpec(memory_space=pltpu.HBM)
    outs = pl.pallas_call(
        body,
        grid=(ntile,),
        in_specs=[colspec(w, cb) for (_, w, cb) in rows]
        + [pl.BlockSpec(p.shape, lambda i: (0, 0), pipeline_mode=pl.Buffered(1)) for p in pars] + [hbm] * nex,
        out_specs=[colspec(w, 0) for (w, _) in out_rows]
        + [pl.BlockSpec(s, lambda i: (0, 0)) for s in out_accs] + [hbm] * nex,
        out_shape=[jax.ShapeDtypeStruct((R, w), dt) for (w, dt) in out_rows]
        + [jax.ShapeDtypeStruct(s, F32) for s in out_accs] + _exchange_out_shapes(hosted, scatter),
        scratch_shapes=_exchange_scratch(nex) if nex else [],
        compiler_params=_cparams(("arbitrary",), 56),
        name=name,
    )(*[r[0] for r in rows], *pars, *hosted)
    return (outs[:nor + noa], outs[nor + noa:]) if nex else outs


def _rms_fn(x, g):
    return x * lax.rsqrt(jnp.mean(x * x, axis=-1, keepdims=True) + RMS_EPS) * g


FUSED_TILE = 512
FUSED_CHUNKS = 2


def _relu_squared(u):
    pos = jnp.maximum(u.astype(F32), 0.0)
    return pos * pos


def _down_proj_loss(u, w_down, x1, tgt, g):
    d = x1.shape[1]

    def fn(uv, xv, tv, wv, gv):
        x2 = xv + _bdot_raw(_relu_squared(uv), wv, NN)
        y, vjp = jax.vjp(_rms_fn, x2, gv)
        err = y - tv
        loss = 0.5 * jnp.sum(jnp.sum(err * err, axis=-1, keepdims=True), axis=0, keepdims=True) / d
        dx, dg = vjp(err / d)
        return dx, dx, jnp.broadcast_to(loss, (1, LANES)), dg

    return _rowwise(fn, [u, x1, tgt], [w_down, g], [(d, F32), (d, BF16)], [(1, LANES), g.shape],
                    tile=FUSED_TILE, nsub=FUSED_CHUNKS, name="mlp_down_final_norm_loss")


def _proj_bwd_norm_bwd(cts, weights_t, x, dres, g, name, hosted=(), scatter=False):
    n = len(cts)

    def fn(*vals):
        ctv, (xv, dresv), wv, gv = vals[:n], vals[n:n + 2], vals[n + 2:2 * n + 2], vals[-1]
        dh = _bdot_raw(ctv[0], wv[0], NT)
        for c, w in zip(ctv[1:], wv[1:]):
            dh = dh + _bdot_raw(c, w, NT)
        _, vjp = jax.vjp(_rms_fn, xv, gv)
        dx, dg = vjp(dh)
        return dx + dresv, dg

    return _rowwise(fn, [*cts, x, dres], [*weights_t, g], [(x.shape[1], F32)], [g.shape],
                    tile=FUSED_TILE, nsub=FUSED_CHUNKS, name=name, hosted=hosted, scatter=scatter)


def _head_sum_matrix():
    i = lax.broadcasted_iota(jnp.int32, (RWKV_WIDTH, RWKV_WIDTH), 0) // HEAD_DIM
    j = lax.broadcasted_iota(jnp.int32, (RWKV_WIDTH, RWKV_WIDTH), 1) // HEAD_DIM
    return (i == j).astype(BF16)


def _head_sums_raw(x, esum):
    hi = x.astype(BF16)
    lo = (x - hi.astype(F32)).astype(BF16)
    return _dot(hi, esum) + _dot(lo, esum)


@jax.custom_vjp
def _head_sums(x, esum):
    return _head_sums_raw(x, esum)


_head_sums.defvjp(lambda x, esum: (_head_sums_raw(x, esum), esum),
                  lambda esum, ct: (_head_sums_raw(ct, esum), jnp.zeros_like(esum)))


def _prep_core(xr, xk, xv, xwa, xg, w0, w2p, a0, a2p, g2, k_k, k_a, esum):
    lw = -DECAY_SCALE * _sigmoid(w0 + _bdot_nn(jnp.tanh(xwa), w2p))
    a = _sigmoid(a0 + _bdot_nn(xwa, a2p))
    g = _bdot_nn(_sigmoid(xg), g2)
    kk0 = xk * k_k
    kk = kk0 * jnp.minimum(lax.rsqrt(_head_sums(kk0 * kk0, esum)), 1.0 / L2_EPS)
    k = xk * (1.0 + (a - 1.0) * k_a)
    return xr, lw, k, xv, kk, a, g


_SEGS = ((0, 512), (512, 1024), (1024, 1536), (1536, 1664), (1664, 1792))


PREP_TILE = 256
SUBLANES = 8


def _shifted_tokens(z_ref, zprev_ref, tile_index, seq):
    zc = z_ref[...]
    start = (tile_index * PREP_TILE) % seq == 0
    before = jnp.where(start, 0.0, zprev_ref[SUBLANES - 1:SUBLANES, :])
    rowid = lax.broadcasted_iota(jnp.int32, zc.shape, 0)
    return zc, jnp.where(rowid == 0, before, pltpu.roll(zc, 1, 0))


def _prep_specs(z, mu, pars, index):
    width = z.shape[1]
    per = PREP_TILE // SUBLANES
    return ([pl.BlockSpec((PREP_TILE, width), lambda i: (index(i), 0)),
             pl.BlockSpec((SUBLANES, width), lambda i: (jnp.maximum(index(i) * per - 1, 0), 0))],
            [pl.BlockSpec(p.shape, lambda i: (0, 0)) for p in (mu, *pars)])


def _norm_in_proj_prep(x, seq, g, w_attn, w_rw, mu, pars):
    rows, d = x.shape
    chunk = FUSED_TILE // FUSED_CHUNKS
    npar = len(pars)
    wa_width, wr_width = w_attn.shape[1], w_rw.shape[1]

    def body(x_ref, g_ref, wa_ref, wr_ref, mu_ref, *rest):
        par_refs = rest[:npar]
        h_ref, za_ref, zr_ref = rest[npar:npar + 3]
        out_refs, carry = rest[npar + 3:-1], rest[-1]
        step = pl.program_id(0)

        @pl.when(step == 0)
        def _():
            carry[...] = jnp.zeros_like(carry)

        pv = [p[...] for p in par_refs]
        for sub in range(FUSED_CHUNKS):
            at = slice(sub * chunk, (sub + 1) * chunk)
            h = _rms_fn(x_ref[at, :], g_ref[...])
            h_ref[at, :] = h.astype(h_ref.dtype)
            za_ref[at, :] = _bdot_raw(h, wa_ref[...], NN).astype(za_ref.dtype)
            zc = _bdot_raw(h, wr_ref[...], NN)
            zr_ref[at, :] = zc
            start = (step * FUSED_TILE + sub * chunk) % seq == 0
            before = jnp.where(start, 0.0, carry[SUBLANES - 1:SUBLANES, :])
            rowid = lax.broadcasted_iota(jnp.int32, zc.shape, 0)
            zp = jnp.where(rowid == 0, before, pltpu.roll(zc, 1, 0))
            carry[...] = zc[chunk - SUBLANES:chunk, :]
            zs = zc + (zp - zc) * mu_ref[...]
            outs = _prep_core(*[zs[:, a:b] for a, b in _SEGS], *pv)
            for ref, o in zip(out_refs, outs):
                ref[at, :] = o

    tiled = lambda width: pl.BlockSpec((FUSED_TILE, width), lambda i: (i, 0))
    resident = lambda a: pl.BlockSpec(a.shape, lambda i: (0, 0), pipeline_mode=pl.Buffered(1))
    return pl.pallas_call(
        body,
        grid=(rows // FUSED_TILE,),
        in_specs=[tiled(d)] + [resident(a) for a in (g, w_attn, w_rw, mu, *pars)],
        out_specs=[tiled(d), tiled(wa_width), tiled(wr_width)] + [tiled(RWKV_WIDTH)] * 7,
        out_shape=[jax.ShapeDtypeStruct((rows, d), BF16), jax.ShapeDtypeStruct((rows, wa_width), BF16),
                   jax.ShapeDtypeStruct((rows, wr_width), F32)] + [jax.ShapeDtypeStruct((rows, RWKV_WIDTH), F32)] * 7,
        scratch_shapes=[pltpu.VMEM((SUBLANES, wr_width), F32)],
        compiler_params=_cparams(("arbitrary",), 56),
        name="attn_norm_in_proj_rwkv_prep",
    )(x, g, w_attn, w_rw, mu, *pars)


def _prep_bwd(z, seq, cts, mu, pars):
    rows, width = z.shape
    ntile = rows // PREP_TILE
    npar, nct = len(pars), len(cts)
    acc_shapes = [(1, b - a) for a, b in _SEGS] + [p.shape for p in pars[:-1]]

    def body(z_ref, zprev_ref, *rest):
        ct_refs = rest[:nct]
        mu_ref = rest[nct]
        par_refs = rest[nct + 1:nct + 1 + npar]
        dz_ref = rest[nct + 1 + npar]
        acc_refs = rest[nct + 2 + npar:-1]
        carry = rest[-1]
        step = pl.program_id(0)
        tile_index = ntile - 1 - step

        @pl.when(step == 0)
        def _():
            carry[...] = jnp.zeros_like(carry)

        zc, zp = _shifted_tokens(z_ref, zprev_ref, tile_index, seq)
        mu_v = mu_ref[...]
        diff = zp - zc
        zs = zc + diff * mu_v
        dra, drb, dlw, dka, dkb, dva, dvb, dkk, da, dg = [c[...] for c in ct_refs]
        pv = [p[...] for p in par_refs]
        _, vjp = jax.vjp(lambda *args: _prep_core(*args, pv[-1]), *[zs[:, a:b] for a, b in _SEGS], *pv[:-1])
        grads = vjp((dra + drb, dlw, dka + dkb, dva + dvb, dkk, da, dg))
        dsegs, dpars = grads[:5], grads[5:]
        last_of_sequence = ((tile_index + 1) * PREP_TILE) % seq == 0
        accs = []
        for ds, (a, b) in zip(dsegs, _SEGS):
            mu_s = mu_v[:, a:b]
            dzp = ds * mu_s
            after = jnp.where(last_of_sequence, 0.0, carry[0:1, a:b])
            rowid = lax.broadcasted_iota(jnp.int32, dzp.shape, 0)
            from_next = jnp.where(rowid == PREP_TILE - 1, after, pltpu.roll(dzp, PREP_TILE - 1, 0))
            dz_ref[:, a:b] = (ds * (1.0 - mu_s) + from_next).astype(dz_ref.dtype)
            carry[:, a:b] = dzp[0:SUBLANES, :]
            accs.append(jnp.sum(ds * diff[:, a:b], axis=0, keepdims=True))
        accs.extend(dpars)

        def accumulate(ref, o):
            @pl.when(step == 0)
            def _():
                ref[...] = o

            @pl.when(step > 0)
            def _():
                ref[...] += o

        for ref, o in zip(acc_refs, accs):
            accumulate(ref, o)

    rev = lambda i: ntile - 1 - i
    zspecs, pspecs = _prep_specs(z, mu, pars, rev)
    return pl.pallas_call(
        body,
        grid=(ntile,),
        in_specs=zspecs + [pl.BlockSpec((PREP_TILE, RWKV_WIDTH), lambda i: (rev(i), 0))] * nct + pspecs,
        out_specs=[pl.BlockSpec((PREP_TILE, width), lambda i: (rev(i), 0))]
        + [pl.BlockSpec(s, lambda i: (0, 0)) for s in acc_shapes],
        out_shape=[jax.ShapeDtypeStruct((rows, width), BF16)] + [jax.ShapeDtypeStruct(s, F32) for s in acc_shapes],
        scratch_shapes=[pltpu.VMEM((SUBLANES, width), F32)],
        compiler_params=_cparams(("arbitrary",), 56),
        name="rwkv_prep_bwd",
    )(z, z, *cts, mu, *pars)


def _post_fn(y, r, k, v, g, ln_w, ln_b, r_k, esum):
    mean = _head_sums(y, esum) * (1.0 / HEAD_DIM)
    yc = y - mean
    var = _head_sums(yc * yc, esum) * (1.0 / HEAD_DIM)
    yn = yc * lax.rsqrt(var + GN_EPS) * ln_w + ln_b
    bonus = _head_sums(r * k * r_k, esum) * v
    return (yn + bonus) * g


def _post_out_proj_norm(y, r, k, v, g, attn_out, x, pars, w_attn_rows, w_rwkv_rows, g_norm):
    npar = len(pars)

    def fn(yv, rv, kv, vv, gv, av, xv, *rest):
        wa, wr, gn = rest[npar:]
        rw = _post_fn(yv, rv, kv, vv, gv, *rest[:npar])
        x1 = xv + _bdot_raw(av, wa, NN) + _bdot_raw(rw, wr, NN)
        return rw, x1, _rms_fn(x1, gn)

    d = x.shape[1]
    return _rowwise(fn, [y, r, k, v, g, attn_out, x], [*pars, w_attn_rows, w_rwkv_rows, g_norm],
                    [(RWKV_WIDTH, BF16), (d, F32), (d, BF16)], [],
                    tile=FUSED_TILE, nsub=FUSED_CHUNKS, name="rwkv_post_out_proj_mlp_norm")


def _out_proj_bwd_post_bwd(dx1, y, r, k, v, g, pars, w_attn_rows, w_rwkv_rows):
    npar = len(pars)

    def fn(dxv, yv, rv, kv, vv, gv, *rest):
        wa, wr = rest[npar:]
        esum = rest[npar - 1]
        d_attn = _bdot_raw(dxv, wa, NT)
        d_rw = _bdot_raw(dxv, wr, NT)
        _, vjp = jax.vjp(lambda *a: _post_fn(*a, esum), yv, rv, kv, vv, gv, *rest[:npar - 1])
        return (d_attn, *vjp(d_rw))

    return _rowwise(fn, [dx1, y, r, k, v, g], [*pars, w_attn_rows, w_rwkv_rows],
                    [(RWKV_WIDTH, BF16)] * 2 + [(RWKV_WIDTH, F32)] * 4,
                    [p.shape for p in pars[:-1]], tile=FUSED_TILE, nsub=FUSED_CHUNKS, name="out_proj_bwd_rwkv_post_bwd")


def _tri_inverses(ms, tick=lambda: None):
    n = ms[0].shape[0]
    row = lax.broadcasted_iota(jnp.int32, (n, n), 0)
    col = lax.broadcasted_iota(jnp.int32, (n, n), 1)
    eye = jnp.where(row == col, 1.0, 0.0)
    t_inv = [eye + m for m in ms]
    power = [_bdot_raw(m, m, NN) for m in ms]
    steps = int(math.log2(n // 2)) - 1
    for step in range(steps):
        if step < steps - 1:
            both = [_bdot_raw(jnp.concatenate([p, t], axis=0), p, NN) for p, t in zip(power, t_inv)]
            power = [b[:n] for b in both]
            t_inv = [t + b[n:] for t, b in zip(t_inv, both)]
        else:
            t_inv = [t + _bdot_raw(t, p, NN) for t, p in zip(t_inv, power)]
        tick()
    return t_inv


def _tri_solve_bwd(res, dus):
    t_inv, us = res
    dxs = tuple(_bdot_raw(t, du, TN) for t, du in zip(t_inv, dus))
    dms = tuple(_bdot_raw(dx, u, NT) for dx, u in zip(dxs, us))
    return dms, dxs


@functools.partial(jax.custom_vjp, nondiff_argnums=(3, 4))
def _kept_bdot_c(a, b, kept, ca, cb):
    return kept.astype(F32)


def _kept_bdot_c_fwd(a, b, kept, ca, cb):
    return kept.astype(F32), (a, b, kept)


def _kept_bdot_c_bwd(ca, cb, res, ct):
    a, b, kept = res
    return (*_bdot_c_bwd(ca, cb, (a, b), ct), jnp.zeros_like(kept))


_kept_bdot_c.defvjp(_kept_bdot_c_fwd, _kept_bdot_c_bwd)


def _kept_bdot(a, b, kept, dn):
    return _kept_bdot_c(a, b, kept, dn[0][0][0], dn[0][1][0])


@jax.custom_vjp
def _kept_tri_solve(ms, xs, t_inv, us):
    return tuple(u.astype(F32) for u in us)


def _kept_tri_solve_fwd(ms, xs, t_inv, us):
    return tuple(u.astype(F32) for u in us), (t_inv, us)


def _kept_tri_solve_bwd(res, dus):
    t_inv, us = res
    dms, dxs = _tri_solve_bwd(res, dus)
    return dms, dxs, tuple(jnp.zeros_like(t) for t in t_inv), tuple(jnp.zeros_like(u) for u in us)


_kept_tri_solve.defvjp(_kept_tri_solve_fwd, _kept_tri_solve_bwd)


def _chunk_fn(ss, rs, lws, ks, vs, kks, als, kept=None, tick=lambda: None):
    c = rs[0].shape[0]
    n = 2 * c
    row = lax.broadcasted_iota(jnp.int32, (n, n), 0)
    col = lax.broadcasted_iota(jnp.int32, (n, n), 1)
    incl = (row % c) >= (col % c)
    strict = (row % c) > (col % c)
    lane = lax.broadcasted_iota(jnp.int32, (1, LANES), 1)
    m_lo = jnp.where(lane < HEAD_DIM, 1.0, 0.0)
    m_hi = 1.0 - m_lo

    def stack(a):
        return jnp.concatenate([a * m_lo, a * m_hi], axis=0)

    cums = [_cumsum_rows(lw) for lw in lws]
    totals = [jnp.sum(lw, axis=0, keepdims=True) for lw in lws]
    bs = [kk * al for kk, al in zip(kks, als)]
    grows = [jnp.exp(-cum) for cum in cums]
    a_s = [stack(-kk * jnp.exp(cum - lw)) for kk, cum, lw in zip(kks, cums, lws)]
    b_s = [stack(b * g) for b, g in zip(bs, grows)]
    k_s = [stack(k * g) for k, g in zip(ks, grows)]
    r_s = [stack(r * jnp.exp(cum)) for r, cum in zip(rs, cums)]
    v_s = [stack(v) for v in vs]
    tick()
    pair = lambda p, q: jnp.concatenate([p, q], axis=0)
    ar_s = [pair(a, r) for a, r in zip(a_s, r_s)]
    if kept is None:
        products = [_bdot_raw(ar, pair(b, k), NT) for ar, b, k in zip(ar_s, b_s, k_s)]
    else:
        products = [_kept_bdot(ar, pair(b, k), kp, NT) for ar, b, k, kp in zip(ar_s, b_s, k_s, kept[0])]
    tick()
    blocks = [_quarters(p) for p in products]
    m_ab = [jnp.where(strict, q[0], 0.0) for q in blocks]
    m_ak = [jnp.where(strict, q[1], 0.0) for q in blocks]
    m_rb = [jnp.where(incl, q[2], 0.0) for q in blocks]
    m_rk = [jnp.where(incl, q[3], 0.0) for q in blocks]
    from_state = [_halves(_bdot(ar, s, NT)) for ar, s in zip(ar_s, ss)]
    tick()
    from_v = [_halves(_bdot(pair(mk, mr), v)) for mk, mr, v in zip(m_ak, m_rk, v_s)]
    tick()
    x = tuple(fs[0] + fv[0] for fs, fv in zip(from_state, from_v))
    if kept is None:
        t_inv = _tri_inverses(m_ab, tick)
        u = [_bdot_raw(t, xx, NN) for t, xx in zip(t_inv, x)]
        tick()
    else:
        u = _kept_tri_solve(tuple(m_ab), x, kept[1], kept[2])
    y = [_fold_rows(fs[1] + _bdot(mb, uu) + fv[1]) for fs, mb, uu, fv in zip(from_state, m_rb, u, from_v)]
    tick()
    tails = [jnp.exp(tot - cum) for tot, cum in zip(totals, cums)]
    s_new = [s * jnp.exp(tot) + _bdot(pair(uu, v), pair(stack(b * tl), stack(k * tl)), TN)
             for s, tot, uu, b, tl, v, k in zip(ss, totals, u, bs, tails, v_s, ks)]
    if kept is None:
        keep = lambda vals: tuple(v.astype(BF16) for v in vals)
        return tuple(y), tuple(s_new), (keep(products), keep(t_inv), keep(u))
    return tuple(y), tuple(s_new)


def _chains(bsz, npair):
    return [(b, p, slice(p * LANES, (p + 1) * LANES)) for b in range(bsz) for p in range(npair)]


def _hosted_exchange(first, last, ex_in, ex_out, sems, scatter):
    if not ex_in:
        return

    @pl.when(first)
    def _():
        _exchange_start(_exchange_copies(ex_in, ex_out, *sems, scatter, arrivals=False))

    @pl.when(last)
    def _():
        _exchange_wait(_exchange_copies(ex_in, ex_out, *sems, scatter, arrivals=True))


def _rwkv_attn_fwd(r, lw, k, v, kk, al, z, sink_rows, hosted=(), scatter=False):
    bsz, t, w = r.shape
    npair, nchunk = w // LANES, t // CHUNK
    nb = t // BLOCK
    assert bsz * nb == nchunk
    chains = _chains(bsz, npair)
    nex = len(hosted)
    apair = ATTN_WIDTH // LANES

    def body(*refs):
        r_ref, lw_ref, k_ref, v_ref, kk_ref, al_ref = refs[:6]
        q_ref, kp_ref, kc_ref, vp_ref, vc_ref, sink_ref = refs[6:12]
        ex_in = refs[12:12 + nex]
        y_ref, sall_ref, prod_ref, tinv_ref, u_ref, o_ref, p_ref, ps_ref = refs[12 + nex:20 + nex]
        ex_out = refs[20 + nex:20 + 2 * nex]
        s_scr = refs[20 + 2 * nex]
        step = pl.program_id(0)

        @pl.when(step == 0)
        def _():
            s_scr[...] = jnp.zeros_like(s_scr)

        _hosted_exchange(step == 0, step == nchunk - 1, ex_in, ex_out, refs[21 + 2 * nex:], scatter)
        qs = tuple(q_ref[0, :, pair * LANES:(pair + 1) * LANES] for pair in range(apair))
        attn = []
        stages = _attn_block_stages(qs, kp_ref[0], kc_ref[0], vp_ref[0], vc_ref[0], _sink_values(sink_ref),
                                    step % nb == 0, attn)
        ss = tuple(s_scr[i] for i in range(len(chains)))
        for i, s in enumerate(ss):
            sall_ref[0, i] = s
        ys, s_new, kept = _chunk_fn(ss, *[tuple(ref[b, :, cols] for b, _, cols in chains)
                                          for ref in (r_ref, lw_ref, k_ref, v_ref, kk_ref, al_ref)],
                                    tick=lambda: next(stages, None))
        for _ in stages:
            pass
        for i, (b, _, cols) in enumerate(chains):
            y_ref[b, :, cols] = ys[i]
            s_scr[i] = s_new[i]
            prod_ref[0, i], tinv_ref[0, i], u_ref[0, i] = kept[0][i], kept[1][i], kept[2][i]
        outs, probs, psinks = attn
        for pair in range(apair):
            o_ref[0, :, pair * LANES:(pair + 1) * LANES] = outs[pair].astype(o_ref.dtype)
        for h, p in enumerate(probs):
            p_ref[0, 0, h] = p
        ps_ref[0] = _head_columns(psinks)

    spec = pl.BlockSpec((bsz, CHUNK, w), lambda c: (0, c, 0))
    hbm = pl.BlockSpec(memory_space=pltpu.HBM)
    per_chunk = lambda n: pl.BlockSpec((1, len(chains), n, n), lambda c: (c, 0, 0, 0))
    kept_shape = lambda n: jax.ShapeDtypeStruct((nchunk, len(chains), n, n), BF16)
    kcol, vcol = ATTN_WIDTH // KV_WIDTH, ATTN_WIDTH // KV_WIDTH + 1
    before = lambda c: jnp.maximum(c % nb - 1, 0)
    outs = pl.pallas_call(
        body,
        grid=(nchunk,),
        in_specs=[spec] * 6
        + [pl.BlockSpec((1, BLOCK, ATTN_WIDTH), lambda c: (c // nb, c % nb, 0)),
           pl.BlockSpec((1, BLOCK, KV_WIDTH), lambda c: (c // nb, before(c), kcol)),
           pl.BlockSpec((1, BLOCK, KV_WIDTH), lambda c: (c // nb, c % nb, kcol)),
           pl.BlockSpec((1, BLOCK, KV_WIDTH), lambda c: (c // nb, before(c), vcol)),
           pl.BlockSpec((1, BLOCK, KV_WIDTH), lambda c: (c // nb, c % nb, vcol)),
           pl.BlockSpec(sink_rows.shape, lambda c: (0, 0))]
        + [hbm] * nex,
        out_specs=[spec, per_chunk(LANES), per_chunk(4 * CHUNK), per_chunk(2 * CHUNK), per_chunk(2 * CHUNK),
                   pl.BlockSpec((1, BLOCK, ATTN_WIDTH), lambda c: (c // nb, c % nb, 0)),
                   pl.BlockSpec((1, 1, N_ATTN_HEADS, BLOCK, 2 * BLOCK), lambda c: (c // nb, c % nb, 0, 0, 0)),
                   pl.BlockSpec((1, BLOCK, LANES), lambda c: (c // nb, c % nb, 0))]
        + [hbm] * nex,
        out_shape=[jax.ShapeDtypeStruct((bsz, t, w), F32),
                   jax.ShapeDtypeStruct((nchunk, len(chains), LANES, LANES), F32),
                   kept_shape(4 * CHUNK), kept_shape(2 * CHUNK), kept_shape(2 * CHUNK),
                   jax.ShapeDtypeStruct((bsz, t, ATTN_WIDTH), BF16),
                   jax.ShapeDtypeStruct((bsz, nb, N_ATTN_HEADS, BLOCK, 2 * BLOCK), F32),
                   jax.ShapeDtypeStruct((bsz, t, LANES), F32)]
        + _exchange_out_shapes(hosted, scatter),
        scratch_shapes=[pltpu.VMEM((len(chains), LANES, LANES), F32)] + (_exchange_scratch(nex) if nex else []),
        compiler_params=_cparams(("arbitrary",), 48),
        name="rwkv_chunk_swa_fwd",
    )(r, lw, k, v, kk, al, z, z, z, z, z, sink_rows, *hosted)
    return outs[0], outs[1:5], outs[5:8], outs[8:]


def _rwkv_attn_bwd(r, lw, k, v, kk, al, from_fwd, dy, z, dout, probs, psinks, hosted=(), scatter=False):
    bsz, t, w = r.shape
    npair, nchunk = w // LANES, t // CHUNK
    nb = t // BLOCK
    assert bsz * nb == nchunk
    chains = _chains(bsz, npair)
    nex = len(hosted)
    apair = ATTN_WIDTH // LANES

    def body(*refs):
        r_ref, lw_ref, k_ref, v_ref, kk_ref, al_ref, s_ref, prod_ref, tinv_ref, u_ref, dy_ref = refs[:11]
        q_ref, kp_ref, kc_ref, vp_ref, vc_ref, do_ref, p_ref, ps_ref = refs[11:19]
        ex_in = refs[19:19 + nex]
        out_refs = refs[19 + nex:25 + nex]
        dz_ref, dsink_ref = refs[25 + nex:27 + nex]
        ex_out = refs[27 + nex:27 + 2 * nex]
        ds_scr, carry = refs[27 + 2 * nex:29 + 2 * nex]
        step = pl.program_id(0)

        @pl.when(step == 0)
        def _():
            ds_scr[...] = jnp.zeros_like(ds_scr)
            dsink_ref[...] = jnp.zeros_like(dsink_ref)

        @pl.when(step % nb == 0)
        def _():
            carry[...] = jnp.zeros_like(carry)

        _hosted_exchange(step == 0, step == nchunk - 1, ex_in, ex_out, refs[29 + 2 * nex:], scatter)
        qs = tuple(q_ref[0, :, pair * LANES:(pair + 1) * LANES] for pair in range(apair))
        dos = tuple(do_ref[0, :, pair * LANES:(pair + 1) * LANES] for pair in range(apair))
        attn = []
        stages = _attn_block_bwd_stages(qs, kp_ref[0], kc_ref[0], vp_ref[0], vc_ref[0], dos,
                                        [p_ref[0, 0, h] for h in range(N_ATTN_HEADS)], step % nb == nb - 1, attn)
        ss = tuple(s_ref[0, i] for i in range(len(chains)))
        kept = tuple(tuple(ref[0, i] for i in range(len(chains))) for ref in (prod_ref, tinv_ref, u_ref))
        _, vjp = jax.vjp(functools.partial(_chunk_fn, kept=kept), ss,
                         *[tuple(ref[b, :, cols] for b, _, cols in chains)
                           for ref in (r_ref, lw_ref, k_ref, v_ref, kk_ref, al_ref)])
        rules = [0]

        def one_stage_per_round():
            rules[0] += 1
            if rules[0] % len(chains) == 0:
                next(stages, None)

        _BESIDE_BACKWARD[0] = one_stage_per_round
        try:
            grads = vjp((tuple(dy_ref[b, :, cols].astype(F32) for b, _, cols in chains),
                         tuple(ds_scr[i] for i in range(len(chains)))))
        finally:
            _BESIDE_BACKWARD[0] = None
        for _ in stages:
            pass
        for i, (b, _, cols) in enumerate(chains):
            ds_scr[i] = grads[0][i]
            for ref, gval in zip(out_refs, grads[1:]):
                ref[b, :, cols] = gval[i]
        dqs, dkp, dkc, dvp, dvc, deltas = attn
        for pair in range(apair):
            dz_ref[0, :, pair * LANES:(pair + 1) * LANES] = dqs[pair].astype(dz_ref.dtype)
        dsink_ref[...] -= jnp.sum(ps_ref[0] * _head_columns(deltas), axis=0, keepdims=True)
        dz_ref[0, :, ATTN_WIDTH:ATTN_WIDTH + KV_WIDTH] = (dkc + carry[0]).astype(dz_ref.dtype)
        dz_ref[0, :, ATTN_WIDTH + KV_WIDTH:QKV_WIDTH] = (dvc + carry[1]).astype(dz_ref.dtype)
        carry[0] = dkp
        carry[1] = dvp

    spec = pl.BlockSpec((bsz, CHUNK, w), lambda c: (0, nchunk - 1 - c, 0))
    per_chunk = lambda n: pl.BlockSpec((1, len(chains), n, n), lambda c: (nchunk - 1 - c, 0, 0, 0))
    hbm = pl.BlockSpec(memory_space=pltpu.HBM)
    kcol, vcol = ATTN_WIDTH // KV_WIDTH, ATTN_WIDTH // KV_WIDTH + 1
    seq_of = lambda c: c // nb
    blk = lambda c: nb - 1 - c % nb
    before = lambda c: jnp.maximum(blk(c) - 1, 0)
    outs = pl.pallas_call(
        body,
        grid=(nchunk,),
        in_specs=[spec] * 6 + [per_chunk(LANES), per_chunk(4 * CHUNK), per_chunk(2 * CHUNK), per_chunk(2 * CHUNK), spec]
        + [pl.BlockSpec((1, BLOCK, ATTN_WIDTH), lambda c: (seq_of(c), blk(c), 0)),
           pl.BlockSpec((1, BLOCK, KV_WIDTH), lambda c: (seq_of(c), before(c), kcol)),
           pl.BlockSpec((1, BLOCK, KV_WIDTH), lambda c: (seq_of(c), blk(c), kcol)),
           pl.BlockSpec((1, BLOCK, KV_WIDTH), lambda c: (seq_of(c), before(c), vcol)),
           pl.BlockSpec((1, BLOCK, KV_WIDTH), lambda c: (seq_of(c), blk(c), vcol)),
           pl.BlockSpec((1, BLOCK, ATTN_WIDTH), lambda c: (seq_of(c), blk(c), 0)),
           pl.BlockSpec((1, 1, N_ATTN_HEADS, BLOCK, 2 * BLOCK), lambda c: (seq_of(c), blk(c), 0, 0, 0)),
           pl.BlockSpec((1, BLOCK, LANES), lambda c: (seq_of(c), blk(c), 0))]
        + [hbm] * nex,
        out_specs=[spec] * 6
        + [pl.BlockSpec((1, BLOCK, QKV_WIDTH), lambda c: (seq_of(c), blk(c), 0)),
           pl.BlockSpec((1, LANES), lambda c: (0, 0))]
        + [hbm] * nex,
        out_shape=[jax.ShapeDtypeStruct((bsz, t, w), F32)] * 6
        + [jax.ShapeDtypeStruct((bsz, t, QKV_WIDTH), BF16), jax.ShapeDtypeStruct((1, LANES), F32)]
        + _exchange_out_shapes(hosted, scatter),
        scratch_shapes=[pltpu.VMEM((len(chains), LANES, LANES), F32), pltpu.VMEM((2, BLOCK, KV_WIDTH), F32)]
        + (_exchange_scratch(nex) if nex else []),
        compiler_params=_cparams(("arbitrary",), 48),
        name="rwkv_chunk_swa_bwd",
    )(r, lw, k, v, kk, al, *from_fwd, dy, z, z, z, z, z, dout, probs, psinks, *hosted)
    return outs[:6], outs[6:8], outs[8:]


def _alibi_slope(head):
    return 2.0 ** (-8.0 * (head + 1) / N_ATTN_HEADS)


def _attn_setup(first):
    row = lax.broadcasted_iota(jnp.int32, (BLOCK, 2 * BLOCK), 0)
    col = lax.broadcasted_iota(jnp.int32, (BLOCK, 2 * BLOCK), 1)
    lane = lax.broadcasted_iota(jnp.int32, (1, LANES), 1)
    halves = [jnp.where((lane // HEAD_DIM) == half, 1.0, 0.0) for half in range(2)]
    srow = lax.broadcasted_iota(jnp.int32, (LANES, LANES), 0)
    scol = lax.broadcasted_iota(jnp.int32, (LANES, LANES), 1)
    swap = jnp.where((srow + HEAD_DIM) % LANES == scol, 1.0, 0.0)
    dist = row - col + BLOCK
    valid = jnp.logical_and(jnp.logical_and(dist >= 0, dist < BLOCK),
                            jnp.logical_or(col >= BLOCK, jnp.logical_not(first)))
    return halves, swap, dist.astype(F32), valid, HEAD_DIM ** -0.5


def _attn_keys_values(kp, kc, vp, vc, swap, npair):
    stored = (jnp.concatenate([kp, kc], axis=0), jnp.concatenate([vp, vc], axis=0))
    swapped = tuple(_bdot_raw(t, swap, NN) for t in stored)
    heads = [(pair, half) for pair in range(npair) for half in range(2)]
    return heads, [half == pair // 2 for pair, half in heads], stored, swapped


def _attn_block_stages(qs, kp, kc, vp, vc, sinks, first, result):
    halves, swap, dist, valid, scale = _attn_setup(first)
    heads, as_stored, stored, swapped = _attn_keys_values(kp, kc, vp, vc, swap, len(qs))
    kv = [stored if own else swapped for own in as_stored]
    slopes = [_alibi_slope(2 * pair + half) for pair, half in heads]
    qa = [qs[pair] * halves[half] for pair, half in heads]
    yield
    s = [jnp.where(valid, _bdot_raw(q, t[0], NT) * scale - sl * dist, NEG_INF) for q, t, sl in zip(qa, kv, slopes)]
    yield
    mx = [jnp.maximum(jnp.max(a, axis=-1, keepdims=True), sk) for a, sk in zip(s, sinks)]
    yield
    e = [jnp.exp(a - m) for a, m in zip(s, mx)]
    yield
    es = [jnp.exp(sk - m) for sk, m in zip(sinks, mx)]
    inv = [1.0 / (jnp.sum(a, axis=-1, keepdims=True) + b) for a, b in zip(e, es)]
    yield
    probs = [a * i for a, i in zip(e, inv)]
    yield
    o = [_bdot_raw(p, t[1], NN) for p, t in zip(probs, kv)]
    yield
    outs = tuple(o[2 * pair] * halves[0] + o[2 * pair + 1] * halves[1] for pair in range(len(qs)))
    result.extend([outs, probs, [b * i for b, i in zip(es, inv)]])


def _attn_block_bwd_stages(qs, kp, kc, vp, vc, dos, probs, first, result):
    halves, swap, _, _, scale = _attn_setup(first)
    heads, as_stored, stored, swapped = _attn_keys_values(kp, kc, vp, vc, swap, len(qs))
    kv = [stored if own else swapped for own in as_stored]
    qa = [qs[pair] * halves[half] for pair, half in heads]
    do = [dos[pair] * halves[half] for pair, half in heads]
    yield
    dp = [_bdot_raw(d, t[1], NT) for d, t in zip(do, kv)]
    yield
    delta = [jnp.sum(p * d, axis=-1, keepdims=True) for p, d in zip(probs, dp)]
    yield
    ds = [p * (d - dl) for p, d, dl in zip(probs, dp, delta)]
    yield
    dq = [_bdot_raw(g, t[0], NN) * (scale * halves[half]) for g, t, (_, half) in zip(ds, kv, heads)]
    yield
    dk = [_bdot_raw(g, q, TN) * scale for g, q in zip(ds, qa)]
    yield
    dv = [_bdot_raw(p, d, TN) for p, d in zip(probs, do)]
    yield
    dqs = tuple(dq[2 * pair] + dq[2 * pair + 1] for pair in range(len(qs)))

    def total(parts):
        direct = sum(g for g, own in zip(parts, as_stored) if own)
        return direct + _bdot_raw(sum(g for g, own in zip(parts, as_stored) if not own), swap, NN)

    dk_all, dv_all = total(dk), total(dv)
    result.extend([dqs, dk_all[:BLOCK], dk_all[BLOCK:], dv_all[:BLOCK], dv_all[BLOCK:], delta])


def _sink_values(sink_ref):
    return [jnp.max(sink_ref[h:h + 1, :], axis=-1, keepdims=True) for h in range(N_ATTN_HEADS)]


def _head_columns(cols):
    lane = lax.broadcasted_iota(jnp.int32, (1, LANES), 1)
    return sum(c * jnp.where(lane == h, 1.0, 0.0) for h, c in enumerate(cols))


def _exchange_out_shapes(arrays, scatter):
    return [jax.ShapeDtypeStruct((N_DEV,) + (a.shape[1:] if scatter else a.shape), a.dtype) for a in arrays]


def _exchange_scratch(n):
    return [pltpu.SemaphoreType.DMA((n, N_DEV - 1)), pltpu.SemaphoreType.DMA((n, N_DEV - 1)),
            pltpu.SemaphoreType.DMA((n,))]


def _exchange_copies(ins, outs, send_sems, recv_sems, local_sems, scatter, arrivals=True):
    x, y, c = lax.axis_index("x"), lax.axis_index("y"), lax.axis_index("c")
    me = 4 * x + 2 * y + c
    copies = []
    for i in range(len(ins)):
        own = pltpu.make_async_copy(ins[i].at[me] if scatter else ins[i], outs[i].at[me], local_sems.at[i])
        copies.append((own, None, True))
        for d in range(1, N_DEV):
            px = 1 - x if d & 4 else x
            py = 1 - y if d & 2 else y
            pc = 1 - c if d & 1 else c
            peer = 4 * px + 2 * py + pc
            src = ins[i].at[peer] if scatter else ins[i]
            send = pltpu.make_async_remote_copy(src, outs[i].at[me], send_sems.at[i, d - 1], recv_sems.at[i, d - 1],
                                                device_id=(px, py, pc), device_id_type=MESH)
            recv = pltpu.make_async_remote_copy(src, outs[i].at[peer], send_sems.at[i, d - 1], recv_sems.at[i, d - 1],
                                                device_id=(px, py, pc), device_id_type=MESH) if arrivals else None
            copies.append((send, recv, False))
    return copies


def _exchange_start(copies):
    for send, _, _ in copies:
        send.start()


def _exchange_wait(copies):
    for send, recv, local in copies:
        if local:
            send.wait()
        else:
            send.wait_send()
            recv.wait_recv()


def _gather_two_level(arrays, name):
    n = len(arrays)

    def body(*refs):
        ins, outs = refs[:n], refs[n:2 * n]
        send_sems, recv_sems, local_sems = refs[2 * n:]
        x, y, c = lax.axis_index("x"), lax.axis_index("y"), lax.axis_index("c")
        index = lambda px, py, pc: 4 * px + 2 * py + pc
        sibling = (x, y, 1 - c)
        chips = [(1 - x, y), (x, 1 - y), (1 - x, 1 - y)]

        def copy(i, k, block, to, src=None):
            slot = outs[i].at[index(*block)]
            return pltpu.make_async_remote_copy(slot if src is None else src, slot, send_sems.at[i, k],
                                                recv_sems.at[i, k], device_id=to, device_id_type=MESH)

        local, sends = [], []
        for i in range(n):
            own = pltpu.make_async_copy(ins[i], outs[i].at[index(x, y, c)], local_sems.at[i])
            own.start()
            local.append(own)
            first = [copy(i, 0, (x, y, c), sibling, src=ins[i])]
            first += [copy(i, 1 + j, (x, y, c), (*chip, c), src=ins[i]) for j, chip in enumerate(chips)]
            for cp in first:
                cp.start()
            sends += first
        for i in range(n):
            for j, chip in enumerate(chips):
                copy(i, 1 + j, (*chip, c), (x, y, c)).wait_recv()
                onward = copy(i, 4 + j, (*chip, c), sibling)
                onward.start()
                sends.append(onward)
        for i in range(n):
            copy(i, 0, sibling, (x, y, c)).wait_recv()
            for j, chip in enumerate(chips):
                copy(i, 4 + j, (*chip, 1 - c), (x, y, c)).wait_recv()
        for cp in sends:
            cp.wait_send()
        for cp in local:
            cp.wait()

    hbm = pl.BlockSpec(memory_space=pltpu.HBM)
    return pl.pallas_call(
        body,
        in_specs=[hbm] * n,
        out_specs=[hbm] * n,
        out_shape=_exchange_out_shapes(arrays, False),
        scratch_shapes=_exchange_scratch(n),
        name=name,
    )(*arrays)


def _exchange(arrays, *, scatter, name):
    n = len(arrays)

    def body(*refs):
        copies = _exchange_copies(refs[:n], refs[n:2 * n], *refs[2 * n:], scatter)
        _exchange_start(copies)
        _exchange_wait(copies)

    hbm = pl.BlockSpec(memory_space=pltpu.HBM)
    return pl.pallas_call(
        body,
        in_specs=[hbm] * n,
        out_specs=[hbm] * n,
        out_shape=_exchange_out_shapes(arrays, scatter),
        scratch_shapes=_exchange_scratch(n),
        name=name,
    )(*arrays)


def _adamw(parts, w, m, v, name):
    rows, cols = w.shape
    tr = _pick(rows, (256, 128, 64, 8))
    c1 = 1.0 / (1.0 - ADAM_B1 ** ADAM_STEP)
    c2 = 1.0 / (1.0 - ADAM_B2 ** ADAM_STEP)

    def body(p_ref, w_ref, m_ref, v_ref, g_ref, d_ref, mo_ref, vo_ref):
        g = p_ref[0].astype(F32)
        for s in range(1, N_DEV):
            g = g + p_ref[s].astype(F32)
        mn = ADAM_B1 * m_ref[...] + (1.0 - ADAM_B1) * g
        vn = ADAM_B2 * v_ref[...] + (1.0 - ADAM_B2) * (g * g)
        g_ref[...] = g
        mo_ref[...] = mn
        vo_ref[...] = vn
        d_ref[...] = -ADAM_LR * ((mn * c1) / (jnp.sqrt(vn * c2) + ADAM_EPS) + ADAM_WD * w_ref[...])

    spec = pl.BlockSpec((tr, cols), lambda i: (i, 0))
    return pl.pallas_call(
        body,
        grid=(rows // tr,),
        in_specs=[pl.BlockSpec((N_DEV, tr, cols), lambda i: (0, i, 0)), spec, spec, spec],
        out_specs=[spec] * 4,
        out_shape=[jax.ShapeDtypeStruct((rows, cols), F32)] * 4,
        compiler_params=_cparams(("parallel",), 48),
        name=name,
    )(parts, w, m, v)


_VECTOR_PARAMS = ("attn_norm_g", "attn_sinks", "rwkv_mu", "w0", "a0", "k_k", "k_a", "r_k", "ln_x_w", "ln_x_b",
                  "mlp_norm_g", "final_norm_g")
_WEIGHT_NAMES = ("attn_norm_g", "w_in", "attn_sinks", "rwkv_mu", "w0", "w2", "a0", "a2", "g2", "k_k", "k_a", "r_k",
                 "ln_x_w", "ln_x_b", "w_out", "mlp_norm_g", "w_up", "w_down", "final_norm_g")


def _pack_vectors(vals):
    pieces = []
    for name in _VECTOR_PARAMS:
        flat = vals[name].reshape(1, -1)
        pad = (-flat.shape[1]) % LANES
        pieces.append(jnp.pad(flat, ((0, 0), (0, pad))) if pad else flat)
    return jnp.concatenate(pieces, axis=1)


def _unpack_vectors(packed, like):
    out, col = {}, 0
    for name in _VECTOR_PARAMS:
        size = like[name].size
        out[name] = packed[0, col:col + size].reshape(like[name].shape)
        col += size + (-size) % LANES
    return out


def kernel(x, attn_norm_g, w_in, attn_sinks, rwkv_mu, w0, w2, a0, a2, g2, k_k, k_a, r_k, ln_x_w, ln_x_b, w_out, mlp_norm_g, w_up, w_down, final_norm_g, loss_target, m_attn_norm_g, m_w_in, m_attn_sinks, m_rwkv_mu, m_w0, m_w2, m_a0, m_a2, m_g2, m_k_k, m_k_a, m_r_k, m_ln_x_w, m_ln_x_b, m_w_out, m_mlp_norm_g, m_w_up, m_w_down, m_final_norm_g, v_attn_norm_g, v_w_in, v_attn_sinks, v_rwkv_mu, v_w0, v_w2, v_a0, v_a2, v_g2, v_k_k, v_k_a, v_r_k, v_ln_x_w, v_ln_x_b, v_w_out, v_mlp_norm_g, v_w_up, v_w_down, v_final_norm_g):
    weights = dict(attn_norm_g=attn_norm_g, w_in=w_in, attn_sinks=attn_sinks, rwkv_mu=rwkv_mu, w0=w0, w2=w2, a0=a0,
                   a2=a2, g2=g2, k_k=k_k, k_a=k_a, r_k=r_k, ln_x_w=ln_x_w, ln_x_b=ln_x_b, w_out=w_out,
                   mlp_norm_g=mlp_norm_g, w_up=w_up, w_down=w_down, final_norm_g=final_norm_g)
    mom1 = dict(attn_norm_g=m_attn_norm_g, w_in=m_w_in, attn_sinks=m_attn_sinks, rwkv_mu=m_rwkv_mu, w0=m_w0, w2=m_w2,
                a0=m_a0, a2=m_a2, g2=m_g2, k_k=m_k_k, k_a=m_k_a, r_k=m_r_k, ln_x_w=m_ln_x_w, ln_x_b=m_ln_x_b,
                w_out=m_w_out, mlp_norm_g=m_mlp_norm_g, w_up=m_w_up, w_down=m_w_down, final_norm_g=m_final_norm_g)
    mom2 = dict(attn_norm_g=v_attn_norm_g, w_in=v_w_in, attn_sinks=v_attn_sinks, rwkv_mu=v_rwkv_mu, w0=v_w0, w2=v_w2,
                a0=v_a0, a2=v_a2, g2=v_g2, k_k=v_k_k, k_a=v_k_a, r_k=v_r_k, ln_x_w=v_ln_x_w, ln_x_b=v_ln_x_b,
                w_out=v_w_out, mlp_norm_g=v_mlp_norm_g, w_up=v_w_up, w_down=v_w_down, final_norm_g=v_final_norm_g)
    bsz, seq, d_model = x.shape
    rows = bsz * seq
    d_in = N_DEV * w_in.shape[2]
    d_ff = N_DEV * w_up.shape[2]

    gathered = _gather_two_level([w_in[0].astype(BF16), w2[0], a2[0], g2[0]], name="gather_in_weights")
    cols_first = lambda a: a.transpose(1, 0, 2).reshape(a.shape[1], -1)
    w_in_f = cols_first(gathered[0])
    w_attn, w_rw = w_in_f[:, :QKV_WIDTH], w_in_f[:, QKV_WIDTH:]
    w2_f, a2_f, g2_f = cols_first(gathered[1]), cols_first(gathered[2]), cols_first(gathered[3])
    lora = w2_f.shape[0]
    w2p = jnp.concatenate([w2_f, jnp.zeros_like(a2_f)], axis=0)
    a2p = jnp.concatenate([jnp.zeros_like(w2_f), a2_f], axis=0)

    esum = _head_sum_matrix()
    sink_rows = jnp.broadcast_to(attn_sinks.reshape(N_ATTN_HEADS, 1), (N_ATTN_HEADS, LANES))
    prep_pars = [w0, w2p, a0, a2p, g2_f, k_k, k_a, esum]
    post_pars = [ln_x_w, ln_x_b, r_k, esum]

    x2d = x.reshape(rows, d_model)
    h1, z_attn, z_rw, r, lw, k, v, kk, al, gate = _norm_in_proj_prep(x2d, seq, attn_norm_g, w_attn, w_rw, rwkv_mu,
                                                                       prep_pars)
    z_attn3 = z_attn.reshape(bsz, seq, QKV_WIDTH)
    as3 = lambda a: a.reshape(bsz, seq, RWKV_WIDTH)
    y, from_fwd, (attn_out, attn_probs, attn_psinks), late = _rwkv_attn_fwd(
        as3(r), as3(lw), as3(k), as3(v), as3(kk), as3(al), z_attn3, sink_rows,
        hosted=[w_out[0].astype(BF16), w_up[0].astype(BF16), w_down[0].astype(BF16)])
    w_out_f = late[0].reshape(-1, d_model)
    w_up_f = cols_first(late[1])
    w_down_f = late[2].reshape(-1, d_model)
    y2 = y.reshape(rows, RWKV_WIDTH)
    attn_out2d = attn_out.reshape(rows, ATTN_WIDTH)
    w_out_attn, w_out_rw = w_out_f[:ATTN_WIDTH], w_out_f[ATTN_WIDTH:]
    rw_out, x1, h2 = _post_out_proj_norm(y2, r, k, v, gate, attn_out2d, x2d, post_pars, w_out_attn, w_out_rw,
                                         mlp_norm_g)

    u = _matmul(h2, w_up_f, "nn", name="mlp_up", out_dtypes=(BF16,), tm=2048)
    dx2, dx2_b, loss_vec, g_final = _down_proj_loss(u, w_down_f, x1, loss_target.reshape(rows, d_model),
                                             final_norm_g.reshape(1, d_model))

    g_w_down = _matmul(u, dx2_b, "tn", name="grad_w_down", out_dtypes=(BF16,), a_map=_relu_squared, tm=2048)
    du = _matmul(dx2_b, w_down_f, "nt", name="mlp_down_bwd", extras=(u,), out_dtypes=(BF16,), tm=2048,
                 epilogue=lambda acc, uv: (acc * (2.0 * jnp.maximum(uv.astype(F32), 0.0)),))
    g_w_up = _matmul(h2, du, "tn", name="grad_w_up", out_dtypes=(BF16,), tn=2048)
    dx1, g_mlp_norm = _proj_bwd_norm_bwd([du], [w_up_f], x1, dx2, mlp_norm_g, "mlp_up_bwd_norm_bwd")
    g_w_out = jnp.concatenate([_matmul(attn_out2d, dx1, "tn", name="grad_w_out_attn", out_dtypes=(BF16,)),
                               _matmul(rw_out, dx1, "tn", name="grad_w_out_rwkv", out_dtypes=(BF16,))], axis=0)
    d_attn_out, dy, dr_a, dk_a, dv_a, dgate, g_ln_w, g_ln_b, g_r_k = _out_proj_bwd_post_bwd(
        dx1, y2, r, k, v, gate, post_pars, w_out_attn, w_out_rw)
    by_cols = lambda a: a.reshape(a.shape[0], N_DEV, -1).transpose(1, 0, 2)
    (dr_b, dlw, dk_b, dv_b, dkk, dal), (dz_attn, g_sink_lanes), (p_w_out, p_w_up, p_w_down) = _rwkv_attn_bwd(
        as3(r), as3(lw), as3(k), as3(v), as3(kk), as3(al), from_fwd, as3(dy),
        z_attn3, d_attn_out.reshape(bsz, seq, ATTN_WIDTH), attn_probs, attn_psinks,
        hosted=[g_w_out.reshape(N_DEV, -1, d_model), by_cols(g_w_up), g_w_down.reshape(N_DEV, -1, d_model)],
        scatter=True)
    flat = lambda a: a.reshape(rows, RWKV_WIDTH)
    (dz_rw, gmu_r, gmu_k, gmu_v, gmu_wa, gmu_g, g_w0, g_w2p, g_a0, g_a2p, g_g2, g_k_k, g_k_a) = _prep_bwd(
        z_rw, seq, [dr_a, flat(dr_b), flat(dlw), dk_a, flat(dk_b), dv_a, flat(dv_b), flat(dkk), flat(dal), dgate],
        rwkv_mu, prep_pars)
    dz_attn = dz_attn.reshape(rows, QKV_WIDTH)
    g_w_in = jnp.concatenate([_matmul(h1, dz_attn, "tn", name="grad_w_in_attn", out_dtypes=(BF16,)),
                              _matmul(h1, dz_rw, "tn", name="grad_w_in_rwkv", out_dtypes=(BF16,), tn=1792)], axis=1)
    lora_grads = jnp.concatenate([g_w2p[:lora], g_a2p[lora:], g_g2], axis=0)
    (dx, g_attn_norm), (p_w_in, p_lora) = _proj_bwd_norm_bwd(
        [dz_attn, dz_rw], [w_attn, w_rw], x2d, dx1, attn_norm_g, "in_proj_bwd_norm_bwd",
        hosted=[by_cols(g_w_in), by_cols(lora_grads)], scatter=True)

    vec_grads = dict(attn_norm_g=g_attn_norm, attn_sinks=g_sink_lanes[0, :N_ATTN_HEADS], rwkv_mu=jnp.concatenate(
        [gmu_r, gmu_k, gmu_v, gmu_wa, gmu_g], axis=1), w0=g_w0, a0=g_a0, k_k=g_k_k, k_a=g_k_a, r_k=g_r_k,
        ln_x_w=g_ln_w, ln_x_b=g_ln_b, mlp_norm_g=g_mlp_norm, final_norm_g=g_final)
    packed = _pack_vectors(vec_grads)
    nvec = packed.shape[1]
    everyone = _exchange([jnp.concatenate([packed, loss_vec], axis=1)], scatter=False, name="gather_vector_grads")[0]
    vec_parts = everyone[:, :, :nvec]
    loss = jnp.sum(everyone[:, 0, nvec])

    grads, delta, new_m, new_v = {}, {}, {}, {}

    def update(name, part, shape2d):
        res = _adamw(part, weights[name].reshape(shape2d), mom1[name].reshape(shape2d), mom2[name].reshape(shape2d),
                     "adamw_" + name)
        for store, val in zip((grads, delta, new_m, new_v), res):
            store[name] = val.reshape(weights[name].shape)

    update("w_in", p_w_in, w_in.shape[1:])
    update("w_out", p_w_out, w_out.shape[1:])
    update("w_up", p_w_up, w_up.shape[1:])
    update("w_down", p_w_down, w_down.shape[1:])
    stack = lambda d: jnp.concatenate([d["w2"][0], d["a2"][0], d["g2"][0]], axis=0)
    lora_res = _adamw(p_lora, stack(weights), stack(mom1), stack(mom2), "adamw_lora")
    for store, val in zip((grads, delta, new_m, new_v), lora_res):
        store["w2"], store["a2"], store["g2"] = val[None, :lora], val[None, lora:2 * lora], val[None, 2 * lora:]
    vec_res = _adamw(vec_parts, _pack_vectors(weights), _pack_vectors(mom1), _pack_vectors(mom2), "adamw_vectors")
    for store, val in zip((grads, delta, new_m, new_v), vec_res):
        store.update(_unpack_vectors(val, weights))

    return (loss, dx.reshape(x.shape), *[grads[n] for n in _WEIGHT_NAMES], *[delta[n] for n in _WEIGHT_NAMES],
            *[new_m[n] for n in _WEIGHT_NAMES], *[new_v[n] for n in _WEIGHT_NAMES])
```
